```python
import math
import jax, jax.numpy as jnp
from jax import lax
import numpy as np

D_MODEL = 1024
BATCH = 8
SEQ = 8192
DEPTH = 4

HEAD_DIM = 64
H_FOX = 8
H_SB = 8
N_HEADS = H_FOX + H_SB
MIX_WIDTH = N_HEADS * HEAD_DIM
ATTN_IN = 3 * MIX_WIDTH + H_FOX
CONV_WIDTH = D_MODEL
CONV_K = 3
FFN_CONV_K = 3
D_FF = 2816
Q_BLOCK = 128
N_ATTN = (DEPTH + 1) // 2
N_CONV = DEPTH // 2
EPS = 1e-6

kernel_name = 'fox_stickbreak_shortconv_hybrid'


def rms_norm(x, g):
    xf = x.astype(jnp.float32)
    y = xf * lax.rsqrt(jnp.mean(xf * xf, axis=-1, keepdims=True) + EPS)
    return (y * g.astype(jnp.float32)).astype(x.dtype)


def causal_dwconv(x, w):
    k = w.shape[0]
    s = x.shape[1]
    xp = jnp.pad(x, ((0, 0), (k - 1, 0), (0, 0)))
    return sum(xp[:, j:j + s] * w[j] for j in range(k))


def split_query_blocks(t):
    b, h, s = t.shape[:3]
    rest = t.shape[3:]
    t = t.reshape((b, h, s // Q_BLOCK, Q_BLOCK) + rest)
    return jnp.moveaxis(t, 2, 0)


def merge_query_blocks(t):
    nb, b, h, qb, d = t.shape
    return jnp.moveaxis(t, 0, 2).reshape(b, h, nb * qb, d)


def forgetting_attention(q, k, v, log_f):
    s, dh = q.shape[2], q.shape[3]
    cum_f = jnp.cumsum(log_f, axis=-1)
    key_pos = jnp.arange(s)
    scale = dh ** -0.5
    nb = s // Q_BLOCK

    def one_block(args):
        i, qi, fi = args
        q_pos = i * Q_BLOCK + jnp.arange(Q_BLOCK)
        logits = jnp.einsum('bhqd,bhkd->bhqk', qi, k,
                            preferred_element_type=jnp.float32) * scale
        logits = logits + fi[..., :, None] - cum_f[:, :, None, :]
        causal = key_pos[None, :] <= q_pos[:, None]
        logits = jnp.where(causal, logits, -jnp.inf)
        p = jax.nn.softmax(logits, axis=-1)
        return jnp.einsum('bhqk,bhkd->bhqd', p.astype(v.dtype), v)

    out = lax.map(one_block, (jnp.arange(nb), split_query_blocks(q), split_query_blocks(cum_f)))
    return merge_query_blocks(out)


def stick_breaking_attention(q, k, v):
    s, dh = q.shape[2], q.shape[3]
    key_pos = jnp.arange(s)
    scale = dh ** -0.5
    nb = s // Q_BLOCK

    def one_block(args):
        i, qi = args
        q_pos = i * Q_BLOCK + jnp.arange(Q_BLOCK)
        z = jnp.einsum('bhqd,bhkd->bhqk', qi, k,
                       preferred_element_type=jnp.float32) * scale
        strict = key_pos[None, :] < q_pos[:, None]
        log_beta = jax.nn.log_sigmoid(z)
        log_one_minus = jnp.where(strict, jax.nn.log_sigmoid(-z), 0.0)
        key_axis = log_one_minus.ndim - 1
        later = lax.cumsum(log_one_minus, axis=key_axis, reverse=True) - log_one_minus
        w = jnp.where(strict, jnp.exp(log_beta + later), 0.0)
        return jnp.einsum('bhqk,bhkd->bhqd', w.astype(v.dtype), v)

    out = lax.map(one_block, (jnp.arange(nb), split_query_blocks(q)))
    return merge_query_blocks(out)


def attention_mixer(h, norm_g, w_in, f_bias, fox_q_g, fox_k_g, sb_q_g, sb_k_g, w_out):
    b, s, _ = h.shape
    xn = rms_norm(h, norm_g)
    proj = xn @ w_in

    def heads(t):
        return t.reshape(b, s, N_HEADS, HEAD_DIM).transpose(0, 2, 1, 3)

    q = heads(proj[..., :MIX_WIDTH])
    k = heads(proj[..., MIX_WIDTH:2 * MIX_WIDTH])
    v = heads(proj[..., 2 * MIX_WIDTH:3 * MIX_WIDTH])
    f_logit = proj[..., 3 * MIX_WIDTH:].astype(jnp.float32) + f_bias.astype(jnp.float32)
    log_f = jax.nn.log_sigmoid(f_logit).transpose(0, 2, 1)

    q_fox = rms_norm(q[:, :H_FOX], fox_q_g)
    k_fox = rms_norm(k[:, :H_FOX], fox_k_g)
    q_sb = rms_norm(q[:, H_FOX:], sb_q_g)
    k_sb = rms_norm(k[:, H_FOX:], sb_k_g)

    o_fox = forgetting_attention(q_fox, k_fox, v[:, :H_FOX], log_f)
    o_sb = stick_breaking_attention(q_sb, k_sb, v[:, H_FOX:])
    o = jnp.concatenate([o_fox, o_sb], axis=1)
    o = o.transpose(0, 2, 1, 3).reshape(b, s, MIX_WIDTH)
    return o @ w_out


def short_conv_mixer(h, norm_g, w_in, conv_w, w_out):
    xn = rms_norm(h, norm_g)
    proj = xn @ w_in
    gate_b = proj[..., :CONV_WIDTH]
    gate_c = proj[..., CONV_WIDTH:2 * CONV_WIDTH]
    u = proj[..., 2 * CONV_WIDTH:]
    y = gate_b * causal_dwconv(gate_c * u, conv_w)
    return y @ w_out


def conv_ffn(h, norm_g, w_up, conv_w, w_down):
    xn = rms_norm(h, norm_g)
    u = causal_dwconv(xn @ w_up, conv_w)
    g, val = u[..., :D_FF], u[..., D_FF:]
    return (jax.nn.silu(g) * val) @ w_down


def _fwd_setup_inputs(seed: int = 0) -> dict:
    key = jax.random.key(seed)
    ks = jax.random.split(key, 20)
    f32 = jnp.float32
    out_scale = (2 * DEPTH) ** -0.5

    def normal(k, shape, scale):
        return scale * jax.random.normal(k, shape, f32)

    def gain(k, shape):
        return 1.0 + 0.05 * jax.random.normal(k, shape, f32)

    x = jax.random.normal(ks[0], (BATCH, SEQ, D_MODEL), f32)
    attn_norm = gain(ks[1], (N_ATTN, D_MODEL))
    attn_w_in = normal(ks[2], (N_ATTN, D_MODEL, ATTN_IN), D_MODEL ** -0.5)
    attn_f_bias = jnp.linspace(1.0, 6.0, H_FOX, dtype=f32)[None, :] + normal(ks[3], (N_ATTN, H_FOX), 0.1)
    fox_q_gain = gain(ks[4], (N_ATTN, HEAD_DIM))
    fox_k_gain = gain(ks[5], (N_ATTN, HEAD_DIM))
    sb_q_gain = gain(ks[6], (N_ATTN, HEAD_DIM))
    sb_k_gain = gain(ks[7], (N_ATTN, HEAD_DIM))
    attn_w_out = normal(ks[8], (N_ATTN, MIX_WIDTH, D_MODEL), out_scale * MIX_WIDTH ** -0.5)
    conv_norm = gain(ks[9], (N_CONV, D_MODEL))
    conv_w_in = normal(ks[10], (N_CONV, D_MODEL, 3 * CONV_WIDTH), D_MODEL ** -0.5)
    conv_kernel = normal(ks[11], (N_CONV, CONV_K, CONV_WIDTH), CONV_K ** -0.5)
    conv_w_out = normal(ks[12], (N_CONV, CONV_WIDTH, D_MODEL), out_scale * CONV_WIDTH ** -0.5)
    ffn_norm = gain(ks[13], (DEPTH, D_MODEL))
    ffn_w_up = normal(ks[14], (DEPTH, D_MODEL, 2 * D_FF), D_MODEL ** -0.5)
    ffn_conv = normal(ks[15], (DEPTH, FFN_CONV_K, 2 * D_FF), FFN_CONV_K ** -0.5)
    ffn_w_down = normal(ks[16], (DEPTH, D_FF, D_MODEL), out_scale * D_FF ** -0.5)
    return {
        'x': x,
        'attn_norm': attn_norm, 'attn_w_in': attn_w_in, 'attn_f_bias': attn_f_bias,
        'fox_q_gain': fox_q_gain, 'fox_k_gain': fox_k_gain,
        'sb_q_gain': sb_q_gain, 'sb_k_gain': sb_k_gain, 'attn_w_out': attn_w_out,
        'conv_norm': conv_norm, 'conv_w_in': conv_w_in, 'conv_kernel': conv_kernel,
        'conv_w_out': conv_w_out,
        'ffn_norm': ffn_norm, 'ffn_w_up': ffn_w_up, 'ffn_conv': ffn_conv, 'ffn_w_down': ffn_w_down,
    }


def _fwd_reference(x, attn_norm, attn_w_in, attn_f_bias, fox_q_gain, fox_k_gain, sb_q_gain,
              sb_k_gain, attn_w_out, conv_norm, conv_w_in, conv_kernel, conv_w_out,
              ffn_norm, ffn_w_up, ffn_conv, ffn_w_down):
    h = x
    for layer in range(DEPTH):
        i = layer // 2
        if layer % 2 == 0:
            h = h + attention_mixer(h, attn_norm[i], attn_w_in[i], attn_f_bias[i],
                                    fox_q_gain[i], fox_k_gain[i], sb_q_gain[i], sb_k_gain[i],
                                    attn_w_out[i])
        else:
            h = h + short_conv_mixer(h, conv_norm[i], conv_w_in[i], conv_kernel[i], conv_w_out[i])
        h = h + conv_ffn(h, ffn_norm[layer], ffn_w_up[layer], ffn_conv[layer], ffn_w_down[layer])
    return h


import jax as _jax
import jax.numpy as _jnp

TWIN_FORMAT = 'train_step'
FWD_PARAMS = ['x', 'attn_norm', 'attn_w_in', 'attn_f_bias', 'fox_q_gain', 'fox_k_gain', 'sb_q_gain', 'sb_k_gain', 'attn_w_out', 'conv_norm', 'conv_w_in', 'conv_kernel', 'conv_w_out', 'ffn_norm', 'ffn_w_up', 'ffn_conv', 'ffn_w_down']
TWIN_WEIGHTS = ['attn_norm', 'attn_w_in', 'attn_f_bias', 'fox_q_gain', 'fox_k_gain', 'sb_q_gain', 'sb_k_gain', 'attn_w_out', 'conv_norm', 'conv_w_in', 'conv_kernel', 'conv_w_out', 'ffn_norm', 'ffn_w_up', 'ffn_conv', 'ffn_w_down']
TWIN_DIFF_INPUT = 'x'
TWIN_INPUTS = ['x', 'attn_norm', 'attn_w_in', 'attn_f_bias', 'fox_q_gain', 'fox_k_gain', 'sb_q_gain', 'sb_k_gain', 'attn_w_out', 'conv_norm', 'conv_w_in', 'conv_kernel', 'conv_w_out', 'ffn_norm', 'ffn_w_up', 'ffn_conv', 'ffn_w_down', 'loss_target', 'm_attn_norm', 'm_attn_w_in', 'm_attn_f_bias', 'm_fox_q_gain', 'm_fox_k_gain', 'm_sb_q_gain', 'm_sb_k_gain', 'm_attn_w_out', 'm_conv_norm', 'm_conv_w_in', 'm_conv_kernel', 'm_conv_w_out', 'm_ffn_norm', 'm_ffn_w_up', 'm_ffn_conv', 'm_ffn_w_down', 'v_attn_norm', 'v_attn_w_in', 'v_attn_f_bias', 'v_fox_q_gain', 'v_fox_k_gain', 'v_sb_q_gain', 'v_sb_k_gain', 'v_attn_w_out', 'v_conv_norm', 'v_conv_w_in', 'v_conv_kernel', 'v_conv_w_out', 'v_ffn_norm', 'v_ffn_w_up', 'v_ffn_conv', 'v_ffn_w_down']
TWIN_OUTPUTS = ['loss', 'grad_x', 'grad_attn_norm', 'grad_attn_w_in', 'grad_attn_f_bias', 'grad_fox_q_gain', 'grad_fox_k_gain', 'grad_sb_q_gain', 'grad_sb_k_gain', 'grad_attn_w_out', 'grad_conv_norm', 'grad_conv_w_in', 'grad_conv_kernel', 'grad_conv_w_out', 'grad_ffn_norm', 'grad_ffn_w_up', 'grad_ffn_conv', 'grad_ffn_w_down', 'delta_attn_norm', 'delta_attn_w_in', 'delta_attn_f_bias', 'delta_fox_q_gain', 'delta_fox_k_gain', 'delta_sb_q_gain', 'delta_sb_k_gain', 'delta_attn_w_out', 'delta_conv_norm', 'delta_conv_w_in', 'delta_conv_kernel', 'delta_conv_w_out', 'delta_ffn_norm', 'delta_ffn_w_up', 'delta_ffn_conv', 'delta_ffn_w_down', 'new_m_attn_norm', 'new_m_attn_w_in', 'new_m_attn_f_bias', 'new_m_fox_q_gain', 'new_m_fox_k_gain', 'new_m_sb_q_gain', 'new_m_sb_k_gain', 'new_m_attn_w_out', 'new_m_conv_norm', 'new_m_conv_w_in', 'new_m_conv_kernel', 'new_m_conv_w_out', 'new_m_ffn_norm', 'new_m_ffn_w_up', 'new_m_ffn_conv', 'new_m_ffn_w_down', 'new_v_attn_norm', 'new_v_attn_w_in', 'new_v_attn_f_bias', 'new_v_fox_q_gain', 'new_v_fox_k_gain', 'new_v_sb_q_gain', 'new_v_sb_k_gain', 'new_v_attn_w_out', 'new_v_conv_norm', 'new_v_conv_w_in', 'new_v_conv_kernel', 'new_v_conv_w_out', 'new_v_ffn_norm', 'new_v_ffn_w_up', 'new_v_ffn_conv', 'new_v_ffn_w_down']
TWIN_LEAF_KINDS = {'loss': 'loss', 'grad_x': 'grad_x', 'grad_attn_norm': 'grad_w', 'grad_attn_w_in': 'grad_w', 'grad_attn_f_bias': 'grad_w', 'grad_fox_q_gain': 'grad_w', 'grad_fox_k_gain': 'grad_w', 'grad_sb_q_gain': 'grad_w', 'grad_sb_k_gain': 'grad_w', 'grad_attn_w_out': 'grad_w', 'grad_conv_norm': 'grad_w', 'grad_conv_w_in': 'grad_w', 'grad_conv_kernel': 'grad_w', 'grad_conv_w_out': 'grad_w', 'grad_ffn_norm': 'grad_w', 'grad_ffn_w_up': 'grad_w', 'grad_ffn_conv': 'grad_w', 'grad_ffn_w_down': 'grad_w', 'delta_attn_norm': 'delta_w', 'delta_attn_w_in': 'delta_w', 'delta_attn_f_bias': 'delta_w', 'delta_fox_q_gain': 'delta_w', 'delta_fox_k_gain': 'delta_w', 'delta_sb_q_gain': 'delta_w', 'delta_sb_k_gain': 'delta_w', 'delta_attn_w_out': 'delta_w', 'delta_conv_norm': 'delta_w', 'delta_conv_w_in': 'delta_w', 'delta_conv_kernel': 'delta_w', 'delta_conv_w_out': 'delta_w', 'delta_ffn_norm': 'delta_w', 'delta_ffn_w_up': 'delta_w', 'delta_ffn_conv': 'delta_w', 'delta_ffn_w_down': 'delta_w', 'new_m_attn_norm': 'new_m', 'new_m_attn_w_in': 'new_m', 'new_m_attn_f_bias': 'new_m', 'new_m_fox_q_gain': 'new_m', 'new_m_fox_k_gain': 'new_m', 'new_m_sb_q_gain': 'new_m', 'new_m_sb_k_gain': 'new_m', 'new_m_attn_w_out': 'new_m', 'new_m_conv_norm': 'new_m', 'new_m_conv_w_in': 'new_m', 'new_m_conv_kernel': 'new_m', 'new_m_conv_w_out': 'new_m', 'new_m_ffn_norm': 'new_m', 'new_m_ffn_w_up': 'new_m', 'new_m_ffn_conv': 'new_m', 'new_m_ffn_w_down': 'new_m', 'new_v_attn_norm': 'new_v', 'new_v_attn_w_in': 'new_v', 'new_v_attn_f_bias': 'new_v', 'new_v_fox_q_gain': 'new_v', 'new_v_fox_k_gain': 'new_v', 'new_v_sb_q_gain': 'new_v', 'new_v_sb_k_gain': 'new_v', 'new_v_attn_w_out': 'new_v', 'new_v_conv_norm': 'new_v', 'new_v_conv_w_in': 'new_v', 'new_v_conv_kernel': 'new_v', 'new_v_conv_w_out': 'new_v', 'new_v_ffn_norm': 'new_v', 'new_v_ffn_w_up': 'new_v', 'new_v_ffn_conv': 'new_v', 'new_v_ffn_w_down': 'new_v'}


def _forward(args):
    return _fwd_reference(*[args[k] for k in FWD_PARAMS])


def _output_shape():
    def fwd():
        inp = _fwd_setup_inputs(0)
        return _fwd_reference(*[inp[k] for k in FWD_PARAMS])
    out = _jax.eval_shape(fwd)
    return out.shape, out.dtype

N_MICROBATCH = 1
ADAM_LR = 0.001
ADAM_B1 = 0.9
ADAM_B2 = 0.999
ADAM_EPS = 1e-08
ADAM_WD = 0.01
ADAM_STEP = 10
PER_EXAMPLE_BATCH_AXIS = {'x': 0, 'loss_target': 0}
SHARED_INPUTS = []
_WEIGHT_DTYPES = {'attn_norm': _jnp.float32, 'attn_w_in': _jnp.float32, 'attn_f_bias': _jnp.float32, 'fox_q_gain': _jnp.float32, 'fox_k_gain': _jnp.float32, 'sb_q_gain': _jnp.float32, 'sb_k_gain': _jnp.float32, 'attn_w_out': _jnp.float32, 'conv_norm': _jnp.float32, 'conv_w_in': _jnp.float32, 'conv_kernel': _jnp.float32, 'conv_w_out': _jnp.float32, 'ffn_norm': _jnp.float32, 'ffn_w_up': _jnp.float32, 'ffn_conv': _jnp.float32, 'ffn_w_down': _jnp.float32}
MOMENT_SCALE = {'attn_norm': 1.976158e+00, 'attn_w_in': 9.011850e-02, 'attn_f_bias': 1.752482e+01, 'fox_q_gain': 1.931709e+00, 'fox_k_gain': 1.926844e+00, 'sb_q_gain': 3.837790e+00, 'sb_k_gain': 3.878717e+00, 'attn_w_out': 3.631971e-01, 'conv_norm': 2.392511e+01, 'conv_w_in': 2.738373e-01, 'conv_kernel': 4.531633e+00, 'conv_w_out': 7.195080e-01, 'ffn_norm': 6.318236e+00, 'ffn_w_up': 8.474346e-02, 'ffn_conv': 8.285499e-01, 'ffn_w_down': 3.962545e-01}


def _to_microbatches(a, axis):
    t = _jnp.moveaxis(a, axis, 0)
    t = t.reshape((N_MICROBATCH, t.shape[0] // N_MICROBATCH) + t.shape[1:])
    return _jnp.moveaxis(t, 1, axis + 1)


def setup_inputs(seed: int = 0) -> dict:
    inp = _fwd_setup_inputs(seed)
    key = _jax.random.fold_in(_jax.random.key(seed), 7919)
    shape, _ = _output_shape()
    out = dict(inp)
    out["loss_target"] = _jax.random.normal(_jax.random.fold_in(key, 0), shape, _jnp.float32)
    for i, name in enumerate(TWIN_WEIGHTS):
        w = inp[name].astype(_jnp.float32)
        if MOMENT_SCALE is None:
            s = _jnp.sqrt(_jnp.mean(_jnp.square(w)) + 1e-30)
        else:
            s = MOMENT_SCALE[name]
        km, kv = _jax.random.split(_jax.random.fold_in(key, i + 1))
        out[name] = w
        out["m_" + name] = s * _jax.random.normal(km, w.shape, _jnp.float32)
        out["v_" + name] = (s * s) * _jax.random.uniform(kv, w.shape, _jnp.float32, 0.5, 1.5)
    if N_MICROBATCH > 1:
        for name, axis in PER_EXAMPLE_BATCH_AXIS.items():
            out[name] = _to_microbatches(out[name], axis)
    return {'x': out['x'], 'attn_norm': out['attn_norm'], 'attn_w_in': out['attn_w_in'], 'attn_f_bias': out['attn_f_bias'], 'fox_q_gain': out['fox_q_gain'], 'fox_k_gain': out['fox_k_gain'], 'sb_q_gain': out['sb_q_gain'], 'sb_k_gain': out['sb_k_gain'], 'attn_w_out': out['attn_w_out'], 'conv_norm': out['conv_norm'], 'conv_w_in': out['conv_w_in'], 'conv_kernel': out['conv_kernel'], 'conv_w_out': out['conv_w_out'], 'ffn_norm': out['ffn_norm'], 'ffn_w_up': out['ffn_w_up'], 'ffn_conv': out['ffn_conv'], 'ffn_w_down': out['ffn_w_down'], 'loss_target': out['loss_target'], 'm_attn_norm': out['m_attn_norm'], 'm_attn_w_in': out['m_attn_w_in'], 'm_attn_f_bias': out['m_attn_f_bias'], 'm_fox_q_gain': out['m_fox_q_gain'], 'm_fox_k_gain': out['m_fox_k_gain'], 'm_sb_q_gain': out['m_sb_q_gain'], 'm_sb_k_gain': out['m_sb_k_gain'], 'm_attn_w_out': out['m_attn_w_out'], 'm_conv_norm': out['m_conv_norm'], 'm_conv_w_in': out['m_conv_w_in'], 'm_conv_kernel': out['m_conv_kernel'], 'm_conv_w_out': out['m_conv_w_out'], 'm_ffn_norm': out['m_ffn_norm'], 'm_ffn_w_up': out['m_ffn_w_up'], 'm_ffn_conv': out['m_ffn_conv'], 'm_ffn_w_down': out['m_ffn_w_down'], 'v_attn_norm': out['v_attn_norm'], 'v_attn_w_in': out['v_attn_w_in'], 'v_attn_f_bias': out['v_attn_f_bias'], 'v_fox_q_gain': out['v_fox_q_gain'], 'v_fox_k_gain': out['v_fox_k_gain'], 'v_sb_q_gain': out['v_sb_q_gain'], 'v_sb_k_gain': out['v_sb_k_gain'], 'v_attn_w_out': out['v_attn_w_out'], 'v_conv_norm': out['v_conv_norm'], 'v_conv_w_in': out['v_conv_w_in'], 'v_conv_kernel': out['v_conv_kernel'], 'v_conv_w_out': out['v_conv_w_out'], 'v_ffn_norm': out['v_ffn_norm'], 'v_ffn_w_up': out['v_ffn_w_up'], 'v_ffn_conv': out['v_ffn_conv'], 'v_ffn_w_down': out['v_ffn_w_down']}


def _loss(weights, diff, rest, loss_target):
    with _jax.named_scope("forward"):
        args = {**rest, TWIN_DIFF_INPUT: diff, **{k: w.astype(_WEIGHT_DTYPES[k]) for k, w in weights.items()}}
        y = _forward(args)
    with _jax.named_scope("loss_head"):
        err = _jnp.square(y.astype(_jnp.float32) - loss_target)
        return 0.5 * _jnp.sum(_jnp.mean(err, axis=-1)) if err.ndim else 0.5 * err


def _adamw(w, g, m, v):
    m = ADAM_B1 * m + (1.0 - ADAM_B1) * g
    v = ADAM_B2 * v + (1.0 - ADAM_B2) * _jnp.square(g)
    m_hat = m / (1.0 - ADAM_B1 ** ADAM_STEP)
    v_hat = v / (1.0 - ADAM_B2 ** ADAM_STEP)
    delta = -ADAM_LR * (m_hat / (_jnp.sqrt(v_hat) + ADAM_EPS) + ADAM_WD * w)
    return delta, m, v


def reference(x, attn_norm, attn_w_in, attn_f_bias, fox_q_gain, fox_k_gain, sb_q_gain, sb_k_gain, attn_w_out, conv_norm, conv_w_in, conv_kernel, conv_w_out, ffn_norm, ffn_w_up, ffn_conv, ffn_w_down, loss_target, m_attn_norm, m_attn_w_in, m_attn_f_bias, m_fox_q_gain, m_fox_k_gain, m_sb_q_gain, m_sb_k_gain, m_attn_w_out, m_conv_norm, m_conv_w_in, m_conv_kernel, m_conv_w_out, m_ffn_norm, m_ffn_w_up, m_ffn_conv, m_ffn_w_down, v_attn_norm, v_attn_w_in, v_attn_f_bias, v_fox_q_gain, v_fox_k_gain, v_sb_q_gain, v_sb_k_gain, v_attn_w_out, v_conv_norm, v_conv_w_in, v_conv_kernel, v_conv_w_out, v_ffn_norm, v_ffn_w_up, v_ffn_conv, v_ffn_w_down):
    given = dict(x=x, attn_norm=attn_norm, attn_w_in=attn_w_in, attn_f_bias=attn_f_bias, fox_q_gain=fox_q_gain, fox_k_gain=fox_k_gain, sb_q_gain=sb_q_gain, sb_k_gain=sb_k_gain, attn_w_out=attn_w_out, conv_norm=conv_norm, conv_w_in=conv_w_in, conv_kernel=conv_kernel, conv_w_out=conv_w_out, ffn_norm=ffn_norm, ffn_w_up=ffn_w_up, ffn_conv=ffn_conv, ffn_w_down=ffn_w_down, loss_target=loss_target, m_attn_norm=m_attn_norm, m_attn_w_in=m_attn_w_in, m_attn_f_bias=m_attn_f_bias, m_fox_q_gain=m_fox_q_gain, m_fox_k_gain=m_fox_k_gain, m_sb_q_gain=m_sb_q_gain, m_sb_k_gain=m_sb_k_gain, m_attn_w_out=m_attn_w_out, m_conv_norm=m_conv_norm, m_conv_w_in=m_conv_w_in, m_conv_kernel=m_conv_kernel, m_conv_w_out=m_conv_w_out, m_ffn_norm=m_ffn_norm, m_ffn_w_up=m_ffn_w_up, m_ffn_conv=m_ffn_conv, m_ffn_w_down=m_ffn_w_down, v_attn_norm=v_attn_norm, v_attn_w_in=v_attn_w_in, v_attn_f_bias=v_attn_f_bias, v_fox_q_gain=v_fox_q_gain, v_fox_k_gain=v_fox_k_gain, v_sb_q_gain=v_sb_q_gain, v_sb_k_gain=v_sb_k_gain, v_attn_w_out=v_attn_w_out, v_conv_norm=v_conv_norm, v_conv_w_in=v_conv_w_in, v_conv_kernel=v_conv_kernel, v_conv_w_out=v_conv_w_out, v_ffn_norm=v_ffn_norm, v_ffn_w_up=v_ffn_w_up, v_ffn_conv=v_ffn_conv, v_ffn_w_down=v_ffn_w_down)
    weights = {n: given[n] for n in TWIN_WEIGHTS}
    shared = {n: given[n] for n in SHARED_INPUTS}
    per_example = {n: given[n] for n in ['x']}
    grad_fn = _jax.value_and_grad(_loss, argnums=(0, 1))

    def one_microbatch(ex, loss_target):
        ex = dict(ex)
        diff = ex.pop(TWIN_DIFF_INPUT)
        return grad_fn(weights, diff, {**shared, **ex}, loss_target)

    if N_MICROBATCH == 1:
        loss, (grad_w, grad_x) = one_microbatch(per_example, given["loss_target"])
    else:
        def body(carry, xs):
            loss_sum, grad_sum = carry
            l_k, (gw_k, gx_k) = one_microbatch(xs[0], xs[1])
            with _jax.named_scope("update"):
                return (loss_sum + l_k, _jax.tree.map(_jnp.add, grad_sum, gw_k)), gx_k

        init = (_jnp.zeros((), _jnp.float32), _jax.tree.map(_jnp.zeros_like, weights))
        (loss, grad_w), grad_x = _jax.lax.scan(body, init, (per_example, given["loss_target"]))
    with _jax.named_scope("update"):
        delta_w, new_m, new_v = {}, {}, {}
        for n in TWIN_WEIGHTS:
            delta_w[n], new_m[n], new_v[n] = _adamw(weights[n], grad_w[n], given["m_" + n], given["v_" + n])
    return (loss, grad_x, *[grad_w[n] for n in TWIN_WEIGHTS], *[delta_w[n] for n in TWIN_WEIGHTS],
            *[new_m[n] for n in TWIN_WEIGHTS], *[new_v[n] for n in TWIN_WEIGHTS])
```

```python
import functools

import jax
import jax.numpy as jnp
from jax import lax
from jax.experimental import pallas as pl
from jax.experimental.pallas import tpu as pltpu

F32 = jnp.float32
BF16 = jnp.bfloat16

D_MODEL = 1024
HEAD_DIM = 64
H_FOX = 8
H_SB = 8
N_HEADS = H_FOX + H_SB
MIX = N_HEADS * HEAD_DIM
ATTN_IN = 3 * MIX + H_FOX
ATTN_IN_PAD = 3 * MIX + 128
D_FF = 2816
EPS = 1e-6
SCALE = HEAD_DIM ** -0.5
NEG = -1e30

ADAM_LR = 0.001
ADAM_B1 = 0.9
ADAM_B2 = 0.999
ADAM_EPS = 1e-08
ADAM_WD = 0.01
ADAM_STEP = 10

VMEM_LIMIT = 56 * 1024 * 1024
HALO = 8
BQ = 256
N_CHIPS = 4
MESH = pl.DeviceIdType.MESH


def _params(**kw):
    return pltpu.CompilerParams(vmem_limit_bytes=VMEM_LIMIT, **kw)


def _dot(a, b):
    return jnp.dot(a, b, preferred_element_type=F32)


def _dot_nt(a, b):
    return lax.dot_general(a, b, (((1,), (1,)), ((), ())), preferred_element_type=F32)


def _dot_tn(a, b):
    return lax.dot_general(a, b, (((0,), (0,)), ((), ())), preferred_element_type=F32)


def _split2(x):
    hi = x.astype(BF16)
    lo = (x - hi.astype(F32)).astype(BF16)
    return hi, lo


def _split3(x):
    hi = x.astype(BF16)
    r = x - hi.astype(F32)
    mid = r.astype(BF16)
    lo = (r - mid.astype(F32)).astype(BF16)
    return hi, mid, lo


def mm_nn(a, b, *, add=None, out_dtype=F32, parts=1, tm=512, tn=512, name):
    m, k = a.shape
    n = b.shape[1]
    np_ = n // parts
    nb = np_ // tn
    assert m % tm == 0 and np_ % tn == 0

    def body(*refs):
        if add is None:
            a_ref, b_ref, o_ref = refs
            acc = _dot(a_ref[...].astype(BF16), b_ref[...])
        else:
            a_ref, b_ref, r_ref, o_ref = refs
            acc = _dot(a_ref[...].astype(BF16), b_ref[...]) + r_ref[...]
        o_ref[...] = acc.astype(out_dtype).reshape(o_ref.shape)

    in_specs = [pl.BlockSpec((tm, k), lambda i, j: (i, 0)), pl.BlockSpec((k, tn), lambda i, j: (0, j))]
    args = [a, b]
    if add is not None:
        in_specs.append(pl.BlockSpec((tm, tn), lambda i, j: (i, j)))
        args.append(add)
    if parts == 1:
        out_spec = pl.BlockSpec((tm, tn), lambda i, j: (i, j))
        out_shape = jax.ShapeDtypeStruct((m, n), out_dtype)
    else:
        out_spec = pl.BlockSpec((1, tm, tn), lambda i, j: (j // nb, i, j % nb))
        out_shape = jax.ShapeDtypeStruct((parts, m, np_), out_dtype)
    return pl.pallas_call(body, grid=(m // tm, n // tn), in_specs=in_specs, out_specs=out_spec,
                          out_shape=out_shape, compiler_params=_params(), name=name)(*args)


def mm_nt(a3, b, *, out_dtype=F32, tm=512, tn=512, name):
    p, m, kp = a3.shape
    n = b.shape[0]
    assert m % tm == 0 and n % tn == 0 and b.shape[1] == p * kp

    def body(a_ref, b_ref, o_ref, acc_ref):
        part = pl.program_id(2)
        prod = _dot_nt(a_ref[0].astype(BF16), b_ref[...])

        @pl.when(part == 0)
        def _():
            acc_ref[...] = prod

        @pl.when(part > 0)
        def _():
            acc_ref[...] += prod

        @pl.when(part == p - 1)
        def _():
            o_ref[...] = acc_ref[...].astype(out_dtype)

    return pl.pallas_call(
        body, grid=(m // tm, n // tn, p),
        in_specs=[pl.BlockSpec((1, tm, kp), lambda i, j, q: (q, i, 0)), pl.BlockSpec((tn, kp), lambda i, j, q: (j, q))],
        out_specs=pl.BlockSpec((tm, tn), lambda i, j, q: (i, j)),
        out_shape=jax.ShapeDtypeStruct((m, n), out_dtype),
        scratch_shapes=[pltpu.VMEM((tm, tn), F32)],
        compiler_params=_params(), name=name)(a3, b)


def mm_tn(a, b3, *, tk=512, tn=512, tt=512, name):
    t, k = a.shape
    p, _, np_ = b3.shape
    nb = np_ // tn
    assert t % tt == 0 and k % tk == 0 and np_ % tn == 0

    def body(a_ref, b_ref, o_ref):
        prod = _dot_tn(a_ref[...].astype(BF16), b_ref[0].astype(BF16))

        @pl.when(pl.program_id(2) == 0)
        def _():
            o_ref[...] = prod

        @pl.when(pl.program_id(2) > 0)
        def _():
            o_ref[...] += prod

    return pl.pallas_call(
        body, grid=(k // tk, p * nb, t // tt),
        in_specs=[pl.BlockSpec((tt, tk), lambda i, j, s: (s, i)), pl.BlockSpec((1, tt, tn), lambda i, j, s: (j // nb, s, j % nb))],
        out_specs=pl.BlockSpec((tk, tn), lambda i, j, s: (i, j)),
        out_shape=jax.ShapeDtypeStruct((k, p * np_), F32),
        compiler_params=_params(), name=name)(a, b3)


def rms_fwd(h, g, *, name, tm=512):
    t, d = h.shape

    def body(h_ref, g_ref, o_ref):
        x = h_ref[...]
        r = lax.rsqrt(jnp.mean(x * x, axis=-1, keepdims=True) + EPS)
        o_ref[...] = (x * r * g_ref[...]).astype(BF16)

    return pl.pallas_call(
        body, grid=(t // tm,),
        in_specs=[pl.BlockSpec((tm, d), lambda i: (i, 0)), pl.BlockSpec((1, d), lambda i: (0, 0))],
        out_specs=pl.BlockSpec((tm, d), lambda i: (i, 0)),
        out_shape=jax.ShapeDtypeStruct((t, d), BF16), compiler_params=_params(), name=name)(h, g)


def rms_bwd(dxn, h, g, dres, *, name, tm=512):
    t, d = h.shape

    def body(dxn_ref, h_ref, g_ref, dres_ref, dh_ref, dg_ref):
        x = h_ref[...]
        dy = dxn_ref[...]
        r = lax.rsqrt(jnp.mean(x * x, axis=-1, keepdims=True) + EPS)
        gy = dy * g_ref[...]
        dot = jnp.mean(gy * x, axis=-1, keepdims=True)
        dh_ref[...] = dres_ref[...] + r * gy - x * (r * r * r * dot)
        part = jnp.sum(dy * x * r, axis=0, keepdims=True)

        @pl.when(pl.program_id(0) == 0)
        def _():
            dg_ref[...] = part

        @pl.when(pl.program_id(0) > 0)
        def _():
            dg_ref[...] += part

    row = pl.BlockSpec((tm, d), lambda i: (i, 0))
    vec = pl.BlockSpec((1, d), lambda i: (0, 0))
    return pl.pallas_call(
        body, grid=(t // tm,), in_specs=[row, row, vec, row], out_specs=[row, vec],
        out_shape=[jax.ShapeDtypeStruct((t, d), F32), jax.ShapeDtypeStruct((1, d), F32)],
        compiler_params=_params(), name=name)(dxn, h, g, dres)


def _causal3(x, w):
    return w[0:1] * pltpu.roll(x, 2, 0) + w[1:2] * pltpu.roll(x, 1, 0) + w[2:3] * x


def _anticausal3(z, w):
    n = z.shape[0]
    return w[2:3] * z + w[1:2] * pltpu.roll(z, n - 1, 0) + w[0:1] * pltpu.roll(z, n - 2, 0)


def _prev_spec(part, tm, tc, nrow8):
    del nrow8
    return pl.BlockSpec((1, HALO, tc), lambda j, i: (part, jnp.maximum(i * (tm // HALO) - 1, 0), j))


def _next_spec(part, tm, tc, nrow8):
    return pl.BlockSpec((1, HALO, tc), lambda j, i: (part, jnp.minimum((i + 1) * (tm // HALO), nrow8 - 1), j))


def _tile_spec(part, tm, tc):
    return pl.BlockSpec((1, tm, tc), lambda j, i: (part, i, j))


def _shifted_rows(x_ext, tm):
    x2 = pltpu.roll(x_ext, 2, 0)[HALO:HALO + tm]
    x1 = pltpu.roll(x_ext, 1, 0)[HALO:HALO + tm]
    x0 = x_ext[HALO:HALO + tm]
    return x2, x1, x0


def _acc_rows(ref, val, first):
    @pl.when(first)
    def _():
        ref[...] = val

    @pl.when(jnp.logical_not(first))
    def _():
        ref[...] += val


def ffn_act_fwd(up2, cw2, *, name, tm=512, tc=256):
    _, t, f = up2.shape
    n8 = t // HALO

    def body(g_ref, v_ref, gp_ref, vp_ref, wg_ref, wv_ref, o_ref):
        first = pl.program_id(1) == 0
        keep = jnp.where(first, 0.0, 1.0)
        g_ext = jnp.concatenate([gp_ref[0] * keep, g_ref[0]], axis=0)
        v_ext = jnp.concatenate([vp_ref[0] * keep, v_ref[0]], axis=0)
        ug = _causal3(g_ext, wg_ref[0])[HALO:]
        uv = _causal3(v_ext, wv_ref[0])[HALO:]
        o_ref[...] = (ug * jax.nn.sigmoid(ug) * uv).astype(BF16)

    wspec = lambda part: pl.BlockSpec((1, 3, tc), lambda j, i: (part, 0, j))
    return pl.pallas_call(
        body, grid=(f // tc, t // tm),
        in_specs=[_tile_spec(0, tm, tc), _tile_spec(1, tm, tc), _prev_spec(0, tm, tc, n8), _prev_spec(1, tm, tc, n8),
                  wspec(0), wspec(1)],
        out_specs=pl.BlockSpec((tm, tc), lambda j, i: (i, j)),
        out_shape=jax.ShapeDtypeStruct((t, f), BF16), compiler_params=_params(), name=name)(up2, up2, up2, up2, cw2, cw2)


def ffn_act_bwd(dact, up2, cw2, *, name, tm=512, tc=256):
    _, t, f = up2.shape
    n8 = t // HALO

    def body(d_ref, dn_ref, g_ref, v_ref, gp_ref, vp_ref, gn_ref, vn_ref, wg_ref, wv_ref, dup_ref, dw_ref):
        i = pl.program_id(1)
        first = i == 0
        keep_p = jnp.where(first, 0.0, 1.0)
        keep_n = jnp.where(i == pl.num_programs(1) - 1, 0.0, 1.0)
        wg = wg_ref[0]
        wv = wv_ref[0]
        g_ext = jnp.concatenate([gp_ref[0] * keep_p, g_ref[0], gn_ref[0]], axis=0)
        v_ext = jnp.concatenate([vp_ref[0] * keep_p, v_ref[0], vn_ref[0]], axis=0)
        d_ext = jnp.concatenate([d_ref[...], dn_ref[...] * keep_n], axis=0)
        ug = _causal3(g_ext, wg)[HALO:]
        uv = _causal3(v_ext, wv)[HALO:]
        s = jax.nn.sigmoid(ug)
        dg = d_ext * uv * (s * (1.0 + ug * (1.0 - s)))
        dv = d_ext * (ug * s)
        dup_ref[0] = _anticausal3(dg, wg)[:tm].astype(BF16)
        dup_ref[1] = _anticausal3(dv, wv)[:tm].astype(BF16)
        g2, g1, g0 = _shifted_rows(g_ext, tm)
        v2, v1, v0 = _shifted_rows(v_ext, tm)
        dgt = dg[:tm]
        dvt = dv[:tm]
        zero = jnp.zeros((HALO - 3, tc), F32)
        rows_g = [jnp.sum(dgt * x, axis=0, keepdims=True) for x in (g2, g1, g0)] + [zero]
        rows_v = [jnp.sum(dvt * x, axis=0, keepdims=True) for x in (v2, v1, v0)] + [zero]
        _acc_rows(dw_ref, jnp.stack([jnp.concatenate(rows_g, axis=0), jnp.concatenate(rows_v, axis=0)]), first)

    wspec = lambda part: pl.BlockSpec((1, 3, tc), lambda j, i: (part, 0, j))
    return pl.pallas_call(
        body, grid=(f // tc, t // tm),
        in_specs=[pl.BlockSpec((tm, tc), lambda j, i: (i, j)),
                  pl.BlockSpec((HALO, tc), lambda j, i: (jnp.minimum((i + 1) * (tm // HALO), n8 - 1), j)),
                  _tile_spec(0, tm, tc), _tile_spec(1, tm, tc), _prev_spec(0, tm, tc, n8), _prev_spec(1, tm, tc, n8),
                  _next_spec(0, tm, tc, n8), _next_spec(1, tm, tc, n8), wspec(0), wspec(1)],
        out_specs=[pl.BlockSpec((2, tm, tc), lambda j, i: (0, i, j)), pl.BlockSpec((2, HALO, tc), lambda j, i: (0, 0, j))],
        out_shape=[jax.ShapeDtypeStruct((2, t, f), BF16), jax.ShapeDtypeStruct((2, HALO, f), F32)],
        compiler_params=_params(), name=name)(dact, dact, up2, up2, up2, up2, up2, up2, cw2, cw2)


def conv_mix_fwd(proj3, ck, *, name, tm=512, tc=256):
    _, t, c = proj3.shape
    n8 = t // HALO

    def body(b_ref, c_ref, u_ref, cp_ref, up_ref, w_ref, o_ref):
        keep = jnp.where(pl.program_id(1) == 0, 0.0, 1.0)
        cu_ext = jnp.concatenate([cp_ref[0] * up_ref[0] * keep, c_ref[0] * u_ref[0]], axis=0)
        o_ref[...] = (b_ref[0] * _causal3(cu_ext, w_ref[0])[HALO:]).astype(BF16)

    return pl.pallas_call(
        body, grid=(c // tc, t // tm),
        in_specs=[_tile_spec(0, tm, tc), _tile_spec(1, tm, tc), _tile_spec(2, tm, tc), _prev_spec(1, tm, tc, n8),
                  _prev_spec(2, tm, tc, n8), pl.BlockSpec((1, 3, tc), lambda j, i: (0, 0, j))],
        out_specs=pl.BlockSpec((tm, tc), lambda j, i: (i, j)),
        out_shape=jax.ShapeDtypeStruct((t, c), BF16), compiler_params=_params(), name=name)(proj3, proj3, proj3, proj3, proj3, ck)


def conv_mix_bwd(dy, proj3, ck, *, name, tm=512, tc=256):
    _, t, c = proj3.shape
    n8 = t // HALO

    def body(d_ref, dn_ref, b_ref, c_ref, u_ref, cp_ref, up_ref, bn_ref, w_ref, dp_ref, dw_ref):
        i = pl.program_id(1)
        first = i == 0
        keep_p = jnp.where(first, 0.0, 1.0)
        keep_n = jnp.where(i == pl.num_programs(1) - 1, 0.0, 1.0)
        w = w_ref[0]
        cu_ext = jnp.concatenate([cp_ref[0] * up_ref[0] * keep_p, c_ref[0] * u_ref[0]], axis=0)
        cv = _causal3(cu_ext, w)[HALO:]
        dyt = d_ref[...]
        d_ext = jnp.concatenate([dyt, dn_ref[...] * keep_n], axis=0)
        b_ext = jnp.concatenate([b_ref[0], bn_ref[0]], axis=0)
        dcv = d_ext * b_ext
        dcu = _anticausal3(dcv, w)[:tm]
        dp_ref[0] = (dyt * cv).astype(BF16)
        dp_ref[1] = (dcu * u_ref[0]).astype(BF16)
        dp_ref[2] = (dcu * c_ref[0]).astype(BF16)
        x2, x1, x0 = _shifted_rows(cu_ext, tm)
        dcvt = dcv[:tm]
        rows = [jnp.sum(dcvt * x, axis=0, keepdims=True) for x in (x2, x1, x0)] + [jnp.zeros((HALO - 3, tc), F32)]
        _acc_rows(dw_ref, jnp.concatenate(rows, axis=0)[None], first)

    return pl.pallas_call(
        body, grid=(c // tc, t // tm),
        in_specs=[pl.BlockSpec((tm, tc), lambda j, i: (i, j)),
                  pl.BlockSpec((HALO, tc), lambda j, i: (jnp.minimum((i + 1) * (tm // HALO), n8 - 1), j)),
                  _tile_spec(0, tm, tc), _tile_spec(1, tm, tc), _tile_spec(2, tm, tc),
                  _prev_spec(1, tm, tc, n8), _prev_spec(2, tm, tc, n8),
                  _next_spec(0, tm, tc, n8), pl.BlockSpec((1, 3, tc), lambda j, i: (0, 0, j))],
        out_specs=[pl.BlockSpec((3, tm, tc), lambda j, i: (0, i, j)), pl.BlockSpec((1, HALO, tc), lambda j, i: (0, 0, j))],
        out_shape=[jax.ShapeDtypeStruct((3, t, c), BF16), jax.ShapeDtypeStruct((1, HALO, c), F32)],
        compiler_params=_params(), name=name)(dy, dy, proj3, proj3, proj3, proj3, proj3, proj3, ck)


def _head_sums(x, bd):
    hi, lo = _split2(x)
    return _dot(hi, bd) + _dot(lo, bd)


def attn_prep_fwd(proj, gq, gk, fbias, bd, *, name, tm=256):
    t = proj.shape[0]

    def body(q_ref, k_ref, v_ref, f_ref, gq_ref, gk_ref, fb_ref, bd_ref, qs_ref, kn_ref, vb_ref, lf_ref):
        bd = bd_ref[...]

        def headnorm(x_ref, g_ref, o_ref, scale):
            for c in range(MIX // 128):
                sl = slice(128 * c, 128 * (c + 1))
                x = x_ref[:, sl]
                r = lax.rsqrt(_head_sums(x * x, bd) * (1.0 / HEAD_DIM) + EPS)
                o_ref[:, sl] = (x * r * (g_ref[:, sl] * scale)).astype(BF16)

        headnorm(q_ref, gq_ref, qs_ref, SCALE)
        headnorm(k_ref, gk_ref, kn_ref, 1.0)
        vb_ref[...] = v_ref[...].astype(BF16)
        fl = f_ref[...] + fb_ref[...]
        logf = jnp.minimum(fl, 0.0) - jnp.log(1.0 + jnp.exp(-jnp.abs(fl)))
        lf_ref[...] = logf.T[0:H_FOX, :]

    col = lambda c: pl.BlockSpec((tm, MIX), lambda i: (i, c))
    vec = pl.BlockSpec((1, MIX), lambda i: (0, 0))
    out = pl.BlockSpec((tm, MIX), lambda i: (i, 0))
    return pl.pallas_call(
        body, grid=(t // tm,),
        in_specs=[col(0), col(1), col(2), pl.BlockSpec((tm, 128), lambda i: (i, 3 * MIX // 128)), vec, vec,
                  pl.BlockSpec((1, 128), lambda i: (0, 0)), pl.BlockSpec((128, 128), lambda i: (0, 0))],
        out_specs=[out, out, out, pl.BlockSpec((H_FOX, tm), lambda i: (0, i))],
        out_shape=[jax.ShapeDtypeStruct((t, MIX), BF16)] * 3 + [jax.ShapeDtypeStruct((H_FOX, t), F32)],
        compiler_params=_params(), name=name)(proj, proj, proj, proj, gq, gk, fbias, bd)


def attn_prep_bwd(proj, dqs, dkn, dv, dfl, gq, gk, bd, *, name, tm=256):
    t = proj.shape[0]

    def body(q_ref, k_ref, dq_ref, dk_ref, dv_ref, dfl_ref, gq_ref, gk_ref, bd_ref, dp_ref, dgq_ref, dgk_ref):
        bd = bd_ref[...]
        first = pl.program_id(0) == 0

        def back(x_ref, d_ref, g_ref, col0, scale, dg_ref):
            parts = []
            for c in range(MIX // 128):
                sl = slice(128 * c, 128 * (c + 1))
                x = x_ref[:, sl]
                r = lax.rsqrt(_head_sums(x * x, bd) * (1.0 / HEAD_DIM) + EPS)
                dn = d_ref[:, sl] * scale
                gy = dn * g_ref[:, sl]
                hs = _head_sums(gy * x, bd) * (1.0 / HEAD_DIM)
                dp_ref[:, col0 + 128 * c:col0 + 128 * (c + 1)] = (r * gy - x * (r * r * r * hs)).astype(BF16)
                parts.append(jnp.sum(dn * x * r, axis=0, keepdims=True))
            _acc_rows(dg_ref, jnp.concatenate(parts, axis=1), first)

        back(q_ref, dq_ref, gq_ref, 0, SCALE, dgq_ref)
        back(k_ref, dk_ref, gk_ref, MIX, 1.0, dgk_ref)
        dp_ref[:, 2 * MIX:3 * MIX] = dv_ref[...].astype(BF16)
        dp_ref[:, 3 * MIX:] = dfl_ref[...]

    col = lambda c: pl.BlockSpec((tm, MIX), lambda i: (i, c))
    row = pl.BlockSpec((tm, MIX), lambda i: (i, 0))
    vec = pl.BlockSpec((1, MIX), lambda i: (0, 0))
    return pl.pallas_call(
        body, grid=(t // tm,),
        in_specs=[col(0), col(1), row, row, row, pl.BlockSpec((tm, 128), lambda i: (i, 0)), vec, vec,
                  pl.BlockSpec((128, 128), lambda i: (0, 0))],
        out_specs=[pl.BlockSpec((tm, ATTN_IN_PAD), lambda i: (i, 0)), vec, vec],
        out_shape=[jax.ShapeDtypeStruct((t, ATTN_IN_PAD), BF16), jax.ShapeDtypeStruct((1, MIX), F32),
                   jax.ShapeDtypeStruct((1, MIX), F32)],
        compiler_params=_params(), name=name)(proj, proj, dqs, dkn, dv, dfl, gq, gk, bd)


def gate_cumsum(logf3, tri, *, name):
    nc, r, _ = logf3.shape

    def body(x_ref, tri_ref, o_ref):
        tri_m = tri_ref[...]

        def step(c, carry):
            hi, mid, lo = _split3(x_ref[c])
            cs = _dot(hi, tri_m) + _dot(mid, tri_m) + _dot(lo, tri_m) + carry
            o_ref[c] = cs
            return cs[:, 127:128]

        lax.fori_loop(0, nc, step, jnp.zeros((r, 1), F32))

    return pl.pallas_call(body, out_shape=jax.ShapeDtypeStruct(logf3.shape, F32), compiler_params=_params(),
                          name=name)(logf3, tri)


def gate_cumsum_bwd(dcum3, logf3, tri, *, name):
    nc, r, _ = dcum3.shape

    def body(x_ref, lf_ref, tri_ref, o_ref, s_ref):
        tri_m = tri_ref[...]

        def step(n, carry):
            car, tot = carry
            c = nc - 1 - n
            hi, mid, lo = _split3(x_ref[c])
            cs = _dot(hi, tri_m) + _dot(mid, tri_m) + _dot(lo, tri_m) + car
            dl = cs * (1.0 - jnp.exp(lf_ref[c]))
            o_ref[c] = dl
            return cs[:, 0:1], tot + dl

        _, tot = lax.fori_loop(0, nc, step, (jnp.zeros((r, 1), F32), jnp.zeros((r, 128), F32)))
        s_ref[...] = jnp.broadcast_to(jnp.sum(tot, axis=1, keepdims=True), tot.shape)

    return pl.pallas_call(body, out_shape=[jax.ShapeDtypeStruct(dcum3.shape, F32), jax.ShapeDtypeStruct((r, 128), F32)],
                          compiler_params=_params(), name=name)(dcum3, logf3, tri)


def _causal_iota():
    row = lax.broadcasted_iota(jnp.int32, (BQ, BQ), 0)
    col = lax.broadcasted_iota(jnp.int32, (BQ, BQ), 1)
    return row, col


def _head_specs(nj, head0):
    qin = pl.BlockSpec((1, BQ, HEAD_DIM), lambda h, i: (h + head0, i, 0))
    kin = pl.BlockSpec((1, nj, BQ, HEAD_DIM), lambda h, i: (h + head0, 0, 0, 0))
    qspec = pl.BlockSpec((1, BQ, HEAD_DIM), lambda h, i: (h, i, 0))
    kspec = pl.BlockSpec((1, nj, BQ, HEAD_DIM), lambda h, i: (h, 0, 0, 0))
    return qin, kin, qspec, kspec


def fox_fwd(qs, kn4, v4, fcol, frow4, *, name):
    _, t, dh = qs.shape
    nh = H_FOX
    nj = t // BQ

    def body(q_ref, k_ref, v_ref, fc_ref, fr_ref, o_ref, lse_ref):
        i = pl.program_id(1)
        q = q_ref[0]
        fq = fc_ref[0]

        def block(j, carry, diag):
            m, l, acc = carry
            s = _dot_nt(q, k_ref[0, j]) + fq - fr_ref[0, j]
            if diag:
                row, col = _causal_iota()
                s = jnp.where(col <= row, s, NEG)
            m_new = jnp.maximum(m, jnp.max(s, axis=1, keepdims=True))
            p = jnp.exp(s - m_new)
            alpha = jnp.exp(m - m_new)
            l = alpha * l + jnp.sum(p, axis=1, keepdims=True)
            p_hi, p_lo = _split2(p)
            acc = alpha * acc + (_dot(p_hi, v_ref[0, j]) + _dot(p_lo, v_ref[0, j]))
            return m_new, l, acc

        init = (jnp.full((BQ, 1), NEG, F32), jnp.zeros((BQ, 1), F32), jnp.zeros((BQ, dh), F32))
        carry = lax.fori_loop(0, i, lambda j, c: block(j, c, False), init)
        m, l, acc = block(i, carry, True)
        o_ref[0] = acc / l
        lse_ref[0] = m + jnp.log(l)

    qin, kin, qspec, _ = _head_specs(nj, 0)
    cspec = pl.BlockSpec((1, BQ, 1), lambda h, i: (h, i, 0))
    return pl.pallas_call(
        body, grid=(nh, nj),
        in_specs=[qin, kin, kin, cspec, pl.BlockSpec((1, nj, 1, BQ), lambda h, i: (h, 0, 0, 0))],
        out_specs=[qspec, cspec],
        out_shape=[jax.ShapeDtypeStruct((nh, t, dh), F32), jax.ShapeDtypeStruct((nh, t, 1), F32)],
        compiler_params=_params(), name=name)(qs, kn4, v4, fcol, frow4)


def fox_bwd(qs, kn4, v4, fcol, frow4, o, do, lse, *, name):
    _, t, dh = qs.shape
    nh = H_FOX
    nj = t // BQ

    def body(q_ref, k_ref, v_ref, fc_ref, fr_ref, o_ref, do_ref, lse_ref, dq_ref, dk_ref, dv_ref, dfk_ref):
        i = pl.program_id(1)

        @pl.when(i == 0)
        def _():
            dk_ref[...] = jnp.zeros_like(dk_ref)
            dv_ref[...] = jnp.zeros_like(dv_ref)
            dfk_ref[...] = jnp.zeros_like(dfk_ref)

        q = q_ref[0]
        do_b = do_ref[0]
        fq = fc_ref[0]
        lse_q = lse_ref[0]
        dd = jnp.sum(do_b.astype(F32) * o_ref[0].astype(F32), axis=1, keepdims=True)

        def block(j, dq, diag):
            k = k_ref[0, j]
            s = _dot_nt(q, k) + fq - fr_ref[0, j]
            p = jnp.exp(s - lse_q)
            if diag:
                row, col = _causal_iota()
                p = jnp.where(col <= row, p, 0.0)
            ds = p * (_dot_nt(do_b, v_ref[0, j]) - dd)
            ds_b = ds.astype(BF16)
            dk_ref[0, j] += _dot_tn(ds_b, q)
            dv_ref[0, j] += _dot_tn(p.astype(BF16), do_b)
            dfk_ref[0, j] -= jnp.sum(ds, axis=0, keepdims=True)
            return dq + _dot(ds_b, k)

        dq = lax.fori_loop(0, i, lambda j, c: block(j, c, False), jnp.zeros((BQ, dh), F32))
        dq_ref[0] = block(i, dq, True)

    qin, kin, qspec, kspec = _head_specs(nj, 0)
    cspec = pl.BlockSpec((1, BQ, 1), lambda h, i: (h, i, 0))
    rspec = pl.BlockSpec((1, nj, 1, BQ), lambda h, i: (h, 0, 0, 0))
    return pl.pallas_call(
        body, grid=(nh, nj),
        in_specs=[qin, kin, kin, cspec, rspec, qspec, qin, cspec],
        out_specs=[qspec, kspec, kspec, rspec],
        out_shape=[jax.ShapeDtypeStruct((nh, t, dh), F32), jax.ShapeDtypeStruct((nh, nj, BQ, dh), F32),
                   jax.ShapeDtypeStruct((nh, nj, BQ, dh), F32), jax.ShapeDtypeStruct((nh, nj, 1, BQ), F32)],
        compiler_params=_params(), name=name)(qs, kn4, v4, fcol, frow4, o, do, lse)


def _sb_logs(z, diag):
    e = jnp.exp(-jnp.abs(z))
    sp = jnp.log(1.0 + e)
    logb = jnp.minimum(z, 0.0) - sp
    lom = -jnp.maximum(z, 0.0) - sp
    strict = None
    if diag:
        row, col = _causal_iota()
        strict = col < row
        lom = jnp.where(strict, lom, 0.0)
    return logb, lom, e, strict


def sb_fwd(qs, kn4, v4, tri, *, name):
    _, t, dh = qs.shape
    nh = H_SB
    nj = t // BQ
    assert nj <= 128

    def body(q_ref, k_ref, v_ref, tri_ref, o_ref, rs_ref):
        i = pl.program_id(1)
        q = q_ref[0]
        tri_m = tri_ref[...]
        lane = lax.broadcasted_iota(jnp.int32, (BQ, 128), 1)

        def block(j, carry, diag):
            run, acc, rall = carry
            logb, lom, _, strict = _sb_logs(_dot_nt(q, k_ref[0, j]), diag)
            hi, lo = _split2(lom)
            w = jnp.exp(logb + (_dot(hi, tri_m) + _dot(lo, tri_m)) + run)
            if diag:
                w = jnp.where(strict, w, 0.0)
            acc = acc + _dot(w.astype(BF16), v_ref[0, j])
            rall = jnp.where(lane == j, run, rall)
            return run + jnp.sum(lom, axis=1, keepdims=True), acc, rall

        init = (jnp.zeros((BQ, 1), F32), jnp.zeros((BQ, dh), F32), jnp.zeros((BQ, 128), F32))
        carry = block(i, init, True)
        _, acc, rall = lax.fori_loop(0, i, lambda n, c: block(i - 1 - n, c, False), carry)
        o_ref[0] = acc.astype(BF16)
        rs_ref[0] = rall

    qin, kin, qspec, _ = _head_specs(nj, H_FOX)
    rspec = pl.BlockSpec((1, BQ, 128), lambda h, i: (h, i, 0))
    return pl.pallas_call(
        body, grid=(nh, nj),
        in_specs=[qin, kin, kin, pl.BlockSpec((BQ, BQ), lambda h, i: (0, 0))],
        out_specs=[qspec, rspec],
        out_shape=[jax.ShapeDtypeStruct((nh, t, dh), BF16), jax.ShapeDtypeStruct((nh, t, 128), F32)],
        compiler_params=_params(), name=name)(qs, kn4, v4, tri)


def sb_bwd(qs, kn4, v4, tri, do, rsave, *, name):
    _, t, dh = qs.shape
    nh = H_SB
    nj = t // BQ

    def body(q_ref, k_ref, v_ref, tri_ref, do_ref, rs_ref, dq_ref, dk_ref, dv_ref):
        i = pl.program_id(1)

        @pl.when(i == 0)
        def _():
            dk_ref[...] = jnp.zeros_like(dk_ref)
            dv_ref[...] = jnp.zeros_like(dv_ref)

        q = q_ref[0]
        do_b = do_ref[0]
        tri_m = tri_ref[...]
        rall = rs_ref[0]
        lane = lax.broadcasted_iota(jnp.int32, (BQ, 128), 1)

        def block(j, carry, diag):
            dq, ecar = carry
            k = k_ref[0, j]
            z = _dot_nt(q, k)
            logb, lom, e, strict = _sb_logs(z, diag)
            hi, lo = _split2(lom)
            run = jnp.sum(jnp.where(lane == j, rall, 0.0), axis=1, keepdims=True)
            w = jnp.exp(logb + (_dot(hi, tri_m) + _dot(lo, tri_m)) + run)
            if diag:
                w = jnp.where(strict, w, 0.0)
            da = w * _dot_nt(do_b, v_ref[0, j])
            before = _dot_nt(da.astype(BF16), tri_m) + ecar
            inv = 1.0 / (1.0 + e)
            beta = jnp.where(z >= 0.0, 1.0, e) * inv
            one_minus = jnp.where(z >= 0.0, e, 1.0) * inv
            dz = da * one_minus - before * beta
            if diag:
                dz = jnp.where(strict, dz, 0.0)
            dz_b = dz.astype(BF16)
            dk_ref[0, j] += _dot_tn(dz_b, q)
            dv_ref[0, j] += _dot_tn(w.astype(BF16), do_b)
            return dq + _dot(dz_b, k), ecar + jnp.sum(da, axis=1, keepdims=True)

        carry = lax.fori_loop(0, i, lambda j, c: block(j, c, False), (jnp.zeros((BQ, dh), F32), jnp.zeros((BQ, 1), F32)))
        dq, _ = block(i, carry, True)
        dq_ref[0] = dq

    qin, kin, qspec, kspec = _head_specs(nj, H_FOX)
    return pl.pallas_call(
        body, grid=(nh, nj),
        in_specs=[qin, kin, kin, pl.BlockSpec((BQ, BQ), lambda h, i: (0, 0)), qin,
                  pl.BlockSpec((1, BQ, 128), lambda h, i: (h, i, 0))],
        out_specs=[qspec, kspec, kspec],
        out_shape=[jax.ShapeDtypeStruct((nh, t, dh), F32), jax.ShapeDtypeStruct((nh, nj, BQ, dh), F32),
                   jax.ShapeDtypeStruct((nh, nj, BQ, dh), F32)],
        compiler_params=_params(), name=name)(qs, kn4, v4, tri, do, rsave)


def loss_head(y, target, *, name, tm=512):
    t, d = y.shape

    def body(y_ref, t_ref, l_ref, dy_ref, acc_ref):
        i = pl.program_id(0)
        diff = y_ref[...] - t_ref[...]
        dy_ref[...] = diff * (1.0 / d)
        part = jnp.sum(diff * diff, axis=0, keepdims=True)

        @pl.when(i == 0)
        def _():
            acc_ref[...] = part

        @pl.when(i > 0)
        def _():
            acc_ref[...] += part

        @pl.when(i == pl.num_programs(0) - 1)
        def _():
            l_ref[...] = jnp.full(l_ref.shape, (0.5 / d) * jnp.sum(acc_ref[...]), F32)

    row = pl.BlockSpec((tm, d), lambda i: (i, 0))
    return pl.pallas_call(
        body, grid=(t // tm,), in_specs=[row, row],
        out_specs=[pl.BlockSpec((8, 128), lambda i: (0, 0)), row],
        out_shape=[jax.ShapeDtypeStruct((8, 128), F32), jax.ShapeDtypeStruct((t, d), F32)],
        scratch_shapes=[pltpu.VMEM((1, d), F32)], compiler_params=_params(), name=name)(y, target)


def _to_heads(a):
    t = a.shape[0]
    return a.reshape(t, N_HEADS, HEAD_DIM).transpose(1, 0, 2)


def _from_heads(a):
    t = a.shape[1]
    return a.transpose(1, 0, 2).reshape(t, MIX)


def _lanes_to_chunks(a):
    r, t = a.shape
    return a.reshape(r, t // 128, 128).transpose(1, 0, 2)


def _chunks_to_lanes(a):
    nc, r, _ = a.shape
    return a.transpose(1, 0, 2).reshape(r, nc * 128)


def _constants():
    idx = jnp.arange(128)
    bd = (idx[:, None] // HEAD_DIM == idx[None, :] // HEAD_DIM).astype(BF16)
    tri_le = (idx[:, None] <= idx[None, :]).astype(BF16)
    tri_ge = (idx[:, None] >= idx[None, :]).astype(BF16)
    jdx = jnp.arange(BQ)
    tri_gt = (jdx[:, None] > jdx[None, :]).astype(BF16)
    return dict(bd=bd, tri_le=tri_le, tri_ge=tri_ge, tri_gt=tri_gt)


def attn_layer_fwd(h, w, cst):
    t = h.shape[0]
    nj = t // BQ
    xn = rms_fwd(h, w["norm"], name="rms_fwd")
    proj = mm_nn(xn, w["w_in"], tn=640, name="attn_in_proj")
    qs, kn, vb, logf = attn_prep_fwd(proj, w["gq"], w["gk"], w["fbias"], cst["bd"], name="attn_prep_fwd")
    logf3 = _lanes_to_chunks(logf)
    cum = _chunks_to_lanes(gate_cumsum(logf3, cst["tri_le"], name="gate_cumsum"))
    fcol = cum.reshape(H_FOX, t, 1)
    frow4 = cum.reshape(H_FOX, nj, 1, BQ)
    qh = _to_heads(qs)
    kh4 = _to_heads(kn).reshape(N_HEADS, nj, BQ, HEAD_DIM)
    vh4 = _to_heads(vb).reshape(N_HEADS, nj, BQ, HEAD_DIM)
    o_f, lse = fox_fwd(qh, kh4, vh4, fcol, frow4, name="fox_fwd")
    o_s, rsave = sb_fwd(qh, kh4, vh4, cst["tri_gt"], name="sb_fwd")
    o = _from_heads(jnp.concatenate([o_f.astype(BF16), o_s], axis=0))
    h2 = mm_nn(o, w["w_out"], add=h, name="mix_out_proj")
    saved = dict(h=h, xn=xn, proj=proj, logf3=logf3, fcol=fcol, frow4=frow4, qh=qh, kh4=kh4, vh4=vh4,
                 o_f=o_f, lse=lse, rsave=rsave, o=o)
    return h2, saved


def attn_layer_bwd(dh, w, s, cst):
    t = dh.shape[0]
    dh3 = dh[None]
    do = mm_nt(dh3, w["w_out"], out_dtype=BF16, name="mix_out_bwd_bf16")
    g_w_out = mm_tn(s["o"], dh3, name="mix_out_wgrad")
    doh = _to_heads(do)
    dq_f, dk_f, dv_f, dfk = fox_bwd(s["qh"], s["kh4"], s["vh4"], s["fcol"], s["frow4"], s["o_f"], doh, s["lse"],
                                    name="fox_bwd")
    dq_s, dk_s, dv_s = sb_bwd(s["qh"], s["kh4"], s["vh4"], cst["tri_gt"], doh, s["rsave"], name="sb_bwd")
    dqs = _from_heads(jnp.concatenate([dq_f, dq_s], axis=0))
    dkn = _from_heads(jnp.concatenate([dk_f, dk_s], axis=0).reshape(N_HEADS, t, HEAD_DIM))
    dv = _from_heads(jnp.concatenate([dv_f, dv_s], axis=0).reshape(N_HEADS, t, HEAD_DIM))
    dcum3 = _lanes_to_chunks(dfk.reshape(H_FOX, t))
    dfl3, dbias = gate_cumsum_bwd(dcum3, s["logf3"], cst["tri_ge"], name="gate_cumsum_bwd")
    dfl = jnp.pad(_chunks_to_lanes(dfl3).T, ((0, 0), (0, 128 - H_FOX))).astype(BF16)
    dproj, dgq, dgk = attn_prep_bwd(s["proj"], dqs, dkn, dv, dfl, w["gq"], w["gk"], cst["bd"], name="attn_prep_bwd")
    g_w_in = mm_tn(s["xn"], dproj[None], tn=640, name="attn_in_wgrad")[:, :ATTN_IN]
    dxn = mm_nt(dproj[None], w["w_in"], name="attn_in_bwd")
    dh2, g_norm = rms_bwd(dxn, s["h"], w["norm"], dh, name="rms_bwd")
    dgq = dgq.reshape(N_HEADS, HEAD_DIM)
    dgk = dgk.reshape(N_HEADS, HEAD_DIM)
    grads = dict(norm=g_norm[0], w_in=g_w_in, f_bias=dbias[:, 0], fox_q=dgq[:H_FOX].sum(0), fox_k=dgk[:H_FOX].sum(0),
                 sb_q=dgq[H_FOX:].sum(0), sb_k=dgk[H_FOX:].sum(0), w_out=g_w_out)
    return dh2, grads


def conv_layer_fwd(h, w):
    xn = rms_fwd(h, w["norm"], name="rms_fwd")
    proj3 = mm_nn(xn, w["w_in"], parts=3, name="conv_in_proj")
    y = conv_mix_fwd(proj3, w["ck"], name="conv_mix_fwd")
    h2 = mm_nn(y, w["w_out"], add=h, name="mix_out_proj")
    return h2, dict(h=h, xn=xn, proj3=proj3, y=y)


def conv_layer_bwd(dh, w, s):
    dh3 = dh[None]
    dy = mm_nt(dh3, w["w_out"], name="mix_out_bwd")
    g_w_out = mm_tn(s["y"], dh3, name="mix_out_wgrad")
    dproj3, dck = conv_mix_bwd(dy, s["proj3"], w["ck"], name="conv_mix_bwd")
    g_w_in = mm_tn(s["xn"], dproj3, name="conv_in_wgrad")
    dxn = mm_nt(dproj3, w["w_in"], name="conv_in_bwd")
    dh2, g_norm = rms_bwd(dxn, s["h"], w["norm"], dh, name="rms_bwd")
    return dh2, dict(norm=g_norm[0], w_in=g_w_in, ck=dck[0, :3], w_out=g_w_out)


def ffn_layer_fwd(h, w):
    xn = rms_fwd(h, w["norm"], name="rms_fwd")
    up2 = mm_nn(xn, w["w_up"], parts=2, tn=1408, name="ffn_up_proj")
    act = ffn_act_fwd(up2, w["cw2"], name="ffn_act_fwd")
    h2 = mm_nn(act, w["w_down"], add=h, name="ffn_down_proj")
    return h2, dict(h=h, xn=xn, up2=up2, act=act)


def ffn_layer_bwd(dh, w, s):
    dh3 = dh[None]
    dact = mm_nt(dh3, w["w_down"], tn=1408, name="ffn_down_bwd")
    g_w_down = mm_tn(s["act"], dh3, tk=1408, name="ffn_down_wgrad")
    dup2, dcw = ffn_act_bwd(dact, s["up2"], w["cw2"], name="ffn_act_bwd")
    g_w_up = mm_tn(s["xn"], dup2, tn=1408, name="ffn_up_wgrad")
    dxn = mm_nt(dup2, w["w_up"], name="ffn_up_bwd")
    dh2, g_norm = rms_bwd(dxn, s["h"], w["norm"], dh, name="rms_bwd")
    g_cw = jnp.concatenate([dcw[0, :3], dcw[1, :3]], axis=1)
    return dh2, dict(norm=g_norm[0], w_up=g_w_up, cw=g_cw, w_down=g_w_down)


def forward_backward(x, target, wa, wc, wf):
    cst = _constants()
    depth = len(wf)
    h = x
    saved = []
    for layer in range(depth):
        i = layer // 2
        if layer % 2 == 0:
            h, sm = attn_layer_fwd(h, wa[i], cst)
        else:
            h, sm = conv_layer_fwd(h, wc[i])
        h, sf = ffn_layer_fwd(h, wf[layer])
        saved.append((sm, sf))
    loss_blk, dh = loss_head(h, target, name="loss_head")
    ga, gc, gf = [None] * len(wa), [None] * len(wc), [None] * depth
    for layer in reversed(range(depth)):
        i = layer // 2
        sm, sf = saved[layer]
        dh, gf[layer] = ffn_layer_bwd(dh, wf[layer], sf)
        if layer % 2 == 0:
            dh, ga[i] = attn_layer_bwd(dh, wa[i], sm, cst)
        else:
            dh, gc[i] = conv_layer_bwd(dh, wc[i], sm)
    return loss_blk, dh, ga, gc, gf


def _part_rows(shape, width, row_mult):
    n = 1
    for s in shape:
        n *= s
    rows = -(-n // width)
    return -(-rows // row_mult) * row_mult


def _pack_rows(arrs, width, row_mult, dtype, total_rows=None):
    parts = []
    used = 0
    for a in arrs:
        rows = _part_rows(a.shape, width, row_mult)
        flat = a.astype(dtype).reshape(-1)
        flat = jnp.pad(flat, (0, rows * width - flat.shape[0]))
        parts.append(flat.reshape(rows, width))
        used += rows
    if total_rows is not None and total_rows > used:
        parts.append(jnp.zeros((total_rows - used, width), dtype))
    return jnp.concatenate(parts, axis=0)


def _unpack_rows(packed, shapes, width, row_mult):
    out = []
    off = 0
    for shape in shapes:
        rows = _part_rows(shape, width, row_mult)
        n = 1
        for s in shape:
            n *= s
        out.append(packed[off:off + rows].reshape(-1)[:n].reshape(shape))
        off += rows
    return out


BIG_NAMES = ("attn_w_in", "attn_w_out", "conv_w_in", "conv_w_out", "ffn_w_up", "ffn_w_down")
BIG_AXIS = {"attn_w_in": 2, "attn_w_out": 1, "conv_w_in": 2, "conv_w_out": 1, "ffn_w_up": 2, "ffn_w_down": 1}
BIG_WIDTH = 1024
BIG_ROW_MULT = 16
BIG_TILE = 512
SMALL_SHARDED = ("conv_norm", "conv_kernel", "ffn_conv")
SMALL_AXIS = {"conv_norm": 1, "conv_kernel": 2, "ffn_conv": 2}
SMALL_REPLICATED = ("attn_norm", "attn_f_bias", "fox_q_gain", "fox_k_gain", "sb_q_gain", "sb_k_gain", "ffn_norm")
WEIGHT_ORDER = ("attn_norm", "attn_w_in", "attn_f_bias", "fox_q_gain", "fox_k_gain", "sb_q_gain", "sb_k_gain",
                "attn_w_out", "conv_norm", "conv_w_in", "conv_kernel", "conv_w_out", "ffn_norm", "ffn_w_up",
                "ffn_conv", "ffn_w_down")


def _big_total_rows(shapes):
    used = sum(_part_rows(s, BIG_WIDTH, BIG_ROW_MULT) for s in shapes)
    return -(-used // BIG_TILE) * BIG_TILE


def _place():
    x, y, c = lax.axis_index("x"), lax.axis_index("y"), lax.axis_index("c")
    other_chips = [(1 - x, y), (x, 1 - y), (1 - x, 1 - y)]
    return x, y, c, other_chips


_ANY = pl.BlockSpec(memory_space=pl.ANY)


def gather_chips(big, small, *, name):
    def body(big_ref, small_ref, obig_ref, osmall_ref, send_sems, recv_sems, local_sems):
        x, y, c, chips = _place()
        k = 2 * x + y
        pairs = [(big_ref, obig_ref), (small_ref, osmall_ref)]

        def copy(j, n, slot):
            px, py = chips[j]
            src, dst = pairs[n]
            return pltpu.make_async_remote_copy(src_ref=src, dst_ref=dst.at[slot], send_sem=send_sems.at[2 * j + n],
                                                recv_sem=recv_sems.at[2 * j + n], device_id=(px, py, c),
                                                device_id_type=MESH)

        local = [pltpu.make_async_copy(src, dst.at[k], local_sems.at[n]) for n, (src, dst) in enumerate(pairs)]
        for cp in local:
            cp.start()
        sends = [copy(j, n, k) for j in range(3) for n in range(2)]
        for cp in sends:
            cp.start()
        for j, (px, py) in enumerate(chips):
            for n in range(2):
                copy(j, n, 2 * px + py).wait_recv()
        for cp in sends:
            cp.wait_send()
        for cp in local:
            cp.wait()

    return pl.pallas_call(
        body, in_specs=[_ANY, _ANY], out_specs=[_ANY, _ANY],
        out_shape=[jax.ShapeDtypeStruct((N_CHIPS,) + big.shape, big.dtype),
                   jax.ShapeDtypeStruct((N_CHIPS,) + small.shape, small.dtype)],
        scratch_shapes=[pltpu.SemaphoreType.DMA((6,)), pltpu.SemaphoreType.DMA((6,)), pltpu.SemaphoreType.DMA((2,))],
        name=name)(big, small)


def scatter_chips(chunks, *, name):
    def body(g_ref, o_ref, send_sems, recv_sems, local_sem):
        x, y, c, chips = _place()
        k = 2 * x + y

        def copy(j, src_slot, dst_slot):
            px, py = chips[j]
            return pltpu.make_async_remote_copy(src_ref=g_ref.at[src_slot], dst_ref=o_ref.at[dst_slot],
                                                send_sem=send_sems.at[j], recv_sem=recv_sems.at[j],
                                                device_id=(px, py, c), device_id_type=MESH)

        local = pltpu.make_async_copy(g_ref.at[k], o_ref.at[k], local_sem)
        local.start()
        sends = [copy(j, 2 * px + py, k) for j, (px, py) in enumerate(chips)]
        for cp in sends:
            cp.start()
        for j, (px, py) in enumerate(chips):
            copy(j, k, 2 * px + py).wait_recv()
        for cp in sends:
            cp.wait_send()
        local.wait()

    return pl.pallas_call(
        body, in_specs=[_ANY], out_specs=_ANY, out_shape=jax.ShapeDtypeStruct(chunks.shape, chunks.dtype),
        scratch_shapes=[pltpu.SemaphoreType.DMA((3,)), pltpu.SemaphoreType.DMA((3,)), pltpu.SemaphoreType.DMA],
        name=name)(chunks)


def swap_cores(a, *, name):
    def body(a_ref, o_ref, send_sem, recv_sem):
        x, y, c, _ = _place()
        cp = pltpu.make_async_remote_copy(src_ref=a_ref, dst_ref=o_ref, send_sem=send_sem, recv_sem=recv_sem,
                                          device_id=(x, y, 1 - c), device_id_type=MESH)
        cp.start()
        cp.wait()

    return pl.pallas_call(
        body, in_specs=[_ANY], out_specs=_ANY, out_shape=jax.ShapeDtypeStruct(a.shape, a.dtype),
        scratch_shapes=[pltpu.SemaphoreType.DMA, pltpu.SemaphoreType.DMA], name=name)(a)


def allreduce_small(p, *, name):
    r, w = p.shape

    def body(p_ref, o_ref, buf, send_sems, recv_sems):
        x, y, c, _ = _place()
        me = 4 * x + 2 * y + c
        buf[me] = p_ref[...]

        def peer_of(m):
            return (1 - x if m & 4 else x, 1 - y if m & 2 else y, 1 - c if m & 1 else c)

        def copy(m, slot):
            return pltpu.make_async_remote_copy(src_ref=p_ref, dst_ref=buf.at[slot], send_sem=send_sems.at[m - 1],
                                                recv_sem=recv_sems.at[m - 1], device_id=peer_of(m),
                                                device_id_type=MESH)

        sends = [copy(m, me) for m in range(1, 8)]
        for cp in sends:
            cp.start()
        for m in range(1, 8):
            px, py, pc = peer_of(m)
            copy(m, 4 * px + 2 * py + pc).wait_recv()
        for cp in sends:
            cp.wait_send()
        acc = buf[0]
        for d in range(1, 8):
            acc = acc + buf[d]
        o_ref[...] = acc

    vm = pl.BlockSpec(memory_space=pltpu.VMEM)
    return pl.pallas_call(
        body, in_specs=[vm], out_specs=vm, out_shape=jax.ShapeDtypeStruct((r, w), F32),
        scratch_shapes=[pltpu.VMEM((8, r, w), F32), pltpu.SemaphoreType.DMA((7,)), pltpu.SemaphoreType.DMA((7,))],
        name=name)(p)


def sum_chips(rv, *, name):
    _, r, w = rv.shape

    def body(a_ref, b_ref, c_ref, d_ref, o_ref):
        o_ref[...] = ((a_ref[0].astype(F32) + b_ref[0].astype(F32)) + c_ref[0].astype(F32)) + d_ref[0].astype(F32)

    spec = lambda kk: pl.BlockSpec((1, BIG_TILE, w), lambda i: (kk, i, 0))
    return pl.pallas_call(
        body, grid=(r // BIG_TILE,), in_specs=[spec(0), spec(1), spec(2), spec(3)],
        out_specs=pl.BlockSpec((BIG_TILE, w), lambda i: (i, 0)), out_shape=jax.ShapeDtypeStruct((r, w), F32),
        compiler_params=_params(), name=name)(rv, rv, rv, rv)


def add_pair(a, b, *, name):
    r, w = a.shape

    def body(a_ref, b_ref, o_ref):
        o_ref[...] = a_ref[...] + b_ref[...]

    spec = pl.BlockSpec((BIG_TILE, w), lambda i: (i, 0))
    return pl.pallas_call(body, grid=(r // BIG_TILE,), in_specs=[spec, spec], out_specs=spec,
                          out_shape=jax.ShapeDtypeStruct((r, w), F32), compiler_params=_params(), name=name)(a, b)


def adamw(w, g, m, v, *, tm, name):
    r, c = w.shape
    assert r % tm == 0

    def body(w_ref, g_ref, m_ref, v_ref, d_ref, nm_ref, nv_ref):
        g_ = g_ref[...]
        m_ = ADAM_B1 * m_ref[...] + (1.0 - ADAM_B1) * g_
        v_ = ADAM_B2 * v_ref[...] + (1.0 - ADAM_B2) * (g_ * g_)
        m_hat = m_ / (1.0 - ADAM_B1 ** ADAM_STEP)
        v_hat = v_ / (1.0 - ADAM_B2 ** ADAM_STEP)
        d_ref[...] = -ADAM_LR * (m_hat / (jnp.sqrt(v_hat) + ADAM_EPS) + ADAM_WD * w_ref[...])
        nm_ref[...] = m_
        nv_ref[...] = v_

    spec = pl.BlockSpec((tm, c), lambda i: (i, 0))
    return pl.pallas_call(body, grid=(r // tm,), in_specs=[spec] * 4, out_specs=[spec] * 3,
                          out_shape=[jax.ShapeDtypeStruct((r, c), F32)] * 3, compiler_params=_params(), name=name)(w, g, m, v)


def kernel(x, attn_norm, attn_w_in, attn_f_bias, fox_q_gain, fox_k_gain, sb_q_gain, sb_k_gain, attn_w_out, conv_norm, conv_w_in, conv_kernel, conv_w_out, ffn_norm, ffn_w_up, ffn_conv, ffn_w_down, loss_target, m_attn_norm, m_attn_w_in, m_attn_f_bias, m_fox_q_gain, m_fox_k_gain, m_sb_q_gain, m_sb_k_gain, m_attn_w_out, m_conv_norm, m_conv_w_in, m_conv_kernel, m_conv_w_out, m_ffn_norm, m_ffn_w_up, m_ffn_conv, m_ffn_w_down, v_attn_norm, v_attn_w_in, v_attn_f_bias, v_fox_q_gain, v_fox_k_gain, v_sb_q_gain, v_sb_k_gain, v_attn_w_out, v_conv_norm, v_conv_w_in, v_conv_kernel, v_conv_w_out, v_ffn_norm, v_ffn_w_up, v_ffn_conv, v_ffn_w_down):
    a = dict(locals())
    chip = 2 * lax.axis_index("x") + lax.axis_index("y")
    n_attn, n_conv, depth = attn_norm.shape[0], conv_norm.shape[0], ffn_norm.shape[0]

    big_shapes = [a[n].shape for n in BIG_NAMES]
    big_rows = _big_total_rows(big_shapes)
    small_shapes = [a[n].shape for n in SMALL_SHARDED]
    packed_w = _pack_rows([a[n] for n in BIG_NAMES], BIG_WIDTH, BIG_ROW_MULT, BF16, big_rows)
    packed_s = _pack_rows([a[n] for n in SMALL_SHARDED], 128, 8, F32)
    gath_w, gath_s = gather_chips(packed_w, packed_s, name="gather_weights")
    full = {}
    per_chip = [_unpack_rows(gath_w[kk], big_shapes, BIG_WIDTH, BIG_ROW_MULT) for kk in range(N_CHIPS)]
    for n, name in enumerate(BIG_NAMES):
        full[name] = jnp.concatenate([per_chip[kk][n] for kk in range(N_CHIPS)], axis=BIG_AXIS[name])
    per_chip = [_unpack_rows(gath_s[kk], small_shapes, 128, 8) for kk in range(N_CHIPS)]
    for n, name in enumerate(SMALL_SHARDED):
        full[name] = jnp.concatenate([per_chip[kk][n] for kk in range(N_CHIPS)], axis=SMALL_AXIS[name])

    wa, wc, wf = [], [], []
    for i in range(n_attn):
        wa.append(dict(
            norm=attn_norm[i][None],
            w_in=jnp.pad(full["attn_w_in"][i], ((0, 0), (0, ATTN_IN_PAD - ATTN_IN))),
            fbias=jnp.pad(attn_f_bias[i], (0, 128 - H_FOX))[None],
            gq=jnp.concatenate([jnp.tile(fox_q_gain[i], H_FOX), jnp.tile(sb_q_gain[i], H_SB)])[None],
            gk=jnp.concatenate([jnp.tile(fox_k_gain[i], H_FOX), jnp.tile(sb_k_gain[i], H_SB)])[None],
            w_out=full["attn_w_out"][i]))
    for i in range(n_conv):
        wc.append(dict(norm=full["conv_norm"][i][None], w_in=full["conv_w_in"][i], ck=full["conv_kernel"][i][None],
                       w_out=full["conv_w_out"][i]))
    for l in range(depth):
        cw = full["ffn_conv"][l]
        wf.append(dict(norm=ffn_norm[l][None], w_up=full["ffn_w_up"][l], cw2=jnp.stack([cw[:, :D_FF], cw[:, D_FF:]]),
                       w_down=full["ffn_w_down"][l]))
    loss_blk, grad_x, ga, gc, gf = forward_backward(x[0], loss_target[0], wa, wc, wf)

    g_full = {
        "attn_w_in": jnp.stack([g["w_in"] for g in ga]), "attn_w_out": jnp.stack([g["w_out"] for g in ga]),
        "conv_w_in": jnp.stack([g["w_in"] for g in gc]), "conv_w_out": jnp.stack([g["w_out"] for g in gc]),
        "ffn_w_up": jnp.stack([g["w_up"] for g in gf]), "ffn_w_down": jnp.stack([g["w_down"] for g in gf]),
    }
    chunks = []
    for kk in range(N_CHIPS):
        parts = []
        for name in BIG_NAMES:
            width = a[name].shape[BIG_AXIS[name]]
            parts.append(lax.slice_in_dim(g_full[name], kk * width, (kk + 1) * width, axis=BIG_AXIS[name]))
        chunks.append(_pack_rows(parts, BIG_WIDTH, BIG_ROW_MULT, BF16, big_rows))
    landed = scatter_chips(jnp.stack(chunks), name="scatter_grads")
    mine = sum_chips(landed, name="sum_chips")
    theirs = swap_cores(mine, name="swap_cores")
    g_big = _unpack_rows(add_pair(mine, theirs, name="add_cores"), big_shapes, BIG_WIDTH, BIG_ROW_MULT)
    grads = dict(zip(BIG_NAMES, g_big))

    small_full = [
        loss_blk,
        jnp.stack([g["norm"] for g in ga]), jnp.stack([g["f_bias"] for g in ga]),
        jnp.stack([g["fox_q"] for g in ga]), jnp.stack([g["fox_k"] for g in ga]),
        jnp.stack([g["sb_q"] for g in ga]), jnp.stack([g["sb_k"] for g in ga]),
        jnp.stack([g["norm"] for g in gf]),
        jnp.stack([g["norm"] for g in gc]), jnp.stack([g["ck"] for g in gc]), jnp.stack([g["cw"] for g in gf]),
    ]
    summed = allreduce_small(_pack_rows(small_full, 128, 8, F32), name="allreduce_small")
    parts = _unpack_rows(summed, [p.shape for p in small_full], 128, 8)
    loss = parts[0][0, 0]
    for name, g in zip(SMALL_REPLICATED, parts[1:8]):
        grads[name] = g
    for name, g in zip(SMALL_SHARDED, parts[8:]):
        width = a[name].shape[SMALL_AXIS[name]]
        grads[name] = lax.dynamic_slice_in_dim(g, chip * width, width, axis=SMALL_AXIS[name])

    delta, new_m, new_v = {}, {}, {}
    for name in BIG_NAMES:
        shape = a[name].shape
        flat = lambda arr: arr.reshape(-1, shape[-1])
        d_, m_, v_ = adamw(flat(a[name]), flat(grads[name]), flat(a["m_" + name]), flat(a["v_" + name]), tm=256,
                           name="adamw")
        delta[name], new_m[name], new_v[name] = d_.reshape(shape), m_.reshape(shape), v_.reshape(shape)
    small_names = SMALL_REPLICATED + SMALL_SHARDED
    small_shapes_local = [a[n].shape for n in small_names]
    pack = lambda prefix, src: _pack_rows([src[prefix + n] for n in small_names], 128, 8, F32)
    packed = adamw(pack("", a), pack("", grads), pack("m_", a), pack("v_", a), tm=8, name="adamw_small")
    for store, buf in zip((delta, new_m, new_v), packed):
        for name, arr in zip(small_names, _unpack_rows(buf, small_shapes_local, 128, 8)):
            store[name] = arr

    return (loss, grad_x[None], *[grads[n] for n in WEIGHT_ORDER], *[delta[n] for n in WEIGHT_ORDER],
            *[new_m[n] for n in WEIGHT_ORDER], *[new_v[n] for n in WEIGHT_ORDER])
```

```python
import functools

import jax
import jax.numpy as jnp
from jax import lax
from jax.experimental import pallas as pl
from jax.experimental.pallas import tpu as pltpu

F32 = jnp.float32
BF16 = jnp.bfloat16

D_MODEL = 1024
HEAD_DIM = 64
H_FOX = 8
H_SB = 8
N_HEADS = H_FOX + H_SB
MIX = N_HEADS * HEAD_DIM
ATTN_IN = 3 * MIX + H_FOX
ATTN_IN_PAD = 3 * MIX + 128
D_FF = 2816
EPS = 1e-6
SCALE = HEAD_DIM ** -0.5
NEG = -1e30

ADAM_LR = 0.001
ADAM_B1 = 0.9
ADAM_B2 = 0.999
ADAM_EPS = 1e-08
ADAM_WD = 0.01
ADAM_STEP = 10

VMEM_LIMIT = 56 * 1024 * 1024
HALO = 8
BQ = 256
N_CHIPS = 4
MESH = pl.DeviceIdType.MESH


def _params(**kw):
    return pltpu.CompilerParams(vmem_limit_bytes=VMEM_LIMIT, **kw)


def _dot(a, b):
    return jnp.dot(a, b, preferred_element_type=F32)


def _dot_nt(a, b):
    return lax.dot_general(a, b, (((1,), (1,)), ((), ())), preferred_element_type=F32)


def _dot_tn(a, b):
    return lax.dot_general(a, b, (((0,), (0,)), ((), ())), preferred_element_type=F32)


def _split2(x):
    hi = x.astype(BF16)
    lo = (x - hi.astype(F32)).astype(BF16)
    return hi, lo


def _split3(x):
    hi = x.astype(BF16)
    r = x - hi.astype(F32)
    mid = r.astype(BF16)
    lo = (r - mid.astype(F32)).astype(BF16)
    return hi, mid, lo


def mm_nn(a, b, *, add=None, out_dtype=F32, parts=1, tm=512, tn=512, name):
    m, k = a.shape
    n = b.shape[1]
    np_ = n // parts
    nb = np_ // tn
    assert m % tm == 0 and np_ % tn == 0

    def body(*refs):
        if add is None:
            a_ref, b_ref, o_ref = refs
            acc = _dot(a_ref[...].astype(BF16), b_ref[...])
        else:
            a_ref, b_ref, r_ref, o_ref = refs
            acc = _dot(a_ref[...].astype(BF16), b_ref[...]) + r_ref[...]
        o_ref[...] = acc.astype(out_dtype).reshape(o_ref.shape)

    in_specs = [pl.BlockSpec((tm, k), lambda i, j: (i, 0)), pl.BlockSpec((k, tn), lambda i, j: (0, j))]
    args = [a, b]
    if add is not None:
        in_specs.append(pl.BlockSpec((tm, tn), lambda i, j: (i, j)))
        args.append(add)
    if parts == 1:
        out_spec = pl.BlockSpec((tm, tn), lambda i, j: (i, j))
        out_shape = jax.ShapeDtypeStruct((m, n), out_dtype)
    else:
        out_spec = pl.BlockSpec((1, tm, tn), lambda i, j: (j // nb, i, j % nb))
        out_shape = jax.ShapeDtypeStruct((parts, m, np_), out_dtype)
    return pl.pallas_call(body, grid=(m // tm, n // tn), in_specs=in_specs, out_specs=out_spec,
                          out_shape=out_shape, compiler_params=_params(), name=name)(*args)


def mm_nt(a3, b, *, out_dtype=F32, tm=512, tn=512, name):
    p, m, kp = a3.shape
    n = b.shape[0]
    assert m % tm == 0 and n % tn == 0 and b.shape[1] == p * kp

    def body(a_ref, b_ref, o_ref, acc_ref):
        part = pl.program_id(2)
        prod = _dot_nt(a_ref[0].astype(BF16), b_ref[...])

        @pl.when(part == 0)
        def _():
            acc_ref[...] = prod

        @pl.when(part > 0)
        def _():
            acc_ref[...] += prod

        @pl.when(part == p - 1)
        def _():
            o_ref[...] = acc_ref[...].astype(out_dtype)

    return pl.pallas_call(
        body, grid=(m // tm, n // tn, p),
        in_specs=[pl.BlockSpec((1, tm, kp), lambda i, j, q: (q, i, 0)), pl.BlockSpec((tn, kp), lambda i, j, q: (j, q))],
        out_specs=pl.BlockSpec((tm, tn), lambda i, j, q: (i, j)),
        out_shape=jax.ShapeDtypeStruct((m, n), out_dtype),
        scratch_shapes=[pltpu.VMEM((tm, tn), F32)],
        compiler_params=_params(), name=name)(a3, b)


def mm_tn(a, b3, *, tk=512, tn=512, tt=512, name):
    t, k = a.shape
    p, _, np_ = b3.shape
    nb = np_ // tn
    assert t % tt == 0 and k % tk == 0 and np_ % tn == 0

    def body(a_ref, b_ref, o_ref):
        prod = _dot_tn(a_ref[...].astype(BF16), b_ref[0].astype(BF16))

        @pl.when(pl.program_id(2) == 0)
        def _():
            o_ref[...] = prod

        @pl.when(pl.program_id(2) > 0)
        def _():
            o_ref[...] += prod

    return pl.pallas_call(
        body, grid=(k // tk, p * nb, t // tt),
        in_specs=[pl.BlockSpec((tt, tk), lambda i, j, s: (s, i)), pl.BlockSpec((1, tt, tn), lambda i, j, s: (j // nb, s, j % nb))],
        out_specs=pl.BlockSpec((tk, tn), lambda i, j, s: (i, j)),
        out_shape=jax.ShapeDtypeStruct((k, p * np_), F32),
        compiler_params=_params(), name=name)(a, b3)


def rms_fwd(h, g, *, name, tm=512):
    t, d = h.shape

    def body(h_ref, g_ref, o_ref):
        x = h_ref[...]
        r = lax.rsqrt(jnp.mean(x * x, axis=-1, keepdims=True) + EPS)
        o_ref[...] = (x * r * g_ref[...]).astype(BF16)

    return pl.pallas_call(
        body, grid=(t // tm,),
        in_specs=[pl.BlockSpec((tm, d), lambda i: (i, 0)), pl.BlockSpec((1, d), lambda i: (0, 0))],
        out_specs=pl.BlockSpec((tm, d), lambda i: (i, 0)),
        out_shape=jax.ShapeDtypeStruct((t, d), BF16), compiler_params=_params(), name=name)(h, g)


def rms_bwd(dxn, h, g, dres, *, name, tm=512):
    t, d = h.shape

    def body(dxn_ref, h_ref, g_ref, dres_ref, dh_ref, dg_ref):
        x = h_ref[...]
        dy = dxn_ref[...]
        r = lax.rsqrt(jnp.mean(x * x, axis=-1, keepdims=True) + EPS)
        gy = dy * g_ref[...]
        dot = jnp.mean(gy * x, axis=-1, keepdims=True)
        dh_ref[...] = dres_ref[...] + r * gy - x * (r * r * r * dot)
        part = jnp.sum(dy * x * r, axis=0, keepdims=True)

        @pl.when(pl.program_id(0) == 0)
        def _():
            dg_ref[...] = part

        @pl.when(pl.program_id(0) > 0)
        def _():
            dg_ref[...] += part

    row = pl.BlockSpec((tm, d), lambda i: (i, 0))
    vec = pl.BlockSpec((1, d), lambda i: (0, 0))
    return pl.pallas_call(
        body, grid=(t // tm,), in_specs=[row, row, vec, row], out_specs=[row, vec],
        out_shape=[jax.ShapeDtypeStruct((t, d), F32), jax.ShapeDtypeStruct((1, d), F32)],
        compiler_params=_params(), name=name)(dxn, h, g, dres)


def _causal3(x, w):
    return w[0:1] * pltpu.roll(x, 2, 0) + w[1:2] * pltpu.roll(x, 1, 0) + w[2:3] * x


def _anticausal3(z, w):
    n = z.shape[0]
    return w[2:3] * z + w[1:2] * pltpu.roll(z, n - 1, 0) + w[0:1] * pltpu.roll(z, n - 2, 0)


def _prev_spec(part, tm, tc, nrow8):
    del nrow8
    return pl.BlockSpec((1, HALO, tc), lambda j, i: (part, jnp.maximum(i * (tm // HALO) - 1, 0), j))


def _next_spec(part, tm, tc, nrow8):
    return pl.BlockSpec((1, HALO, tc), lambda j, i: (part, jnp.minimum((i + 1) * (tm // HALO), nrow8 - 1), j))


def _tile_spec(part, tm, tc):
    return pl.BlockSpec((1, tm, tc), lambda j, i: (part, i, j))


def _shifted_rows(x_ext, tm):
    x2 = pltpu.roll(x_ext, 2, 0)[HALO:HALO + tm]
    x1 = pltpu.roll(x_ext, 1, 0)[HALO:HALO + tm]
    x0 = x_ext[HALO:HALO + tm]
    return x2, x1, x0


def _acc_rows(ref, val, first):
    @pl.when(first)
    def _():
        ref[...] = val

    @pl.when(jnp.logical_not(first))
    def _():
        ref[...] += val


def ffn_act_fwd(up2, cw2, *, name, tm=512, tc=256):
    _, t, f = up2.shape
    n8 = t // HALO

    def body(g_ref, v_ref, gp_ref, vp_ref, wg_ref, wv_ref, o_ref):
        first = pl.program_id(1) == 0
        keep = jnp.where(first, 0.0, 1.0)
        g_ext = jnp.concatenate([gp_ref[0] * keep, g_ref[0]], axis=0)
        v_ext = jnp.concatenate([vp_ref[0] * keep, v_ref[0]], axis=0)
        ug = _causal3(g_ext, wg_ref[0])[HALO:]
        uv = _causal3(v_ext, wv_ref[0])[HALO:]
        o_ref[...] = (ug * jax.nn.sigmoid(ug) * uv).astype(BF16)

    wspec = lambda part: pl.BlockSpec((1, 3, tc), lambda j, i: (part, 0, j))
    return pl.pallas_call(
        body, grid=(f // tc, t // tm),
        in_specs=[_tile_spec(0, tm, tc), _tile_spec(1, tm, tc), _prev_spec(0, tm, tc, n8), _prev_spec(1, tm, tc, n8),
                  wspec(0), wspec(1)],
        out_specs=pl.BlockSpec((tm, tc), lambda j, i: (i, j)),
        out_shape=jax.ShapeDtypeStruct((t, f), BF16), compiler_params=_params(), name=name)(up2, up2, up2, up2, cw2, cw2)


def ffn_act_bwd(dact, up2, cw2, *, name, tm=512, tc=256):
    _, t, f = up2.shape
    n8 = t // HALO

    def body(d_ref, dn_ref, g_ref, v_ref, gp_ref, vp_ref, gn_ref, vn_ref, wg_ref, wv_ref, dup_ref, dw_ref):
        i = pl.program_id(1)
        first = i == 0
        keep_p = jnp.where(first, 0.0, 1.0)
        keep_n = jnp.where(i == pl.num_programs(1) - 1, 0.0, 1.0)
        wg = wg_ref[0]
        wv = wv_ref[0]
        g_ext = jnp.concatenate([gp_ref[0] * keep_p, g_ref[0], gn_ref[0]], axis=0)
        v_ext = jnp.concatenate([vp_ref[0] * keep_p, v_ref[0], vn_ref[0]], axis=0)
        d_ext = jnp.concatenate([d_ref[...], dn_ref[...] * keep_n], axis=0)
        ug = _causal3(g_ext, wg)[HALO:]
        uv = _causal3(v_ext, wv)[HALO:]
        s = jax.nn.sigmoid(ug)
        dg = d_ext * uv * (s * (1.0 + ug * (1.0 - s)))
        dv = d_ext * (ug * s)
        dup_ref[0] = _anticausal3(dg, wg)[:tm].astype(BF16)
        dup_ref[1] = _anticausal3(dv, wv)[:tm].astype(BF16)
        g2, g1, g0 = _shifted_rows(g_ext, tm)
        v2, v1, v0 = _shifted_rows(v_ext, tm)
        dgt = dg[:tm]
        dvt = dv[:tm]
        zero = jnp.zeros((HALO - 3, tc), F32)
        rows_g = [jnp.sum(dgt * x, axis=0, keepdims=True) for x in (g2, g1, g0)] + [zero]
        rows_v = [jnp.sum(dvt * x, axis=0, keepdims=True) for x in (v2, v1, v0)] + [zero]
        _acc_rows(dw_ref, jnp.stack([jnp.concatenate(rows_g, axis=0), jnp.concatenate(rows_v, axis=0)]), first)

    wspec = lambda part: pl.BlockSpec((1, 3, tc), lambda j, i: (part, 0, j))
    return pl.pallas_call(
        body, grid=(f // tc, t // tm),
        in_specs=[pl.BlockSpec((tm, tc), lambda j, i: (i, j)),
                  pl.BlockSpec((HALO, tc), lambda j, i: (jnp.minimum((i + 1) * (tm // HALO), n8 - 1), j)),
                  _tile_spec(0, tm, tc), _tile_spec(1, tm, tc), _prev_spec(0, tm, tc, n8), _prev_spec(1, tm, tc, n8),
                  _next_spec(0, tm, tc, n8), _next_spec(1, tm, tc, n8), wspec(0), wspec(1)],
        out_specs=[pl.BlockSpec((2, tm, tc), lambda j, i: (0, i, j)), pl.BlockSpec((2, HALO, tc), lambda j, i: (0, 0, j))],
        out_shape=[jax.ShapeDtypeStruct((2, t, f), BF16), jax.ShapeDtypeStruct((2, HALO, f), F32)],
        compiler_params=_params(), name=name)(dact, dact, up2, up2, up2, up2, up2, up2, cw2, cw2)


def conv_mix_fwd(proj3, ck, *, name, tm=512, tc=256):
    _, t, c = proj3.shape
    n8 = t // HALO

    def body(b_ref, c_ref, u_ref, cp_ref, up_ref, w_ref, o_ref):
        keep = jnp.where(pl.program_id(1) == 0, 0.0, 1.0)
        cu_ext = jnp.concatenate([cp_ref[0] * up_ref[0] * keep, c_ref[0] * u_ref[0]], axis=0)
        o_ref[...] = (b_ref[0] * _causal3(cu_ext, w_ref[0])[HALO:]).astype(BF16)

    return pl.pallas_call(
        body, grid=(c // tc, t // tm),
        in_specs=[_tile_spec(0, tm, tc), _tile_spec(1, tm, tc), _tile_spec(2, tm, tc), _prev_spec(1, tm, tc, n8),
                  _prev_spec(2, tm, tc, n8), pl.BlockSpec((1, 3, tc), lambda j, i: (0, 0, j))],
        out_specs=pl.BlockSpec((tm, tc), lambda j, i: (i, j)),
        out_shape=jax.ShapeDtypeStruct((t, c), BF16), compiler_params=_params(), name=name)(proj3, proj3, proj3, proj3, proj3, ck)


def conv_mix_bwd(dy, proj3, ck, *, name, tm=512, tc=256):
    _, t, c = proj3.shape
    n8 = t // HALO

    def body(d_ref, dn_ref, b_ref, c_ref, u_ref, cp_ref, up_ref, bn_ref, w_ref, dp_ref, dw_ref):
        i = pl.program_id(1)
        first = i == 0
        keep_p = jnp.where(first, 0.0, 1.0)
        keep_n = jnp.where(i == pl.num_programs(1) - 1, 0.0, 1.0)
        w = w_ref[0]
        cu_ext = jnp.concatenate([cp_ref[0] * up_ref[0] * keep_p, c_ref[0] * u_ref[0]], axis=0)
        cv = _causal3(cu_ext, w)[HALO:]
        dyt = d_ref[...]
        d_ext = jnp.concatenate([dyt, dn_ref[...] * keep_n], axis=0)
        b_ext = jnp.concatenate([b_ref[0], bn_ref[0]], axis=0)
        dcv = d_ext * b_ext
        dcu = _anticausal3(dcv, w)[:tm]
        dp_ref[0] = (dyt * cv).astype(BF16)
        dp_ref[1] = (dcu * u_ref[0]).astype(BF16)
        dp_ref[2] = (dcu * c_ref[0]).astype(BF16)
        x2, x1, x0 = _shifted_rows(cu_ext, tm)
        dcvt = dcv[:tm]
        rows = [jnp.sum(dcvt * x, axis=0, keepdims=True) for x in (x2, x1, x0)] + [jnp.zeros((HALO - 3, tc), F32)]
        _acc_rows(dw_ref, jnp.concatenate(rows, axis=0)[None], first)

    return pl.pallas_call(
        body, grid=(c // tc, t // tm),
        in_specs=[pl.BlockSpec((tm, tc), lambda j, i: (i, j)),
                  pl.BlockSpec((HALO, tc), lambda j, i: (jnp.minimum((i + 1) * (tm // HALO), n8 - 1), j)),
                  _tile_spec(0, tm, tc), _tile_spec(1, tm, tc), _tile_spec(2, tm, tc),
                  _prev_spec(1, tm, tc, n8), _prev_spec(2, tm, tc, n8),
                  _next_spec(0, tm, tc, n8), pl.BlockSpec((1, 3, tc), lambda j, i: (0, 0, j))],
        out_specs=[pl.BlockSpec((3, tm, tc), lambda j, i: (0, i, j)), pl.BlockSpec((1, HALO, tc), lambda j, i: (0, 0, j))],
        out_shape=[jax.ShapeDtypeStruct((3, t, c), BF16), jax.ShapeDtypeStruct((1, HALO, c), F32)],
        compiler_params=_params(), name=name)(dy, dy, proj3, proj3, proj3, proj3, proj3, proj3, ck)


def _head_sums(x, bd):
    hi, lo = _split2(x)
    return _dot(hi, bd) + _dot(lo, bd)


def attn_prep_fwd(proj, gq, gk, fbias, bd, *, name, tm=256):
    t = proj.shape[0]

    def body(q_ref, k_ref, v_ref, f_ref, gq_ref, gk_ref, fb_ref, bd_ref, qs_ref, kn_ref, vb_ref, lf_ref):
        bd = bd_ref[...]

        def headnorm(x_ref, g_ref, o_ref, scale):
            for c in range(MIX // 128):
                sl = slice(128 * c, 128 * (c + 1))
                x = x_ref[:, sl]
                r = lax.rsqrt(_head_sums(x * x, bd) * (1.0 / HEAD_DIM) + EPS)
                o_ref[:, sl] = (x * r * (g_ref[:, sl] * scale)).astype(BF16)

        headnorm(q_ref, gq_ref, qs_ref, SCALE)
        headnorm(k_ref, gk_ref, kn_ref, 1.0)
        vb_ref[...] = v_ref[...].astype(BF16)
        fl = f_ref[...] + fb_ref[...]
        logf = jnp.minimum(fl, 0.0) - jnp.log(1.0 + jnp.exp(-jnp.abs(fl)))
        lf_ref[...] = logf.T[0:H_FOX, :]

    col = lambda c: pl.BlockSpec((tm, MIX), lambda i: (i, c))
    vec = pl.BlockSpec((1, MIX), lambda i: (0, 0))
    out = pl.BlockSpec((tm, MIX), lambda i: (i, 0))
    return pl.pallas_call(
        body, grid=(t // tm,),
        in_specs=[col(0), col(1), col(2), pl.BlockSpec((tm, 128), lambda i: (i, 3 * MIX // 128)), vec, vec,
                  pl.BlockSpec((1, 128), lambda i: (0, 0)), pl.BlockSpec((128, 128), lambda i: (0, 0))],
        out_specs=[out, out, out, pl.BlockSpec((H_FOX, tm), lambda i: (0, i))],
        out_shape=[jax.ShapeDtypeStruct((t, MIX), BF16)] * 3 + [jax.ShapeDtypeStruct((H_FOX, t), F32)],
        compiler_params=_params(), name=name)(proj, proj, proj, proj, gq, gk, fbias, bd)


def attn_prep_bwd(proj, dqs, dkn, dv, dfl, gq, gk, bd, *, name, tm=256):
    t = proj.shape[0]

    def body(q_ref, k_ref, dq_ref, dk_ref, dv_ref, dfl_ref, gq_ref, gk_ref, bd_ref, dp_ref, dgq_ref, dgk_ref):
        bd = bd_ref[...]
        first = pl.program_id(0) == 0

        def back(x_ref, d_ref, g_ref, col0, scale, dg_ref):
            parts = []
            for c in range(MIX // 128):
                sl = slice(128 * c, 128 * (c + 1))
                x = x_ref[:, sl]
                r = lax.rsqrt(_head_sums(x * x, bd) * (1.0 / HEAD_DIM) + EPS)
                dn = d_ref[:, sl] * scale
                gy = dn * g_ref[:, sl]
                hs = _head_sums(gy * x, bd) * (1.0 / HEAD_DIM)
                dp_ref[:, col0 + 128 * c:col0 + 128 * (c + 1)] = (r * gy - x * (r * r * r * hs)).astype(BF16)
                parts.append(jnp.sum(dn * x * r, axis=0, keepdims=True))
            _acc_rows(dg_ref, jnp.concatenate(parts, axis=1), first)

        back(q_ref, dq_ref, gq_ref, 0, SCALE, dgq_ref)
        back(k_ref, dk_ref, gk_ref, MIX, 1.0, dgk_ref)
        dp_ref[:, 2 * MIX:3 * MIX] = dv_ref[...].astype(BF16)
        dp_ref[:, 3 * MIX:] = dfl_ref[...]

    col = lambda c: pl.BlockSpec((tm, MIX), lambda i: (i, c))
    row = pl.BlockSpec((tm, MIX), lambda i: (i, 0))
    vec = pl.BlockSpec((1, MIX), lambda i: (0, 0))
    return pl.pallas_call(
        body, grid=(t // tm,),
        in_specs=[col(0), col(1), row, row, row, pl.BlockSpec((tm, 128), lambda i: (i, 0)), vec, vec,
                  pl.BlockSpec((128, 128), lambda i: (0, 0))],
        out_specs=[pl.BlockSpec((tm, ATTN_IN_PAD), lambda i: (i, 0)), vec, vec],
        out_shape=[jax.ShapeDtypeStruct((t, ATTN_IN_PAD), BF16), jax.ShapeDtypeStruct((1, MIX), F32),
                   jax.ShapeDtypeStruct((1, MIX), F32)],
        compiler_params=_params(), name=name)(proj, proj, dqs, dkn, dv, dfl, gq, gk, bd)


def gate_cumsum(logf3, tri, *, name):
    nc, r, _ = logf3.shape

    def body(x_ref, tri_ref, o_ref):
        tri_m = tri_ref[...]

        def step(c, carry):
            hi, mid, lo = _split3(x_ref[c])
            cs = _dot(hi, tri_m) + _dot(mid, tri_m) + _dot(lo, tri_m) + carry
            o_ref[c] = cs
            return cs[:, 127:128]

        lax.fori_loop(0, nc, step, jnp.zeros((r, 1), F32))

    return pl.pallas_call(body, out_shape=jax.ShapeDtypeStruct(logf3.shape, F32), compiler_params=_params(),
                          name=name)(logf3, tri)


def gate_cumsum_bwd(dcum3, logf3, tri, *, name):
    nc, r, _ = dcum3.shape

    def body(x_ref, lf_ref, tri_ref, o_ref, s_ref):
        tri_m = tri_ref[...]

        def step(n, carry):
            car, tot = carry
            c = nc - 1 - n
            hi, mid, lo = _split3(x_ref[c])
            cs = _dot(hi, tri_m) + _dot(mid, tri_m) + _dot(lo, tri_m) + car
            dl = cs * (1.0 - jnp.exp(lf_ref[c]))
            o_ref[c] = dl
            return cs[:, 0:1], tot + dl

        _, tot = lax.fori_loop(0, nc, step, (jnp.zeros((r, 1), F32), jnp.zeros((r, 128), F32)))
        s_ref[...] = jnp.broadcast_to(jnp.sum(tot, axis=1, keepdims=True), tot.shape)

    return pl.pallas_call(body, out_shape=[jax.ShapeDtypeStruct(dcum3.shape, F32), jax.ShapeDtypeStruct((r, 128), F32)],
                          compiler_params=_params(), name=name)(dcum3, logf3, tri)


def _causal_iota():
    row = lax.broadcasted_iota(jnp.int32, (BQ, BQ), 0)
    col = lax.broadcasted_iota(jnp.int32, (BQ, BQ), 1)
    return row, col


def _head_specs(nj, head0):
    qin = pl.BlockSpec((1, BQ, HEAD_DIM), lambda h, i: (h + head0, i, 0))
    kin = pl.BlockSpec((1, nj, BQ, HEAD_DIM), lambda h, i: (h + head0, 0, 0, 0))
    qin2 = pl.BlockSpec((1, BQ, 2 * HEAD_DIM), lambda h, i: (h + head0, i, 0))
    kin2 = pl.BlockSpec((1, nj, BQ, 2 * HEAD_DIM), lambda h, i: (h + head0, 0, 0, 0))
    qspec = pl.BlockSpec((1, BQ, HEAD_DIM), lambda h, i: (h, i, 0))
    kspec2 = pl.BlockSpec((1, nj, BQ, 2 * HEAD_DIM), lambda h, i: (h, 0, 0, 0))
    return qin, kin, qin2, kin2, qspec, kspec2


STOP = -105.0
STOP_WIDE = -115.0
FIXED_REF_MAX = 40.0


def _store_kmax(k_ref, kmax_ref, nj):
    def step(j, mx):
        kf = k_ref[0, j].astype(F32)
        return jnp.maximum(mx, jnp.max(jnp.sum(kf * kf, axis=1, keepdims=True), axis=0, keepdims=True))

    mx = lax.fori_loop(0, nj, step, jnp.zeros((1, 1), F32))
    kmax_ref[...] = jnp.broadcast_to(jnp.sqrt(mx), kmax_ref.shape)


def _qk_bound(q, kmax_ref):
    qf = q.astype(F32)
    return jnp.sqrt(jnp.sum(qf * qf, axis=1, keepdims=True)) * kmax_ref[0:1, 0:1] * 1.001


def fox_fwd(qs, kn4, va4, fcol, frow4, *, name):
    _, t, dh = qs.shape
    nh = H_FOX
    nj = t // BQ

    def body(q_ref, k_ref, v_ref, fc_ref, fr_ref, o_ref, lse_ref, kmax_ref):
        i = pl.program_id(1)

        @pl.when(i == 0)
        def _():
            _store_kmax(k_ref, kmax_ref, nj)

        q = q_ref[0]
        fq = fc_ref[0]
        bound = _qk_bound(q, kmax_ref)
        row, col = _causal_iota()

        def gate_at_block_end(j):
            return fr_ref[0, j][:, BQ - 1:BQ]

        def pv(p, j):
            p_hi, p_lo = _split2(p)
            return _dot(p_hi, v_ref[0, j]) + _dot(p_lo, v_ref[0, j])

        def walk(block, live, init):
            carry = block(i, init, True)

            def cond(c):
                n, carry = c
                return jnp.logical_and(n < i, live(jnp.maximum(i - 1 - n, 0), carry))

            _, carry = lax.while_loop(cond, lambda c: (c[0] + 1, block(i - 1 - c[0], c[1], False)), (0, carry))
            return carry

        def fixed_reference(_):
            shift = fq - bound

            def block(j, acc, diag):
                p = jnp.exp(_dot_nt(q, k_ref[0, j]) + shift - fr_ref[0, j])
                if diag:
                    p = jnp.where(col <= row, p, 0.0)
                return acc + pv(p, j)

            def live(j, acc):
                return jnp.max(fq - gate_at_block_end(j) - jnp.log(acc[:, dh:dh + 1])) >= STOP_WIDE

            acc = walk(block, live, jnp.zeros((BQ, 2 * dh), F32))
            l = acc[:, dh:dh + 1]
            return acc[:, :dh] / l, bound + jnp.log(l)

        def running_maximum(_):
            def block(j, carry, diag):
                m, acc = carry
                s = _dot_nt(q, k_ref[0, j]) + fq - fr_ref[0, j]
                if diag:
                    s = jnp.where(col <= row, s, NEG)
                m_new = jnp.maximum(m, jnp.max(s, axis=1, keepdims=True))
                return m_new, jnp.exp(m - m_new) * acc + pv(jnp.exp(s - m_new), j)

            def live(j, carry):
                return jnp.max(bound + fq - gate_at_block_end(j) - carry[0]) >= STOP

            m, acc = walk(block, live, (jnp.full((BQ, 1), NEG, F32), jnp.zeros((BQ, 2 * dh), F32)))
            l = acc[:, dh:dh + 1]
            return acc[:, :dh] / l, m + jnp.log(l)

        o, lse = lax.cond(jnp.max(bound) < FIXED_REF_MAX, fixed_reference, running_maximum, 0)
        o_ref[0] = o
        lse_ref[0] = lse

    qin, kin, _, kin2, qspec, _ = _head_specs(nj, 0)
    cspec = pl.BlockSpec((1, BQ, 1), lambda h, i: (h, i, 0))
    return pl.pallas_call(
        body, grid=(nh, nj),
        in_specs=[qin, kin, kin2, cspec, pl.BlockSpec((1, nj, 1, BQ), lambda h, i: (h, 0, 0, 0))],
        out_specs=[qspec, cspec],
        out_shape=[jax.ShapeDtypeStruct((nh, t, dh), F32), jax.ShapeDtypeStruct((nh, t, 1), F32)],
        scratch_shapes=[pltpu.VMEM((8, 128), F32)],
        compiler_params=_params(), name=name)(qs, kn4, va4, fcol, frow4)


def fox_bwd(qs, kn4, v4, qa, doa, fcol, frow4, o, do, lse, *, name):
    _, t, dh = qs.shape
    nh = H_FOX
    nj = t // BQ

    def body(q_ref, k_ref, v_ref, qa_ref, doa_ref, fc_ref, fr_ref, o_ref, do_ref, lse_ref, dq_ref, dkv_ref, dfk_ref,
             kmax_ref):
        i = pl.program_id(1)

        @pl.when(i == 0)
        def _():
            dkv_ref[...] = jnp.zeros_like(dkv_ref)
            dfk_ref[...] = jnp.zeros_like(dfk_ref)
            _store_kmax(k_ref, kmax_ref, nj)

        q = q_ref[0]
        do_b = do_ref[0]
        fq = fc_ref[0]
        lse_q = lse_ref[0]
        dd = jnp.sum(do_b.astype(F32) * o_ref[0], axis=1, keepdims=True)
        rhs = jnp.concatenate([qa_ref[0], doa_ref[0]], axis=0)
        edge = _qk_bound(q, kmax_ref) + fq - lse_q

        def negligible(j):
            return jnp.logical_and(j < i, jnp.max(edge - fr_ref[0, j][:, BQ - 1:BQ]) < STOP_WIDE)

        first = lax.while_loop(negligible, lambda j: j + 1, 0)

        def block(j, dq, diag):
            k = k_ref[0, j]
            p = jnp.exp(_dot_nt(q, k) + fq - fr_ref[0, j] - lse_q)
            if diag:
                row, col = _causal_iota()
                p = jnp.where(col <= row, p, 0.0)
            ds = p * (_dot_nt(do_b, v_ref[0, j]) - dd)
            ds_b = ds.astype(BF16)
            dkv_ref[0, j] += _dot_tn(jnp.concatenate([ds_b, p.astype(BF16)], axis=0), rhs)
            dfk_ref[0, j] -= jnp.sum(ds, axis=0, keepdims=True)
            return dq + _dot(ds_b, k)

        dq = lax.fori_loop(first, i, lambda j, c: block(j, c, False), jnp.zeros((BQ, dh), F32))
        dq_ref[0] = block(i, dq, True)

    qin, kin, qin2, _, qspec, kspec2 = _head_specs(nj, 0)
    cspec = pl.BlockSpec((1, BQ, 1), lambda h, i: (h, i, 0))
    rspec = pl.BlockSpec((1, nj, 1, BQ), lambda h, i: (h, 0, 0, 0))
    return pl.pallas_call(
        body, grid=(nh, nj),
        in_specs=[qin, kin, kin, qin2, qin2, cspec, rspec, qspec, qin, cspec],
        out_specs=[qspec, kspec2, rspec],
        out_shape=[jax.ShapeDtypeStruct((nh, t, dh), F32), jax.ShapeDtypeStruct((nh, nj, BQ, 2 * dh), F32),
                   jax.ShapeDtypeStruct((nh, nj, 1, BQ), F32)],
        scratch_shapes=[pltpu.VMEM((8, 128), F32)],
        compiler_params=_params(), name=name)(qs, kn4, v4, qa, doa, fcol, frow4, o, do, lse)


def _sb_logs(z, diag):
    e = jnp.exp(-jnp.abs(z))
    sp = jnp.log(1.0 + e)
    logb = jnp.minimum(z, 0.0) - sp
    lom = -jnp.maximum(z, 0.0) - sp
    strict = None
    if diag:
        row, col = _causal_iota()
        strict = col < row
        lom = jnp.where(strict, lom, 0.0)
    return logb, lom, e, strict


def sb_fwd(qs, kn4, va4, tri, *, name):
    _, t, dh = qs.shape
    nh = H_SB
    nj = t // BQ
    assert nj <= 128

    def body(q_ref, k_ref, v_ref, tri_ref, o_ref, rs_ref):
        i = pl.program_id(1)
        q = q_ref[0]
        tri_m = tri_ref[...]
        lane = lax.broadcasted_iota(jnp.int32, (BQ, 128), 1)

        def block(j, carry, diag):
            run, acc, rall = carry
            logb, lom, _, strict = _sb_logs(_dot_nt(q, k_ref[0, j]), diag)
            hi, lo = _split2(lom)
            w = jnp.exp(logb + (_dot(hi, tri_m) + _dot(lo, tri_m)) + run)
            if diag:
                w = jnp.where(strict, w, 0.0)
            acc = acc + _dot(w.astype(BF16), v_ref[0, j])
            rall = jnp.where(lane == j, run, rall)
            return run + jnp.sum(lom, axis=1, keepdims=True), acc, rall

        init = (jnp.zeros((BQ, 1), F32), jnp.zeros((BQ, 2 * dh), F32), jnp.full((BQ, 128), NEG, F32))
        carry = block(i, init, True)

        def cond(c):
            n, carry = c
            return jnp.logical_and(n < i, jnp.max(carry[0]) >= STOP)

        _, (_, acc, rall) = lax.while_loop(cond, lambda c: (c[0] + 1, block(i - 1 - c[0], c[1], False)), (0, carry))
        o_ref[0] = acc[:, :dh].astype(BF16)
        rs_ref[0] = rall

    qin, kin, _, kin2, qspec, _ = _head_specs(nj, H_FOX)
    rspec = pl.BlockSpec((1, BQ, 128), lambda h, i: (h, i, 0))
    return pl.pallas_call(
        body, grid=(nh, nj),
        in_specs=[qin, kin, kin2, pl.BlockSpec((BQ, BQ), lambda h, i: (0, 0))],
        out_specs=[qspec, rspec],
        out_shape=[jax.ShapeDtypeStruct((nh, t, dh), BF16), jax.ShapeDtypeStruct((nh, t, 128), F32)],
        compiler_params=_params(), name=name)(qs, kn4, va4, tri)


def sb_bwd(qs, kn4, v4, qa, doa, tri, do, rsave, *, name):
    _, t, dh = qs.shape
    nh = H_SB
    nj = t // BQ

    def body(q_ref, k_ref, v_ref, qa_ref, doa_ref, tri_ref, do_ref, rs_ref, dq_ref, dkv_ref):
        i = pl.program_id(1)

        @pl.when(i == 0)
        def _():
            dkv_ref[...] = jnp.zeros_like(dkv_ref)

        q = q_ref[0]
        do_b = do_ref[0]
        tri_m = tri_ref[...]
        rall = rs_ref[0]
        lane = lax.broadcasted_iota(jnp.int32, (BQ, 128), 1)
        rhs = jnp.concatenate([qa_ref[0], doa_ref[0]], axis=0)
        lane1 = lax.broadcasted_iota(jnp.int32, (1, 128), 1)
        unvisited = jnp.logical_and(lane1 < i, jnp.max(rall, axis=0, keepdims=True) < STOP)
        first = jnp.sum(unvisited.astype(jnp.int32))

        def block(j, carry, diag):
            dq, ecar = carry
            k = k_ref[0, j]
            z = _dot_nt(q, k)
            logb, lom, e, strict = _sb_logs(z, diag)
            hi, lo = _split2(lom)
            run = jnp.sum(jnp.where(lane == j, rall, 0.0), axis=1, keepdims=True)
            w = jnp.exp(logb + (_dot(hi, tri_m) + _dot(lo, tri_m)) + run)
            if diag:
                w = jnp.where(strict, w, 0.0)
            da = w * _dot_nt(do_b, v_ref[0, j])
            before = _dot_nt(da.astype(BF16), tri_m) + ecar
            inv = 1.0 / (1.0 + e)
            beta = jnp.where(z >= 0.0, 1.0, e) * inv
            one_minus = jnp.where(z >= 0.0, e, 1.0) * inv
            dz = da * one_minus - before * beta
            if diag:
                dz = jnp.where(strict, dz, 0.0)
            dz_b = dz.astype(BF16)
            dkv_ref[0, j] += _dot_tn(jnp.concatenate([dz_b, w.astype(BF16)], axis=0), rhs)
            return dq + _dot(dz_b, k), ecar + jnp.sum(da, axis=1, keepdims=True)

        carry = lax.fori_loop(first, i, lambda j, c: block(j, c, False),
                              (jnp.zeros((BQ, dh), F32), jnp.zeros((BQ, 1), F32)))
        dq, _ = block(i, carry, True)
        dq_ref[0] = dq

    qin, kin, qin2, _, qspec, kspec2 = _head_specs(nj, H_FOX)
    return pl.pallas_call(
        body, grid=(nh, nj),
        in_specs=[qin, kin, kin, qin2, qin2, pl.BlockSpec((BQ, BQ), lambda h, i: (0, 0)), qin,
                  pl.BlockSpec((1, BQ, 128), lambda h, i: (h, i, 0))],
        out_specs=[qspec, kspec2],
        out_shape=[jax.ShapeDtypeStruct((nh, t, dh), F32), jax.ShapeDtypeStruct((nh, nj, BQ, 2 * dh), F32)],
        compiler_params=_params(), name=name)(qs, kn4, v4, qa, doa, tri, do, rsave)


def loss_head(y, target, *, name, tm=512):
    t, d = y.shape

    def body(y_ref, t_ref, l_ref, dy_ref, acc_ref):
        i = pl.program_id(0)
        diff = y_ref[...] - t_ref[...]
        dy_ref[...] = diff * (1.0 / d)
        part = jnp.sum(diff * diff, axis=0, keepdims=True)

        @pl.when(i == 0)
        def _():
            acc_ref[...] = part

        @pl.when(i > 0)
        def _():
            acc_ref[...] += part

        @pl.when(i == pl.num_programs(0) - 1)
        def _():
            l_ref[...] = jnp.full(l_ref.shape, (0.5 / d) * jnp.sum(acc_ref[...]), F32)

    row = pl.BlockSpec((tm, d), lambda i: (i, 0))
    return pl.pallas_call(
        body, grid=(t // tm,), in_specs=[row, row],
        out_specs=[pl.BlockSpec((8, 128), lambda i: (0, 0)), row],
        out_shape=[jax.ShapeDtypeStruct((8, 128), F32), jax.ShapeDtypeStruct((t, d), F32)],
        scratch_shapes=[pltpu.VMEM((1, d), F32)], compiler_params=_params(), name=name)(y, target)


def _to_heads(a):
    t = a.shape[0]
    return a.reshape(t, N_HEADS, HEAD_DIM).transpose(1, 0, 2)


def _from_heads(a):
    t = a.shape[1]
    return a.transpose(1, 0, 2).reshape(t, MIX)


def _lanes_to_chunks(a):
    r, t = a.shape
    return a.reshape(r, t // 128, 128).transpose(1, 0, 2)


def _chunks_to_lanes(a):
    nc, r, _ = a.shape
    return a.transpose(1, 0, 2).reshape(r, nc * 128)


def _constants():
    idx = jnp.arange(128)
    bd = (idx[:, None] // HEAD_DIM == idx[None, :] // HEAD_DIM).astype(BF16)
    tri_le = (idx[:, None] <= idx[None, :]).astype(BF16)
    tri_ge = (idx[:, None] >= idx[None, :]).astype(BF16)
    jdx = jnp.arange(BQ)
    tri_gt = (jdx[:, None] > jdx[None, :]).astype(BF16)
    return dict(bd=bd, tri_le=tri_le, tri_ge=tri_ge, tri_gt=tri_gt)


def attn_layer_fwd(h, w, cst):
    t = h.shape[0]
    nj = t // BQ
    xn = rms_fwd(h, w["norm"], name="rms_fwd")
    proj = mm_nn(xn, w["w_in"], tn=640, name="attn_in_proj")
    qs, kn, vb, logf = attn_prep_fwd(proj, w["gq"], w["gk"], w["fbias"], cst["bd"], name="attn_prep_fwd")
    logf3 = _lanes_to_chunks(logf)
    cum = _chunks_to_lanes(gate_cumsum(logf3, cst["tri_le"], name="gate_cumsum"))
    fcol = cum.reshape(H_FOX, t, 1)
    frow4 = cum.reshape(H_FOX, nj, 1, BQ)
    qh = _to_heads(qs)
    kh4 = _to_heads(kn).reshape(N_HEADS, nj, BQ, HEAD_DIM)
    vh4 = _to_heads(vb).reshape(N_HEADS, nj, BQ, HEAD_DIM)
    ones = jnp.ones(vh4.shape[:-1] + (1,), BF16)
    va4 = jnp.concatenate([vh4, ones, jnp.zeros(vh4.shape[:-1] + (HEAD_DIM - 1,), BF16)], axis=-1)
    o_f, lse = fox_fwd(qh, kh4, va4, fcol, frow4, name="fox_fwd")
    o_s, rsave = sb_fwd(qh, kh4, va4, cst["tri_gt"], name="sb_fwd")
    o = _from_heads(jnp.concatenate([o_f.astype(BF16), o_s], axis=0))
    h2 = mm_nn(o, w["w_out"], add=h, name="mix_out_proj")
    saved = dict(h=h, xn=xn, proj=proj, logf3=logf3, fcol=fcol, frow4=frow4, qh=qh, kh4=kh4, vh4=vh4,
                 o_f=o_f, lse=lse, rsave=rsave, o=o)
    return h2, saved


def attn_layer_bwd(dh, w, s, cst):
    t = dh.shape[0]
    dh3 = dh[None]
    do = mm_nt(dh3, w["w_out"], out_dtype=BF16, name="mix_out_bwd_bf16")
    g_w_out = mm_tn(s["o"], dh3, name="mix_out_wgrad")
    doh = _to_heads(do)
    zeros = jnp.zeros_like(doh)
    qa = jnp.concatenate([s["qh"], zeros], axis=-1)
    doa = jnp.concatenate([zeros, doh], axis=-1)
    dq_f, dkv_f, dfk = fox_bwd(s["qh"], s["kh4"], s["vh4"], qa, doa, s["fcol"], s["frow4"], s["o_f"], doh, s["lse"],
                               name="fox_bwd")
    dq_s, dkv_s = sb_bwd(s["qh"], s["kh4"], s["vh4"], qa, doa, cst["tri_gt"], doh, s["rsave"], name="sb_bwd")
    dqs = _from_heads(jnp.concatenate([dq_f, dq_s], axis=0))
    dkv = jnp.concatenate([dkv_f, dkv_s], axis=0).reshape(N_HEADS, t, 2 * HEAD_DIM)
    dkn = _from_heads(dkv[:, :, :HEAD_DIM])
    dv = _from_heads(dkv[:, :, HEAD_DIM:])
    dcum3 = _lanes_to_chunks(dfk.reshape(H_FOX, t))
    dfl3, dbias = gate_cumsum_bwd(dcum3, s["logf3"], cst["tri_ge"], name="gate_cumsum_bwd")
    dfl = jnp.pad(_chunks_to_lanes(dfl3).T, ((0, 0), (0, 128 - H_FOX))).astype(BF16)
    dproj, dgq, dgk = attn_prep_bwd(s["proj"], dqs, dkn, dv, dfl, w["gq"], w["gk"], cst["bd"], name="attn_prep_bwd")
    g_w_in = mm_tn(s["xn"], dproj[None], tn=640, name="attn_in_wgrad")[:, :ATTN_IN]
    dxn = mm_nt(dproj[None], w["w_in"], name="attn_in_bwd")
    dh2, g_norm = rms_bwd(dxn, s["h"], w["norm"], dh, name="rms_bwd")
    dgq = dgq.reshape(N_HEADS, HEAD_DIM)
    dgk = dgk.reshape(N_HEADS, HEAD_DIM)
    grads = dict(norm=g_norm[0], w_in=g_w_in, f_bias=dbias[:, 0], fox_q=dgq[:H_FOX].sum(0), fox_k=dgk[:H_FOX].sum(0),
                 sb_q=dgq[H_FOX:].sum(0), sb_k=dgk[H_FOX:].sum(0), w_out=g_w_out)
    return dh2, grads


def conv_layer_fwd(h, w):
    xn = rms_fwd(h, w["norm"], name="rms_fwd")
    proj3 = mm_nn(xn, w["w_in"], parts=3, name="conv_in_proj")
    y = conv_mix_fwd(proj3, w["ck"], name="conv_mix_fwd")
    h2 = mm_nn(y, w["w_out"], add=h, name="mix_out_proj")
    return h2, dict(h=h, xn=xn, proj3=proj3, y=y)


def conv_layer_bwd(dh, w, s):
    dh3 = dh[None]
    dy = mm_nt(dh3, w["w_out"], name="mix_out_bwd")
    g_w_out = mm_tn(s["y"], dh3, name="mix_out_wgrad")
    dproj3, dck = conv_mix_bwd(dy, s["proj3"], w["ck"], name="conv_mix_bwd")
    g_w_in = mm_tn(s["xn"], dproj3, name="conv_in_wgrad")
    dxn = mm_nt(dproj3, w["w_in"], name="conv_in_bwd")
    dh2, g_norm = rms_bwd(dxn, s["h"], w["norm"], dh, name="rms_bwd")
    return dh2, dict(norm=g_norm[0], w_in=g_w_in, ck=dck[0, :3], w_out=g_w_out)


def ffn_layer_fwd(h, w):
    xn = rms_fwd(h, w["norm"], name="rms_fwd")
    up2 = mm_nn(xn, w["w_up"], parts=2, tn=1408, name="ffn_up_proj")
    act = ffn_act_fwd(up2, w["cw2"], name="ffn_act_fwd")
    h2 = mm_nn(act, w["w_down"], add=h, name="ffn_down_proj")
    return h2, dict(h=h, xn=xn, up2=up2, act=act)


def ffn_layer_bwd(dh, w, s):
    dh3 = dh[None]
    dact = mm_nt(dh3, w["w_down"], tn=1408, name="ffn_down_bwd")
    g_w_down = mm_tn(s["act"], dh3, tk=1408, name="ffn_down_wgrad")
    dup2, dcw = ffn_act_bwd(dact, s["up2"], w["cw2"], name="ffn_act_bwd")
    g_w_up = mm_tn(s["xn"], dup2, tn=1408, name="ffn_up_wgrad")
    dxn = mm_nt(dup2, w["w_up"], name="ffn_up_bwd")
    dh2, g_norm = rms_bwd(dxn, s["h"], w["norm"], dh, name="rms_bwd")
    g_cw = jnp.concatenate([dcw[0, :3], dcw[1, :3]], axis=1)
    return dh2, dict(norm=g_norm[0], w_up=g_w_up, cw=g_cw, w_down=g_w_down)


def forward_backward(x, target, wa, wc, wf):
    cst = _constants()
    depth = len(wf)
    h = x
    saved = []
    for layer in range(depth):
        i = layer // 2
        if layer % 2 == 0:
            h, sm = attn_layer_fwd(h, wa[i], cst)
        else:
            h, sm = conv_layer_fwd(h, wc[i])
        h, sf = ffn_layer_fwd(h, wf[layer])
        saved.append((sm, sf))
    loss_blk, dh = loss_head(h, target, name="loss_head")
    ga, gc, gf = [None] * len(wa), [None] * len(wc), [None] * depth
    for layer in reversed(range(depth)):
        i = layer // 2
        sm, sf = saved[layer]
        dh, gf[layer] = ffn_layer_bwd(dh, wf[layer], sf)
        if layer % 2 == 0:
            dh, ga[i] = attn_layer_bwd(dh, wa[i], sm, cst)
        else:
            dh, gc[i] = conv_layer_bwd(dh, wc[i], sm)
    return loss_blk, dh, ga, gc, gf


def _part_rows(shape, width, row_mult):
    n = 1
    for s in shape:
        n *= s
    rows = -(-n // width)
    return -(-rows // row_mult) * row_mult


def _pack_rows(arrs, width, row_mult, dtype, total_rows=None):
    parts = []
    used = 0
    for a in arrs:
        rows = _part_rows(a.shape, width, row_mult)
        flat = a.astype(dtype).reshape(-1)
        flat = jnp.pad(flat, (0, rows * width - flat.shape[0]))
        parts.append(flat.reshape(rows, width))
        used += rows
    if total_rows is not None and total_rows > used:
        parts.append(jnp.zeros((total_rows - used, width), dtype))
    return jnp.concatenate(parts, axis=0)


def _unpack_rows(packed, shapes, width, row_mult):
    out = []
    off = 0
    for shape in shapes:
        rows = _part_rows(shape, width, row_mult)
        n = 1
        for s in shape:
            n *= s
        out.append(packed[off:off + rows].reshape(-1)[:n].reshape(shape))
        off += rows
    return out


BIG_NAMES = ("attn_w_in", "attn_w_out", "conv_w_in", "conv_w_out", "ffn_w_up", "ffn_w_down")
BIG_AXIS = {"attn_w_in": 2, "attn_w_out": 1, "conv_w_in": 2, "conv_w_out": 1, "ffn_w_up": 2, "ffn_w_down": 1}
BIG_WIDTH = 1024
BIG_ROW_MULT = 16
BIG_TILE = 512
SMALL_SHARDED = ("conv_norm", "conv_kernel", "ffn_conv")
SMALL_AXIS = {"conv_norm": 1, "conv_kernel": 2, "ffn_conv": 2}
SMALL_REPLICATED = ("attn_norm", "attn_f_bias", "fox_q_gain", "fox_k_gain", "sb_q_gain", "sb_k_gain", "ffn_norm")
WEIGHT_ORDER = ("attn_norm", "attn_w_in", "attn_f_bias", "fox_q_gain", "fox_k_gain", "sb_q_gain", "sb_k_gain",
                "attn_w_out", "conv_norm", "conv_w_in", "conv_kernel", "conv_w_out", "ffn_norm", "ffn_w_up",
                "ffn_conv", "ffn_w_down")


def _big_total_rows(shapes):
    used = sum(_part_rows(s, BIG_WIDTH, BIG_ROW_MULT) for s in shapes)
    return -(-used // BIG_TILE) * BIG_TILE


def _place():
    x, y, c = lax.axis_index("x"), lax.axis_index("y"), lax.axis_index("c")
    other_chips = [(1 - x, y), (x, 1 - y), (1 - x, 1 - y)]
    return x, y, c, other_chips


_ANY = pl.BlockSpec(memory_space=pl.ANY)


def gather_chips(big, small, *, name):
    def body(big_ref, small_ref, obig_ref, osmall_ref, send_sems, recv_sems, local_sems):
        x, y, c, chips = _place()
        k = 2 * x + y
        pairs = [(big_ref, obig_ref), (small_ref, osmall_ref)]

        def copy(j, n, slot):
            px, py = chips[j]
            src, dst = pairs[n]
            return pltpu.make_async_remote_copy(src_ref=src, dst_ref=dst.at[slot], send_sem=send_sems.at[2 * j + n],
                                                recv_sem=recv_sems.at[2 * j + n], device_id=(px, py, c),
                                                device_id_type=MESH)

        local = [pltpu.make_async_copy(src, dst.at[k], local_sems.at[n]) for n, (src, dst) in enumerate(pairs)]
        for cp in local:
            cp.start()
        sends = [copy(j, n, k) for j in range(3) for n in range(2)]
        for cp in sends:
            cp.start()
        for j, (px, py) in enumerate(chips):
            for n in range(2):
                copy(j, n, 2 * px + py).wait_recv()
        for cp in sends:
            cp.wait_send()
        for cp in local:
            cp.wait()

    return pl.pallas_call(
        body, in_specs=[_ANY, _ANY], out_specs=[_ANY, _ANY],
        out_shape=[jax.ShapeDtypeStruct((N_CHIPS,) + big.shape, big.dtype),
                   jax.ShapeDtypeStruct((N_CHIPS,) + small.shape, small.dtype)],
        scratch_shapes=[pltpu.SemaphoreType.DMA((6,)), pltpu.SemaphoreType.DMA((6,)), pltpu.SemaphoreType.DMA((2,))],
        name=name)(big, small)


def scatter_chips(chunks, *, name):
    def body(g_ref, o_ref, send_sems, recv_sems, local_sem):
        x, y, c, chips = _place()
        k = 2 * x + y

        def copy(j, src_slot, dst_slot):
            px, py = chips[j]
            return pltpu.make_async_remote_copy(src_ref=g_ref.at[src_slot], dst_ref=o_ref.at[dst_slot],
                                                send_sem=send_sems.at[j], recv_sem=recv_sems.at[j],
                                                device_id=(px, py, c), device_id_type=MESH)

        local = pltpu.make_async_copy(g_ref.at[k], o_ref.at[k], local_sem)
        local.start()
        sends = [copy(j, 2 * px + py, k) for j, (px, py) in enumerate(chips)]
        for cp in sends:
            cp.start()
        for j, (px, py) in enumerate(chips):
            copy(j, k, 2 * px + py).wait_recv()
        for cp in sends:
            cp.wait_send()
        local.wait()

    return pl.pallas_call(
        body, in_specs=[_ANY], out_specs=_ANY, out_shape=jax.ShapeDtypeStruct(chunks.shape, chunks.dtype),
        scratch_shapes=[pltpu.SemaphoreType.DMA((3,)), pltpu.SemaphoreType.DMA((3,)), pltpu.SemaphoreType.DMA],
        name=name)(chunks)


def swap_cores(a, *, name):
    def body(a_ref, o_ref, send_sem, recv_sem):
        x, y, c, _ = _place()
        cp = pltpu.make_async_remote_copy(src_ref=a_ref, dst_ref=o_ref, send_sem=send_sem, recv_sem=recv_sem,
                                          device_id=(x, y, 1 - c), device_id_type=MESH)
        cp.start()
        cp.wait()

    return pl.pallas_call(
        body, in_specs=[_ANY], out_specs=_ANY, out_shape=jax.ShapeDtypeStruct(a.shape, a.dtype),
        scratch_shapes=[pltpu.SemaphoreType.DMA, pltpu.SemaphoreType.DMA], name=name)(a)


def allreduce_small(p, *, name):
    r, w = p.shape

    def body(p_ref, o_ref, buf, send_sems, recv_sems):
        x, y, c, _ = _place()
        me = 4 * x + 2 * y + c
        buf[me] = p_ref[...]

        def peer_of(m):
            return (1 - x if m & 4 else x, 1 - y if m & 2 else y, 1 - c if m & 1 else c)

        def copy(m, slot):
            return pltpu.make_async_remote_copy(src_ref=p_ref, dst_ref=buf.at[slot], send_sem=send_sems.at[m - 1],
                                                recv_sem=recv_sems.at[m - 1], device_id=peer_of(m),
                                                device_id_type=MESH)

        sends = [copy(m, me) for m in range(1, 8)]
        for cp in sends:
            cp.start()
        for m in range(1, 8):
            px, py, pc = peer_of(m)
            copy(m, 4 * px + 2 * py + pc).wait_recv()
        for cp in sends:
            cp.wait_send()
        acc = buf[0]
        for d in range(1, 8):
            acc = acc + buf[d]
        o_ref[...] = acc

    vm = pl.BlockSpec(memory_space=pltpu.VMEM)
    return pl.pallas_call(
        body, in_specs=[vm], out_specs=vm, out_shape=jax.ShapeDtypeStruct((r, w), F32),
        scratch_shapes=[pltpu.VMEM((8, r, w), F32), pltpu.SemaphoreType.DMA((7,)), pltpu.SemaphoreType.DMA((7,))],
        name=name)(p)


def sum_chips(rv, *, name):
    _, r, w = rv.shape

    def body(a_ref, b_ref, c_ref, d_ref, o_ref):
        o_ref[...] = ((a_ref[0].astype(F32) + b_ref[0].astype(F32)) + c_ref[0].astype(F32)) + d_ref[0].astype(F32)

    spec = lambda kk: pl.BlockSpec((1, BIG_TILE, w), lambda i: (kk, i, 0))
    return pl.pallas_call(
        body, grid=(r // BIG_TILE,), in_specs=[spec(0), spec(1), spec(2), spec(3)],
        out_specs=pl.BlockSpec((BIG_TILE, w), lambda i: (i, 0)), out_shape=jax.ShapeDtypeStruct((r, w), F32),
        compiler_params=_params(), name=name)(rv, rv, rv, rv)


def add_pair(a, b, *, name):
    r, w = a.shape

    def body(a_ref, b_ref, o_ref):
        o_ref[...] = a_ref[...] + b_ref[...]

    spec = pl.BlockSpec((BIG_TILE, w), lambda i: (i, 0))
    return pl.pallas_call(body, grid=(r // BIG_TILE,), in_specs=[spec, spec], out_specs=spec,
                          out_shape=jax.ShapeDtypeStruct((r, w), F32), compiler_params=_params(), name=name)(a, b)


def adamw(w, g, m, v, *, tm, name):
    r, c = w.shape
    assert r % tm == 0

    def body(w_ref, g_ref, m_ref, v_ref, d_ref, nm_ref, nv_ref):
        g_ = g_ref[...]
        m_ = ADAM_B1 * m_ref[...] + (1.0 - ADAM_B1) * g_
        v_ = ADAM_B2 * v_ref[...] + (1.0 - ADAM_B2) * (g_ * g_)
        m_hat = m_ / (1.0 - ADAM_B1 ** ADAM_STEP)
        v_hat = v_ / (1.0 - ADAM_B2 ** ADAM_STEP)
        d_ref[...] = -ADAM_LR * (m_hat / (jnp.sqrt(v_hat) + ADAM_EPS) + ADAM_WD * w_ref[...])
        nm_ref[...] = m_
        nv_ref[...] = v_

    spec = pl.BlockSpec((tm, c), lambda i: (i, 0))
    return pl.pallas_call(body, grid=(r // tm,), in_specs=[spec] * 4, out_specs=[spec] * 3,
                          out_shape=[jax.ShapeDtypeStruct((r, c), F32)] * 3, compiler_params=_params(), name=name)(w, g, m, v)


def kernel(x, attn_norm, attn_w_in, attn_f_bias, fox_q_gain, fox_k_gain, sb_q_gain, sb_k_gain, attn_w_out, conv_norm, conv_w_in, conv_kernel, conv_w_out, ffn_norm, ffn_w_up, ffn_conv, ffn_w_down, loss_target, m_attn_norm, m_attn_w_in, m_attn_f_bias, m_fox_q_gain, m_fox_k_gain, m_sb_q_gain, m_sb_k_gain, m_attn_w_out, m_conv_norm, m_conv_w_in, m_conv_kernel, m_conv_w_out, m_ffn_norm, m_ffn_w_up, m_ffn_conv, m_ffn_w_down, v_attn_norm, v_attn_w_in, v_attn_f_bias, v_fox_q_gain, v_fox_k_gain, v_sb_q_gain, v_sb_k_gain, v_attn_w_out, v_conv_norm, v_conv_w_in, v_conv_kernel, v_conv_w_out, v_ffn_norm, v_ffn_w_up, v_ffn_conv, v_ffn_w_down):
    a = dict(locals())
    chip = 2 * lax.axis_index("x") + lax.axis_index("y")
    n_attn, n_conv, depth = attn_norm.shape[0], conv_norm.shape[0], ffn_norm.shape[0]

    big_shapes = [a[n].shape for n in BIG_NAMES]
    big_rows = _big_total_rows(big_shapes)
    small_shapes = [a[n].shape for n in SMALL_SHARDED]
    packed_w = _pack_rows([a[n] for n in BIG_NAMES], BIG_WIDTH, BIG_ROW_MULT, BF16, big_rows)
    packed_s = _pack_rows([a[n] for n in SMALL_SHARDED], 128, 8, F32)
    gath_w, gath_s = gather_chips(packed_w, packed_s, name="gather_weights")
    full = {}
    per_chip = [_unpack_rows(gath_w[kk], big_shapes, BIG_WIDTH, BIG_ROW_MULT) for kk in range(N_CHIPS)]
    for n, name in enumerate(BIG_NAMES):
        full[name] = jnp.concatenate([per_chip[kk][n] for kk in range(N_CHIPS)], axis=BIG_AXIS[name])
    per_chip = [_unpack_rows(gath_s[kk], small_shapes, 128, 8) for kk in range(N_CHIPS)]
    for n, name in enumerate(SMALL_SHARDED):
        full[name] = jnp.concatenate([per_chip[kk][n] for kk in range(N_CHIPS)], axis=SMALL_AXIS[name])

    wa, wc, wf = [], [], []
    for i in range(n_attn):
        wa.append(dict(
            norm=attn_norm[i][None],
            w_in=jnp.pad(full["attn_w_in"][i], ((0, 0), (0, ATTN_IN_PAD - ATTN_IN))),
            fbias=jnp.pad(attn_f_bias[i], (0, 128 - H_FOX))[None],
            gq=jnp.concatenate([jnp.tile(fox_q_gain[i], H_FOX), jnp.tile(sb_q_gain[i], H_SB)])[None],
            gk=jnp.concatenate([jnp.tile(fox_k_gain[i], H_FOX), jnp.tile(sb_k_gain[i], H_SB)])[None],
            w_out=full["attn_w_out"][i]))
    for i in range(n_conv):
        wc.append(dict(norm=full["conv_norm"][i][None], w_in=full["conv_w_in"][i], ck=full["conv_kernel"][i][None],
                       w_out=full["conv_w_out"][i]))
    for l in range(depth):
        cw = full["ffn_conv"][l]
        wf.append(dict(norm=ffn_norm[l][None], w_up=full["ffn_w_up"][l], cw2=jnp.stack([cw[:, :D_FF], cw[:, D_FF:]]),
                       w_down=full["ffn_w_down"][l]))
    loss_blk, grad_x, ga, gc, gf = forward_backward(x[0], loss_target[0], wa, wc, wf)

    g_full = {
        "attn_w_in": jnp.stack([g["w_in"] for g in ga]), "attn_w_out": jnp.stack([g["w_out"] for g in ga]),
        "conv_w_in": jnp.stack([g["w_in"] for g in gc]), "conv_w_out": jnp.stack([g["w_out"] for g in gc]),
        "ffn_w_up": jnp.stack([g["w_up"] for g in gf]), "ffn_w_down": jnp.stack([g["w_down"] for g in gf]),
    }
    chunks = []
    for kk in range(N_CHIPS):
        parts = []
        for name in BIG_NAMES:
            width = a[name].shape[BIG_AXIS[name]]
            parts.append(lax.slice_in_dim(g_full[name], kk * width, (kk + 1) * width, axis=BIG_AXIS[name]))
        chunks.append(_pack_rows(parts, BIG_WIDTH, BIG_ROW_MULT, BF16, big_rows))
    landed = scatter_chips(jnp.stack(chunks), name="scatter_grads")
    mine = sum_chips(landed, name="sum_chips")
    theirs = swap_cores(mine, name="swap_cores")
    g_big = _unpack_rows(add_pair(mine, theirs, name="add_cores"), big_shapes, BIG_WIDTH, BIG_ROW_MULT)
    grads = dict(zip(BIG_NAMES, g_big))

    small_full = [
        loss_blk,
        jnp.stack([g["norm"] for g in ga]), jnp.stack([g["f_bias"] for g in ga]),
        jnp.stack([g["fox_q"] for g in ga]), jnp.stack([g["fox_k"] for g in ga]),
        jnp.stack([g["sb_q"] for g in ga]), jnp.stack([g["sb_k"] for g in ga]),
        jnp.stack([g["norm"] for g in gf]),
        jnp.stack([g["norm"] for g in gc]), jnp.stack([g["ck"] for g in gc]), jnp.stack([g["cw"] for g in gf]),
    ]
    summed = allreduce_small(_pack_rows(small_full, 128, 8, F32), name="allreduce_small")
    parts = _unpack_rows(summed, [p.shape for p in small_full], 128, 8)
    loss = parts[0][0, 0]
    for name, g in zip(SMALL_REPLICATED, parts[1:8]):
        grads[name] = g
    for name, g in zip(SMALL_SHARDED, parts[8:]):
        width = a[name].shape[SMALL_AXIS[name]]
        grads[name] = lax.dynamic_slice_in_dim(g, chip * width, width, axis=SMALL_AXIS[name])

    delta, new_m, new_v = {}, {}, {}
    for name in BIG_NAMES:
        shape = a[name].shape
        flat = lambda arr: arr.reshape(-1, shape[-1])
        d_, m_, v_ = adamw(flat(a[name]), flat(grads[name]), flat(a["m_" + name]), flat(a["v_" + name]), tm=256,
                           name="adamw")
        delta[name], new_m[name], new_v[name] = d_.reshape(shape), m_.reshape(shape), v_.reshape(shape)
    small_names = SMALL_REPLICATED + SMALL_SHARDED
    small_shapes_local = [a[n].shape for n in small_names]
    pack = lambda prefix, src: _pack_rows([src[prefix + n] for n in small_names], 128, 8, F32)
    packed = adamw(pack("", a), pack("", grads), pack("m_", a), pack("v_", a), tm=8, name="adamw_small")
    for store, buf in zip((delta, new_m, new_v), packed):
        for name, arr in zip(small_names, _unpack_rows(buf, small_shapes_local, 128, 8)):
            store[name] = arr

    return (loss, grad_x[None], *[grads[n] for n in WEIGHT_ORDER], *[delta[n] for n in WEIGHT_ORDER],
            *[new_m[n] for n in WEIGHT_ORDER], *[new_v[n] for n in WEIGHT_ORDER])
```

```python
import functools

import jax
import jax.numpy as jnp
from jax import lax
from jax.experimental import pallas as pl
from jax.experimental.pallas import tpu as pltpu

F32 = jnp.float32
BF16 = jnp.bfloat16

D_MODEL = 1024
HEAD_DIM = 64
H_FOX = 8
H_SB = 8
N_HEADS = H_FOX + H_SB
MIX = N_HEADS * HEAD_DIM
ATTN_IN = 3 * MIX + H_FOX
ATTN_IN_PAD = 3 * MIX + 128
D_FF = 2816
EPS = 1e-6
SCALE = HEAD_DIM ** -0.5
NEG = -1e30

ADAM_LR = 0.001
ADAM_B1 = 0.9
ADAM_B2 = 0.999
ADAM_EPS = 1e-08
ADAM_WD = 0.01
ADAM_STEP = 10

VMEM_LIMIT = 56 * 1024 * 1024
HALO = 8
BQ = 512
N_CHIPS = 4
MESH = pl.DeviceIdType.MESH


def _params(**kw):
    return pltpu.CompilerParams(vmem_limit_bytes=VMEM_LIMIT, **kw)


def _dot(a, b):
    return jnp.dot(a, b, preferred_element_type=F32)


def _dot_nt(a, b):
    return lax.dot_general(a, b, (((1,), (1,)), ((), ())), preferred_element_type=F32)


def _dot_tn(a, b):
    return lax.dot_general(a, b, (((0,), (0,)), ((), ())), preferred_element_type=F32)


def _split2(x):
    hi = x.astype(BF16)
    lo = (x - hi.astype(F32)).astype(BF16)
    return hi, lo


def _split3(x):
    hi = x.astype(BF16)
    r = x - hi.astype(F32)
    mid = r.astype(BF16)
    lo = (r - mid.astype(F32)).astype(BF16)
    return hi, mid, lo


def mm_nn(a, b, *, add=None, out_dtype=F32, parts=1, tm=512, tn=512, name):
    m, k = a.shape
    n = b.shape[1]
    np_ = n // parts
    nb = np_ // tn
    assert m % tm == 0 and np_ % tn == 0

    def body(*refs):
        if add is None:
            a_ref, b_ref, o_ref = refs
            acc = _dot(a_ref[...].astype(BF16), b_ref[...])
        else:
            a_ref, b_ref, r_ref, o_ref = refs
            acc = _dot(a_ref[...].astype(BF16), b_ref[...]) + r_ref[...]
        o_ref[...] = acc.astype(out_dtype).reshape(o_ref.shape)

    in_specs = [pl.BlockSpec((tm, k), lambda i, j: (i, 0)), pl.BlockSpec((k, tn), lambda i, j: (0, j))]
    args = [a, b]
    if add is not None:
        in_specs.append(pl.BlockSpec((tm, tn), lambda i, j: (i, j)))
        args.append(add)
    if parts == 1:
        out_spec = pl.BlockSpec((tm, tn), lambda i, j: (i, j))
        out_shape = jax.ShapeDtypeStruct((m, n), out_dtype)
    else:
        out_spec = pl.BlockSpec((1, tm, tn), lambda i, j: (j // nb, i, j % nb))
        out_shape = jax.ShapeDtypeStruct((parts, m, np_), out_dtype)
    return pl.pallas_call(body, grid=(m // tm, n // tn), in_specs=in_specs, out_specs=out_spec,
                          out_shape=out_shape, compiler_params=_params(), name=name)(*args)


def mm_nt(a3, b, *, out_dtype=F32, tm=512, tn=512, name):
    p, m, kp = a3.shape
    n = b.shape[0]
    assert m % tm == 0 and n % tn == 0 and b.shape[1] == p * kp

    def body(a_ref, b_ref, o_ref, acc_ref):
        part = pl.program_id(2)
        prod = _dot_nt(a_ref[0].astype(BF16), b_ref[...])

        @pl.when(part == 0)
        def _():
            acc_ref[...] = prod

        @pl.when(part > 0)
        def _():
            acc_ref[...] += prod

        @pl.when(part == p - 1)
        def _():
            o_ref[...] = acc_ref[...].astype(out_dtype)

    return pl.pallas_call(
        body, grid=(m // tm, n // tn, p),
        in_specs=[pl.BlockSpec((1, tm, kp), lambda i, j, q: (q, i, 0)), pl.BlockSpec((tn, kp), lambda i, j, q: (j, q))],
        out_specs=pl.BlockSpec((tm, tn), lambda i, j, q: (i, j)),
        out_shape=jax.ShapeDtypeStruct((m, n), out_dtype),
        scratch_shapes=[pltpu.VMEM((tm, tn), F32)],
        compiler_params=_params(), name=name)(a3, b)


def mm_tn(a, b3, *, tk=512, tn=512, tt=512, name):
    t, k = a.shape
    p, _, np_ = b3.shape
    nb = np_ // tn
    assert t % tt == 0 and k % tk == 0 and np_ % tn == 0

    def body(a_ref, b_ref, o_ref):
        prod = _dot_tn(a_ref[...].astype(BF16), b_ref[0].astype(BF16))

        @pl.when(pl.program_id(2) == 0)
        def _():
            o_ref[...] = prod

        @pl.when(pl.program_id(2) > 0)
        def _():
            o_ref[...] += prod

    return pl.pallas_call(
        body, grid=(k // tk, p * nb, t // tt),
        in_specs=[pl.BlockSpec((tt, tk), lambda i, j, s: (s, i)), pl.BlockSpec((1, tt, tn), lambda i, j, s: (j // nb, s, j % nb))],
        out_specs=pl.BlockSpec((tk, tn), lambda i, j, s: (i, j)),
        out_shape=jax.ShapeDtypeStruct((k, p * np_), F32),
        compiler_params=_params(), name=name)(a, b3)


def rms_fwd(h, g, *, name, tm=512):
    t, d = h.shape

    def body(h_ref, g_ref, o_ref):
        x = h_ref[...]
        r = lax.rsqrt(jnp.mean(x * x, axis=-1, keepdims=True) + EPS)
        o_ref[...] = (x * r * g_ref[...]).astype(BF16)

    return pl.pallas_call(
        body, grid=(t // tm,),
        in_specs=[pl.BlockSpec((tm, d), lambda i: (i, 0)), pl.BlockSpec((1, d), lambda i: (0, 0))],
        out_specs=pl.BlockSpec((tm, d), lambda i: (i, 0)),
        out_shape=jax.ShapeDtypeStruct((t, d), BF16), compiler_params=_params(), name=name)(h, g)


def rms_bwd(dxn, h, g, dres, *, name, tm=512):
    t, d = h.shape

    def body(dxn_ref, h_ref, g_ref, dres_ref, dh_ref, dg_ref):
        x = h_ref[...]
        dy = dxn_ref[...]
        r = lax.rsqrt(jnp.mean(x * x, axis=-1, keepdims=True) + EPS)
        gy = dy * g_ref[...]
        dot = jnp.mean(gy * x, axis=-1, keepdims=True)
        dh_ref[...] = dres_ref[...] + r * gy - x * (r * r * r * dot)
        part = jnp.sum(dy * x * r, axis=0, keepdims=True)

        @pl.when(pl.program_id(0) == 0)
        def _():
            dg_ref[...] = part

        @pl.when(pl.program_id(0) > 0)
        def _():
            dg_ref[...] += part

    row = pl.BlockSpec((tm, d), lambda i: (i, 0))
    vec = pl.BlockSpec((1, d), lambda i: (0, 0))
    return pl.pallas_call(
        body, grid=(t // tm,), in_specs=[row, row, vec, row], out_specs=[row, vec],
        out_shape=[jax.ShapeDtypeStruct((t, d), F32), jax.ShapeDtypeStruct((1, d), F32)],
        compiler_params=_params(), name=name)(dxn, h, g, dres)


def _causal3(x, w):
    return w[0:1] * pltpu.roll(x, 2, 0) + w[1:2] * pltpu.roll(x, 1, 0) + w[2:3] * x


def _anticausal3(z, w):
    n = z.shape[0]
    return w[2:3] * z + w[1:2] * pltpu.roll(z, n - 1, 0) + w[0:1] * pltpu.roll(z, n - 2, 0)


def _prev_spec(part, tm, tc, nrow8):
    del nrow8
    return pl.BlockSpec((1, HALO, tc), lambda j, i: (part, jnp.maximum(i * (tm // HALO) - 1, 0), j))


def _next_spec(part, tm, tc, nrow8):
    return pl.BlockSpec((1, HALO, tc), lambda j, i: (part, jnp.minimum((i + 1) * (tm // HALO), nrow8 - 1), j))


def _tile_spec(part, tm, tc):
    return pl.BlockSpec((1, tm, tc), lambda j, i: (part, i, j))


def _shifted_rows(x_ext, tm):
    x2 = pltpu.roll(x_ext, 2, 0)[HALO:HALO + tm]
    x1 = pltpu.roll(x_ext, 1, 0)[HALO:HALO + tm]
    x0 = x_ext[HALO:HALO + tm]
    return x2, x1, x0


def _acc_rows(ref, val, first):
    @pl.when(first)
    def _():
        ref[...] = val

    @pl.when(jnp.logical_not(first))
    def _():
        ref[...] += val


def ffn_act_fwd(up2, cw2, *, name, tm=512, tc=256):
    _, t, f = up2.shape
    n8 = t // HALO

    def body(g_ref, v_ref, gp_ref, vp_ref, wg_ref, wv_ref, o_ref):
        first = pl.program_id(1) == 0
        keep = jnp.where(first, 0.0, 1.0)
        g_ext = jnp.concatenate([gp_ref[0] * keep, g_ref[0]], axis=0)
        v_ext = jnp.concatenate([vp_ref[0] * keep, v_ref[0]], axis=0)
        ug = _causal3(g_ext, wg_ref[0])[HALO:]
        uv = _causal3(v_ext, wv_ref[0])[HALO:]
        o_ref[...] = (ug * jax.nn.sigmoid(ug) * uv).astype(BF16)

    wspec = lambda part: pl.BlockSpec((1, 3, tc), lambda j, i: (part, 0, j))
    return pl.pallas_call(
        body, grid=(f // tc, t // tm),
        in_specs=[_tile_spec(0, tm, tc), _tile_spec(1, tm, tc), _prev_spec(0, tm, tc, n8), _prev_spec(1, tm, tc, n8),
                  wspec(0), wspec(1)],
        out_specs=pl.BlockSpec((tm, tc), lambda j, i: (i, j)),
        out_shape=jax.ShapeDtypeStruct((t, f), BF16), compiler_params=_params(), name=name)(up2, up2, up2, up2, cw2, cw2)


def ffn_act_bwd(dact, up2, cw2, *, name, tm=512, tc=256):
    _, t, f = up2.shape
    n8 = t // HALO

    def body(d_ref, dn_ref, g_ref, v_ref, gp_ref, vp_ref, gn_ref, vn_ref, wg_ref, wv_ref, dup_ref, dw_ref):
        i = pl.program_id(1)
        first = i == 0
        keep_p = jnp.where(first, 0.0, 1.0)
        keep_n = jnp.where(i == pl.num_programs(1) - 1, 0.0, 1.0)
        wg = wg_ref[0]
        wv = wv_ref[0]
        g_ext = jnp.concatenate([gp_ref[0] * keep_p, g_ref[0], gn_ref[0]], axis=0)
        v_ext = jnp.concatenate([vp_ref[0] * keep_p, v_ref[0], vn_ref[0]], axis=0)
        d_ext = jnp.concatenate([d_ref[...], dn_ref[...] * keep_n], axis=0)
        ug = _causal3(g_ext, wg)[HALO:]
        uv = _causal3(v_ext, wv)[HALO:]
        s = jax.nn.sigmoid(ug)
        dg = d_ext * uv * (s * (1.0 + ug * (1.0 - s)))
        dv = d_ext * (ug * s)
        dup_ref[0] = _anticausal3(dg, wg)[:tm].astype(BF16)
        dup_ref[1] = _anticausal3(dv, wv)[:tm].astype(BF16)
        g2, g1, g0 = _shifted_rows(g_ext, tm)
        v2, v1, v0 = _shifted_rows(v_ext, tm)
        dgt = dg[:tm]
        dvt = dv[:tm]
        zero = jnp.zeros((HALO - 3, tc), F32)
        rows_g = [jnp.sum(dgt * x, axis=0, keepdims=True) for x in (g2, g1, g0)] + [zero]
        rows_v = [jnp.sum(dvt * x, axis=0, keepdims=True) for x in (v2, v1, v0)] + [zero]
        _acc_rows(dw_ref, jnp.stack([jnp.concatenate(rows_g, axis=0), jnp.concatenate(rows_v, axis=0)]), first)

    wspec = lambda part: pl.BlockSpec((1, 3, tc), lambda j, i: (part, 0, j))
    return pl.pallas_call(
        body, grid=(f // tc, t // tm),
        in_specs=[pl.BlockSpec((tm, tc), lambda j, i: (i, j)),
                  pl.BlockSpec((HALO, tc), lambda j, i: (jnp.minimum((i + 1) * (tm // HALO), n8 - 1), j)),
                  _tile_spec(0, tm, tc), _tile_spec(1, tm, tc), _prev_spec(0, tm, tc, n8), _prev_spec(1, tm, tc, n8),
                  _next_spec(0, tm, tc, n8), _next_spec(1, tm, tc, n8), wspec(0), wspec(1)],
        out_specs=[pl.BlockSpec((2, tm, tc), lambda j, i: (0, i, j)), pl.BlockSpec((2, HALO, tc), lambda j, i: (0, 0, j))],
        out_shape=[jax.ShapeDtypeStruct((2, t, f), BF16), jax.ShapeDtypeStruct((2, HALO, f), F32)],
        compiler_params=_params(), name=name)(dact, dact, up2, up2, up2, up2, up2, up2, cw2, cw2)


def conv_mix_fwd(proj3, ck, *, name, tm=512, tc=256):
    _, t, c = proj3.shape
    n8 = t // HALO

    def body(b_ref, c_ref, u_ref, cp_ref, up_ref, w_ref, o_ref):
        keep = jnp.where(pl.program_id(1) == 0, 0.0, 1.0)
        cu_ext = jnp.concatenate([cp_ref[0] * up_ref[0] * keep, c_ref[0] * u_ref[0]], axis=0)
        o_ref[...] = (b_ref[0] * _causal3(cu_ext, w_ref[0])[HALO:]).astype(BF16)

    return pl.pallas_call(
        body, grid=(c // tc, t // tm),
        in_specs=[_tile_spec(0, tm, tc), _tile_spec(1, tm, tc), _tile_spec(2, tm, tc), _prev_spec(1, tm, tc, n8),
                  _prev_spec(2, tm, tc, n8), pl.BlockSpec((1, 3, tc), lambda j, i: (0, 0, j))],
        out_specs=pl.BlockSpec((tm, tc), lambda j, i: (i, j)),
        out_shape=jax.ShapeDtypeStruct((t, c), BF16), compiler_params=_params(), name=name)(proj3, proj3, proj3, proj3, proj3, ck)


def conv_mix_bwd(dy, proj3, ck, *, name, tm=512, tc=256):
    _, t, c = proj3.shape
    n8 = t // HALO

    def body(d_ref, dn_ref, b_ref, c_ref, u_ref, cp_ref, up_ref, bn_ref, w_ref, dp_ref, dw_ref):
        i = pl.program_id(1)
        first = i == 0
        keep_p = jnp.where(first, 0.0, 1.0)
        keep_n = jnp.where(i == pl.num_programs(1) - 1, 0.0, 1.0)
        w = w_ref[0]
        cu_ext = jnp.concatenate([cp_ref[0] * up_ref[0] * keep_p, c_ref[0] * u_ref[0]], axis=0)
        cv = _causal3(cu_ext, w)[HALO:]
        dyt = d_ref[...]
        d_ext = jnp.concatenate([dyt, dn_ref[...] * keep_n], axis=0)
        b_ext = jnp.concatenate([b_ref[0], bn_ref[0]], axis=0)
        dcv = d_ext * b_ext
        dcu = _anticausal3(dcv, w)[:tm]
        dp_ref[0] = (dyt * cv).astype(BF16)
        dp_ref[1] = (dcu * u_ref[0]).astype(BF16)
        dp_ref[2] = (dcu * c_ref[0]).astype(BF16)
        x2, x1, x0 = _shifted_rows(cu_ext, tm)
        dcvt = dcv[:tm]
        rows = [jnp.sum(dcvt * x, axis=0, keepdims=True) for x in (x2, x1, x0)] + [jnp.zeros((HALO - 3, tc), F32)]
        _acc_rows(dw_ref, jnp.concatenate(rows, axis=0)[None], first)

    return pl.pallas_call(
        body, grid=(c // tc, t // tm),
        in_specs=[pl.BlockSpec((tm, tc), lambda j, i: (i, j)),
                  pl.BlockSpec((HALO, tc), lambda j, i: (jnp.minimum((i + 1) * (tm // HALO), n8 - 1), j)),
                  _tile_spec(0, tm, tc), _tile_spec(1, tm, tc), _tile_spec(2, tm, tc),
                  _prev_spec(1, tm, tc, n8), _prev_spec(2, tm, tc, n8),
                  _next_spec(0, tm, tc, n8), pl.BlockSpec((1, 3, tc), lambda j, i: (0, 0, j))],
        out_specs=[pl.BlockSpec((3, tm, tc), lambda j, i: (0, i, j)), pl.BlockSpec((1, HALO, tc), lambda j, i: (0, 0, j))],
        out_shape=[jax.ShapeDtypeStruct((3, t, c), BF16), jax.ShapeDtypeStruct((1, HALO, c), F32)],
        compiler_params=_params(), name=name)(dy, dy, proj3, proj3, proj3, proj3, proj3, proj3, ck)


def _head_sums(x, bd):
    hi, lo = _split2(x)
    return _dot(hi, bd) + _dot(lo, bd)


def attn_prep_fwd(proj, gq, gk, fbias, bd, *, name, tm=256):
    t = proj.shape[0]

    def body(q_ref, k_ref, v_ref, f_ref, gq_ref, gk_ref, fb_ref, bd_ref, qs_ref, kn_ref, vb_ref, lf_ref):
        bd = bd_ref[...]

        def headnorm(x_ref, g_ref, o_ref, scale):
            for c in range(MIX // 128):
                sl = slice(128 * c, 128 * (c + 1))
                x = x_ref[:, sl]
                r = lax.rsqrt(_head_sums(x * x, bd) * (1.0 / HEAD_DIM) + EPS)
                o_ref[:, sl] = (x * r * (g_ref[:, sl] * scale)).astype(BF16)

        headnorm(q_ref, gq_ref, qs_ref, SCALE)
        headnorm(k_ref, gk_ref, kn_ref, 1.0)
        vb_ref[...] = v_ref[...].astype(BF16)
        fl = f_ref[...] + fb_ref[...]
        logf = jnp.minimum(fl, 0.0) - jnp.log(1.0 + jnp.exp(-jnp.abs(fl)))
        lf_ref[...] = logf.T[0:H_FOX, :]

    col = lambda c: pl.BlockSpec((tm, MIX), lambda i: (i, c))
    vec = pl.BlockSpec((1, MIX), lambda i: (0, 0))
    out = pl.BlockSpec((tm, MIX), lambda i: (i, 0))
    return pl.pallas_call(
        body, grid=(t // tm,),
        in_specs=[col(0), col(1), col(2), pl.BlockSpec((tm, 128), lambda i: (i, 3 * MIX // 128)), vec, vec,
                  pl.BlockSpec((1, 128), lambda i: (0, 0)), pl.BlockSpec((128, 128), lambda i: (0, 0))],
        out_specs=[out, out, out, pl.BlockSpec((H_FOX, tm), lambda i: (0, i))],
        out_shape=[jax.ShapeDtypeStruct((t, MIX), BF16)] * 3 + [jax.ShapeDtypeStruct((H_FOX, t), F32)],
        compiler_params=_params(), name=name)(proj, proj, proj, proj, gq, gk, fbias, bd)


def attn_prep_bwd(proj, dqs, dkn, dv, dfl, gq, gk, bd, *, name, tm=256):
    t = proj.shape[0]

    def body(q_ref, k_ref, dq_ref, dk_ref, dv_ref, dfl_ref, gq_ref, gk_ref, bd_ref, dp_ref, dgq_ref, dgk_ref):
        bd = bd_ref[...]
        first = pl.program_id(0) == 0

        def back(x_ref, d_ref, g_ref, col0, scale, dg_ref):
            parts = []
            for c in range(MIX // 128):
                sl = slice(128 * c, 128 * (c + 1))
                x = x_ref[:, sl]
                r = lax.rsqrt(_head_sums(x * x, bd) * (1.0 / HEAD_DIM) + EPS)
                dn = d_ref[:, sl] * scale
                gy = dn * g_ref[:, sl]
                hs = _head_sums(gy * x, bd) * (1.0 / HEAD_DIM)
                dp_ref[:, col0 + 128 * c:col0 + 128 * (c + 1)] = (r * gy - x * (r * r * r * hs)).astype(BF16)
                parts.append(jnp.sum(dn * x * r, axis=0, keepdims=True))
            _acc_rows(dg_ref, jnp.concatenate(parts, axis=1), first)

        back(q_ref, dq_ref, gq_ref, 0, SCALE, dgq_ref)
        back(k_ref, dk_ref, gk_ref, MIX, 1.0, dgk_ref)
        dp_ref[:, 2 * MIX:3 * MIX] = dv_ref[...].astype(BF16)
        dp_ref[:, 3 * MIX:] = dfl_ref[...]

    col = lambda c: pl.BlockSpec((tm, MIX), lambda i: (i, c))
    row = pl.BlockSpec((tm, MIX), lambda i: (i, 0))
    vec = pl.BlockSpec((1, MIX), lambda i: (0, 0))
    return pl.pallas_call(
        body, grid=(t // tm,),
        in_specs=[col(0), col(1), row, row, row, pl.BlockSpec((tm, 128), lambda i: (i, 0)), vec, vec,
                  pl.BlockSpec((128, 128), lambda i: (0, 0))],
        out_specs=[pl.BlockSpec((tm, ATTN_IN_PAD), lambda i: (i, 0)), vec, vec],
        out_shape=[jax.ShapeDtypeStruct((t, ATTN_IN_PAD), BF16), jax.ShapeDtypeStruct((1, MIX), F32),
                   jax.ShapeDtypeStruct((1, MIX), F32)],
        compiler_params=_params(), name=name)(proj, proj, dqs, dkn, dv, dfl, gq, gk, bd)


def gate_cumsum(logf3, tri, *, name):
    nc, r, _ = logf3.shape

    def body(x_ref, tri_ref, o_ref):
        tri_m = tri_ref[...]

        def step(c, carry):
            hi, mid, lo = _split3(x_ref[c])
            cs = _dot(hi, tri_m) + _dot(mid, tri_m) + _dot(lo, tri_m) + carry
            o_ref[c] = cs
            return cs[:, 127:128]

        lax.fori_loop(0, nc, step, jnp.zeros((r, 1), F32))

    return pl.pallas_call(body, out_shape=jax.ShapeDtypeStruct(logf3.shape, F32), compiler_params=_params(),
                          name=name)(logf3, tri)


def gate_cumsum_bwd(dcum3, logf3, tri, *, name):
    nc, r, _ = dcum3.shape

    def body(x_ref, lf_ref, tri_ref, o_ref, s_ref):
        tri_m = tri_ref[...]

        def step(n, carry):
            car, tot = carry
            c = nc - 1 - n
            hi, mid, lo = _split3(x_ref[c])
            cs = _dot(hi, tri_m) + _dot(mid, tri_m) + _dot(lo, tri_m) + car
            dl = cs * (1.0 - jnp.exp(lf_ref[c]))
            o_ref[c] = dl
            return cs[:, 0:1], tot + dl

        _, tot = lax.fori_loop(0, nc, step, (jnp.zeros((r, 1), F32), jnp.zeros((r, 128), F32)))
        s_ref[...] = jnp.broadcast_to(jnp.sum(tot, axis=1, keepdims=True), tot.shape)

    return pl.pallas_call(body, out_shape=[jax.ShapeDtypeStruct(dcum3.shape, F32), jax.ShapeDtypeStruct((r, 128), F32)],
                          compiler_params=_params(), name=name)(dcum3, logf3, tri)


def _causal_iota():
    row = lax.broadcasted_iota(jnp.int32, (BQ, BQ), 0)
    col = lax.broadcasted_iota(jnp.int32, (BQ, BQ), 1)
    return row, col


def _head_specs(nj, head0):
    qin = pl.BlockSpec((1, BQ, HEAD_DIM), lambda h, i: (h + head0, i, 0))
    kin = pl.BlockSpec((1, nj, BQ, HEAD_DIM), lambda h, i: (h + head0, 0, 0, 0))
    qin2 = pl.BlockSpec((1, BQ, 2 * HEAD_DIM), lambda h, i: (h + head0, i, 0))
    kin2 = pl.BlockSpec((1, nj, BQ, 2 * HEAD_DIM), lambda h, i: (h + head0, 0, 0, 0))
    qspec = pl.BlockSpec((1, BQ, HEAD_DIM), lambda h, i: (h, i, 0))
    kspec2 = pl.BlockSpec((1, nj, BQ, 2 * HEAD_DIM), lambda h, i: (h, 0, 0, 0))
    return qin, kin, qin2, kin2, qspec, kspec2


STOP = -105.0
STOP_WIDE = -115.0
FIXED_REF_MAX = 40.0


def _store_kmax(k_ref, kmax_ref, nj):
    def step(j, mx):
        kf = k_ref[0, j].astype(F32)
        return jnp.maximum(mx, jnp.max(jnp.sum(kf * kf, axis=1, keepdims=True), axis=0, keepdims=True))

    mx = lax.fori_loop(0, nj, step, jnp.zeros((1, 1), F32))
    kmax_ref[...] = jnp.broadcast_to(jnp.sqrt(mx), kmax_ref.shape)


def _qk_bound(q, kmax_ref):
    qf = q.astype(F32)
    return jnp.sqrt(jnp.sum(qf * qf, axis=1, keepdims=True)) * kmax_ref[0:1, 0:1] * 1.001


def fox_fwd(qs, kn4, va4, fcol, frow4, *, name):
    _, t, dh = qs.shape
    nh = H_FOX
    nj = t // BQ

    def body(q_ref, k_ref, v_ref, fc_ref, fr_ref, o_ref, lse_ref, kmax_ref):
        i = pl.program_id(1)

        @pl.when(i == 0)
        def _():
            _store_kmax(k_ref, kmax_ref, nj)

        q = q_ref[0]
        fq = fc_ref[0]
        bound = _qk_bound(q, kmax_ref)
        row, col = _causal_iota()

        def gate_at_block_end(j):
            return fr_ref[0, j][:, BQ - 1:BQ]

        def pv(p, j):
            p_hi, p_lo = _split2(p)
            return _dot(p_hi, v_ref[0, j]) + _dot(p_lo, v_ref[0, j])

        def walk(block, live, init):
            carry = block(i, init, True)

            def cond(c):
                n, carry = c
                return jnp.logical_and(n < i, live(jnp.maximum(i - 1 - n, 0), carry))

            _, carry = lax.while_loop(cond, lambda c: (c[0] + 1, block(i - 1 - c[0], c[1], False)), (0, carry))
            return carry

        def fixed_reference(_):
            shift = fq - bound

            def block(j, acc, diag):
                p = jnp.exp(_dot_nt(q, k_ref[0, j]) + shift - fr_ref[0, j])
                if diag:
                    p = jnp.where(col <= row, p, 0.0)
                return acc + pv(p, j)

            def live(j, acc):
                return jnp.max(fq - gate_at_block_end(j) - jnp.log(acc[:, dh:dh + 1])) >= STOP_WIDE

            acc = walk(block, live, jnp.zeros((BQ, 2 * dh), F32))
            l = acc[:, dh:dh + 1]
            return acc[:, :dh] / l, bound + jnp.log(l)

        def running_maximum(_):
            def block(j, carry, diag):
                m, acc = carry
                s = _dot_nt(q, k_ref[0, j]) + fq - fr_ref[0, j]
                if diag:
                    s = jnp.where(col <= row, s, NEG)
                m_new = jnp.maximum(m, jnp.max(s, axis=1, keepdims=True))
                return m_new, jnp.exp(m - m_new) * acc + pv(jnp.exp(s - m_new), j)

            def live(j, carry):
                return jnp.max(bound + fq - gate_at_block_end(j) - carry[0]) >= STOP

            m, acc = walk(block, live, (jnp.full((BQ, 1), NEG, F32), jnp.zeros((BQ, 2 * dh), F32)))
            l = acc[:, dh:dh + 1]
            return acc[:, :dh] / l, m + jnp.log(l)

        o, lse = lax.cond(jnp.max(bound) < FIXED_REF_MAX, fixed_reference, running_maximum, 0)
        o_ref[0] = o
        lse_ref[0] = lse

    qin, kin, _, kin2, qspec, _ = _head_specs(nj, 0)
    cspec = pl.BlockSpec((1, BQ, 1), lambda h, i: (h, i, 0))
    return pl.pallas_call(
        body, grid=(nh, nj),
        in_specs=[qin, kin, kin2, cspec, pl.BlockSpec((1, nj, 1, BQ), lambda h, i: (h, 0, 0, 0))],
        out_specs=[qspec, cspec],
        out_shape=[jax.ShapeDtypeStruct((nh, t, dh), F32), jax.ShapeDtypeStruct((nh, t, 1), F32)],
        scratch_shapes=[pltpu.VMEM((8, 128), F32)],
        compiler_params=_params(), name=name)(qs, kn4, va4, fcol, frow4)


def fox_bwd(qs, kn4, v4, qa, doa, fcol, frow4, o, do, lse, *, name):
    _, t, dh = qs.shape
    nh = H_FOX
    nj = t // BQ

    def body(q_ref, k_ref, v_ref, qa_ref, doa_ref, fc_ref, fr_ref, o_ref, do_ref, lse_ref, dq_ref, dkv_ref, dfk_ref,
             kmax_ref):
        i = pl.program_id(1)

        @pl.when(i == 0)
        def _():
            dkv_ref[...] = jnp.zeros_like(dkv_ref)
            dfk_ref[...] = jnp.zeros_like(dfk_ref)
            _store_kmax(k_ref, kmax_ref, nj)

        q = q_ref[0]
        do_b = do_ref[0]
        fq = fc_ref[0]
        lse_q = lse_ref[0]
        dd = jnp.sum(do_b.astype(F32) * o_ref[0], axis=1, keepdims=True)
        rhs = jnp.concatenate([qa_ref[0], doa_ref[0]], axis=0)
        edge = _qk_bound(q, kmax_ref) + fq - lse_q

        def negligible(j):
            return jnp.logical_and(j < i, jnp.max(edge - fr_ref[0, j][:, BQ - 1:BQ]) < STOP_WIDE)

        first = lax.while_loop(negligible, lambda j: j + 1, 0)

        def block(j, dq, diag):
            k = k_ref[0, j]
            p = jnp.exp(_dot_nt(q, k) + fq - fr_ref[0, j] - lse_q)
            if diag:
                row, col = _causal_iota()
                p = jnp.where(col <= row, p, 0.0)
            ds = p * (_dot_nt(do_b, v_ref[0, j]) - dd)
            ds_b = ds.astype(BF16)
            dkv_ref[0, j] += _dot_tn(jnp.concatenate([ds_b, p.astype(BF16)], axis=0), rhs)
            dfk_ref[0, j] -= jnp.sum(ds, axis=0, keepdims=True)
            return dq + _dot(ds_b, k)

        dq = lax.fori_loop(first, i, lambda j, c: block(j, c, False), jnp.zeros((BQ, dh), F32))
        dq_ref[0] = block(i, dq, True)

    qin, kin, qin2, _, qspec, kspec2 = _head_specs(nj, 0)
    cspec = pl.BlockSpec((1, BQ, 1), lambda h, i: (h, i, 0))
    rspec = pl.BlockSpec((1, nj, 1, BQ), lambda h, i: (h, 0, 0, 0))
    return pl.pallas_call(
        body, grid=(nh, nj),
        in_specs=[qin, kin, kin, qin2, qin2, cspec, rspec, qspec, qin, cspec],
        out_specs=[qspec, kspec2, rspec],
        out_shape=[jax.ShapeDtypeStruct((nh, t, dh), F32), jax.ShapeDtypeStruct((nh, nj, BQ, 2 * dh), F32),
                   jax.ShapeDtypeStruct((nh, nj, 1, BQ), F32)],
        scratch_shapes=[pltpu.VMEM((8, 128), F32)],
        compiler_params=_params(), name=name)(qs, kn4, v4, qa, doa, fcol, frow4, o, do, lse)


def _sb_logs(z, diag):
    e = jnp.exp(-jnp.abs(z))
    sp = jnp.log(1.0 + e)
    logb = jnp.minimum(z, 0.0) - sp
    lom = -jnp.maximum(z, 0.0) - sp
    strict = None
    if diag:
        row, col = _causal_iota()
        strict = col < row
        lom = jnp.where(strict, lom, 0.0)
    return logb, lom, e, strict


def sb_fwd(qs, kn4, va4, tri, *, name):
    _, t, dh = qs.shape
    nh = H_SB
    nj = t // BQ
    assert nj <= 128

    def body(q_ref, k_ref, v_ref, tri_ref, o_ref, rs_ref):
        i = pl.program_id(1)
        q = q_ref[0]
        tri_m = tri_ref[...]
        lane = lax.broadcasted_iota(jnp.int32, (BQ, 128), 1)

        def block(j, carry, diag):
            run, acc, rall = carry
            logb, lom, _, strict = _sb_logs(_dot_nt(q, k_ref[0, j]), diag)
            hi, lo = _split2(lom)
            w = jnp.exp(logb + (_dot(hi, tri_m) + _dot(lo, tri_m)) + run)
            if diag:
                w = jnp.where(strict, w, 0.0)
            acc = acc + _dot(w.astype(BF16), v_ref[0, j])
            rall = jnp.where(lane == j, run, rall)
            return run + jnp.sum(lom, axis=1, keepdims=True), acc, rall

        init = (jnp.zeros((BQ, 1), F32), jnp.zeros((BQ, 2 * dh), F32), jnp.full((BQ, 128), NEG, F32))
        carry = block(i, init, True)

        def cond(c):
            n, carry = c
            return jnp.logical_and(n < i, jnp.max(carry[0]) >= STOP)

        _, (_, acc, rall) = lax.while_loop(cond, lambda c: (c[0] + 1, block(i - 1 - c[0], c[1], False)), (0, carry))
        o_ref[0] = acc[:, :dh].astype(BF16)
        rs_ref[0] = rall

    qin, kin, _, kin2, qspec, _ = _head_specs(nj, H_FOX)
    rspec = pl.BlockSpec((1, BQ, 128), lambda h, i: (h, i, 0))
    return pl.pallas_call(
        body, grid=(nh, nj),
        in_specs=[qin, kin, kin2, pl.BlockSpec((BQ, BQ), lambda h, i: (0, 0))],
        out_specs=[qspec, rspec],
        out_shape=[jax.ShapeDtypeStruct((nh, t, dh), BF16), jax.ShapeDtypeStruct((nh, t, 128), F32)],
        compiler_params=_params(), name=name)(qs, kn4, va4, tri)


def sb_bwd(qs, kn4, v4, qa, doa, tri, do, rsave, *, name):
    _, t, dh = qs.shape
    nh = H_SB
    nj = t // BQ

    def body(q_ref, k_ref, v_ref, qa_ref, doa_ref, tri_ref, do_ref, rs_ref, dq_ref, dkv_ref):
        i = pl.program_id(1)

        @pl.when(i == 0)
        def _():
            dkv_ref[...] = jnp.zeros_like(dkv_ref)

        q = q_ref[0]
        do_b = do_ref[0]
        tri_m = tri_ref[...]
        rall = rs_ref[0]
        lane = lax.broadcasted_iota(jnp.int32, (BQ, 128), 1)
        rhs = jnp.concatenate([qa_ref[0], doa_ref[0]], axis=0)
        lane1 = lax.broadcasted_iota(jnp.int32, (1, 128), 1)
        unvisited = jnp.logical_and(lane1 < i, jnp.max(rall, axis=0, keepdims=True) < STOP)
        first = jnp.sum(unvisited.astype(jnp.int32))

        def block(j, carry, diag):
            dq, ecar = carry
            k = k_ref[0, j]
            z = _dot_nt(q, k)
            logb, lom, e, strict = _sb_logs(z, diag)
            hi, lo = _split2(lom)
            run = jnp.sum(jnp.where(lane == j, rall, 0.0), axis=1, keepdims=True)
            w = jnp.exp(logb + (_dot(hi, tri_m) + _dot(lo, tri_m)) + run)
            if diag:
                w = jnp.where(strict, w, 0.0)
            da = w * _dot_nt(do_b, v_ref[0, j])
            before = _dot_nt(da.astype(BF16), tri_m) + ecar
            inv = 1.0 / (1.0 + e)
            beta = jnp.where(z >= 0.0, 1.0, e) * inv
            one_minus = jnp.where(z >= 0.0, e, 1.0) * inv
            dz = da * one_minus - before * beta
            if diag:
                dz = jnp.where(strict, dz, 0.0)
            dz_b = dz.astype(BF16)
            dkv_ref[0, j] += _dot_tn(jnp.concatenate([dz_b, w.astype(BF16)], axis=0), rhs)
            return dq + _dot(dz_b, k), ecar + jnp.sum(da, axis=1, keepdims=True)

        carry = lax.fori_loop(first, i, lambda j, c: block(j, c, False),
                              (jnp.zeros((BQ, dh), F32), jnp.zeros((BQ, 1), F32)))
        dq, _ = block(i, carry, True)
        dq_ref[0] = dq

    qin, kin, qin2, _, qspec, kspec2 = _head_specs(nj, H_FOX)
    return pl.pallas_call(
        body, grid=(nh, nj),
        in_specs=[qin, kin, kin, qin2, qin2, pl.BlockSpec((BQ, BQ), lambda h, i: (0, 0)), qin,
                  pl.BlockSpec((1, BQ, 128), lambda h, i: (h, i, 0))],
        out_specs=[qspec, kspec2],
        out_shape=[jax.ShapeDtypeStruct((nh, t, dh), F32), jax.ShapeDtypeStruct((nh, nj, BQ, 2 * dh), F32)],
        compiler_params=_params(), name=name)(qs, kn4, v4, qa, doa, tri, do, rsave)


def loss_head(y, target, *, name, tm=512):
    t, d = y.shape

    def body(y_ref, t_ref, l_ref, dy_ref, acc_ref):
        i = pl.program_id(0)
        diff = y_ref[...] - t_ref[...]
        dy_ref[...] = diff * (1.0 / d)
        part = jnp.sum(diff * diff, axis=0, keepdims=True)

        @pl.when(i == 0)
        def _():
            acc_ref[...] = part

        @pl.when(i > 0)
        def _():
            acc_ref[...] += part

        @pl.when(i == pl.num_programs(0) - 1)
        def _():
            l_ref[...] = jnp.full(l_ref.shape, (0.5 / d) * jnp.sum(acc_ref[...]), F32)

    row = pl.BlockSpec((tm, d), lambda i: (i, 0))
    return pl.pallas_call(
        body, grid=(t // tm,), in_specs=[row, row],
        out_specs=[pl.BlockSpec((8, 128), lambda i: (0, 0)), row],
        out_shape=[jax.ShapeDtypeStruct((8, 128), F32), jax.ShapeDtypeStruct((t, d), F32)],
        scratch_shapes=[pltpu.VMEM((1, d), F32)], compiler_params=_params(), name=name)(y, target)


def _to_heads(a):
    t = a.shape[0]
    return a.reshape(t, N_HEADS, HEAD_DIM).transpose(1, 0, 2)


def _from_heads(a):
    t = a.shape[1]
    return a.transpose(1, 0, 2).reshape(t, MIX)


def _lanes_to_chunks(a):
    r, t = a.shape
    return a.reshape(r, t // 128, 128).transpose(1, 0, 2)


def _chunks_to_lanes(a):
    nc, r, _ = a.shape
    return a.transpose(1, 0, 2).reshape(r, nc * 128)


def _constants():
    idx = jnp.arange(128)
    bd = (idx[:, None] // HEAD_DIM == idx[None, :] // HEAD_DIM).astype(BF16)
    tri_le = (idx[:, None] <= idx[None, :]).astype(BF16)
    tri_ge = (idx[:, None] >= idx[None, :]).astype(BF16)
    jdx = jnp.arange(BQ)
    tri_gt = (jdx[:, None] > jdx[None, :]).astype(BF16)
    return dict(bd=bd, tri_le=tri_le, tri_ge=tri_ge, tri_gt=tri_gt)


def attn_layer_fwd(h, w, cst):
    t = h.shape[0]
    nj = t // BQ
    xn = rms_fwd(h, w["norm"], name="rms_fwd")
    proj = mm_nn(xn, w["w_in"], tn=640, name="attn_in_proj")
    qs, kn, vb, logf = attn_prep_fwd(proj, w["gq"], w["gk"], w["fbias"], cst["bd"], name="attn_prep_fwd")
    logf3 = _lanes_to_chunks(logf)
    cum = _chunks_to_lanes(gate_cumsum(logf3, cst["tri_le"], name="gate_cumsum"))
    fcol = cum.reshape(H_FOX, t, 1)
    frow4 = cum.reshape(H_FOX, nj, 1, BQ)
    qh = _to_heads(qs)
    kh4 = _to_heads(kn).reshape(N_HEADS, nj, BQ, HEAD_DIM)
    vh4 = _to_heads(vb).reshape(N_HEADS, nj, BQ, HEAD_DIM)
    ones = jnp.ones(vh4.shape[:-1] + (1,), BF16)
    va4 = jnp.concatenate([vh4, ones, jnp.zeros(vh4.shape[:-1] + (HEAD_DIM - 1,), BF16)], axis=-1)
    o_f, lse = fox_fwd(qh, kh4, va4, fcol, frow4, name="fox_fwd")
    o_s, rsave = sb_fwd(qh, kh4, va4, cst["tri_gt"], name="sb_fwd")
    o = _from_heads(jnp.concatenate([o_f.astype(BF16), o_s], axis=0))
    h2 = mm_nn(o, w["w_out"], add=h, name="mix_out_proj")
    saved = dict(h=h, xn=xn, proj=proj, logf3=logf3, fcol=fcol, frow4=frow4, qh=qh, kh4=kh4, vh4=vh4,
                 o_f=o_f, lse=lse, rsave=rsave, o=o)
    return h2, saved


def attn_layer_bwd(dh, w, s, cst):
    t = dh.shape[0]
    dh3 = dh[None]
    do = mm_nt(dh3, w["w_out"], out_dtype=BF16, name="mix_out_bwd_bf16")
    g_w_out = mm_tn(s["o"], dh3, name="mix_out_wgrad")
    doh = _to_heads(do)
    zeros = jnp.zeros_like(doh)
    qa = jnp.concatenate([s["qh"], zeros], axis=-1)
    doa = jnp.concatenate([zeros, doh], axis=-1)
    dq_f, dkv_f, dfk = fox_bwd(s["qh"], s["kh4"], s["vh4"], qa, doa, s["fcol"], s["frow4"], s["o_f"], doh, s["lse"],
                               name="fox_bwd")
    dq_s, dkv_s = sb_bwd(s["qh"], s["kh4"], s["vh4"], qa, doa, cst["tri_gt"], doh, s["rsave"], name="sb_bwd")
    dqs = _from_heads(jnp.concatenate([dq_f, dq_s], axis=0))
    dkv = jnp.concatenate([dkv_f, dkv_s], axis=0).reshape(N_HEADS, t, 2 * HEAD_DIM)
    dkn = _from_heads(dkv[:, :, :HEAD_DIM])
    dv = _from_heads(dkv[:, :, HEAD_DIM:])
    dcum3 = _lanes_to_chunks(dfk.reshape(H_FOX, t))
    dfl3, dbias = gate_cumsum_bwd(dcum3, s["logf3"], cst["tri_ge"], name="gate_cumsum_bwd")
    dfl = jnp.pad(_chunks_to_lanes(dfl3).T, ((0, 0), (0, 128 - H_FOX))).astype(BF16)
    dproj, dgq, dgk = attn_prep_bwd(s["proj"], dqs, dkn, dv, dfl, w["gq"], w["gk"], cst["bd"], name="attn_prep_bwd")
    g_w_in = mm_tn(s["xn"], dproj[None], tn=640, name="attn_in_wgrad")[:, :ATTN_IN]
    dxn = mm_nt(dproj[None], w["w_in"], name="attn_in_bwd")
    dh2, g_norm = rms_bwd(dxn, s["h"], w["norm"], dh, name="rms_bwd")
    dgq = dgq.reshape(N_HEADS, HEAD_DIM)
    dgk = dgk.reshape(N_HEADS, HEAD_DIM)
    grads = dict(norm=g_norm[0], w_in=g_w_in, f_bias=dbias[:, 0], fox_q=dgq[:H_FOX].sum(0), fox_k=dgk[:H_FOX].sum(0),
                 sb_q=dgq[H_FOX:].sum(0), sb_k=dgk[H_FOX:].sum(0), w_out=g_w_out)
    return dh2, grads


def conv_layer_fwd(h, w):
    xn = rms_fwd(h, w["norm"], name="rms_fwd")
    proj3 = mm_nn(xn, w["w_in"], parts=3, name="conv_in_proj")
    y = conv_mix_fwd(proj3, w["ck"], name="conv_mix_fwd")
    h2 = mm_nn(y, w["w_out"], add=h, name="mix_out_proj")
    return h2, dict(h=h, xn=xn, proj3=proj3, y=y)


def conv_layer_bwd(dh, w, s):
    dh3 = dh[None]
    dy = mm_nt(dh3, w["w_out"], name="mix_out_bwd")
    g_w_out = mm_tn(s["y"], dh3, name="mix_out_wgrad")
    dproj3, dck = conv_mix_bwd(dy, s["proj3"], w["ck"], name="conv_mix_bwd")
    g_w_in = mm_tn(s["xn"], dproj3, name="conv_in_wgrad")
    dxn = mm_nt(dproj3, w["w_in"], name="conv_in_bwd")
    dh2, g_norm = rms_bwd(dxn, s["h"], w["norm"], dh, name="rms_bwd")
    return dh2, dict(norm=g_norm[0], w_in=g_w_in, ck=dck[0, :3], w_out=g_w_out)


def ffn_layer_fwd(h, w):
    xn = rms_fwd(h, w["norm"], name="rms_fwd")
    up2 = mm_nn(xn, w["w_up"], parts=2, tn=1408, name="ffn_up_proj")
    act = ffn_act_fwd(up2, w["cw2"], name="ffn_act_fwd")
    h2 = mm_nn(act, w["w_down"], add=h, name="ffn_down_proj")
    return h2, dict(h=h, xn=xn, up2=up2, act=act)


def ffn_layer_bwd(dh, w, s):
    dh3 = dh[None]
    dact = mm_nt(dh3, w["w_down"], tn=1408, name="ffn_down_bwd")
    g_w_down = mm_tn(s["act"], dh3, tk=1408, name="ffn_down_wgrad")
    dup2, dcw = ffn_act_bwd(dact, s["up2"], w["cw2"], name="ffn_act_bwd")
    g_w_up = mm_tn(s["xn"], dup2, tn=1408, name="ffn_up_wgrad")
    dxn = mm_nt(dup2, w["w_up"], name="ffn_up_bwd")
    dh2, g_norm = rms_bwd(dxn, s["h"], w["norm"], dh, name="rms_bwd")
    g_cw = jnp.concatenate([dcw[0, :3], dcw[1, :3]], axis=1)
    return dh2, dict(norm=g_norm[0], w_up=g_w_up, cw=g_cw, w_down=g_w_down)


def forward_backward(x, target, wa, wc, wf):
    cst = _constants()
    depth = len(wf)
    h = x
    saved = []
    for layer in range(depth):
        i = layer // 2
        if layer % 2 == 0:
            h, sm = attn_layer_fwd(h, wa[i], cst)
        else:
            h, sm = conv_layer_fwd(h, wc[i])
        h, sf = ffn_layer_fwd(h, wf[layer])
        saved.append((sm, sf))
    loss_blk, dh = loss_head(h, target, name="loss_head")
    ga, gc, gf = [None] * len(wa), [None] * len(wc), [None] * depth
    for layer in reversed(range(depth)):
        i = layer // 2
        sm, sf = saved[layer]
        dh, gf[layer] = ffn_layer_bwd(dh, wf[layer], sf)
        if layer % 2 == 0:
            dh, ga[i] = attn_layer_bwd(dh, wa[i], sm, cst)
        else:
            dh, gc[i] = conv_layer_bwd(dh, wc[i], sm)
    return loss_blk, dh, ga, gc, gf


def _part_rows(shape, width, row_mult):
    n = 1
    for s in shape:
        n *= s
    rows = -(-n // width)
    return -(-rows // row_mult) * row_mult


def _pack_rows(arrs, width, row_mult, dtype, total_rows=None):
    parts = []
    used = 0
    for a in arrs:
        rows = _part_rows(a.shape, width, row_mult)
        flat = a.astype(dtype).reshape(-1)
        flat = jnp.pad(flat, (0, rows * width - flat.shape[0]))
        parts.append(flat.reshape(rows, width))
        used += rows
    if total_rows is not None and total_rows > used:
        parts.append(jnp.zeros((total_rows - used, width), dtype))
    return jnp.concatenate(parts, axis=0)


def _unpack_rows(packed, shapes, width, row_mult):
    out = []
    off = 0
    for shape in shapes:
        rows = _part_rows(shape, width, row_mult)
        n = 1
        for s in shape:
            n *= s
        out.append(packed[off:off + rows].reshape(-1)[:n].reshape(shape))
        off += rows
    return out


BIG_NAMES = ("attn_w_in", "attn_w_out", "conv_w_in", "conv_w_out", "ffn_w_up", "ffn_w_down")
BIG_AXIS = {"attn_w_in": 2, "attn_w_out": 1, "conv_w_in": 2, "conv_w_out": 1, "ffn_w_up": 2, "ffn_w_down": 1}
BIG_WIDTH = 1024
BIG_ROW_MULT = 16
BIG_TILE = 512
SMALL_SHARDED = ("conv_norm", "conv_kernel", "ffn_conv")
SMALL_AXIS = {"conv_norm": 1, "conv_kernel": 2, "ffn_conv": 2}
SMALL_REPLICATED = ("attn_norm", "attn_f_bias", "fox_q_gain", "fox_k_gain", "sb_q_gain", "sb_k_gain", "ffn_norm")
WEIGHT_ORDER = ("attn_norm", "attn_w_in", "attn_f_bias", "fox_q_gain", "fox_k_gain", "sb_q_gain", "sb_k_gain",
                "attn_w_out", "conv_norm", "conv_w_in", "conv_kernel", "conv_w_out", "ffn_norm", "ffn_w_up",
                "ffn_conv", "ffn_w_down")


def _big_total_rows(shapes):
    used = sum(_part_rows(s, BIG_WIDTH, BIG_ROW_MULT) for s in shapes)
    return -(-used // BIG_TILE) * BIG_TILE


def _place():
    x, y, c = lax.axis_index("x"), lax.axis_index("y"), lax.axis_index("c")
    other_chips = [(1 - x, y), (x, 1 - y), (1 - x, 1 - y)]
    return x, y, c, other_chips


_ANY = pl.BlockSpec(memory_space=pl.ANY)


def gather_chips(big, small, *, name):
    def body(big_ref, small_ref, obig_ref, osmall_ref, send_sems, recv_sems, local_sems):
        x, y, c, chips = _place()
        k = 2 * x + y
        pairs = [(big_ref, obig_ref), (small_ref, osmall_ref)]

        def copy(j, n, slot):
            px, py = chips[j]
            src, dst = pairs[n]
            return pltpu.make_async_remote_copy(src_ref=src, dst_ref=dst.at[slot], send_sem=send_sems.at[2 * j + n],
                                                recv_sem=recv_sems.at[2 * j + n], device_id=(px, py, c),
                                                device_id_type=MESH)

        local = [pltpu.make_async_copy(src, dst.at[k], local_sems.at[n]) for n, (src, dst) in enumerate(pairs)]
        for cp in local:
            cp.start()
        sends = [copy(j, n, k) for j in range(3) for n in range(2)]
        for cp in sends:
            cp.start()
        for j, (px, py) in enumerate(chips):
            for n in range(2):
                copy(j, n, 2 * px + py).wait_recv()
        for cp in sends:
            cp.wait_send()
        for cp in local:
            cp.wait()

    return pl.pallas_call(
        body, in_specs=[_ANY, _ANY], out_specs=[_ANY, _ANY],
        out_shape=[jax.ShapeDtypeStruct((N_CHIPS,) + big.shape, big.dtype),
                   jax.ShapeDtypeStruct((N_CHIPS,) + small.shape, small.dtype)],
        scratch_shapes=[pltpu.SemaphoreType.DMA((6,)), pltpu.SemaphoreType.DMA((6,)), pltpu.SemaphoreType.DMA((2,))],
        name=name)(big, small)


def scatter_chips(chunks, *, name):
    def body(g_ref, o_ref, send_sems, recv_sems, local_sem):
        x, y, c, chips = _place()
        k = 2 * x + y

        def copy(j, src_slot, dst_slot):
            px, py = chips[j]
            return pltpu.make_async_remote_copy(src_ref=g_ref.at[src_slot], dst_ref=o_ref.at[dst_slot],
                                                send_sem=send_sems.at[j], recv_sem=recv_sems.at[j],
                                                device_id=(px, py, c), device_id_type=MESH)

        local = pltpu.make_async_copy(g_ref.at[k], o_ref.at[k], local_sem)
        local.start()
        sends = [copy(j, 2 * px + py, k) for j, (px, py) in enumerate(chips)]
        for cp in sends:
            cp.start()
        for j, (px, py) in enumerate(chips):
            copy(j, k, 2 * px + py).wait_recv()
        for cp in sends:
            cp.wait_send()
        local.wait()

    return pl.pallas_call(
        body, in_specs=[_ANY], out_specs=_ANY, out_shape=jax.ShapeDtypeStruct(chunks.shape, chunks.dtype),
        scratch_shapes=[pltpu.SemaphoreType.DMA((3,)), pltpu.SemaphoreType.DMA((3,)), pltpu.SemaphoreType.DMA],
        name=name)(chunks)


def swap_cores(a, *, name):
    def body(a_ref, o_ref, send_sem, recv_sem):
        x, y, c, _ = _place()
        cp = pltpu.make_async_remote_copy(src_ref=a_ref, dst_ref=o_ref, send_sem=send_sem, recv_sem=recv_sem,
                                          device_id=(x, y, 1 - c), device_id_type=MESH)
        cp.start()
        cp.wait()

    return pl.pallas_call(
        body, in_specs=[_ANY], out_specs=_ANY, out_shape=jax.ShapeDtypeStruct(a.shape, a.dtype),
        scratch_shapes=[pltpu.SemaphoreType.DMA, pltpu.SemaphoreType.DMA], name=name)(a)


def allreduce_small(p, *, name):
    r, w = p.shape

    def body(p_ref, o_ref, buf, send_sems, recv_sems):
        x, y, c, _ = _place()
        me = 4 * x + 2 * y + c
        buf[me] = p_ref[...]

        def peer_of(m):
            return (1 - x if m & 4 else x, 1 - y if m & 2 else y, 1 - c if m & 1 else c)

        def copy(m, slot):
            return pltpu.make_async_remote_copy(src_ref=p_ref, dst_ref=buf.at[slot], send_sem=send_sems.at[m - 1],
                                                recv_sem=recv_sems.at[m - 1], device_id=peer_of(m),
                                                device_id_type=MESH)

        sends = [copy(m, me) for m in range(1, 8)]
        for cp in sends:
            cp.start()
        for m in range(1, 8):
            px, py, pc = peer_of(m)
            copy(m, 4 * px + 2 * py + pc).wait_recv()
        for cp in sends:
            cp.wait_send()
        acc = buf[0]
        for d in range(1, 8):
            acc = acc + buf[d]
        o_ref[...] = acc

    vm = pl.BlockSpec(memory_space=pltpu.VMEM)
    return pl.pallas_call(
        body, in_specs=[vm], out_specs=vm, out_shape=jax.ShapeDtypeStruct((r, w), F32),
        scratch_shapes=[pltpu.VMEM((8, r, w), F32), pltpu.SemaphoreType.DMA((7,)), pltpu.SemaphoreType.DMA((7,))],
        name=name)(p)


def sum_chips(rv, *, name):
    _, r, w = rv.shape

    def body(a_ref, b_ref, c_ref, d_ref, o_ref):
        o_ref[...] = ((a_ref[0].astype(F32) + b_ref[0].astype(F32)) + c_ref[0].astype(F32)) + d_ref[0].astype(F32)

    spec = lambda kk: pl.BlockSpec((1, BIG_TILE, w), lambda i: (kk, i, 0))
    return pl.pallas_call(
        body, grid=(r // BIG_TILE,), in_specs=[spec(0), spec(1), spec(2), spec(3)],
        out_specs=pl.BlockSpec((BIG_TILE, w), lambda i: (i, 0)), out_shape=jax.ShapeDtypeStruct((r, w), F32),
        compiler_params=_params(), name=name)(rv, rv, rv, rv)


def add_pair(a, b, *, name):
    r, w = a.shape

    def body(a_ref, b_ref, o_ref):
        o_ref[...] = a_ref[...] + b_ref[...]

    spec = pl.BlockSpec((BIG_TILE, w), lambda i: (i, 0))
    return pl.pallas_call(body, grid=(r // BIG_TILE,), in_specs=[spec, spec], out_specs=spec,
                          out_shape=jax.ShapeDtypeStruct((r, w), F32), compiler_params=_params(), name=name)(a, b)


def adamw(w, g, m, v, *, tm, name):
    r, c = w.shape
    assert r % tm == 0

    def body(w_ref, g_ref, m_ref, v_ref, d_ref, nm_ref, nv_ref):
        g_ = g_ref[...]
        m_ = ADAM_B1 * m_ref[...] + (1.0 - ADAM_B1) * g_
        v_ = ADAM_B2 * v_ref[...] + (1.0 - ADAM_B2) * (g_ * g_)
        m_hat = m_ / (1.0 - ADAM_B1 ** ADAM_STEP)
        v_hat = v_ / (1.0 - ADAM_B2 ** ADAM_STEP)
        d_ref[...] = -ADAM_LR * (m_hat / (jnp.sqrt(v_hat) + ADAM_EPS) + ADAM_WD * w_ref[...])
        nm_ref[...] = m_
        nv_ref[...] = v_

    spec = pl.BlockSpec((tm, c), lambda i: (i, 0))
    return pl.pallas_call(body, grid=(r // tm,), in_specs=[spec] * 4, out_specs=[spec] * 3,
                          out_shape=[jax.ShapeDtypeStruct((r, c), F32)] * 3, compiler_params=_params(), name=name)(w, g, m, v)


def kernel(x, attn_norm, attn_w_in, attn_f_bias, fox_q_gain, fox_k_gain, sb_q_gain, sb_k_gain, attn_w_out, conv_norm, conv_w_in, conv_kernel, conv_w_out, ffn_norm, ffn_w_up, ffn_conv, ffn_w_down, loss_target, m_attn_norm, m_attn_w_in, m_attn_f_bias, m_fox_q_gain, m_fox_k_gain, m_sb_q_gain, m_sb_k_gain, m_attn_w_out, m_conv_norm, m_conv_w_in, m_conv_kernel, m_conv_w_out, m_ffn_norm, m_ffn_w_up, m_ffn_conv, m_ffn_w_down, v_attn_norm, v_attn_w_in, v_attn_f_bias, v_fox_q_gain, v_fox_k_gain, v_sb_q_gain, v_sb_k_gain, v_attn_w_out, v_conv_norm, v_conv_w_in, v_conv_kernel, v_conv_w_out, v_ffn_norm, v_ffn_w_up, v_ffn_conv, v_ffn_w_down):
    a = dict(locals())
    chip = 2 * lax.axis_index("x") + lax.axis_index("y")
    n_attn, n_conv, depth = attn_norm.shape[0], conv_norm.shape[0], ffn_norm.shape[0]

    big_shapes = [a[n].shape for n in BIG_NAMES]
    big_rows = _big_total_rows(big_shapes)
    small_shapes = [a[n].shape for n in SMALL_SHARDED]
    packed_w = _pack_rows([a[n] for n in BIG_NAMES], BIG_WIDTH, BIG_ROW_MULT, BF16, big_rows)
    packed_s = _pack_rows([a[n] for n in SMALL_SHARDED], 128, 8, F32)
    gath_w, gath_s = gather_chips(packed_w, packed_s, name="gather_weights")
    full = {}
    per_chip = [_unpack_rows(gath_w[kk], big_shapes, BIG_WIDTH, BIG_ROW_MULT) for kk in range(N_CHIPS)]
    for n, name in enumerate(BIG_NAMES):
        full[name] = jnp.concatenate([per_chip[kk][n] for kk in range(N_CHIPS)], axis=BIG_AXIS[name])
    per_chip = [_unpack_rows(gath_s[kk], small_shapes, 128, 8) for kk in range(N_CHIPS)]
    for n, name in enumerate(SMALL_SHARDED):
        full[name] = jnp.concatenate([per_chip[kk][n] for kk in range(N_CHIPS)], axis=SMALL_AXIS[name])

    wa, wc, wf = [], [], []
    for i in range(n_attn):
        wa.append(dict(
            norm=attn_norm[i][None],
            w_in=jnp.pad(full["attn_w_in"][i], ((0, 0), (0, ATTN_IN_PAD - ATTN_IN))),
            fbias=jnp.pad(attn_f_bias[i], (0, 128 - H_FOX))[None],
            gq=jnp.concatenate([jnp.tile(fox_q_gain[i], H_FOX), jnp.tile(sb_q_gain[i], H_SB)])[None],
            gk=jnp.concatenate([jnp.tile(fox_k_gain[i], H_FOX), jnp.tile(sb_k_gain[i], H_SB)])[None],
            w_out=full["attn_w_out"][i]))
    for i in range(n_conv):
        wc.append(dict(norm=full["conv_norm"][i][None], w_in=full["conv_w_in"][i], ck=full["conv_kernel"][i][None],
                       w_out=full["conv_w_out"][i]))
    for l in range(depth):
        cw = full["ffn_conv"][l]
        wf.append(dict(norm=ffn_norm[l][None], w_up=full["ffn_w_up"][l], cw2=jnp.stack([cw[:, :D_FF], cw[:, D_FF:]]),
                       w_down=full["ffn_w_down"][l]))
    loss_blk, grad_x, ga, gc, gf = forward_backward(x[0], loss_target[0], wa, wc, wf)

    g_full = {
        "attn_w_in": jnp.stack([g["w_in"] for g in ga]), "attn_w_out": jnp.stack([g["w_out"] for g in ga]),
        "conv_w_in": jnp.stack([g["w_in"] for g in gc]), "conv_w_out": jnp.stack([g["w_out"] for g in gc]),
        "ffn_w_up": jnp.stack([g["w_up"] for g in gf]), "ffn_w_down": jnp.stack([g["w_down"] for g in gf]),
    }
    chunks = []
    for kk in range(N_CHIPS):
        parts = []
        for name in BIG_NAMES:
            width = a[name].shape[BIG_AXIS[name]]
            parts.append(lax.slice_in_dim(g_full[name], kk * width, (kk + 1) * width, axis=BIG_AXIS[name]))
        chunks.append(_pack_rows(parts, BIG_WIDTH, BIG_ROW_MULT, BF16, big_rows))
    landed = scatter_chips(jnp.stack(chunks), name="scatter_grads")
    mine = sum_chips(landed, name="sum_chips")
    theirs = swap_cores(mine, name="swap_cores")
    g_big = _unpack_rows(add_pair(mine, theirs, name="add_cores"), big_shapes, BIG_WIDTH, BIG_ROW_MULT)
    grads = dict(zip(BIG_NAMES, g_big))

    small_full = [
        loss_blk,
        jnp.stack([g["norm"] for g in ga]), jnp.stack([g["f_bias"] for g in ga]),
        jnp.stack([g["fox_q"] for g in ga]), jnp.stack([g["fox_k"] for g in ga]),
        jnp.stack([g["sb_q"] for g in ga]), jnp.stack([g["sb_k"] for g in ga]),
        jnp.stack([g["norm"] for g in gf]),
        jnp.stack([g["norm"] for g in gc]), jnp.stack([g["ck"] for g in gc]), jnp.stack([g["cw"] for g in gf]),
    ]
    summed = allreduce_small(_pack_rows(small_full, 128, 8, F32), name="allreduce_small")
    parts = _unpack_rows(summed, [p.shape for p in small_full], 128, 8)
    loss = parts[0][0, 0]
    for name, g in zip(SMALL_REPLICATED, parts[1:8]):
        grads[name] = g
    for name, g in zip(SMALL_SHARDED, parts[8:]):
        width = a[name].shape[SMALL_AXIS[name]]
        grads[name] = lax.dynamic_slice_in_dim(g, chip * width, width, axis=SMALL_AXIS[name])

    delta, new_m, new_v = {}, {}, {}
    for name in BIG_NAMES:
        shape = a[name].shape
        flat = lambda arr: arr.reshape(-1, shape[-1])
        d_, m_, v_ = adamw(flat(a[name]), flat(grads[name]), flat(a["m_" + name]), flat(a["v_" + name]), tm=256,
                           name="adamw")
        delta[name], new_m[name], new_v[name] = d_.reshape(shape), m_.reshape(shape), v_.reshape(shape)
    small_names = SMALL_REPLICATED + SMALL_SHARDED
    small_shapes_local = [a[n].shape for n in small_names]
    pack = lambda prefix, src: _pack_rows([src[prefix + n] for n in small_names], 128, 8, F32)
    packed = adamw(pack("", a), pack("", grads), pack("m_", a), pack("v_", a), tm=8, name="adamw_small")
    for store, buf in zip((delta, new_m, new_v), packed):
        for name, arr in zip(small_names, _unpack_rows(buf, small_shapes_local, 128, 8)):
            store[name] = arr

    return (loss, grad_x[None], *[grads[n] for n in WEIGHT_ORDER], *[delta[n] for n in WEIGHT_ORDER],
            *[new_m[n] for n in WEIGHT_ORDER], *[new_v[n] for n in WEIGHT_ORDER])
```

```python
import functools

import jax
import jax.numpy as jnp
from jax import lax
from jax.experimental import pallas as pl
from jax.experimental.pallas import tpu as pltpu

F32 = jnp.float32
BF16 = jnp.bfloat16

D_MODEL = 1024
HEAD_DIM = 64
H_FOX = 8
H_SB = 8
N_HEADS = H_FOX + H_SB
MIX = N_HEADS * HEAD_DIM
ATTN_IN = 3 * MIX + H_FOX
ATTN_IN_PAD = 3 * MIX + 128
D_FF = 2816
EPS = 1e-6
SCALE = HEAD_DIM ** -0.5
NEG = -1e30

ADAM_LR = 0.001
ADAM_B1 = 0.9
ADAM_B2 = 0.999
ADAM_EPS = 1e-08
ADAM_WD = 0.01
ADAM_STEP = 10

VMEM_LIMIT = 56 * 1024 * 1024
HALO = 8
BQ = 512
N_CHIPS = 4
MESH = pl.DeviceIdType.MESH


def _params(**kw):
    return pltpu.CompilerParams(vmem_limit_bytes=VMEM_LIMIT, **kw)


def _dot(a, b):
    return jnp.dot(a, b, preferred_element_type=F32)


def _dot_nt(a, b):
    return lax.dot_general(a, b, (((1,), (1,)), ((), ())), preferred_element_type=F32)


def _dot_tn(a, b):
    return lax.dot_general(a, b, (((0,), (0,)), ((), ())), preferred_element_type=F32)


def _split2(x):
    hi = x.astype(BF16)
    lo = (x - hi.astype(F32)).astype(BF16)
    return hi, lo


def _split3(x):
    hi = x.astype(BF16)
    r = x - hi.astype(F32)
    mid = r.astype(BF16)
    lo = (r - mid.astype(F32)).astype(BF16)
    return hi, mid, lo


def mm_nn(a, b, *, add=None, out_dtype=F32, parts=1, tm=512, tn=512, name):
    m, k = a.shape
    n = b.shape[1]
    np_ = n // parts
    nb = np_ // tn
    assert m % tm == 0 and np_ % tn == 0

    def body(*refs):
        if add is None:
            a_ref, b_ref, o_ref = refs
            acc = _dot(a_ref[...].astype(BF16), b_ref[...])
        else:
            a_ref, b_ref, r_ref, o_ref = refs
            acc = _dot(a_ref[...].astype(BF16), b_ref[...]) + r_ref[...]
        o_ref[...] = acc.astype(out_dtype).reshape(o_ref.shape)

    in_specs = [pl.BlockSpec((tm, k), lambda i, j: (i, 0)), pl.BlockSpec((k, tn), lambda i, j: (0, j))]
    args = [a, b]
    if add is not None:
        in_specs.append(pl.BlockSpec((tm, tn), lambda i, j: (i, j)))
        args.append(add)
    if parts == 1:
        out_spec = pl.BlockSpec((tm, tn), lambda i, j: (i, j))
        out_shape = jax.ShapeDtypeStruct((m, n), out_dtype)
    else:
        out_spec = pl.BlockSpec((1, tm, tn), lambda i, j: (j // nb, i, j % nb))
        out_shape = jax.ShapeDtypeStruct((parts, m, np_), out_dtype)
    return pl.pallas_call(body, grid=(m // tm, n // tn), in_specs=in_specs, out_specs=out_spec,
                          out_shape=out_shape, compiler_params=_params(), name=name)(*args)


def mm_nt(a3, b, *, out_dtype=F32, tm=512, tn=512, name):
    p, m, kp = a3.shape
    n = b.shape[0]
    assert m % tm == 0 and n % tn == 0 and b.shape[1] == p * kp

    def body(a_ref, b_ref, o_ref, acc_ref):
        part = pl.program_id(2)
        prod = _dot_nt(a_ref[0].astype(BF16), b_ref[...])

        @pl.when(part == 0)
        def _():
            acc_ref[...] = prod

        @pl.when(part > 0)
        def _():
            acc_ref[...] += prod

        @pl.when(part == p - 1)
        def _():
            o_ref[...] = acc_ref[...].astype(out_dtype)

    return pl.pallas_call(
        body, grid=(m // tm, n // tn, p),
        in_specs=[pl.BlockSpec((1, tm, kp), lambda i, j, q: (q, i, 0)), pl.BlockSpec((tn, kp), lambda i, j, q: (j, q))],
        out_specs=pl.BlockSpec((tm, tn), lambda i, j, q: (i, j)),
        out_shape=jax.ShapeDtypeStruct((m, n), out_dtype),
        scratch_shapes=[pltpu.VMEM((tm, tn), F32)],
        compiler_params=_params(), name=name)(a3, b)


def mm_tn(a, b3, *, tk=512, tn=512, tt=2048, name):
    t, k = a.shape
    p, _, np_ = b3.shape
    nb = np_ // tn
    tt = min(tt, t)
    assert t % tt == 0 and k % tk == 0 and np_ % tn == 0

    def body(a_ref, b_ref, o_ref):
        prod = _dot_tn(a_ref[...].astype(BF16), b_ref[0].astype(BF16))

        @pl.when(pl.program_id(2) == 0)
        def _():
            o_ref[...] = prod

        @pl.when(pl.program_id(2) > 0)
        def _():
            o_ref[...] += prod

    return pl.pallas_call(
        body, grid=(k // tk, p * nb, t // tt),
        in_specs=[pl.BlockSpec((tt, tk), lambda i, j, s: (s, i)), pl.BlockSpec((1, tt, tn), lambda i, j, s: (j // nb, s, j % nb))],
        out_specs=pl.BlockSpec((tk, tn), lambda i, j, s: (i, j)),
        out_shape=jax.ShapeDtypeStruct((k, p * np_), F32),
        compiler_params=_params(), name=name)(a, b3)


def rms_fwd(h, g, *, name, tm=512):
    t, d = h.shape

    def body(h_ref, g_ref, o_ref):
        x = h_ref[...]
        r = lax.rsqrt(jnp.mean(x * x, axis=-1, keepdims=True) + EPS)
        o_ref[...] = (x * r * g_ref[...]).astype(BF16)

    return pl.pallas_call(
        body, grid=(t // tm,),
        in_specs=[pl.BlockSpec((tm, d), lambda i: (i, 0)), pl.BlockSpec((1, d), lambda i: (0, 0))],
        out_specs=pl.BlockSpec((tm, d), lambda i: (i, 0)),
        out_shape=jax.ShapeDtypeStruct((t, d), BF16), compiler_params=_params(), name=name)(h, g)


def rms_bwd(dxn, h, g, dres, *, name, tm=512):
    t, d = h.shape

    def body(dxn_ref, h_ref, g_ref, dres_ref, dh_ref, dg_ref):
        x = h_ref[...]
        dy = dxn_ref[...]
        r = lax.rsqrt(jnp.mean(x * x, axis=-1, keepdims=True) + EPS)
        gy = dy * g_ref[...]
        dot = jnp.mean(gy * x, axis=-1, keepdims=True)
        dh_ref[...] = dres_ref[...] + r * gy - x * (r * r * r * dot)
        part = jnp.sum(dy * x * r, axis=0, keepdims=True)

        @pl.when(pl.program_id(0) == 0)
        def _():
            dg_ref[...] = part

        @pl.when(pl.program_id(0) > 0)
        def _():
            dg_ref[...] += part

    row = pl.BlockSpec((tm, d), lambda i: (i, 0))
    vec = pl.BlockSpec((1, d), lambda i: (0, 0))
    return pl.pallas_call(
        body, grid=(t // tm,), in_specs=[row, row, vec, row], out_specs=[row, vec],
        out_shape=[jax.ShapeDtypeStruct((t, d), F32), jax.ShapeDtypeStruct((1, d), F32)],
        compiler_params=_params(), name=name)(dxn, h, g, dres)


def _causal3(x, w):
    return w[0:1] * pltpu.roll(x, 2, 0) + w[1:2] * pltpu.roll(x, 1, 0) + w[2:3] * x


def _anticausal3(z, w):
    n = z.shape[0]
    return w[2:3] * z + w[1:2] * pltpu.roll(z, n - 1, 0) + w[0:1] * pltpu.roll(z, n - 2, 0)


def _prev_spec(part, tm, tc, nrow8):
    del nrow8
    return pl.BlockSpec((1, HALO, tc), lambda j, i: (part, jnp.maximum(i * (tm // HALO) - 1, 0), j))


def _next_spec(part, tm, tc, nrow8):
    return pl.BlockSpec((1, HALO, tc), lambda j, i: (part, jnp.minimum((i + 1) * (tm // HALO), nrow8 - 1), j))


def _tile_spec(part, tm, tc):
    return pl.BlockSpec((1, tm, tc), lambda j, i: (part, i, j))


def _shifted_rows(x_ext, tm):
    x2 = pltpu.roll(x_ext, 2, 0)[HALO:HALO + tm]
    x1 = pltpu.roll(x_ext, 1, 0)[HALO:HALO + tm]
    x0 = x_ext[HALO:HALO + tm]
    return x2, x1, x0


def _acc_rows(ref, val, first):
    @pl.when(first)
    def _():
        ref[...] = val

    @pl.when(jnp.logical_not(first))
    def _():
        ref[...] += val


def ffn_act_fwd(up2, cw2, *, name, tm=512, tc=1408):
    _, t, f = up2.shape
    n8 = t // HALO

    def body(g_ref, v_ref, gp_ref, vp_ref, wg_ref, wv_ref, o_ref):
        first = pl.program_id(1) == 0
        keep = jnp.where(first, 0.0, 1.0)
        g_ext = jnp.concatenate([gp_ref[0] * keep, g_ref[0]], axis=0)
        v_ext = jnp.concatenate([vp_ref[0] * keep, v_ref[0]], axis=0)
        ug = _causal3(g_ext, wg_ref[0])[HALO:]
        uv = _causal3(v_ext, wv_ref[0])[HALO:]
        o_ref[...] = (ug * jax.nn.sigmoid(ug) * uv).astype(BF16)

    wspec = lambda part: pl.BlockSpec((1, 3, tc), lambda j, i: (part, 0, j))
    return pl.pallas_call(
        body, grid=(f // tc, t // tm),
        in_specs=[_tile_spec(0, tm, tc), _tile_spec(1, tm, tc), _prev_spec(0, tm, tc, n8), _prev_spec(1, tm, tc, n8),
                  wspec(0), wspec(1)],
        out_specs=pl.BlockSpec((tm, tc), lambda j, i: (i, j)),
        out_shape=jax.ShapeDtypeStruct((t, f), BF16), compiler_params=_params(), name=name)(up2, up2, up2, up2, cw2, cw2)


def ffn_act_bwd(dact, up2, cw2, *, name, tm=256, tc=1408):
    _, t, f = up2.shape
    n8 = t // HALO

    def body(d_ref, dn_ref, g_ref, v_ref, gp_ref, vp_ref, gn_ref, vn_ref, wg_ref, wv_ref, dup_ref, dw_ref):
        i = pl.program_id(1)
        first = i == 0
        keep_p = jnp.where(first, 0.0, 1.0)
        keep_n = jnp.where(i == pl.num_programs(1) - 1, 0.0, 1.0)
        wg = wg_ref[0]
        wv = wv_ref[0]
        g_ext = jnp.concatenate([gp_ref[0] * keep_p, g_ref[0], gn_ref[0]], axis=0)
        v_ext = jnp.concatenate([vp_ref[0] * keep_p, v_ref[0], vn_ref[0]], axis=0)
        d_ext = jnp.concatenate([d_ref[...], dn_ref[...] * keep_n], axis=0)
        ug = _causal3(g_ext, wg)[HALO:]
        uv = _causal3(v_ext, wv)[HALO:]
        s = jax.nn.sigmoid(ug)
        dg = d_ext * uv * (s * (1.0 + ug * (1.0 - s)))
        dv = d_ext * (ug * s)
        dup_ref[0] = _anticausal3(dg, wg)[:tm].astype(BF16)
        dup_ref[1] = _anticausal3(dv, wv)[:tm].astype(BF16)
        g2, g1, g0 = _shifted_rows(g_ext, tm)
        v2, v1, v0 = _shifted_rows(v_ext, tm)
        dgt = dg[:tm]
        dvt = dv[:tm]
        zero = jnp.zeros((HALO - 3, tc), F32)
        rows_g = [jnp.sum(dgt * x, axis=0, keepdims=True) for x in (g2, g1, g0)] + [zero]
        rows_v = [jnp.sum(dvt * x, axis=0, keepdims=True) for x in (v2, v1, v0)] + [zero]
        _acc_rows(dw_ref, jnp.stack([jnp.concatenate(rows_g, axis=0), jnp.concatenate(rows_v, axis=0)]), first)

    wspec = lambda part: pl.BlockSpec((1, 3, tc), lambda j, i: (part, 0, j))
    return pl.pallas_call(
        body, grid=(f // tc, t // tm),
        in_specs=[pl.BlockSpec((tm, tc), lambda j, i: (i, j)),
                  pl.BlockSpec((HALO, tc), lambda j, i: (jnp.minimum((i + 1) * (tm // HALO), n8 - 1), j)),
                  _tile_spec(0, tm, tc), _tile_spec(1, tm, tc), _prev_spec(0, tm, tc, n8), _prev_spec(1, tm, tc, n8),
                  _next_spec(0, tm, tc, n8), _next_spec(1, tm, tc, n8), wspec(0), wspec(1)],
        out_specs=[pl.BlockSpec((2, tm, tc), lambda j, i: (0, i, j)), pl.BlockSpec((2, HALO, tc), lambda j, i: (0, 0, j))],
        out_shape=[jax.ShapeDtypeStruct((2, t, f), BF16), jax.ShapeDtypeStruct((2, HALO, f), F32)],
        compiler_params=_params(), name=name)(dact, dact, up2, up2, up2, up2, up2, up2, cw2, cw2)


def conv_mix_fwd(proj3, ck, *, name, tm=512, tc=512):
    _, t, c = proj3.shape
    n8 = t // HALO

    def body(b_ref, c_ref, u_ref, cp_ref, up_ref, w_ref, o_ref):
        keep = jnp.where(pl.program_id(1) == 0, 0.0, 1.0)
        cu_ext = jnp.concatenate([cp_ref[0] * up_ref[0] * keep, c_ref[0] * u_ref[0]], axis=0)
        o_ref[...] = (b_ref[0] * _causal3(cu_ext, w_ref[0])[HALO:]).astype(BF16)

    return pl.pallas_call(
        body, grid=(c // tc, t // tm),
        in_specs=[_tile_spec(0, tm, tc), _tile_spec(1, tm, tc), _tile_spec(2, tm, tc), _prev_spec(1, tm, tc, n8),
                  _prev_spec(2, tm, tc, n8), pl.BlockSpec((1, 3, tc), lambda j, i: (0, 0, j))],
        out_specs=pl.BlockSpec((tm, tc), lambda j, i: (i, j)),
        out_shape=jax.ShapeDtypeStruct((t, c), BF16), compiler_params=_params(), name=name)(proj3, proj3, proj3, proj3, proj3, ck)


def conv_mix_bwd(dy, proj3, ck, *, name, tm=512, tc=512):
    _, t, c = proj3.shape
    n8 = t // HALO

    def body(d_ref, dn_ref, b_ref, c_ref, u_ref, cp_ref, up_ref, bn_ref, w_ref, dp_ref, dw_ref):
        i = pl.program_id(1)
        first = i == 0
        keep_p = jnp.where(first, 0.0, 1.0)
        keep_n = jnp.where(i == pl.num_programs(1) - 1, 0.0, 1.0)
        w = w_ref[0]
        cu_ext = jnp.concatenate([cp_ref[0] * up_ref[0] * keep_p, c_ref[0] * u_ref[0]], axis=0)
        cv = _causal3(cu_ext, w)[HALO:]
        dyt = d_ref[...]
        d_ext = jnp.concatenate([dyt, dn_ref[...] * keep_n], axis=0)
        b_ext = jnp.concatenate([b_ref[0], bn_ref[0]], axis=0)
        dcv = d_ext * b_ext
        dcu = _anticausal3(dcv, w)[:tm]
        dp_ref[0] = (dyt * cv).astype(BF16)
        dp_ref[1] = (dcu * u_ref[0]).astype(BF16)
        dp_ref[2] = (dcu * c_ref[0]).astype(BF16)
        x2, x1, x0 = _shifted_rows(cu_ext, tm)
        dcvt = dcv[:tm]
        rows = [jnp.sum(dcvt * x, axis=0, keepdims=True) for x in (x2, x1, x0)] + [jnp.zeros((HALO - 3, tc), F32)]
        _acc_rows(dw_ref, jnp.concatenate(rows, axis=0)[None], first)

    return pl.pallas_call(
        body, grid=(c // tc, t // tm),
        in_specs=[pl.BlockSpec((tm, tc), lambda j, i: (i, j)),
                  pl.BlockSpec((HALO, tc), lambda j, i: (jnp.minimum((i + 1) * (tm // HALO), n8 - 1), j)),
                  _tile_spec(0, tm, tc), _tile_spec(1, tm, tc), _tile_spec(2, tm, tc),
                  _prev_spec(1, tm, tc, n8), _prev_spec(2, tm, tc, n8),
                  _next_spec(0, tm, tc, n8), pl.BlockSpec((1, 3, tc), lambda j, i: (0, 0, j))],
        out_specs=[pl.BlockSpec((3, tm, tc), lambda j, i: (0, i, j)), pl.BlockSpec((1, HALO, tc), lambda j, i: (0, 0, j))],
        out_shape=[jax.ShapeDtypeStruct((3, t, c), BF16), jax.ShapeDtypeStruct((1, HALO, c), F32)],
        compiler_params=_params(), name=name)(dy, dy, proj3, proj3, proj3, proj3, proj3, proj3, ck)


def _head_sums(x, bd):
    hi, lo = _split2(x)
    return _dot(hi, bd) + _dot(lo, bd)


def attn_prep_fwd(proj, gq, gk, fbias, bd, *, name, tm=256):
    t = proj.shape[0]

    def body(q_ref, k_ref, v_ref, f_ref, gq_ref, gk_ref, fb_ref, bd_ref, qs_ref, kn_ref, vb_ref, lf_ref):
        bd = bd_ref[...]

        def headnorm(x_ref, g_ref, o_ref, scale):
            for c in range(MIX // 128):
                sl = slice(128 * c, 128 * (c + 1))
                x = x_ref[:, sl]
                r = lax.rsqrt(_head_sums(x * x, bd) * (1.0 / HEAD_DIM) + EPS)
                o_ref[:, sl] = (x * r * (g_ref[:, sl] * scale)).astype(BF16)

        headnorm(q_ref, gq_ref, qs_ref, SCALE)
        headnorm(k_ref, gk_ref, kn_ref, 1.0)
        vb_ref[...] = v_ref[...].astype(BF16)
        fl = f_ref[...] + fb_ref[...]
        logf = jnp.minimum(fl, 0.0) - jnp.log(1.0 + jnp.exp(-jnp.abs(fl)))
        lf_ref[...] = logf.T[0:H_FOX, :]

    col = lambda c: pl.BlockSpec((tm, MIX), lambda i: (i, c))
    vec = pl.BlockSpec((1, MIX), lambda i: (0, 0))
    out = pl.BlockSpec((tm, MIX), lambda i: (i, 0))
    return pl.pallas_call(
        body, grid=(t // tm,),
        in_specs=[col(0), col(1), col(2), pl.BlockSpec((tm, 128), lambda i: (i, 3 * MIX // 128)), vec, vec,
                  pl.BlockSpec((1, 128), lambda i: (0, 0)), pl.BlockSpec((128, 128), lambda i: (0, 0))],
        out_specs=[out, out, out, pl.BlockSpec((H_FOX, tm), lambda i: (0, i))],
        out_shape=[jax.ShapeDtypeStruct((t, MIX), BF16)] * 3 + [jax.ShapeDtypeStruct((H_FOX, t), F32)],
        compiler_params=_params(), name=name)(proj, proj, proj, proj, gq, gk, fbias, bd)


def attn_prep_bwd(proj, dqs, dkn, dv, dfl, gq, gk, bd, *, name, tm=256):
    t = proj.shape[0]

    def body(q_ref, k_ref, dq_ref, dk_ref, dv_ref, dfl_ref, gq_ref, gk_ref, bd_ref, dp_ref, dgq_ref, dgk_ref):
        bd = bd_ref[...]
        first = pl.program_id(0) == 0

        def back(x_ref, d_ref, g_ref, col0, scale, dg_ref):
            parts = []
            for c in range(MIX // 128):
                sl = slice(128 * c, 128 * (c + 1))
                x = x_ref[:, sl]
                r = lax.rsqrt(_head_sums(x * x, bd) * (1.0 / HEAD_DIM) + EPS)
                dn = d_ref[:, sl] * scale
                gy = dn * g_ref[:, sl]
                hs = _head_sums(gy * x, bd) * (1.0 / HEAD_DIM)
                dp_ref[:, col0 + 128 * c:col0 + 128 * (c + 1)] = (r * gy - x * (r * r * r * hs)).astype(BF16)
                parts.append(jnp.sum(dn * x * r, axis=0, keepdims=True))
            _acc_rows(dg_ref, jnp.concatenate(parts, axis=1), first)

        back(q_ref, dq_ref, gq_ref, 0, SCALE, dgq_ref)
        back(k_ref, dk_ref, gk_ref, MIX, 1.0, dgk_ref)
        dp_ref[:, 2 * MIX:3 * MIX] = dv_ref[...].astype(BF16)
        dp_ref[:, 3 * MIX:] = dfl_ref[...]

    col = lambda c: pl.BlockSpec((tm, MIX), lambda i: (i, c))
    row = pl.BlockSpec((tm, MIX), lambda i: (i, 0))
    vec = pl.BlockSpec((1, MIX), lambda i: (0, 0))
    return pl.pallas_call(
        body, grid=(t // tm,),
        in_specs=[col(0), col(1), row, row, row, pl.BlockSpec((tm, 128), lambda i: (i, 0)), vec, vec,
                  pl.BlockSpec((128, 128), lambda i: (0, 0))],
        out_specs=[pl.BlockSpec((tm, ATTN_IN_PAD), lambda i: (i, 0)), vec, vec],
        out_shape=[jax.ShapeDtypeStruct((t, ATTN_IN_PAD), BF16), jax.ShapeDtypeStruct((1, MIX), F32),
                   jax.ShapeDtypeStruct((1, MIX), F32)],
        compiler_params=_params(), name=name)(proj, proj, dqs, dkn, dv, dfl, gq, gk, bd)


def gate_cumsum(logf3, tri, *, name):
    nc, r, _ = logf3.shape

    def body(x_ref, tri_ref, o_ref):
        tri_m = tri_ref[...]

        def step(c, carry):
            hi, mid, lo = _split3(x_ref[c])
            cs = _dot(hi, tri_m) + _dot(mid, tri_m) + _dot(lo, tri_m) + carry
            o_ref[c] = cs
            return cs[:, 127:128]

        lax.fori_loop(0, nc, step, jnp.zeros((r, 1), F32))

    return pl.pallas_call(body, out_shape=jax.ShapeDtypeStruct(logf3.shape, F32), compiler_params=_params(),
                          name=name)(logf3, tri)


def gate_cumsum_bwd(dcum3, logf3, tri, *, name):
    nc, r, _ = dcum3.shape

    def body(x_ref, lf_ref, tri_ref, o_ref, s_ref):
        tri_m = tri_ref[...]

        def step(n, carry):
            car, tot = carry
            c = nc - 1 - n
            hi, mid, lo = _split3(x_ref[c])
            cs = _dot(hi, tri_m) + _dot(mid, tri_m) + _dot(lo, tri_m) + car
            dl = cs * (1.0 - jnp.exp(lf_ref[c]))
            o_ref[c] = dl
            return cs[:, 0:1], tot + dl

        _, tot = lax.fori_loop(0, nc, step, (jnp.zeros((r, 1), F32), jnp.zeros((r, 128), F32)))
        s_ref[...] = jnp.broadcast_to(jnp.sum(tot, axis=1, keepdims=True), tot.shape)

    return pl.pallas_call(body, out_shape=[jax.ShapeDtypeStruct(dcum3.shape, F32), jax.ShapeDtypeStruct((r, 128), F32)],
                          compiler_params=_params(), name=name)(dcum3, logf3, tri)


def _causal_iota():
    row = lax.broadcasted_iota(jnp.int32, (BQ, BQ), 0)
    col = lax.broadcasted_iota(jnp.int32, (BQ, BQ), 1)
    return row, col


def _head_specs(nj, head0):
    qin = pl.BlockSpec((1, BQ, HEAD_DIM), lambda h, i: (h + head0, i, 0))
    kin = pl.BlockSpec((1, nj, BQ, HEAD_DIM), lambda h, i: (h + head0, 0, 0, 0))
    qin2 = pl.BlockSpec((1, BQ, 2 * HEAD_DIM), lambda h, i: (h + head0, i, 0))
    kin2 = pl.BlockSpec((1, nj, BQ, 2 * HEAD_DIM), lambda h, i: (h + head0, 0, 0, 0))
    qspec = pl.BlockSpec((1, BQ, HEAD_DIM), lambda h, i: (h, i, 0))
    kspec2 = pl.BlockSpec((1, nj, BQ, 2 * HEAD_DIM), lambda h, i: (h, 0, 0, 0))
    return qin, kin, qin2, kin2, qspec, kspec2


STOP = -105.0
STOP_WIDE = -115.0
FIXED_REF_MAX = 40.0


def _store_kmax(k_ref, kmax_ref, nj):
    def step(j, mx):
        kf = k_ref[0, j].astype(F32)
        return jnp.maximum(mx, jnp.max(jnp.sum(kf * kf, axis=1, keepdims=True), axis=0, keepdims=True))

    mx = lax.fori_loop(0, nj, step, jnp.zeros((1, 1), F32))
    kmax_ref[...] = jnp.broadcast_to(jnp.sqrt(mx), kmax_ref.shape)


def _qk_bound(q, kmax_ref):
    qf = q.astype(F32)
    return jnp.sqrt(jnp.sum(qf * qf, axis=1, keepdims=True)) * kmax_ref[0:1, 0:1] * 1.001


def _first_and_last_step():
    h, i = pl.program_id(0), pl.program_id(1)
    first = jnp.logical_and(h == 0, i == 0)
    last = jnp.logical_and(h == pl.num_programs(0) - 1, i == pl.num_programs(1) - 1)
    return first, last


def fox_fwd(qs, kn4, va4, fcol, frow4, *, name, gather=None):
    _, t, dh = qs.shape
    nh = H_FOX
    nj = t // BQ

    def body(*refs):
        if gather is None:
            q_ref, k_ref, v_ref, fc_ref, fr_ref, o_ref, lse_ref, kmax_ref = refs
        else:
            q_ref, k_ref, v_ref, fc_ref, fr_ref, src_ref, o_ref, lse_ref, dst_ref, kmax_ref = refs[:10]
            first_step, last_step = _first_and_last_step()

            @pl.when(first_step)
            def _():
                _chip_gather(src_ref, dst_ref, *refs[10:])[0]()

        i = pl.program_id(1)

        @pl.when(i == 0)
        def _():
            _store_kmax(k_ref, kmax_ref, nj)

        q = q_ref[0]
        fq = fc_ref[0]
        bound = _qk_bound(q, kmax_ref)
        row, col = _causal_iota()

        def gate_at_block_end(j):
            return fr_ref[0, j][:, BQ - 1:BQ]

        def pv(p, j):
            p_hi, p_lo = _split2(p)
            return _dot(p_hi, v_ref[0, j]) + _dot(p_lo, v_ref[0, j])

        def walk(block, live, init):
            carry = block(i, init, True)

            def cond(c):
                n, carry = c
                return jnp.logical_and(n < i, live(jnp.maximum(i - 1 - n, 0), carry))

            _, carry = lax.while_loop(cond, lambda c: (c[0] + 1, block(i - 1 - c[0], c[1], False)), (0, carry))
            return carry

        def fixed_reference(_):
            shift = fq - bound

            def block(j, acc, diag):
                p = jnp.exp(_dot_nt(q, k_ref[0, j]) + shift - fr_ref[0, j])
                if diag:
                    p = jnp.where(col <= row, p, 0.0)
                return acc + pv(p, j)

            def live(j, acc):
                return jnp.max(fq - gate_at_block_end(j) - jnp.log(acc[:, dh:dh + 1])) >= STOP_WIDE

            acc = walk(block, live, jnp.zeros((BQ, 2 * dh), F32))
            l = acc[:, dh:dh + 1]
            return acc[:, :dh] / l, bound + jnp.log(l)

        def running_maximum(_):
            def block(j, carry, diag):
                m, acc = carry
                s = _dot_nt(q, k_ref[0, j]) + fq - fr_ref[0, j]
                if diag:
                    s = jnp.where(col <= row, s, NEG)
                m_new = jnp.maximum(m, jnp.max(s, axis=1, keepdims=True))
                return m_new, jnp.exp(m - m_new) * acc + pv(jnp.exp(s - m_new), j)

            def live(j, carry):
                return jnp.max(bound + fq - gate_at_block_end(j) - carry[0]) >= STOP

            m, acc = walk(block, live, (jnp.full((BQ, 1), NEG, F32), jnp.zeros((BQ, 2 * dh), F32)))
            l = acc[:, dh:dh + 1]
            return acc[:, :dh] / l, m + jnp.log(l)

        o, lse = lax.cond(jnp.max(bound) < FIXED_REF_MAX, fixed_reference, running_maximum, 0)
        o_ref[0] = o
        lse_ref[0] = lse

        if gather is not None:
            @pl.when(last_step)
            def _():
                _chip_gather(src_ref, dst_ref, *refs[10:])[1]()

    qin, kin, _, kin2, qspec, _ = _head_specs(nj, 0)
    cspec = pl.BlockSpec((1, BQ, 1), lambda h, i: (h, i, 0))
    in_specs = [qin, kin, kin2, cspec, pl.BlockSpec((1, nj, 1, BQ), lambda h, i: (h, 0, 0, 0))]
    out_specs = [qspec, cspec]
    out_shape = [jax.ShapeDtypeStruct((nh, t, dh), F32), jax.ShapeDtypeStruct((nh, t, 1), F32)]
    scratch = [pltpu.VMEM((8, 128), F32)]
    args = [qs, kn4, va4, fcol, frow4]
    if gather is not None:
        in_specs.append(_ANY)
        out_specs.append(_ANY)
        out_shape.append(jax.ShapeDtypeStruct((N_CHIPS,) + gather.shape, gather.dtype))
        scratch += _chip_sems()
        args.append(gather)
    return pl.pallas_call(body, grid=(nh, nj), in_specs=in_specs, out_specs=out_specs, out_shape=out_shape,
                          scratch_shapes=scratch, compiler_params=_params(), name=name)(*args)


def fox_bwd(qs, kn4, v4, qa, doa, fcol, frow4, o, do, lse, *, name, scatter=None):
    _, t, dh = qs.shape
    nh = H_FOX
    nj = t // BQ

    def body(*refs):
        q_ref, k_ref, v_ref, qa_ref, doa_ref, fc_ref, fr_ref, o_ref, do_ref, lse_ref = refs[:10]
        if scatter is None:
            dq_ref, dkv_ref, dfk_ref, kmax_ref = refs[10:]
        else:
            g_ref, dq_ref, dkv_ref, dfk_ref, land_ref, kmax_ref = refs[10:16]
            first_step, last_step = _first_and_last_step()

            @pl.when(first_step)
            def _():
                _chip_scatter(g_ref, land_ref, *refs[16:])[0]()

        i = pl.program_id(1)

        @pl.when(i == 0)
        def _():
            dkv_ref[...] = jnp.zeros_like(dkv_ref)
            dfk_ref[...] = jnp.zeros_like(dfk_ref)
            _store_kmax(k_ref, kmax_ref, nj)

        q = q_ref[0]
        do_b = do_ref[0]
        fq = fc_ref[0]
        lse_q = lse_ref[0]
        dd = jnp.sum(do_b.astype(F32) * o_ref[0], axis=1, keepdims=True)
        rhs = jnp.concatenate([qa_ref[0], doa_ref[0]], axis=0)
        edge = _qk_bound(q, kmax_ref) + fq - lse_q

        def negligible(j):
            return jnp.logical_and(j < i, jnp.max(edge - fr_ref[0, j][:, BQ - 1:BQ]) < STOP_WIDE)

        first = lax.while_loop(negligible, lambda j: j + 1, 0)

        def block(j, dq, diag):
            k = k_ref[0, j]
            p = jnp.exp(_dot_nt(q, k) + fq - fr_ref[0, j] - lse_q)
            if diag:
                row, col = _causal_iota()
                p = jnp.where(col <= row, p, 0.0)
            ds = p * (_dot_nt(do_b, v_ref[0, j]) - dd)
            ds_b = ds.astype(BF16)
            dkv_ref[0, j] += _dot_tn(jnp.concatenate([ds_b, p.astype(BF16)], axis=0), rhs)
            dfk_ref[0, j] -= jnp.sum(ds, axis=0, keepdims=True)
            return dq + _dot(ds_b, k)

        dq = lax.fori_loop(first, i, lambda j, c: block(j, c, False), jnp.zeros((BQ, dh), F32))
        dq_ref[0] = block(i, dq, True)

        if scatter is not None:
            @pl.when(last_step)
            def _():
                _chip_scatter(g_ref, land_ref, *refs[16:])[1]()

    qin, kin, qin2, _, qspec, kspec2 = _head_specs(nj, 0)
    cspec = pl.BlockSpec((1, BQ, 1), lambda h, i: (h, i, 0))
    rspec = pl.BlockSpec((1, nj, 1, BQ), lambda h, i: (h, 0, 0, 0))
    in_specs = [qin, kin, kin, qin2, qin2, cspec, rspec, qspec, qin, cspec]
    out_specs = [qspec, kspec2, rspec]
    out_shape = [jax.ShapeDtypeStruct((nh, t, dh), F32), jax.ShapeDtypeStruct((nh, nj, BQ, 2 * dh), F32),
                 jax.ShapeDtypeStruct((nh, nj, 1, BQ), F32)]
    scratch = [pltpu.VMEM((8, 128), F32)]
    args = [qs, kn4, v4, qa, doa, fcol, frow4, o, do, lse]
    if scatter is not None:
        in_specs.append(_ANY)
        out_specs.append(_ANY)
        out_shape.append(jax.ShapeDtypeStruct(scatter.shape, scatter.dtype))
        scratch += _chip_sems()
        args.append(scatter)
    return pl.pallas_call(body, grid=(nh, nj), in_specs=in_specs, out_specs=out_specs, out_shape=out_shape,
                          scratch_shapes=scratch, compiler_params=_params(), name=name)(*args)


def _sb_logs(z, diag):
    e = jnp.exp(-jnp.abs(z))
    sp = jnp.log(1.0 + e)
    logb = jnp.minimum(z, 0.0) - sp
    lom = -jnp.maximum(z, 0.0) - sp
    strict = None
    if diag:
        row, col = _causal_iota()
        strict = col < row
        lom = jnp.where(strict, lom, 0.0)
    return logb, lom, e, strict


def sb_fwd(qs, kn4, va4, tri, *, name):
    _, t, dh = qs.shape
    nh = H_SB
    nj = t // BQ
    assert nj <= 128

    def body(q_ref, k_ref, v_ref, tri_ref, o_ref, rs_ref):
        i = pl.program_id(1)
        q = q_ref[0]
        tri_m = tri_ref[...]
        lane = lax.broadcasted_iota(jnp.int32, (BQ, 128), 1)

        def block(j, carry, diag):
            run, acc, rall = carry
            logb, lom, _, strict = _sb_logs(_dot_nt(q, k_ref[0, j]), diag)
            hi, lo = _split2(lom)
            w = jnp.exp(logb + (_dot(hi, tri_m) + _dot(lo, tri_m)) + run)
            if diag:
                w = jnp.where(strict, w, 0.0)
            acc = acc + _dot(w.astype(BF16), v_ref[0, j])
            rall = jnp.where(lane == j, run, rall)
            return run + jnp.sum(lom, axis=1, keepdims=True), acc, rall

        init = (jnp.zeros((BQ, 1), F32), jnp.zeros((BQ, 2 * dh), F32), jnp.full((BQ, 128), NEG, F32))
        carry = block(i, init, True)

        def cond(c):
            n, carry = c
            return jnp.logical_and(n < i, jnp.max(carry[0]) >= STOP)

        _, (_, acc, rall) = lax.while_loop(cond, lambda c: (c[0] + 1, block(i - 1 - c[0], c[1], False)), (0, carry))
        o_ref[0] = acc[:, :dh].astype(BF16)
        rs_ref[0] = rall

    qin, kin, _, kin2, qspec, _ = _head_specs(nj, H_FOX)
    rspec = pl.BlockSpec((1, BQ, 128), lambda h, i: (h, i, 0))
    return pl.pallas_call(
        body, grid=(nh, nj),
        in_specs=[qin, kin, kin2, pl.BlockSpec((BQ, BQ), lambda h, i: (0, 0))],
        out_specs=[qspec, rspec],
        out_shape=[jax.ShapeDtypeStruct((nh, t, dh), BF16), jax.ShapeDtypeStruct((nh, t, 128), F32)],
        compiler_params=_params(), name=name)(qs, kn4, va4, tri)


def sb_bwd(qs, kn4, v4, qa, doa, tri, do, rsave, *, name):
    _, t, dh = qs.shape
    nh = H_SB
    nj = t // BQ

    def body(q_ref, k_ref, v_ref, qa_ref, doa_ref, tri_ref, do_ref, rs_ref, dq_ref, dkv_ref):
        i = pl.program_id(1)

        @pl.when(i == 0)
        def _():
            dkv_ref[...] = jnp.zeros_like(dkv_ref)

        q = q_ref[0]
        do_b = do_ref[0]
        tri_m = tri_ref[...]
        rall = rs_ref[0]
        lane = lax.broadcasted_iota(jnp.int32, (BQ, 128), 1)
        rhs = jnp.concatenate([qa_ref[0], doa_ref[0]], axis=0)
        lane1 = lax.broadcasted_iota(jnp.int32, (1, 128), 1)
        unvisited = jnp.logical_and(lane1 < i, jnp.max(rall, axis=0, keepdims=True) < STOP)
        first = jnp.sum(unvisited.astype(jnp.int32))

        def block(j, carry, diag):
            dq, ecar = carry
            k = k_ref[0, j]
            z = _dot_nt(q, k)
            logb, lom, e, strict = _sb_logs(z, diag)
            hi, lo = _split2(lom)
            run = jnp.sum(jnp.where(lane == j, rall, 0.0), axis=1, keepdims=True)
            w = jnp.exp(logb + (_dot(hi, tri_m) + _dot(lo, tri_m)) + run)
            if diag:
                w = jnp.where(strict, w, 0.0)
            da = w * _dot_nt(do_b, v_ref[0, j])
            before = _dot_nt(da.astype(BF16), tri_m) + ecar
            inv = 1.0 / (1.0 + e)
            beta = jnp.where(z >= 0.0, 1.0, e) * inv
            one_minus = jnp.where(z >= 0.0, e, 1.0) * inv
            dz = da * one_minus - before * beta
            if diag:
                dz = jnp.where(strict, dz, 0.0)
            dz_b = dz.astype(BF16)
            dkv_ref[0, j] += _dot_tn(jnp.concatenate([dz_b, w.astype(BF16)], axis=0), rhs)
            return dq + _dot(dz_b, k), ecar + jnp.sum(da, axis=1, keepdims=True)

        carry = lax.fori_loop(first, i, lambda j, c: block(j, c, False),
                              (jnp.zeros((BQ, dh), F32), jnp.zeros((BQ, 1), F32)))
        dq, _ = block(i, carry, True)
        dq_ref[0] = dq

    qin, kin, qin2, _, qspec, kspec2 = _head_specs(nj, H_FOX)
    return pl.pallas_call(
        body, grid=(nh, nj),
        in_specs=[qin, kin, kin, qin2, qin2, pl.BlockSpec((BQ, BQ), lambda h, i: (0, 0)), qin,
                  pl.BlockSpec((1, BQ, 128), lambda h, i: (h, i, 0))],
        out_specs=[qspec, kspec2],
        out_shape=[jax.ShapeDtypeStruct((nh, t, dh), F32), jax.ShapeDtypeStruct((nh, nj, BQ, 2 * dh), F32)],
        compiler_params=_params(), name=name)(qs, kn4, v4, qa, doa, tri, do, rsave)


def loss_head(y, target, *, name, tm=512):
    t, d = y.shape

    def body(y_ref, t_ref, l_ref, dy_ref, acc_ref):
        i = pl.program_id(0)
        diff = y_ref[...] - t_ref[...]
        dy_ref[...] = diff * (1.0 / d)
        part = jnp.sum(diff * diff, axis=0, keepdims=True)

        @pl.when(i == 0)
        def _():
            acc_ref[...] = part

        @pl.when(i > 0)
        def _():
            acc_ref[...] += part

        @pl.when(i == pl.num_programs(0) - 1)
        def _():
            l_ref[...] = jnp.full(l_ref.shape, (0.5 / d) * jnp.sum(acc_ref[...]), F32)

    row = pl.BlockSpec((tm, d), lambda i: (i, 0))
    return pl.pallas_call(
        body, grid=(t // tm,), in_specs=[row, row],
        out_specs=[pl.BlockSpec((8, 128), lambda i: (0, 0)), row],
        out_shape=[jax.ShapeDtypeStruct((8, 128), F32), jax.ShapeDtypeStruct((t, d), F32)],
        scratch_shapes=[pltpu.VMEM((1, d), F32)], compiler_params=_params(), name=name)(y, target)


def _to_heads(a):
    t = a.shape[0]
    return a.reshape(t, N_HEADS, HEAD_DIM).transpose(1, 0, 2)


def _from_heads(a):
    t = a.shape[1]
    return a.transpose(1, 0, 2).reshape(t, MIX)


def _lanes_to_chunks(a):
    r, t = a.shape
    return a.reshape(r, t // 128, 128).transpose(1, 0, 2)


def _chunks_to_lanes(a):
    nc, r, _ = a.shape
    return a.transpose(1, 0, 2).reshape(r, nc * 128)


def _constants():
    idx = jnp.arange(128)
    bd = (idx[:, None] // HEAD_DIM == idx[None, :] // HEAD_DIM).astype(BF16)
    tri_le = (idx[:, None] <= idx[None, :]).astype(BF16)
    tri_ge = (idx[:, None] >= idx[None, :]).astype(BF16)
    jdx = jnp.arange(BQ)
    tri_gt = (jdx[:, None] > jdx[None, :]).astype(BF16)
    return dict(bd=bd, tri_le=tri_le, tri_ge=tri_ge, tri_gt=tri_gt)


def attn_layer_fwd(h, w, cst, gather=None):
    t = h.shape[0]
    nj = t // BQ
    xn = rms_fwd(h, w["norm"], name="rms_fwd")
    proj = mm_nn(xn, w["w_in"], tn=640, name="attn_in_proj")
    qs, kn, vb, logf = attn_prep_fwd(proj, w["gq"], w["gk"], w["fbias"], cst["bd"], name="attn_prep_fwd")
    logf3 = _lanes_to_chunks(logf)
    cum = _chunks_to_lanes(gate_cumsum(logf3, cst["tri_le"], name="gate_cumsum"))
    fcol = cum.reshape(H_FOX, t, 1)
    frow4 = cum.reshape(H_FOX, nj, 1, BQ)
    qh = _to_heads(qs)
    kh4 = _to_heads(kn).reshape(N_HEADS, nj, BQ, HEAD_DIM)
    vh4 = _to_heads(vb).reshape(N_HEADS, nj, BQ, HEAD_DIM)
    ones = jnp.ones(vh4.shape[:-1] + (1,), BF16)
    va4 = jnp.concatenate([vh4, ones, jnp.zeros(vh4.shape[:-1] + (HEAD_DIM - 1,), BF16)], axis=-1)
    if gather is None:
        (o_f, lse), gathered = fox_fwd(qh, kh4, va4, fcol, frow4, name="fox_fwd"), None
    else:
        o_f, lse, gathered = fox_fwd(qh, kh4, va4, fcol, frow4, name="fox_fwd_gather", gather=gather)
    o_s, rsave = sb_fwd(qh, kh4, va4, cst["tri_gt"], name="sb_fwd")
    o = _from_heads(jnp.concatenate([o_f.astype(BF16), o_s], axis=0))
    h2 = mm_nn(o, w["w_out"], add=h, name="mix_out_proj")
    saved = dict(h=h, xn=xn, proj=proj, logf3=logf3, fcol=fcol, frow4=frow4, qh=qh, kh4=kh4, vh4=vh4,
                 o_f=o_f, lse=lse, rsave=rsave, o=o)
    return h2, saved, gathered


def attn_layer_bwd(dh, w, s, cst, scatter=None):
    t = dh.shape[0]
    dh3 = dh[None]
    do = mm_nt(dh3, w["w_out"], out_dtype=BF16, name="mix_out_bwd_bf16")
    g_w_out = mm_tn(s["o"], dh3, name="mix_out_wgrad")
    doh = _to_heads(do)
    zeros = jnp.zeros_like(doh)
    qa = jnp.concatenate([s["qh"], zeros], axis=-1)
    doa = jnp.concatenate([zeros, doh], axis=-1)
    fox_args = (s["qh"], s["kh4"], s["vh4"], qa, doa, s["fcol"], s["frow4"], s["o_f"], doh, s["lse"])
    if scatter is None:
        (dq_f, dkv_f, dfk), landed = fox_bwd(*fox_args, name="fox_bwd"), None
    else:
        dq_f, dkv_f, dfk, landed = fox_bwd(*fox_args, name="fox_bwd_scatter", scatter=scatter)
    dq_s, dkv_s = sb_bwd(s["qh"], s["kh4"], s["vh4"], qa, doa, cst["tri_gt"], doh, s["rsave"], name="sb_bwd")
    dqs = _from_heads(jnp.concatenate([dq_f, dq_s], axis=0))
    dkv = jnp.concatenate([dkv_f, dkv_s], axis=0).reshape(N_HEADS, t, 2 * HEAD_DIM)
    dkn = _from_heads(dkv[:, :, :HEAD_DIM])
    dv = _from_heads(dkv[:, :, HEAD_DIM:])
    dcum3 = _lanes_to_chunks(dfk.reshape(H_FOX, t))
    dfl3, dbias = gate_cumsum_bwd(dcum3, s["logf3"], cst["tri_ge"], name="gate_cumsum_bwd")
    dfl = jnp.pad(_chunks_to_lanes(dfl3).T, ((0, 0), (0, 128 - H_FOX))).astype(BF16)
    dproj, dgq, dgk = attn_prep_bwd(s["proj"], dqs, dkn, dv, dfl, w["gq"], w["gk"], cst["bd"], name="attn_prep_bwd")
    g_w_in = mm_tn(s["xn"], dproj[None], tn=640, name="attn_in_wgrad")[:, :ATTN_IN]
    dxn = mm_nt(dproj[None], w["w_in"], name="attn_in_bwd")
    dh2, g_norm = rms_bwd(dxn, s["h"], w["norm"], dh, name="rms_bwd")
    dgq = dgq.reshape(N_HEADS, HEAD_DIM)
    dgk = dgk.reshape(N_HEADS, HEAD_DIM)
    grads = dict(norm=g_norm[0], w_in=g_w_in, f_bias=dbias[:, 0], fox_q=dgq[:H_FOX].sum(0), fox_k=dgk[:H_FOX].sum(0),
                 sb_q=dgq[H_FOX:].sum(0), sb_k=dgk[H_FOX:].sum(0), w_out=g_w_out)
    return dh2, grads, landed


def conv_layer_fwd(h, w):
    xn = rms_fwd(h, w["norm"], name="rms_fwd")
    proj3 = mm_nn(xn, w["w_in"], parts=3, name="conv_in_proj")
    y = conv_mix_fwd(proj3, w["ck"], name="conv_mix_fwd")
    h2 = mm_nn(y, w["w_out"], add=h, name="mix_out_proj")
    return h2, dict(h=h, xn=xn, proj3=proj3, y=y)


def conv_layer_bwd(dh, w, s):
    dh3 = dh[None]
    dy = mm_nt(dh3, w["w_out"], name="mix_out_bwd")
    g_w_out = mm_tn(s["y"], dh3, name="mix_out_wgrad")
    dproj3, dck = conv_mix_bwd(dy, s["proj3"], w["ck"], name="conv_mix_bwd")
    g_w_in = mm_tn(s["xn"], dproj3, name="conv_in_wgrad")
    dxn = mm_nt(dproj3, w["w_in"], name="conv_in_bwd")
    dh2, g_norm = rms_bwd(dxn, s["h"], w["norm"], dh, name="rms_bwd")
    return dh2, dict(norm=g_norm[0], w_in=g_w_in, ck=dck[0, :3], w_out=g_w_out)


def ffn_layer_fwd(h, w):
    xn = rms_fwd(h, w["norm"], name="rms_fwd")
    up2 = mm_nn(xn, w["w_up"], parts=2, tn=1408, name="ffn_up_proj")
    act = ffn_act_fwd(up2, w["cw2"], name="ffn_act_fwd")
    h2 = mm_nn(act, w["w_down"], add=h, name="ffn_down_proj")
    return h2, dict(h=h, xn=xn, up2=up2, act=act)


def ffn_layer_bwd(dh, w, s):
    dh3 = dh[None]
    dact = mm_nt(dh3, w["w_down"], tn=1408, name="ffn_down_bwd")
    g_w_down = mm_tn(s["act"], dh3, tk=1408, name="ffn_down_wgrad")
    dup2, dcw = ffn_act_bwd(dact, s["up2"], w["cw2"], name="ffn_act_bwd")
    g_w_up = mm_tn(s["xn"], dup2, tn=1408, name="ffn_up_wgrad")
    dxn = mm_nt(dup2, w["w_up"], name="ffn_up_bwd")
    dh2, g_norm = rms_bwd(dxn, s["h"], w["norm"], dh, name="rms_bwd")
    g_cw = jnp.concatenate([dcw[0, :3], dcw[1, :3]], axis=1)
    return dh2, dict(norm=g_norm[0], w_up=g_w_up, cw=g_cw, w_down=g_w_down)


def forward_backward(x, target, wa, wc, wf, *, late_weights=None, late_chunks=None):
    cst = _constants()
    h = x
    saved = []
    layer = 0
    while layer == 0 or layer < len(wf):
        i = layer // 2
        if layer % 2 == 0:
            h, sm, gathered = attn_layer_fwd(h, wa[i], cst, gather=late_weights[0] if late_weights and layer == 0 else None)
            if gathered is not None:
                wa, wc, wf = late_weights[1](gathered)
        else:
            h, sm = conv_layer_fwd(h, wc[i])
        h, sf = ffn_layer_fwd(h, wf[layer])
        saved.append((sm, sf))
        layer += 1
    depth = len(wf)
    loss_blk, dh = loss_head(h, target, name="loss_head")
    ga, gc, gf = [None] * len(wa), [None] * len(wc), [None] * depth
    landed = None
    for layer in reversed(range(depth)):
        i = layer // 2
        sm, sf = saved[layer]
        dh, gf[layer] = ffn_layer_bwd(dh, wf[layer], sf)
        if layer % 2 == 0:
            chunks = late_chunks(ga, gc, gf) if late_chunks and layer == 0 else None
            dh, ga[i], got = attn_layer_bwd(dh, wa[i], sm, cst, scatter=chunks)
            landed = got if got is not None else landed
        else:
            dh, gc[i] = conv_layer_bwd(dh, wc[i], sm)
    return loss_blk, dh, ga, gc, gf, landed


def _part_rows(shape, width, row_mult):
    n = 1
    for s in shape:
        n *= s
    rows = -(-n // width)
    return -(-rows // row_mult) * row_mult


def _pack_rows(arrs, width, row_mult, dtype, total_rows=None):
    parts = []
    used = 0
    for a in arrs:
        rows = _part_rows(a.shape, width, row_mult)
        flat = a.astype(dtype).reshape(-1)
        flat = jnp.pad(flat, (0, rows * width - flat.shape[0]))
        parts.append(flat.reshape(rows, width))
        used += rows
    if total_rows is not None and total_rows > used:
        parts.append(jnp.zeros((total_rows - used, width), dtype))
    return jnp.concatenate(parts, axis=0)


def _unpack_rows(packed, shapes, width, row_mult):
    out = []
    off = 0
    for shape in shapes:
        rows = _part_rows(shape, width, row_mult)
        n = 1
        for s in shape:
            n *= s
        out.append(packed[off:off + rows].reshape(-1)[:n].reshape(shape))
        off += rows
    return out


BIG_NAMES = ("attn_w_in", "attn_w_out", "conv_w_in", "conv_w_out", "ffn_w_up", "ffn_w_down")
BIG_AXIS = {"attn_w_in": 2, "attn_w_out": 1, "conv_w_in": 2, "conv_w_out": 1, "ffn_w_up": 2, "ffn_w_down": 1}
BIG_WIDTH = 1024
BIG_ROW_MULT = 16
BIG_TILE = 512
SMALL_TILE = 128
SMALL_SHARDED = ("conv_norm", "conv_kernel", "ffn_conv")
SMALL_AXIS = {"conv_norm": 1, "conv_kernel": 2, "ffn_conv": 2}
SMALL_REPLICATED = ("attn_norm", "attn_f_bias", "fox_q_gain", "fox_k_gain", "sb_q_gain", "sb_k_gain", "ffn_norm")
WEIGHT_ORDER = ("attn_norm", "attn_w_in", "attn_f_bias", "fox_q_gain", "fox_k_gain", "sb_q_gain", "sb_k_gain",
                "attn_w_out", "conv_norm", "conv_w_in", "conv_kernel", "conv_w_out", "ffn_norm", "ffn_w_up",
                "ffn_conv", "ffn_w_down")


def _big_total_rows(shapes):
    used = sum(_part_rows(s, BIG_WIDTH, BIG_ROW_MULT) for s in shapes)
    tile = BIG_TILE if used >= 8 * BIG_TILE else SMALL_TILE
    return -(-used // tile) * tile


def _place():
    x, y, c = lax.axis_index("x"), lax.axis_index("y"), lax.axis_index("c")
    other_chips = [(1 - x, y), (x, 1 - y), (1 - x, 1 - y)]
    return x, y, c, other_chips


_ANY = pl.BlockSpec(memory_space=pl.ANY)


def _chip_sems():
    return [pltpu.SemaphoreType.DMA((3,)), pltpu.SemaphoreType.DMA((3,)), pltpu.SemaphoreType.DMA]


def _chip_gather(src_ref, dst_ref, send_sems, recv_sems, local_sem):
    x, y, c, chips = _place()
    k = 2 * x + y

    def copy(j, slot):
        px, py = chips[j]
        return pltpu.make_async_remote_copy(src_ref=src_ref, dst_ref=dst_ref.at[slot], send_sem=send_sems.at[j],
                                            recv_sem=recv_sems.at[j], device_id=(px, py, c), device_id_type=MESH)

    def local():
        return pltpu.make_async_copy(src_ref, dst_ref.at[k], local_sem)

    def start():
        local().start()
        for j in range(3):
            copy(j, k).start()

    def finish():
        for j, (px, py) in enumerate(chips):
            copy(j, 2 * px + py).wait_recv()
        for j in range(3):
            copy(j, k).wait_send()
        local().wait()

    return start, finish


def _chip_scatter(g_ref, o_ref, send_sems, recv_sems, local_sem):
    x, y, c, chips = _place()
    k = 2 * x + y

    def copy(j, src_slot, dst_slot):
        px, py = chips[j]
        return pltpu.make_async_remote_copy(src_ref=g_ref.at[src_slot], dst_ref=o_ref.at[dst_slot],
                                            send_sem=send_sems.at[j], recv_sem=recv_sems.at[j],
                                            device_id=(px, py, c), device_id_type=MESH)

    def local():
        return pltpu.make_async_copy(g_ref.at[k], o_ref.at[k], local_sem)

    def start():
        local().start()
        for j, (px, py) in enumerate(chips):
            copy(j, 2 * px + py, k).start()

    def finish():
        for j, (px, py) in enumerate(chips):
            copy(j, k, 2 * px + py).wait_recv()
        for j, (px, py) in enumerate(chips):
            copy(j, 2 * px + py, k).wait_send()
        local().wait()

    return start, finish


def gather_chips(arrs, *, name):
    n = len(arrs)

    def body(*refs):
        hooks = [_chip_gather(refs[m], refs[n + m], *refs[2 * n + 3 * m:2 * n + 3 * m + 3]) for m in range(n)]
        for start, _ in hooks:
            start()
        for _, finish in hooks:
            finish()

    return pl.pallas_call(
        body, in_specs=[_ANY] * n, out_specs=[_ANY] * n,
        out_shape=[jax.ShapeDtypeStruct((N_CHIPS,) + a.shape, a.dtype) for a in arrs],
        scratch_shapes=_chip_sems() * n, name=name)(*arrs)


def scatter_chips(chunks, *, name):
    def body(g_ref, o_ref, send_sems, recv_sems, local_sem):
        start, finish = _chip_scatter(g_ref, o_ref, send_sems, recv_sems, local_sem)
        start()
        finish()

    return pl.pallas_call(
        body, in_specs=[_ANY], out_specs=_ANY, out_shape=jax.ShapeDtypeStruct(chunks.shape, chunks.dtype),
        scratch_shapes=_chip_sems(), name=name)(chunks)


def swap_cores(arrs, *, name):
    n = len(arrs)

    def body(*refs):
        x, y, c, _ = _place()
        copies = [pltpu.make_async_remote_copy(src_ref=refs[m], dst_ref=refs[n + m], send_sem=refs[2 * n + 2 * m],
                                               recv_sem=refs[2 * n + 2 * m + 1], device_id=(x, y, 1 - c),
                                               device_id_type=MESH) for m in range(n)]
        for cp in copies:
            cp.start()
        for cp in copies:
            cp.wait()

    return pl.pallas_call(
        body, in_specs=[_ANY] * n, out_specs=[_ANY] * n,
        out_shape=[jax.ShapeDtypeStruct(a.shape, a.dtype) for a in arrs],
        scratch_shapes=[pltpu.SemaphoreType.DMA, pltpu.SemaphoreType.DMA] * n, name=name)(*arrs)


def allreduce_small(p, *, name):
    r, w = p.shape

    def body(p_ref, o_ref, buf, send_sems, recv_sems):
        x, y, c, _ = _place()
        me = 4 * x + 2 * y + c
        buf[me] = p_ref[...]

        def peer_of(m):
            return (1 - x if m & 4 else x, 1 - y if m & 2 else y, 1 - c if m & 1 else c)

        def copy(m, slot):
            return pltpu.make_async_remote_copy(src_ref=p_ref, dst_ref=buf.at[slot], send_sem=send_sems.at[m - 1],
                                                recv_sem=recv_sems.at[m - 1], device_id=peer_of(m),
                                                device_id_type=MESH)

        sends = [copy(m, me) for m in range(1, 8)]
        for cp in sends:
            cp.start()
        for m in range(1, 8):
            px, py, pc = peer_of(m)
            copy(m, 4 * px + 2 * py + pc).wait_recv()
        for cp in sends:
            cp.wait_send()
        acc = buf[0]
        for d in range(1, 8):
            acc = acc + buf[d]
        o_ref[...] = acc

    vm = pl.BlockSpec(memory_space=pltpu.VMEM)
    return pl.pallas_call(
        body, in_specs=[vm], out_specs=vm, out_shape=jax.ShapeDtypeStruct((r, w), F32),
        scratch_shapes=[pltpu.VMEM((8, r, w), F32), pltpu.SemaphoreType.DMA((7,)), pltpu.SemaphoreType.DMA((7,))],
        name=name)(p)


def sum_chips(rv, *, name):
    _, r, w = rv.shape
    tile = BIG_TILE if r % BIG_TILE == 0 else SMALL_TILE

    def body(a_ref, b_ref, c_ref, d_ref, o_ref):
        o_ref[...] = ((a_ref[0].astype(F32) + b_ref[0].astype(F32)) + c_ref[0].astype(F32)) + d_ref[0].astype(F32)

    spec = lambda kk: pl.BlockSpec((1, tile, w), lambda i: (kk, i, 0))
    return pl.pallas_call(
        body, grid=(r // tile,), in_specs=[spec(0), spec(1), spec(2), spec(3)],
        out_specs=pl.BlockSpec((tile, w), lambda i: (i, 0)), out_shape=jax.ShapeDtypeStruct((r, w), F32),
        compiler_params=_params(), name=name)(rv, rv, rv, rv)


def add_pair(a, b, *, name):
    r, w = a.shape
    tile = BIG_TILE if r % BIG_TILE == 0 else SMALL_TILE

    def body(a_ref, b_ref, o_ref):
        o_ref[...] = a_ref[...] + b_ref[...]

    spec = pl.BlockSpec((tile, w), lambda i: (i, 0))
    return pl.pallas_call(body, grid=(r // tile,), in_specs=[spec, spec], out_specs=spec,
                          out_shape=jax.ShapeDtypeStruct((r, w), F32), compiler_params=_params(), name=name)(a, b)


def adamw(w, g, m, v, *, tm, name):
    r, c = w.shape
    assert r % tm == 0

    def body(w_ref, g_ref, m_ref, v_ref, d_ref, nm_ref, nv_ref):
        g_ = g_ref[...]
        m_ = ADAM_B1 * m_ref[...] + (1.0 - ADAM_B1) * g_
        v_ = ADAM_B2 * v_ref[...] + (1.0 - ADAM_B2) * (g_ * g_)
        m_hat = m_ / (1.0 - ADAM_B1 ** ADAM_STEP)
        v_hat = v_ / (1.0 - ADAM_B2 ** ADAM_STEP)
        d_ref[...] = -ADAM_LR * (m_hat / (jnp.sqrt(v_hat) + ADAM_EPS) + ADAM_WD * w_ref[...])
        nm_ref[...] = m_
        nv_ref[...] = v_

    spec = pl.BlockSpec((tm, c), lambda i: (i, 0))
    return pl.pallas_call(body, grid=(r // tm,), in_specs=[spec] * 4, out_specs=[spec] * 3,
                          out_shape=[jax.ShapeDtypeStruct((r, c), F32)] * 3, compiler_params=_params(), name=name)(w, g, m, v)


def kernel(x, attn_norm, attn_w_in, attn_f_bias, fox_q_gain, fox_k_gain, sb_q_gain, sb_k_gain, attn_w_out, conv_norm, conv_w_in, conv_kernel, conv_w_out, ffn_norm, ffn_w_up, ffn_conv, ffn_w_down, loss_target, m_attn_norm, m_attn_w_in, m_attn_f_bias, m_fox_q_gain, m_fox_k_gain, m_sb_q_gain, m_sb_k_gain, m_attn_w_out, m_conv_norm, m_conv_w_in, m_conv_kernel, m_conv_w_out, m_ffn_norm, m_ffn_w_up, m_ffn_conv, m_ffn_w_down, v_attn_norm, v_attn_w_in, v_attn_f_bias, v_fox_q_gain, v_fox_k_gain, v_sb_q_gain, v_sb_k_gain, v_attn_w_out, v_conv_norm, v_conv_w_in, v_conv_kernel, v_conv_w_out, v_ffn_norm, v_ffn_w_up, v_ffn_conv, v_ffn_w_down):
    a = dict(locals())
    chip = 2 * lax.axis_index("x") + lax.axis_index("y")
    n_attn, n_conv, depth = attn_norm.shape[0], conv_norm.shape[0], ffn_norm.shape[0]

    units = [(name, l) for name in BIG_NAMES for l in range(a[name].shape[0])]
    early = [("attn_w_in", 0), ("attn_w_out", 0)]
    late = [u for u in units if u not in early]

    def unit_shape(u):
        return a[u[0]].shape[1:]

    def pack_units(us, get):
        return _pack_rows([get(u) for u in us], BIG_WIDTH, BIG_ROW_MULT, BF16, _big_total_rows([unit_shape(u) for u in us]))

    def unpack_units(packed, us):
        return dict(zip(us, _unpack_rows(packed, [unit_shape(u) for u in us], BIG_WIDTH, BIG_ROW_MULT)))

    def full_units(gathered, us):
        per_chip = [unpack_units(gathered[kk], us) for kk in range(N_CHIPS)]
        return {u: jnp.concatenate([per_chip[kk][u] for kk in range(N_CHIPS)], axis=BIG_AXIS[u[0]] - 1) for u in us}

    def shard(u):
        return a[u[0]][u[1]]

    small_shapes = [a[n].shape for n in SMALL_SHARDED]
    packed_s = _pack_rows([a[n] for n in SMALL_SHARDED], 128, 8, F32)
    gath_e, gath_s = gather_chips([pack_units(early, shard), packed_s], name="gather_weights")
    full_e = full_units(gath_e, early)
    full = {}
    per_chip = [_unpack_rows(gath_s[kk], small_shapes, 128, 8) for kk in range(N_CHIPS)]
    for n, name in enumerate(SMALL_SHARDED):
        full[name] = jnp.concatenate([per_chip[kk][n] for kk in range(N_CHIPS)], axis=SMALL_AXIS[name])

    def attn_weights(i, fu):
        return dict(
            norm=attn_norm[i][None],
            w_in=jnp.pad(fu[("attn_w_in", i)], ((0, 0), (0, ATTN_IN_PAD - ATTN_IN))),
            fbias=jnp.pad(attn_f_bias[i], (0, 128 - H_FOX))[None],
            gq=jnp.concatenate([jnp.tile(fox_q_gain[i], H_FOX), jnp.tile(sb_q_gain[i], H_SB)])[None],
            gk=jnp.concatenate([jnp.tile(fox_k_gain[i], H_FOX), jnp.tile(sb_k_gain[i], H_SB)])[None],
            w_out=fu[("attn_w_out", i)])

    def build_weights(gathered):
        fu = {**full_e, **full_units(gathered, late)}
        wa = [attn_weights(i, fu) for i in range(n_attn)]
        wc = [dict(norm=full["conv_norm"][i][None], w_in=fu[("conv_w_in", i)], ck=full["conv_kernel"][i][None],
                   w_out=fu[("conv_w_out", i)]) for i in range(n_conv)]
        wf = []
        for l in range(depth):
            cw = full["ffn_conv"][l]
            wf.append(dict(norm=ffn_norm[l][None], w_up=fu[("ffn_w_up", l)], cw2=jnp.stack([cw[:, :D_FF], cw[:, D_FF:]]),
                           w_down=fu[("ffn_w_down", l)]))
        return wa, wc, wf

    def chunk_of(u, kk, ga, gc, gf):
        name, l = u
        g = {"attn_w_in": lambda: ga[l]["w_in"], "attn_w_out": lambda: ga[l]["w_out"],
             "conv_w_in": lambda: gc[l]["w_in"], "conv_w_out": lambda: gc[l]["w_out"],
             "ffn_w_up": lambda: gf[l]["w_up"], "ffn_w_down": lambda: gf[l]["w_down"]}[name]()
        width = a[name].shape[BIG_AXIS[name]]
        return lax.slice_in_dim(g, kk * width, (kk + 1) * width, axis=BIG_AXIS[name] - 1)

    def chunks_of(us, ga, gc, gf):
        return jnp.stack([pack_units(us, lambda u: chunk_of(u, kk, ga, gc, gf)) for kk in range(N_CHIPS)])

    loss_blk, grad_x, ga, gc, gf, landed_late = forward_backward(
        x[0], loss_target[0], [attn_weights(0, full_e)], [], [],
        late_weights=(pack_units(late, shard), build_weights),
        late_chunks=lambda ga, gc, gf: chunks_of(late, ga, gc, gf))

    landed_early = scatter_chips(chunks_of(early, ga, gc, gf), name="scatter_grads")
    mine = [sum_chips(landed_early, name="sum_chips"), sum_chips(landed_late, name="sum_chips")]
    theirs = swap_cores(mine, name="swap_cores")
    g_units = {**unpack_units(add_pair(mine[0], theirs[0], name="add_cores"), early),
               **unpack_units(add_pair(mine[1], theirs[1], name="add_cores"), late)}
    grads = {name: jnp.stack([g_units[(name, l)] for l in range(a[name].shape[0])]) for name in BIG_NAMES}

    small_full = [
        loss_blk,
        jnp.stack([g["norm"] for g in ga]), jnp.stack([g["f_bias"] for g in ga]),
        jnp.stack([g["fox_q"] for g in ga]), jnp.stack([g["fox_k"] for g in ga]),
        jnp.stack([g["sb_q"] for g in ga]), jnp.stack([g["sb_k"] for g in ga]),
        jnp.stack([g["norm"] for g in gf]),
        jnp.stack([g["norm"] for g in gc]), jnp.stack([g["ck"] for g in gc]), jnp.stack([g["cw"] for g in gf]),
    ]
    summed = allreduce_small(_pack_rows(small_full, 128, 8, F32), name="allreduce_small")
    parts = _unpack_rows(summed, [p.shape for p in small_full], 128, 8)
    loss = parts[0][0, 0]
    for name, g in zip(SMALL_REPLICATED, parts[1:8]):
        grads[name] = g
    for name, g in zip(SMALL_SHARDED, parts[8:]):
        width = a[name].shape[SMALL_AXIS[name]]
        grads[name] = lax.dynamic_slice_in_dim(g, chip * width, width, axis=SMALL_AXIS[name])

    delta, new_m, new_v = {}, {}, {}
    for name in BIG_NAMES:
        shape = a[name].shape
        flat = lambda arr: arr.reshape(-1, shape[-1])
        d_, m_, v_ = adamw(flat(a[name]), flat(grads[name]), flat(a["m_" + name]), flat(a["v_" + name]), tm=256,
                           name="adamw")
        delta[name], new_m[name], new_v[name] = d_.reshape(shape), m_.reshape(shape), v_.reshape(shape)
    small_names = SMALL_REPLICATED + SMALL_SHARDED
    small_shapes_local = [a[n].shape for n in small_names]
    pack = lambda prefix, src: _pack_rows([src[prefix + n] for n in small_names], 128, 8, F32)
    packed = adamw(pack("", a), pack("", grads), pack("m_", a), pack("v_", a), tm=8, name="adamw_small")
    for store, buf in zip((delta, new_m, new_v), packed):
        for name, arr in zip(small_names, _unpack_rows(buf, small_shapes_local, 128, 8)):
            store[name] = arr

    return (loss, grad_x[None], *[grads[n] for n in WEIGHT_ORDER], *[delta[n] for n in WEIGHT_ORDER],
            *[new_m[n] for n in WEIGHT_ORDER], *[new_v[n] for n in WEIGHT_ORDER])
```

```python
import functools

import jax
import jax.numpy as jnp
from jax import lax
from jax.experimental import pallas as pl
from jax.experimental.pallas import tpu as pltpu

F32 = jnp.float32
BF16 = jnp.bfloat16

D_MODEL = 1024
HEAD_DIM = 64
H_FOX = 8
H_SB = 8
N_HEADS = H_FOX + H_SB
MIX = N_HEADS * HEAD_DIM
ATTN_IN = 3 * MIX + H_FOX
ATTN_IN_PAD = 3 * MIX + 128
D_FF = 2816
EPS = 1e-6
SCALE = HEAD_DIM ** -0.5
NEG = -1e30

ADAM_LR = 0.001
ADAM_B1 = 0.9
ADAM_B2 = 0.999
ADAM_EPS = 1e-08
ADAM_WD = 0.01
ADAM_STEP = 10

VMEM_LIMIT = 56 * 1024 * 1024
HALO = 8
BQ = 512
N_CHIPS = 4
MESH = pl.DeviceIdType.MESH


def _params(**kw):
    return pltpu.CompilerParams(vmem_limit_bytes=VMEM_LIMIT, **kw)


def _dot(a, b):
    return jnp.dot(a, b, preferred_element_type=F32)


def _dot_nt(a, b):
    return lax.dot_general(a, b, (((1,), (1,)), ((), ())), preferred_element_type=F32)


def _dot_tn(a, b):
    return lax.dot_general(a, b, (((0,), (0,)), ((), ())), preferred_element_type=F32)


def _split2(x):
    hi = x.astype(BF16)
    lo = (x - hi.astype(F32)).astype(BF16)
    return hi, lo


def _split3(x):
    hi = x.astype(BF16)
    r = x - hi.astype(F32)
    mid = r.astype(BF16)
    lo = (r - mid.astype(F32)).astype(BF16)
    return hi, mid, lo


def mm_nn(a, b, *, add=None, out_dtype=F32, parts=1, tm=1024, tn=512, name):
    m, k = a.shape
    n = b.shape[1]
    np_ = n // parts
    nb = np_ // tn
    tm = min(tm, m)
    assert m % tm == 0 and np_ % tn == 0

    def body(*refs):
        if add is None:
            a_ref, b_ref, o_ref = refs
            acc = _dot(a_ref[...].astype(BF16), b_ref[...])
        else:
            a_ref, b_ref, r_ref, o_ref = refs
            acc = _dot(a_ref[...].astype(BF16), b_ref[...]) + r_ref[...]
        o_ref[...] = acc.astype(out_dtype).reshape(o_ref.shape)

    in_specs = [pl.BlockSpec((tm, k), lambda i, j: (i, 0)), pl.BlockSpec((k, tn), lambda i, j: (0, j))]
    args = [a, b]
    if add is not None:
        in_specs.append(pl.BlockSpec((tm, tn), lambda i, j: (i, j)))
        args.append(add)
    if parts == 1:
        out_spec = pl.BlockSpec((tm, tn), lambda i, j: (i, j))
        out_shape = jax.ShapeDtypeStruct((m, n), out_dtype)
    else:
        out_spec = pl.BlockSpec((1, tm, tn), lambda i, j: (j // nb, i, j % nb))
        out_shape = jax.ShapeDtypeStruct((parts, m, np_), out_dtype)
    return pl.pallas_call(body, grid=(m // tm, n // tn), in_specs=in_specs, out_specs=out_spec,
                          out_shape=out_shape, compiler_params=_params(), name=name)(*args)


def mm_nt(a3, b, *, out_dtype=F32, tm=1024, tn=512, name):
    p, m, kp = a3.shape
    n = b.shape[0]
    tm = min(tm, m)
    assert m % tm == 0 and n % tn == 0 and b.shape[1] == p * kp

    def body(a_ref, b_ref, o_ref, acc_ref):
        part = pl.program_id(2)
        prod = _dot_nt(a_ref[0].astype(BF16), b_ref[...])

        @pl.when(part == 0)
        def _():
            acc_ref[...] = prod

        @pl.when(part > 0)
        def _():
            acc_ref[...] += prod

        @pl.when(part == p - 1)
        def _():
            o_ref[...] = acc_ref[...].astype(out_dtype)

    return pl.pallas_call(
        body, grid=(m // tm, n // tn, p),
        in_specs=[pl.BlockSpec((1, tm, kp), lambda i, j, q: (q, i, 0)), pl.BlockSpec((tn, kp), lambda i, j, q: (j, q))],
        out_specs=pl.BlockSpec((tm, tn), lambda i, j, q: (i, j)),
        out_shape=jax.ShapeDtypeStruct((m, n), out_dtype),
        scratch_shapes=[pltpu.VMEM((tm, tn), F32)],
        compiler_params=_params(), name=name)(a3, b)


def mm_tn(a, b3, *, tk=512, tn=512, tt=2048, name):
    t, k = a.shape
    p, _, np_ = b3.shape
    nb = np_ // tn
    tt = min(tt, t)
    assert t % tt == 0 and k % tk == 0 and np_ % tn == 0

    def body(a_ref, b_ref, o_ref):
        prod = _dot_tn(a_ref[...].astype(BF16), b_ref[0].astype(BF16))

        @pl.when(pl.program_id(2) == 0)
        def _():
            o_ref[...] = prod

        @pl.when(pl.program_id(2) > 0)
        def _():
            o_ref[...] += prod

    return pl.pallas_call(
        body, grid=(k // tk, p * nb, t // tt),
        in_specs=[pl.BlockSpec((tt, tk), lambda i, j, s: (s, i)), pl.BlockSpec((1, tt, tn), lambda i, j, s: (j // nb, s, j % nb))],
        out_specs=pl.BlockSpec((tk, tn), lambda i, j, s: (i, j)),
        out_shape=jax.ShapeDtypeStruct((k, p * np_), F32),
        compiler_params=_params(), name=name)(a, b3)


def rms_fwd(h, g, *, name, tm=512):
    t, d = h.shape

    def body(h_ref, g_ref, o_ref):
        x = h_ref[...]
        r = lax.rsqrt(jnp.mean(x * x, axis=-1, keepdims=True) + EPS)
        o_ref[...] = (x * r * g_ref[...]).astype(BF16)

    return pl.pallas_call(
        body, grid=(t // tm,),
        in_specs=[pl.BlockSpec((tm, d), lambda i: (i, 0)), pl.BlockSpec((1, d), lambda i: (0, 0))],
        out_specs=pl.BlockSpec((tm, d), lambda i: (i, 0)),
        out_shape=jax.ShapeDtypeStruct((t, d), BF16), compiler_params=_params(), name=name)(h, g)


def mm_nt_rms_bwd(a3, b, h, g, dres, *, name, tm=512):
    p, t, kp = a3.shape
    d = b.shape[0]
    tm = min(tm, t)
    assert t % tm == 0 and b.shape[1] == p * kp

    def body(a_ref, b_ref, h_ref, g_ref, dres_ref, dh_ref, dg_ref, acc_ref):
        i = pl.program_id(0)
        part = pl.program_id(1)
        prod = _dot_nt(a_ref[0].astype(BF16), b_ref[...])

        @pl.when(part == 0)
        def _():
            acc_ref[...] = prod

        @pl.when(part > 0)
        def _():
            acc_ref[...] += prod

        @pl.when(part == p - 1)
        def _():
            x = h_ref[...]
            dy = acc_ref[...]
            r = lax.rsqrt(jnp.mean(x * x, axis=-1, keepdims=True) + EPS)
            gy = dy * g_ref[...]
            dot = jnp.mean(gy * x, axis=-1, keepdims=True)
            dh_ref[...] = dres_ref[...] + r * gy - x * (r * r * r * dot)
            _acc_rows(dg_ref, jnp.sum(dy * x * r, axis=0, keepdims=True), i == 0)

    row = pl.BlockSpec((tm, d), lambda i, q: (i, 0))
    vec = pl.BlockSpec((1, d), lambda i, q: (0, 0))
    return pl.pallas_call(
        body, grid=(t // tm, p),
        in_specs=[pl.BlockSpec((1, tm, kp), lambda i, q: (q, i, 0)), pl.BlockSpec((d, kp), lambda i, q: (0, q)),
                  row, vec, row],
        out_specs=[row, vec],
        out_shape=[jax.ShapeDtypeStruct((t, d), F32), jax.ShapeDtypeStruct((1, d), F32)],
        scratch_shapes=[pltpu.VMEM((tm, d), F32)], compiler_params=_params(), name=name)(a3, b, h, g, dres)


def _causal3(x, w):
    return w[0:1] * pltpu.roll(x, 2, 0) + w[1:2] * pltpu.roll(x, 1, 0) + w[2:3] * x


def _anticausal3(z, w):
    n = z.shape[0]
    return w[2:3] * z + w[1:2] * pltpu.roll(z, n - 1, 0) + w[0:1] * pltpu.roll(z, n - 2, 0)


def _prev_spec(part, tm, tc, nrow8):
    del nrow8
    return pl.BlockSpec((1, HALO, tc), lambda j, i: (part, jnp.maximum(i * (tm // HALO) - 1, 0), j))


def _next_spec(part, tm, tc, nrow8):
    return pl.BlockSpec((1, HALO, tc), lambda j, i: (part, jnp.minimum((i + 1) * (tm // HALO), nrow8 - 1), j))


def _tile_spec(part, tm, tc):
    return pl.BlockSpec((1, tm, tc), lambda j, i: (part, i, j))


def _shifted_rows(x_ext, tm):
    x2 = pltpu.roll(x_ext, 2, 0)[HALO:HALO + tm]
    x1 = pltpu.roll(x_ext, 1, 0)[HALO:HALO + tm]
    x0 = x_ext[HALO:HALO + tm]
    return x2, x1, x0


def _acc_rows(ref, val, first):
    @pl.when(first)
    def _():
        ref[...] = val

    @pl.when(jnp.logical_not(first))
    def _():
        ref[...] += val


def ffn_act_down_fwd(up2, cw2, w_down, h, *, name, tm=256, tc=1408):
    _, t, f = up2.shape
    d = h.shape[1]
    tm = min(tm, t)

    def body(g_ref, v_ref, gp_ref, vp_ref, w_ref, wd_ref, h_ref, o_ref, act_ref):
        keep = jnp.where(pl.program_id(0) == 0, 0.0, 1.0)
        for cc in range(f // tc):
            cols = slice(cc * tc, (cc + 1) * tc)
            g_ext = jnp.concatenate([gp_ref[0, :, cols] * keep, g_ref[0, :, cols]], axis=0)
            v_ext = jnp.concatenate([vp_ref[0, :, cols] * keep, v_ref[0, :, cols]], axis=0)
            ug = _causal3(g_ext, w_ref[0, :, cols])[HALO:]
            uv = _causal3(v_ext, w_ref[1, :, cols])[HALO:]
            act_ref[:, cols] = (ug * jax.nn.sigmoid(ug) * uv).astype(BF16)
        o_ref[...] = _dot(act_ref[...], wd_ref[...]) + h_ref[...]

    tile = lambda part: pl.BlockSpec((1, tm, f), lambda i: (part, i, 0))
    prev = lambda part: pl.BlockSpec((1, HALO, f), lambda i: (part, jnp.maximum(i * (tm // HALO) - 1, 0), 0))
    row = pl.BlockSpec((tm, d), lambda i: (i, 0))
    return pl.pallas_call(
        body, grid=(t // tm,),
        in_specs=[tile(0), tile(1), prev(0), prev(1), pl.BlockSpec((2, 3, f), lambda i: (0, 0, 0)),
                  pl.BlockSpec((f, d), lambda i: (0, 0)), row],
        out_specs=[row, pl.BlockSpec((tm, f), lambda i: (i, 0))],
        out_shape=[jax.ShapeDtypeStruct((t, d), F32), jax.ShapeDtypeStruct((t, f), BF16)],
        compiler_params=_params(), name=name)(up2, up2, up2, up2, cw2, w_down, h)


def ffn_act_bwd(dh, w_down, up2, cw2, *, name, tm=256, tc=1408):
    _, t, f = up2.shape
    d = dh.shape[1]
    n8 = t // HALO

    def body(d_ref, dn_ref, wd_ref, g_ref, v_ref, gp_ref, vp_ref, gn_ref, vn_ref, wg_ref, wv_ref, dup_ref, dw_ref):
        i = pl.program_id(1)
        first = i == 0
        keep_p = jnp.where(first, 0.0, 1.0)
        keep_n = jnp.where(i == pl.num_programs(1) - 1, 0.0, 1.0)
        wg = wg_ref[0]
        wv = wv_ref[0]
        g_ext = jnp.concatenate([gp_ref[0] * keep_p, g_ref[0], gn_ref[0]], axis=0)
        v_ext = jnp.concatenate([vp_ref[0] * keep_p, v_ref[0], vn_ref[0]], axis=0)
        dh_ext = jnp.concatenate([d_ref[...], dn_ref[...] * keep_n], axis=0)
        d_ext = _dot_nt(dh_ext.astype(BF16), wd_ref[...])
        ug = _causal3(g_ext, wg)[HALO:]
        uv = _causal3(v_ext, wv)[HALO:]
        s = jax.nn.sigmoid(ug)
        dg = d_ext * uv * (s * (1.0 + ug * (1.0 - s)))
        dv = d_ext * (ug * s)
        dup_ref[0] = _anticausal3(dg, wg)[:tm].astype(BF16)
        dup_ref[1] = _anticausal3(dv, wv)[:tm].astype(BF16)
        g2, g1, g0 = _shifted_rows(g_ext, tm)
        v2, v1, v0 = _shifted_rows(v_ext, tm)
        dgt = dg[:tm]
        dvt = dv[:tm]
        zero = jnp.zeros((HALO - 3, tc), F32)
        rows_g = [jnp.sum(dgt * x, axis=0, keepdims=True) for x in (g2, g1, g0)] + [zero]
        rows_v = [jnp.sum(dvt * x, axis=0, keepdims=True) for x in (v2, v1, v0)] + [zero]
        _acc_rows(dw_ref, jnp.stack([jnp.concatenate(rows_g, axis=0), jnp.concatenate(rows_v, axis=0)]), first)

    wspec = lambda part: pl.BlockSpec((1, 3, tc), lambda j, i: (part, 0, j))
    return pl.pallas_call(
        body, grid=(f // tc, t // tm),
        in_specs=[pl.BlockSpec((tm, d), lambda j, i: (i, 0)),
                  pl.BlockSpec((HALO, d), lambda j, i: (jnp.minimum((i + 1) * (tm // HALO), n8 - 1), 0)),
                  pl.BlockSpec((tc, d), lambda j, i: (j, 0)),
                  _tile_spec(0, tm, tc), _tile_spec(1, tm, tc), _prev_spec(0, tm, tc, n8), _prev_spec(1, tm, tc, n8),
                  _next_spec(0, tm, tc, n8), _next_spec(1, tm, tc, n8), wspec(0), wspec(1)],
        out_specs=[pl.BlockSpec((2, tm, tc), lambda j, i: (0, i, j)), pl.BlockSpec((2, HALO, tc), lambda j, i: (0, 0, j))],
        out_shape=[jax.ShapeDtypeStruct((2, t, f), BF16), jax.ShapeDtypeStruct((2, HALO, f), F32)],
        compiler_params=_params(), name=name)(dh, dh, w_down, up2, up2, up2, up2, up2, up2, cw2, cw2)


def conv_mix_fwd(proj3, ck, *, name, tm=512, tc=512):
    _, t, c = proj3.shape
    n8 = t // HALO

    def body(b_ref, c_ref, u_ref, cp_ref, up_ref, w_ref, o_ref):
        keep = jnp.where(pl.program_id(1) == 0, 0.0, 1.0)
        cu_ext = jnp.concatenate([cp_ref[0] * up_ref[0] * keep, c_ref[0] * u_ref[0]], axis=0)
        o_ref[...] = (b_ref[0] * _causal3(cu_ext, w_ref[0])[HALO:]).astype(BF16)

    return pl.pallas_call(
        body, grid=(c // tc, t // tm),
        in_specs=[_tile_spec(0, tm, tc), _tile_spec(1, tm, tc), _tile_spec(2, tm, tc), _prev_spec(1, tm, tc, n8),
                  _prev_spec(2, tm, tc, n8), pl.BlockSpec((1, 3, tc), lambda j, i: (0, 0, j))],
        out_specs=pl.BlockSpec((tm, tc), lambda j, i: (i, j)),
        out_shape=jax.ShapeDtypeStruct((t, c), BF16), compiler_params=_params(), name=name)(proj3, proj3, proj3, proj3, proj3, ck)


def conv_mix_bwd(dh, w_out, proj3, ck, *, name, tm=512, tc=512):
    _, t, c = proj3.shape
    d = dh.shape[1]
    n8 = t // HALO

    def body(d_ref, dn_ref, wo_ref, b_ref, c_ref, u_ref, cp_ref, up_ref, bn_ref, w_ref, dp_ref, dw_ref):
        i = pl.program_id(1)
        first = i == 0
        keep_p = jnp.where(first, 0.0, 1.0)
        keep_n = jnp.where(i == pl.num_programs(1) - 1, 0.0, 1.0)
        w = w_ref[0]
        cu_ext = jnp.concatenate([cp_ref[0] * up_ref[0] * keep_p, c_ref[0] * u_ref[0]], axis=0)
        cv = _causal3(cu_ext, w)[HALO:]
        dh_ext = jnp.concatenate([d_ref[...], dn_ref[...] * keep_n], axis=0)
        d_ext = _dot_nt(dh_ext.astype(BF16), wo_ref[...])
        dyt = d_ext[:tm]
        b_ext = jnp.concatenate([b_ref[0], bn_ref[0]], axis=0)
        dcv = d_ext * b_ext
        dcu = _anticausal3(dcv, w)[:tm]
        dp_ref[0] = (dyt * cv).astype(BF16)
        dp_ref[1] = (dcu * u_ref[0]).astype(BF16)
        dp_ref[2] = (dcu * c_ref[0]).astype(BF16)
        x2, x1, x0 = _shifted_rows(cu_ext, tm)
        dcvt = dcv[:tm]
        rows = [jnp.sum(dcvt * x, axis=0, keepdims=True) for x in (x2, x1, x0)] + [jnp.zeros((HALO - 3, tc), F32)]
        _acc_rows(dw_ref, jnp.concatenate(rows, axis=0)[None], first)

    return pl.pallas_call(
        body, grid=(c // tc, t // tm),
        in_specs=[pl.BlockSpec((tm, d), lambda j, i: (i, 0)),
                  pl.BlockSpec((HALO, d), lambda j, i: (jnp.minimum((i + 1) * (tm // HALO), n8 - 1), 0)),
                  pl.BlockSpec((tc, d), lambda j, i: (j, 0)),
                  _tile_spec(0, tm, tc), _tile_spec(1, tm, tc), _tile_spec(2, tm, tc),
                  _prev_spec(1, tm, tc, n8), _prev_spec(2, tm, tc, n8),
                  _next_spec(0, tm, tc, n8), pl.BlockSpec((1, 3, tc), lambda j, i: (0, 0, j))],
        out_specs=[pl.BlockSpec((3, tm, tc), lambda j, i: (0, i, j)), pl.BlockSpec((1, HALO, tc), lambda j, i: (0, 0, j))],
        out_shape=[jax.ShapeDtypeStruct((3, t, c), BF16), jax.ShapeDtypeStruct((1, HALO, c), F32)],
        compiler_params=_params(), name=name)(dh, dh, w_out, proj3, proj3, proj3, proj3, proj3, proj3, ck)


def _head_sums(x, bd):
    hi, lo = _split2(x)
    return _dot(hi, bd) + _dot(lo, bd)


def attn_prep_fwd(proj, gq, gk, fbias, bd, *, name, tm=256):
    t = proj.shape[0]

    def body(q_ref, k_ref, v_ref, f_ref, gq_ref, gk_ref, fb_ref, bd_ref, qs_ref, kn_ref, vb_ref, lf_ref):
        bd = bd_ref[...]

        def headnorm(x_ref, g_ref, o_ref, scale):
            for c in range(MIX // 128):
                sl = slice(128 * c, 128 * (c + 1))
                x = x_ref[:, sl]
                r = lax.rsqrt(_head_sums(x * x, bd) * (1.0 / HEAD_DIM) + EPS)
                o_ref[:, sl] = (x * r * (g_ref[:, sl] * scale)).astype(BF16)

        headnorm(q_ref, gq_ref, qs_ref, SCALE)
        headnorm(k_ref, gk_ref, kn_ref, 1.0)
        vb_ref[...] = v_ref[...].astype(BF16)
        fl = f_ref[...] + fb_ref[...]
        logf = jnp.minimum(fl, 0.0) - jnp.log(1.0 + jnp.exp(-jnp.abs(fl)))
        lf_ref[...] = logf.T[0:H_FOX, :]

    col = lambda c: pl.BlockSpec((tm, MIX), lambda i: (i, c))
    vec = pl.BlockSpec((1, MIX), lambda i: (0, 0))
    out = pl.BlockSpec((tm, MIX), lambda i: (i, 0))
    return pl.pallas_call(
        body, grid=(t // tm,),
        in_specs=[col(0), col(1), col(2), pl.BlockSpec((tm, 128), lambda i: (i, 3 * MIX // 128)), vec, vec,
                  pl.BlockSpec((1, 128), lambda i: (0, 0)), pl.BlockSpec((128, 128), lambda i: (0, 0))],
        out_specs=[out, out, out, pl.BlockSpec((H_FOX, tm), lambda i: (0, i))],
        out_shape=[jax.ShapeDtypeStruct((t, MIX), BF16)] * 3 + [jax.ShapeDtypeStruct((H_FOX, t), F32)],
        compiler_params=_params(), name=name)(proj, proj, proj, proj, gq, gk, fbias, bd)


def attn_prep_bwd(proj, dqs, dkn, dv, dfl, gq, gk, bd, *, name, tm=256):
    t = proj.shape[0]

    def body(q_ref, k_ref, dq_ref, dk_ref, dv_ref, dfl_ref, gq_ref, gk_ref, bd_ref, dp_ref, dgq_ref, dgk_ref):
        bd = bd_ref[...]
        first = pl.program_id(0) == 0

        def back(x_ref, d_ref, g_ref, col0, scale, dg_ref):
            parts = []
            for c in range(MIX // 128):
                sl = slice(128 * c, 128 * (c + 1))
                x = x_ref[:, sl]
                r = lax.rsqrt(_head_sums(x * x, bd) * (1.0 / HEAD_DIM) + EPS)
                dn = d_ref[:, sl] * scale
                gy = dn * g_ref[:, sl]
                hs = _head_sums(gy * x, bd) * (1.0 / HEAD_DIM)
                dp_ref[:, col0 + 128 * c:col0 + 128 * (c + 1)] = (r * gy - x * (r * r * r * hs)).astype(BF16)
                parts.append(jnp.sum(dn * x * r, axis=0, keepdims=True))
            _acc_rows(dg_ref, jnp.concatenate(parts, axis=1), first)

        back(q_ref, dq_ref, gq_ref, 0, SCALE, dgq_ref)
        back(k_ref, dk_ref, gk_ref, MIX, 1.0, dgk_ref)
        dp_ref[:, 2 * MIX:3 * MIX] = dv_ref[...].astype(BF16)
        dp_ref[:, 3 * MIX:] = dfl_ref[...]

    col = lambda c: pl.BlockSpec((tm, MIX), lambda i: (i, c))
    row = pl.BlockSpec((tm, MIX), lambda i: (i, 0))
    vec = pl.BlockSpec((1, MIX), lambda i: (0, 0))
    return pl.pallas_call(
        body, grid=(t // tm,),
        in_specs=[col(0), col(1), row, row, row, pl.BlockSpec((tm, 128), lambda i: (i, 0)), vec, vec,
                  pl.BlockSpec((128, 128), lambda i: (0, 0))],
        out_specs=[pl.BlockSpec((tm, ATTN_IN_PAD), lambda i: (i, 0)), vec, vec],
        out_shape=[jax.ShapeDtypeStruct((t, ATTN_IN_PAD), BF16), jax.ShapeDtypeStruct((1, MIX), F32),
                   jax.ShapeDtypeStruct((1, MIX), F32)],
        compiler_params=_params(), name=name)(proj, proj, dqs, dkn, dv, dfl, gq, gk, bd)


def gate_cumsum(logf3, tri, *, name):
    nc, r, _ = logf3.shape

    def body(x_ref, tri_ref, o_ref):
        tri_m = tri_ref[...]

        def step(c, carry):
            hi, mid, lo = _split3(x_ref[c])
            cs = _dot(hi, tri_m) + _dot(mid, tri_m) + _dot(lo, tri_m) + carry
            o_ref[c] = cs
            return cs[:, 127:128]

        lax.fori_loop(0, nc, step, jnp.zeros((r, 1), F32))

    return pl.pallas_call(body, out_shape=jax.ShapeDtypeStruct(logf3.shape, F32), compiler_params=_params(),
                          name=name)(logf3, tri)


def gate_cumsum_bwd(dcum3, logf3, tri, *, name):
    nc, r, _ = dcum3.shape

    def body(x_ref, lf_ref, tri_ref, o_ref, s_ref):
        tri_m = tri_ref[...]

        def step(n, carry):
            car, tot = carry
            c = nc - 1 - n
            hi, mid, lo = _split3(x_ref[c])
            cs = _dot(hi, tri_m) + _dot(mid, tri_m) + _dot(lo, tri_m) + car
            dl = cs * (1.0 - jnp.exp(lf_ref[c]))
            o_ref[c] = dl
            return cs[:, 0:1], tot + dl

        _, tot = lax.fori_loop(0, nc, step, (jnp.zeros((r, 1), F32), jnp.zeros((r, 128), F32)))
        s_ref[...] = jnp.broadcast_to(jnp.sum(tot, axis=1, keepdims=True), tot.shape)

    return pl.pallas_call(body, out_shape=[jax.ShapeDtypeStruct(dcum3.shape, F32), jax.ShapeDtypeStruct((r, 128), F32)],
                          compiler_params=_params(), name=name)(dcum3, logf3, tri)


def _causal_iota():
    row = lax.broadcasted_iota(jnp.int32, (BQ, BQ), 0)
    col = lax.broadcasted_iota(jnp.int32, (BQ, BQ), 1)
    return row, col


def _head_specs(nj, head0):
    qin = pl.BlockSpec((1, BQ, HEAD_DIM), lambda h, i: (h + head0, i, 0))
    kin = pl.BlockSpec((1, nj, BQ, HEAD_DIM), lambda h, i: (h + head0, 0, 0, 0))
    qin2 = pl.BlockSpec((1, BQ, 2 * HEAD_DIM), lambda h, i: (h + head0, i, 0))
    kin2 = pl.BlockSpec((1, nj, BQ, 2 * HEAD_DIM), lambda h, i: (h + head0, 0, 0, 0))
    qspec = pl.BlockSpec((1, BQ, HEAD_DIM), lambda h, i: (h, i, 0))
    kspec2 = pl.BlockSpec((1, nj, BQ, 2 * HEAD_DIM), lambda h, i: (h, 0, 0, 0))
    return qin, kin, qin2, kin2, qspec, kspec2


STOP = -105.0
STOP_WIDE = -115.0
FIXED_REF_MAX = 40.0


def _store_kmax(k_ref, kmax_ref, nj):
    def step(j, mx):
        kf = k_ref[0, j].astype(F32)
        return jnp.maximum(mx, jnp.max(jnp.sum(kf * kf, axis=1, keepdims=True), axis=0, keepdims=True))

    mx = lax.fori_loop(0, nj, step, jnp.zeros((1, 1), F32))
    kmax_ref[...] = jnp.broadcast_to(jnp.sqrt(mx), kmax_ref.shape)


def _qk_bound(q, kmax_ref):
    qf = q.astype(F32)
    return jnp.sqrt(jnp.sum(qf * qf, axis=1, keepdims=True)) * kmax_ref[0:1, 0:1] * 1.001


def _first_and_last_step():
    h, i = pl.program_id(0), pl.program_id(1)
    first = jnp.logical_and(h == 0, i == 0)
    last = jnp.logical_and(h == pl.num_programs(0) - 1, i == pl.num_programs(1) - 1)
    return first, last


def fox_fwd(qs, kn4, va4, fcol, frow4, *, name, gather=None):
    _, t, dh = qs.shape
    nh = H_FOX
    nj = t // BQ

    def body(*refs):
        if gather is None:
            q_ref, k_ref, v_ref, fc_ref, fr_ref, o_ref, lse_ref, kmax_ref = refs
        else:
            q_ref, k_ref, v_ref, fc_ref, fr_ref, src_ref, o_ref, lse_ref, dst_ref, kmax_ref = refs[:10]
            first_step, last_step = _first_and_last_step()

            @pl.when(first_step)
            def _():
                _chip_gather(src_ref, dst_ref, *refs[10:])[0]()

        i = pl.program_id(1)

        @pl.when(i == 0)
        def _():
            _store_kmax(k_ref, kmax_ref, nj)

        q = q_ref[0]
        fq = fc_ref[0]
        bound = _qk_bound(q, kmax_ref)
        row, col = _causal_iota()

        def gate_at_block_end(j):
            return fr_ref[0, j][:, BQ - 1:BQ]

        def pv(p, j):
            p_hi, p_lo = _split2(p)
            return _dot(p_hi, v_ref[0, j]) + _dot(p_lo, v_ref[0, j])

        def walk(block, live, init):
            carry = block(i, init, True)

            def cond(c):
                n, carry = c
                return jnp.logical_and(n < i, live(jnp.maximum(i - 1 - n, 0), carry))

            _, carry = lax.while_loop(cond, lambda c: (c[0] + 1, block(i - 1 - c[0], c[1], False)), (0, carry))
            return carry

        def fixed_reference(_):
            shift = fq - bound

            def block(j, acc, diag):
                p = jnp.exp(_dot_nt(q, k_ref[0, j]) + shift - fr_ref[0, j])
                if diag:
                    p = jnp.where(col <= row, p, 0.0)
                return acc + pv(p, j)

            def live(j, acc):
                return jnp.max(fq - gate_at_block_end(j) - jnp.log(acc[:, dh:dh + 1])) >= STOP_WIDE

            acc = walk(block, live, jnp.zeros((BQ, 2 * dh), F32))
            l = acc[:, dh:dh + 1]
            return acc[:, :dh] / l, bound + jnp.log(l)

        def running_maximum(_):
            def block(j, carry, diag):
                m, acc = carry
                s = _dot_nt(q, k_ref[0, j]) + fq - fr_ref[0, j]
                if diag:
                    s = jnp.where(col <= row, s, NEG)
                m_new = jnp.maximum(m, jnp.max(s, axis=1, keepdims=True))
                return m_new, jnp.exp(m - m_new) * acc + pv(jnp.exp(s - m_new), j)

            def live(j, carry):
                return jnp.max(bound + fq - gate_at_block_end(j) - carry[0]) >= STOP

            m, acc = walk(block, live, (jnp.full((BQ, 1), NEG, F32), jnp.zeros((BQ, 2 * dh), F32)))
            l = acc[:, dh:dh + 1]
            return acc[:, :dh] / l, m + jnp.log(l)

        o, lse = lax.cond(jnp.max(bound) < FIXED_REF_MAX, fixed_reference, running_maximum, 0)
        o_ref[0] = o
        lse_ref[0] = lse

        if gather is not None:
            @pl.when(last_step)
            def _():
                _chip_gather(src_ref, dst_ref, *refs[10:])[1]()

    qin, kin, _, kin2, qspec, _ = _head_specs(nj, 0)
    cspec = pl.BlockSpec((1, BQ, 1), lambda h, i: (h, i, 0))
    in_specs = [qin, kin, kin2, cspec, pl.BlockSpec((1, nj, 1, BQ), lambda h, i: (h, 0, 0, 0))]
    out_specs = [qspec, cspec]
    out_shape = [jax.ShapeDtypeStruct((nh, t, dh), F32), jax.ShapeDtypeStruct((nh, t, 1), F32)]
    scratch = [pltpu.VMEM((8, 128), F32)]
    args = [qs, kn4, va4, fcol, frow4]
    if gather is not None:
        in_specs.append(_ANY)
        out_specs.append(_ANY)
        out_shape.append(jax.ShapeDtypeStruct((N_CHIPS,) + gather.shape, gather.dtype))
        scratch += _chip_sems()
        args.append(gather)
    return pl.pallas_call(body, grid=(nh, nj), in_specs=in_specs, out_specs=out_specs, out_shape=out_shape,
                          scratch_shapes=scratch, compiler_params=_params(), name=name)(*args)


def fox_bwd(qs, kn4, v4, qa, doa, fcol, frow4, o, do, lse, *, name, scatter=None):
    _, t, dh = qs.shape
    nh = H_FOX
    nj = t // BQ

    def body(*refs):
        q_ref, k_ref, v_ref, qa_ref, doa_ref, fc_ref, fr_ref, o_ref, do_ref, lse_ref = refs[:10]
        if scatter is None:
            dq_ref, dkv_ref, dfk_ref, kmax_ref = refs[10:]
        else:
            g_ref, dq_ref, dkv_ref, dfk_ref, land_ref, kmax_ref = refs[10:16]
            first_step, last_step = _first_and_last_step()

            @pl.when(first_step)
            def _():
                _chip_scatter(g_ref, land_ref, *refs[16:])[0]()

        i = pl.program_id(1)

        @pl.when(i == 0)
        def _():
            dkv_ref[...] = jnp.zeros_like(dkv_ref)
            dfk_ref[...] = jnp.zeros_like(dfk_ref)
            _store_kmax(k_ref, kmax_ref, nj)

        q = q_ref[0]
        do_b = do_ref[0]
        fq = fc_ref[0]
        lse_q = lse_ref[0]
        dd = jnp.sum(do_b.astype(F32) * o_ref[0], axis=1, keepdims=True)
        rhs = jnp.concatenate([qa_ref[0], doa_ref[0]], axis=0)
        edge = _qk_bound(q, kmax_ref) + fq - lse_q

        def negligible(j):
            return jnp.logical_and(j < i, jnp.max(edge - fr_ref[0, j][:, BQ - 1:BQ]) < STOP_WIDE)

        first = lax.while_loop(negligible, lambda j: j + 1, 0)

        def block(j, dq, diag):
            k = k_ref[0, j]
            p = jnp.exp(_dot_nt(q, k) + fq - fr_ref[0, j] - lse_q)
            if diag:
                row, col = _causal_iota()
                p = jnp.where(col <= row, p, 0.0)
            ds = p * (_dot_nt(do_b, v_ref[0, j]) - dd)
            ds_b = ds.astype(BF16)
            dkv_ref[0, j] += _dot_tn(jnp.concatenate([ds_b, p.astype(BF16)], axis=0), rhs)
            dfk_ref[0, j] -= jnp.sum(ds, axis=0, keepdims=True)
            return dq + _dot(ds_b, k)

        dq = lax.fori_loop(first, i, lambda j, c: block(j, c, False), jnp.zeros((BQ, dh), F32))
        dq_ref[0] = block(i, dq, True)

        if scatter is not None:
            @pl.when(last_step)
            def _():
                _chip_scatter(g_ref, land_ref, *refs[16:])[1]()

    qin, kin, qin2, _, qspec, kspec2 = _head_specs(nj, 0)
    cspec = pl.BlockSpec((1, BQ, 1), lambda h, i: (h, i, 0))
    rspec = pl.BlockSpec((1, nj, 1, BQ), lambda h, i: (h, 0, 0, 0))
    in_specs = [qin, kin, kin, qin2, qin2, cspec, rspec, qspec, qin, cspec]
    out_specs = [qspec, kspec2, rspec]
    out_shape = [jax.ShapeDtypeStruct((nh, t, dh), F32), jax.ShapeDtypeStruct((nh, nj, BQ, 2 * dh), F32),
                 jax.ShapeDtypeStruct((nh, nj, 1, BQ), F32)]
    scratch = [pltpu.VMEM((8, 128), F32)]
    args = [qs, kn4, v4, qa, doa, fcol, frow4, o, do, lse]
    if scatter is not None:
        in_specs.append(_ANY)
        out_specs.append(_ANY)
        out_shape.append(jax.ShapeDtypeStruct(scatter.shape, scatter.dtype))
        scratch += _chip_sems()
        args.append(scatter)
    return pl.pallas_call(body, grid=(nh, nj), in_specs=in_specs, out_specs=out_specs, out_shape=out_shape,
                          scratch_shapes=scratch, compiler_params=_params(), name=name)(*args)


def _sb_logs(z, diag):
    e = jnp.exp(-jnp.abs(z))
    sp = jnp.log(1.0 + e)
    logb = jnp.minimum(z, 0.0) - sp
    lom = -jnp.maximum(z, 0.0) - sp
    strict = None
    if diag:
        row, col = _causal_iota()
        strict = col < row
        lom = jnp.where(strict, lom, 0.0)
    return logb, lom, e, strict


def sb_fwd(qs, kn4, va4, tri, *, name, gather=None):
    _, t, dh = qs.shape
    nh = H_SB
    nj = t // BQ
    assert nj <= 128

    def body(*refs):
        if gather is None:
            q_ref, k_ref, v_ref, tri_ref, o_ref, rs_ref = refs
        else:
            q_ref, k_ref, v_ref, tri_ref, src_ref, o_ref, rs_ref, dst_ref = refs[:8]
            first_step, last_step = _first_and_last_step()

            @pl.when(first_step)
            def _():
                _chip_gather(src_ref, dst_ref, *refs[8:])[0]()

        i = pl.program_id(1)
        q = q_ref[0]
        tri_m = tri_ref[...]
        lane = lax.broadcasted_iota(jnp.int32, (BQ, 128), 1)

        def block(j, carry, diag):
            run, acc, rall = carry
            logb, lom, _, strict = _sb_logs(_dot_nt(q, k_ref[0, j]), diag)
            hi, lo = _split2(lom)
            w = jnp.exp(logb + (_dot(hi, tri_m) + _dot(lo, tri_m)) + run)
            if diag:
                w = jnp.where(strict, w, 0.0)
            acc = acc + _dot(w.astype(BF16), v_ref[0, j])
            rall = jnp.where(lane == j, run, rall)
            return run + jnp.sum(lom, axis=1, keepdims=True), acc, rall

        init = (jnp.zeros((BQ, 1), F32), jnp.zeros((BQ, 2 * dh), F32), jnp.full((BQ, 128), NEG, F32))
        carry = block(i, init, True)

        def cond(c):
            n, carry = c
            return jnp.logical_and(n < i, jnp.max(carry[0]) >= STOP)

        _, (_, acc, rall) = lax.while_loop(cond, lambda c: (c[0] + 1, block(i - 1 - c[0], c[1], False)), (0, carry))
        o_ref[0] = acc[:, :dh].astype(BF16)
        rs_ref[0] = rall

        if gather is not None:
            @pl.when(last_step)
            def _():
                _chip_gather(src_ref, dst_ref, *refs[8:])[1]()

    qin, kin, _, kin2, qspec, _ = _head_specs(nj, H_FOX)
    rspec = pl.BlockSpec((1, BQ, 128), lambda h, i: (h, i, 0))
    in_specs = [qin, kin, kin2, pl.BlockSpec((BQ, BQ), lambda h, i: (0, 0))]
    out_specs = [qspec, rspec]
    out_shape = [jax.ShapeDtypeStruct((nh, t, dh), BF16), jax.ShapeDtypeStruct((nh, t, 128), F32)]
    scratch = []
    args = [qs, kn4, va4, tri]
    if gather is not None:
        in_specs.append(_ANY)
        out_specs.append(_ANY)
        out_shape.append(jax.ShapeDtypeStruct((N_CHIPS,) + gather.shape, gather.dtype))
        scratch += _chip_sems()
        args.append(gather)
    return pl.pallas_call(body, grid=(nh, nj), in_specs=in_specs, out_specs=out_specs, out_shape=out_shape,
                          scratch_shapes=scratch, compiler_params=_params(), name=name)(*args)


def sb_bwd(qs, kn4, v4, qa, doa, tri, do, rsave, *, name):
    _, t, dh = qs.shape
    nh = H_SB
    nj = t // BQ

    def body(q_ref, k_ref, v_ref, qa_ref, doa_ref, tri_ref, do_ref, rs_ref, dq_ref, dkv_ref):
        i = pl.program_id(1)

        @pl.when(i == 0)
        def _():
            dkv_ref[...] = jnp.zeros_like(dkv_ref)

        q = q_ref[0]
        do_b = do_ref[0]
        tri_m = tri_ref[...]
        rall = rs_ref[0]
        lane = lax.broadcasted_iota(jnp.int32, (BQ, 128), 1)
        rhs = jnp.concatenate([qa_ref[0], doa_ref[0]], axis=0)
        lane1 = lax.broadcasted_iota(jnp.int32, (1, 128), 1)
        unvisited = jnp.logical_and(lane1 < i, jnp.max(rall, axis=0, keepdims=True) < STOP)
        first = jnp.sum(unvisited.astype(jnp.int32))

        def block(j, carry, diag):
            dq, ecar = carry
            k = k_ref[0, j]
            z = _dot_nt(q, k)
            logb, lom, e, strict = _sb_logs(z, diag)
            hi, lo = _split2(lom)
            run = jnp.sum(jnp.where(lane == j, rall, 0.0), axis=1, keepdims=True)
            w = jnp.exp(logb + (_dot(hi, tri_m) + _dot(lo, tri_m)) + run)
            if diag:
                w = jnp.where(strict, w, 0.0)
            da = w * _dot_nt(do_b, v_ref[0, j])
            before = _dot_nt(da.astype(BF16), tri_m) + ecar
            inv = 1.0 / (1.0 + e)
            beta = jnp.where(z >= 0.0, 1.0, e) * inv
            one_minus = jnp.where(z >= 0.0, e, 1.0) * inv
            dz = da * one_minus - before * beta
            if diag:
                dz = jnp.where(strict, dz, 0.0)
            dz_b = dz.astype(BF16)
            dkv_ref[0, j] += _dot_tn(jnp.concatenate([dz_b, w.astype(BF16)], axis=0), rhs)
            return dq + _dot(dz_b, k), ecar + jnp.sum(da, axis=1, keepdims=True)

        carry = lax.fori_loop(first, i, lambda j, c: block(j, c, False),
                              (jnp.zeros((BQ, dh), F32), jnp.zeros((BQ, 1), F32)))
        dq, _ = block(i, carry, True)
        dq_ref[0] = dq

    qin, kin, qin2, _, qspec, kspec2 = _head_specs(nj, H_FOX)
    return pl.pallas_call(
        body, grid=(nh, nj),
        in_specs=[qin, kin, kin, qin2, qin2, pl.BlockSpec((BQ, BQ), lambda h, i: (0, 0)), qin,
                  pl.BlockSpec((1, BQ, 128), lambda h, i: (h, i, 0))],
        out_specs=[qspec, kspec2],
        out_shape=[jax.ShapeDtypeStruct((nh, t, dh), F32), jax.ShapeDtypeStruct((nh, nj, BQ, 2 * dh), F32)],
        compiler_params=_params(), name=name)(qs, kn4, v4, qa, doa, tri, do, rsave)


def loss_head(y, target, *, name, tm=512):
    t, d = y.shape

    def body(y_ref, t_ref, l_ref, dy_ref, acc_ref):
        i = pl.program_id(0)
        diff = y_ref[...] - t_ref[...]
        dy_ref[...] = diff * (1.0 / d)
        part = jnp.sum(diff * diff, axis=0, keepdims=True)

        @pl.when(i == 0)
        def _():
            acc_ref[...] = part

        @pl.when(i > 0)
        def _():
            acc_ref[...] += part

        @pl.when(i == pl.num_programs(0) - 1)
        def _():
            l_ref[...] = jnp.full(l_ref.shape, (0.5 / d) * jnp.sum(acc_ref[...]), F32)

    row = pl.BlockSpec((tm, d), lambda i: (i, 0))
    return pl.pallas_call(
        body, grid=(t // tm,), in_specs=[row, row],
        out_specs=[pl.BlockSpec((8, 128), lambda i: (0, 0)), row],
        out_shape=[jax.ShapeDtypeStruct((8, 128), F32), jax.ShapeDtypeStruct((t, d), F32)],
        scratch_shapes=[pltpu.VMEM((1, d), F32)], compiler_params=_params(), name=name)(y, target)


def _to_heads(a):
    t = a.shape[0]
    return a.reshape(t, N_HEADS, HEAD_DIM).transpose(1, 0, 2)


def _from_heads(a):
    t = a.shape[1]
    return a.transpose(1, 0, 2).reshape(t, MIX)


def _lanes_to_chunks(a):
    r, t = a.shape
    return a.reshape(r, t // 128, 128).transpose(1, 0, 2)


def _chunks_to_lanes(a):
    nc, r, _ = a.shape
    return a.transpose(1, 0, 2).reshape(r, nc * 128)


def _constants():
    idx = jnp.arange(128)
    bd = (idx[:, None] // HEAD_DIM == idx[None, :] // HEAD_DIM).astype(BF16)
    tri_le = (idx[:, None] <= idx[None, :]).astype(BF16)
    tri_ge = (idx[:, None] >= idx[None, :]).astype(BF16)
    jdx = jnp.arange(BQ)
    tri_gt = (jdx[:, None] > jdx[None, :]).astype(BF16)
    return dict(bd=bd, tri_le=tri_le, tri_ge=tri_ge, tri_gt=tri_gt)


def attn_layer_fwd(h, w, cst, gather=None):
    t = h.shape[0]
    nj = t // BQ
    xn = rms_fwd(h, w["norm"], name="rms_fwd")
    proj = mm_nn(xn, w["w_in"], tn=640, name="attn_in_proj")
    qs, kn, vb, logf = attn_prep_fwd(proj, w["gq"], w["gk"], w["fbias"], cst["bd"], name="attn_prep_fwd")
    logf3 = _lanes_to_chunks(logf)
    cum = _chunks_to_lanes(gate_cumsum(logf3, cst["tri_le"], name="gate_cumsum"))
    fcol = cum.reshape(H_FOX, t, 1)
    frow4 = cum.reshape(H_FOX, nj, 1, BQ)
    qh = _to_heads(qs)
    kh4 = _to_heads(kn).reshape(N_HEADS, nj, BQ, HEAD_DIM)
    vh4 = _to_heads(vb).reshape(N_HEADS, nj, BQ, HEAD_DIM)
    ones = jnp.ones(vh4.shape[:-1] + (1,), BF16)
    va4 = jnp.concatenate([vh4, ones, jnp.zeros(vh4.shape[:-1] + (HEAD_DIM - 1,), BF16)], axis=-1)
    if gather is None:
        (o_f, lse), (o_s, rsave), gathered = (fox_fwd(qh, kh4, va4, fcol, frow4, name="fox_fwd"),
                                              sb_fwd(qh, kh4, va4, cst["tri_gt"], name="sb_fwd"), None)
    else:
        o_f, lse, gathered_a = fox_fwd(qh, kh4, va4, fcol, frow4, name="fox_fwd_gather", gather=gather[0])
        o_s, rsave, gathered_b = sb_fwd(qh, kh4, va4, cst["tri_gt"], name="sb_fwd_gather", gather=gather[1])
        gathered = (gathered_a, gathered_b)
    o = _from_heads(jnp.concatenate([o_f.astype(BF16), o_s], axis=0))
    h2 = mm_nn(o, w["w_out"], add=h, name="mix_out_proj")
    saved = dict(h=h, xn=xn, proj=proj, logf3=logf3, fcol=fcol, frow4=frow4, qh=qh, kh4=kh4, vh4=vh4,
                 o_f=o_f, lse=lse, rsave=rsave, o=o)
    return h2, saved, gathered


def attn_layer_bwd(dh, w, s, cst, scatter=None):
    t = dh.shape[0]
    dh3 = dh[None]
    do = mm_nt(dh3, w["w_out"], out_dtype=BF16, name="mix_out_bwd_bf16")
    g_w_out = mm_tn(s["o"], dh3, name="mix_out_wgrad")
    doh = _to_heads(do)
    zeros = jnp.zeros_like(doh)
    qa = jnp.concatenate([s["qh"], zeros], axis=-1)
    doa = jnp.concatenate([zeros, doh], axis=-1)
    fox_args = (s["qh"], s["kh4"], s["vh4"], qa, doa, s["fcol"], s["frow4"], s["o_f"], doh, s["lse"])
    if scatter is None:
        (dq_f, dkv_f, dfk), landed = fox_bwd(*fox_args, name="fox_bwd"), None
    else:
        dq_f, dkv_f, dfk, landed = fox_bwd(*fox_args, name="fox_bwd_scatter", scatter=scatter)
    dq_s, dkv_s = sb_bwd(s["qh"], s["kh4"], s["vh4"], qa, doa, cst["tri_gt"], doh, s["rsave"], name="sb_bwd")
    dqs = _from_heads(jnp.concatenate([dq_f, dq_s], axis=0))
    dkv = jnp.concatenate([dkv_f, dkv_s], axis=0).reshape(N_HEADS, t, 2 * HEAD_DIM)
    dkn = _from_heads(dkv[:, :, :HEAD_DIM])
    dv = _from_heads(dkv[:, :, HEAD_DIM:])
    dcum3 = _lanes_to_chunks(dfk.reshape(H_FOX, t))
    dfl3, dbias = gate_cumsum_bwd(dcum3, s["logf3"], cst["tri_ge"], name="gate_cumsum_bwd")
    dfl = jnp.pad(_chunks_to_lanes(dfl3).T, ((0, 0), (0, 128 - H_FOX))).astype(BF16)
    dproj, dgq, dgk = attn_prep_bwd(s["proj"], dqs, dkn, dv, dfl, w["gq"], w["gk"], cst["bd"], name="attn_prep_bwd")
    g_w_in = mm_tn(s["xn"], dproj[None], tn=640, name="attn_in_wgrad")[:, :ATTN_IN]
    dh2, g_norm = mm_nt_rms_bwd(dproj[None], w["w_in"], s["h"], w["norm"], dh, name="attn_in_bwd")
    dgq = dgq.reshape(N_HEADS, HEAD_DIM)
    dgk = dgk.reshape(N_HEADS, HEAD_DIM)
    grads = dict(norm=g_norm[0], w_in=g_w_in, f_bias=dbias[:, 0], fox_q=dgq[:H_FOX].sum(0), fox_k=dgk[:H_FOX].sum(0),
                 sb_q=dgq[H_FOX:].sum(0), sb_k=dgk[H_FOX:].sum(0), w_out=g_w_out)
    return dh2, grads, landed


def conv_layer_fwd(h, w):
    xn = rms_fwd(h, w["norm"], name="rms_fwd")
    proj3 = mm_nn(xn, w["w_in"], parts=3, name="conv_in_proj")
    y = conv_mix_fwd(proj3, w["ck"], name="conv_mix_fwd")
    h2 = mm_nn(y, w["w_out"], add=h, name="mix_out_proj")
    return h2, dict(h=h, xn=xn, proj3=proj3, y=y)


def conv_layer_bwd(dh, w, s):
    dh3 = dh[None]
    g_w_out = mm_tn(s["y"], dh3, name="mix_out_wgrad")
    dproj3, dck = conv_mix_bwd(dh, w["w_out"], s["proj3"], w["ck"], name="conv_mix_bwd")
    g_w_in = mm_tn(s["xn"], dproj3, name="conv_in_wgrad")
    dh2, g_norm = mm_nt_rms_bwd(dproj3, w["w_in"], s["h"], w["norm"], dh, name="conv_in_bwd")
    return dh2, dict(norm=g_norm[0], w_in=g_w_in, ck=dck[0, :3], w_out=g_w_out)


def ffn_layer_fwd(h, w):
    xn = rms_fwd(h, w["norm"], name="rms_fwd")
    up2 = mm_nn(xn, w["w_up"], parts=2, tn=1408, name="ffn_up_proj")
    h2, act = ffn_act_down_fwd(up2, w["cw2"], w["w_down"], h, name="ffn_act_down_fwd")
    return h2, dict(h=h, xn=xn, up2=up2, act=act)


def ffn_layer_bwd(dh, w, s):
    dh3 = dh[None]
    g_w_down = mm_tn(s["act"], dh3, tk=1408, name="ffn_down_wgrad")
    dup2, dcw = ffn_act_bwd(dh, w["w_down"], s["up2"], w["cw2"], name="ffn_act_bwd")
    g_w_up = mm_tn(s["xn"], dup2, tn=1408, name="ffn_up_wgrad")
    dh2, g_norm = mm_nt_rms_bwd(dup2, w["w_up"], s["h"], w["norm"], dh, name="ffn_up_bwd")
    g_cw = jnp.concatenate([dcw[0, :3], dcw[1, :3]], axis=1)
    return dh2, dict(norm=g_norm[0], w_up=g_w_up, cw=g_cw, w_down=g_w_down)


def forward_backward(x, target, wa, wc, wf, *, late_weights=None, late_chunks=None):
    cst = _constants()
    h = x
    saved = []
    layer = 0
    while layer == 0 or layer < len(wf):
        i = layer // 2
        if layer % 2 == 0:
            h, sm, gathered = attn_layer_fwd(h, wa[i], cst, gather=late_weights[0] if late_weights and layer == 0 else None)
            if gathered is not None:
                wa, wc, wf = late_weights[1](gathered)
        else:
            h, sm = conv_layer_fwd(h, wc[i])
        h, sf = ffn_layer_fwd(h, wf[layer])
        saved.append((sm, sf))
        layer += 1
    depth = len(wf)
    loss_blk, dh = loss_head(h, target, name="loss_head")
    ga, gc, gf = [None] * len(wa), [None] * len(wc), [None] * depth
    landed = None
    for layer in reversed(range(depth)):
        i = layer // 2
        sm, sf = saved[layer]
        dh, gf[layer] = ffn_layer_bwd(dh, wf[layer], sf)
        if layer % 2 == 0:
            chunks = late_chunks(ga, gc, gf) if late_chunks and layer == 0 else None
            dh, ga[i], got = attn_layer_bwd(dh, wa[i], sm, cst, scatter=chunks)
            landed = got if got is not None else landed
        else:
            dh, gc[i] = conv_layer_bwd(dh, wc[i], sm)
    return loss_blk, dh, ga, gc, gf, landed


def _part_rows(shape, width, row_mult):
    n = 1
    for s in shape:
        n *= s
    rows = -(-n // width)
    return -(-rows // row_mult) * row_mult


def _pack_rows(arrs, width, row_mult, dtype, total_rows=None):
    parts = []
    used = 0
    for a in arrs:
        rows = _part_rows(a.shape, width, row_mult)
        flat = a.astype(dtype).reshape(-1)
        flat = jnp.pad(flat, (0, rows * width - flat.shape[0]))
        parts.append(flat.reshape(rows, width))
        used += rows
    if total_rows is not None and total_rows > used:
        parts.append(jnp.zeros((total_rows - used, width), dtype))
    return jnp.concatenate(parts, axis=0)


def _unpack_rows(packed, shapes, width, row_mult):
    out = []
    off = 0
    for shape in shapes:
        rows = _part_rows(shape, width, row_mult)
        n = 1
        for s in shape:
            n *= s
        out.append(packed[off:off + rows].reshape(-1)[:n].reshape(shape))
        off += rows
    return out


BIG_NAMES = ("attn_w_in", "attn_w_out", "conv_w_in", "conv_w_out", "ffn_w_up", "ffn_w_down")
BIG_AXIS = {"attn_w_in": 2, "attn_w_out": 1, "conv_w_in": 2, "conv_w_out": 1, "ffn_w_up": 2, "ffn_w_down": 1}
BIG_WIDTH = 1024
BIG_ROW_MULT = 16
BIG_TILE = 512
SMALL_TILE = 128
SMALL_SHARDED = ("conv_norm", "conv_kernel", "ffn_conv")
SMALL_AXIS = {"conv_norm": 1, "conv_kernel": 2, "ffn_conv": 2}
SMALL_REPLICATED = ("attn_norm", "attn_f_bias", "fox_q_gain", "fox_k_gain", "sb_q_gain", "sb_k_gain", "ffn_norm")
WEIGHT_ORDER = ("attn_norm", "attn_w_in", "attn_f_bias", "fox_q_gain", "fox_k_gain", "sb_q_gain", "sb_k_gain",
                "attn_w_out", "conv_norm", "conv_w_in", "conv_kernel", "conv_w_out", "ffn_norm", "ffn_w_up",
                "ffn_conv", "ffn_w_down")


def _big_total_rows(shapes):
    used = sum(_part_rows(s, BIG_WIDTH, BIG_ROW_MULT) for s in shapes)
    tile = BIG_TILE if used >= 8 * BIG_TILE else SMALL_TILE
    return -(-used // tile) * tile


def _place():
    x, y, c = lax.axis_index("x"), lax.axis_index("y"), lax.axis_index("c")
    other_chips = [(1 - x, y), (x, 1 - y), (1 - x, 1 - y)]
    return x, y, c, other_chips


_ANY = pl.BlockSpec(memory_space=pl.ANY)


def _chip_sems():
    return [pltpu.SemaphoreType.DMA((3,)), pltpu.SemaphoreType.DMA((3,)), pltpu.SemaphoreType.DMA]


def _chip_gather(src_ref, dst_ref, send_sems, recv_sems, local_sem):
    x, y, c, chips = _place()
    k = 2 * x + y

    def copy(j, slot):
        px, py = chips[j]
        return pltpu.make_async_remote_copy(src_ref=src_ref, dst_ref=dst_ref.at[slot], send_sem=send_sems.at[j],
                                            recv_sem=recv_sems.at[j], device_id=(px, py, c), device_id_type=MESH)

    def local():
        return pltpu.make_async_copy(src_ref, dst_ref.at[k], local_sem)

    def start():
        local().start()
        for j in range(3):
            copy(j, k).start()

    def finish():
        for j, (px, py) in enumerate(chips):
            copy(j, 2 * px + py).wait_recv()
        for j in range(3):
            copy(j, k).wait_send()
        local().wait()

    return start, finish


def _chip_scatter(g_ref, o_ref, send_sems, recv_sems, local_sem):
    x, y, c, chips = _place()
    k = 2 * x + y

    def copy(j, src_slot, dst_slot):
        px, py = chips[j]
        return pltpu.make_async_remote_copy(src_ref=g_ref.at[src_slot], dst_ref=o_ref.at[dst_slot],
                                            send_sem=send_sems.at[j], recv_sem=recv_sems.at[j],
                                            device_id=(px, py, c), device_id_type=MESH)

    def local():
        return pltpu.make_async_copy(g_ref.at[k], o_ref.at[k], local_sem)

    def start():
        local().start()
        for j, (px, py) in enumerate(chips):
            copy(j, 2 * px + py, k).start()

    def finish():
        for j, (px, py) in enumerate(chips):
            copy(j, k, 2 * px + py).wait_recv()
        for j, (px, py) in enumerate(chips):
            copy(j, 2 * px + py, k).wait_send()
        local().wait()

    return start, finish


def gather_chips(arrs, *, name):
    n = len(arrs)

    def body(*refs):
        hooks = [_chip_gather(refs[m], refs[n + m], *refs[2 * n + 3 * m:2 * n + 3 * m + 3]) for m in range(n)]
        for start, _ in hooks:
            start()
        for _, finish in hooks:
            finish()

    return pl.pallas_call(
        body, in_specs=[_ANY] * n, out_specs=[_ANY] * n,
        out_shape=[jax.ShapeDtypeStruct((N_CHIPS,) + a.shape, a.dtype) for a in arrs],
        scratch_shapes=_chip_sems() * n, name=name)(*arrs)


def scatter_chips(chunks, *, name):
    def body(g_ref, o_ref, send_sems, recv_sems, local_sem):
        start, finish = _chip_scatter(g_ref, o_ref, send_sems, recv_sems, local_sem)
        start()
        finish()

    return pl.pallas_call(
        body, in_specs=[_ANY], out_specs=_ANY, out_shape=jax.ShapeDtypeStruct(chunks.shape, chunks.dtype),
        scratch_shapes=_chip_sems(), name=name)(chunks)


def swap_cores(arrs, *, name):
    n = len(arrs)

    def body(*refs):
        x, y, c, _ = _place()
        copies = [pltpu.make_async_remote_copy(src_ref=refs[m], dst_ref=refs[n + m], send_sem=refs[2 * n + 2 * m],
                                               recv_sem=refs[2 * n + 2 * m + 1], device_id=(x, y, 1 - c),
                                               device_id_type=MESH) for m in range(n)]
        for cp in copies:
            cp.start()
        for cp in copies:
            cp.wait()

    return pl.pallas_call(
        body, in_specs=[_ANY] * n, out_specs=[_ANY] * n,
        out_shape=[jax.ShapeDtypeStruct(a.shape, a.dtype) for a in arrs],
        scratch_shapes=[pltpu.SemaphoreType.DMA, pltpu.SemaphoreType.DMA] * n, name=name)(*arrs)


def allreduce_small(p, *, name):
    r, w = p.shape

    def body(p_ref, o_ref, buf, send_sems, recv_sems):
        x, y, c, _ = _place()
        me = 4 * x + 2 * y + c
        buf[me] = p_ref[...]

        def peer_of(m):
            return (1 - x if m & 4 else x, 1 - y if m & 2 else y, 1 - c if m & 1 else c)

        def copy(m, slot):
            return pltpu.make_async_remote_copy(src_ref=p_ref, dst_ref=buf.at[slot], send_sem=send_sems.at[m - 1],
                                                recv_sem=recv_sems.at[m - 1], device_id=peer_of(m),
                                                device_id_type=MESH)

        sends = [copy(m, me) for m in range(1, 8)]
        for cp in sends:
            cp.start()
        for m in range(1, 8):
            px, py, pc = peer_of(m)
            copy(m, 4 * px + 2 * py + pc).wait_recv()
        for cp in sends:
            cp.wait_send()
        acc = buf[0]
        for d in range(1, 8):
            acc = acc + buf[d]
        o_ref[...] = acc

    vm = pl.BlockSpec(memory_space=pltpu.VMEM)
    return pl.pallas_call(
        body, in_specs=[vm], out_specs=vm, out_shape=jax.ShapeDtypeStruct((r, w), F32),
        scratch_shapes=[pltpu.VMEM((8, r, w), F32), pltpu.SemaphoreType.DMA((7,)), pltpu.SemaphoreType.DMA((7,))],
        name=name)(p)


def sum_chips(rv, *, name):
    _, r, w = rv.shape
    tile = BIG_TILE if r % BIG_TILE == 0 else SMALL_TILE

    def body(a_ref, b_ref, c_ref, d_ref, o_ref):
        o_ref[...] = ((a_ref[0].astype(F32) + b_ref[0].astype(F32)) + c_ref[0].astype(F32)) + d_ref[0].astype(F32)

    spec = lambda kk: pl.BlockSpec((1, tile, w), lambda i: (kk, i, 0))
    return pl.pallas_call(
        body, grid=(r // tile,), in_specs=[spec(0), spec(1), spec(2), spec(3)],
        out_specs=pl.BlockSpec((tile, w), lambda i: (i, 0)), out_shape=jax.ShapeDtypeStruct((r, w), F32),
        compiler_params=_params(), name=name)(rv, rv, rv, rv)


def add_pair(a, b, *, name):
    r, w = a.shape
    tile = BIG_TILE if r % BIG_TILE == 0 else SMALL_TILE

    def body(a_ref, b_ref, o_ref):
        o_ref[...] = a_ref[...] + b_ref[...]

    spec = pl.BlockSpec((tile, w), lambda i: (i, 0))
    return pl.pallas_call(body, grid=(r // tile,), in_specs=[spec, spec], out_specs=spec,
                          out_shape=jax.ShapeDtypeStruct((r, w), F32), compiler_params=_params(), name=name)(a, b)


def adamw(w, g, m, v, *, tm, name):
    r, c = w.shape
    assert r % tm == 0

    def body(w_ref, g_ref, m_ref, v_ref, d_ref, nm_ref, nv_ref):
        g_ = g_ref[...]
        m_ = ADAM_B1 * m_ref[...] + (1.0 - ADAM_B1) * g_
        v_ = ADAM_B2 * v_ref[...] + (1.0 - ADAM_B2) * (g_ * g_)
        m_hat = m_ / (1.0 - ADAM_B1 ** ADAM_STEP)
        v_hat = v_ / (1.0 - ADAM_B2 ** ADAM_STEP)
        d_ref[...] = -ADAM_LR * (m_hat / (jnp.sqrt(v_hat) + ADAM_EPS) + ADAM_WD * w_ref[...])
        nm_ref[...] = m_
        nv_ref[...] = v_

    spec = pl.BlockSpec((tm, c), lambda i: (i, 0))
    return pl.pallas_call(body, grid=(r // tm,), in_specs=[spec] * 4, out_specs=[spec] * 3,
                          out_shape=[jax.ShapeDtypeStruct((r, c), F32)] * 3, compiler_params=_params(), name=name)(w, g, m, v)


def kernel(x, attn_norm, attn_w_in, attn_f_bias, fox_q_gain, fox_k_gain, sb_q_gain, sb_k_gain, attn_w_out, conv_norm, conv_w_in, conv_kernel, conv_w_out, ffn_norm, ffn_w_up, ffn_conv, ffn_w_down, loss_target, m_attn_norm, m_attn_w_in, m_attn_f_bias, m_fox_q_gain, m_fox_k_gain, m_sb_q_gain, m_sb_k_gain, m_attn_w_out, m_conv_norm, m_conv_w_in, m_conv_kernel, m_conv_w_out, m_ffn_norm, m_ffn_w_up, m_ffn_conv, m_ffn_w_down, v_attn_norm, v_attn_w_in, v_attn_f_bias, v_fox_q_gain, v_fox_k_gain, v_sb_q_gain, v_sb_k_gain, v_attn_w_out, v_conv_norm, v_conv_w_in, v_conv_kernel, v_conv_w_out, v_ffn_norm, v_ffn_w_up, v_ffn_conv, v_ffn_w_down):
    a = dict(locals())
    chip = 2 * lax.axis_index("x") + lax.axis_index("y")
    n_attn, n_conv, depth = attn_norm.shape[0], conv_norm.shape[0], ffn_norm.shape[0]

    units = [(name, l) for name in BIG_NAMES for l in range(a[name].shape[0])]
    early = [("attn_w_in", 0), ("attn_w_out", 0)]
    late = [u for u in units if u not in early]
    late_b = [("conv_w_in", n_conv - 1), ("conv_w_out", n_conv - 1), ("ffn_w_up", depth - 1), ("ffn_w_down", depth - 1)]
    late_a = [u for u in late if u not in late_b]

    def unit_shape(u):
        return a[u[0]].shape[1:]

    def pack_units(us, get):
        return _pack_rows([get(u) for u in us], BIG_WIDTH, BIG_ROW_MULT, BF16, _big_total_rows([unit_shape(u) for u in us]))

    def unpack_units(packed, us):
        return dict(zip(us, _unpack_rows(packed, [unit_shape(u) for u in us], BIG_WIDTH, BIG_ROW_MULT)))

    def full_units(gathered, us):
        per_chip = [unpack_units(gathered[kk], us) for kk in range(N_CHIPS)]
        return {u: jnp.concatenate([per_chip[kk][u] for kk in range(N_CHIPS)], axis=BIG_AXIS[u[0]] - 1) for u in us}

    def shard(u):
        return a[u[0]][u[1]]

    small_shapes = [a[n].shape for n in SMALL_SHARDED]
    packed_s = _pack_rows([a[n] for n in SMALL_SHARDED], 128, 8, F32)
    gath_e, gath_s = gather_chips([pack_units(early, shard), packed_s], name="gather_weights")
    full_e = full_units(gath_e, early)
    full = {}
    per_chip = [_unpack_rows(gath_s[kk], small_shapes, 128, 8) for kk in range(N_CHIPS)]
    for n, name in enumerate(SMALL_SHARDED):
        full[name] = jnp.concatenate([per_chip[kk][n] for kk in range(N_CHIPS)], axis=SMALL_AXIS[name])

    def attn_weights(i, fu):
        return dict(
            norm=attn_norm[i][None],
            w_in=jnp.pad(fu[("attn_w_in", i)], ((0, 0), (0, ATTN_IN_PAD - ATTN_IN))),
            fbias=jnp.pad(attn_f_bias[i], (0, 128 - H_FOX))[None],
            gq=jnp.concatenate([jnp.tile(fox_q_gain[i], H_FOX), jnp.tile(sb_q_gain[i], H_SB)])[None],
            gk=jnp.concatenate([jnp.tile(fox_k_gain[i], H_FOX), jnp.tile(sb_k_gain[i], H_SB)])[None],
            w_out=fu[("attn_w_out", i)])

    def build_weights(gathered):
        fu = {**full_e, **full_units(gathered[0], late_a), **full_units(gathered[1], late_b)}
        wa = [attn_weights(i, fu) for i in range(n_attn)]
        wc = [dict(norm=full["conv_norm"][i][None], w_in=fu[("conv_w_in", i)], ck=full["conv_kernel"][i][None],
                   w_out=fu[("conv_w_out", i)]) for i in range(n_conv)]
        wf = []
        for l in range(depth):
            cw = full["ffn_conv"][l]
            wf.append(dict(norm=ffn_norm[l][None], w_up=fu[("ffn_w_up", l)], cw2=jnp.stack([cw[:, :D_FF], cw[:, D_FF:]]),
                           w_down=fu[("ffn_w_down", l)]))
        return wa, wc, wf

    def chunk_of(u, kk, ga, gc, gf):
        name, l = u
        g = {"attn_w_in": lambda: ga[l]["w_in"], "attn_w_out": lambda: ga[l]["w_out"],
             "conv_w_in": lambda: gc[l]["w_in"], "conv_w_out": lambda: gc[l]["w_out"],
             "ffn_w_up": lambda: gf[l]["w_up"], "ffn_w_down": lambda: gf[l]["w_down"]}[name]()
        width = a[name].shape[BIG_AXIS[name]]
        return lax.slice_in_dim(g, kk * width, (kk + 1) * width, axis=BIG_AXIS[name] - 1)

    def chunks_of(us, ga, gc, gf):
        return jnp.stack([pack_units(us, lambda u: chunk_of(u, kk, ga, gc, gf)) for kk in range(N_CHIPS)])

    loss_blk, grad_x, ga, gc, gf, landed_late = forward_backward(
        x[0], loss_target[0], [attn_weights(0, full_e)], [], [],
        late_weights=((pack_units(late_a, shard), pack_units(late_b, shard)), build_weights),
        late_chunks=lambda ga, gc, gf: chunks_of(late, ga, gc, gf))

    landed_early = scatter_chips(chunks_of(early, ga, gc, gf), name="scatter_grads")
    mine = [sum_chips(landed_early, name="sum_chips"), sum_chips(landed_late, name="sum_chips")]
    theirs = swap_cores(mine, name="swap_cores")
    g_units = {**unpack_units(add_pair(mine[0], theirs[0], name="add_cores"), early),
               **unpack_units(add_pair(mine[1], theirs[1], name="add_cores"), late)}
    grads = {name: jnp.stack([g_units[(name, l)] for l in range(a[name].shape[0])]) for name in BIG_NAMES}

    small_full = [
        loss_blk,
        jnp.stack([g["norm"] for g in ga]), jnp.stack([g["f_bias"] for g in ga]),
        jnp.stack([g["fox_q"] for g in ga]), jnp.stack([g["fox_k"] for g in ga]),
        jnp.stack([g["sb_q"] for g in ga]), jnp.stack([g["sb_k"] for g in ga]),
        jnp.stack([g["norm"] for g in gf]),
        jnp.stack([g["norm"] for g in gc]), jnp.stack([g["ck"] for g in gc]), jnp.stack([g["cw"] for g in gf]),
    ]
    summed = allreduce_small(_pack_rows(small_full, 128, 8, F32), name="allreduce_small")
    parts = _unpack_rows(summed, [p.shape for p in small_full], 128, 8)
    loss = parts[0][0, 0]
    for name, g in zip(SMALL_REPLICATED, parts[1:8]):
        grads[name] = g
    for name, g in zip(SMALL_SHARDED, parts[8:]):
        width = a[name].shape[SMALL_AXIS[name]]
        grads[name] = lax.dynamic_slice_in_dim(g, chip * width, width, axis=SMALL_AXIS[name])

    delta, new_m, new_v = {}, {}, {}
    for name in BIG_NAMES:
        shape = a[name].shape
        flat = lambda arr: arr.reshape(-1, shape[-1])
        d_, m_, v_ = adamw(flat(a[name]), flat(grads[name]), flat(a["m_" + name]), flat(a["v_" + name]), tm=256,
                           name="adamw")
        delta[name], new_m[name], new_v[name] = d_.reshape(shape), m_.reshape(shape), v_.reshape(shape)
    small_names = SMALL_REPLICATED + SMALL_SHARDED
    small_shapes_local = [a[n].shape for n in small_names]
    pack = lambda prefix, src: _pack_rows([src[prefix + n] for n in small_names], 128, 8, F32)
    packed = adamw(pack("", a), pack("", grads), pack("m_", a), pack("v_", a), tm=8, name="adamw_small")
    for store, buf in zip((delta, new_m, new_v), packed):
        for name, arr in zip(small_names, _unpack_rows(buf, small_shapes_local, 128, 8)):
            store[name] = arr

    return (loss, grad_x[None], *[grads[n] for n in WEIGHT_ORDER], *[delta[n] for n in WEIGHT_ORDER],
            *[new_m[n] for n in WEIGHT_ORDER], *[new_v[n] for n in WEIGHT_ORDER])
```

```python
import functools

import jax
import jax.numpy as jnp
from jax import lax
from jax.experimental import pallas as pl
from jax.experimental.pallas import tpu as pltpu

F32 = jnp.float32
BF16 = jnp.bfloat16

D_MODEL = 1024
HEAD_DIM = 64
H_FOX = 8
H_SB = 8
N_HEADS = H_FOX + H_SB
MIX = N_HEADS * HEAD_DIM
ATTN_IN = 3 * MIX + H_FOX
ATTN_IN_PAD = 3 * MIX + 128
D_FF = 2816
EPS = 1e-6
SCALE = HEAD_DIM ** -0.5
NEG = -1e30

ADAM_LR = 0.001
ADAM_B1 = 0.9
ADAM_B2 = 0.999
ADAM_EPS = 1e-08
ADAM_WD = 0.01
ADAM_STEP = 10

VMEM_LIMIT = 56 * 1024 * 1024
HALO = 8
BQ = 512
N_CHIPS = 4
MESH = pl.DeviceIdType.MESH


def _params(**kw):
    return pltpu.CompilerParams(vmem_limit_bytes=VMEM_LIMIT, **kw)


def _dot(a, b):
    return jnp.dot(a, b, preferred_element_type=F32)


def _dot_nt(a, b):
    return lax.dot_general(a, b, (((1,), (1,)), ((), ())), preferred_element_type=F32)


def _dot_tn(a, b):
    return lax.dot_general(a, b, (((0,), (0,)), ((), ())), preferred_element_type=F32)


def _split2(x):
    hi = x.astype(BF16)
    lo = (x - hi.astype(F32)).astype(BF16)
    return hi, lo


def _split3(x):
    hi = x.astype(BF16)
    r = x - hi.astype(F32)
    mid = r.astype(BF16)
    lo = (r - mid.astype(F32)).astype(BF16)
    return hi, mid, lo


def mm_nn(a, b, *, add=None, out_dtype=F32, parts=1, tm=1024, tn=512, name):
    m, k = a.shape
    n = b.shape[1]
    np_ = n // parts
    nb = np_ // tn
    tm = min(tm, m)
    assert m % tm == 0 and np_ % tn == 0

    def body(*refs):
        if add is None:
            a_ref, b_ref, o_ref = refs
            acc = _dot(a_ref[...].astype(BF16), b_ref[...])
        else:
            a_ref, b_ref, r_ref, o_ref = refs
            acc = _dot(a_ref[...].astype(BF16), b_ref[...]) + r_ref[...]
        o_ref[...] = acc.astype(out_dtype).reshape(o_ref.shape)

    in_specs = [pl.BlockSpec((tm, k), lambda i, j: (i, 0)), pl.BlockSpec((k, tn), lambda i, j: (0, j))]
    args = [a, b]
    if add is not None:
        in_specs.append(pl.BlockSpec((tm, tn), lambda i, j: (i, j)))
        args.append(add)
    if parts == 1:
        out_spec = pl.BlockSpec((tm, tn), lambda i, j: (i, j))
        out_shape = jax.ShapeDtypeStruct((m, n), out_dtype)
    else:
        out_spec = pl.BlockSpec((1, tm, tn), lambda i, j: (j // nb, i, j % nb))
        out_shape = jax.ShapeDtypeStruct((parts, m, np_), out_dtype)
    return pl.pallas_call(body, grid=(m // tm, n // tn), in_specs=in_specs, out_specs=out_spec,
                          out_shape=out_shape, compiler_params=_params(), name=name)(*args)


def mm_nt(a3, b, *, out_dtype=F32, tm=1024, tn=512, name):
    p, m, kp = a3.shape
    n = b.shape[0]
    tm = min(tm, m)
    assert m % tm == 0 and n % tn == 0 and b.shape[1] == p * kp

    def body(a_ref, b_ref, o_ref, acc_ref):
        part = pl.program_id(2)
        prod = _dot_nt(a_ref[0].astype(BF16), b_ref[...])

        @pl.when(part == 0)
        def _():
            acc_ref[...] = prod

        @pl.when(part > 0)
        def _():
            acc_ref[...] += prod

        @pl.when(part == p - 1)
        def _():
            o_ref[...] = acc_ref[...].astype(out_dtype)

    return pl.pallas_call(
        body, grid=(m // tm, n // tn, p),
        in_specs=[pl.BlockSpec((1, tm, kp), lambda i, j, q: (q, i, 0)), pl.BlockSpec((tn, kp), lambda i, j, q: (j, q))],
        out_specs=pl.BlockSpec((tm, tn), lambda i, j, q: (i, j)),
        out_shape=jax.ShapeDtypeStruct((m, n), out_dtype),
        scratch_shapes=[pltpu.VMEM((tm, tn), F32)],
        compiler_params=_params(), name=name)(a3, b)


def mm_tn(a, b3, *, tk=512, tn=512, tt=2048, name):
    t, k = a.shape
    p, _, np_ = b3.shape
    nb = np_ // tn
    tt = min(tt, t)
    assert t % tt == 0 and k % tk == 0 and np_ % tn == 0

    def body(a_ref, b_ref, o_ref):
        prod = _dot_tn(a_ref[...].astype(BF16), b_ref[0].astype(BF16))

        @pl.when(pl.program_id(2) == 0)
        def _():
            o_ref[...] = prod

        @pl.when(pl.program_id(2) > 0)
        def _():
            o_ref[...] += prod

    return pl.pallas_call(
        body, grid=(k // tk, p * nb, t // tt),
        in_specs=[pl.BlockSpec((tt, tk), lambda i, j, s: (s, i)), pl.BlockSpec((1, tt, tn), lambda i, j, s: (j // nb, s, j % nb))],
        out_specs=pl.BlockSpec((tk, tn), lambda i, j, s: (i, j)),
        out_shape=jax.ShapeDtypeStruct((k, p * np_), F32),
        compiler_params=_params(), name=name)(a, b3)


def rms_mm_nn(h, g, b, *, parts=1, tm=1024, tn=512, name):
    t, d = h.shape
    n = b.shape[1]
    np_ = n // parts
    nb = np_ // tn
    tm = min(tm, t)
    assert t % tm == 0 and np_ % tn == 0

    def body(h_ref, g_ref, b_ref, xn_ref, o_ref):
        @pl.when(pl.program_id(1) == 0)
        def _():
            x = h_ref[...]
            r = lax.rsqrt(jnp.mean(x * x, axis=-1, keepdims=True) + EPS)
            xn_ref[...] = (x * r * g_ref[...]).astype(BF16)

        o_ref[...] = _dot(xn_ref[...], b_ref[...]).reshape(o_ref.shape)

    if parts == 1:
        out_spec = pl.BlockSpec((tm, tn), lambda i, j: (i, j))
        out_shape = jax.ShapeDtypeStruct((t, n), F32)
    else:
        out_spec = pl.BlockSpec((1, tm, tn), lambda i, j: (j // nb, i, j % nb))
        out_shape = jax.ShapeDtypeStruct((parts, t, np_), F32)
    row = pl.BlockSpec((tm, d), lambda i, j: (i, 0))
    return pl.pallas_call(
        body, grid=(t // tm, n // tn),
        in_specs=[row, pl.BlockSpec((1, d), lambda i, j: (0, 0)), pl.BlockSpec((d, tn), lambda i, j: (0, j))],
        out_specs=[row, out_spec], out_shape=[jax.ShapeDtypeStruct((t, d), BF16), out_shape],
        compiler_params=_params(), name=name)(h, g, b)


def mm_nt_rms_bwd(a3, b, h, g, dres, *, name, tm=512):
    p, t, kp = a3.shape
    d = b.shape[0]
    tm = min(tm, t)
    assert t % tm == 0 and b.shape[1] == p * kp

    def body(a_ref, b_ref, h_ref, g_ref, dres_ref, dh_ref, dg_ref, acc_ref):
        i = pl.program_id(0)
        part = pl.program_id(1)
        prod = _dot_nt(a_ref[0].astype(BF16), b_ref[...])

        @pl.when(part == 0)
        def _():
            acc_ref[...] = prod

        @pl.when(part > 0)
        def _():
            acc_ref[...] += prod

        @pl.when(part == p - 1)
        def _():
            x = h_ref[...]
            dy = acc_ref[...]
            r = lax.rsqrt(jnp.mean(x * x, axis=-1, keepdims=True) + EPS)
            gy = dy * g_ref[...]
            dot = jnp.mean(gy * x, axis=-1, keepdims=True)
            dh_ref[...] = dres_ref[...] + r * gy - x * (r * r * r * dot)
            _acc_rows(dg_ref, jnp.sum(dy * x * r, axis=0, keepdims=True), i == 0)

    row = pl.BlockSpec((tm, d), lambda i, q: (i, 0))
    vec = pl.BlockSpec((1, d), lambda i, q: (0, 0))
    return pl.pallas_call(
        body, grid=(t // tm, p),
        in_specs=[pl.BlockSpec((1, tm, kp), lambda i, q: (q, i, 0)), pl.BlockSpec((d, kp), lambda i, q: (0, q)),
                  row, vec, row],
        out_specs=[row, vec],
        out_shape=[jax.ShapeDtypeStruct((t, d), F32), jax.ShapeDtypeStruct((1, d), F32)],
        scratch_shapes=[pltpu.VMEM((tm, d), F32)], compiler_params=_params(), name=name)(a3, b, h, g, dres)


def _causal3(x, w):
    return w[0:1] * pltpu.roll(x, 2, 0) + w[1:2] * pltpu.roll(x, 1, 0) + w[2:3] * x


def _causal3_taps(x_ext, w, tm):
    x2 = pltpu.roll(x_ext, 2, 0)
    x1 = pltpu.roll(x_ext, 1, 0)
    y = w[0:1] * x2 + w[1:2] * x1 + w[2:3] * x_ext
    return y, (x2[HALO:HALO + tm], x1[HALO:HALO + tm], x_ext[HALO:HALO + tm])


def _anticausal3(z, w):
    n = z.shape[0]
    return w[2:3] * z + w[1:2] * pltpu.roll(z, n - 1, 0) + w[0:1] * pltpu.roll(z, n - 2, 0)


def _prev_spec(part, tm, tc, nrow8):
    del nrow8
    return pl.BlockSpec((1, HALO, tc), lambda j, i: (part, jnp.maximum(i * (tm // HALO) - 1, 0), j))


def _next_spec(part, tm, tc, nrow8):
    return pl.BlockSpec((1, HALO, tc), lambda j, i: (part, jnp.minimum((i + 1) * (tm // HALO), nrow8 - 1), j))


def _tile_spec(part, tm, tc):
    return pl.BlockSpec((1, tm, tc), lambda j, i: (part, i, j))


def _acc_rows(ref, val, first):
    @pl.when(first)
    def _():
        ref[...] = val

    @pl.when(jnp.logical_not(first))
    def _():
        ref[...] += val


def ffn_act_down_fwd(up2, cw2, w_down, h, *, name, tm=256, tc=1408):
    _, t, f = up2.shape
    d = h.shape[1]
    tm = min(tm, t)

    def body(g_ref, v_ref, gp_ref, vp_ref, w_ref, wd_ref, h_ref, o_ref, act_ref):
        keep = jnp.where(pl.program_id(0) == 0, 0.0, 1.0)
        for cc in range(f // tc):
            cols = slice(cc * tc, (cc + 1) * tc)
            g_ext = jnp.concatenate([gp_ref[0, :, cols] * keep, g_ref[0, :, cols]], axis=0)
            v_ext = jnp.concatenate([vp_ref[0, :, cols] * keep, v_ref[0, :, cols]], axis=0)
            ug = _causal3(g_ext, w_ref[0, :, cols])[HALO:]
            uv = _causal3(v_ext, w_ref[1, :, cols])[HALO:]
            act_ref[:, cols] = (ug * jax.nn.sigmoid(ug) * uv).astype(BF16)
        o_ref[...] = _dot(act_ref[...], wd_ref[...]) + h_ref[...]

    tile = lambda part: pl.BlockSpec((1, tm, f), lambda i: (part, i, 0))
    prev = lambda part: pl.BlockSpec((1, HALO, f), lambda i: (part, jnp.maximum(i * (tm // HALO) - 1, 0), 0))
    row = pl.BlockSpec((tm, d), lambda i: (i, 0))
    return pl.pallas_call(
        body, grid=(t // tm,),
        in_specs=[tile(0), tile(1), prev(0), prev(1), pl.BlockSpec((2, 3, f), lambda i: (0, 0, 0)),
                  pl.BlockSpec((f, d), lambda i: (0, 0)), row],
        out_specs=[row, pl.BlockSpec((tm, f), lambda i: (i, 0))],
        out_shape=[jax.ShapeDtypeStruct((t, d), F32), jax.ShapeDtypeStruct((t, f), BF16)],
        compiler_params=_params(), name=name)(up2, up2, up2, up2, cw2, w_down, h)


def ffn_act_bwd(dh, w_down, up2, cw2, *, name, tm=256, tc=1408):
    _, t, f = up2.shape
    d = dh.shape[1]
    n8 = t // HALO

    def body(d_ref, dn_ref, wd_ref, g_ref, v_ref, gp_ref, vp_ref, gn_ref, vn_ref, wg_ref, wv_ref, dup_ref, dw_ref):
        i = pl.program_id(1)
        first = i == 0
        keep_p = jnp.where(first, 0.0, 1.0)
        keep_n = jnp.where(i == pl.num_programs(1) - 1, 0.0, 1.0)
        wg = wg_ref[0]
        wv = wv_ref[0]
        g_ext = jnp.concatenate([gp_ref[0] * keep_p, g_ref[0], gn_ref[0]], axis=0)
        v_ext = jnp.concatenate([vp_ref[0] * keep_p, v_ref[0], vn_ref[0]], axis=0)
        dh_ext = jnp.concatenate([d_ref[...], dn_ref[...] * keep_n], axis=0)
        d_ext = _dot_nt(dh_ext.astype(BF16), wd_ref[...])
        ug, (g2, g1, g0) = _causal3_taps(g_ext, wg, tm)
        uv, (v2, v1, v0) = _causal3_taps(v_ext, wv, tm)
        ug = ug[HALO:]
        uv = uv[HALO:]
        s = jax.nn.sigmoid(ug)
        dg = d_ext * uv * (s * (1.0 + ug * (1.0 - s)))
        dv = d_ext * (ug * s)
        dup_ref[0] = _anticausal3(dg, wg)[:tm].astype(BF16)
        dup_ref[1] = _anticausal3(dv, wv)[:tm].astype(BF16)
        dgt = dg[:tm]
        dvt = dv[:tm]
        zero = jnp.zeros((HALO - 3, tc), F32)
        rows_g = [jnp.sum(dgt * x, axis=0, keepdims=True) for x in (g2, g1, g0)] + [zero]
        rows_v = [jnp.sum(dvt * x, axis=0, keepdims=True) for x in (v2, v1, v0)] + [zero]
        _acc_rows(dw_ref, jnp.stack([jnp.concatenate(rows_g, axis=0), jnp.concatenate(rows_v, axis=0)]), first)

    wspec = lambda part: pl.BlockSpec((1, 3, tc), lambda j, i: (part, 0, j))
    return pl.pallas_call(
        body, grid=(f // tc, t // tm),
        in_specs=[pl.BlockSpec((tm, d), lambda j, i: (i, 0)),
                  pl.BlockSpec((HALO, d), lambda j, i: (jnp.minimum((i + 1) * (tm // HALO), n8 - 1), 0)),
                  pl.BlockSpec((tc, d), lambda j, i: (j, 0)),
                  _tile_spec(0, tm, tc), _tile_spec(1, tm, tc), _prev_spec(0, tm, tc, n8), _prev_spec(1, tm, tc, n8),
                  _next_spec(0, tm, tc, n8), _next_spec(1, tm, tc, n8), wspec(0), wspec(1)],
        out_specs=[pl.BlockSpec((2, tm, tc), lambda j, i: (0, i, j)), pl.BlockSpec((2, HALO, tc), lambda j, i: (0, 0, j))],
        out_shape=[jax.ShapeDtypeStruct((2, t, f), BF16), jax.ShapeDtypeStruct((2, HALO, f), F32)],
        compiler_params=_params(), name=name)(dh, dh, w_down, up2, up2, up2, up2, up2, up2, cw2, cw2)


def conv_mix_fwd(proj3, ck, *, name, tm=512, tc=512):
    _, t, c = proj3.shape
    n8 = t // HALO

    def body(b_ref, c_ref, u_ref, cp_ref, up_ref, w_ref, o_ref):
        keep = jnp.where(pl.program_id(1) == 0, 0.0, 1.0)
        cu_ext = jnp.concatenate([cp_ref[0] * up_ref[0] * keep, c_ref[0] * u_ref[0]], axis=0)
        o_ref[...] = (b_ref[0] * _causal3(cu_ext, w_ref[0])[HALO:]).astype(BF16)

    return pl.pallas_call(
        body, grid=(c // tc, t // tm),
        in_specs=[_tile_spec(0, tm, tc), _tile_spec(1, tm, tc), _tile_spec(2, tm, tc), _prev_spec(1, tm, tc, n8),
                  _prev_spec(2, tm, tc, n8), pl.BlockSpec((1, 3, tc), lambda j, i: (0, 0, j))],
        out_specs=pl.BlockSpec((tm, tc), lambda j, i: (i, j)),
        out_shape=jax.ShapeDtypeStruct((t, c), BF16), compiler_params=_params(), name=name)(proj3, proj3, proj3, proj3, proj3, ck)


def conv_mix_bwd(dh, w_out, proj3, ck, *, name, tm=512, tc=512):
    _, t, c = proj3.shape
    d = dh.shape[1]
    n8 = t // HALO

    def body(d_ref, dn_ref, wo_ref, b_ref, c_ref, u_ref, cp_ref, up_ref, bn_ref, w_ref, dp_ref, dw_ref):
        i = pl.program_id(1)
        first = i == 0
        keep_p = jnp.where(first, 0.0, 1.0)
        keep_n = jnp.where(i == pl.num_programs(1) - 1, 0.0, 1.0)
        w = w_ref[0]
        cu_ext = jnp.concatenate([cp_ref[0] * up_ref[0] * keep_p, c_ref[0] * u_ref[0]], axis=0)
        cv, (x2, x1, x0) = _causal3_taps(cu_ext, w, tm)
        cv = cv[HALO:]
        dh_ext = jnp.concatenate([d_ref[...], dn_ref[...] * keep_n], axis=0)
        d_ext = _dot_nt(dh_ext.astype(BF16), wo_ref[...])
        dyt = d_ext[:tm]
        b_ext = jnp.concatenate([b_ref[0], bn_ref[0]], axis=0)
        dcv = d_ext * b_ext
        dcu = _anticausal3(dcv, w)[:tm]
        dp_ref[0] = (dyt * cv).astype(BF16)
        dp_ref[1] = (dcu * u_ref[0]).astype(BF16)
        dp_ref[2] = (dcu * c_ref[0]).astype(BF16)
        dcvt = dcv[:tm]
        rows = [jnp.sum(dcvt * x, axis=0, keepdims=True) for x in (x2, x1, x0)] + [jnp.zeros((HALO - 3, tc), F32)]
        _acc_rows(dw_ref, jnp.concatenate(rows, axis=0)[None], first)

    return pl.pallas_call(
        body, grid=(c // tc, t // tm),
        in_specs=[pl.BlockSpec((tm, d), lambda j, i: (i, 0)),
                  pl.BlockSpec((HALO, d), lambda j, i: (jnp.minimum((i + 1) * (tm // HALO), n8 - 1), 0)),
                  pl.BlockSpec((tc, d), lambda j, i: (j, 0)),
                  _tile_spec(0, tm, tc), _tile_spec(1, tm, tc), _tile_spec(2, tm, tc),
                  _prev_spec(1, tm, tc, n8), _prev_spec(2, tm, tc, n8),
                  _next_spec(0, tm, tc, n8), pl.BlockSpec((1, 3, tc), lambda j, i: (0, 0, j))],
        out_specs=[pl.BlockSpec((3, tm, tc), lambda j, i: (0, i, j)), pl.BlockSpec((1, HALO, tc), lambda j, i: (0, 0, j))],
        out_shape=[jax.ShapeDtypeStruct((3, t, c), BF16), jax.ShapeDtypeStruct((1, HALO, c), F32)],
        compiler_params=_params(), name=name)(dh, dh, w_out, proj3, proj3, proj3, proj3, proj3, proj3, ck)


def _head_sums(x, bd):
    hi, lo = _split2(x)
    return _dot(hi, bd) + _dot(lo, bd)


def attn_prep_fwd(proj, gq, gk, fbias, bd, *, name, tm=256):
    t = proj.shape[0]

    def body(q_ref, k_ref, v_ref, f_ref, gq_ref, gk_ref, fb_ref, bd_ref, qs_ref, kn_ref, vb_ref, lf_ref):
        bd = bd_ref[...]

        def headnorm(x_ref, g_ref, o_ref, scale):
            for c in range(MIX // 128):
                sl = slice(128 * c, 128 * (c + 1))
                x = x_ref[:, sl]
                r = lax.rsqrt(_head_sums(x * x, bd) * (1.0 / HEAD_DIM) + EPS)
                o_ref[:, sl] = (x * r * (g_ref[:, sl] * scale)).astype(BF16)

        headnorm(q_ref, gq_ref, qs_ref, SCALE)
        headnorm(k_ref, gk_ref, kn_ref, 1.0)
        vb_ref[...] = v_ref[...].astype(BF16)
        fl = f_ref[...] + fb_ref[...]
        logf = jnp.minimum(fl, 0.0) - jnp.log(1.0 + jnp.exp(-jnp.abs(fl)))
        lf_ref[...] = logf.T[0:H_FOX, :]

    col = lambda c: pl.BlockSpec((tm, MIX), lambda i: (i, c))
    vec = pl.BlockSpec((1, MIX), lambda i: (0, 0))
    out = pl.BlockSpec((tm, MIX), lambda i: (i, 0))
    return pl.pallas_call(
        body, grid=(t // tm,),
        in_specs=[col(0), col(1), col(2), pl.BlockSpec((tm, 128), lambda i: (i, 3 * MIX // 128)), vec, vec,
                  pl.BlockSpec((1, 128), lambda i: (0, 0)), pl.BlockSpec((128, 128), lambda i: (0, 0))],
        out_specs=[out, out, out, pl.BlockSpec((H_FOX, tm), lambda i: (0, i))],
        out_shape=[jax.ShapeDtypeStruct((t, MIX), BF16)] * 3 + [jax.ShapeDtypeStruct((H_FOX, t), F32)],
        compiler_params=_params(), name=name)(proj, proj, proj, proj, gq, gk, fbias, bd)


def attn_prep_bwd(proj, dqs, dkn, dv, dfl, gq, gk, bd, *, name, tm=256):
    t = proj.shape[0]

    def body(q_ref, k_ref, dq_ref, dk_ref, dv_ref, dfl_ref, gq_ref, gk_ref, bd_ref, dp_ref, dgq_ref, dgk_ref):
        bd = bd_ref[...]
        first = pl.program_id(0) == 0

        def back(x_ref, d_ref, g_ref, col0, scale, dg_ref):
            parts = []
            for c in range(MIX // 128):
                sl = slice(128 * c, 128 * (c + 1))
                x = x_ref[:, sl]
                r = lax.rsqrt(_head_sums(x * x, bd) * (1.0 / HEAD_DIM) + EPS)
                dn = d_ref[:, sl] * scale
                gy = dn * g_ref[:, sl]
                hs = _head_sums(gy * x, bd) * (1.0 / HEAD_DIM)
                dp_ref[:, col0 + 128 * c:col0 + 128 * (c + 1)] = (r * gy - x * (r * r * r * hs)).astype(BF16)
                parts.append(jnp.sum(dn * x * r, axis=0, keepdims=True))
            _acc_rows(dg_ref, jnp.concatenate(parts, axis=1), first)

        back(q_ref, dq_ref, gq_ref, 0, SCALE, dgq_ref)
        back(k_ref, dk_ref, gk_ref, MIX, 1.0, dgk_ref)
        dp_ref[:, 2 * MIX:3 * MIX] = dv_ref[...].astype(BF16)
        dp_ref[:, 3 * MIX:] = dfl_ref[...]

    col = lambda c: pl.BlockSpec((tm, MIX), lambda i: (i, c))
    row = pl.BlockSpec((tm, MIX), lambda i: (i, 0))
    vec = pl.BlockSpec((1, MIX), lambda i: (0, 0))
    return pl.pallas_call(
        body, grid=(t // tm,),
        in_specs=[col(0), col(1), row, row, row, pl.BlockSpec((tm, 128), lambda i: (i, 0)), vec, vec,
                  pl.BlockSpec((128, 128), lambda i: (0, 0))],
        out_specs=[pl.BlockSpec((tm, ATTN_IN_PAD), lambda i: (i, 0)), vec, vec],
        out_shape=[jax.ShapeDtypeStruct((t, ATTN_IN_PAD), BF16), jax.ShapeDtypeStruct((1, MIX), F32),
                   jax.ShapeDtypeStruct((1, MIX), F32)],
        compiler_params=_params(), name=name)(proj, proj, dqs, dkn, dv, dfl, gq, gk, bd)


def gate_cumsum(logf3, tri, *, name):
    nc, r, _ = logf3.shape

    def body(x_ref, tri_ref, o_ref):
        tri_m = tri_ref[...]

        def step(c, carry):
            hi, mid, lo = _split3(x_ref[c])
            cs = _dot(hi, tri_m) + _dot(mid, tri_m) + _dot(lo, tri_m) + carry
            o_ref[c] = cs
            return cs[:, 127:128]

        lax.fori_loop(0, nc, step, jnp.zeros((r, 1), F32))

    return pl.pallas_call(body, out_shape=jax.ShapeDtypeStruct(logf3.shape, F32), compiler_params=_params(),
                          name=name)(logf3, tri)


def gate_cumsum_bwd(dcum3, logf3, tri, *, name):
    nc, r, _ = dcum3.shape

    def body(x_ref, lf_ref, tri_ref, o_ref, s_ref):
        tri_m = tri_ref[...]

        def step(n, carry):
            car, tot = carry
            c = nc - 1 - n
            hi, mid, lo = _split3(x_ref[c])
            cs = _dot(hi, tri_m) + _dot(mid, tri_m) + _dot(lo, tri_m) + car
            dl = cs * (1.0 - jnp.exp(lf_ref[c]))
            o_ref[c] = dl
            return cs[:, 0:1], tot + dl

        _, tot = lax.fori_loop(0, nc, step, (jnp.zeros((r, 1), F32), jnp.zeros((r, 128), F32)))
        s_ref[...] = jnp.broadcast_to(jnp.sum(tot, axis=1, keepdims=True), tot.shape)

    return pl.pallas_call(body, out_shape=[jax.ShapeDtypeStruct(dcum3.shape, F32), jax.ShapeDtypeStruct((r, 128), F32)],
                          compiler_params=_params(), name=name)(dcum3, logf3, tri)


def _causal_iota():
    row = lax.broadcasted_iota(jnp.int32, (BQ, BQ), 0)
    col = lax.broadcasted_iota(jnp.int32, (BQ, BQ), 1)
    return row, col


def _head_specs(nj, head0):
    qin = pl.BlockSpec((1, BQ, HEAD_DIM), lambda h, i: (h + head0, i, 0))
    kin = pl.BlockSpec((1, nj, BQ, HEAD_DIM), lambda h, i: (h + head0, 0, 0, 0))
    qin2 = pl.BlockSpec((1, BQ, 2 * HEAD_DIM), lambda h, i: (h + head0, i, 0))
    kin2 = pl.BlockSpec((1, nj, BQ, 2 * HEAD_DIM), lambda h, i: (h + head0, 0, 0, 0))
    qspec = pl.BlockSpec((1, BQ, HEAD_DIM), lambda h, i: (h, i, 0))
    kspec2 = pl.BlockSpec((1, nj, BQ, 2 * HEAD_DIM), lambda h, i: (h, 0, 0, 0))
    return qin, kin, qin2, kin2, qspec, kspec2


STOP = -105.0
STOP_WIDE = -115.0
FIXED_REF_MAX = 40.0


def _store_kmax(k_ref, kmax_ref, nj):
    def step(j, mx):
        kf = k_ref[0, j].astype(F32)
        return jnp.maximum(mx, jnp.max(jnp.sum(kf * kf, axis=1, keepdims=True), axis=0, keepdims=True))

    mx = lax.fori_loop(0, nj, step, jnp.zeros((1, 1), F32))
    kmax_ref[...] = jnp.broadcast_to(jnp.sqrt(mx), kmax_ref.shape)


def _qk_bound(q, kmax_ref):
    qf = q.astype(F32)
    return jnp.sqrt(jnp.sum(qf * qf, axis=1, keepdims=True)) * kmax_ref[0:1, 0:1] * 1.001


def _first_and_last_step():
    h, i = pl.program_id(0), pl.program_id(1)
    first = jnp.logical_and(h == 0, i == 0)
    last = jnp.logical_and(h == pl.num_programs(0) - 1, i == pl.num_programs(1) - 1)
    return first, last


def fox_fwd(qs, kn4, va4, fcol, frow4, *, name, gather=None):
    _, t, dh = qs.shape
    nh = H_FOX
    nj = t // BQ

    def body(*refs):
        if gather is None:
            q_ref, k_ref, v_ref, fc_ref, fr_ref, o_ref, lse_ref, kmax_ref = refs
        else:
            q_ref, k_ref, v_ref, fc_ref, fr_ref, src_ref, o_ref, lse_ref, dst_ref, kmax_ref = refs[:10]
            first_step, last_step = _first_and_last_step()

            @pl.when(first_step)
            def _():
                _chip_gather(src_ref, dst_ref, *refs[10:])[0]()

        i = pl.program_id(1)

        @pl.when(i == 0)
        def _():
            _store_kmax(k_ref, kmax_ref, nj)

        q = q_ref[0]
        fq = fc_ref[0]
        bound = _qk_bound(q, kmax_ref)
        row, col = _causal_iota()

        def gate_at_block_end(j):
            return fr_ref[0, j][:, BQ - 1:BQ]

        def pv(p, j):
            p_hi, p_lo = _split2(p)
            return _dot(p_hi, v_ref[0, j]) + _dot(p_lo, v_ref[0, j])

        def walk(block, live, init):
            carry = block(i, init, True)

            def cond(c):
                n, carry = c
                return jnp.logical_and(n < i, live(jnp.maximum(i - 1 - n, 0), carry))

            _, carry = lax.while_loop(cond, lambda c: (c[0] + 1, block(i - 1 - c[0], c[1], False)), (0, carry))
            return carry

        def fixed_reference(_):
            shift = fq - bound

            def block(j, acc, diag):
                p = jnp.exp(_dot_nt(q, k_ref[0, j]) + shift - fr_ref[0, j])
                if diag:
                    p = jnp.where(col <= row, p, 0.0)
                return acc + pv(p, j)

            def live(j, acc):
                return jnp.max(fq - gate_at_block_end(j) - jnp.log(acc[:, dh:dh + 1])) >= STOP_WIDE

            acc = walk(block, live, jnp.zeros((BQ, 2 * dh), F32))
            l = acc[:, dh:dh + 1]
            return acc[:, :dh] / l, bound + jnp.log(l)

        def running_maximum(_):
            def block(j, carry, diag):
                m, acc = carry
                s = _dot_nt(q, k_ref[0, j]) + fq - fr_ref[0, j]
                if diag:
                    s = jnp.where(col <= row, s, NEG)
                m_new = jnp.maximum(m, jnp.max(s, axis=1, keepdims=True))
                return m_new, jnp.exp(m - m_new) * acc + pv(jnp.exp(s - m_new), j)

            def live(j, carry):
                return jnp.max(bound + fq - gate_at_block_end(j) - carry[0]) >= STOP

            m, acc = walk(block, live, (jnp.full((BQ, 1), NEG, F32), jnp.zeros((BQ, 2 * dh), F32)))
            l = acc[:, dh:dh + 1]
            return acc[:, :dh] / l, m + jnp.log(l)

        o, lse = lax.cond(jnp.max(bound) < FIXED_REF_MAX, fixed_reference, running_maximum, 0)
        o_ref[0] = o
        lse_ref[0] = lse

        if gather is not None:
            @pl.when(last_step)
            def _():
                _chip_gather(src_ref, dst_ref, *refs[10:])[1]()

    qin, kin, _, kin2, qspec, _ = _head_specs(nj, 0)
    cspec = pl.BlockSpec((1, BQ, 1), lambda h, i: (h, i, 0))
    in_specs = [qin, kin, kin2, cspec, pl.BlockSpec((1, nj, 1, BQ), lambda h, i: (h, 0, 0, 0))]
    out_specs = [qspec, cspec]
    out_shape = [jax.ShapeDtypeStruct((nh, t, dh), F32), jax.ShapeDtypeStruct((nh, t, 1), F32)]
    scratch = [pltpu.VMEM((8, 128), F32)]
    args = [qs, kn4, va4, fcol, frow4]
    if gather is not None:
        in_specs.append(_ANY)
        out_specs.append(_ANY)
        out_shape.append(jax.ShapeDtypeStruct((N_CHIPS,) + gather.shape, gather.dtype))
        scratch += _chip_sems()
        args.append(gather)
    return pl.pallas_call(body, grid=(nh, nj), in_specs=in_specs, out_specs=out_specs, out_shape=out_shape,
                          scratch_shapes=scratch, compiler_params=_params(), name=name)(*args)


def fox_bwd(qs, kn4, v4, qa, doa, fcol, frow4, o, do, lse, *, name, scatter=None):
    _, t, dh = qs.shape
    nh = H_FOX
    nj = t // BQ

    def body(*refs):
        q_ref, k_ref, v_ref, qa_ref, doa_ref, fc_ref, fr_ref, o_ref, do_ref, lse_ref = refs[:10]
        if scatter is None:
            dq_ref, dkv_ref, dfk_ref, kmax_ref = refs[10:]
        else:
            g_ref, dq_ref, dkv_ref, dfk_ref, land_ref, kmax_ref = refs[10:16]
            first_step, last_step = _first_and_last_step()

            @pl.when(first_step)
            def _():
                _chip_scatter(g_ref, land_ref, *refs[16:])[0]()

        i = pl.program_id(1)

        @pl.when(i == 0)
        def _():
            dkv_ref[...] = jnp.zeros_like(dkv_ref)
            dfk_ref[...] = jnp.zeros_like(dfk_ref)
            _store_kmax(k_ref, kmax_ref, nj)

        q = q_ref[0]
        do_b = do_ref[0]
        fq = fc_ref[0]
        lse_q = lse_ref[0]
        dd = jnp.sum(do_b.astype(F32) * o_ref[0], axis=1, keepdims=True)
        rhs = jnp.concatenate([qa_ref[0], doa_ref[0]], axis=0)
        edge = _qk_bound(q, kmax_ref) + fq - lse_q

        def negligible(j):
            return jnp.logical_and(j < i, jnp.max(edge - fr_ref[0, j][:, BQ - 1:BQ]) < STOP_WIDE)

        first = lax.while_loop(negligible, lambda j: j + 1, 0)

        def block(j, dq, diag):
            k = k_ref[0, j]
            p = jnp.exp(_dot_nt(q, k) + fq - fr_ref[0, j] - lse_q)
            if diag:
                row, col = _causal_iota()
                p = jnp.where(col <= row, p, 0.0)
            ds = p * (_dot_nt(do_b, v_ref[0, j]) - dd)
            ds_b = ds.astype(BF16)
            dkv_ref[0, j] += _dot_tn(jnp.concatenate([ds_b, p.astype(BF16)], axis=0), rhs)
            dfk_ref[0, j] -= jnp.sum(ds, axis=0, keepdims=True)
            return dq + _dot(ds_b, k)

        dq = lax.fori_loop(first, i, lambda j, c: block(j, c, False), jnp.zeros((BQ, dh), F32))
        dq_ref[0] = block(i, dq, True)

        if scatter is not None:
            @pl.when(last_step)
            def _():
                _chip_scatter(g_ref, land_ref, *refs[16:])[1]()

    qin, kin, qin2, _, qspec, kspec2 = _head_specs(nj, 0)
    cspec = pl.BlockSpec((1, BQ, 1), lambda h, i: (h, i, 0))
    rspec = pl.BlockSpec((1, nj, 1, BQ), lambda h, i: (h, 0, 0, 0))
    in_specs = [qin, kin, kin, qin2, qin2, cspec, rspec, qspec, qin, cspec]
    out_specs = [qspec, kspec2, rspec]
    out_shape = [jax.ShapeDtypeStruct((nh, t, dh), F32), jax.ShapeDtypeStruct((nh, nj, BQ, 2 * dh), F32),
                 jax.ShapeDtypeStruct((nh, nj, 1, BQ), F32)]
    scratch = [pltpu.VMEM((8, 128), F32)]
    args = [qs, kn4, v4, qa, doa, fcol, frow4, o, do, lse]
    if scatter is not None:
        in_specs.append(_ANY)
        out_specs.append(_ANY)
        out_shape.append(jax.ShapeDtypeStruct(scatter.shape, scatter.dtype))
        scratch += _chip_sems()
        args.append(scatter)
    return pl.pallas_call(body, grid=(nh, nj), in_specs=in_specs, out_specs=out_specs, out_shape=out_shape,
                          scratch_shapes=scratch, compiler_params=_params(), name=name)(*args)


def _sb_logs(z, diag):
    e = jnp.exp(-jnp.abs(z))
    sp = jnp.log(1.0 + e)
    logb = jnp.minimum(z, 0.0) - sp
    lom = -jnp.maximum(z, 0.0) - sp
    strict = None
    if diag:
        row, col = _causal_iota()
        strict = col < row
        lom = jnp.where(strict, lom, 0.0)
    return logb, lom, e, strict


def sb_fwd(qs, kn4, va4, tri, *, name, gather=None):
    _, t, dh = qs.shape
    nh = H_SB
    nj = t // BQ
    assert nj <= 128

    def body(*refs):
        if gather is None:
            q_ref, k_ref, v_ref, tri_ref, o_ref, rs_ref = refs
        else:
            q_ref, k_ref, v_ref, tri_ref, src_ref, o_ref, rs_ref, dst_ref = refs[:8]
            first_step, last_step = _first_and_last_step()

            @pl.when(first_step)
            def _():
                _chip_gather(src_ref, dst_ref, *refs[8:])[0]()

        i = pl.program_id(1)
        q = q_ref[0]
        tri_m = tri_ref[...]
        lane = lax.broadcasted_iota(jnp.int32, (BQ, 128), 1)

        def block(j, carry, diag):
            run, acc, rall = carry
            logb, lom, _, strict = _sb_logs(_dot_nt(q, k_ref[0, j]), diag)
            hi, lo = _split2(lom)
            w = jnp.exp(logb + (_dot(hi, tri_m) + _dot(lo, tri_m)) + run)
            if diag:
                w = jnp.where(strict, w, 0.0)
            acc = acc + _dot(w.astype(BF16), v_ref[0, j])
            rall = jnp.where(lane == j, run, rall)
            return run + jnp.sum(lom, axis=1, keepdims=True), acc, rall

        init = (jnp.zeros((BQ, 1), F32), jnp.zeros((BQ, 2 * dh), F32), jnp.full((BQ, 128), NEG, F32))
        carry = block(i, init, True)

        def cond(c):
            n, carry = c
            return jnp.logical_and(n < i, jnp.max(carry[0]) >= STOP)

        _, (_, acc, rall) = lax.while_loop(cond, lambda c: (c[0] + 1, block(i - 1 - c[0], c[1], False)), (0, carry))
        o_ref[0] = acc[:, :dh].astype(BF16)
        rs_ref[0] = rall

        if gather is not None:
            @pl.when(last_step)
            def _():
                _chip_gather(src_ref, dst_ref, *refs[8:])[1]()

    qin, kin, _, kin2, qspec, _ = _head_specs(nj, H_FOX)
    rspec = pl.BlockSpec((1, BQ, 128), lambda h, i: (h, i, 0))
    in_specs = [qin, kin, kin2, pl.BlockSpec((BQ, BQ), lambda h, i: (0, 0))]
    out_specs = [qspec, rspec]
    out_shape = [jax.ShapeDtypeStruct((nh, t, dh), BF16), jax.ShapeDtypeStruct((nh, t, 128), F32)]
    scratch = []
    args = [qs, kn4, va4, tri]
    if gather is not None:
        in_specs.append(_ANY)
        out_specs.append(_ANY)
        out_shape.append(jax.ShapeDtypeStruct((N_CHIPS,) + gather.shape, gather.dtype))
        scratch += _chip_sems()
        args.append(gather)
    return pl.pallas_call(body, grid=(nh, nj), in_specs=in_specs, out_specs=out_specs, out_shape=out_shape,
                          scratch_shapes=scratch, compiler_params=_params(), name=name)(*args)


def sb_bwd(qs, kn4, v4, qa, doa, tri, do, rsave, *, name):
    _, t, dh = qs.shape
    nh = H_SB
    nj = t // BQ

    def body(q_ref, k_ref, v_ref, qa_ref, doa_ref, tri_ref, do_ref, rs_ref, dq_ref, dkv_ref):
        i = pl.program_id(1)

        @pl.when(i == 0)
        def _():
            dkv_ref[...] = jnp.zeros_like(dkv_ref)

        q = q_ref[0]
        do_b = do_ref[0]
        tri_m = tri_ref[...]
        rall = rs_ref[0]
        lane = lax.broadcasted_iota(jnp.int32, (BQ, 128), 1)
        rhs = jnp.concatenate([qa_ref[0], doa_ref[0]], axis=0)
        lane1 = lax.broadcasted_iota(jnp.int32, (1, 128), 1)
        unvisited = jnp.logical_and(lane1 < i, jnp.max(rall, axis=0, keepdims=True) < STOP)
        first = jnp.sum(unvisited.astype(jnp.int32))

        def block(j, carry, diag):
            dq, ecar = carry
            k = k_ref[0, j]
            z = _dot_nt(q, k)
            logb, lom, e, strict = _sb_logs(z, diag)
            hi, lo = _split2(lom)
            run = jnp.sum(jnp.where(lane == j, rall, 0.0), axis=1, keepdims=True)
            w = jnp.exp(logb + (_dot(hi, tri_m) + _dot(lo, tri_m)) + run)
            if diag:
                w = jnp.where(strict, w, 0.0)
            da = w * _dot_nt(do_b, v_ref[0, j])
            before = _dot_nt(da.astype(BF16), tri_m) + ecar
            inv = 1.0 / (1.0 + e)
            beta = jnp.where(z >= 0.0, 1.0, e) * inv
            one_minus = jnp.where(z >= 0.0, e, 1.0) * inv
            dz = da * one_minus - before * beta
            if diag:
                dz = jnp.where(strict, dz, 0.0)
            dz_b = dz.astype(BF16)
            dkv_ref[0, j] += _dot_tn(jnp.concatenate([dz_b, w.astype(BF16)], axis=0), rhs)
            return dq + _dot(dz_b, k), ecar + jnp.sum(da, axis=1, keepdims=True)

        carry = lax.fori_loop(first, i, lambda j, c: block(j, c, False),
                              (jnp.zeros((BQ, dh), F32), jnp.zeros((BQ, 1), F32)))
        dq, _ = block(i, carry, True)
        dq_ref[0] = dq

    qin, kin, qin2, _, qspec, kspec2 = _head_specs(nj, H_FOX)
    return pl.pallas_call(
        body, grid=(nh, nj),
        in_specs=[qin, kin, kin, qin2, qin2, pl.BlockSpec((BQ, BQ), lambda h, i: (0, 0)), qin,
                  pl.BlockSpec((1, BQ, 128), lambda h, i: (h, i, 0))],
        out_specs=[qspec, kspec2],
        out_shape=[jax.ShapeDtypeStruct((nh, t, dh), F32), jax.ShapeDtypeStruct((nh, nj, BQ, 2 * dh), F32)],
        compiler_params=_params(), name=name)(qs, kn4, v4, qa, doa, tri, do, rsave)


def loss_head(y, target, *, name, tm=512):
    t, d = y.shape

    def body(y_ref, t_ref, l_ref, dy_ref, acc_ref):
        i = pl.program_id(0)
        diff = y_ref[...] - t_ref[...]
        dy_ref[...] = diff * (1.0 / d)
        part = jnp.sum(diff * diff, axis=0, keepdims=True)

        @pl.when(i == 0)
        def _():
            acc_ref[...] = part

        @pl.when(i > 0)
        def _():
            acc_ref[...] += part

        @pl.when(i == pl.num_programs(0) - 1)
        def _():
            l_ref[...] = jnp.full(l_ref.shape, (0.5 / d) * jnp.sum(acc_ref[...]), F32)

    row = pl.BlockSpec((tm, d), lambda i: (i, 0))
    return pl.pallas_call(
        body, grid=(t // tm,), in_specs=[row, row],
        out_specs=[pl.BlockSpec((8, 128), lambda i: (0, 0)), row],
        out_shape=[jax.ShapeDtypeStruct((8, 128), F32), jax.ShapeDtypeStruct((t, d), F32)],
        scratch_shapes=[pltpu.VMEM((1, d), F32)], compiler_params=_params(), name=name)(y, target)


def _to_heads(a):
    t = a.shape[0]
    return a.reshape(t, N_HEADS, HEAD_DIM).transpose(1, 0, 2)


def _from_heads(a):
    t = a.shape[1]
    return a.transpose(1, 0, 2).reshape(t, MIX)


def _lanes_to_chunks(a):
    r, t = a.shape
    return a.reshape(r, t // 128, 128).transpose(1, 0, 2)


def _chunks_to_lanes(a):
    nc, r, _ = a.shape
    return a.transpose(1, 0, 2).reshape(r, nc * 128)


def _constants():
    idx = jnp.arange(128)
    bd = (idx[:, None] // HEAD_DIM == idx[None, :] // HEAD_DIM).astype(BF16)
    tri_le = (idx[:, None] <= idx[None, :]).astype(BF16)
    tri_ge = (idx[:, None] >= idx[None, :]).astype(BF16)
    jdx = jnp.arange(BQ)
    tri_gt = (jdx[:, None] > jdx[None, :]).astype(BF16)
    return dict(bd=bd, tri_le=tri_le, tri_ge=tri_ge, tri_gt=tri_gt)


def attn_layer_fwd(h, w, cst, gather=None):
    t = h.shape[0]
    nj = t // BQ
    xn, proj = rms_mm_nn(h, w["norm"], w["w_in"], tn=640, name="attn_in_proj")
    qs, kn, vb, logf = attn_prep_fwd(proj, w["gq"], w["gk"], w["fbias"], cst["bd"], name="attn_prep_fwd")
    logf3 = _lanes_to_chunks(logf)
    cum = _chunks_to_lanes(gate_cumsum(logf3, cst["tri_le"], name="gate_cumsum"))
    fcol = cum.reshape(H_FOX, t, 1)
    frow4 = cum.reshape(H_FOX, nj, 1, BQ)
    qh = _to_heads(qs)
    kh4 = _to_heads(kn).reshape(N_HEADS, nj, BQ, HEAD_DIM)
    vh4 = _to_heads(vb).reshape(N_HEADS, nj, BQ, HEAD_DIM)
    ones = jnp.ones(vh4.shape[:-1] + (1,), BF16)
    va4 = jnp.concatenate([vh4, ones, jnp.zeros(vh4.shape[:-1] + (HEAD_DIM - 1,), BF16)], axis=-1)
    if gather is None:
        (o_f, lse), (o_s, rsave), gathered = (fox_fwd(qh, kh4, va4, fcol, frow4, name="fox_fwd"),
                                              sb_fwd(qh, kh4, va4, cst["tri_gt"], name="sb_fwd"), None)
    else:
        o_f, lse, gathered_a = fox_fwd(qh, kh4, va4, fcol, frow4, name="fox_fwd_gather", gather=gather[0])
        o_s, rsave, gathered_b = sb_fwd(qh, kh4, va4, cst["tri_gt"], name="sb_fwd_gather", gather=gather[1])
        gathered = gather[2]((gathered_a, gathered_b))
        w = gathered[0][0]
    o = _from_heads(jnp.concatenate([o_f.astype(BF16), o_s], axis=0))
    h2 = mm_nn(o, w["w_out"], add=h, name="mix_out_proj")
    saved = dict(h=h, xn=xn, proj=proj, logf3=logf3, fcol=fcol, frow4=frow4, qh=qh, kh4=kh4, vh4=vh4,
                 o_f=o_f, lse=lse, rsave=rsave, o=o)
    return h2, saved, gathered


def attn_layer_bwd(dh, w, s, cst, scatter=None):
    t = dh.shape[0]
    dh3 = dh[None]
    do = mm_nt(dh3, w["w_out"], out_dtype=BF16, name="mix_out_bwd_bf16")
    g_w_out = mm_tn(s["o"], dh3, name="mix_out_wgrad")
    doh = _to_heads(do)
    zeros = jnp.zeros_like(doh)
    qa = jnp.concatenate([s["qh"], zeros], axis=-1)
    doa = jnp.concatenate([zeros, doh], axis=-1)
    fox_args = (s["qh"], s["kh4"], s["vh4"], qa, doa, s["fcol"], s["frow4"], s["o_f"], doh, s["lse"])
    if scatter is None:
        (dq_f, dkv_f, dfk), landed = fox_bwd(*fox_args, name="fox_bwd"), None
    else:
        dq_f, dkv_f, dfk, landed = fox_bwd(*fox_args, name="fox_bwd_scatter", scatter=scatter(g_w_out))
    dq_s, dkv_s = sb_bwd(s["qh"], s["kh4"], s["vh4"], qa, doa, cst["tri_gt"], doh, s["rsave"], name="sb_bwd")
    dqs = _from_heads(jnp.concatenate([dq_f, dq_s], axis=0))
    dkv = jnp.concatenate([dkv_f, dkv_s], axis=0).reshape(N_HEADS, t, 2 * HEAD_DIM)
    dkn = _from_heads(dkv[:, :, :HEAD_DIM])
    dv = _from_heads(dkv[:, :, HEAD_DIM:])
    dcum3 = _lanes_to_chunks(dfk.reshape(H_FOX, t))
    dfl3, dbias = gate_cumsum_bwd(dcum3, s["logf3"], cst["tri_ge"], name="gate_cumsum_bwd")
    dfl = jnp.pad(_chunks_to_lanes(dfl3).T, ((0, 0), (0, 128 - H_FOX))).astype(BF16)
    dproj, dgq, dgk = attn_prep_bwd(s["proj"], dqs, dkn, dv, dfl, w["gq"], w["gk"], cst["bd"], name="attn_prep_bwd")
    g_w_in = mm_tn(s["xn"], dproj[None], tn=640, name="attn_in_wgrad")[:, :ATTN_IN]
    dh2, g_norm = mm_nt_rms_bwd(dproj[None], w["w_in"], s["h"], w["norm"], dh, name="attn_in_bwd")
    dgq = dgq.reshape(N_HEADS, HEAD_DIM)
    dgk = dgk.reshape(N_HEADS, HEAD_DIM)
    grads = dict(norm=g_norm[0], w_in=g_w_in, f_bias=dbias[:, 0], fox_q=dgq[:H_FOX].sum(0), fox_k=dgk[:H_FOX].sum(0),
                 sb_q=dgq[H_FOX:].sum(0), sb_k=dgk[H_FOX:].sum(0), w_out=g_w_out)
    return dh2, grads, landed


def conv_layer_fwd(h, w):
    xn, proj3 = rms_mm_nn(h, w["norm"], w["w_in"], parts=3, name="conv_in_proj")
    y = conv_mix_fwd(proj3, w["ck"], name="conv_mix_fwd")
    h2 = mm_nn(y, w["w_out"], add=h, name="mix_out_proj")
    return h2, dict(h=h, xn=xn, proj3=proj3, y=y)


def conv_layer_bwd(dh, w, s):
    dh3 = dh[None]
    g_w_out = mm_tn(s["y"], dh3, name="mix_out_wgrad")
    dproj3, dck = conv_mix_bwd(dh, w["w_out"], s["proj3"], w["ck"], name="conv_mix_bwd")
    g_w_in = mm_tn(s["xn"], dproj3, name="conv_in_wgrad")
    dh2, g_norm = mm_nt_rms_bwd(dproj3, w["w_in"], s["h"], w["norm"], dh, name="conv_in_bwd")
    return dh2, dict(norm=g_norm[0], w_in=g_w_in, ck=dck[0, :3], w_out=g_w_out)


def ffn_layer_fwd(h, w):
    xn, up2 = rms_mm_nn(h, w["norm"], w["w_up"], parts=2, tn=1408, name="ffn_up_proj")
    h2, act = ffn_act_down_fwd(up2, w["cw2"], w["w_down"], h, name="ffn_act_down_fwd")
    return h2, dict(h=h, xn=xn, up2=up2, act=act)


def ffn_layer_bwd(dh, w, s):
    dh3 = dh[None]
    g_w_down = mm_tn(s["act"], dh3, tk=1408, name="ffn_down_wgrad")
    dup2, dcw = ffn_act_bwd(dh, w["w_down"], s["up2"], w["cw2"], name="ffn_act_bwd")
    g_w_up = mm_tn(s["xn"], dup2, tn=1408, name="ffn_up_wgrad")
    dh2, g_norm = mm_nt_rms_bwd(dup2, w["w_up"], s["h"], w["norm"], dh, name="ffn_up_bwd")
    g_cw = jnp.concatenate([dcw[0, :3], dcw[1, :3]], axis=1)
    return dh2, dict(norm=g_norm[0], w_up=g_w_up, cw=g_cw, w_down=g_w_down)


def forward_backward(x, target, wa, wc, wf, *, late_weights=None, late_chunks=None):
    cst = _constants()
    h = x
    saved = []
    layer = 0
    while layer == 0 or layer < len(wf):
        i = layer // 2
        if layer % 2 == 0:
            h, sm, built = attn_layer_fwd(h, wa[i], cst, gather=late_weights if late_weights and layer == 0 else None)
            if built is not None:
                wa, wc, wf = built
        else:
            h, sm = conv_layer_fwd(h, wc[i])
        h, sf = ffn_layer_fwd(h, wf[layer])
        saved.append((sm, sf))
        layer += 1
    depth = len(wf)
    loss_blk, dh = loss_head(h, target, name="loss_head")
    ga, gc, gf = [None] * len(wa), [None] * len(wc), [None] * depth
    landed = None
    for layer in reversed(range(depth)):
        i = layer // 2
        sm, sf = saved[layer]
        dh, gf[layer] = ffn_layer_bwd(dh, wf[layer], sf)
        if layer % 2 == 0:
            chunks = None
            if late_chunks and layer == 0:
                chunks = lambda g_w_out: late_chunks([dict(w_out=g_w_out)] + ga[1:], gc, gf)
            dh, ga[i], got = attn_layer_bwd(dh, wa[i], sm, cst, scatter=chunks)
            landed = got if got is not None else landed
        else:
            dh, gc[i] = conv_layer_bwd(dh, wc[i], sm)
    return loss_blk, dh, ga, gc, gf, landed


def _part_rows(shape, width, row_mult):
    n = 1
    for s in shape:
        n *= s
    rows = -(-n // width)
    return -(-rows // row_mult) * row_mult


def _pack_rows(arrs, width, row_mult, dtype, total_rows=None):
    parts = []
    used = 0
    for a in arrs:
        rows = _part_rows(a.shape, width, row_mult)
        flat = a.astype(dtype).reshape(-1)
        flat = jnp.pad(flat, (0, rows * width - flat.shape[0]))
        parts.append(flat.reshape(rows, width))
        used += rows
    if total_rows is not None and total_rows > used:
        parts.append(jnp.zeros((total_rows - used, width), dtype))
    return jnp.concatenate(parts, axis=0)


def _unpack_rows(packed, shapes, width, row_mult):
    out = []
    off = 0
    for shape in shapes:
        rows = _part_rows(shape, width, row_mult)
        n = 1
        for s in shape:
            n *= s
        out.append(packed[off:off + rows].reshape(-1)[:n].reshape(shape))
        off += rows
    return out


BIG_NAMES = ("attn_w_in", "attn_w_out", "conv_w_in", "conv_w_out", "ffn_w_up", "ffn_w_down")
BIG_AXIS = {"attn_w_in": 2, "attn_w_out": 1, "conv_w_in": 2, "conv_w_out": 1, "ffn_w_up": 2, "ffn_w_down": 1}
BIG_WIDTH = 1024
BIG_ROW_MULT = 16
BIG_TILE = 512
SMALL_TILE = 128
SMALL_SHARDED = ("conv_norm", "conv_kernel", "ffn_conv")
SMALL_AXIS = {"conv_norm": 1, "conv_kernel": 2, "ffn_conv": 2}
SMALL_REPLICATED = ("attn_norm", "attn_f_bias", "fox_q_gain", "fox_k_gain", "sb_q_gain", "sb_k_gain", "ffn_norm")
WEIGHT_ORDER = ("attn_norm", "attn_w_in", "attn_f_bias", "fox_q_gain", "fox_k_gain", "sb_q_gain", "sb_k_gain",
                "attn_w_out", "conv_norm", "conv_w_in", "conv_kernel", "conv_w_out", "ffn_norm", "ffn_w_up",
                "ffn_conv", "ffn_w_down")


def _big_total_rows(shapes):
    used = sum(_part_rows(s, BIG_WIDTH, BIG_ROW_MULT) for s in shapes)
    tile = BIG_TILE if used >= 8 * BIG_TILE else SMALL_TILE
    return -(-used // tile) * tile


def _place():
    x, y, c = lax.axis_index("x"), lax.axis_index("y"), lax.axis_index("c")
    other_chips = [(1 - x, y), (x, 1 - y), (1 - x, 1 - y)]
    return x, y, c, other_chips


_ANY = pl.BlockSpec(memory_space=pl.ANY)


def _chip_sems():
    return [pltpu.SemaphoreType.DMA((3,)), pltpu.SemaphoreType.DMA((3,)), pltpu.SemaphoreType.DMA]


def _chip_gather(src_ref, dst_ref, send_sems, recv_sems, local_sem):
    x, y, c, chips = _place()
    k = 2 * x + y

    def copy(j, slot):
        px, py = chips[j]
        return pltpu.make_async_remote_copy(src_ref=src_ref, dst_ref=dst_ref.at[slot], send_sem=send_sems.at[j],
                                            recv_sem=recv_sems.at[j], device_id=(px, py, c), device_id_type=MESH)

    def local():
        return pltpu.make_async_copy(src_ref, dst_ref.at[k], local_sem)

    def start():
        local().start()
        for j in range(3):
            copy(j, k).start()

    def finish():
        for j, (px, py) in enumerate(chips):
            copy(j, 2 * px + py).wait_recv()
        for j in range(3):
            copy(j, k).wait_send()
        local().wait()

    return start, finish


def _chip_scatter(g_ref, o_ref, send_sems, recv_sems, local_sem):
    x, y, c, chips = _place()
    k = 2 * x + y

    def copy(j, src_slot, dst_slot):
        px, py = chips[j]
        return pltpu.make_async_remote_copy(src_ref=g_ref.at[src_slot], dst_ref=o_ref.at[dst_slot],
                                            send_sem=send_sems.at[j], recv_sem=recv_sems.at[j],
                                            device_id=(px, py, c), device_id_type=MESH)

    def local():
        return pltpu.make_async_copy(g_ref.at[k], o_ref.at[k], local_sem)

    def start():
        local().start()
        for j, (px, py) in enumerate(chips):
            copy(j, 2 * px + py, k).start()

    def finish():
        for j, (px, py) in enumerate(chips):
            copy(j, k, 2 * px + py).wait_recv()
        for j, (px, py) in enumerate(chips):
            copy(j, 2 * px + py, k).wait_send()
        local().wait()

    return start, finish


def gather_chips(arrs, *, name):
    n = len(arrs)

    def body(*refs):
        hooks = [_chip_gather(refs[m], refs[n + m], *refs[2 * n + 3 * m:2 * n + 3 * m + 3]) for m in range(n)]
        for start, _ in hooks:
            start()
        for _, finish in hooks:
            finish()

    return pl.pallas_call(
        body, in_specs=[_ANY] * n, out_specs=[_ANY] * n,
        out_shape=[jax.ShapeDtypeStruct((N_CHIPS,) + a.shape, a.dtype) for a in arrs],
        scratch_shapes=_chip_sems() * n, name=name)(*arrs)


def scatter_chips(chunks, *, name):
    def body(g_ref, o_ref, send_sems, recv_sems, local_sem):
        start, finish = _chip_scatter(g_ref, o_ref, send_sems, recv_sems, local_sem)
        start()
        finish()

    return pl.pallas_call(
        body, in_specs=[_ANY], out_specs=_ANY, out_shape=jax.ShapeDtypeStruct(chunks.shape, chunks.dtype),
        scratch_shapes=_chip_sems(), name=name)(chunks)


def swap_cores(arrs, *, name):
    n = len(arrs)

    def body(*refs):
        x, y, c, _ = _place()
        copies = [pltpu.make_async_remote_copy(src_ref=refs[m], dst_ref=refs[n + m], send_sem=refs[2 * n + 2 * m],
                                               recv_sem=refs[2 * n + 2 * m + 1], device_id=(x, y, 1 - c),
                                               device_id_type=MESH) for m in range(n)]
        for cp in copies:
            cp.start()
        for cp in copies:
            cp.wait()

    return pl.pallas_call(
        body, in_specs=[_ANY] * n, out_specs=[_ANY] * n,
        out_shape=[jax.ShapeDtypeStruct(a.shape, a.dtype) for a in arrs],
        scratch_shapes=[pltpu.SemaphoreType.DMA, pltpu.SemaphoreType.DMA] * n, name=name)(*arrs)


def allreduce_small(p, *, name):
    r, w = p.shape

    def body(p_ref, o_ref, buf, send_sems, recv_sems):
        x, y, c, _ = _place()
        me = 4 * x + 2 * y + c
        buf[me] = p_ref[...]

        def peer_of(m):
            return (1 - x if m & 4 else x, 1 - y if m & 2 else y, 1 - c if m & 1 else c)

        def copy(m, slot):
            return pltpu.make_async_remote_copy(src_ref=p_ref, dst_ref=buf.at[slot], send_sem=send_sems.at[m - 1],
                                                recv_sem=recv_sems.at[m - 1], device_id=peer_of(m),
                                                device_id_type=MESH)

        sends = [copy(m, me) for m in range(1, 8)]
        for cp in sends:
            cp.start()
        for m in range(1, 8):
            px, py, pc = peer_of(m)
            copy(m, 4 * px + 2 * py + pc).wait_recv()
        for cp in sends:
            cp.wait_send()
        acc = buf[0]
        for d in range(1, 8):
            acc = acc + buf[d]
        o_ref[...] = acc

    vm = pl.BlockSpec(memory_space=pltpu.VMEM)
    return pl.pallas_call(
        body, in_specs=[vm], out_specs=vm, out_shape=jax.ShapeDtypeStruct((r, w), F32),
        scratch_shapes=[pltpu.VMEM((8, r, w), F32), pltpu.SemaphoreType.DMA((7,)), pltpu.SemaphoreType.DMA((7,))],
        name=name)(p)


def sum_chips(rv, *, name):
    _, r, w = rv.shape
    tile = BIG_TILE if r % BIG_TILE == 0 else SMALL_TILE

    def body(a_ref, b_ref, c_ref, d_ref, o_ref):
        o_ref[...] = ((a_ref[0].astype(F32) + b_ref[0].astype(F32)) + c_ref[0].astype(F32)) + d_ref[0].astype(F32)

    spec = lambda kk: pl.BlockSpec((1, tile, w), lambda i: (kk, i, 0))
    return pl.pallas_call(
        body, grid=(r // tile,), in_specs=[spec(0), spec(1), spec(2), spec(3)],
        out_specs=pl.BlockSpec((tile, w), lambda i: (i, 0)), out_shape=jax.ShapeDtypeStruct((r, w), F32),
        compiler_params=_params(), name=name)(rv, rv, rv, rv)


def add_pair(a, b, *, name):
    r, w = a.shape
    tile = BIG_TILE if r % BIG_TILE == 0 else SMALL_TILE

    def body(a_ref, b_ref, o_ref):
        o_ref[...] = a_ref[...] + b_ref[...]

    spec = pl.BlockSpec((tile, w), lambda i: (i, 0))
    return pl.pallas_call(body, grid=(r // tile,), in_specs=[spec, spec], out_specs=spec,
                          out_shape=jax.ShapeDtypeStruct((r, w), F32), compiler_params=_params(), name=name)(a, b)


def adamw(w, g, m, v, *, tm, name):
    r, c = w.shape
    assert r % tm == 0

    def body(w_ref, g_ref, m_ref, v_ref, d_ref, nm_ref, nv_ref):
        g_ = g_ref[...]
        m_ = ADAM_B1 * m_ref[...] + (1.0 - ADAM_B1) * g_
        v_ = ADAM_B2 * v_ref[...] + (1.0 - ADAM_B2) * (g_ * g_)
        m_hat = m_ / (1.0 - ADAM_B1 ** ADAM_STEP)
        v_hat = v_ / (1.0 - ADAM_B2 ** ADAM_STEP)
        d_ref[...] = -ADAM_LR * (m_hat / (jnp.sqrt(v_hat) + ADAM_EPS) + ADAM_WD * w_ref[...])
        nm_ref[...] = m_
        nv_ref[...] = v_

    spec = pl.BlockSpec((tm, c), lambda i: (i, 0))
    return pl.pallas_call(body, grid=(r // tm,), in_specs=[spec] * 4, out_specs=[spec] * 3,
                          out_shape=[jax.ShapeDtypeStruct((r, c), F32)] * 3, compiler_params=_params(), name=name)(w, g, m, v)


def kernel(x, attn_norm, attn_w_in, attn_f_bias, fox_q_gain, fox_k_gain, sb_q_gain, sb_k_gain, attn_w_out, conv_norm, conv_w_in, conv_kernel, conv_w_out, ffn_norm, ffn_w_up, ffn_conv, ffn_w_down, loss_target, m_attn_norm, m_attn_w_in, m_attn_f_bias, m_fox_q_gain, m_fox_k_gain, m_sb_q_gain, m_sb_k_gain, m_attn_w_out, m_conv_norm, m_conv_w_in, m_conv_kernel, m_conv_w_out, m_ffn_norm, m_ffn_w_up, m_ffn_conv, m_ffn_w_down, v_attn_norm, v_attn_w_in, v_attn_f_bias, v_fox_q_gain, v_fox_k_gain, v_sb_q_gain, v_sb_k_gain, v_attn_w_out, v_conv_norm, v_conv_w_in, v_conv_kernel, v_conv_w_out, v_ffn_norm, v_ffn_w_up, v_ffn_conv, v_ffn_w_down):
    a = dict(locals())
    chip = 2 * lax.axis_index("x") + lax.axis_index("y")
    n_attn, n_conv, depth = attn_norm.shape[0], conv_norm.shape[0], ffn_norm.shape[0]

    units = [(name, l) for name in BIG_NAMES for l in range(a[name].shape[0])]
    early = [("attn_w_in", 0)]
    late = [u for u in units if u not in early]
    late_b = [("attn_w_out", 0), ("conv_w_in", n_conv - 1), ("conv_w_out", n_conv - 1), ("ffn_w_up", depth - 1),
              ("ffn_w_down", depth - 1)]
    late_a = [u for u in late if u not in late_b]

    def unit_shape(u):
        return a[u[0]].shape[1:]

    def pack_units(us, get):
        return _pack_rows([get(u) for u in us], BIG_WIDTH, BIG_ROW_MULT, BF16, _big_total_rows([unit_shape(u) for u in us]))

    def unpack_units(packed, us):
        return dict(zip(us, _unpack_rows(packed, [unit_shape(u) for u in us], BIG_WIDTH, BIG_ROW_MULT)))

    def full_units(gathered, us):
        per_chip = [unpack_units(gathered[kk], us) for kk in range(N_CHIPS)]
        return {u: jnp.concatenate([per_chip[kk][u] for kk in range(N_CHIPS)], axis=BIG_AXIS[u[0]] - 1) for u in us}

    def shard(u):
        return a[u[0]][u[1]]

    small_shapes = [a[n].shape for n in SMALL_SHARDED]
    packed_s = _pack_rows([a[n] for n in SMALL_SHARDED], 128, 8, F32)
    gath_e, gath_s = gather_chips([pack_units(early, shard), packed_s], name="gather_weights")
    full_e = full_units(gath_e, early)
    full = {}
    per_chip = [_unpack_rows(gath_s[kk], small_shapes, 128, 8) for kk in range(N_CHIPS)]
    for n, name in enumerate(SMALL_SHARDED):
        full[name] = jnp.concatenate([per_chip[kk][n] for kk in range(N_CHIPS)], axis=SMALL_AXIS[name])

    def attn_weights(i, fu):
        return dict(
            norm=attn_norm[i][None],
            w_in=jnp.pad(fu[("attn_w_in", i)], ((0, 0), (0, ATTN_IN_PAD - ATTN_IN))),
            fbias=jnp.pad(attn_f_bias[i], (0, 128 - H_FOX))[None],
            gq=jnp.concatenate([jnp.tile(fox_q_gain[i], H_FOX), jnp.tile(sb_q_gain[i], H_SB)])[None],
            gk=jnp.concatenate([jnp.tile(fox_k_gain[i], H_FOX), jnp.tile(sb_k_gain[i], H_SB)])[None],
            w_out=fu.get(("attn_w_out", i)))

    def build_weights(gathered):
        fu = {**full_e, **full_units(gathered[0], late_a), **full_units(gathered[1], late_b)}
        wa = [attn_weights(i, fu) for i in range(n_attn)]
        wc = [dict(norm=full["conv_norm"][i][None], w_in=fu[("conv_w_in", i)], ck=full["conv_kernel"][i][None],
                   w_out=fu[("conv_w_out", i)]) for i in range(n_conv)]
        wf = []
        for l in range(depth):
            cw = full["ffn_conv"][l]
            wf.append(dict(norm=ffn_norm[l][None], w_up=fu[("ffn_w_up", l)], cw2=jnp.stack([cw[:, :D_FF], cw[:, D_FF:]]),
                           w_down=fu[("ffn_w_down", l)]))
        return wa, wc, wf

    def chunk_of(u, kk, ga, gc, gf):
        name, l = u
        g = {"attn_w_in": lambda: ga[l]["w_in"], "attn_w_out": lambda: ga[l]["w_out"],
             "conv_w_in": lambda: gc[l]["w_in"], "conv_w_out": lambda: gc[l]["w_out"],
             "ffn_w_up": lambda: gf[l]["w_up"], "ffn_w_down": lambda: gf[l]["w_down"]}[name]()
        width = a[name].shape[BIG_AXIS[name]]
        return lax.slice_in_dim(g, kk * width, (kk + 1) * width, axis=BIG_AXIS[name] - 1)

    def chunks_of(us, ga, gc, gf):
        return jnp.stack([pack_units(us, lambda u: chunk_of(u, kk, ga, gc, gf)) for kk in range(N_CHIPS)])

    loss_blk, grad_x, ga, gc, gf, landed_late = forward_backward(
        x[0], loss_target[0], [attn_weights(0, full_e)], [], [],
        late_weights=(pack_units(late_a, shard), pack_units(late_b, shard), build_weights),
        late_chunks=lambda ga, gc, gf: chunks_of(late, ga, gc, gf))

    landed_early = scatter_chips(chunks_of(early, ga, gc, gf), name="scatter_grads")
    mine = [sum_chips(landed_early, name="sum_chips"), sum_chips(landed_late, name="sum_chips")]
    theirs = swap_cores(mine, name="swap_cores")
    g_units = {**unpack_units(add_pair(mine[0], theirs[0], name="add_cores"), early),
               **unpack_units(add_pair(mine[1], theirs[1], name="add_cores"), late)}
    grads = {name: jnp.stack([g_units[(name, l)] for l in range(a[name].shape[0])]) for name in BIG_NAMES}

    small_full = [
        loss_blk,
        jnp.stack([g["norm"] for g in ga]), jnp.stack([g["f_bias"] for g in ga]),
        jnp.stack([g["fox_q"] for g in ga]), jnp.stack([g["fox_k"] for g in ga]),
        jnp.stack([g["sb_q"] for g in ga]), jnp.stack([g["sb_k"] for g in ga]),
        jnp.stack([g["norm"] for g in gf]),
        jnp.stack([g["norm"] for g in gc]), jnp.stack([g["ck"] for g in gc]), jnp.stack([g["cw"] for g in gf]),
    ]
    summed = allreduce_small(_pack_rows(small_full, 128, 8, F32), name="allreduce_small")
    parts = _unpack_rows(summed, [p.shape for p in small_full], 128, 8)
    loss = parts[0][0, 0]
    for name, g in zip(SMALL_REPLICATED, parts[1:8]):
        grads[name] = g
    for name, g in zip(SMALL_SHARDED, parts[8:]):
        width = a[name].shape[SMALL_AXIS[name]]
        grads[name] = lax.dynamic_slice_in_dim(g, chip * width, width, axis=SMALL_AXIS[name])

    delta, new_m, new_v = {}, {}, {}
    for name in BIG_NAMES:
        shape = a[name].shape
        flat = lambda arr: arr.reshape(-1, shape[-1])
        d_, m_, v_ = adamw(flat(a[name]), flat(grads[name]), flat(a["m_" + name]), flat(a["v_" + name]), tm=256,
                           name="adamw")
        delta[name], new_m[name], new_v[name] = d_.reshape(shape), m_.reshape(shape), v_.reshape(shape)
    small_names = SMALL_REPLICATED + SMALL_SHARDED
    small_shapes_local = [a[n].shape for n in small_names]
    pack = lambda prefix, src: _pack_rows([src[prefix + n] for n in small_names], 128, 8, F32)
    packed = adamw(pack("", a), pack("", grads), pack("m_", a), pack("v_", a), tm=8, name="adamw_small")
    for store, buf in zip((delta, new_m, new_v), packed):
        for name, arr in zip(small_names, _unpack_rows(buf, small_shapes_local, 128, 8)):
            store[name] = arr

    return (loss, grad_x[None], *[grads[n] for n in WEIGHT_ORDER], *[delta[n] for n in WEIGHT_ORDER],
            *[new_m[n] for n in WEIGHT_ORDER], *[new_v[n] for n in WEIGHT_ORDER])
```

```python
import functools

import jax
import jax.numpy as jnp
from jax import lax
from jax.experimental import pallas as pl
from jax.experimental.pallas import tpu as pltpu

F32 = jnp.float32
BF16 = jnp.bfloat16

D_MODEL = 1024
HEAD_DIM = 64
H_FOX = 8
H_SB = 8
N_HEADS = H_FOX + H_SB
MIX = N_HEADS * HEAD_DIM
ATTN_IN = 3 * MIX + H_FOX
ATTN_IN_PAD = 3 * MIX + 128
D_FF = 2816
EPS = 1e-6
SCALE = HEAD_DIM ** -0.5
NEG = -1e30

ADAM_LR = 0.001
ADAM_B1 = 0.9
ADAM_B2 = 0.999
ADAM_EPS = 1e-08
ADAM_WD = 0.01
ADAM_STEP = 10

VMEM_LIMIT = 56 * 1024 * 1024
HALO = 8
BQ = 512
N_CHIPS = 4
MESH = pl.DeviceIdType.MESH


def _params(**kw):
    return pltpu.CompilerParams(vmem_limit_bytes=VMEM_LIMIT, **kw)


def _dot(a, b):
    return jnp.dot(a, b, preferred_element_type=F32)


def _dot_nt(a, b):
    return lax.dot_general(a, b, (((1,), (1,)), ((), ())), preferred_element_type=F32)


def _dot_tn(a, b):
    return lax.dot_general(a, b, (((0,), (0,)), ((), ())), preferred_element_type=F32)


def _split2(x):
    hi = x.astype(BF16)
    lo = (x - hi.astype(F32)).astype(BF16)
    return hi, lo


def _split3(x):
    hi = x.astype(BF16)
    r = x - hi.astype(F32)
    mid = r.astype(BF16)
    lo = (r - mid.astype(F32)).astype(BF16)
    return hi, mid, lo


def mm_nn(a, b, *, add=None, out_dtype=F32, parts=1, tm=1024, tn=512, name):
    m, k = a.shape
    n = b.shape[1]
    np_ = n // parts
    nb = np_ // tn
    tm = min(tm, m)
    assert m % tm == 0 and np_ % tn == 0

    def body(*refs):
        if add is None:
            a_ref, b_ref, o_ref = refs
            acc = _dot(a_ref[...].astype(BF16), b_ref[...])
        else:
            a_ref, b_ref, r_ref, o_ref = refs
            acc = _dot(a_ref[...].astype(BF16), b_ref[...]) + r_ref[...]
        o_ref[...] = acc.astype(out_dtype).reshape(o_ref.shape)

    in_specs = [pl.BlockSpec((tm, k), lambda i, j: (i, 0)), pl.BlockSpec((k, tn), lambda i, j: (0, j))]
    args = [a, b]
    if add is not None:
        in_specs.append(pl.BlockSpec((tm, tn), lambda i, j: (i, j)))
        args.append(add)
    if parts == 1:
        out_spec = pl.BlockSpec((tm, tn), lambda i, j: (i, j))
        out_shape = jax.ShapeDtypeStruct((m, n), out_dtype)
    else:
        out_spec = pl.BlockSpec((1, tm, tn), lambda i, j: (j // nb, i, j % nb))
        out_shape = jax.ShapeDtypeStruct((parts, m, np_), out_dtype)
    return pl.pallas_call(body, grid=(m // tm, n // tn), in_specs=in_specs, out_specs=out_spec,
                          out_shape=out_shape, compiler_params=_params(), name=name)(*args)


def mm_nt(a3, b, *, out_dtype=F32, tm=1024, tn=512, name):
    p, m, kp = a3.shape
    n = b.shape[0]
    tm = min(tm, m)
    assert m % tm == 0 and n % tn == 0 and b.shape[1] == p * kp

    def body(a_ref, b_ref, o_ref, acc_ref):
        part = pl.program_id(2)
        prod = _dot_nt(a_ref[0].astype(BF16), b_ref[...])

        @pl.when(part == 0)
        def _():
            acc_ref[...] = prod

        @pl.when(part > 0)
        def _():
            acc_ref[...] += prod

        @pl.when(part == p - 1)
        def _():
            o_ref[...] = acc_ref[...].astype(out_dtype)

    return pl.pallas_call(
        body, grid=(m // tm, n // tn, p),
        in_specs=[pl.BlockSpec((1, tm, kp), lambda i, j, q: (q, i, 0)), pl.BlockSpec((tn, kp), lambda i, j, q: (j, q))],
        out_specs=pl.BlockSpec((tm, tn), lambda i, j, q: (i, j)),
        out_shape=jax.ShapeDtypeStruct((m, n), out_dtype),
        scratch_shapes=[pltpu.VMEM((tm, tn), F32)],
        compiler_params=_params(), name=name)(a3, b)


def mm_tn(a, b3, *, tk=512, tn=512, tt=2048, name):
    t, k = a.shape
    p, _, np_ = b3.shape
    nb = np_ // tn
    tt = min(tt, t)
    assert t % tt == 0 and k % tk == 0 and np_ % tn == 0

    def body(a_ref, b_ref, o_ref):
        prod = _dot_tn(a_ref[...].astype(BF16), b_ref[0].astype(BF16))

        @pl.when(pl.program_id(2) == 0)
        def _():
            o_ref[...] = prod

        @pl.when(pl.program_id(2) > 0)
        def _():
            o_ref[...] += prod

    return pl.pallas_call(
        body, grid=(k // tk, p * nb, t // tt),
        in_specs=[pl.BlockSpec((tt, tk), lambda i, j, s: (s, i)), pl.BlockSpec((1, tt, tn), lambda i, j, s: (j // nb, s, j % nb))],
        out_specs=pl.BlockSpec((tk, tn), lambda i, j, s: (i, j)),
        out_shape=jax.ShapeDtypeStruct((k, p * np_), F32),
        compiler_params=_params(), name=name)(a, b3)


def rms_mm_nn(h, g, b, *, parts=1, tm=1024, tn=512, name):
    t, d = h.shape
    n = b.shape[1]
    np_ = n // parts
    nb = np_ // tn
    tm = min(tm, t)
    assert t % tm == 0 and np_ % tn == 0

    def body(h_ref, g_ref, b_ref, xn_ref, o_ref):
        @pl.when(pl.program_id(1) == 0)
        def _():
            x = h_ref[...]
            r = lax.rsqrt(jnp.mean(x * x, axis=-1, keepdims=True) + EPS)
            xn_ref[...] = (x * r * g_ref[...]).astype(BF16)

        o_ref[...] = _dot(xn_ref[...], b_ref[...]).reshape(o_ref.shape)

    if parts == 1:
        out_spec = pl.BlockSpec((tm, tn), lambda i, j: (i, j))
        out_shape = jax.ShapeDtypeStruct((t, n), F32)
    else:
        out_spec = pl.BlockSpec((1, tm, tn), lambda i, j: (j // nb, i, j % nb))
        out_shape = jax.ShapeDtypeStruct((parts, t, np_), F32)
    row = pl.BlockSpec((tm, d), lambda i, j: (i, 0))
    return pl.pallas_call(
        body, grid=(t // tm, n // tn),
        in_specs=[row, pl.BlockSpec((1, d), lambda i, j: (0, 0)), pl.BlockSpec((d, tn), lambda i, j: (0, j))],
        out_specs=[row, out_spec], out_shape=[jax.ShapeDtypeStruct((t, d), BF16), out_shape],
        compiler_params=_params(), name=name)(h, g, b)


def mm_nt_rms_bwd(a3, b, h, g, dres, *, name, tm=512):
    p, t, kp = a3.shape
    d = b.shape[0]
    tm = min(tm, t)
    assert t % tm == 0 and b.shape[1] == p * kp

    def body(a_ref, b_ref, h_ref, g_ref, dres_ref, dh_ref, dg_ref, acc_ref):
        i = pl.program_id(0)
        part = pl.program_id(1)
        prod = _dot_nt(a_ref[0].astype(BF16), b_ref[...])

        @pl.when(part == 0)
        def _():
            acc_ref[...] = prod

        @pl.when(part > 0)
        def _():
            acc_ref[...] += prod

        @pl.when(part == p - 1)
        def _():
            x = h_ref[...]
            dy = acc_ref[...]
            r = lax.rsqrt(jnp.mean(x * x, axis=-1, keepdims=True) + EPS)
            gy = dy * g_ref[...]
            dot = jnp.mean(gy * x, axis=-1, keepdims=True)
            dh_ref[...] = dres_ref[...] + r * gy - x * (r * r * r * dot)
            _acc_rows(dg_ref, jnp.sum(dy * x * r, axis=0, keepdims=True), i == 0)

    row = pl.BlockSpec((tm, d), lambda i, q: (i, 0))
    vec = pl.BlockSpec((1, d), lambda i, q: (0, 0))
    return pl.pallas_call(
        body, grid=(t // tm, p),
        in_specs=[pl.BlockSpec((1, tm, kp), lambda i, q: (q, i, 0)), pl.BlockSpec((d, kp), lambda i, q: (0, q)),
                  row, vec, row],
        out_specs=[row, vec],
        out_shape=[jax.ShapeDtypeStruct((t, d), F32), jax.ShapeDtypeStruct((1, d), F32)],
        scratch_shapes=[pltpu.VMEM((tm, d), F32)], compiler_params=_params(), name=name)(a3, b, h, g, dres)


def _causal3(x, w):
    return w[0:1] * pltpu.roll(x, 2, 0) + w[1:2] * pltpu.roll(x, 1, 0) + w[2:3] * x


def _causal3_taps(x_ext, w, tm):
    x2 = pltpu.roll(x_ext, 2, 0)
    x1 = pltpu.roll(x_ext, 1, 0)
    y = w[0:1] * x2 + w[1:2] * x1 + w[2:3] * x_ext
    return y, (x2[HALO:HALO + tm], x1[HALO:HALO + tm], x_ext[HALO:HALO + tm])


def _anticausal3(z, w):
    n = z.shape[0]
    return w[2:3] * z + w[1:2] * pltpu.roll(z, n - 1, 0) + w[0:1] * pltpu.roll(z, n - 2, 0)


def _prev_spec(part, tm, tc, nrow8):
    del nrow8
    return pl.BlockSpec((1, HALO, tc), lambda j, i: (part, jnp.maximum(i * (tm // HALO) - 1, 0), j))


def _next_spec(part, tm, tc, nrow8):
    return pl.BlockSpec((1, HALO, tc), lambda j, i: (part, jnp.minimum((i + 1) * (tm // HALO), nrow8 - 1), j))


def _tile_spec(part, tm, tc):
    return pl.BlockSpec((1, tm, tc), lambda j, i: (part, i, j))


def _acc_rows(ref, val, first):
    @pl.when(first)
    def _():
        ref[...] = val

    @pl.when(jnp.logical_not(first))
    def _():
        ref[...] += val


def ffn_act_down_fwd(up2, cw2, w_down, h, *, name, tm=256, tc=1408):
    _, t, f = up2.shape
    d = h.shape[1]
    tm = min(tm, t)

    def body(g_ref, v_ref, gp_ref, vp_ref, w_ref, wd_ref, h_ref, o_ref, act_ref):
        keep = jnp.where(pl.program_id(0) == 0, 0.0, 1.0)
        for cc in range(f // tc):
            cols = slice(cc * tc, (cc + 1) * tc)
            g_ext = jnp.concatenate([gp_ref[0, :, cols] * keep, g_ref[0, :, cols]], axis=0)
            v_ext = jnp.concatenate([vp_ref[0, :, cols] * keep, v_ref[0, :, cols]], axis=0)
            ug = _causal3(g_ext, w_ref[0, :, cols])[HALO:]
            uv = _causal3(v_ext, w_ref[1, :, cols])[HALO:]
            act_ref[:, cols] = (ug * jax.nn.sigmoid(ug) * uv).astype(BF16)
        o_ref[...] = _dot(act_ref[...], wd_ref[...]) + h_ref[...]

    tile = lambda part: pl.BlockSpec((1, tm, f), lambda i: (part, i, 0))
    prev = lambda part: pl.BlockSpec((1, HALO, f), lambda i: (part, jnp.maximum(i * (tm // HALO) - 1, 0), 0))
    row = pl.BlockSpec((tm, d), lambda i: (i, 0))
    return pl.pallas_call(
        body, grid=(t // tm,),
        in_specs=[tile(0), tile(1), prev(0), prev(1), pl.BlockSpec((2, 3, f), lambda i: (0, 0, 0)),
                  pl.BlockSpec((f, d), lambda i: (0, 0)), row],
        out_specs=[row, pl.BlockSpec((tm, f), lambda i: (i, 0))],
        out_shape=[jax.ShapeDtypeStruct((t, d), F32), jax.ShapeDtypeStruct((t, f), BF16)],
        compiler_params=_params(), name=name)(up2, up2, up2, up2, cw2, w_down, h)


def ffn_act_bwd(dh, w_down, up2, cw2, *, name, tm=256, tc=1408):
    _, t, f = up2.shape
    d = dh.shape[1]
    n8 = t // HALO

    def body(d_ref, dn_ref, wd_ref, g_ref, v_ref, gp_ref, vp_ref, gn_ref, vn_ref, wg_ref, wv_ref, dup_ref, dw_ref):
        i = pl.program_id(1)
        first = i == 0
        keep_p = jnp.where(first, 0.0, 1.0)
        keep_n = jnp.where(i == pl.num_programs(1) - 1, 0.0, 1.0)
        wg = wg_ref[0]
        wv = wv_ref[0]
        g_ext = jnp.concatenate([gp_ref[0] * keep_p, g_ref[0], gn_ref[0]], axis=0)
        v_ext = jnp.concatenate([vp_ref[0] * keep_p, v_ref[0], vn_ref[0]], axis=0)
        dh_ext = jnp.concatenate([d_ref[...], dn_ref[...] * keep_n], axis=0)
        d_ext = _dot_nt(dh_ext.astype(BF16), wd_ref[...])
        ug, (g2, g1, g0) = _causal3_taps(g_ext, wg, tm)
        uv, (v2, v1, v0) = _causal3_taps(v_ext, wv, tm)
        ug = ug[HALO:]
        uv = uv[HALO:]
        s = jax.nn.sigmoid(ug)
        dg = d_ext * uv * (s * (1.0 + ug * (1.0 - s)))
        dv = d_ext * (ug * s)
        dup_ref[0] = _anticausal3(dg, wg)[:tm].astype(BF16)
        dup_ref[1] = _anticausal3(dv, wv)[:tm].astype(BF16)
        dgt = dg[:tm]
        dvt = dv[:tm]
        zero = jnp.zeros((HALO - 3, tc), F32)
        rows_g = [jnp.sum(dgt * x, axis=0, keepdims=True) for x in (g2, g1, g0)] + [zero]
        rows_v = [jnp.sum(dvt * x, axis=0, keepdims=True) for x in (v2, v1, v0)] + [zero]
        _acc_rows(dw_ref, jnp.stack([jnp.concatenate(rows_g, axis=0), jnp.concatenate(rows_v, axis=0)]), first)

    wspec = lambda part: pl.BlockSpec((1, 3, tc), lambda j, i: (part, 0, j))
    return pl.pallas_call(
        body, grid=(f // tc, t // tm),
        in_specs=[pl.BlockSpec((tm, d), lambda j, i: (i, 0)),
                  pl.BlockSpec((HALO, d), lambda j, i: (jnp.minimum((i + 1) * (tm // HALO), n8 - 1), 0)),
                  pl.BlockSpec((tc, d), lambda j, i: (j, 0)),
                  _tile_spec(0, tm, tc), _tile_spec(1, tm, tc), _prev_spec(0, tm, tc, n8), _prev_spec(1, tm, tc, n8),
                  _next_spec(0, tm, tc, n8), _next_spec(1, tm, tc, n8), wspec(0), wspec(1)],
        out_specs=[pl.BlockSpec((2, tm, tc), lambda j, i: (0, i, j)), pl.BlockSpec((2, HALO, tc), lambda j, i: (0, 0, j))],
        out_shape=[jax.ShapeDtypeStruct((2, t, f), BF16), jax.ShapeDtypeStruct((2, HALO, f), F32)],
        compiler_params=_params(), name=name)(dh, dh, w_down, up2, up2, up2, up2, up2, up2, cw2, cw2)


def conv_mix_fwd(proj3, ck, *, name, tm=512, tc=512):
    _, t, c = proj3.shape
    n8 = t // HALO

    def body(b_ref, c_ref, u_ref, cp_ref, up_ref, w_ref, o_ref):
        keep = jnp.where(pl.program_id(1) == 0, 0.0, 1.0)
        cu_ext = jnp.concatenate([cp_ref[0] * up_ref[0] * keep, c_ref[0] * u_ref[0]], axis=0)
        o_ref[...] = (b_ref[0] * _causal3(cu_ext, w_ref[0])[HALO:]).astype(BF16)

    return pl.pallas_call(
        body, grid=(c // tc, t // tm),
        in_specs=[_tile_spec(0, tm, tc), _tile_spec(1, tm, tc), _tile_spec(2, tm, tc), _prev_spec(1, tm, tc, n8),
                  _prev_spec(2, tm, tc, n8), pl.BlockSpec((1, 3, tc), lambda j, i: (0, 0, j))],
        out_specs=pl.BlockSpec((tm, tc), lambda j, i: (i, j)),
        out_shape=jax.ShapeDtypeStruct((t, c), BF16), compiler_params=_params(), name=name)(proj3, proj3, proj3, proj3, proj3, ck)


def conv_mix_bwd(dh, w_out, proj3, ck, *, name, tm=512, tc=512):
    _, t, c = proj3.shape
    d = dh.shape[1]
    n8 = t // HALO

    def body(d_ref, dn_ref, wo_ref, b_ref, c_ref, u_ref, cp_ref, up_ref, bn_ref, w_ref, dp_ref, dw_ref):
        i = pl.program_id(1)
        first = i == 0
        keep_p = jnp.where(first, 0.0, 1.0)
        keep_n = jnp.where(i == pl.num_programs(1) - 1, 0.0, 1.0)
        w = w_ref[0]
        cu_ext = jnp.concatenate([cp_ref[0] * up_ref[0] * keep_p, c_ref[0] * u_ref[0]], axis=0)
        cv, (x2, x1, x0) = _causal3_taps(cu_ext, w, tm)
        cv = cv[HALO:]
        dh_ext = jnp.concatenate([d_ref[...], dn_ref[...] * keep_n], axis=0)
        d_ext = _dot_nt(dh_ext.astype(BF16), wo_ref[...])
        dyt = d_ext[:tm]
        b_ext = jnp.concatenate([b_ref[0], bn_ref[0]], axis=0)
        dcv = d_ext * b_ext
        dcu = _anticausal3(dcv, w)[:tm]
        dp_ref[0] = (dyt * cv).astype(BF16)
        dp_ref[1] = (dcu * u_ref[0]).astype(BF16)
        dp_ref[2] = (dcu * c_ref[0]).astype(BF16)
        dcvt = dcv[:tm]
        rows = [jnp.sum(dcvt * x, axis=0, keepdims=True) for x in (x2, x1, x0)] + [jnp.zeros((HALO - 3, tc), F32)]
        _acc_rows(dw_ref, jnp.concatenate(rows, axis=0)[None], first)

    return pl.pallas_call(
        body, grid=(c // tc, t // tm),
        in_specs=[pl.BlockSpec((tm, d), lambda j, i: (i, 0)),
                  pl.BlockSpec((HALO, d), lambda j, i: (jnp.minimum((i + 1) * (tm // HALO), n8 - 1), 0)),
                  pl.BlockSpec((tc, d), lambda j, i: (j, 0)),
                  _tile_spec(0, tm, tc), _tile_spec(1, tm, tc), _tile_spec(2, tm, tc),
                  _prev_spec(1, tm, tc, n8), _prev_spec(2, tm, tc, n8),
                  _next_spec(0, tm, tc, n8), pl.BlockSpec((1, 3, tc), lambda j, i: (0, 0, j))],
        out_specs=[pl.BlockSpec((3, tm, tc), lambda j, i: (0, i, j)), pl.BlockSpec((1, HALO, tc), lambda j, i: (0, 0, j))],
        out_shape=[jax.ShapeDtypeStruct((3, t, c), BF16), jax.ShapeDtypeStruct((1, HALO, c), F32)],
        compiler_params=_params(), name=name)(dh, dh, w_out, proj3, proj3, proj3, proj3, proj3, proj3, ck)


def _head_sums(x, bd):
    hi, lo = _split2(x)
    return _dot(hi, bd) + _dot(lo, bd)


def attn_prep_fwd(proj, gq, gk, fbias, bd, *, name, tm=256):
    t = proj.shape[0]

    def body(q_ref, k_ref, v_ref, f_ref, gq_ref, gk_ref, fb_ref, bd_ref, qs_ref, kn_ref, vb_ref, lf_ref):
        bd = bd_ref[...]

        def headnorm(x_ref, g_ref, o_ref, scale):
            for c in range(MIX // 128):
                sl = slice(128 * c, 128 * (c + 1))
                x = x_ref[:, sl]
                r = lax.rsqrt(_head_sums(x * x, bd) * (1.0 / HEAD_DIM) + EPS)
                o_ref[:, sl] = (x * r * (g_ref[:, sl] * scale)).astype(BF16)

        headnorm(q_ref, gq_ref, qs_ref, SCALE)
        headnorm(k_ref, gk_ref, kn_ref, 1.0)
        vb_ref[...] = v_ref[...].astype(BF16)
        fl = f_ref[...] + fb_ref[...]
        logf = jnp.minimum(fl, 0.0) - jnp.log(1.0 + jnp.exp(-jnp.abs(fl)))
        lf_ref[...] = logf.T[0:H_FOX, :]

    col = lambda c: pl.BlockSpec((tm, MIX), lambda i: (i, c))
    vec = pl.BlockSpec((1, MIX), lambda i: (0, 0))
    out = pl.BlockSpec((tm, MIX), lambda i: (i, 0))
    return pl.pallas_call(
        body, grid=(t // tm,),
        in_specs=[col(0), col(1), col(2), pl.BlockSpec((tm, 128), lambda i: (i, 3 * MIX // 128)), vec, vec,
                  pl.BlockSpec((1, 128), lambda i: (0, 0)), pl.BlockSpec((128, 128), lambda i: (0, 0))],
        out_specs=[out, out, out, pl.BlockSpec((H_FOX, tm), lambda i: (0, i))],
        out_shape=[jax.ShapeDtypeStruct((t, MIX), BF16)] * 3 + [jax.ShapeDtypeStruct((H_FOX, t), F32)],
        compiler_params=_params(), name=name)(proj, proj, proj, proj, gq, gk, fbias, bd)


def attn_prep_bwd(proj, dqs, dkn, dv, dfl, gq, gk, bd, *, name, tm=256):
    t = proj.shape[0]

    def body(q_ref, k_ref, dq_ref, dk_ref, dv_ref, dfl_ref, gq_ref, gk_ref, bd_ref, dp_ref, dgq_ref, dgk_ref):
        bd = bd_ref[...]
        first = pl.program_id(0) == 0

        def back(x_ref, d_ref, g_ref, col0, scale, dg_ref):
            parts = []
            for c in range(MIX // 128):
                sl = slice(128 * c, 128 * (c + 1))
                x = x_ref[:, sl]
                r = lax.rsqrt(_head_sums(x * x, bd) * (1.0 / HEAD_DIM) + EPS)
                dn = d_ref[:, sl] * scale
                gy = dn * g_ref[:, sl]
                hs = _head_sums(gy * x, bd) * (1.0 / HEAD_DIM)
                dp_ref[:, col0 + 128 * c:col0 + 128 * (c + 1)] = (r * gy - x * (r * r * r * hs)).astype(BF16)
                parts.append(jnp.sum(dn * x * r, axis=0, keepdims=True))
            _acc_rows(dg_ref, jnp.concatenate(parts, axis=1), first)

        back(q_ref, dq_ref, gq_ref, 0, SCALE, dgq_ref)
        back(k_ref, dk_ref, gk_ref, MIX, 1.0, dgk_ref)
        dp_ref[:, 2 * MIX:3 * MIX] = dv_ref[...].astype(BF16)
        dp_ref[:, 3 * MIX:] = dfl_ref[...]

    col = lambda c: pl.BlockSpec((tm, MIX), lambda i: (i, c))
    row = pl.BlockSpec((tm, MIX), lambda i: (i, 0))
    vec = pl.BlockSpec((1, MIX), lambda i: (0, 0))
    return pl.pallas_call(
        body, grid=(t // tm,),
        in_specs=[col(0), col(1), row, row, row, pl.BlockSpec((tm, 128), lambda i: (i, 0)), vec, vec,
                  pl.BlockSpec((128, 128), lambda i: (0, 0))],
        out_specs=[pl.BlockSpec((tm, ATTN_IN_PAD), lambda i: (i, 0)), vec, vec],
        out_shape=[jax.ShapeDtypeStruct((t, ATTN_IN_PAD), BF16), jax.ShapeDtypeStruct((1, MIX), F32),
                   jax.ShapeDtypeStruct((1, MIX), F32)],
        compiler_params=_params(), name=name)(proj, proj, dqs, dkn, dv, dfl, gq, gk, bd)


def gate_cumsum(logf3, tri, *, name):
    nc, r, _ = logf3.shape

    def body(x_ref, tri_ref, o_ref):
        tri_m = tri_ref[...]

        def step(c, carry):
            hi, mid, lo = _split3(x_ref[c])
            cs = _dot(hi, tri_m) + _dot(mid, tri_m) + _dot(lo, tri_m) + carry
            o_ref[c] = cs
            return cs[:, 127:128]

        lax.fori_loop(0, nc, step, jnp.zeros((r, 1), F32))

    return pl.pallas_call(body, out_shape=jax.ShapeDtypeStruct(logf3.shape, F32), compiler_params=_params(),
                          name=name)(logf3, tri)


def gate_cumsum_bwd(dcum3, logf3, tri, *, name):
    nc, r, _ = dcum3.shape

    def body(x_ref, lf_ref, tri_ref, o_ref, s_ref):
        tri_m = tri_ref[...]

        def step(n, carry):
            car, tot = carry
            c = nc - 1 - n
            hi, mid, lo = _split3(x_ref[c])
            cs = _dot(hi, tri_m) + _dot(mid, tri_m) + _dot(lo, tri_m) + car
            dl = cs * (1.0 - jnp.exp(lf_ref[c]))
            o_ref[c] = dl
            return cs[:, 0:1], tot + dl

        _, tot = lax.fori_loop(0, nc, step, (jnp.zeros((r, 1), F32), jnp.zeros((r, 128), F32)))
        s_ref[...] = jnp.broadcast_to(jnp.sum(tot, axis=1, keepdims=True), tot.shape)

    return pl.pallas_call(body, out_shape=[jax.ShapeDtypeStruct(dcum3.shape, F32), jax.ShapeDtypeStruct((r, 128), F32)],
                          compiler_params=_params(), name=name)(dcum3, logf3, tri)


def _causal_iota():
    row = lax.broadcasted_iota(jnp.int32, (BQ, BQ), 0)
    col = lax.broadcasted_iota(jnp.int32, (BQ, BQ), 1)
    return row, col


def _head_specs(nj, head0):
    qin = pl.BlockSpec((1, BQ, HEAD_DIM), lambda h, i: (h + head0, i, 0))
    kin = pl.BlockSpec((1, nj, BQ, HEAD_DIM), lambda h, i: (h + head0, 0, 0, 0))
    qin2 = pl.BlockSpec((1, BQ, 2 * HEAD_DIM), lambda h, i: (h + head0, i, 0))
    kin2 = pl.BlockSpec((1, nj, BQ, 2 * HEAD_DIM), lambda h, i: (h + head0, 0, 0, 0))
    qspec = pl.BlockSpec((1, BQ, HEAD_DIM), lambda h, i: (h, i, 0))
    kspec2 = pl.BlockSpec((1, nj, BQ, 2 * HEAD_DIM), lambda h, i: (h, 0, 0, 0))
    return qin, kin, qin2, kin2, qspec, kspec2


STOP = -105.0
STOP_WIDE = -115.0
FIXED_REF_MAX = 40.0


def _store_kmax(k_ref, kmax_ref, nj):
    def step(j, mx):
        kf = k_ref[0, j].astype(F32)
        return jnp.maximum(mx, jnp.max(jnp.sum(kf * kf, axis=1, keepdims=True), axis=0, keepdims=True))

    mx = lax.fori_loop(0, nj, step, jnp.zeros((1, 1), F32))
    kmax_ref[...] = jnp.broadcast_to(jnp.sqrt(mx), kmax_ref.shape)


def _qk_bound(q, kmax_ref):
    qf = q.astype(F32)
    return jnp.sqrt(jnp.sum(qf * qf, axis=1, keepdims=True)) * kmax_ref[0:1, 0:1] * 1.001


def _first_and_last_step():
    h, i = pl.program_id(0), pl.program_id(1)
    first = jnp.logical_and(h == 0, i == 0)
    last = jnp.logical_and(h == pl.num_programs(0) - 1, i == pl.num_programs(1) - 1)
    return first, last


def fox_fwd(qs, kn4, va4, fcol, frow4, *, name, gather=None):
    _, t, dh = qs.shape
    nh = H_FOX
    nj = t // BQ

    def body(*refs):
        if gather is None:
            q_ref, k_ref, v_ref, fc_ref, fr_ref, o_ref, lse_ref, kmax_ref = refs
        else:
            q_ref, k_ref, v_ref, fc_ref, fr_ref, src_ref, o_ref, lse_ref, dst_ref, kmax_ref = refs[:10]
            first_step, last_step = _first_and_last_step()

            @pl.when(first_step)
            def _():
                _chip_gather(src_ref, dst_ref, *refs[10:])[0]()

        i = pl.program_id(1)

        @pl.when(i == 0)
        def _():
            _store_kmax(k_ref, kmax_ref, nj)

        q = q_ref[0]
        fq = fc_ref[0]
        bound = _qk_bound(q, kmax_ref)
        row, col = _causal_iota()

        def gate_at_block_end(j):
            return fr_ref[0, j][:, BQ - 1:BQ]

        def pv(p, j):
            p_hi, p_lo = _split2(p)
            return _dot(p_hi, v_ref[0, j]) + _dot(p_lo, v_ref[0, j])

        def walk(block, live, init):
            carry = block(i, init, True)

            def cond(c):
                n, carry = c
                return jnp.logical_and(n < i, live(jnp.maximum(i - 1 - n, 0), carry))

            _, carry = lax.while_loop(cond, lambda c: (c[0] + 1, block(i - 1 - c[0], c[1], False)), (0, carry))
            return carry

        def fixed_reference(_):
            shift = fq - bound

            def probs(j, offset):
                return jnp.exp(_dot_nt(q, k_ref[0, j]) + (shift + offset) - fr_ref[0, j])

            def live(c):
                n, acc = c
                gate = gate_at_block_end(jnp.maximum(i - 1 - n, 0))
                return jnp.logical_and(n < i, jnp.max(fq - gate - jnp.log(acc[:, dh:dh + 1])) >= STOP_WIDE)

            def two_blocks(c):
                n, acc = c
                ja = i - 1 - n
                jb = i - 2 - n
                absent = jnp.where(jb >= 0, 0.0, NEG)
                jb = jnp.maximum(jb, 0)
                return n + 2, acc + (pv(probs(ja, 0.0), ja) + pv(probs(jb, absent), jb))

            acc = pv(jnp.where(col <= row, probs(i, 0.0), 0.0), i)
            _, acc = lax.while_loop(live, two_blocks, (0, acc))
            l = acc[:, dh:dh + 1]
            return acc[:, :dh] / l, bound + jnp.log(l)

        def running_maximum(_):
            def block(j, carry, diag):
                m, acc = carry
                s = _dot_nt(q, k_ref[0, j]) + fq - fr_ref[0, j]
                if diag:
                    s = jnp.where(col <= row, s, NEG)
                m_new = jnp.maximum(m, jnp.max(s, axis=1, keepdims=True))
                return m_new, jnp.exp(m - m_new) * acc + pv(jnp.exp(s - m_new), j)

            def live(j, carry):
                return jnp.max(bound + fq - gate_at_block_end(j) - carry[0]) >= STOP

            m, acc = walk(block, live, (jnp.full((BQ, 1), NEG, F32), jnp.zeros((BQ, 2 * dh), F32)))
            l = acc[:, dh:dh + 1]
            return acc[:, :dh] / l, m + jnp.log(l)

        o, lse = lax.cond(jnp.max(bound) < FIXED_REF_MAX, fixed_reference, running_maximum, 0)
        o_ref[0] = o
        lse_ref[0] = lse

        if gather is not None:
            @pl.when(last_step)
            def _():
                _chip_gather(src_ref, dst_ref, *refs[10:])[1]()

    qin, kin, _, kin2, qspec, _ = _head_specs(nj, 0)
    cspec = pl.BlockSpec((1, BQ, 1), lambda h, i: (h, i, 0))
    in_specs = [qin, kin, kin2, cspec, pl.BlockSpec((1, nj, 1, BQ), lambda h, i: (h, 0, 0, 0))]
    out_specs = [qspec, cspec]
    out_shape = [jax.ShapeDtypeStruct((nh, t, dh), F32), jax.ShapeDtypeStruct((nh, t, 1), F32)]
    scratch = [pltpu.VMEM((8, 128), F32)]
    args = [qs, kn4, va4, fcol, frow4]
    if gather is not None:
        in_specs.append(_ANY)
        out_specs.append(_ANY)
        out_shape.append(jax.ShapeDtypeStruct((N_CHIPS,) + gather.shape, gather.dtype))
        scratch += _chip_sems()
        args.append(gather)
    return pl.pallas_call(body, grid=(nh, nj), in_specs=in_specs, out_specs=out_specs, out_shape=out_shape,
                          scratch_shapes=scratch, compiler_params=_params(), name=name)(*args)


def fox_bwd(qs, kn4, v4, qa, doa, fcol, frow4, o, do, lse, *, name, scatter=None):
    _, t, dh = qs.shape
    nh = H_FOX
    nj = t // BQ

    def body(*refs):
        q_ref, k_ref, v_ref, qa_ref, doa_ref, fc_ref, fr_ref, o_ref, do_ref, lse_ref = refs[:10]
        if scatter is None:
            dq_ref, dkv_ref, dfk_ref, kmax_ref = refs[10:]
        else:
            g_ref, dq_ref, dkv_ref, dfk_ref, land_ref, kmax_ref = refs[10:16]
            first_step, last_step = _first_and_last_step()

            @pl.when(first_step)
            def _():
                _chip_scatter(g_ref, land_ref, *refs[16:])[0]()

        i = pl.program_id(1)

        @pl.when(i == 0)
        def _():
            dkv_ref[...] = jnp.zeros_like(dkv_ref)
            dfk_ref[...] = jnp.zeros_like(dfk_ref)
            _store_kmax(k_ref, kmax_ref, nj)

        q = q_ref[0]
        do_b = do_ref[0]
        fq = fc_ref[0]
        lse_q = lse_ref[0]
        dd = jnp.sum(do_b.astype(F32) * o_ref[0], axis=1, keepdims=True)
        rhs = jnp.concatenate([qa_ref[0], doa_ref[0]], axis=0)
        edge = _qk_bound(q, kmax_ref) + fq - lse_q

        def negligible(j):
            return jnp.logical_and(j < i, jnp.max(edge - fr_ref[0, j][:, BQ - 1:BQ]) < STOP_WIDE)

        first = lax.while_loop(negligible, lambda j: j + 1, 0)

        shift = fq - lse_q

        def block(j, offset, diag):
            k = k_ref[0, j]
            p = jnp.exp(_dot_nt(q, k) + (shift + offset) - fr_ref[0, j])
            if diag:
                row, col = _causal_iota()
                p = jnp.where(col <= row, p, 0.0)
            ds = p * (_dot_nt(do_b, v_ref[0, j]) - dd)
            ds_b = ds.astype(BF16)
            dkv_ref[0, j] += _dot_tn(jnp.concatenate([ds_b, p.astype(BF16)], axis=0), rhs)
            dfk_ref[0, j] -= jnp.sum(ds, axis=0, keepdims=True)
            return _dot(ds_b, k)

        def two_blocks(n, dq):
            ja = first + 2 * n
            jb = ja + 1
            absent = jnp.where(jb < i, 0.0, NEG)
            jb = jnp.minimum(jb, i - 1)
            return dq + (block(ja, 0.0, False) + block(jb, absent, False))

        dq = lax.fori_loop(0, (i - first + 1) // 2, two_blocks, jnp.zeros((BQ, dh), F32))
        dq_ref[0] = dq + block(i, 0.0, True)

        if scatter is not None:
            @pl.when(last_step)
            def _():
                _chip_scatter(g_ref, land_ref, *refs[16:])[1]()

    qin, kin, qin2, _, qspec, kspec2 = _head_specs(nj, 0)
    cspec = pl.BlockSpec((1, BQ, 1), lambda h, i: (h, i, 0))
    rspec = pl.BlockSpec((1, nj, 1, BQ), lambda h, i: (h, 0, 0, 0))
    in_specs = [qin, kin, kin, qin2, qin2, cspec, rspec, qspec, qin, cspec]
    out_specs = [qspec, kspec2, rspec]
    out_shape = [jax.ShapeDtypeStruct((nh, t, dh), F32), jax.ShapeDtypeStruct((nh, nj, BQ, 2 * dh), F32),
                 jax.ShapeDtypeStruct((nh, nj, 1, BQ), F32)]
    scratch = [pltpu.VMEM((8, 128), F32)]
    args = [qs, kn4, v4, qa, doa, fcol, frow4, o, do, lse]
    if scatter is not None:
        in_specs.append(_ANY)
        out_specs.append(_ANY)
        out_shape.append(jax.ShapeDtypeStruct(scatter.shape, scatter.dtype))
        scratch += _chip_sems()
        args.append(scatter)
    return pl.pallas_call(body, grid=(nh, nj), in_specs=in_specs, out_specs=out_specs, out_shape=out_shape,
                          scratch_shapes=scratch, compiler_params=_params(), name=name)(*args)


def _sb_logs(z, diag):
    e = jnp.exp(-jnp.abs(z))
    sp = jnp.log(1.0 + e)
    logb = jnp.minimum(z, 0.0) - sp
    lom = -jnp.maximum(z, 0.0) - sp
    strict = None
    if diag:
        row, col = _causal_iota()
        strict = col < row
        lom = jnp.where(strict, lom, 0.0)
    return logb, lom, e, strict


def sb_fwd(qs, kn4, va4, tri, *, name, gather=None):
    _, t, dh = qs.shape
    nh = H_SB
    nj = t // BQ
    assert nj <= 128

    def body(*refs):
        if gather is None:
            q_ref, k_ref, v_ref, tri_ref, o_ref, rs_ref = refs
        else:
            q_ref, k_ref, v_ref, tri_ref, src_ref, o_ref, rs_ref, dst_ref = refs[:8]
            first_step, last_step = _first_and_last_step()

            @pl.when(first_step)
            def _():
                _chip_gather(src_ref, dst_ref, *refs[8:])[0]()

        i = pl.program_id(1)
        q = q_ref[0]
        tri_m = tri_ref[...]
        lane = lax.broadcasted_iota(jnp.int32, (BQ, 128), 1)

        def block(j, carry, diag):
            run, acc, rall = carry
            logb, lom, _, strict = _sb_logs(_dot_nt(q, k_ref[0, j]), diag)
            hi, lo = _split2(lom)
            w = jnp.exp(logb + (_dot(hi, tri_m) + _dot(lo, tri_m)) + run)
            if diag:
                w = jnp.where(strict, w, 0.0)
            acc = acc + _dot(w.astype(BF16), v_ref[0, j])
            rall = jnp.where(lane == j, run, rall)
            return run + jnp.sum(lom, axis=1, keepdims=True), acc, rall

        init = (jnp.zeros((BQ, 1), F32), jnp.zeros((BQ, 2 * dh), F32), jnp.full((BQ, 128), NEG, F32))
        carry = block(i, init, True)

        def cond(c):
            n, carry = c
            return jnp.logical_and(n < i, jnp.max(carry[0]) >= STOP)

        _, (_, acc, rall) = lax.while_loop(cond, lambda c: (c[0] + 1, block(i - 1 - c[0], c[1], False)), (0, carry))
        o_ref[0] = acc[:, :dh].astype(BF16)
        rs_ref[0] = rall

        if gather is not None:
            @pl.when(last_step)
            def _():
                _chip_gather(src_ref, dst_ref, *refs[8:])[1]()

    qin, kin, _, kin2, qspec, _ = _head_specs(nj, H_FOX)
    rspec = pl.BlockSpec((1, BQ, 128), lambda h, i: (h, i, 0))
    in_specs = [qin, kin, kin2, pl.BlockSpec((BQ, BQ), lambda h, i: (0, 0))]
    out_specs = [qspec, rspec]
    out_shape = [jax.ShapeDtypeStruct((nh, t, dh), BF16), jax.ShapeDtypeStruct((nh, t, 128), F32)]
    scratch = []
    args = [qs, kn4, va4, tri]
    if gather is not None:
        in_specs.append(_ANY)
        out_specs.append(_ANY)
        out_shape.append(jax.ShapeDtypeStruct((N_CHIPS,) + gather.shape, gather.dtype))
        scratch += _chip_sems()
        args.append(gather)
    return pl.pallas_call(body, grid=(nh, nj), in_specs=in_specs, out_specs=out_specs, out_shape=out_shape,
                          scratch_shapes=scratch, compiler_params=_params(), name=name)(*args)


def sb_bwd(qs, kn4, v4, qa, doa, tri, do, rsave, *, name):
    _, t, dh = qs.shape
    nh = H_SB
    nj = t // BQ

    def body(q_ref, k_ref, v_ref, qa_ref, doa_ref, tri_ref, do_ref, rs_ref, dq_ref, dkv_ref):
        i = pl.program_id(1)

        @pl.when(i == 0)
        def _():
            dkv_ref[...] = jnp.zeros_like(dkv_ref)

        q = q_ref[0]
        do_b = do_ref[0]
        tri_m = tri_ref[...]
        rall = rs_ref[0]
        lane = lax.broadcasted_iota(jnp.int32, (BQ, 128), 1)
        rhs = jnp.concatenate([qa_ref[0], doa_ref[0]], axis=0)
        lane1 = lax.broadcasted_iota(jnp.int32, (1, 128), 1)
        unvisited = jnp.logical_and(lane1 < i, jnp.max(rall, axis=0, keepdims=True) < STOP)
        first = jnp.sum(unvisited.astype(jnp.int32))

        def block(j, carry, diag):
            dq, ecar = carry
            k = k_ref[0, j]
            z = _dot_nt(q, k)
            logb, lom, e, strict = _sb_logs(z, diag)
            hi, lo = _split2(lom)
            run = jnp.sum(jnp.where(lane == j, rall, 0.0), axis=1, keepdims=True)
            w = jnp.exp(logb + (_dot(hi, tri_m) + _dot(lo, tri_m)) + run)
            if diag:
                w = jnp.where(strict, w, 0.0)
            da = w * _dot_nt(do_b, v_ref[0, j])
            before = _dot_nt(da.astype(BF16), tri_m) + ecar
            inv = 1.0 / (1.0 + e)
            beta = jnp.where(z >= 0.0, 1.0, e) * inv
            one_minus = jnp.where(z >= 0.0, e, 1.0) * inv
            dz = da * one_minus - before * beta
            if diag:
                dz = jnp.where(strict, dz, 0.0)
            dz_b = dz.astype(BF16)
            dkv_ref[0, j] += _dot_tn(jnp.concatenate([dz_b, w.astype(BF16)], axis=0), rhs)
            return dq + _dot(dz_b, k), ecar + jnp.sum(da, axis=1, keepdims=True)

        carry = lax.fori_loop(first, i, lambda j, c: block(j, c, False),
                              (jnp.zeros((BQ, dh), F32), jnp.zeros((BQ, 1), F32)))
        dq, _ = block(i, carry, True)
        dq_ref[0] = dq

    qin, kin, qin2, _, qspec, kspec2 = _head_specs(nj, H_FOX)
    return pl.pallas_call(
        body, grid=(nh, nj),
        in_specs=[qin, kin, kin, qin2, qin2, pl.BlockSpec((BQ, BQ), lambda h, i: (0, 0)), qin,
                  pl.BlockSpec((1, BQ, 128), lambda h, i: (h, i, 0))],
        out_specs=[qspec, kspec2],
        out_shape=[jax.ShapeDtypeStruct((nh, t, dh), F32), jax.ShapeDtypeStruct((nh, nj, BQ, 2 * dh), F32)],
        compiler_params=_params(), name=name)(qs, kn4, v4, qa, doa, tri, do, rsave)


def loss_head(y, target, *, name, tm=512):
    t, d = y.shape

    def body(y_ref, t_ref, l_ref, dy_ref, acc_ref):
        i = pl.program_id(0)
        diff = y_ref[...] - t_ref[...]
        dy_ref[...] = diff * (1.0 / d)
        part = jnp.sum(diff * diff, axis=0, keepdims=True)

        @pl.when(i == 0)
        def _():
            acc_ref[...] = part

        @pl.when(i > 0)
        def _():
            acc_ref[...] += part

        @pl.when(i == pl.num_programs(0) - 1)
        def _():
            l_ref[...] = jnp.full(l_ref.shape, (0.5 / d) * jnp.sum(acc_ref[...]), F32)

    row = pl.BlockSpec((tm, d), lambda i: (i, 0))
    return pl.pallas_call(
        body, grid=(t // tm,), in_specs=[row, row],
        out_specs=[pl.BlockSpec((8, 128), lambda i: (0, 0)), row],
        out_shape=[jax.ShapeDtypeStruct((8, 128), F32), jax.ShapeDtypeStruct((t, d), F32)],
        scratch_shapes=[pltpu.VMEM((1, d), F32)], compiler_params=_params(), name=name)(y, target)


def _to_heads(a):
    t = a.shape[0]
    return a.reshape(t, N_HEADS, HEAD_DIM).transpose(1, 0, 2)


def _from_heads(a):
    t = a.shape[1]
    return a.transpose(1, 0, 2).reshape(t, MIX)


def _lanes_to_chunks(a):
    r, t = a.shape
    return a.reshape(r, t // 128, 128).transpose(1, 0, 2)


def _chunks_to_lanes(a):
    nc, r, _ = a.shape
    return a.transpose(1, 0, 2).reshape(r, nc * 128)


def _constants():
    idx = jnp.arange(128)
    bd = (idx[:, None] // HEAD_DIM == idx[None, :] // HEAD_DIM).astype(BF16)
    tri_le = (idx[:, None] <= idx[None, :]).astype(BF16)
    tri_ge = (idx[:, None] >= idx[None, :]).astype(BF16)
    jdx = jnp.arange(BQ)
    tri_gt = (jdx[:, None] > jdx[None, :]).astype(BF16)
    return dict(bd=bd, tri_le=tri_le, tri_ge=tri_ge, tri_gt=tri_gt)


def attn_layer_fwd(h, w, cst, gather=None):
    t = h.shape[0]
    nj = t // BQ
    xn, proj = rms_mm_nn(h, w["norm"], w["w_in"], tn=640, name="attn_in_proj")
    qs, kn, vb, logf = attn_prep_fwd(proj, w["gq"], w["gk"], w["fbias"], cst["bd"], name="attn_prep_fwd")
    logf3 = _lanes_to_chunks(logf)
    cum = _chunks_to_lanes(gate_cumsum(logf3, cst["tri_le"], name="gate_cumsum"))
    fcol = cum.reshape(H_FOX, t, 1)
    frow4 = cum.reshape(H_FOX, nj, 1, BQ)
    qh = _to_heads(qs)
    kh4 = _to_heads(kn).reshape(N_HEADS, nj, BQ, HEAD_DIM)
    vh4 = _to_heads(vb).reshape(N_HEADS, nj, BQ, HEAD_DIM)
    ones = jnp.ones(vh4.shape[:-1] + (1,), BF16)
    va4 = jnp.concatenate([vh4, ones, jnp.zeros(vh4.shape[:-1] + (HEAD_DIM - 1,), BF16)], axis=-1)
    if gather is None:
        (o_f, lse), (o_s, rsave), gathered = (fox_fwd(qh, kh4, va4, fcol, frow4, name="fox_fwd"),
                                              sb_fwd(qh, kh4, va4, cst["tri_gt"], name="sb_fwd"), None)
    else:
        o_f, lse, gathered_a = fox_fwd(qh, kh4, va4, fcol, frow4, name="fox_fwd_gather", gather=gather[0])
        o_s, rsave, gathered_b = sb_fwd(qh, kh4, va4, cst["tri_gt"], name="sb_fwd_gather", gather=gather[1])
        gathered = gather[2]((gathered_a, gathered_b))
        w = gathered[0][0]
    o = _from_heads(jnp.concatenate([o_f.astype(BF16), o_s], axis=0))
    h2 = mm_nn(o, w["w_out"], add=h, name="mix_out_proj")
    saved = dict(h=h, xn=xn, proj=proj, logf3=logf3, fcol=fcol, frow4=frow4, qh=qh, kh4=kh4, vh4=vh4,
                 o_f=o_f, lse=lse, rsave=rsave, o=o)
    return h2, saved, gathered


def attn_layer_bwd(dh, w, s, cst, scatter=None):
    t = dh.shape[0]
    dh3 = dh[None]
    do = mm_nt(dh3, w["w_out"], out_dtype=BF16, name="mix_out_bwd_bf16")
    g_w_out = mm_tn(s["o"], dh3, name="mix_out_wgrad")
    doh = _to_heads(do)
    zeros = jnp.zeros_like(doh)
    qa = jnp.concatenate([s["qh"], zeros], axis=-1)
    doa = jnp.concatenate([zeros, doh], axis=-1)
    fox_args = (s["qh"], s["kh4"], s["vh4"], qa, doa, s["fcol"], s["frow4"], s["o_f"], doh, s["lse"])
    if scatter is None:
        (dq_f, dkv_f, dfk), landed = fox_bwd(*fox_args, name="fox_bwd"), None
    else:
        dq_f, dkv_f, dfk, landed = fox_bwd(*fox_args, name="fox_bwd_scatter", scatter=scatter(g_w_out))
    dq_s, dkv_s = sb_bwd(s["qh"], s["kh4"], s["vh4"], qa, doa, cst["tri_gt"], doh, s["rsave"], name="sb_bwd")
    dqs = _from_heads(jnp.concatenate([dq_f, dq_s], axis=0))
    dkv = jnp.concatenate([dkv_f, dkv_s], axis=0).reshape(N_HEADS, t, 2 * HEAD_DIM)
    dkn = _from_heads(dkv[:, :, :HEAD_DIM])
    dv = _from_heads(dkv[:, :, HEAD_DIM:])
    dcum3 = _lanes_to_chunks(dfk.reshape(H_FOX, t))
    dfl3, dbias = gate_cumsum_bwd(dcum3, s["logf3"], cst["tri_ge"], name="gate_cumsum_bwd")
    dfl = jnp.pad(_chunks_to_lanes(dfl3).T, ((0, 0), (0, 128 - H_FOX))).astype(BF16)
    dproj, dgq, dgk = attn_prep_bwd(s["proj"], dqs, dkn, dv, dfl, w["gq"], w["gk"], cst["bd"], name="attn_prep_bwd")
    g_w_in = mm_tn(s["xn"], dproj[None], tn=640, name="attn_in_wgrad")[:, :ATTN_IN]
    dh2, g_norm = mm_nt_rms_bwd(dproj[None], w["w_in"], s["h"], w["norm"], dh, name="attn_in_bwd")
    dgq = dgq.reshape(N_HEADS, HEAD_DIM)
    dgk = dgk.reshape(N_HEADS, HEAD_DIM)
    grads = dict(norm=g_norm[0], w_in=g_w_in, f_bias=dbias[:, 0], fox_q=dgq[:H_FOX].sum(0), fox_k=dgk[:H_FOX].sum(0),
                 sb_q=dgq[H_FOX:].sum(0), sb_k=dgk[H_FOX:].sum(0), w_out=g_w_out)
    return dh2, grads, landed


def conv_layer_fwd(h, w):
    xn, proj3 = rms_mm_nn(h, w["norm"], w["w_in"], parts=3, name="conv_in_proj")
    y = conv_mix_fwd(proj3, w["ck"], name="conv_mix_fwd")
    h2 = mm_nn(y, w["w_out"], add=h, name="mix_out_proj")
    return h2, dict(h=h, xn=xn, proj3=proj3, y=y)


def conv_layer_bwd(dh, w, s):
    dh3 = dh[None]
    g_w_out = mm_tn(s["y"], dh3, name="mix_out_wgrad")
    dproj3, dck = conv_mix_bwd(dh, w["w_out"], s["proj3"], w["ck"], name="conv_mix_bwd")
    g_w_in = mm_tn(s["xn"], dproj3, name="conv_in_wgrad")
    dh2, g_norm = mm_nt_rms_bwd(dproj3, w["w_in"], s["h"], w["norm"], dh, name="conv_in_bwd")
    return dh2, dict(norm=g_norm[0], w_in=g_w_in, ck=dck[0, :3], w_out=g_w_out)


def ffn_layer_fwd(h, w):
    xn, up2 = rms_mm_nn(h, w["norm"], w["w_up"], parts=2, tn=1408, name="ffn_up_proj")
    h2, act = ffn_act_down_fwd(up2, w["cw2"], w["w_down"], h, name="ffn_act_down_fwd")
    return h2, dict(h=h, xn=xn, up2=up2, act=act)


def ffn_layer_bwd(dh, w, s):
    dh3 = dh[None]
    g_w_down = mm_tn(s["act"], dh3, tk=1408, name="ffn_down_wgrad")
    dup2, dcw = ffn_act_bwd(dh, w["w_down"], s["up2"], w["cw2"], name="ffn_act_bwd")
    g_w_up = mm_tn(s["xn"], dup2, tn=1408, name="ffn_up_wgrad")
    dh2, g_norm = mm_nt_rms_bwd(dup2, w["w_up"], s["h"], w["norm"], dh, name="ffn_up_bwd")
    g_cw = jnp.concatenate([dcw[0, :3], dcw[1, :3]], axis=1)
    return dh2, dict(norm=g_norm[0], w_up=g_w_up, cw=g_cw, w_down=g_w_down)


def forward_backward(x, target, wa, wc, wf, *, late_weights=None, late_chunks=None):
    cst = _constants()
    h = x
    saved = []
    layer = 0
    while layer == 0 or layer < len(wf):
        i = layer // 2
        if layer % 2 == 0:
            h, sm, built = attn_layer_fwd(h, wa[i], cst, gather=late_weights if late_weights and layer == 0 else None)
            if built is not None:
                wa, wc, wf = built
        else:
            h, sm = conv_layer_fwd(h, wc[i])
        h, sf = ffn_layer_fwd(h, wf[layer])
        saved.append((sm, sf))
        layer += 1
    depth = len(wf)
    loss_blk, dh = loss_head(h, target, name="loss_head")
    ga, gc, gf = [None] * len(wa), [None] * len(wc), [None] * depth
    landed = None
    for layer in reversed(range(depth)):
        i = layer // 2
        sm, sf = saved[layer]
        dh, gf[layer] = ffn_layer_bwd(dh, wf[layer], sf)
        if layer % 2 == 0:
            chunks = None
            if late_chunks and layer == 0:
                chunks = lambda g_w_out: late_chunks([dict(w_out=g_w_out)] + ga[1:], gc, gf)
            dh, ga[i], got = attn_layer_bwd(dh, wa[i], sm, cst, scatter=chunks)
            landed = got if got is not None else landed
        else:
            dh, gc[i] = conv_layer_bwd(dh, wc[i], sm)
    return loss_blk, dh, ga, gc, gf, landed


def _part_rows(shape, width, row_mult):
    n = 1
    for s in shape:
        n *= s
    rows = -(-n // width)
    return -(-rows // row_mult) * row_mult


def _pack_rows(arrs, width, row_mult, dtype, total_rows=None):
    parts = []
    used = 0
    for a in arrs:
        rows = _part_rows(a.shape, width, row_mult)
        flat = a.astype(dtype).reshape(-1)
        flat = jnp.pad(flat, (0, rows * width - flat.shape[0]))
        parts.append(flat.reshape(rows, width))
        used += rows
    if total_rows is not None and total_rows > used:
        parts.append(jnp.zeros((total_rows - used, width), dtype))
    return jnp.concatenate(parts, axis=0)


def _unpack_rows(packed, shapes, width, row_mult):
    out = []
    off = 0
    for shape in shapes:
        rows = _part_rows(shape, width, row_mult)
        n = 1
        for s in shape:
            n *= s
        out.append(packed[off:off + rows].reshape(-1)[:n].reshape(shape))
        off += rows
    return out


BIG_NAMES = ("attn_w_in", "attn_w_out", "conv_w_in", "conv_w_out", "ffn_w_up", "ffn_w_down")
BIG_AXIS = {"attn_w_in": 2, "attn_w_out": 1, "conv_w_in": 2, "conv_w_out": 1, "ffn_w_up": 2, "ffn_w_down": 1}
BIG_WIDTH = 1024
BIG_ROW_MULT = 16
BIG_TILE = 512
SMALL_TILE = 128
SMALL_SHARDED = ("conv_norm", "conv_kernel", "ffn_conv")
SMALL_AXIS = {"conv_norm": 1, "conv_kernel": 2, "ffn_conv": 2}
SMALL_REPLICATED = ("attn_norm", "attn_f_bias", "fox_q_gain", "fox_k_gain", "sb_q_gain", "sb_k_gain", "ffn_norm")
WEIGHT_ORDER = ("attn_norm", "attn_w_in", "attn_f_bias", "fox_q_gain", "fox_k_gain", "sb_q_gain", "sb_k_gain",
                "attn_w_out", "conv_norm", "conv_w_in", "conv_kernel", "conv_w_out", "ffn_norm", "ffn_w_up",
                "ffn_conv", "ffn_w_down")


def _big_total_rows(shapes):
    used = sum(_part_rows(s, BIG_WIDTH, BIG_ROW_MULT) for s in shapes)
    tile = BIG_TILE if used >= 8 * BIG_TILE else SMALL_TILE
    return -(-used // tile) * tile


def _place():
    x, y, c = lax.axis_index("x"), lax.axis_index("y"), lax.axis_index("c")
    other_chips = [(1 - x, y), (x, 1 - y), (1 - x, 1 - y)]
    return x, y, c, other_chips


_ANY = pl.BlockSpec(memory_space=pl.ANY)


def _chip_sems():
    return [pltpu.SemaphoreType.DMA((3,)), pltpu.SemaphoreType.DMA((3,)), pltpu.SemaphoreType.DMA]


def _chip_gather(src_ref, dst_ref, send_sems, recv_sems, local_sem):
    x, y, c, chips = _place()
    k = 2 * x + y

    def copy(j, slot):
        px, py = chips[j]
        return pltpu.make_async_remote_copy(src_ref=src_ref, dst_ref=dst_ref.at[slot], send_sem=send_sems.at[j],
                                            recv_sem=recv_sems.at[j], device_id=(px, py, c), device_id_type=MESH)

    def local():
        return pltpu.make_async_copy(src_ref, dst_ref.at[k], local_sem)

    def start():
        local().start()
        for j in range(3):
            copy(j, k).start()

    def finish():
        for j, (px, py) in enumerate(chips):
            copy(j, 2 * px + py).wait_recv()
        for j in range(3):
            copy(j, k).wait_send()
        local().wait()

    return start, finish


def _chip_scatter(g_ref, o_ref, send_sems, recv_sems, local_sem):
    x, y, c, chips = _place()
    k = 2 * x + y

    def copy(j, src_slot, dst_slot):
        px, py = chips[j]
        return pltpu.make_async_remote_copy(src_ref=g_ref.at[src_slot], dst_ref=o_ref.at[dst_slot],
                                            send_sem=send_sems.at[j], recv_sem=recv_sems.at[j],
                                            device_id=(px, py, c), device_id_type=MESH)

    def local():
        return pltpu.make_async_copy(g_ref.at[k], o_ref.at[k], local_sem)

    def start():
        local().start()
        for j, (px, py) in enumerate(chips):
            copy(j, 2 * px + py, k).start()

    def finish():
        for j, (px, py) in enumerate(chips):
            copy(j, k, 2 * px + py).wait_recv()
        for j, (px, py) in enumerate(chips):
            copy(j, 2 * px + py, k).wait_send()
        local().wait()

    return start, finish


def gather_chips(arrs, *, name):
    n = len(arrs)

    def body(*refs):
        hooks = [_chip_gather(refs[m], refs[n + m], *refs[2 * n + 3 * m:2 * n + 3 * m + 3]) for m in range(n)]
        for start, _ in hooks:
            start()
        for _, finish in hooks:
            finish()

    return pl.pallas_call(
        body, in_specs=[_ANY] * n, out_specs=[_ANY] * n,
        out_shape=[jax.ShapeDtypeStruct((N_CHIPS,) + a.shape, a.dtype) for a in arrs],
        scratch_shapes=_chip_sems() * n, name=name)(*arrs)


def scatter_chips(chunks, *, name):
    def body(g_ref, o_ref, send_sems, recv_sems, local_sem):
        start, finish = _chip_scatter(g_ref, o_ref, send_sems, recv_sems, local_sem)
        start()
        finish()

    return pl.pallas_call(
        body, in_specs=[_ANY], out_specs=_ANY, out_shape=jax.ShapeDtypeStruct(chunks.shape, chunks.dtype),
        scratch_shapes=_chip_sems(), name=name)(chunks)


def swap_cores(arrs, *, name):
    n = len(arrs)

    def body(*refs):
        x, y, c, _ = _place()
        copies = [pltpu.make_async_remote_copy(src_ref=refs[m], dst_ref=refs[n + m], send_sem=refs[2 * n + 2 * m],
                                               recv_sem=refs[2 * n + 2 * m + 1], device_id=(x, y, 1 - c),
                                               device_id_type=MESH) for m in range(n)]
        for cp in copies:
            cp.start()
        for cp in copies:
            cp.wait()

    return pl.pallas_call(
        body, in_specs=[_ANY] * n, out_specs=[_ANY] * n,
        out_shape=[jax.ShapeDtypeStruct(a.shape, a.dtype) for a in arrs],
        scratch_shapes=[pltpu.SemaphoreType.DMA, pltpu.SemaphoreType.DMA] * n, name=name)(*arrs)


def allreduce_small(p, *, name):
    r, w = p.shape

    def body(p_ref, o_ref, buf, send_sems, recv_sems):
        x, y, c, _ = _place()
        me = 4 * x + 2 * y + c
        buf[me] = p_ref[...]

        def peer_of(m):
            return (1 - x if m & 4 else x, 1 - y if m & 2 else y, 1 - c if m & 1 else c)

        def copy(m, slot):
            return pltpu.make_async_remote_copy(src_ref=p_ref, dst_ref=buf.at[slot], send_sem=send_sems.at[m - 1],
                                                recv_sem=recv_sems.at[m - 1], device_id=peer_of(m),
                                                device_id_type=MESH)

        sends = [copy(m, me) for m in range(1, 8)]
        for cp in sends:
            cp.start()
        for m in range(1, 8):
            px, py, pc = peer_of(m)
            copy(m, 4 * px + 2 * py + pc).wait_recv()
        for cp in sends:
            cp.wait_send()
        acc = buf[0]
        for d in range(1, 8):
            acc = acc + buf[d]
        o_ref[...] = acc

    vm = pl.BlockSpec(memory_space=pltpu.VMEM)
    return pl.pallas_call(
        body, in_specs=[vm], out_specs=vm, out_shape=jax.ShapeDtypeStruct((r, w), F32),
        scratch_shapes=[pltpu.VMEM((8, r, w), F32), pltpu.SemaphoreType.DMA((7,)), pltpu.SemaphoreType.DMA((7,))],
        name=name)(p)


def sum_chips(rv, *, name):
    _, r, w = rv.shape
    tile = BIG_TILE if r % BIG_TILE == 0 else SMALL_TILE

    def body(a_ref, b_ref, c_ref, d_ref, o_ref):
        o_ref[...] = ((a_ref[0].astype(F32) + b_ref[0].astype(F32)) + c_ref[0].astype(F32)) + d_ref[0].astype(F32)

    spec = lambda kk: pl.BlockSpec((1, tile, w), lambda i: (kk, i, 0))
    return pl.pallas_call(
        body, grid=(r // tile,), in_specs=[spec(0), spec(1), spec(2), spec(3)],
        out_specs=pl.BlockSpec((tile, w), lambda i: (i, 0)), out_shape=jax.ShapeDtypeStruct((r, w), F32),
        compiler_params=_params(), name=name)(rv, rv, rv, rv)


def add_pair(a, b, *, name):
    r, w = a.shape
    tile = BIG_TILE if r % BIG_TILE == 0 else SMALL_TILE

    def body(a_ref, b_ref, o_ref):
        o_ref[...] = a_ref[...] + b_ref[...]

    spec = pl.BlockSpec((tile, w), lambda i: (i, 0))
    return pl.pallas_call(body, grid=(r // tile,), in_specs=[spec, spec], out_specs=spec,
                          out_shape=jax.ShapeDtypeStruct((r, w), F32), compiler_params=_params(), name=name)(a, b)


def adamw(w, g, m, v, *, tm, name):
    r, c = w.shape
    assert r % tm == 0

    def body(w_ref, g_ref, m_ref, v_ref, d_ref, nm_ref, nv_ref):
        g_ = g_ref[...]
        m_ = ADAM_B1 * m_ref[...] + (1.0 - ADAM_B1) * g_
        v_ = ADAM_B2 * v_ref[...] + (1.0 - ADAM_B2) * (g_ * g_)
        m_hat = m_ / (1.0 - ADAM_B1 ** ADAM_STEP)
        v_hat = v_ / (1.0 - ADAM_B2 ** ADAM_STEP)
        d_ref[...] = -ADAM_LR * (m_hat / (jnp.sqrt(v_hat) + ADAM_EPS) + ADAM_WD * w_ref[...])
        nm_ref[...] = m_
        nv_ref[...] = v_

    spec = pl.BlockSpec((tm, c), lambda i: (i, 0))
    return pl.pallas_call(body, grid=(r // tm,), in_specs=[spec] * 4, out_specs=[spec] * 3,
                          out_shape=[jax.ShapeDtypeStruct((r, c), F32)] * 3, compiler_params=_params(), name=name)(w, g, m, v)


def kernel(x, attn_norm, attn_w_in, attn_f_bias, fox_q_gain, fox_k_gain, sb_q_gain, sb_k_gain, attn_w_out, conv_norm, conv_w_in, conv_kernel, conv_w_out, ffn_norm, ffn_w_up, ffn_conv, ffn_w_down, loss_target, m_attn_norm, m_attn_w_in, m_attn_f_bias, m_fox_q_gain, m_fox_k_gain, m_sb_q_gain, m_sb_k_gain, m_attn_w_out, m_conv_norm, m_conv_w_in, m_conv_kernel, m_conv_w_out, m_ffn_norm, m_ffn_w_up, m_ffn_conv, m_ffn_w_down, v_attn_norm, v_attn_w_in, v_attn_f_bias, v_fox_q_gain, v_fox_k_gain, v_sb_q_gain, v_sb_k_gain, v_attn_w_out, v_conv_norm, v_conv_w_in, v_conv_kernel, v_conv_w_out, v_ffn_norm, v_ffn_w_up, v_ffn_conv, v_ffn_w_down):
    a = dict(locals())
    chip = 2 * lax.axis_index("x") + lax.axis_index("y")
    n_attn, n_conv, depth = attn_norm.shape[0], conv_norm.shape[0], ffn_norm.shape[0]

    units = [(name, l) for name in BIG_NAMES for l in range(a[name].shape[0])]
    early = [("attn_w_in", 0)]
    late = [u for u in units if u not in early]
    late_b = [("attn_w_out", 0), ("conv_w_in", n_conv - 1), ("conv_w_out", n_conv - 1), ("ffn_w_up", depth - 1),
              ("ffn_w_down", depth - 1)]
    late_a = [u for u in late if u not in late_b]

    def unit_shape(u):
        return a[u[0]].shape[1:]

    def pack_units(us, get):
        return _pack_rows([get(u) for u in us], BIG_WIDTH, BIG_ROW_MULT, BF16, _big_total_rows([unit_shape(u) for u in us]))

    def unpack_units(packed, us):
        return dict(zip(us, _unpack_rows(packed, [unit_shape(u) for u in us], BIG_WIDTH, BIG_ROW_MULT)))

    def full_units(gathered, us):
        per_chip = [unpack_units(gathered[kk], us) for kk in range(N_CHIPS)]
        return {u: jnp.concatenate([per_chip[kk][u] for kk in range(N_CHIPS)], axis=BIG_AXIS[u[0]] - 1) for u in us}

    def shard(u):
        return a[u[0]][u[1]]

    small_shapes = [a[n].shape for n in SMALL_SHARDED]
    packed_s = _pack_rows([a[n] for n in SMALL_SHARDED], 128, 8, F32)
    gath_e, gath_s = gather_chips([pack_units(early, shard), packed_s], name="gather_weights")
    full_e = full_units(gath_e, early)
    full = {}
    per_chip = [_unpack_rows(gath_s[kk], small_shapes, 128, 8) for kk in range(N_CHIPS)]
    for n, name in enumerate(SMALL_SHARDED):
        full[name] = jnp.concatenate([per_chip[kk][n] for kk in range(N_CHIPS)], axis=SMALL_AXIS[name])

    def attn_weights(i, fu):
        return dict(
            norm=attn_norm[i][None],
            w_in=jnp.pad(fu[("attn_w_in", i)], ((0, 0), (0, ATTN_IN_PAD - ATTN_IN))),
            fbias=jnp.pad(attn_f_bias[i], (0, 128 - H_FOX))[None],
            gq=jnp.concatenate([jnp.tile(fox_q_gain[i], H_FOX), jnp.tile(sb_q_gain[i], H_SB)])[None],
            gk=jnp.concatenate([jnp.tile(fox_k_gain[i], H_FOX), jnp.tile(sb_k_gain[i], H_SB)])[None],
            w_out=fu.get(("attn_w_out", i)))

    def build_weights(gathered):
        fu = {**full_e, **full_units(gathered[0], late_a), **full_units(gathered[1], late_b)}
        wa = [attn_weights(i, fu) for i in range(n_attn)]
        wc = [dict(norm=full["conv_norm"][i][None], w_in=fu[("conv_w_in", i)], ck=full["conv_kernel"][i][None],
                   w_out=fu[("conv_w_out", i)]) for i in range(n_conv)]
        wf = []
        for l in range(depth):
            cw = full["ffn_conv"][l]
            wf.append(dict(norm=ffn_norm[l][None], w_up=fu[("ffn_w_up", l)], cw2=jnp.stack([cw[:, :D_FF], cw[:, D_FF:]]),
                           w_down=fu[("ffn_w_down", l)]))
        return wa, wc, wf

    def chunk_of(u, kk, ga, gc, gf):
        name, l = u
        g = {"attn_w_in": lambda: ga[l]["w_in"], "attn_w_out": lambda: ga[l]["w_out"],
             "conv_w_in": lambda: gc[l]["w_in"], "conv_w_out": lambda: gc[l]["w_out"],
             "ffn_w_up": lambda: gf[l]["w_up"], "ffn_w_down": lambda: gf[l]["w_down"]}[name]()
        width = a[name].shape[BIG_AXIS[name]]
        return lax.slice_in_dim(g, kk * width, (kk + 1) * width, axis=BIG_AXIS[name] - 1)

    def chunks_of(us, ga, gc, gf):
        return jnp.stack([pack_units(us, lambda u: chunk_of(u, kk, ga, gc, gf)) for kk in range(N_CHIPS)])

    loss_blk, grad_x, ga, gc, gf, landed_late = forward_backward(
        x[0], loss_target[0], [attn_weights(0, full_e)], [], [],
        late_weights=(pack_units(late_a, shard), pack_units(late_b, shard), build_weights),
        late_chunks=lambda ga, gc, gf: chunks_of(late, ga, gc, gf))

    landed_early = scatter_chips(chunks_of(early, ga, gc, gf), name="scatter_grads")
    mine = [sum_chips(landed_early, name="sum_chips"), sum_chips(landed_late, name="sum_chips")]
    theirs = swap_cores(mine, name="swap_cores")
    g_units = {**unpack_units(add_pair(mine[0], theirs[0], name="add_cores"), early),
               **unpack_units(add_pair(mine[1], theirs[1], name="add_cores"), late)}
    grads = {name: jnp.stack([g_units[(name, l)] for l in range(a[name].shape[0])]) for name in BIG_NAMES}

    small_full = [
        loss_blk,
        jnp.stack([g["norm"] for g in ga]), jnp.stack([g["f_bias"] for g in ga]),
        jnp.stack([g["fox_q"] for g in ga]), jnp.stack([g["fox_k"] for g in ga]),
        jnp.stack([g["sb_q"] for g in ga]), jnp.stack([g["sb_k"] for g in ga]),
        jnp.stack([g["norm"] for g in gf]),
        jnp.stack([g["norm"] for g in gc]), jnp.stack([g["ck"] for g in gc]), jnp.stack([g["cw"] for g in gf]),
    ]
    summed = allreduce_small(_pack_rows(small_full, 128, 8, F32), name="allreduce_small")
    parts = _unpack_rows(summed, [p.shape for p in small_full], 128, 8)
    loss = parts[0][0, 0]
    for name, g in zip(SMALL_REPLICATED, parts[1:8]):
        grads[name] = g
    for name, g in zip(SMALL_SHARDED, parts[8:]):
        width = a[name].shape[SMALL_AXIS[name]]
        grads[name] = lax.dynamic_slice_in_dim(g, chip * width, width, axis=SMALL_AXIS[name])

    delta, new_m, new_v = {}, {}, {}
    for name in BIG_NAMES:
        shape = a[name].shape
        flat = lambda arr: arr.reshape(-1, shape[-1])
        d_, m_, v_ = adamw(flat(a[name]), flat(grads[name]), flat(a["m_" + name]), flat(a["v_" + name]), tm=256,
                           name="adamw")
        delta[name], new_m[name], new_v[name] = d_.reshape(shape), m_.reshape(shape), v_.reshape(shape)
    small_names = SMALL_REPLICATED + SMALL_SHARDED
    small_shapes_local = [a[n].shape for n in small_names]
    pack = lambda prefix, src: _pack_rows([src[prefix + n] for n in small_names], 128, 8, F32)
    packed = adamw(pack("", a), pack("", grads), pack("m_", a), pack("v_", a), tm=8, name="adamw_small")
    for store, buf in zip((delta, new_m, new_v), packed):
        for name, arr in zip(small_names, _unpack_rows(buf, small_shapes_local, 128, 8)):
            store[name] = arr

    return (loss, grad_x[None], *[grads[n] for n in WEIGHT_ORDER], *[delta[n] for n in WEIGHT_ORDER],
            *[new_m[n] for n in WEIGHT_ORDER], *[new_v[n] for n in WEIGHT_ORDER])
```

```python
import functools

import jax
import jax.numpy as jnp
from jax import lax
from jax.experimental import pallas as pl
from jax.experimental.pallas import tpu as pltpu

F32 = jnp.float32
BF16 = jnp.bfloat16

D_MODEL = 1024
HEAD_DIM = 64
H_FOX = 8
H_SB = 8
N_HEADS = H_FOX + H_SB
MIX = N_HEADS * HEAD_DIM
ATTN_IN = 3 * MIX + H_FOX
ATTN_IN_PAD = 3 * MIX + 128
D_FF = 2816
EPS = 1e-6
SCALE = HEAD_DIM ** -0.5
NEG = -1e30

ADAM_LR = 0.001
ADAM_B1 = 0.9
ADAM_B2 = 0.999
ADAM_EPS = 1e-08
ADAM_WD = 0.01
ADAM_STEP = 10

VMEM_LIMIT = 56 * 1024 * 1024
HALO = 8
BQ = 512
N_CHIPS = 4
MESH = pl.DeviceIdType.MESH


def _params(**kw):
    return pltpu.CompilerParams(vmem_limit_bytes=VMEM_LIMIT, **kw)


def _dot(a, b):
    return jnp.dot(a, b, preferred_element_type=F32)


def _dot_nt(a, b):
    return lax.dot_general(a, b, (((1,), (1,)), ((), ())), preferred_element_type=F32)


def _dot_tn(a, b):
    return lax.dot_general(a, b, (((0,), (0,)), ((), ())), preferred_element_type=F32)


def _split2(x):
    hi = x.astype(BF16)
    lo = (x - hi.astype(F32)).astype(BF16)
    return hi, lo


def _split3(x):
    hi = x.astype(BF16)
    r = x - hi.astype(F32)
    mid = r.astype(BF16)
    lo = (r - mid.astype(F32)).astype(BF16)
    return hi, mid, lo


def mm_nn(a, b, *, add=None, out_dtype=F32, parts=1, tm=1024, tn=512, name):
    m, k = a.shape
    n = b.shape[1]
    np_ = n // parts
    nb = np_ // tn
    tm = min(tm, m)
    assert m % tm == 0 and np_ % tn == 0

    def body(*refs):
        if add is None:
            a_ref, b_ref, o_ref = refs
            acc = _dot(a_ref[...].astype(BF16), b_ref[...])
        else:
            a_ref, b_ref, r_ref, o_ref = refs
            acc = _dot(a_ref[...].astype(BF16), b_ref[...]) + r_ref[...]
        o_ref[...] = acc.astype(out_dtype).reshape(o_ref.shape)

    in_specs = [pl.BlockSpec((tm, k), lambda i, j: (i, 0)), pl.BlockSpec((k, tn), lambda i, j: (0, j))]
    args = [a, b]
    if add is not None:
        in_specs.append(pl.BlockSpec((tm, tn), lambda i, j: (i, j)))
        args.append(add)
    if parts == 1:
        out_spec = pl.BlockSpec((tm, tn), lambda i, j: (i, j))
        out_shape = jax.ShapeDtypeStruct((m, n), out_dtype)
    else:
        out_spec = pl.BlockSpec((1, tm, tn), lambda i, j: (j // nb, i, j % nb))
        out_shape = jax.ShapeDtypeStruct((parts, m, np_), out_dtype)
    return pl.pallas_call(body, grid=(m // tm, n // tn), in_specs=in_specs, out_specs=out_spec,
                          out_shape=out_shape, compiler_params=_params(), name=name)(*args)


def mm_nt(a3, b, *, out_dtype=F32, tm=1024, tn=512, name):
    p, m, kp = a3.shape
    n = b.shape[0]
    tm = min(tm, m)
    assert m % tm == 0 and n % tn == 0 and b.shape[1] == p * kp

    def body(a_ref, b_ref, o_ref, acc_ref):
        part = pl.program_id(2)
        prod = _dot_nt(a_ref[0].astype(BF16), b_ref[...])

        @pl.when(part == 0)
        def _():
            acc_ref[...] = prod

        @pl.when(part > 0)
        def _():
            acc_ref[...] += prod

        @pl.when(part == p - 1)
        def _():
            o_ref[...] = acc_ref[...].astype(out_dtype)

    return pl.pallas_call(
        body, grid=(m // tm, n // tn, p),
        in_specs=[pl.BlockSpec((1, tm, kp), lambda i, j, q: (q, i, 0)), pl.BlockSpec((tn, kp), lambda i, j, q: (j, q))],
        out_specs=pl.BlockSpec((tm, tn), lambda i, j, q: (i, j)),
        out_shape=jax.ShapeDtypeStruct((m, n), out_dtype),
        scratch_shapes=[pltpu.VMEM((tm, tn), F32)],
        compiler_params=_params(), name=name)(a3, b)


def mm_tn(a, b3, *, tk=512, tn=512, tt=2048, name):
    t, k = a.shape
    p, _, np_ = b3.shape
    nb = np_ // tn
    tt = min(tt, t)
    assert t % tt == 0 and k % tk == 0 and np_ % tn == 0

    def body(a_ref, b_ref, o_ref):
        prod = _dot_tn(a_ref[...].astype(BF16), b_ref[0].astype(BF16))

        @pl.when(pl.program_id(2) == 0)
        def _():
            o_ref[...] = prod

        @pl.when(pl.program_id(2) > 0)
        def _():
            o_ref[...] += prod

    return pl.pallas_call(
        body, grid=(k // tk, p * nb, t // tt),
        in_specs=[pl.BlockSpec((tt, tk), lambda i, j, s: (s, i)), pl.BlockSpec((1, tt, tn), lambda i, j, s: (j // nb, s, j % nb))],
        out_specs=pl.BlockSpec((tk, tn), lambda i, j, s: (i, j)),
        out_shape=jax.ShapeDtypeStruct((k, p * np_), F32),
        compiler_params=_params(), name=name)(a, b3)


def rms_mm_nn(h, g, b, *, parts=1, tm=1024, tn=512, name):
    t, d = h.shape
    n = b.shape[1]
    np_ = n // parts
    nb = np_ // tn
    tm = min(tm, t)
    assert t % tm == 0 and np_ % tn == 0

    def body(h_ref, g_ref, b_ref, xn_ref, o_ref):
        @pl.when(pl.program_id(1) == 0)
        def _():
            x = h_ref[...]
            r = lax.rsqrt(jnp.mean(x * x, axis=-1, keepdims=True) + EPS)
            xn_ref[...] = (x * r * g_ref[...]).astype(BF16)

        o_ref[...] = _dot(xn_ref[...], b_ref[...]).reshape(o_ref.shape)

    if parts == 1:
        out_spec = pl.BlockSpec((tm, tn), lambda i, j: (i, j))
        out_shape = jax.ShapeDtypeStruct((t, n), F32)
    else:
        out_spec = pl.BlockSpec((1, tm, tn), lambda i, j: (j // nb, i, j % nb))
        out_shape = jax.ShapeDtypeStruct((parts, t, np_), F32)
    row = pl.BlockSpec((tm, d), lambda i, j: (i, 0))
    return pl.pallas_call(
        body, grid=(t // tm, n // tn),
        in_specs=[row, pl.BlockSpec((1, d), lambda i, j: (0, 0)), pl.BlockSpec((d, tn), lambda i, j: (0, j))],
        out_specs=[row, out_spec], out_shape=[jax.ShapeDtypeStruct((t, d), BF16), out_shape],
        compiler_params=_params(), name=name)(h, g, b)


def mm_nt_rms_bwd(a3, b, h, g, dres, *, name, tm=512):
    p, t, kp = a3.shape
    d = b.shape[0]
    tm = min(tm, t)
    assert t % tm == 0 and b.shape[1] == p * kp

    def body(a_ref, b_ref, h_ref, g_ref, dres_ref, dh_ref, dg_ref, acc_ref):
        i = pl.program_id(0)
        part = pl.program_id(1)
        prod = _dot_nt(a_ref[0].astype(BF16), b_ref[...])

        @pl.when(part == 0)
        def _():
            acc_ref[...] = prod

        @pl.when(part > 0)
        def _():
            acc_ref[...] += prod

        @pl.when(part == p - 1)
        def _():
            x = h_ref[...]
            dy = acc_ref[...]
            r = lax.rsqrt(jnp.mean(x * x, axis=-1, keepdims=True) + EPS)
            gy = dy * g_ref[...]
            dot = jnp.mean(gy * x, axis=-1, keepdims=True)
            dh_ref[...] = dres_ref[...] + r * gy - x * (r * r * r * dot)
            _acc_rows(dg_ref, jnp.sum(dy * x * r, axis=0, keepdims=True), i == 0)

    row = pl.BlockSpec((tm, d), lambda i, q: (i, 0))
    vec = pl.BlockSpec((1, d), lambda i, q: (0, 0))
    return pl.pallas_call(
        body, grid=(t // tm, p),
        in_specs=[pl.BlockSpec((1, tm, kp), lambda i, q: (q, i, 0)), pl.BlockSpec((d, kp), lambda i, q: (0, q)),
                  row, vec, row],
        out_specs=[row, vec],
        out_shape=[jax.ShapeDtypeStruct((t, d), F32), jax.ShapeDtypeStruct((1, d), F32)],
        scratch_shapes=[pltpu.VMEM((tm, d), F32)], compiler_params=_params(), name=name)(a3, b, h, g, dres)


def _causal3(x, w):
    return w[0:1] * pltpu.roll(x, 2, 0) + w[1:2] * pltpu.roll(x, 1, 0) + w[2:3] * x


def _causal3_taps(x_ext, w, tm):
    x2 = pltpu.roll(x_ext, 2, 0)
    x1 = pltpu.roll(x_ext, 1, 0)
    y = w[0:1] * x2 + w[1:2] * x1 + w[2:3] * x_ext
    return y, (x2[HALO:HALO + tm], x1[HALO:HALO + tm], x_ext[HALO:HALO + tm])


def _anticausal3(z, w):
    n = z.shape[0]
    return w[2:3] * z + w[1:2] * pltpu.roll(z, n - 1, 0) + w[0:1] * pltpu.roll(z, n - 2, 0)


def _prev_spec(part, tm, tc, nrow8):
    del nrow8
    return pl.BlockSpec((1, HALO, tc), lambda j, i: (part, jnp.maximum(i * (tm // HALO) - 1, 0), j))


def _next_spec(part, tm, tc, nrow8):
    return pl.BlockSpec((1, HALO, tc), lambda j, i: (part, jnp.minimum((i + 1) * (tm // HALO), nrow8 - 1), j))


def _tile_spec(part, tm, tc):
    return pl.BlockSpec((1, tm, tc), lambda j, i: (part, i, j))


def _acc_rows(ref, val, first):
    @pl.when(first)
    def _():
        ref[...] = val

    @pl.when(jnp.logical_not(first))
    def _():
        ref[...] += val


def ffn_act_down_fwd(up2, cw2, w_down, h, *, name, tm=256, tc=1408):
    _, t, f = up2.shape
    d = h.shape[1]
    tm = min(tm, t)

    def body(g_ref, v_ref, gp_ref, vp_ref, w_ref, wd_ref, h_ref, o_ref, act_ref):
        keep = jnp.where(pl.program_id(0) == 0, 0.0, 1.0)
        for cc in range(f // tc):
            cols = slice(cc * tc, (cc + 1) * tc)
            g_ext = jnp.concatenate([gp_ref[0, :, cols] * keep, g_ref[0, :, cols]], axis=0)
            v_ext = jnp.concatenate([vp_ref[0, :, cols] * keep, v_ref[0, :, cols]], axis=0)
            ug = _causal3(g_ext, w_ref[0, :, cols])[HALO:]
            uv = _causal3(v_ext, w_ref[1, :, cols])[HALO:]
            act_ref[:, cols] = (ug * jax.nn.sigmoid(ug) * uv).astype(BF16)
        o_ref[...] = _dot(act_ref[...], wd_ref[...]) + h_ref[...]

    tile = lambda part: pl.BlockSpec((1, tm, f), lambda i: (part, i, 0))
    prev = lambda part: pl.BlockSpec((1, HALO, f), lambda i: (part, jnp.maximum(i * (tm // HALO) - 1, 0), 0))
    row = pl.BlockSpec((tm, d), lambda i: (i, 0))
    return pl.pallas_call(
        body, grid=(t // tm,),
        in_specs=[tile(0), tile(1), prev(0), prev(1), pl.BlockSpec((2, 3, f), lambda i: (0, 0, 0)),
                  pl.BlockSpec((f, d), lambda i: (0, 0)), row],
        out_specs=[row, pl.BlockSpec((tm, f), lambda i: (i, 0))],
        out_shape=[jax.ShapeDtypeStruct((t, d), F32), jax.ShapeDtypeStruct((t, f), BF16)],
        compiler_params=_params(), name=name)(up2, up2, up2, up2, cw2, w_down, h)


def ffn_act_bwd(dh, w_down, up2, cw2, *, name, tm=256, tc=1408):
    _, t, f = up2.shape
    d = dh.shape[1]
    n8 = t // HALO

    def body(d_ref, dn_ref, wd_ref, g_ref, v_ref, gp_ref, vp_ref, gn_ref, vn_ref, wg_ref, wv_ref, dup_ref, dw_ref):
        i = pl.program_id(1)
        first = i == 0
        keep_p = jnp.where(first, 0.0, 1.0)
        keep_n = jnp.where(i == pl.num_programs(1) - 1, 0.0, 1.0)
        wg = wg_ref[0]
        wv = wv_ref[0]
        g_ext = jnp.concatenate([gp_ref[0] * keep_p, g_ref[0], gn_ref[0]], axis=0)
        v_ext = jnp.concatenate([vp_ref[0] * keep_p, v_ref[0], vn_ref[0]], axis=0)
        dh_ext = jnp.concatenate([d_ref[...], dn_ref[...] * keep_n], axis=0)
        d_ext = _dot_nt(dh_ext.astype(BF16), wd_ref[...])
        ug, (g2, g1, g0) = _causal3_taps(g_ext, wg, tm)
        uv, (v2, v1, v0) = _causal3_taps(v_ext, wv, tm)
        ug = ug[HALO:]
        uv = uv[HALO:]
        s = jax.nn.sigmoid(ug)
        dg = d_ext * uv * (s * (1.0 + ug * (1.0 - s)))
        dv = d_ext * (ug * s)
        dup_ref[0] = _anticausal3(dg, wg)[:tm].astype(BF16)
        dup_ref[1] = _anticausal3(dv, wv)[:tm].astype(BF16)
        dgt = dg[:tm]
        dvt = dv[:tm]
        zero = jnp.zeros((HALO - 3, tc), F32)
        rows_g = [jnp.sum(dgt * x, axis=0, keepdims=True) for x in (g2, g1, g0)] + [zero]
        rows_v = [jnp.sum(dvt * x, axis=0, keepdims=True) for x in (v2, v1, v0)] + [zero]
        _acc_rows(dw_ref, jnp.stack([jnp.concatenate(rows_g, axis=0), jnp.concatenate(rows_v, axis=0)]), first)

    wspec = lambda part: pl.BlockSpec((1, 3, tc), lambda j, i: (part, 0, j))
    return pl.pallas_call(
        body, grid=(f // tc, t // tm),
        in_specs=[pl.BlockSpec((tm, d), lambda j, i: (i, 0)),
                  pl.BlockSpec((HALO, d), lambda j, i: (jnp.minimum((i + 1) * (tm // HALO), n8 - 1), 0)),
                  pl.BlockSpec((tc, d), lambda j, i: (j, 0)),
                  _tile_spec(0, tm, tc), _tile_spec(1, tm, tc), _prev_spec(0, tm, tc, n8), _prev_spec(1, tm, tc, n8),
                  _next_spec(0, tm, tc, n8), _next_spec(1, tm, tc, n8), wspec(0), wspec(1)],
        out_specs=[pl.BlockSpec((2, tm, tc), lambda j, i: (0, i, j)), pl.BlockSpec((2, HALO, tc), lambda j, i: (0, 0, j))],
        out_shape=[jax.ShapeDtypeStruct((2, t, f), BF16), jax.ShapeDtypeStruct((2, HALO, f), F32)],
        compiler_params=_params(), name=name)(dh, dh, w_down, up2, up2, up2, up2, up2, up2, cw2, cw2)


def conv_mix_fwd(proj3, ck, *, name, tm=512, tc=512):
    _, t, c = proj3.shape
    n8 = t // HALO

    def body(b_ref, c_ref, u_ref, cp_ref, up_ref, w_ref, o_ref):
        keep = jnp.where(pl.program_id(1) == 0, 0.0, 1.0)
        cu_ext = jnp.concatenate([cp_ref[0] * up_ref[0] * keep, c_ref[0] * u_ref[0]], axis=0)
        o_ref[...] = (b_ref[0] * _causal3(cu_ext, w_ref[0])[HALO:]).astype(BF16)

    return pl.pallas_call(
        body, grid=(c // tc, t // tm),
        in_specs=[_tile_spec(0, tm, tc), _tile_spec(1, tm, tc), _tile_spec(2, tm, tc), _prev_spec(1, tm, tc, n8),
                  _prev_spec(2, tm, tc, n8), pl.BlockSpec((1, 3, tc), lambda j, i: (0, 0, j))],
        out_specs=pl.BlockSpec((tm, tc), lambda j, i: (i, j)),
        out_shape=jax.ShapeDtypeStruct((t, c), BF16), compiler_params=_params(), name=name)(proj3, proj3, proj3, proj3, proj3, ck)


def conv_mix_bwd(dh, w_out, proj3, ck, *, name, tm=512, tc=512):
    _, t, c = proj3.shape
    d = dh.shape[1]
    n8 = t // HALO

    def body(d_ref, dn_ref, wo_ref, b_ref, c_ref, u_ref, cp_ref, up_ref, bn_ref, w_ref, dp_ref, dw_ref):
        i = pl.program_id(1)
        first = i == 0
        keep_p = jnp.where(first, 0.0, 1.0)
        keep_n = jnp.where(i == pl.num_programs(1) - 1, 0.0, 1.0)
        w = w_ref[0]
        cu_ext = jnp.concatenate([cp_ref[0] * up_ref[0] * keep_p, c_ref[0] * u_ref[0]], axis=0)
        cv, (x2, x1, x0) = _causal3_taps(cu_ext, w, tm)
        cv = cv[HALO:]
        dh_ext = jnp.concatenate([d_ref[...], dn_ref[...] * keep_n], axis=0)
        d_ext = _dot_nt(dh_ext.astype(BF16), wo_ref[...])
        dyt = d_ext[:tm]
        b_ext = jnp.concatenate([b_ref[0], bn_ref[0]], axis=0)
        dcv = d_ext * b_ext
        dcu = _anticausal3(dcv, w)[:tm]
        dp_ref[0] = (dyt * cv).astype(BF16)
        dp_ref[1] = (dcu * u_ref[0]).astype(BF16)
        dp_ref[2] = (dcu * c_ref[0]).astype(BF16)
        dcvt = dcv[:tm]
        rows = [jnp.sum(dcvt * x, axis=0, keepdims=True) for x in (x2, x1, x0)] + [jnp.zeros((HALO - 3, tc), F32)]
        _acc_rows(dw_ref, jnp.concatenate(rows, axis=0)[None], first)

    return pl.pallas_call(
        body, grid=(c // tc, t // tm),
        in_specs=[pl.BlockSpec((tm, d), lambda j, i: (i, 0)),
                  pl.BlockSpec((HALO, d), lambda j, i: (jnp.minimum((i + 1) * (tm // HALO), n8 - 1), 0)),
                  pl.BlockSpec((tc, d), lambda j, i: (j, 0)),
                  _tile_spec(0, tm, tc), _tile_spec(1, tm, tc), _tile_spec(2, tm, tc),
                  _prev_spec(1, tm, tc, n8), _prev_spec(2, tm, tc, n8),
                  _next_spec(0, tm, tc, n8), pl.BlockSpec((1, 3, tc), lambda j, i: (0, 0, j))],
        out_specs=[pl.BlockSpec((3, tm, tc), lambda j, i: (0, i, j)), pl.BlockSpec((1, HALO, tc), lambda j, i: (0, 0, j))],
        out_shape=[jax.ShapeDtypeStruct((3, t, c), BF16), jax.ShapeDtypeStruct((1, HALO, c), F32)],
        compiler_params=_params(), name=name)(dh, dh, w_out, proj3, proj3, proj3, proj3, proj3, proj3, ck)


def _head_sums(x, bd):
    hi, lo = _split2(x)
    return _dot(hi, bd) + _dot(lo, bd)


def attn_prep_fwd(proj, gq, gk, fbias, bd, *, name, tm=256):
    t = proj.shape[0]

    def body(q_ref, k_ref, v_ref, f_ref, gq_ref, gk_ref, fb_ref, bd_ref, qs_ref, kn_ref, vb_ref, lf_ref):
        bd = bd_ref[...]

        def headnorm(x_ref, g_ref, o_ref, scale):
            for c in range(MIX // 128):
                sl = slice(128 * c, 128 * (c + 1))
                x = x_ref[:, sl]
                r = lax.rsqrt(_head_sums(x * x, bd) * (1.0 / HEAD_DIM) + EPS)
                o_ref[:, sl] = (x * r * (g_ref[:, sl] * scale)).astype(BF16)

        headnorm(q_ref, gq_ref, qs_ref, SCALE)
        headnorm(k_ref, gk_ref, kn_ref, 1.0)
        vb_ref[...] = v_ref[...].astype(BF16)
        fl = f_ref[...] + fb_ref[...]
        logf = jnp.minimum(fl, 0.0) - jnp.log(1.0 + jnp.exp(-jnp.abs(fl)))
        lf_ref[...] = logf.T[0:H_FOX, :]

    col = lambda c: pl.BlockSpec((tm, MIX), lambda i: (i, c))
    vec = pl.BlockSpec((1, MIX), lambda i: (0, 0))
    out = pl.BlockSpec((tm, MIX), lambda i: (i, 0))
    return pl.pallas_call(
        body, grid=(t // tm,),
        in_specs=[col(0), col(1), col(2), pl.BlockSpec((tm, 128), lambda i: (i, 3 * MIX // 128)), vec, vec,
                  pl.BlockSpec((1, 128), lambda i: (0, 0)), pl.BlockSpec((128, 128), lambda i: (0, 0))],
        out_specs=[out, out, out, pl.BlockSpec((H_FOX, tm), lambda i: (0, i))],
        out_shape=[jax.ShapeDtypeStruct((t, MIX), BF16)] * 3 + [jax.ShapeDtypeStruct((H_FOX, t), F32)],
        compiler_params=_params(), name=name)(proj, proj, proj, proj, gq, gk, fbias, bd)


def attn_prep_bwd(proj, dqs, dkn, dv, dfl, gq, gk, bd, *, name, tm=256):
    t = proj.shape[0]

    def body(q_ref, k_ref, dq_ref, dk_ref, dv_ref, dfl_ref, gq_ref, gk_ref, bd_ref, dp_ref, dgq_ref, dgk_ref):
        bd = bd_ref[...]
        first = pl.program_id(0) == 0

        def back(x_ref, d_ref, g_ref, col0, scale, dg_ref):
            parts = []
            for c in range(MIX // 128):
                sl = slice(128 * c, 128 * (c + 1))
                x = x_ref[:, sl]
                r = lax.rsqrt(_head_sums(x * x, bd) * (1.0 / HEAD_DIM) + EPS)
                dn = d_ref[:, sl] * scale
                gy = dn * g_ref[:, sl]
                hs = _head_sums(gy * x, bd) * (1.0 / HEAD_DIM)
                dp_ref[:, col0 + 128 * c:col0 + 128 * (c + 1)] = (r * gy - x * (r * r * r * hs)).astype(BF16)
                parts.append(jnp.sum(dn * x * r, axis=0, keepdims=True))
            _acc_rows(dg_ref, jnp.concatenate(parts, axis=1), first)

        back(q_ref, dq_ref, gq_ref, 0, SCALE, dgq_ref)
        back(k_ref, dk_ref, gk_ref, MIX, 1.0, dgk_ref)
        dp_ref[:, 2 * MIX:3 * MIX] = dv_ref[...].astype(BF16)
        dp_ref[:, 3 * MIX:] = dfl_ref[...]

    col = lambda c: pl.BlockSpec((tm, MIX), lambda i: (i, c))
    row = pl.BlockSpec((tm, MIX), lambda i: (i, 0))
    vec = pl.BlockSpec((1, MIX), lambda i: (0, 0))
    return pl.pallas_call(
        body, grid=(t // tm,),
        in_specs=[col(0), col(1), row, row, row, pl.BlockSpec((tm, 128), lambda i: (i, 0)), vec, vec,
                  pl.BlockSpec((128, 128), lambda i: (0, 0))],
        out_specs=[pl.BlockSpec((tm, ATTN_IN_PAD), lambda i: (i, 0)), vec, vec],
        out_shape=[jax.ShapeDtypeStruct((t, ATTN_IN_PAD), BF16), jax.ShapeDtypeStruct((1, MIX), F32),
                   jax.ShapeDtypeStruct((1, MIX), F32)],
        compiler_params=_params(), name=name)(proj, proj, dqs, dkn, dv, dfl, gq, gk, bd)


def gate_cumsum(logf3, tri, *, name):
    nc, r, _ = logf3.shape

    def body(x_ref, tri_ref, o_ref):
        tri_m = tri_ref[...]

        def step(c, carry):
            hi, mid, lo = _split3(x_ref[c])
            cs = _dot(hi, tri_m) + _dot(mid, tri_m) + _dot(lo, tri_m) + carry
            o_ref[c] = cs
            return cs[:, 127:128]

        lax.fori_loop(0, nc, step, jnp.zeros((r, 1), F32))

    return pl.pallas_call(body, out_shape=jax.ShapeDtypeStruct(logf3.shape, F32), compiler_params=_params(),
                          name=name)(logf3, tri)


def gate_cumsum_bwd(dcum3, logf3, tri, *, name):
    nc, r, _ = dcum3.shape

    def body(x_ref, lf_ref, tri_ref, o_ref, s_ref):
        tri_m = tri_ref[...]

        def step(n, carry):
            car, tot = carry
            c = nc - 1 - n
            hi, mid, lo = _split3(x_ref[c])
            cs = _dot(hi, tri_m) + _dot(mid, tri_m) + _dot(lo, tri_m) + car
            dl = cs * (1.0 - jnp.exp(lf_ref[c]))
            o_ref[c] = dl
            return cs[:, 0:1], tot + dl

        _, tot = lax.fori_loop(0, nc, step, (jnp.zeros((r, 1), F32), jnp.zeros((r, 128), F32)))
        s_ref[...] = jnp.broadcast_to(jnp.sum(tot, axis=1, keepdims=True), tot.shape)

    return pl.pallas_call(body, out_shape=[jax.ShapeDtypeStruct(dcum3.shape, F32), jax.ShapeDtypeStruct((r, 128), F32)],
                          compiler_params=_params(), name=name)(dcum3, logf3, tri)


def _causal_iota():
    row = lax.broadcasted_iota(jnp.int32, (BQ, BQ), 0)
    col = lax.broadcasted_iota(jnp.int32, (BQ, BQ), 1)
    return row, col


def _head_specs(nj, head0):
    qin = pl.BlockSpec((1, BQ, HEAD_DIM), lambda h, i: (h + head0, i, 0))
    kin = pl.BlockSpec((1, nj, BQ, HEAD_DIM), lambda h, i: (h + head0, 0, 0, 0))
    qin2 = pl.BlockSpec((1, BQ, 2 * HEAD_DIM), lambda h, i: (h + head0, i, 0))
    kin2 = pl.BlockSpec((1, nj, BQ, 2 * HEAD_DIM), lambda h, i: (h + head0, 0, 0, 0))
    qspec = pl.BlockSpec((1, BQ, HEAD_DIM), lambda h, i: (h, i, 0))
    kspec2 = pl.BlockSpec((1, nj, BQ, 2 * HEAD_DIM), lambda h, i: (h, 0, 0, 0))
    return qin, kin, qin2, kin2, qspec, kspec2


STOP = -105.0
STOP_WIDE = -115.0
FIXED_REF_MAX = 40.0


def _store_kmax(k_ref, kmax_ref, nj):
    def step(j, mx):
        kf = k_ref[0, j].astype(F32)
        return jnp.maximum(mx, jnp.max(jnp.sum(kf * kf, axis=1, keepdims=True), axis=0, keepdims=True))

    mx = lax.fori_loop(0, nj, step, jnp.zeros((1, 1), F32))
    kmax_ref[...] = jnp.broadcast_to(jnp.sqrt(mx), kmax_ref.shape)


def _qk_bound(q, kmax_ref):
    qf = q.astype(F32)
    return jnp.sqrt(jnp.sum(qf * qf, axis=1, keepdims=True)) * kmax_ref[0:1, 0:1] * 1.001


def _first_and_last_step():
    h, i = pl.program_id(0), pl.program_id(1)
    first = jnp.logical_and(h == 0, i == 0)
    last = jnp.logical_and(h == pl.num_programs(0) - 1, i == pl.num_programs(1) - 1)
    return first, last


def fox_fwd(qs, kn4, va4, fcol, frow4, *, name, gather=None):
    _, t, dh = qs.shape
    nh = H_FOX
    nj = t // BQ

    def body(*refs):
        if gather is None:
            q_ref, k_ref, v_ref, fc_ref, fr_ref, o_ref, lse_ref, kmax_ref = refs
        else:
            q_ref, k_ref, v_ref, fc_ref, fr_ref, src_ref, o_ref, lse_ref, dst_ref, kmax_ref = refs[:10]
            first_step, last_step = _first_and_last_step()

            @pl.when(first_step)
            def _():
                _chip_gather(src_ref, dst_ref, *refs[10:])[0]()

        i = pl.program_id(1)

        @pl.when(i == 0)
        def _():
            _store_kmax(k_ref, kmax_ref, nj)

        q = q_ref[0]
        fq = fc_ref[0]
        bound = _qk_bound(q, kmax_ref)
        row, col = _causal_iota()

        def gate_at_block_end(j):
            return fr_ref[0, j][:, BQ - 1:BQ]

        def pv(p, j):
            p_hi, p_lo = _split2(p)
            return _dot(p_hi, v_ref[0, j]) + _dot(p_lo, v_ref[0, j])

        def walk(block, live, init):
            carry = block(i, init, True)

            def cond(c):
                n, carry = c
                return jnp.logical_and(n < i, live(jnp.maximum(i - 1 - n, 0), carry))

            _, carry = lax.while_loop(cond, lambda c: (c[0] + 1, block(i - 1 - c[0], c[1], False)), (0, carry))
            return carry

        def fixed_reference(_):
            shift = fq - bound

            def probs(j, offset):
                return jnp.exp(_dot_nt(q, k_ref[0, j]) + (shift + offset) - fr_ref[0, j])

            def live(c):
                n, acc = c
                gate = gate_at_block_end(jnp.maximum(i - 1 - n, 0))
                return jnp.logical_and(n < i, jnp.max(fq - gate - jnp.log(acc[:, dh:dh + 1])) >= STOP_WIDE)

            def two_blocks(c):
                n, acc = c
                ja = i - 1 - n
                jb = i - 2 - n
                absent = jnp.where(jb >= 0, 0.0, NEG)
                jb = jnp.maximum(jb, 0)
                return n + 2, acc + (pv(probs(ja, 0.0), ja) + pv(probs(jb, absent), jb))

            acc = pv(jnp.where(col <= row, probs(i, 0.0), 0.0), i)
            _, acc = lax.while_loop(live, two_blocks, (0, acc))
            l = acc[:, dh:dh + 1]
            return acc[:, :dh] / l, bound + jnp.log(l)

        def running_maximum(_):
            def block(j, carry, diag):
                m, acc = carry
                s = _dot_nt(q, k_ref[0, j]) + fq - fr_ref[0, j]
                if diag:
                    s = jnp.where(col <= row, s, NEG)
                m_new = jnp.maximum(m, jnp.max(s, axis=1, keepdims=True))
                return m_new, jnp.exp(m - m_new) * acc + pv(jnp.exp(s - m_new), j)

            def live(j, carry):
                return jnp.max(bound + fq - gate_at_block_end(j) - carry[0]) >= STOP

            m, acc = walk(block, live, (jnp.full((BQ, 1), NEG, F32), jnp.zeros((BQ, 2 * dh), F32)))
            l = acc[:, dh:dh + 1]
            return acc[:, :dh] / l, m + jnp.log(l)

        o, lse = lax.cond(jnp.max(bound) < FIXED_REF_MAX, fixed_reference, running_maximum, 0)
        o_ref[0] = o
        lse_ref[0] = lse

        if gather is not None:
            @pl.when(last_step)
            def _():
                _chip_gather(src_ref, dst_ref, *refs[10:])[1]()

    qin, kin, _, kin2, qspec, _ = _head_specs(nj, 0)
    cspec = pl.BlockSpec((1, BQ, 1), lambda h, i: (h, i, 0))
    in_specs = [qin, kin, kin2, cspec, pl.BlockSpec((1, nj, 1, BQ), lambda h, i: (h, 0, 0, 0))]
    out_specs = [qspec, cspec]
    out_shape = [jax.ShapeDtypeStruct((nh, t, dh), F32), jax.ShapeDtypeStruct((nh, t, 1), F32)]
    scratch = [pltpu.VMEM((8, 128), F32)]
    args = [qs, kn4, va4, fcol, frow4]
    if gather is not None:
        in_specs.append(_ANY)
        out_specs.append(_ANY)
        out_shape.append(jax.ShapeDtypeStruct((N_CHIPS,) + gather.shape, gather.dtype))
        scratch += _chip_sems()
        args.append(gather)
    return pl.pallas_call(body, grid=(nh, nj), in_specs=in_specs, out_specs=out_specs, out_shape=out_shape,
                          scratch_shapes=scratch, compiler_params=_params(), name=name)(*args)


def fox_bwd(qs, kn4, v4, qa, doa, fcol, frow4, o, do, lse, *, name, scatter=None):
    _, t, dh = qs.shape
    nh = H_FOX
    nj = t // BQ

    def body(*refs):
        q_ref, k_ref, v_ref, qa_ref, doa_ref, fc_ref, fr_ref, o_ref, do_ref, lse_ref = refs[:10]
        if scatter is None:
            dq_ref, dkv_ref, dfk_ref, kmax_ref = refs[10:]
        else:
            g_ref, dq_ref, dkv_ref, dfk_ref, land_ref, kmax_ref = refs[10:16]
            first_step, last_step = _first_and_last_step()

            @pl.when(first_step)
            def _():
                _chip_scatter(g_ref, land_ref, *refs[16:])[0]()

        i = pl.program_id(1)

        @pl.when(i == 0)
        def _():
            dkv_ref[...] = jnp.zeros_like(dkv_ref)
            dfk_ref[...] = jnp.zeros_like(dfk_ref)
            _store_kmax(k_ref, kmax_ref, nj)

        q = q_ref[0]
        do_b = do_ref[0]
        fq = fc_ref[0]
        lse_q = lse_ref[0]
        dd = jnp.sum(do_b.astype(F32) * o_ref[0], axis=1, keepdims=True)
        rhs = jnp.concatenate([qa_ref[0], doa_ref[0]], axis=0)
        edge = _qk_bound(q, kmax_ref) + fq - lse_q

        def negligible(j):
            return jnp.logical_and(j < i, jnp.max(edge - fr_ref[0, j][:, BQ - 1:BQ]) < STOP_WIDE)

        first = lax.while_loop(negligible, lambda j: j + 1, 0)

        shift = fq - lse_q

        def block(j, offset, diag):
            k = k_ref[0, j]
            p = jnp.exp(_dot_nt(q, k) + (shift + offset) - fr_ref[0, j])
            if diag:
                row, col = _causal_iota()
                p = jnp.where(col <= row, p, 0.0)
            ds = p * (_dot_nt(do_b, v_ref[0, j]) - dd)
            ds_b = ds.astype(BF16)
            dkv_ref[0, j] += _dot_tn(jnp.concatenate([ds_b, p.astype(BF16)], axis=0), rhs)
            dfk_ref[0, j] -= jnp.sum(ds, axis=0, keepdims=True)
            return _dot(ds_b, k)

        def two_blocks(n, dq):
            ja = first + 2 * n
            jb = ja + 1
            absent = jnp.where(jb < i, 0.0, NEG)
            jb = jnp.minimum(jb, i - 1)
            return dq + (block(ja, 0.0, False) + block(jb, absent, False))

        dq = lax.fori_loop(0, (i - first + 1) // 2, two_blocks, jnp.zeros((BQ, dh), F32))
        dq_ref[0] = dq + block(i, 0.0, True)

        if scatter is not None:
            @pl.when(last_step)
            def _():
                _chip_scatter(g_ref, land_ref, *refs[16:])[1]()

    qin, kin, qin2, _, qspec, kspec2 = _head_specs(nj, 0)
    cspec = pl.BlockSpec((1, BQ, 1), lambda h, i: (h, i, 0))
    rspec = pl.BlockSpec((1, nj, 1, BQ), lambda h, i: (h, 0, 0, 0))
    in_specs = [qin, kin, kin, qin2, qin2, cspec, rspec, qspec, qin, cspec]
    out_specs = [qspec, kspec2, rspec]
    out_shape = [jax.ShapeDtypeStruct((nh, t, dh), F32), jax.ShapeDtypeStruct((nh, nj, BQ, 2 * dh), F32),
                 jax.ShapeDtypeStruct((nh, nj, 1, BQ), F32)]
    scratch = [pltpu.VMEM((8, 128), F32)]
    args = [qs, kn4, v4, qa, doa, fcol, frow4, o, do, lse]
    if scatter is not None:
        in_specs.append(_ANY)
        out_specs.append(_ANY)
        out_shape.append(jax.ShapeDtypeStruct(scatter.shape, scatter.dtype))
        scratch += _chip_sems()
        args.append(scatter)
    return pl.pallas_call(body, grid=(nh, nj), in_specs=in_specs, out_specs=out_specs, out_shape=out_shape,
                          scratch_shapes=scratch, compiler_params=_params(), name=name)(*args)


def _sb_logs(z, diag):
    e = jnp.exp(-jnp.abs(z))
    sp = jnp.log(1.0 + e)
    logb = jnp.minimum(z, 0.0) - sp
    lom = -jnp.maximum(z, 0.0) - sp
    strict = None
    if diag:
        row, col = _causal_iota()
        strict = col < row
        lom = jnp.where(strict, lom, 0.0)
    return logb, lom, e, strict


SB_GROUP = BQ // 2


def _sums_over_later_keys(lom, tri_m):
    halves = [lom[:, :SB_GROUP], lom[:, SB_GROUP:]]
    totals = [jnp.sum(x, axis=1, keepdims=True) for x in halves]
    within = []
    for x in halves:
        hi, lo = _split2(x)
        within.append(_dot(hi, tri_m) + _dot(lo, tri_m))
    return jnp.concatenate([within[0] + totals[1], within[1]], axis=1), totals[0] + totals[1]


def _sums_over_earlier_keys(da, tri_m):
    halves = [da[:, :SB_GROUP], da[:, SB_GROUP:]]
    totals = [jnp.sum(x, axis=1, keepdims=True) for x in halves]
    within = [_dot_nt(x.astype(BF16), tri_m) for x in halves]
    return jnp.concatenate([within[0], within[1] + totals[0]], axis=1), totals[0] + totals[1]


def sb_fwd(qs, kn4, va4, tri, *, name, gather=None):
    _, t, dh = qs.shape
    nh = H_SB
    nj = t // BQ
    assert nj <= 128

    def body(*refs):
        if gather is None:
            q_ref, k_ref, v_ref, tri_ref, o_ref, rs_ref = refs
        else:
            q_ref, k_ref, v_ref, tri_ref, src_ref, o_ref, rs_ref, dst_ref = refs[:8]
            first_step, last_step = _first_and_last_step()

            @pl.when(first_step)
            def _():
                _chip_gather(src_ref, dst_ref, *refs[8:])[0]()

        i = pl.program_id(1)
        q = q_ref[0]
        tri_m = tri_ref[...]
        lane = lax.broadcasted_iota(jnp.int32, (BQ, 128), 1)

        def block(j, carry, diag):
            run, acc, rall = carry
            logb, lom, _, strict = _sb_logs(_dot_nt(q, k_ref[0, j]), diag)
            later, total = _sums_over_later_keys(lom, tri_m)
            w = jnp.exp(logb + later + run)
            if diag:
                w = jnp.where(strict, w, 0.0)
            acc = acc + _dot(w.astype(BF16), v_ref[0, j])
            rall = jnp.where(lane == j, run, rall)
            return run + total, acc, rall

        init = (jnp.zeros((BQ, 1), F32), jnp.zeros((BQ, 2 * dh), F32), jnp.full((BQ, 128), NEG, F32))
        carry = block(i, init, True)

        def cond(c):
            n, carry = c
            return jnp.logical_and(n < i, jnp.max(carry[0]) >= STOP)

        _, (_, acc, rall) = lax.while_loop(cond, lambda c: (c[0] + 1, block(i - 1 - c[0], c[1], False)), (0, carry))
        o_ref[0] = acc[:, :dh].astype(BF16)
        rs_ref[0] = rall

        if gather is not None:
            @pl.when(last_step)
            def _():
                _chip_gather(src_ref, dst_ref, *refs[8:])[1]()

    qin, kin, _, kin2, qspec, _ = _head_specs(nj, H_FOX)
    rspec = pl.BlockSpec((1, BQ, 128), lambda h, i: (h, i, 0))
    in_specs = [qin, kin, kin2, pl.BlockSpec((SB_GROUP, SB_GROUP), lambda h, i: (0, 0))]
    out_specs = [qspec, rspec]
    out_shape = [jax.ShapeDtypeStruct((nh, t, dh), BF16), jax.ShapeDtypeStruct((nh, t, 128), F32)]
    scratch = []
    args = [qs, kn4, va4, tri]
    if gather is not None:
        in_specs.append(_ANY)
        out_specs.append(_ANY)
        out_shape.append(jax.ShapeDtypeStruct((N_CHIPS,) + gather.shape, gather.dtype))
        scratch += _chip_sems()
        args.append(gather)
    return pl.pallas_call(body, grid=(nh, nj), in_specs=in_specs, out_specs=out_specs, out_shape=out_shape,
                          scratch_shapes=scratch, compiler_params=_params(), name=name)(*args)


def sb_bwd(qs, kn4, v4, qa, doa, tri, do, rsave, *, name, scatter=None):
    _, t, dh = qs.shape
    nh = H_SB
    nj = t // BQ

    def body(*refs):
        q_ref, k_ref, v_ref, qa_ref, doa_ref, tri_ref, do_ref, rs_ref = refs[:8]
        if scatter is None:
            dq_ref, dkv_ref = refs[8:]
        else:
            g_ref, dq_ref, dkv_ref, land_ref = refs[8:12]
            first_step, last_step = _first_and_last_step()

            @pl.when(first_step)
            def _():
                _chip_scatter(g_ref, land_ref, *refs[12:])[0]()

        i = pl.program_id(1)

        @pl.when(i == 0)
        def _():
            dkv_ref[...] = jnp.zeros_like(dkv_ref)

        q = q_ref[0]
        do_b = do_ref[0]
        tri_m = tri_ref[...]
        rall = rs_ref[0]
        lane = lax.broadcasted_iota(jnp.int32, (BQ, 128), 1)
        rhs = jnp.concatenate([qa_ref[0], doa_ref[0]], axis=0)
        lane1 = lax.broadcasted_iota(jnp.int32, (1, 128), 1)
        unvisited = jnp.logical_and(lane1 < i, jnp.max(rall, axis=0, keepdims=True) < STOP)
        first = jnp.sum(unvisited.astype(jnp.int32))

        def block(j, carry, diag):
            dq, ecar = carry
            k = k_ref[0, j]
            z = _dot_nt(q, k)
            logb, lom, e, strict = _sb_logs(z, diag)
            run = jnp.sum(jnp.where(lane == j, rall, 0.0), axis=1, keepdims=True)
            w = jnp.exp(logb + _sums_over_later_keys(lom, tri_m)[0] + run)
            if diag:
                w = jnp.where(strict, w, 0.0)
            da = w * _dot_nt(do_b, v_ref[0, j])
            earlier, da_total = _sums_over_earlier_keys(da, tri_m)
            before = earlier + ecar
            inv = 1.0 / (1.0 + e)
            beta = jnp.where(z >= 0.0, 1.0, e) * inv
            one_minus = jnp.where(z >= 0.0, e, 1.0) * inv
            dz = da * one_minus - before * beta
            if diag:
                dz = jnp.where(strict, dz, 0.0)
            dz_b = dz.astype(BF16)
            dkv_ref[0, j] += _dot_tn(jnp.concatenate([dz_b, w.astype(BF16)], axis=0), rhs)
            return dq + _dot(dz_b, k), ecar + da_total

        carry = lax.fori_loop(first, i, lambda j, c: block(j, c, False),
                              (jnp.zeros((BQ, dh), F32), jnp.zeros((BQ, 1), F32)))
        dq, _ = block(i, carry, True)
        dq_ref[0] = dq

        if scatter is not None:
            @pl.when(last_step)
            def _():
                _chip_scatter(g_ref, land_ref, *refs[12:])[1]()

    qin, kin, qin2, _, qspec, kspec2 = _head_specs(nj, H_FOX)
    in_specs = [qin, kin, kin, qin2, qin2, pl.BlockSpec((SB_GROUP, SB_GROUP), lambda h, i: (0, 0)), qin,
                pl.BlockSpec((1, BQ, 128), lambda h, i: (h, i, 0))]
    out_specs = [qspec, kspec2]
    out_shape = [jax.ShapeDtypeStruct((nh, t, dh), F32), jax.ShapeDtypeStruct((nh, nj, BQ, 2 * dh), F32)]
    scratch = []
    args = [qs, kn4, v4, qa, doa, tri, do, rsave]
    if scatter is not None:
        in_specs.append(_ANY)
        out_specs.append(_ANY)
        out_shape.append(jax.ShapeDtypeStruct(scatter.shape, scatter.dtype))
        scratch += _chip_sems()
        args.append(scatter)
    return pl.pallas_call(body, grid=(nh, nj), in_specs=in_specs, out_specs=out_specs, out_shape=out_shape,
                          scratch_shapes=scratch, compiler_params=_params(), name=name)(*args)


def loss_head(y, target, *, name, tm=512):
    t, d = y.shape

    def body(y_ref, t_ref, l_ref, dy_ref, acc_ref):
        i = pl.program_id(0)
        diff = y_ref[...] - t_ref[...]
        dy_ref[...] = diff * (1.0 / d)
        part = jnp.sum(diff * diff, axis=0, keepdims=True)

        @pl.when(i == 0)
        def _():
            acc_ref[...] = part

        @pl.when(i > 0)
        def _():
            acc_ref[...] += part

        @pl.when(i == pl.num_programs(0) - 1)
        def _():
            l_ref[...] = jnp.full(l_ref.shape, (0.5 / d) * jnp.sum(acc_ref[...]), F32)

    row = pl.BlockSpec((tm, d), lambda i: (i, 0))
    return pl.pallas_call(
        body, grid=(t // tm,), in_specs=[row, row],
        out_specs=[pl.BlockSpec((8, 128), lambda i: (0, 0)), row],
        out_shape=[jax.ShapeDtypeStruct((8, 128), F32), jax.ShapeDtypeStruct((t, d), F32)],
        scratch_shapes=[pltpu.VMEM((1, d), F32)], compiler_params=_params(), name=name)(y, target)


def _to_heads(a):
    t = a.shape[0]
    return a.reshape(t, N_HEADS, HEAD_DIM).transpose(1, 0, 2)


def _from_heads(a):
    t = a.shape[1]
    return a.transpose(1, 0, 2).reshape(t, MIX)


def _lanes_to_chunks(a):
    r, t = a.shape
    return a.reshape(r, t // 128, 128).transpose(1, 0, 2)


def _chunks_to_lanes(a):
    nc, r, _ = a.shape
    return a.transpose(1, 0, 2).reshape(r, nc * 128)


def _constants():
    idx = jnp.arange(128)
    bd = (idx[:, None] // HEAD_DIM == idx[None, :] // HEAD_DIM).astype(BF16)
    tri_le = (idx[:, None] <= idx[None, :]).astype(BF16)
    tri_ge = (idx[:, None] >= idx[None, :]).astype(BF16)
    jdx = jnp.arange(SB_GROUP)
    tri_gt = (jdx[:, None] > jdx[None, :]).astype(BF16)
    return dict(bd=bd, tri_le=tri_le, tri_ge=tri_ge, tri_gt=tri_gt)


def attn_layer_fwd(h, w, cst, gather=None):
    t = h.shape[0]
    nj = t // BQ
    xn, proj = rms_mm_nn(h, w["norm"], w["w_in"], tn=640, name="attn_in_proj")
    qs, kn, vb, logf = attn_prep_fwd(proj, w["gq"], w["gk"], w["fbias"], cst["bd"], name="attn_prep_fwd")
    logf3 = _lanes_to_chunks(logf)
    cum = _chunks_to_lanes(gate_cumsum(logf3, cst["tri_le"], name="gate_cumsum"))
    fcol = cum.reshape(H_FOX, t, 1)
    frow4 = cum.reshape(H_FOX, nj, 1, BQ)
    qh = _to_heads(qs)
    kh4 = _to_heads(kn).reshape(N_HEADS, nj, BQ, HEAD_DIM)
    vh4 = _to_heads(vb).reshape(N_HEADS, nj, BQ, HEAD_DIM)
    ones = jnp.ones(vh4.shape[:-1] + (1,), BF16)
    va4 = jnp.concatenate([vh4, ones, jnp.zeros(vh4.shape[:-1] + (HEAD_DIM - 1,), BF16)], axis=-1)
    if gather is None:
        (o_f, lse), (o_s, rsave), gathered = (fox_fwd(qh, kh4, va4, fcol, frow4, name="fox_fwd"),
                                              sb_fwd(qh, kh4, va4, cst["tri_gt"], name="sb_fwd"), None)
    else:
        o_f, lse, gathered_a = fox_fwd(qh, kh4, va4, fcol, frow4, name="fox_fwd_gather", gather=gather[0])
        o_s, rsave, gathered_b = sb_fwd(qh, kh4, va4, cst["tri_gt"], name="sb_fwd_gather", gather=gather[1])
        gathered = gather[2]((gathered_a, gathered_b))
        w = gathered[0][0]
    o = _from_heads(jnp.concatenate([o_f.astype(BF16), o_s], axis=0))
    h2 = mm_nn(o, w["w_out"], add=h, name="mix_out_proj")
    saved = dict(h=h, xn=xn, proj=proj, logf3=logf3, fcol=fcol, frow4=frow4, qh=qh, kh4=kh4, vh4=vh4,
                 o_f=o_f, lse=lse, rsave=rsave, o=o)
    return h2, saved, gathered


def attn_layer_bwd(dh, w, s, cst, scatter=None):
    t = dh.shape[0]
    dh3 = dh[None]
    do = mm_nt(dh3, w["w_out"], out_dtype=BF16, name="mix_out_bwd_bf16")
    g_w_out = mm_tn(s["o"], dh3, name="mix_out_wgrad")
    doh = _to_heads(do)
    zeros = jnp.zeros_like(doh)
    qa = jnp.concatenate([s["qh"], zeros], axis=-1)
    doa = jnp.concatenate([zeros, doh], axis=-1)
    fox_args = (s["qh"], s["kh4"], s["vh4"], qa, doa, s["fcol"], s["frow4"], s["o_f"], doh, s["lse"])
    sb_args = (s["qh"], s["kh4"], s["vh4"], qa, doa, cst["tri_gt"], doh, s["rsave"])
    if scatter is None:
        (dq_f, dkv_f, dfk), (dq_s, dkv_s), landed = fox_bwd(*fox_args, name="fox_bwd"), sb_bwd(*sb_args, name="sb_bwd"), None
    else:
        chunks_a, chunks_b = scatter(g_w_out)
        dq_f, dkv_f, dfk, landed_a = fox_bwd(*fox_args, name="fox_bwd_scatter", scatter=chunks_a)
        dq_s, dkv_s, landed_b = sb_bwd(*sb_args, name="sb_bwd_scatter", scatter=chunks_b)
        landed = (landed_a, landed_b)
    dqs = _from_heads(jnp.concatenate([dq_f, dq_s], axis=0))
    dkv = jnp.concatenate([dkv_f, dkv_s], axis=0).reshape(N_HEADS, t, 2 * HEAD_DIM)
    dkn = _from_heads(dkv[:, :, :HEAD_DIM])
    dv = _from_heads(dkv[:, :, HEAD_DIM:])
    dcum3 = _lanes_to_chunks(dfk.reshape(H_FOX, t))
    dfl3, dbias = gate_cumsum_bwd(dcum3, s["logf3"], cst["tri_ge"], name="gate_cumsum_bwd")
    dfl = jnp.pad(_chunks_to_lanes(dfl3).T, ((0, 0), (0, 128 - H_FOX))).astype(BF16)
    dproj, dgq, dgk = attn_prep_bwd(s["proj"], dqs, dkn, dv, dfl, w["gq"], w["gk"], cst["bd"], name="attn_prep_bwd")
    g_w_in = mm_tn(s["xn"], dproj[None], tn=640, name="attn_in_wgrad")[:, :ATTN_IN]
    dh2, g_norm = mm_nt_rms_bwd(dproj[None], w["w_in"], s["h"], w["norm"], dh, name="attn_in_bwd")
    dgq = dgq.reshape(N_HEADS, HEAD_DIM)
    dgk = dgk.reshape(N_HEADS, HEAD_DIM)
    grads = dict(norm=g_norm[0], w_in=g_w_in, f_bias=dbias[:, 0], fox_q=dgq[:H_FOX].sum(0), fox_k=dgk[:H_FOX].sum(0),
                 sb_q=dgq[H_FOX:].sum(0), sb_k=dgk[H_FOX:].sum(0), w_out=g_w_out)
    return dh2, grads, landed


def conv_layer_fwd(h, w):
    xn, proj3 = rms_mm_nn(h, w["norm"], w["w_in"], parts=3, name="conv_in_proj")
    y = conv_mix_fwd(proj3, w["ck"], name="conv_mix_fwd")
    h2 = mm_nn(y, w["w_out"], add=h, name="mix_out_proj")
    return h2, dict(h=h, xn=xn, proj3=proj3, y=y)


def conv_layer_bwd(dh, w, s):
    dh3 = dh[None]
    g_w_out = mm_tn(s["y"], dh3, name="mix_out_wgrad")
    dproj3, dck = conv_mix_bwd(dh, w["w_out"], s["proj3"], w["ck"], name="conv_mix_bwd")
    g_w_in = mm_tn(s["xn"], dproj3, name="conv_in_wgrad")
    dh2, g_norm = mm_nt_rms_bwd(dproj3, w["w_in"], s["h"], w["norm"], dh, name="conv_in_bwd")
    return dh2, dict(norm=g_norm[0], w_in=g_w_in, ck=dck[0, :3], w_out=g_w_out)


def ffn_layer_fwd(h, w):
    xn, up2 = rms_mm_nn(h, w["norm"], w["w_up"], parts=2, tn=1408, name="ffn_up_proj")
    h2, act = ffn_act_down_fwd(up2, w["cw2"], w["w_down"], h, name="ffn_act_down_fwd")
    return h2, dict(h=h, xn=xn, up2=up2, act=act)


def ffn_layer_bwd(dh, w, s):
    dh3 = dh[None]
    g_w_down = mm_tn(s["act"], dh3, tk=1408, name="ffn_down_wgrad")
    dup2, dcw = ffn_act_bwd(dh, w["w_down"], s["up2"], w["cw2"], name="ffn_act_bwd")
    g_w_up = mm_tn(s["xn"], dup2, tn=1408, name="ffn_up_wgrad")
    dh2, g_norm = mm_nt_rms_bwd(dup2, w["w_up"], s["h"], w["norm"], dh, name="ffn_up_bwd")
    g_cw = jnp.concatenate([dcw[0, :3], dcw[1, :3]], axis=1)
    return dh2, dict(norm=g_norm[0], w_up=g_w_up, cw=g_cw, w_down=g_w_down)


def forward_backward(x, target, wa, wc, wf, *, late_weights=None, late_chunks=None):
    cst = _constants()
    h = x
    saved = []
    layer = 0
    while layer == 0 or layer < len(wf):
        i = layer // 2
        if layer % 2 == 0:
            h, sm, built = attn_layer_fwd(h, wa[i], cst, gather=late_weights if late_weights and layer == 0 else None)
            if built is not None:
                wa, wc, wf = built
        else:
            h, sm = conv_layer_fwd(h, wc[i])
        h, sf = ffn_layer_fwd(h, wf[layer])
        saved.append((sm, sf))
        layer += 1
    depth = len(wf)
    loss_blk, dh = loss_head(h, target, name="loss_head")
    ga, gc, gf = [None] * len(wa), [None] * len(wc), [None] * depth
    landed = None
    for layer in reversed(range(depth)):
        i = layer // 2
        sm, sf = saved[layer]
        dh, gf[layer] = ffn_layer_bwd(dh, wf[layer], sf)
        if layer % 2 == 0:
            chunks = None
            if late_chunks and layer == 0:
                chunks = lambda g_w_out: late_chunks([dict(w_out=g_w_out)] + ga[1:], gc, gf)
            dh, ga[i], got = attn_layer_bwd(dh, wa[i], sm, cst, scatter=chunks)
            landed = got if got is not None else landed
        else:
            dh, gc[i] = conv_layer_bwd(dh, wc[i], sm)
    return loss_blk, dh, ga, gc, gf, landed


def _part_rows(shape, width, row_mult):
    n = 1
    for s in shape:
        n *= s
    rows = -(-n // width)
    return -(-rows // row_mult) * row_mult


def _pack_rows(arrs, width, row_mult, dtype, total_rows=None):
    parts = []
    used = 0
    for a in arrs:
        rows = _part_rows(a.shape, width, row_mult)
        flat = a.astype(dtype).reshape(-1)
        flat = jnp.pad(flat, (0, rows * width - flat.shape[0]))
        parts.append(flat.reshape(rows, width))
        used += rows
    if total_rows is not None and total_rows > used:
        parts.append(jnp.zeros((total_rows - used, width), dtype))
    return jnp.concatenate(parts, axis=0)


def _unpack_rows(packed, shapes, width, row_mult):
    out = []
    off = 0
    for shape in shapes:
        rows = _part_rows(shape, width, row_mult)
        n = 1
        for s in shape:
            n *= s
        out.append(packed[off:off + rows].reshape(-1)[:n].reshape(shape))
        off += rows
    return out


BIG_NAMES = ("attn_w_in", "attn_w_out", "conv_w_in", "conv_w_out", "ffn_w_up", "ffn_w_down")
BIG_AXIS = {"attn_w_in": 2, "attn_w_out": 1, "conv_w_in": 2, "conv_w_out": 1, "ffn_w_up": 2, "ffn_w_down": 1}
BIG_WIDTH = 1024
BIG_ROW_MULT = 16
BIG_TILE = 512
SMALL_TILE = 128
SMALL_SHARDED = ("conv_norm", "conv_kernel", "ffn_conv")
SMALL_AXIS = {"conv_norm": 1, "conv_kernel": 2, "ffn_conv": 2}
SMALL_REPLICATED = ("attn_norm", "attn_f_bias", "fox_q_gain", "fox_k_gain", "sb_q_gain", "sb_k_gain", "ffn_norm")
WEIGHT_ORDER = ("attn_norm", "attn_w_in", "attn_f_bias", "fox_q_gain", "fox_k_gain", "sb_q_gain", "sb_k_gain",
                "attn_w_out", "conv_norm", "conv_w_in", "conv_kernel", "conv_w_out", "ffn_norm", "ffn_w_up",
                "ffn_conv", "ffn_w_down")


def _big_total_rows(shapes):
    used = sum(_part_rows(s, BIG_WIDTH, BIG_ROW_MULT) for s in shapes)
    tile = BIG_TILE if used >= 8 * BIG_TILE else SMALL_TILE
    return -(-used // tile) * tile


def _place():
    x, y, c = lax.axis_index("x"), lax.axis_index("y"), lax.axis_index("c")
    other_chips = [(1 - x, y), (x, 1 - y), (1 - x, 1 - y)]
    return x, y, c, other_chips


_ANY = pl.BlockSpec(memory_space=pl.ANY)


def _chip_sems():
    return [pltpu.SemaphoreType.DMA((3,)), pltpu.SemaphoreType.DMA((3,)), pltpu.SemaphoreType.DMA]


def _chip_gather(src_ref, dst_ref, send_sems, recv_sems, local_sem):
    x, y, c, chips = _place()
    k = 2 * x + y

    def copy(j, slot):
        px, py = chips[j]
        return pltpu.make_async_remote_copy(src_ref=src_ref, dst_ref=dst_ref.at[slot], send_sem=send_sems.at[j],
                                            recv_sem=recv_sems.at[j], device_id=(px, py, c), device_id_type=MESH)

    def local():
        return pltpu.make_async_copy(src_ref, dst_ref.at[k], local_sem)

    def start():
        local().start()
        for j in range(3):
            copy(j, k).start()

    def finish():
        for j, (px, py) in enumerate(chips):
            copy(j, 2 * px + py).wait_recv()
        for j in range(3):
            copy(j, k).wait_send()
        local().wait()

    return start, finish


def _chip_scatter(g_ref, o_ref, send_sems, recv_sems, local_sem):
    x, y, c, chips = _place()
    k = 2 * x + y

    def copy(j, src_slot, dst_slot):
        px, py = chips[j]
        return pltpu.make_async_remote_copy(src_ref=g_ref.at[src_slot], dst_ref=o_ref.at[dst_slot],
                                            send_sem=send_sems.at[j], recv_sem=recv_sems.at[j],
                                            device_id=(px, py, c), device_id_type=MESH)

    def local():
        return pltpu.make_async_copy(g_ref.at[k], o_ref.at[k], local_sem)

    def start():
        local().start()
        for j, (px, py) in enumerate(chips):
            copy(j, 2 * px + py, k).start()

    def finish():
        for j, (px, py) in enumerate(chips):
            copy(j, k, 2 * px + py).wait_recv()
        for j, (px, py) in enumerate(chips):
            copy(j, 2 * px + py, k).wait_send()
        local().wait()

    return start, finish


def gather_chips(arrs, *, name):
    n = len(arrs)

    def body(*refs):
        hooks = [_chip_gather(refs[m], refs[n + m], *refs[2 * n + 3 * m:2 * n + 3 * m + 3]) for m in range(n)]
        for start, _ in hooks:
            start()
        for _, finish in hooks:
            finish()

    return pl.pallas_call(
        body, in_specs=[_ANY] * n, out_specs=[_ANY] * n,
        out_shape=[jax.ShapeDtypeStruct((N_CHIPS,) + a.shape, a.dtype) for a in arrs],
        scratch_shapes=_chip_sems() * n, name=name)(*arrs)


def scatter_chips(chunks, *, name):
    def body(g_ref, o_ref, send_sems, recv_sems, local_sem):
        start, finish = _chip_scatter(g_ref, o_ref, send_sems, recv_sems, local_sem)
        start()
        finish()

    return pl.pallas_call(
        body, in_specs=[_ANY], out_specs=_ANY, out_shape=jax.ShapeDtypeStruct(chunks.shape, chunks.dtype),
        scratch_shapes=_chip_sems(), name=name)(chunks)


def swap_cores(arrs, *, name):
    n = len(arrs)

    def body(*refs):
        x, y, c, _ = _place()
        copies = [pltpu.make_async_remote_copy(src_ref=refs[m], dst_ref=refs[n + m], send_sem=refs[2 * n + 2 * m],
                                               recv_sem=refs[2 * n + 2 * m + 1], device_id=(x, y, 1 - c),
                                               device_id_type=MESH) for m in range(n)]
        for cp in copies:
            cp.start()
        for cp in copies:
            cp.wait()

    return pl.pallas_call(
        body, in_specs=[_ANY] * n, out_specs=[_ANY] * n,
        out_shape=[jax.ShapeDtypeStruct(a.shape, a.dtype) for a in arrs],
        scratch_shapes=[pltpu.SemaphoreType.DMA, pltpu.SemaphoreType.DMA] * n, name=name)(*arrs)


def allreduce_small(p, *, name):
    r, w = p.shape

    def body(p_ref, o_ref, buf, send_sems, recv_sems):
        x, y, c, _ = _place()
        me = 4 * x + 2 * y + c
        buf[me] = p_ref[...]

        def peer_of(m):
            return (1 - x if m & 4 else x, 1 - y if m & 2 else y, 1 - c if m & 1 else c)

        def copy(m, slot):
            return pltpu.make_async_remote_copy(src_ref=p_ref, dst_ref=buf.at[slot], send_sem=send_sems.at[m - 1],
                                                recv_sem=recv_sems.at[m - 1], device_id=peer_of(m),
                                                device_id_type=MESH)

        sends = [copy(m, me) for m in range(1, 8)]
        for cp in sends:
            cp.start()
        for m in range(1, 8):
            px, py, pc = peer_of(m)
            copy(m, 4 * px + 2 * py + pc).wait_recv()
        for cp in sends:
            cp.wait_send()
        acc = buf[0]
        for d in range(1, 8):
            acc = acc + buf[d]
        o_ref[...] = acc

    vm = pl.BlockSpec(memory_space=pltpu.VMEM)
    return pl.pallas_call(
        body, in_specs=[vm], out_specs=vm, out_shape=jax.ShapeDtypeStruct((r, w), F32),
        scratch_shapes=[pltpu.VMEM((8, r, w), F32), pltpu.SemaphoreType.DMA((7,)), pltpu.SemaphoreType.DMA((7,))],
        name=name)(p)


def sum_chips(rv, *, name):
    _, r, w = rv.shape
    tile = BIG_TILE if r % BIG_TILE == 0 else SMALL_TILE

    def body(a_ref, b_ref, c_ref, d_ref, o_ref):
        o_ref[...] = ((a_ref[0].astype(F32) + b_ref[0].astype(F32)) + c_ref[0].astype(F32)) + d_ref[0].astype(F32)

    spec = lambda kk: pl.BlockSpec((1, tile, w), lambda i: (kk, i, 0))
    return pl.pallas_call(
        body, grid=(r // tile,), in_specs=[spec(0), spec(1), spec(2), spec(3)],
        out_specs=pl.BlockSpec((tile, w), lambda i: (i, 0)), out_shape=jax.ShapeDtypeStruct((r, w), F32),
        compiler_params=_params(), name=name)(rv, rv, rv, rv)


def add_pair(a, b, *, name):
    r, w = a.shape
    tile = BIG_TILE if r % BIG_TILE == 0 else SMALL_TILE

    def body(a_ref, b_ref, o_ref):
        o_ref[...] = a_ref[...] + b_ref[...]

    spec = pl.BlockSpec((tile, w), lambda i: (i, 0))
    return pl.pallas_call(body, grid=(r // tile,), in_specs=[spec, spec], out_specs=spec,
                          out_shape=jax.ShapeDtypeStruct((r, w), F32), compiler_params=_params(), name=name)(a, b)


def adamw(w, g, m, v, *, tm, name):
    r, c = w.shape
    assert r % tm == 0

    def body(w_ref, g_ref, m_ref, v_ref, d_ref, nm_ref, nv_ref):
        g_ = g_ref[...]
        m_ = ADAM_B1 * m_ref[...] + (1.0 - ADAM_B1) * g_
        v_ = ADAM_B2 * v_ref[...] + (1.0 - ADAM_B2) * (g_ * g_)
        m_hat = m_ / (1.0 - ADAM_B1 ** ADAM_STEP)
        v_hat = v_ / (1.0 - ADAM_B2 ** ADAM_STEP)
        d_ref[...] = -ADAM_LR * (m_hat / (jnp.sqrt(v_hat) + ADAM_EPS) + ADAM_WD * w_ref[...])
        nm_ref[...] = m_
        nv_ref[...] = v_

    spec = pl.BlockSpec((tm, c), lambda i: (i, 0))
    return pl.pallas_call(body, grid=(r // tm,), in_specs=[spec] * 4, out_specs=[spec] * 3,
                          out_shape=[jax.ShapeDtypeStruct((r, c), F32)] * 3, compiler_params=_params(), name=name)(w, g, m, v)


def kernel(x, attn_norm, attn_w_in, attn_f_bias, fox_q_gain, fox_k_gain, sb_q_gain, sb_k_gain, attn_w_out, conv_norm, conv_w_in, conv_kernel, conv_w_out, ffn_norm, ffn_w_up, ffn_conv, ffn_w_down, loss_target, m_attn_norm, m_attn_w_in, m_attn_f_bias, m_fox_q_gain, m_fox_k_gain, m_sb_q_gain, m_sb_k_gain, m_attn_w_out, m_conv_norm, m_conv_w_in, m_conv_kernel, m_conv_w_out, m_ffn_norm, m_ffn_w_up, m_ffn_conv, m_ffn_w_down, v_attn_norm, v_attn_w_in, v_attn_f_bias, v_fox_q_gain, v_fox_k_gain, v_sb_q_gain, v_sb_k_gain, v_attn_w_out, v_conv_norm, v_conv_w_in, v_conv_kernel, v_conv_w_out, v_ffn_norm, v_ffn_w_up, v_ffn_conv, v_ffn_w_down):
    a = dict(locals())
    chip = 2 * lax.axis_index("x") + lax.axis_index("y")
    n_attn, n_conv, depth = attn_norm.shape[0], conv_norm.shape[0], ffn_norm.shape[0]

    units = [(name, l) for name in BIG_NAMES for l in range(a[name].shape[0])]
    early = [("attn_w_in", 0)]
    late = [u for u in units if u not in early]
    late_b = [("attn_w_out", 0), ("conv_w_in", n_conv - 1), ("conv_w_out", n_conv - 1), ("ffn_w_up", depth - 1),
              ("ffn_w_down", depth - 1)]
    late_a = [u for u in late if u not in late_b]
    late_sb = [("ffn_w_up", 0), ("ffn_w_down", 0), ("ffn_w_up", 1), ("ffn_w_down", 1)]
    late_sa = [u for u in late if u not in late_sb]

    def unit_shape(u):
        return a[u[0]].shape[1:]

    def pack_units(us, get):
        return _pack_rows([get(u) for u in us], BIG_WIDTH, BIG_ROW_MULT, BF16, _big_total_rows([unit_shape(u) for u in us]))

    def unpack_units(packed, us):
        return dict(zip(us, _unpack_rows(packed, [unit_shape(u) for u in us], BIG_WIDTH, BIG_ROW_MULT)))

    def full_units(gathered, us):
        per_chip = [unpack_units(gathered[kk], us) for kk in range(N_CHIPS)]
        return {u: jnp.concatenate([per_chip[kk][u] for kk in range(N_CHIPS)], axis=BIG_AXIS[u[0]] - 1) for u in us}

    def shard(u):
        return a[u[0]][u[1]]

    small_shapes = [a[n].shape for n in SMALL_SHARDED]
    packed_s = _pack_rows([a[n] for n in SMALL_SHARDED], 128, 8, F32)
    gath_e, gath_s = gather_chips([pack_units(early, shard), packed_s], name="gather_weights")
    full_e = full_units(gath_e, early)
    full = {}
    per_chip = [_unpack_rows(gath_s[kk], small_shapes, 128, 8) for kk in range(N_CHIPS)]
    for n, name in enumerate(SMALL_SHARDED):
        full[name] = jnp.concatenate([per_chip[kk][n] for kk in range(N_CHIPS)], axis=SMALL_AXIS[name])

    def attn_weights(i, fu):
        return dict(
            norm=attn_norm[i][None],
            w_in=jnp.pad(fu[("attn_w_in", i)], ((0, 0), (0, ATTN_IN_PAD - ATTN_IN))),
            fbias=jnp.pad(attn_f_bias[i], (0, 128 - H_FOX))[None],
            gq=jnp.concatenate([jnp.tile(fox_q_gain[i], H_FOX), jnp.tile(sb_q_gain[i], H_SB)])[None],
            gk=jnp.concatenate([jnp.tile(fox_k_gain[i], H_FOX), jnp.tile(sb_k_gain[i], H_SB)])[None],
            w_out=fu.get(("attn_w_out", i)))

    def build_weights(gathered):
        fu = {**full_e, **full_units(gathered[0], late_a), **full_units(gathered[1], late_b)}
        wa = [attn_weights(i, fu) for i in range(n_attn)]
        wc = [dict(norm=full["conv_norm"][i][None], w_in=fu[("conv_w_in", i)], ck=full["conv_kernel"][i][None],
                   w_out=fu[("conv_w_out", i)]) for i in range(n_conv)]
        wf = []
        for l in range(depth):
            cw = full["ffn_conv"][l]
            wf.append(dict(norm=ffn_norm[l][None], w_up=fu[("ffn_w_up", l)], cw2=jnp.stack([cw[:, :D_FF], cw[:, D_FF:]]),
                           w_down=fu[("ffn_w_down", l)]))
        return wa, wc, wf

    def chunk_of(u, kk, ga, gc, gf):
        name, l = u
        g = {"attn_w_in": lambda: ga[l]["w_in"], "attn_w_out": lambda: ga[l]["w_out"],
             "conv_w_in": lambda: gc[l]["w_in"], "conv_w_out": lambda: gc[l]["w_out"],
             "ffn_w_up": lambda: gf[l]["w_up"], "ffn_w_down": lambda: gf[l]["w_down"]}[name]()
        width = a[name].shape[BIG_AXIS[name]]
        return lax.slice_in_dim(g, kk * width, (kk + 1) * width, axis=BIG_AXIS[name] - 1)

    def chunks_of(us, ga, gc, gf):
        return jnp.stack([pack_units(us, lambda u: chunk_of(u, kk, ga, gc, gf)) for kk in range(N_CHIPS)])

    loss_blk, grad_x, ga, gc, gf, landed_late = forward_backward(
        x[0], loss_target[0], [attn_weights(0, full_e)], [], [],
        late_weights=(pack_units(late_a, shard), pack_units(late_b, shard), build_weights),
        late_chunks=lambda ga, gc, gf: (chunks_of(late_sa, ga, gc, gf), chunks_of(late_sb, ga, gc, gf)))

    landed = [scatter_chips(chunks_of(early, ga, gc, gf), name="scatter_grads"), landed_late[0], landed_late[1]]
    mine = [sum_chips(buf, name="sum_chips") for buf in landed]
    theirs = swap_cores(mine, name="swap_cores")
    g_units = {}
    for us, m, th in zip((early, late_sa, late_sb), mine, theirs):
        g_units.update(unpack_units(add_pair(m, th, name="add_cores"), us))
    grads = {name: jnp.stack([g_units[(name, l)] for l in range(a[name].shape[0])]) for name in BIG_NAMES}

    small_full = [
        loss_blk,
        jnp.stack([g["norm"] for g in ga]), jnp.stack([g["f_bias"] for g in ga]),
        jnp.stack([g["fox_q"] for g in ga]), jnp.stack([g["fox_k"] for g in ga]),
        jnp.stack([g["sb_q"] for g in ga]), jnp.stack([g["sb_k"] for g in ga]),
        jnp.stack([g["norm"] for g in gf]),
        jnp.stack([g["norm"] for g in gc]), jnp.stack([g["ck"] for g in gc]), jnp.stack([g["cw"] for g in gf]),
    ]
    summed = allreduce_small(_pack_rows(small_full, 128, 8, F32), name="allreduce_small")
    parts = _unpack_rows(summed, [p.shape for p in small_full], 128, 8)
    loss = parts[0][0, 0]
    for name, g in zip(SMALL_REPLICATED, parts[1:8]):
        grads[name] = g
    for name, g in zip(SMALL_SHARDED, parts[8:]):
        width = a[name].shape[SMALL_AXIS[name]]
        grads[name] = lax.dynamic_slice_in_dim(g, chip * width, width, axis=SMALL_AXIS[name])

    delta, new_m, new_v = {}, {}, {}
    for name in BIG_NAMES:
        shape = a[name].shape
        flat = lambda arr: arr.reshape(-1, shape[-1])
        d_, m_, v_ = adamw(flat(a[name]), flat(grads[name]), flat(a["m_" + name]), flat(a["v_" + name]), tm=256,
                           name="adamw")
        delta[name], new_m[name], new_v[name] = d_.reshape(shape), m_.reshape(shape), v_.reshape(shape)
    small_names = SMALL_REPLICATED + SMALL_SHARDED
    small_shapes_local = [a[n].shape for n in small_names]
    pack = lambda prefix, src: _pack_rows([src[prefix + n] for n in small_names], 128, 8, F32)
    packed = adamw(pack("", a), pack("", grads), pack("m_", a), pack("v_", a), tm=8, name="adamw_small")
    for store, buf in zip((delta, new_m, new_v), packed):
        for name, arr in zip(small_names, _unpack_rows(buf, small_shapes_local, 128, 8)):
            store[name] = arr

    return (loss, grad_x[None], *[grads[n] for n in WEIGHT_ORDER], *[delta[n] for n in WEIGHT_ORDER],
            *[new_m[n] for n in WEIGHT_ORDER], *[new_v[n] for n in WEIGHT_ORDER])
```

```python
import functools

import jax
import jax.numpy as jnp
from jax import lax
from jax.experimental import pallas as pl
from jax.experimental.pallas import tpu as pltpu

F32 = jnp.float32
BF16 = jnp.bfloat16

D_MODEL = 1024
HEAD_DIM = 64
H_FOX = 8
H_SB = 8
N_HEADS = H_FOX + H_SB
MIX = N_HEADS * HEAD_DIM
ATTN_IN = 3 * MIX + H_FOX
ATTN_IN_PAD = 3 * MIX + 128
D_FF = 2816
EPS = 1e-6
SCALE = HEAD_DIM ** -0.5
NEG = -1e30

ADAM_LR = 0.001
ADAM_B1 = 0.9
ADAM_B2 = 0.999
ADAM_EPS = 1e-08
ADAM_WD = 0.01
ADAM_STEP = 10

VMEM_LIMIT = 56 * 1024 * 1024
HALO = 8
BQ = 512
N_CHIPS = 4
MESH = pl.DeviceIdType.MESH


def _params(**kw):
    return pltpu.CompilerParams(vmem_limit_bytes=VMEM_LIMIT, **kw)


def _dot(a, b):
    return jnp.dot(a, b, preferred_element_type=F32)


def _dot_nt(a, b):
    return lax.dot_general(a, b, (((1,), (1,)), ((), ())), preferred_element_type=F32)


def _dot_tn(a, b):
    return lax.dot_general(a, b, (((0,), (0,)), ((), ())), preferred_element_type=F32)


def _split2(x):
    hi = x.astype(BF16)
    lo = (x - hi.astype(F32)).astype(BF16)
    return hi, lo


def _split3(x):
    hi = x.astype(BF16)
    r = x - hi.astype(F32)
    mid = r.astype(BF16)
    lo = (r - mid.astype(F32)).astype(BF16)
    return hi, mid, lo


def mm_nn(a, b, *, add=None, out_dtype=F32, parts=1, tm=1024, tn=512, name):
    m, k = a.shape
    n = b.shape[1]
    np_ = n // parts
    nb = np_ // tn
    tm = min(tm, m)
    assert m % tm == 0 and np_ % tn == 0

    def body(*refs):
        if add is None:
            a_ref, b_ref, o_ref = refs
            acc = _dot(a_ref[...].astype(BF16), b_ref[...])
        else:
            a_ref, b_ref, r_ref, o_ref = refs
            acc = _dot(a_ref[...].astype(BF16), b_ref[...]) + r_ref[...]
        o_ref[...] = acc.astype(out_dtype).reshape(o_ref.shape)

    in_specs = [pl.BlockSpec((tm, k), lambda i, j: (i, 0)), pl.BlockSpec((k, tn), lambda i, j: (0, j))]
    args = [a, b]
    if add is not None:
        in_specs.append(pl.BlockSpec((tm, tn), lambda i, j: (i, j)))
        args.append(add)
    if parts == 1:
        out_spec = pl.BlockSpec((tm, tn), lambda i, j: (i, j))
        out_shape = jax.ShapeDtypeStruct((m, n), out_dtype)
    else:
        out_spec = pl.BlockSpec((1, tm, tn), lambda i, j: (j // nb, i, j % nb))
        out_shape = jax.ShapeDtypeStruct((parts, m, np_), out_dtype)
    return pl.pallas_call(body, grid=(m // tm, n // tn), in_specs=in_specs, out_specs=out_spec,
                          out_shape=out_shape, compiler_params=_params(), name=name)(*args)


def mm_nt(a3, b, *, out_dtype=F32, tm=1024, tn=512, name):
    p, m, kp = a3.shape
    n = b.shape[0]
    tm = min(tm, m)
    assert m % tm == 0 and n % tn == 0 and b.shape[1] == p * kp

    def body(a_ref, b_ref, o_ref, acc_ref):
        part = pl.program_id(2)
        prod = _dot_nt(a_ref[0].astype(BF16), b_ref[...])

        @pl.when(part == 0)
        def _():
            acc_ref[...] = prod

        @pl.when(part > 0)
        def _():
            acc_ref[...] += prod

        @pl.when(part == p - 1)
        def _():
            o_ref[...] = acc_ref[...].astype(out_dtype)

    return pl.pallas_call(
        body, grid=(m // tm, n // tn, p),
        in_specs=[pl.BlockSpec((1, tm, kp), lambda i, j, q: (q, i, 0)), pl.BlockSpec((tn, kp), lambda i, j, q: (j, q))],
        out_specs=pl.BlockSpec((tm, tn), lambda i, j, q: (i, j)),
        out_shape=jax.ShapeDtypeStruct((m, n), out_dtype),
        scratch_shapes=[pltpu.VMEM((tm, tn), F32)],
        compiler_params=_params(), name=name)(a3, b)


def mm_tn(a, b3, *, tk=512, tn=512, tt=2048, name):
    t, k = a.shape
    p, _, np_ = b3.shape
    nb = np_ // tn
    tt = min(tt, t)
    assert t % tt == 0 and k % tk == 0 and np_ % tn == 0

    def body(a_ref, b_ref, o_ref):
        prod = _dot_tn(a_ref[...].astype(BF16), b_ref[0].astype(BF16))

        @pl.when(pl.program_id(2) == 0)
        def _():
            o_ref[...] = prod

        @pl.when(pl.program_id(2) > 0)
        def _():
            o_ref[...] += prod

    return pl.pallas_call(
        body, grid=(k // tk, p * nb, t // tt),
        in_specs=[pl.BlockSpec((tt, tk), lambda i, j, s: (s, i)), pl.BlockSpec((1, tt, tn), lambda i, j, s: (j // nb, s, j % nb))],
        out_specs=pl.BlockSpec((tk, tn), lambda i, j, s: (i, j)),
        out_shape=jax.ShapeDtypeStruct((k, p * np_), F32),
        compiler_params=_params(), name=name)(a, b3)


def rms_mm_nn(h, g, b, *, parts=1, tm=1024, tn=512, name):
    t, d = h.shape
    n = b.shape[1]
    np_ = n // parts
    nb = np_ // tn
    tm = min(tm, t)
    assert t % tm == 0 and np_ % tn == 0

    def body(h_ref, g_ref, b_ref, xn_ref, o_ref):
        @pl.when(pl.program_id(1) == 0)
        def _():
            x = h_ref[...]
            r = lax.rsqrt(jnp.mean(x * x, axis=-1, keepdims=True) + EPS)
            xn_ref[...] = (x * r * g_ref[...]).astype(BF16)

        o_ref[...] = _dot(xn_ref[...], b_ref[...]).reshape(o_ref.shape)

    if parts == 1:
        out_spec = pl.BlockSpec((tm, tn), lambda i, j: (i, j))
        out_shape = jax.ShapeDtypeStruct((t, n), F32)
    else:
        out_spec = pl.BlockSpec((1, tm, tn), lambda i, j: (j // nb, i, j % nb))
        out_shape = jax.ShapeDtypeStruct((parts, t, np_), F32)
    row = pl.BlockSpec((tm, d), lambda i, j: (i, 0))
    return pl.pallas_call(
        body, grid=(t // tm, n // tn),
        in_specs=[row, pl.BlockSpec((1, d), lambda i, j: (0, 0)), pl.BlockSpec((d, tn), lambda i, j: (0, j))],
        out_specs=[row, out_spec], out_shape=[jax.ShapeDtypeStruct((t, d), BF16), out_shape],
        compiler_params=_params(), name=name)(h, g, b)


def mm_nt_rms_bwd(a3, b, h, g, dres, *, name, tm=512):
    p, t, kp = a3.shape
    d = b.shape[0]
    tm = min(tm, t)
    assert t % tm == 0 and b.shape[1] == p * kp

    def body(a_ref, b_ref, h_ref, g_ref, dres_ref, dh_ref, dg_ref, acc_ref):
        i = pl.program_id(0)
        part = pl.program_id(1)
        prod = _dot_nt(a_ref[0].astype(BF16), b_ref[...])

        @pl.when(part == 0)
        def _():
            acc_ref[...] = prod

        @pl.when(part > 0)
        def _():
            acc_ref[...] += prod

        @pl.when(part == p - 1)
        def _():
            x = h_ref[...]
            dy = acc_ref[...]
            r = lax.rsqrt(jnp.mean(x * x, axis=-1, keepdims=True) + EPS)
            gy = dy * g_ref[...]
            dot = jnp.mean(gy * x, axis=-1, keepdims=True)
            dh_ref[...] = dres_ref[...] + r * gy - x * (r * r * r * dot)
            _acc_rows(dg_ref, jnp.sum(dy * x * r, axis=0, keepdims=True), i == 0)

    row = pl.BlockSpec((tm, d), lambda i, q: (i, 0))
    vec = pl.BlockSpec((1, d), lambda i, q: (0, 0))
    return pl.pallas_call(
        body, grid=(t // tm, p),
        in_specs=[pl.BlockSpec((1, tm, kp), lambda i, q: (q, i, 0)), pl.BlockSpec((d, kp), lambda i, q: (0, q)),
                  row, vec, row],
        out_specs=[row, vec],
        out_shape=[jax.ShapeDtypeStruct((t, d), F32), jax.ShapeDtypeStruct((1, d), F32)],
        scratch_shapes=[pltpu.VMEM((tm, d), F32)], compiler_params=_params(), name=name)(a3, b, h, g, dres)


def _causal3(x, w):
    return w[0:1] * pltpu.roll(x, 2, 0) + w[1:2] * pltpu.roll(x, 1, 0) + w[2:3] * x


def _causal3_taps(x_ext, w, tm):
    x2 = pltpu.roll(x_ext, 2, 0)
    x1 = pltpu.roll(x_ext, 1, 0)
    y = w[0:1] * x2 + w[1:2] * x1 + w[2:3] * x_ext
    return y, (x2[HALO:HALO + tm], x1[HALO:HALO + tm], x_ext[HALO:HALO + tm])


def _anticausal3(z, w):
    n = z.shape[0]
    return w[2:3] * z + w[1:2] * pltpu.roll(z, n - 1, 0) + w[0:1] * pltpu.roll(z, n - 2, 0)


def _prev_spec(part, tm, tc, nrow8):
    del nrow8
    return pl.BlockSpec((1, HALO, tc), lambda j, i: (part, jnp.maximum(i * (tm // HALO) - 1, 0), j))


def _next_spec(part, tm, tc, nrow8):
    return pl.BlockSpec((1, HALO, tc), lambda j, i: (part, jnp.minimum((i + 1) * (tm // HALO), nrow8 - 1), j))


def _tile_spec(part, tm, tc):
    return pl.BlockSpec((1, tm, tc), lambda j, i: (part, i, j))


def _acc_rows(ref, val, first):
    @pl.when(first)
    def _():
        ref[...] = val

    @pl.when(jnp.logical_not(first))
    def _():
        ref[...] += val


def ffn_act_down_fwd(up2, cw2, w_down, h, *, name, tm=256, tc=1408):
    _, t, f = up2.shape
    d = h.shape[1]
    tm = min(tm, t)

    def body(g_ref, v_ref, gp_ref, vp_ref, w_ref, wd_ref, h_ref, o_ref, act_ref):
        keep = jnp.where(pl.program_id(0) == 0, 0.0, 1.0)
        for cc in range(f // tc):
            cols = slice(cc * tc, (cc + 1) * tc)
            g_ext = jnp.concatenate([gp_ref[0, :, cols] * keep, g_ref[0, :, cols]], axis=0)
            v_ext = jnp.concatenate([vp_ref[0, :, cols] * keep, v_ref[0, :, cols]], axis=0)
            ug = _causal3(g_ext, w_ref[0, :, cols])[HALO:]
            uv = _causal3(v_ext, w_ref[1, :, cols])[HALO:]
            act_ref[:, cols] = (ug * jax.nn.sigmoid(ug) * uv).astype(BF16)
        o_ref[...] = _dot(act_ref[...], wd_ref[...]) + h_ref[...]

    tile = lambda part: pl.BlockSpec((1, tm, f), lambda i: (part, i, 0))
    prev = lambda part: pl.BlockSpec((1, HALO, f), lambda i: (part, jnp.maximum(i * (tm // HALO) - 1, 0), 0))
    row = pl.BlockSpec((tm, d), lambda i: (i, 0))
    return pl.pallas_call(
        body, grid=(t // tm,),
        in_specs=[tile(0), tile(1), prev(0), prev(1), pl.BlockSpec((2, 3, f), lambda i: (0, 0, 0)),
                  pl.BlockSpec((f, d), lambda i: (0, 0)), row],
        out_specs=[row, pl.BlockSpec((tm, f), lambda i: (i, 0))],
        out_shape=[jax.ShapeDtypeStruct((t, d), F32), jax.ShapeDtypeStruct((t, f), BF16)],
        compiler_params=_params(), name=name)(up2, up2, up2, up2, cw2, w_down, h)


def ffn_act_bwd(dh, w_down, up2, cw2, *, name, tm=256, tc=1408):
    _, t, f = up2.shape
    d = dh.shape[1]
    n8 = t // HALO

    def body(d_ref, dn_ref, wd_ref, g_ref, v_ref, gp_ref, vp_ref, gn_ref, vn_ref, wg_ref, wv_ref, dup_ref, dw_ref):
        i = pl.program_id(1)
        first = i == 0
        keep_p = jnp.where(first, 0.0, 1.0)
        keep_n = jnp.where(i == pl.num_programs(1) - 1, 0.0, 1.0)
        wg = wg_ref[0]
        wv = wv_ref[0]
        g_ext = jnp.concatenate([gp_ref[0] * keep_p, g_ref[0], gn_ref[0]], axis=0)
        v_ext = jnp.concatenate([vp_ref[0] * keep_p, v_ref[0], vn_ref[0]], axis=0)
        dh_ext = jnp.concatenate([d_ref[...], dn_ref[...] * keep_n], axis=0)
        d_ext = _dot_nt(dh_ext.astype(BF16), wd_ref[...])
        ug, (g2, g1, g0) = _causal3_taps(g_ext, wg, tm)
        uv, (v2, v1, v0) = _causal3_taps(v_ext, wv, tm)
        ug = ug[HALO:]
        uv = uv[HALO:]
        s = jax.nn.sigmoid(ug)
        dg = d_ext * uv * (s * (1.0 + ug * (1.0 - s)))
        dv = d_ext * (ug * s)
        dup_ref[0] = _anticausal3(dg, wg)[:tm].astype(BF16)
        dup_ref[1] = _anticausal3(dv, wv)[:tm].astype(BF16)
        dgt = dg[:tm]
        dvt = dv[:tm]
        zero = jnp.zeros((HALO - 3, tc), F32)
        rows_g = [jnp.sum(dgt * x, axis=0, keepdims=True) for x in (g2, g1, g0)] + [zero]
        rows_v = [jnp.sum(dvt * x, axis=0, keepdims=True) for x in (v2, v1, v0)] + [zero]
        _acc_rows(dw_ref, jnp.stack([jnp.concatenate(rows_g, axis=0), jnp.concatenate(rows_v, axis=0)]), first)

    wspec = lambda part: pl.BlockSpec((1, 3, tc), lambda j, i: (part, 0, j))
    return pl.pallas_call(
        body, grid=(f // tc, t // tm),
        in_specs=[pl.BlockSpec((tm, d), lambda j, i: (i, 0)),
                  pl.BlockSpec((HALO, d), lambda j, i: (jnp.minimum((i + 1) * (tm // HALO), n8 - 1), 0)),
                  pl.BlockSpec((tc, d), lambda j, i: (j, 0)),
                  _tile_spec(0, tm, tc), _tile_spec(1, tm, tc), _prev_spec(0, tm, tc, n8), _prev_spec(1, tm, tc, n8),
                  _next_spec(0, tm, tc, n8), _next_spec(1, tm, tc, n8), wspec(0), wspec(1)],
        out_specs=[pl.BlockSpec((2, tm, tc), lambda j, i: (0, i, j)), pl.BlockSpec((2, HALO, tc), lambda j, i: (0, 0, j))],
        out_shape=[jax.ShapeDtypeStruct((2, t, f), BF16), jax.ShapeDtypeStruct((2, HALO, f), F32)],
        compiler_params=_params(), name=name)(dh, dh, w_down, up2, up2, up2, up2, up2, up2, cw2, cw2)


def conv_mix_fwd(proj3, ck, *, name, tm=512, tc=512):
    _, t, c = proj3.shape
    n8 = t // HALO

    def body(b_ref, c_ref, u_ref, cp_ref, up_ref, w_ref, o_ref):
        keep = jnp.where(pl.program_id(1) == 0, 0.0, 1.0)
        cu_ext = jnp.concatenate([cp_ref[0] * up_ref[0] * keep, c_ref[0] * u_ref[0]], axis=0)
        o_ref[...] = (b_ref[0] * _causal3(cu_ext, w_ref[0])[HALO:]).astype(BF16)

    return pl.pallas_call(
        body, grid=(c // tc, t // tm),
        in_specs=[_tile_spec(0, tm, tc), _tile_spec(1, tm, tc), _tile_spec(2, tm, tc), _prev_spec(1, tm, tc, n8),
                  _prev_spec(2, tm, tc, n8), pl.BlockSpec((1, 3, tc), lambda j, i: (0, 0, j))],
        out_specs=pl.BlockSpec((tm, tc), lambda j, i: (i, j)),
        out_shape=jax.ShapeDtypeStruct((t, c), BF16), compiler_params=_params(), name=name)(proj3, proj3, proj3, proj3, proj3, ck)


def conv_mix_bwd(dh, w_out, proj3, ck, *, name, tm=512, tc=512):
    _, t, c = proj3.shape
    d = dh.shape[1]
    n8 = t // HALO

    def body(d_ref, dn_ref, wo_ref, b_ref, c_ref, u_ref, cp_ref, up_ref, bn_ref, w_ref, dp_ref, dw_ref):
        i = pl.program_id(1)
        first = i == 0
        keep_p = jnp.where(first, 0.0, 1.0)
        keep_n = jnp.where(i == pl.num_programs(1) - 1, 0.0, 1.0)
        w = w_ref[0]
        cu_ext = jnp.concatenate([cp_ref[0] * up_ref[0] * keep_p, c_ref[0] * u_ref[0]], axis=0)
        cv, (x2, x1, x0) = _causal3_taps(cu_ext, w, tm)
        cv = cv[HALO:]
        dh_ext = jnp.concatenate([d_ref[...], dn_ref[...] * keep_n], axis=0)
        d_ext = _dot_nt(dh_ext.astype(BF16), wo_ref[...])
        dyt = d_ext[:tm]
        b_ext = jnp.concatenate([b_ref[0], bn_ref[0]], axis=0)
        dcv = d_ext * b_ext
        dcu = _anticausal3(dcv, w)[:tm]
        dp_ref[0] = (dyt * cv).astype(BF16)
        dp_ref[1] = (dcu * u_ref[0]).astype(BF16)
        dp_ref[2] = (dcu * c_ref[0]).astype(BF16)
        dcvt = dcv[:tm]
        rows = [jnp.sum(dcvt * x, axis=0, keepdims=True) for x in (x2, x1, x0)] + [jnp.zeros((HALO - 3, tc), F32)]
        _acc_rows(dw_ref, jnp.concatenate(rows, axis=0)[None], first)

    return pl.pallas_call(
        body, grid=(c // tc, t // tm),
        in_specs=[pl.BlockSpec((tm, d), lambda j, i: (i, 0)),
                  pl.BlockSpec((HALO, d), lambda j, i: (jnp.minimum((i + 1) * (tm // HALO), n8 - 1), 0)),
                  pl.BlockSpec((tc, d), lambda j, i: (j, 0)),
                  _tile_spec(0, tm, tc), _tile_spec(1, tm, tc), _tile_spec(2, tm, tc),
                  _prev_spec(1, tm, tc, n8), _prev_spec(2, tm, tc, n8),
                  _next_spec(0, tm, tc, n8), pl.BlockSpec((1, 3, tc), lambda j, i: (0, 0, j))],
        out_specs=[pl.BlockSpec((3, tm, tc), lambda j, i: (0, i, j)), pl.BlockSpec((1, HALO, tc), lambda j, i: (0, 0, j))],
        out_shape=[jax.ShapeDtypeStruct((3, t, c), BF16), jax.ShapeDtypeStruct((1, HALO, c), F32)],
        compiler_params=_params(), name=name)(dh, dh, w_out, proj3, proj3, proj3, proj3, proj3, proj3, ck)


def _head_sums(x, bd):
    hi, lo = _split2(x)
    return _dot(hi, bd) + _dot(lo, bd)


def attn_prep_fwd(proj, gq, gk, fbias, bd, *, name, tm=256):
    t = proj.shape[0]

    def body(q_ref, k_ref, v_ref, f_ref, gq_ref, gk_ref, fb_ref, bd_ref, qs_ref, kn_ref, vb_ref, lf_ref):
        bd = bd_ref[...]

        def headnorm(x_ref, g_ref, o_ref, scale):
            for c in range(MIX // 128):
                sl = slice(128 * c, 128 * (c + 1))
                x = x_ref[:, sl]
                r = lax.rsqrt(_head_sums(x * x, bd) * (1.0 / HEAD_DIM) + EPS)
                o_ref[:, sl] = (x * r * (g_ref[:, sl] * scale)).astype(BF16)

        headnorm(q_ref, gq_ref, qs_ref, SCALE)
        headnorm(k_ref, gk_ref, kn_ref, 1.0)
        vb_ref[...] = v_ref[...].astype(BF16)
        fl = f_ref[...] + fb_ref[...]
        logf = jnp.minimum(fl, 0.0) - jnp.log(1.0 + jnp.exp(-jnp.abs(fl)))
        lf_ref[...] = logf.T[0:H_FOX, :]

    col = lambda c: pl.BlockSpec((tm, MIX), lambda i: (i, c))
    vec = pl.BlockSpec((1, MIX), lambda i: (0, 0))
    out = pl.BlockSpec((tm, MIX), lambda i: (i, 0))
    return pl.pallas_call(
        body, grid=(t // tm,),
        in_specs=[col(0), col(1), col(2), pl.BlockSpec((tm, 128), lambda i: (i, 3 * MIX // 128)), vec, vec,
                  pl.BlockSpec((1, 128), lambda i: (0, 0)), pl.BlockSpec((128, 128), lambda i: (0, 0))],
        out_specs=[out, out, out, pl.BlockSpec((H_FOX, tm), lambda i: (0, i))],
        out_shape=[jax.ShapeDtypeStruct((t, MIX), BF16)] * 3 + [jax.ShapeDtypeStruct((H_FOX, t), F32)],
        compiler_params=_params(), name=name)(proj, proj, proj, proj, gq, gk, fbias, bd)


def attn_prep_bwd(proj, dq_f, dq_s, dkv_f, dkv_s, dfl, gq, gk, bd, *, name, tm=256):
    t = proj.shape[0]
    per_group = H_FOX // 2

    def body(q_ref, k_ref, dqf_ref, dqs_ref, dkvf_ref, dkvs_ref, dfl_ref, gq_ref, gk_ref, bd_ref, dp_ref, dgq_ref,
             dgk_ref):
        bd = bd_ref[...]
        first = pl.program_id(0) == 0
        low = lax.broadcasted_iota(jnp.int32, (tm, 128), 1) < HEAD_DIM

        def two_heads(fox_ref, sb_ref, c):
            ref = fox_ref if c < per_group else sb_ref
            return ref[2 * (c % per_group)], ref[2 * (c % per_group) + 1]

        def low_halves(ab):
            return jnp.where(low, ab[0], pltpu.roll(ab[1], HEAD_DIM, 1))

        def high_halves(ab):
            return jnp.where(low, pltpu.roll(ab[0], HEAD_DIM, 1), ab[1])

        def back(x_ref, grad, g_ref, col0, scale, dg_ref):
            parts = []
            for c in range(MIX // 128):
                sl = slice(128 * c, 128 * (c + 1))
                x = x_ref[:, sl]
                r = lax.rsqrt(_head_sums(x * x, bd) * (1.0 / HEAD_DIM) + EPS)
                dn = grad(c) * scale
                gy = dn * g_ref[:, sl]
                hs = _head_sums(gy * x, bd) * (1.0 / HEAD_DIM)
                dp_ref[:, col0 + 128 * c:col0 + 128 * (c + 1)] = (r * gy - x * (r * r * r * hs)).astype(BF16)
                parts.append(jnp.sum(dn * x * r, axis=0, keepdims=True))
            _acc_rows(dg_ref, jnp.concatenate(parts, axis=1), first)

        back(q_ref, lambda c: low_halves(two_heads(dqf_ref, dqs_ref, c)), gq_ref, 0, SCALE, dgq_ref)
        back(k_ref, lambda c: low_halves(two_heads(dkvf_ref, dkvs_ref, c)), gk_ref, MIX, 1.0, dgk_ref)
        for c in range(MIX // 128):
            dp_ref[:, 2 * MIX + 128 * c:2 * MIX + 128 * (c + 1)] = high_halves(two_heads(dkvf_ref, dkvs_ref, c)).astype(BF16)
        dp_ref[:, 3 * MIX:] = dfl_ref[...]

    col = lambda c: pl.BlockSpec((tm, MIX), lambda i: (i, c))
    heads = pl.BlockSpec((H_FOX, tm, 128), lambda i: (0, i, 0))
    vec = pl.BlockSpec((1, MIX), lambda i: (0, 0))
    return pl.pallas_call(
        body, grid=(t // tm,),
        in_specs=[col(0), col(1), heads, heads, heads, heads, pl.BlockSpec((tm, 128), lambda i: (i, 0)), vec, vec,
                  pl.BlockSpec((128, 128), lambda i: (0, 0))],
        out_specs=[pl.BlockSpec((tm, ATTN_IN_PAD), lambda i: (i, 0)), vec, vec],
        out_shape=[jax.ShapeDtypeStruct((t, ATTN_IN_PAD), BF16), jax.ShapeDtypeStruct((1, MIX), F32),
                   jax.ShapeDtypeStruct((1, MIX), F32)],
        compiler_params=_params(), name=name)(proj, proj, dq_f, dq_s, dkv_f, dkv_s, dfl, gq, gk, bd)


def gate_cumsum(logf3, tri, *, name):
    nc, r, _ = logf3.shape

    def body(x_ref, tri_ref, o_ref):
        tri_m = tri_ref[...]

        def step(c, carry):
            hi, mid, lo = _split3(x_ref[c])
            cs = _dot(hi, tri_m) + _dot(mid, tri_m) + _dot(lo, tri_m) + carry
            o_ref[c] = cs
            return cs[:, 127:128]

        lax.fori_loop(0, nc, step, jnp.zeros((r, 1), F32))

    return pl.pallas_call(body, out_shape=jax.ShapeDtypeStruct(logf3.shape, F32), compiler_params=_params(),
                          name=name)(logf3, tri)


def gate_cumsum_bwd(dcum3, logf3, tri, *, name):
    nc, r, _ = dcum3.shape

    def body(x_ref, lf_ref, tri_ref, o_ref, s_ref):
        tri_m = tri_ref[...]

        def step(n, carry):
            car, tot = carry
            c = nc - 1 - n
            hi, mid, lo = _split3(x_ref[c])
            cs = _dot(hi, tri_m) + _dot(mid, tri_m) + _dot(lo, tri_m) + car
            dl = cs * (1.0 - jnp.exp(lf_ref[c]))
            o_ref[c] = dl
            return cs[:, 0:1], tot + dl

        _, tot = lax.fori_loop(0, nc, step, (jnp.zeros((r, 1), F32), jnp.zeros((r, 128), F32)))
        s_ref[...] = jnp.broadcast_to(jnp.sum(tot, axis=1, keepdims=True), tot.shape)

    return pl.pallas_call(body, out_shape=[jax.ShapeDtypeStruct(dcum3.shape, F32), jax.ShapeDtypeStruct((r, 128), F32)],
                          compiler_params=_params(), name=name)(dcum3, logf3, tri)


def _causal_iota():
    row = lax.broadcasted_iota(jnp.int32, (BQ, BQ), 0)
    col = lax.broadcasted_iota(jnp.int32, (BQ, BQ), 1)
    return row, col


def _head_specs(nj, head0):
    qin = pl.BlockSpec((1, BQ, HEAD_DIM), lambda h, i: (h + head0, i, 0))
    kin = pl.BlockSpec((1, nj, BQ, HEAD_DIM), lambda h, i: (h + head0, 0, 0, 0))
    qin2 = pl.BlockSpec((1, BQ, 2 * HEAD_DIM), lambda h, i: (h + head0, i, 0))
    kin2 = pl.BlockSpec((1, nj, BQ, 2 * HEAD_DIM), lambda h, i: (h + head0, 0, 0, 0))
    qspec = pl.BlockSpec((1, BQ, HEAD_DIM), lambda h, i: (h, i, 0))
    kspec2 = pl.BlockSpec((1, nj, BQ, 2 * HEAD_DIM), lambda h, i: (h, 0, 0, 0))
    return qin, kin, qin2, kin2, qspec, kspec2


STOP = -105.0
STOP_WIDE = -115.0
FIXED_REF_MAX = 40.0


def _store_kmax(k_ref, kmax_ref, nj):
    def step(j, mx):
        kf = k_ref[0, j].astype(F32)
        return jnp.maximum(mx, jnp.max(jnp.sum(kf * kf, axis=1, keepdims=True), axis=0, keepdims=True))

    mx = lax.fori_loop(0, nj, step, jnp.zeros((1, 1), F32))
    kmax_ref[...] = jnp.broadcast_to(jnp.sqrt(mx), kmax_ref.shape)


def _qk_bound(q, kmax_ref):
    qf = q.astype(F32)
    return jnp.sqrt(jnp.sum(qf * qf, axis=1, keepdims=True)) * kmax_ref[0:1, 0:1] * 1.001


def _first_and_last_step():
    h, i = pl.program_id(0), pl.program_id(1)
    first = jnp.logical_and(h == 0, i == 0)
    last = jnp.logical_and(h == pl.num_programs(0) - 1, i == pl.num_programs(1) - 1)
    return first, last


def fox_fwd(qs, kn4, va4, fcol, frow4, *, name, gather=None):
    _, t, dh = qs.shape
    nh = H_FOX
    nj = t // BQ

    def body(*refs):
        if gather is None:
            q_ref, k_ref, v_ref, fc_ref, fr_ref, o_ref, lse_ref, kmax_ref = refs
        else:
            q_ref, k_ref, v_ref, fc_ref, fr_ref, src_ref, o_ref, lse_ref, dst_ref, kmax_ref = refs[:10]
            first_step, last_step = _first_and_last_step()

            @pl.when(first_step)
            def _():
                _chip_gather(src_ref, dst_ref, *refs[10:])[0]()

        i = pl.program_id(1)

        @pl.when(i == 0)
        def _():
            _store_kmax(k_ref, kmax_ref, nj)

        q = q_ref[0]
        fq = fc_ref[0]
        bound = _qk_bound(q, kmax_ref)
        row, col = _causal_iota()

        def gate_at_block_end(j):
            return fr_ref[0, j][:, BQ - 1:BQ]

        def pv(p, j):
            p_hi, p_lo = _split2(p)
            return _dot(p_hi, v_ref[0, j]) + _dot(p_lo, v_ref[0, j])

        def walk(block, live, init):
            carry = block(i, init, True)

            def cond(c):
                n, carry = c
                return jnp.logical_and(n < i, live(jnp.maximum(i - 1 - n, 0), carry))

            _, carry = lax.while_loop(cond, lambda c: (c[0] + 1, block(i - 1 - c[0], c[1], False)), (0, carry))
            return carry

        def fixed_reference(_):
            shift = fq - bound

            def probs(j, offset):
                return jnp.exp(_dot_nt(q, k_ref[0, j]) + (shift + offset) - fr_ref[0, j])

            def live(c):
                n, acc = c
                gate = gate_at_block_end(jnp.maximum(i - 1 - n, 0))
                return jnp.logical_and(n < i, jnp.max(fq - gate - jnp.log(acc[:, dh:dh + 1])) >= STOP_WIDE)

            def two_blocks(c):
                n, acc = c
                ja = i - 1 - n
                jb = i - 2 - n
                absent = jnp.where(jb >= 0, 0.0, NEG)
                jb = jnp.maximum(jb, 0)
                return n + 2, acc + (pv(probs(ja, 0.0), ja) + pv(probs(jb, absent), jb))

            acc = pv(jnp.where(col <= row, probs(i, 0.0), 0.0), i)
            _, acc = lax.while_loop(live, two_blocks, (0, acc))
            l = acc[:, dh:dh + 1]
            return acc[:, :dh] / l, bound + jnp.log(l)

        def running_maximum(_):
            def block(j, carry, diag):
                m, acc = carry
                s = _dot_nt(q, k_ref[0, j]) + fq - fr_ref[0, j]
                if diag:
                    s = jnp.where(col <= row, s, NEG)
                m_new = jnp.maximum(m, jnp.max(s, axis=1, keepdims=True))
                return m_new, jnp.exp(m - m_new) * acc + pv(jnp.exp(s - m_new), j)

            def live(j, carry):
                return jnp.max(bound + fq - gate_at_block_end(j) - carry[0]) >= STOP

            m, acc = walk(block, live, (jnp.full((BQ, 1), NEG, F32), jnp.zeros((BQ, 2 * dh), F32)))
            l = acc[:, dh:dh + 1]
            return acc[:, :dh] / l, m + jnp.log(l)

        o, lse = lax.cond(jnp.max(bound) < FIXED_REF_MAX, fixed_reference, running_maximum, 0)
        o_ref[0] = o
        lse_ref[0] = lse

        if gather is not None:
            @pl.when(last_step)
            def _():
                _chip_gather(src_ref, dst_ref, *refs[10:])[1]()

    qin, kin, _, kin2, qspec, _ = _head_specs(nj, 0)
    cspec = pl.BlockSpec((1, BQ, 1), lambda h, i: (h, i, 0))
    in_specs = [qin, kin, kin2, cspec, pl.BlockSpec((1, nj, 1, BQ), lambda h, i: (h, 0, 0, 0))]
    out_specs = [qspec, cspec]
    out_shape = [jax.ShapeDtypeStruct((nh, t, dh), F32), jax.ShapeDtypeStruct((nh, t, 1), F32)]
    scratch = [pltpu.VMEM((8, 128), F32)]
    args = [qs, kn4, va4, fcol, frow4]
    if gather is not None:
        in_specs.append(_ANY)
        out_specs.append(_ANY)
        out_shape.append(jax.ShapeDtypeStruct((N_CHIPS,) + gather.shape, gather.dtype))
        scratch += _chip_sems()
        args.append(gather)
    return pl.pallas_call(body, grid=(nh, nj), in_specs=in_specs, out_specs=out_specs, out_shape=out_shape,
                          scratch_shapes=scratch, compiler_params=_params(), name=name)(*args)


def fox_bwd(qs, kn4, v4, qa, doa, fcol, frow4, o, do, lse, *, name, scatter=None):
    _, t, dh = qs.shape
    nh = H_FOX
    nj = t // BQ

    def body(*refs):
        q_ref, k_ref, v_ref, qa_ref, doa_ref, fc_ref, fr_ref, o_ref, do_ref, lse_ref = refs[:10]
        if scatter is None:
            dq_ref, dkv_ref, dfk_ref, kmax_ref = refs[10:]
        else:
            g_ref, dq_ref, dkv_ref, dfk_ref, land_ref, kmax_ref = refs[10:16]
            first_step, last_step = _first_and_last_step()

            @pl.when(first_step)
            def _():
                _chip_scatter(g_ref, land_ref, *refs[16:])[0]()

        i = pl.program_id(1)

        @pl.when(i == 0)
        def _():
            dkv_ref[...] = jnp.zeros_like(dkv_ref)
            dfk_ref[...] = jnp.zeros_like(dfk_ref)
            _store_kmax(k_ref, kmax_ref, nj)

        q = q_ref[0]
        do_b = do_ref[0]
        fq = fc_ref[0]
        lse_q = lse_ref[0]
        dd = jnp.sum(do_b.astype(F32) * o_ref[0], axis=1, keepdims=True)
        rhs = jnp.concatenate([qa_ref[0], doa_ref[0]], axis=0)
        edge = _qk_bound(q, kmax_ref) + fq - lse_q

        def negligible(j):
            return jnp.logical_and(j < i, jnp.max(edge - fr_ref[0, j][:, BQ - 1:BQ]) < STOP_WIDE)

        first = lax.while_loop(negligible, lambda j: j + 1, 0)

        shift = fq - lse_q

        def block(j, offset, diag):
            k = k_ref[0, j]
            p = jnp.exp(_dot_nt(q, k) + (shift + offset) - fr_ref[0, j])
            if diag:
                row, col = _causal_iota()
                p = jnp.where(col <= row, p, 0.0)
            ds = p * (_dot_nt(do_b, v_ref[0, j]) - dd)
            ds_b = ds.astype(BF16)
            dkv_ref[0, j] += _dot_tn(jnp.concatenate([ds_b, p.astype(BF16)], axis=0), rhs)
            dfk_ref[0, j] -= jnp.sum(ds, axis=0, keepdims=True)
            return _dot(ds_b, k)

        def two_blocks(n, dq):
            ja = first + 2 * n
            jb = ja + 1
            absent = jnp.where(jb < i, 0.0, NEG)
            jb = jnp.minimum(jb, i - 1)
            return dq + (block(ja, 0.0, False) + block(jb, absent, False))

        dq = lax.fori_loop(0, (i - first + 1) // 2, two_blocks, jnp.zeros((BQ, dh), F32))
        dq_ref[0] = jnp.zeros((BQ, 2 * dh), F32)
        dq_ref[0, :, :dh] = dq + block(i, 0.0, True)

        if scatter is not None:
            @pl.when(last_step)
            def _():
                _chip_scatter(g_ref, land_ref, *refs[16:])[1]()

    qin, kin, qin2, _, qspec, kspec2 = _head_specs(nj, 0)
    cspec = pl.BlockSpec((1, BQ, 1), lambda h, i: (h, i, 0))
    rspec = pl.BlockSpec((1, nj, 1, BQ), lambda h, i: (h, 0, 0, 0))
    in_specs = [qin, kin, kin, qin2, qin2, cspec, rspec, qspec, qin, cspec]
    out_specs = [pl.BlockSpec((1, BQ, 2 * dh), lambda h, i: (h, i, 0)), kspec2, rspec]
    out_shape = [jax.ShapeDtypeStruct((nh, t, 2 * dh), F32), jax.ShapeDtypeStruct((nh, nj, BQ, 2 * dh), F32),
                 jax.ShapeDtypeStruct((nh, nj, 1, BQ), F32)]
    scratch = [pltpu.VMEM((8, 128), F32)]
    args = [qs, kn4, v4, qa, doa, fcol, frow4, o, do, lse]
    if scatter is not None:
        in_specs.append(_ANY)
        out_specs.append(_ANY)
        out_shape.append(jax.ShapeDtypeStruct(scatter.shape, scatter.dtype))
        scratch += _chip_sems()
        args.append(scatter)
    return pl.pallas_call(body, grid=(nh, nj), in_specs=in_specs, out_specs=out_specs, out_shape=out_shape,
                          scratch_shapes=scratch, compiler_params=_params(), name=name)(*args)


def _sb_logs(z, diag):
    e = jnp.exp(-jnp.abs(z))
    sp = jnp.log(1.0 + e)
    logb = jnp.minimum(z, 0.0) - sp
    lom = -jnp.maximum(z, 0.0) - sp
    strict = None
    if diag:
        row, col = _causal_iota()
        strict = col < row
        lom = jnp.where(strict, lom, 0.0)
    return logb, lom, e, strict


SB_GROUP = BQ // 2


def _sums_over_later_keys(lom, tri_m):
    halves = [lom[:, :SB_GROUP], lom[:, SB_GROUP:]]
    totals = [jnp.sum(x, axis=1, keepdims=True) for x in halves]
    within = []
    for x in halves:
        hi, lo = _split2(x)
        within.append(_dot(hi, tri_m) + _dot(lo, tri_m))
    return jnp.concatenate([within[0] + totals[1], within[1]], axis=1), totals[0] + totals[1]


def _sums_over_earlier_keys(da, tri_m):
    halves = [da[:, :SB_GROUP], da[:, SB_GROUP:]]
    totals = [jnp.sum(x, axis=1, keepdims=True) for x in halves]
    within = [_dot_nt(x.astype(BF16), tri_m) for x in halves]
    return jnp.concatenate([within[0], within[1] + totals[0]], axis=1), totals[0] + totals[1]


def sb_fwd(qs, kn4, va4, tri, *, name, gather=None):
    _, t, dh = qs.shape
    nh = H_SB
    nj = t // BQ
    assert nj <= 128

    def body(*refs):
        if gather is None:
            q_ref, k_ref, v_ref, tri_ref, o_ref, rs_ref = refs
        else:
            q_ref, k_ref, v_ref, tri_ref, src_ref, o_ref, rs_ref, dst_ref = refs[:8]
            first_step, last_step = _first_and_last_step()

            @pl.when(first_step)
            def _():
                _chip_gather(src_ref, dst_ref, *refs[8:])[0]()

        i = pl.program_id(1)
        q = q_ref[0]
        tri_m = tri_ref[...]
        lane = lax.broadcasted_iota(jnp.int32, (BQ, 128), 1)

        def block(j, carry, diag):
            run, acc, rall = carry
            logb, lom, _, strict = _sb_logs(_dot_nt(q, k_ref[0, j]), diag)
            later, total = _sums_over_later_keys(lom, tri_m)
            w = jnp.exp(logb + later + run)
            if diag:
                w = jnp.where(strict, w, 0.0)
            acc = acc + _dot(w.astype(BF16), v_ref[0, j])
            rall = jnp.where(lane == j, run, rall)
            return run + total, acc, rall

        init = (jnp.zeros((BQ, 1), F32), jnp.zeros((BQ, 2 * dh), F32), jnp.full((BQ, 128), NEG, F32))
        carry = block(i, init, True)

        def cond(c):
            n, carry = c
            return jnp.logical_and(n < i, jnp.max(carry[0]) >= STOP)

        _, (_, acc, rall) = lax.while_loop(cond, lambda c: (c[0] + 1, block(i - 1 - c[0], c[1], False)), (0, carry))
        o_ref[0] = acc[:, :dh].astype(BF16)
        rs_ref[0] = rall

        if gather is not None:
            @pl.when(last_step)
            def _():
                _chip_gather(src_ref, dst_ref, *refs[8:])[1]()

    qin, kin, _, kin2, qspec, _ = _head_specs(nj, H_FOX)
    rspec = pl.BlockSpec((1, BQ, 128), lambda h, i: (h, i, 0))
    in_specs = [qin, kin, kin2, pl.BlockSpec((SB_GROUP, SB_GROUP), lambda h, i: (0, 0))]
    out_specs = [qspec, rspec]
    out_shape = [jax.ShapeDtypeStruct((nh, t, dh), BF16), jax.ShapeDtypeStruct((nh, t, 128), F32)]
    scratch = []
    args = [qs, kn4, va4, tri]
    if gather is not None:
        in_specs.append(_ANY)
        out_specs.append(_ANY)
        out_shape.append(jax.ShapeDtypeStruct((N_CHIPS,) + gather.shape, gather.dtype))
        scratch += _chip_sems()
        args.append(gather)
    return pl.pallas_call(body, grid=(nh, nj), in_specs=in_specs, out_specs=out_specs, out_shape=out_shape,
                          scratch_shapes=scratch, compiler_params=_params(), name=name)(*args)


def sb_bwd(qs, kn4, v4, qa, doa, tri, do, rsave, *, name, scatter=None):
    _, t, dh = qs.shape
    nh = H_SB
    nj = t // BQ

    def body(*refs):
        q_ref, k_ref, v_ref, qa_ref, doa_ref, tri_ref, do_ref, rs_ref = refs[:8]
        if scatter is None:
            dq_ref, dkv_ref = refs[8:]
        else:
            g_ref, dq_ref, dkv_ref, land_ref = refs[8:12]
            first_step, last_step = _first_and_last_step()

            @pl.when(first_step)
            def _():
                _chip_scatter(g_ref, land_ref, *refs[12:])[0]()

        i = pl.program_id(1)

        @pl.when(i == 0)
        def _():
            dkv_ref[...] = jnp.zeros_like(dkv_ref)

        q = q_ref[0]
        do_b = do_ref[0]
        tri_m = tri_ref[...]
        rall = rs_ref[0]
        lane = lax.broadcasted_iota(jnp.int32, (BQ, 128), 1)
        rhs = jnp.concatenate([qa_ref[0], doa_ref[0]], axis=0)
        lane1 = lax.broadcasted_iota(jnp.int32, (1, 128), 1)
        unvisited = jnp.logical_and(lane1 < i, jnp.max(rall, axis=0, keepdims=True) < STOP)
        first = jnp.sum(unvisited.astype(jnp.int32))

        def block(j, carry, diag):
            dq, ecar = carry
            k = k_ref[0, j]
            z = _dot_nt(q, k)
            logb, lom, e, strict = _sb_logs(z, diag)
            run = jnp.sum(jnp.where(lane == j, rall, 0.0), axis=1, keepdims=True)
            w = jnp.exp(logb + _sums_over_later_keys(lom, tri_m)[0] + run)
            if diag:
                w = jnp.where(strict, w, 0.0)
            da = w * _dot_nt(do_b, v_ref[0, j])
            earlier, da_total = _sums_over_earlier_keys(da, tri_m)
            before = earlier + ecar
            inv = 1.0 / (1.0 + e)
            beta = jnp.where(z >= 0.0, 1.0, e) * inv
            one_minus = jnp.where(z >= 0.0, e, 1.0) * inv
            dz = da * one_minus - before * beta
            if diag:
                dz = jnp.where(strict, dz, 0.0)
            dz_b = dz.astype(BF16)
            dkv_ref[0, j] += _dot_tn(jnp.concatenate([dz_b, w.astype(BF16)], axis=0), rhs)
            return dq + _dot(dz_b, k), ecar + da_total

        carry = lax.fori_loop(first, i, lambda j, c: block(j, c, False),
                              (jnp.zeros((BQ, dh), F32), jnp.zeros((BQ, 1), F32)))
        dq, _ = block(i, carry, True)
        dq_ref[0] = jnp.zeros((BQ, 2 * dh), F32)
        dq_ref[0, :, :dh] = dq

        if scatter is not None:
            @pl.when(last_step)
            def _():
                _chip_scatter(g_ref, land_ref, *refs[12:])[1]()

    qin, kin, qin2, _, qspec, kspec2 = _head_specs(nj, H_FOX)
    in_specs = [qin, kin, kin, qin2, qin2, pl.BlockSpec((SB_GROUP, SB_GROUP), lambda h, i: (0, 0)), qin,
                pl.BlockSpec((1, BQ, 128), lambda h, i: (h, i, 0))]
    out_specs = [pl.BlockSpec((1, BQ, 2 * dh), lambda h, i: (h, i, 0)), kspec2]
    out_shape = [jax.ShapeDtypeStruct((nh, t, 2 * dh), F32), jax.ShapeDtypeStruct((nh, nj, BQ, 2 * dh), F32)]
    scratch = []
    args = [qs, kn4, v4, qa, doa, tri, do, rsave]
    if scatter is not None:
        in_specs.append(_ANY)
        out_specs.append(_ANY)
        out_shape.append(jax.ShapeDtypeStruct(scatter.shape, scatter.dtype))
        scratch += _chip_sems()
        args.append(scatter)
    return pl.pallas_call(body, grid=(nh, nj), in_specs=in_specs, out_specs=out_specs, out_shape=out_shape,
                          scratch_shapes=scratch, compiler_params=_params(), name=name)(*args)


def loss_head(y, target, *, name, tm=512):
    t, d = y.shape

    def body(y_ref, t_ref, l_ref, dy_ref, acc_ref):
        i = pl.program_id(0)
        diff = y_ref[...] - t_ref[...]
        dy_ref[...] = diff * (1.0 / d)
        part = jnp.sum(diff * diff, axis=0, keepdims=True)

        @pl.when(i == 0)
        def _():
            acc_ref[...] = part

        @pl.when(i > 0)
        def _():
            acc_ref[...] += part

        @pl.when(i == pl.num_programs(0) - 1)
        def _():
            l_ref[...] = jnp.full(l_ref.shape, (0.5 / d) * jnp.sum(acc_ref[...]), F32)

    row = pl.BlockSpec((tm, d), lambda i: (i, 0))
    return pl.pallas_call(
        body, grid=(t // tm,), in_specs=[row, row],
        out_specs=[pl.BlockSpec((8, 128), lambda i: (0, 0)), row],
        out_shape=[jax.ShapeDtypeStruct((8, 128), F32), jax.ShapeDtypeStruct((t, d), F32)],
        scratch_shapes=[pltpu.VMEM((1, d), F32)], compiler_params=_params(), name=name)(y, target)


def _to_heads(a):
    t = a.shape[0]
    return a.reshape(t, N_HEADS, HEAD_DIM).transpose(1, 0, 2)


def _from_heads(a):
    t = a.shape[1]
    return a.transpose(1, 0, 2).reshape(t, MIX)


def _lanes_to_chunks(a):
    r, t = a.shape
    return a.reshape(r, t // 128, 128).transpose(1, 0, 2)


def _chunks_to_lanes(a):
    nc, r, _ = a.shape
    return a.transpose(1, 0, 2).reshape(r, nc * 128)


def _constants():
    idx = jnp.arange(128)
    bd = (idx[:, None] // HEAD_DIM == idx[None, :] // HEAD_DIM).astype(BF16)
    tri_le = (idx[:, None] <= idx[None, :]).astype(BF16)
    tri_ge = (idx[:, None] >= idx[None, :]).astype(BF16)
    jdx = jnp.arange(SB_GROUP)
    tri_gt = (jdx[:, None] > jdx[None, :]).astype(BF16)
    return dict(bd=bd, tri_le=tri_le, tri_ge=tri_ge, tri_gt=tri_gt)


def attn_layer_fwd(h, w, cst, gather=None):
    t = h.shape[0]
    nj = t // BQ
    xn, proj = rms_mm_nn(h, w["norm"], w["w_in"], tn=640, name="attn_in_proj")
    qs, kn, vb, logf = attn_prep_fwd(proj, w["gq"], w["gk"], w["fbias"], cst["bd"], name="attn_prep_fwd")
    logf3 = _lanes_to_chunks(logf)
    cum = _chunks_to_lanes(gate_cumsum(logf3, cst["tri_le"], name="gate_cumsum"))
    fcol = cum.reshape(H_FOX, t, 1)
    frow4 = cum.reshape(H_FOX, nj, 1, BQ)
    qh = _to_heads(qs)
    kh4 = _to_heads(kn).reshape(N_HEADS, nj, BQ, HEAD_DIM)
    vh4 = _to_heads(vb).reshape(N_HEADS, nj, BQ, HEAD_DIM)
    ones = jnp.ones(vh4.shape[:-1] + (1,), BF16)
    va4 = jnp.concatenate([vh4, ones, jnp.zeros(vh4.shape[:-1] + (HEAD_DIM - 1,), BF16)], axis=-1)
    if gather is None:
        (o_f, lse), (o_s, rsave), gathered = (fox_fwd(qh, kh4, va4, fcol, frow4, name="fox_fwd"),
                                              sb_fwd(qh, kh4, va4, cst["tri_gt"], name="sb_fwd"), None)
    else:
        o_f, lse, gathered_a = fox_fwd(qh, kh4, va4, fcol, frow4, name="fox_fwd_gather", gather=gather[0])
        o_s, rsave, gathered_b = sb_fwd(qh, kh4, va4, cst["tri_gt"], name="sb_fwd_gather", gather=gather[1])
        gathered = gather[2]((gathered_a, gathered_b))
        w = gathered[0][0]
    o = _from_heads(jnp.concatenate([o_f.astype(BF16), o_s], axis=0))
    h2 = mm_nn(o, w["w_out"], add=h, name="mix_out_proj")
    saved = dict(h=h, xn=xn, proj=proj, logf3=logf3, fcol=fcol, frow4=frow4, qh=qh, kh4=kh4, vh4=vh4,
                 o_f=o_f, lse=lse, rsave=rsave, o=o)
    return h2, saved, gathered


def attn_layer_bwd(dh, w, s, cst, scatter=None):
    t = dh.shape[0]
    dh3 = dh[None]
    do = mm_nt(dh3, w["w_out"], out_dtype=BF16, name="mix_out_bwd_bf16")
    g_w_out = mm_tn(s["o"], dh3, name="mix_out_wgrad")
    doh = _to_heads(do)
    zeros = jnp.zeros_like(doh)
    qa = jnp.concatenate([s["qh"], zeros], axis=-1)
    doa = jnp.concatenate([zeros, doh], axis=-1)
    fox_args = (s["qh"], s["kh4"], s["vh4"], qa, doa, s["fcol"], s["frow4"], s["o_f"], doh, s["lse"])
    sb_args = (s["qh"], s["kh4"], s["vh4"], qa, doa, cst["tri_gt"], doh, s["rsave"])
    if scatter is None:
        (dq_f, dkv_f, dfk), (dq_s, dkv_s), landed = fox_bwd(*fox_args, name="fox_bwd"), sb_bwd(*sb_args, name="sb_bwd"), None
    else:
        chunks_a, chunks_b = scatter(g_w_out)
        dq_f, dkv_f, dfk, landed_a = fox_bwd(*fox_args, name="fox_bwd_scatter", scatter=chunks_a)
        dq_s, dkv_s, landed_b = sb_bwd(*sb_args, name="sb_bwd_scatter", scatter=chunks_b)
        landed = (landed_a, landed_b)
    dcum3 = _lanes_to_chunks(dfk.reshape(H_FOX, t))
    dfl3, dbias = gate_cumsum_bwd(dcum3, s["logf3"], cst["tri_ge"], name="gate_cumsum_bwd")
    dfl = jnp.pad(_chunks_to_lanes(dfl3).T, ((0, 0), (0, 128 - H_FOX))).astype(BF16)
    wide = (H_FOX, t, 2 * HEAD_DIM)
    dproj, dgq, dgk = attn_prep_bwd(s["proj"], dq_f, dq_s, dkv_f.reshape(wide), dkv_s.reshape(wide), dfl, w["gq"],
                                    w["gk"], cst["bd"], name="attn_prep_bwd")
    g_w_in = mm_tn(s["xn"], dproj[None], tn=640, name="attn_in_wgrad")[:, :ATTN_IN]
    dh2, g_norm = mm_nt_rms_bwd(dproj[None], w["w_in"], s["h"], w["norm"], dh, name="attn_in_bwd")
    dgq = dgq.reshape(N_HEADS, HEAD_DIM)
    dgk = dgk.reshape(N_HEADS, HEAD_DIM)
    grads = dict(norm=g_norm[0], w_in=g_w_in, f_bias=dbias[:, 0], fox_q=dgq[:H_FOX].sum(0), fox_k=dgk[:H_FOX].sum(0),
                 sb_q=dgq[H_FOX:].sum(0), sb_k=dgk[H_FOX:].sum(0), w_out=g_w_out)
    return dh2, grads, landed


def conv_layer_fwd(h, w):
    xn, proj3 = rms_mm_nn(h, w["norm"], w["w_in"], parts=3, name="conv_in_proj")
    y = conv_mix_fwd(proj3, w["ck"], name="conv_mix_fwd")
    h2 = mm_nn(y, w["w_out"], add=h, name="mix_out_proj")
    return h2, dict(h=h, xn=xn, proj3=proj3, y=y)


def conv_layer_bwd(dh, w, s):
    dh3 = dh[None]
    g_w_out = mm_tn(s["y"], dh3, name="mix_out_wgrad")
    dproj3, dck = conv_mix_bwd(dh, w["w_out"], s["proj3"], w["ck"], name="conv_mix_bwd")
    g_w_in = mm_tn(s["xn"], dproj3, name="conv_in_wgrad")
    dh2, g_norm = mm_nt_rms_bwd(dproj3, w["w_in"], s["h"], w["norm"], dh, name="conv_in_bwd")
    return dh2, dict(norm=g_norm[0], w_in=g_w_in, ck=dck[0, :3], w_out=g_w_out)


def ffn_layer_fwd(h, w):
    xn, up2 = rms_mm_nn(h, w["norm"], w["w_up"], parts=2, tn=1408, name="ffn_up_proj")
    h2, act = ffn_act_down_fwd(up2, w["cw2"], w["w_down"], h, name="ffn_act_down_fwd")
    return h2, dict(h=h, xn=xn, up2=up2, act=act)


def ffn_layer_bwd(dh, w, s):
    dh3 = dh[None]
    g_w_down = mm_tn(s["act"], dh3, tk=1408, name="ffn_down_wgrad")
    dup2, dcw = ffn_act_bwd(dh, w["w_down"], s["up2"], w["cw2"], name="ffn_act_bwd")
    g_w_up = mm_tn(s["xn"], dup2, tn=1408, name="ffn_up_wgrad")
    dh2, g_norm = mm_nt_rms_bwd(dup2, w["w_up"], s["h"], w["norm"], dh, name="ffn_up_bwd")
    g_cw = jnp.concatenate([dcw[0, :3], dcw[1, :3]], axis=1)
    return dh2, dict(norm=g_norm[0], w_up=g_w_up, cw=g_cw, w_down=g_w_down)


def forward_backward(x, target, wa, wc, wf, *, late_weights=None, late_chunks=None):
    cst = _constants()
    h = x
    saved = []
    layer = 0
    while layer == 0 or layer < len(wf):
        i = layer // 2
        if layer % 2 == 0:
            h, sm, built = attn_layer_fwd(h, wa[i], cst, gather=late_weights if late_weights and layer == 0 else None)
            if built is not None:
                wa, wc, wf = built
        else:
            h, sm = conv_layer_fwd(h, wc[i])
        h, sf = ffn_layer_fwd(h, wf[layer])
        saved.append((sm, sf))
        layer += 1
    depth = len(wf)
    loss_blk, dh = loss_head(h, target, name="loss_head")
    ga, gc, gf = [None] * len(wa), [None] * len(wc), [None] * depth
    landed = None
    for layer in reversed(range(depth)):
        i = layer // 2
        sm, sf = saved[layer]
        dh, gf[layer] = ffn_layer_bwd(dh, wf[layer], sf)
        if layer % 2 == 0:
            chunks = None
            if late_chunks and layer == 0:
                chunks = lambda g_w_out: late_chunks([dict(w_out=g_w_out)] + ga[1:], gc, gf)
            dh, ga[i], got = attn_layer_bwd(dh, wa[i], sm, cst, scatter=chunks)
            landed = got if got is not None else landed
        else:
            dh, gc[i] = conv_layer_bwd(dh, wc[i], sm)
    return loss_blk, dh, ga, gc, gf, landed


def _part_rows(shape, width, row_mult):
    n = 1
    for s in shape:
        n *= s
    rows = -(-n // width)
    return -(-rows // row_mult) * row_mult


def _pack_rows(arrs, width, row_mult, dtype, total_rows=None):
    parts = []
    used = 0
    for a in arrs:
        rows = _part_rows(a.shape, width, row_mult)
        flat = a.astype(dtype).reshape(-1)
        flat = jnp.pad(flat, (0, rows * width - flat.shape[0]))
        parts.append(flat.reshape(rows, width))
        used += rows
    if total_rows is not None and total_rows > used:
        parts.append(jnp.zeros((total_rows - used, width), dtype))
    return jnp.concatenate(parts, axis=0)


def _unpack_rows(packed, shapes, width, row_mult):
    out = []
    off = 0
    for shape in shapes:
        rows = _part_rows(shape, width, row_mult)
        n = 1
        for s in shape:
            n *= s
        out.append(packed[off:off + rows].reshape(-1)[:n].reshape(shape))
        off += rows
    return out


BIG_NAMES = ("attn_w_in", "attn_w_out", "conv_w_in", "conv_w_out", "ffn_w_up", "ffn_w_down")
BIG_AXIS = {"attn_w_in": 2, "attn_w_out": 1, "conv_w_in": 2, "conv_w_out": 1, "ffn_w_up": 2, "ffn_w_down": 1}
BIG_WIDTH = 1024
BIG_ROW_MULT = 16
BIG_TILE = 512
SMALL_TILE = 128
SMALL_SHARDED = ("conv_norm", "conv_kernel", "ffn_conv")
SMALL_AXIS = {"conv_norm": 1, "conv_kernel": 2, "ffn_conv": 2}
SMALL_REPLICATED = ("attn_norm", "attn_f_bias", "fox_q_gain", "fox_k_gain", "sb_q_gain", "sb_k_gain", "ffn_norm")
WEIGHT_ORDER = ("attn_norm", "attn_w_in", "attn_f_bias", "fox_q_gain", "fox_k_gain", "sb_q_gain", "sb_k_gain",
                "attn_w_out", "conv_norm", "conv_w_in", "conv_kernel", "conv_w_out", "ffn_norm", "ffn_w_up",
                "ffn_conv", "ffn_w_down")


def _big_total_rows(shapes):
    used = sum(_part_rows(s, BIG_WIDTH, BIG_ROW_MULT) for s in shapes)
    tile = BIG_TILE if used >= 8 * BIG_TILE else SMALL_TILE
    return -(-used // tile) * tile


def _place():
    x, y, c = lax.axis_index("x"), lax.axis_index("y"), lax.axis_index("c")
    other_chips = [(1 - x, y), (x, 1 - y), (1 - x, 1 - y)]
    return x, y, c, other_chips


_ANY = pl.BlockSpec(memory_space=pl.ANY)


def _chip_sems():
    return [pltpu.SemaphoreType.DMA((3,)), pltpu.SemaphoreType.DMA((3,)), pltpu.SemaphoreType.DMA]


def _chip_gather(src_ref, dst_ref, send_sems, recv_sems, local_sem):
    x, y, c, chips = _place()
    k = 2 * x + y

    def copy(j, slot):
        px, py = chips[j]
        return pltpu.make_async_remote_copy(src_ref=src_ref, dst_ref=dst_ref.at[slot], send_sem=send_sems.at[j],
                                            recv_sem=recv_sems.at[j], device_id=(px, py, c), device_id_type=MESH)

    def local():
        return pltpu.make_async_copy(src_ref, dst_ref.at[k], local_sem)

    def start():
        local().start()
        for j in range(3):
            copy(j, k).start()

    def finish():
        for j, (px, py) in enumerate(chips):
            copy(j, 2 * px + py).wait_recv()
        for j in range(3):
            copy(j, k).wait_send()
        local().wait()

    return start, finish


def _chip_scatter(g_ref, o_ref, send_sems, recv_sems, local_sem):
    x, y, c, chips = _place()
    k = 2 * x + y

    def copy(j, src_slot, dst_slot):
        px, py = chips[j]
        return pltpu.make_async_remote_copy(src_ref=g_ref.at[src_slot], dst_ref=o_ref.at[dst_slot],
                                            send_sem=send_sems.at[j], recv_sem=recv_sems.at[j],
                                            device_id=(px, py, c), device_id_type=MESH)

    def local():
        return pltpu.make_async_copy(g_ref.at[k], o_ref.at[k], local_sem)

    def start():
        local().start()
        for j, (px, py) in enumerate(chips):
            copy(j, 2 * px + py, k).start()

    def finish():
        for j, (px, py) in enumerate(chips):
            copy(j, k, 2 * px + py).wait_recv()
        for j, (px, py) in enumerate(chips):
            copy(j, 2 * px + py, k).wait_send()
        local().wait()

    return start, finish


def gather_chips(arrs, *, name):
    n = len(arrs)

    def body(*refs):
        hooks = [_chip_gather(refs[m], refs[n + m], *refs[2 * n + 3 * m:2 * n + 3 * m + 3]) for m in range(n)]
        for start, _ in hooks:
            start()
        for _, finish in hooks:
            finish()

    return pl.pallas_call(
        body, in_specs=[_ANY] * n, out_specs=[_ANY] * n,
        out_shape=[jax.ShapeDtypeStruct((N_CHIPS,) + a.shape, a.dtype) for a in arrs],
        scratch_shapes=_chip_sems() * n, name=name)(*arrs)


def scatter_chips(chunks, *, name):
    def body(g_ref, o_ref, send_sems, recv_sems, local_sem):
        start, finish = _chip_scatter(g_ref, o_ref, send_sems, recv_sems, local_sem)
        start()
        finish()

    return pl.pallas_call(
        body, in_specs=[_ANY], out_specs=_ANY, out_shape=jax.ShapeDtypeStruct(chunks.shape, chunks.dtype),
        scratch_shapes=_chip_sems(), name=name)(chunks)


def swap_cores(arrs, *, name):
    n = len(arrs)

    def body(*refs):
        x, y, c, _ = _place()
        copies = [pltpu.make_async_remote_copy(src_ref=refs[m], dst_ref=refs[n + m], send_sem=refs[2 * n + 2 * m],
                                               recv_sem=refs[2 * n + 2 * m + 1], device_id=(x, y, 1 - c),
                                               device_id_type=MESH) for m in range(n)]
        for cp in copies:
            cp.start()
        for cp in copies:
            cp.wait()

    return pl.pallas_call(
        body, in_specs=[_ANY] * n, out_specs=[_ANY] * n,
        out_shape=[jax.ShapeDtypeStruct(a.shape, a.dtype) for a in arrs],
        scratch_shapes=[pltpu.SemaphoreType.DMA, pltpu.SemaphoreType.DMA] * n, name=name)(*arrs)


def allreduce_small(p, *, name):
    r, w = p.shape

    def body(p_ref, o_ref, buf, send_sems, recv_sems):
        x, y, c, _ = _place()
        me = 4 * x + 2 * y + c
        buf[me] = p_ref[...]

        def peer_of(m):
            return (1 - x if m & 4 else x, 1 - y if m & 2 else y, 1 - c if m & 1 else c)

        def copy(m, slot):
            return pltpu.make_async_remote_copy(src_ref=p_ref, dst_ref=buf.at[slot], send_sem=send_sems.at[m - 1],
                                                recv_sem=recv_sems.at[m - 1], device_id=peer_of(m),
                                                device_id_type=MESH)

        sends = [copy(m, me) for m in range(1, 8)]
        for cp in sends:
            cp.start()
        for m in range(1, 8):
            px, py, pc = peer_of(m)
            copy(m, 4 * px + 2 * py + pc).wait_recv()
        for cp in sends:
            cp.wait_send()
        acc = buf[0]
        for d in range(1, 8):
            acc = acc + buf[d]
        o_ref[...] = acc

    vm = pl.BlockSpec(memory_space=pltpu.VMEM)
    return pl.pallas_call(
        body, in_specs=[vm], out_specs=vm, out_shape=jax.ShapeDtypeStruct((r, w), F32),
        scratch_shapes=[pltpu.VMEM((8, r, w), F32), pltpu.SemaphoreType.DMA((7,)), pltpu.SemaphoreType.DMA((7,))],
        name=name)(p)


def sum_chips(rv, *, name):
    _, r, w = rv.shape
    tile = BIG_TILE if r % BIG_TILE == 0 else SMALL_TILE

    def body(a_ref, b_ref, c_ref, d_ref, o_ref):
        o_ref[...] = ((a_ref[0].astype(F32) + b_ref[0].astype(F32)) + c_ref[0].astype(F32)) + d_ref[0].astype(F32)

    spec = lambda kk: pl.BlockSpec((1, tile, w), lambda i: (kk, i, 0))
    return pl.pallas_call(
        body, grid=(r // tile,), in_specs=[spec(0), spec(1), spec(2), spec(3)],
        out_specs=pl.BlockSpec((tile, w), lambda i: (i, 0)), out_shape=jax.ShapeDtypeStruct((r, w), F32),
        compiler_params=_params(), name=name)(rv, rv, rv, rv)


def add_pair(a, b, *, name):
    r, w = a.shape
    tile = BIG_TILE if r % BIG_TILE == 0 else SMALL_TILE

    def body(a_ref, b_ref, o_ref):
        o_ref[...] = a_ref[...] + b_ref[...]

    spec = pl.BlockSpec((tile, w), lambda i: (i, 0))
    return pl.pallas_call(body, grid=(r // tile,), in_specs=[spec, spec], out_specs=spec,
                          out_shape=jax.ShapeDtypeStruct((r, w), F32), compiler_params=_params(), name=name)(a, b)


def adamw(w, g, m, v, *, tm, name):
    r, c = w.shape
    assert r % tm == 0

    def body(w_ref, g_ref, m_ref, v_ref, d_ref, nm_ref, nv_ref):
        g_ = g_ref[...]
        m_ = ADAM_B1 * m_ref[...] + (1.0 - ADAM_B1) * g_
        v_ = ADAM_B2 * v_ref[...] + (1.0 - ADAM_B2) * (g_ * g_)
        m_hat = m_ / (1.0 - ADAM_B1 ** ADAM_STEP)
        v_hat = v_ / (1.0 - ADAM_B2 ** ADAM_STEP)
        d_ref[...] = -ADAM_LR * (m_hat / (jnp.sqrt(v_hat) + ADAM_EPS) + ADAM_WD * w_ref[...])
        nm_ref[...] = m_
        nv_ref[...] = v_

    spec = pl.BlockSpec((tm, c), lambda i: (i, 0))
    return pl.pallas_call(body, grid=(r // tm,), in_specs=[spec] * 4, out_specs=[spec] * 3,
                          out_shape=[jax.ShapeDtypeStruct((r, c), F32)] * 3, compiler_params=_params(), name=name)(w, g, m, v)


def kernel(x, attn_norm, attn_w_in, attn_f_bias, fox_q_gain, fox_k_gain, sb_q_gain, sb_k_gain, attn_w_out, conv_norm, conv_w_in, conv_kernel, conv_w_out, ffn_norm, ffn_w_up, ffn_conv, ffn_w_down, loss_target, m_attn_norm, m_attn_w_in, m_attn_f_bias, m_fox_q_gain, m_fox_k_gain, m_sb_q_gain, m_sb_k_gain, m_attn_w_out, m_conv_norm, m_conv_w_in, m_conv_kernel, m_conv_w_out, m_ffn_norm, m_ffn_w_up, m_ffn_conv, m_ffn_w_down, v_attn_norm, v_attn_w_in, v_attn_f_bias, v_fox_q_gain, v_fox_k_gain, v_sb_q_gain, v_sb_k_gain, v_attn_w_out, v_conv_norm, v_conv_w_in, v_conv_kernel, v_conv_w_out, v_ffn_norm, v_ffn_w_up, v_ffn_conv, v_ffn_w_down):
    a = dict(locals())
    chip = 2 * lax.axis_index("x") + lax.axis_index("y")
    n_attn, n_conv, depth = attn_norm.shape[0], conv_norm.shape[0], ffn_norm.shape[0]

    units = [(name, l) for name in BIG_NAMES for l in range(a[name].shape[0])]
    early = [("attn_w_in", 0)]
    late = [u for u in units if u not in early]
    late_b = [("attn_w_out", 0), ("conv_w_in", n_conv - 1), ("conv_w_out", n_conv - 1), ("ffn_w_up", depth - 1),
              ("ffn_w_down", depth - 1)]
    late_a = [u for u in late if u not in late_b]
    late_sb = [("ffn_w_up", 0), ("ffn_w_down", 0), ("ffn_w_up", 1), ("ffn_w_down", 1)]
    late_sa = [u for u in late if u not in late_sb]

    def unit_shape(u):
        return a[u[0]].shape[1:]

    def pack_units(us, get):
        return _pack_rows([get(u) for u in us], BIG_WIDTH, BIG_ROW_MULT, BF16, _big_total_rows([unit_shape(u) for u in us]))

    def unpack_units(packed, us):
        return dict(zip(us, _unpack_rows(packed, [unit_shape(u) for u in us], BIG_WIDTH, BIG_ROW_MULT)))

    def full_units(gathered, us):
        per_chip = [unpack_units(gathered[kk], us) for kk in range(N_CHIPS)]
        return {u: jnp.concatenate([per_chip[kk][u] for kk in range(N_CHIPS)], axis=BIG_AXIS[u[0]] - 1) for u in us}

    def shard(u):
        return a[u[0]][u[1]]

    small_shapes = [a[n].shape for n in SMALL_SHARDED]
    packed_s = _pack_rows([a[n] for n in SMALL_SHARDED], 128, 8, F32)
    gath_e, gath_s = gather_chips([pack_units(early, shard), packed_s], name="gather_weights")
    full_e = full_units(gath_e, early)
    full = {}
    per_chip = [_unpack_rows(gath_s[kk], small_shapes, 128, 8) for kk in range(N_CHIPS)]
    for n, name in enumerate(SMALL_SHARDED):
        full[name] = jnp.concatenate([per_chip[kk][n] for kk in range(N_CHIPS)], axis=SMALL_AXIS[name])

    def attn_weights(i, fu):
        return dict(
            norm=attn_norm[i][None],
            w_in=jnp.pad(fu[("attn_w_in", i)], ((0, 0), (0, ATTN_IN_PAD - ATTN_IN))),
            fbias=jnp.pad(attn_f_bias[i], (0, 128 - H_FOX))[None],
            gq=jnp.concatenate([jnp.tile(fox_q_gain[i], H_FOX), jnp.tile(sb_q_gain[i], H_SB)])[None],
            gk=jnp.concatenate([jnp.tile(fox_k_gain[i], H_FOX), jnp.tile(sb_k_gain[i], H_SB)])[None],
            w_out=fu.get(("attn_w_out", i)))

    def build_weights(gathered):
        fu = {**full_e, **full_units(gathered[0], late_a), **full_units(gathered[1], late_b)}
        wa = [attn_weights(i, fu) for i in range(n_attn)]
        wc = [dict(norm=full["conv_norm"][i][None], w_in=fu[("conv_w_in", i)], ck=full["conv_kernel"][i][None],
                   w_out=fu[("conv_w_out", i)]) for i in range(n_conv)]
        wf = []
        for l in range(depth):
            cw = full["ffn_conv"][l]
            wf.append(dict(norm=ffn_norm[l][None], w_up=fu[("ffn_w_up", l)], cw2=jnp.stack([cw[:, :D_FF], cw[:, D_FF:]]),
                           w_down=fu[("ffn_w_down", l)]))
        return wa, wc, wf

    def chunk_of(u, kk, ga, gc, gf):
        name, l = u
        g = {"attn_w_in": lambda: ga[l]["w_in"], "attn_w_out": lambda: ga[l]["w_out"],
             "conv_w_in": lambda: gc[l]["w_in"], "conv_w_out": lambda: gc[l]["w_out"],
             "ffn_w_up": lambda: gf[l]["w_up"], "ffn_w_down": lambda: gf[l]["w_down"]}[name]()
        width = a[name].shape[BIG_AXIS[name]]
        return lax.slice_in_dim(g, kk * width, (kk + 1) * width, axis=BIG_AXIS[name] - 1)

    def chunks_of(us, ga, gc, gf):
        return jnp.stack([pack_units(us, lambda u: chunk_of(u, kk, ga, gc, gf)) for kk in range(N_CHIPS)])

    loss_blk, grad_x, ga, gc, gf, landed_late = forward_backward(
        x[0], loss_target[0], [attn_weights(0, full_e)], [], [],
        late_weights=(pack_units(late_a, shard), pack_units(late_b, shard), build_weights),
        late_chunks=lambda ga, gc, gf: (chunks_of(late_sa, ga, gc, gf), chunks_of(late_sb, ga, gc, gf)))

    landed = [scatter_chips(chunks_of(early, ga, gc, gf), name="scatter_grads"), landed_late[0], landed_late[1]]
    mine = [sum_chips(buf, name="sum_chips") for buf in landed]
    theirs = swap_cores(mine, name="swap_cores")
    g_units = {}
    for us, m, th in zip((early, late_sa, late_sb), mine, theirs):
        g_units.update(unpack_units(add_pair(m, th, name="add_cores"), us))
    grads = {name: jnp.stack([g_units[(name, l)] for l in range(a[name].shape[0])]) for name in BIG_NAMES}

    small_full = [
        loss_blk,
        jnp.stack([g["norm"] for g in ga]), jnp.stack([g["f_bias"] for g in ga]),
        jnp.stack([g["fox_q"] for g in ga]), jnp.stack([g["fox_k"] for g in ga]),
        jnp.stack([g["sb_q"] for g in ga]), jnp.stack([g["sb_k"] for g in ga]),
        jnp.stack([g["norm"] for g in gf]),
        jnp.stack([g["norm"] for g in gc]), jnp.stack([g["ck"] for g in gc]), jnp.stack([g["cw"] for g in gf]),
    ]
    summed = allreduce_small(_pack_rows(small_full, 128, 8, F32), name="allreduce_small")
    parts = _unpack_rows(summed, [p.shape for p in small_full], 128, 8)
    loss = parts[0][0, 0]
    for name, g in zip(SMALL_REPLICATED, parts[1:8]):
        grads[name] = g
    for name, g in zip(SMALL_SHARDED, parts[8:]):
        width = a[name].shape[SMALL_AXIS[name]]
        grads[name] = lax.dynamic_slice_in_dim(g, chip * width, width, axis=SMALL_AXIS[name])

    delta, new_m, new_v = {}, {}, {}
    for name in BIG_NAMES:
        shape = a[name].shape
        flat = lambda arr: arr.reshape(-1, shape[-1])
        d_, m_, v_ = adamw(flat(a[name]), flat(grads[name]), flat(a["m_" + name]), flat(a["v_" + name]), tm=256,
                           name="adamw")
        delta[name], new_m[name], new_v[name] = d_.reshape(shape), m_.reshape(shape), v_.reshape(shape)
    small_names = SMALL_REPLICATED + SMALL_SHARDED
    small_shapes_local = [a[n].shape for n in small_names]
    pack = lambda prefix, src: _pack_rows([src[prefix + n] for n in small_names], 128, 8, F32)
    packed = adamw(pack("", a), pack("", grads), pack("m_", a), pack("v_", a), tm=8, name="adamw_small")
    for store, buf in zip((delta, new_m, new_v), packed):
        for name, arr in zip(small_names, _unpack_rows(buf, small_shapes_local, 128, 8)):
            store[name] = arr

    return (loss, grad_x[None], *[grads[n] for n in WEIGHT_ORDER], *[delta[n] for n in WEIGHT_ORDER],
            *[new_m[n] for n in WEIGHT_ORDER], *[new_v[n] for n in WEIGHT_ORDER])
```

```python
import functools

import jax
import jax.numpy as jnp
from jax import lax
from jax.experimental import pallas as pl
from jax.experimental.pallas import tpu as pltpu

F32 = jnp.float32
BF16 = jnp.bfloat16

D_MODEL = 1024
HEAD_DIM = 64
H_FOX = 8
H_SB = 8
N_HEADS = H_FOX + H_SB
MIX = N_HEADS * HEAD_DIM
ATTN_IN = 3 * MIX + H_FOX
ATTN_IN_PAD = 3 * MIX + 128
D_FF = 2816
EPS = 1e-6
SCALE = HEAD_DIM ** -0.5
NEG = -1e30

ADAM_LR = 0.001
ADAM_B1 = 0.9
ADAM_B2 = 0.999
ADAM_EPS = 1e-08
ADAM_WD = 0.01
ADAM_STEP = 10

VMEM_LIMIT = 56 * 1024 * 1024
HALO = 8
BQ = 512
N_CHIPS = 4
MESH = pl.DeviceIdType.MESH


def _params(**kw):
    return pltpu.CompilerParams(vmem_limit_bytes=VMEM_LIMIT, **kw)


def _dot(a, b):
    return jnp.dot(a, b, preferred_element_type=F32)


def _dot_nt(a, b):
    return lax.dot_general(a, b, (((1,), (1,)), ((), ())), preferred_element_type=F32)


def _dot_tn(a, b):
    return lax.dot_general(a, b, (((0,), (0,)), ((), ())), preferred_element_type=F32)


def _split2(x):
    hi = x.astype(BF16)
    lo = (x - hi.astype(F32)).astype(BF16)
    return hi, lo


def _split3(x):
    hi = x.astype(BF16)
    r = x - hi.astype(F32)
    mid = r.astype(BF16)
    lo = (r - mid.astype(F32)).astype(BF16)
    return hi, mid, lo


def mm_nn(a, b, *, add=None, out_dtype=F32, parts=1, tm=1024, tn=512, name):
    m, k = a.shape
    n = b.shape[1]
    np_ = n // parts
    nb = np_ // tn
    tm = min(tm, m)
    assert m % tm == 0 and np_ % tn == 0

    def body(*refs):
        if add is None:
            a_ref, b_ref, o_ref = refs
            acc = _dot(a_ref[...].astype(BF16), b_ref[...])
        else:
            a_ref, b_ref, r_ref, o_ref = refs
            acc = _dot(a_ref[...].astype(BF16), b_ref[...]) + r_ref[...]
        o_ref[...] = acc.astype(out_dtype).reshape(o_ref.shape)

    in_specs = [pl.BlockSpec((tm, k), lambda i, j: (i, 0)), pl.BlockSpec((k, tn), lambda i, j: (0, j))]
    args = [a, b]
    if add is not None:
        in_specs.append(pl.BlockSpec((tm, tn), lambda i, j: (i, j)))
        args.append(add)
    if parts == 1:
        out_spec = pl.BlockSpec((tm, tn), lambda i, j: (i, j))
        out_shape = jax.ShapeDtypeStruct((m, n), out_dtype)
    else:
        out_spec = pl.BlockSpec((1, tm, tn), lambda i, j: (j // nb, i, j % nb))
        out_shape = jax.ShapeDtypeStruct((parts, m, np_), out_dtype)
    return pl.pallas_call(body, grid=(m // tm, n // tn), in_specs=in_specs, out_specs=out_spec,
                          out_shape=out_shape, compiler_params=_params(), name=name)(*args)


def mm_nt(a3, b, *, out_dtype=F32, tm=1024, tn=512, name):
    p, m, kp = a3.shape
    n = b.shape[0]
    tm = min(tm, m)
    assert m % tm == 0 and n % tn == 0 and b.shape[1] == p * kp

    def body(a_ref, b_ref, o_ref, acc_ref):
        part = pl.program_id(2)
        prod = _dot_nt(a_ref[0].astype(BF16), b_ref[...])

        @pl.when(part == 0)
        def _():
            acc_ref[...] = prod

        @pl.when(part > 0)
        def _():
            acc_ref[...] += prod

        @pl.when(part == p - 1)
        def _():
            o_ref[...] = acc_ref[...].astype(out_dtype)

    return pl.pallas_call(
        body, grid=(m // tm, n // tn, p),
        in_specs=[pl.BlockSpec((1, tm, kp), lambda i, j, q: (q, i, 0)), pl.BlockSpec((tn, kp), lambda i, j, q: (j, q))],
        out_specs=pl.BlockSpec((tm, tn), lambda i, j, q: (i, j)),
        out_shape=jax.ShapeDtypeStruct((m, n), out_dtype),
        scratch_shapes=[pltpu.VMEM((tm, tn), F32)],
        compiler_params=_params(), name=name)(a3, b)


def mm_tn(a, b3, *, tk=512, tn=512, tt=2048, name):
    t, k = a.shape
    p, _, np_ = b3.shape
    nb = np_ // tn
    tt = min(tt, t)
    assert t % tt == 0 and k % tk == 0 and np_ % tn == 0

    def body(a_ref, b_ref, o_ref):
        prod = _dot_tn(a_ref[...].astype(BF16), b_ref[0].astype(BF16))

        @pl.when(pl.program_id(2) == 0)
        def _():
            o_ref[...] = prod

        @pl.when(pl.program_id(2) > 0)
        def _():
            o_ref[...] += prod

    return pl.pallas_call(
        body, grid=(k // tk, p * nb, t // tt),
        in_specs=[pl.BlockSpec((tt, tk), lambda i, j, s: (s, i)), pl.BlockSpec((1, tt, tn), lambda i, j, s: (j // nb, s, j % nb))],
        out_specs=pl.BlockSpec((tk, tn), lambda i, j, s: (i, j)),
        out_shape=jax.ShapeDtypeStruct((k, p * np_), F32),
        compiler_params=_params(), name=name)(a, b3)


def rms_mm_nn(h, g, b, *, parts=1, tm=1024, tn=512, name):
    t, d = h.shape
    n = b.shape[1]
    np_ = n // parts
    nb = np_ // tn
    tm = min(tm, t)
    assert t % tm == 0 and np_ % tn == 0

    def body(h_ref, g_ref, b_ref, xn_ref, o_ref):
        @pl.when(pl.program_id(1) == 0)
        def _():
            x = h_ref[...]
            r = lax.rsqrt(jnp.mean(x * x, axis=-1, keepdims=True) + EPS)
            xn_ref[...] = (x * r * g_ref[...]).astype(BF16)

        o_ref[...] = _dot(xn_ref[...], b_ref[...]).reshape(o_ref.shape)

    if parts == 1:
        out_spec = pl.BlockSpec((tm, tn), lambda i, j: (i, j))
        out_shape = jax.ShapeDtypeStruct((t, n), F32)
    else:
        out_spec = pl.BlockSpec((1, tm, tn), lambda i, j: (j // nb, i, j % nb))
        out_shape = jax.ShapeDtypeStruct((parts, t, np_), F32)
    row = pl.BlockSpec((tm, d), lambda i, j: (i, 0))
    return pl.pallas_call(
        body, grid=(t // tm, n // tn),
        in_specs=[row, pl.BlockSpec((1, d), lambda i, j: (0, 0)), pl.BlockSpec((d, tn), lambda i, j: (0, j))],
        out_specs=[row, out_spec], out_shape=[jax.ShapeDtypeStruct((t, d), BF16), out_shape],
        compiler_params=_params(), name=name)(h, g, b)


def mm_nt_rms_bwd(a3, b, h, g, dres, *, name, tm=512):
    p, t, kp = a3.shape
    d = b.shape[0]
    tm = min(tm, t)
    assert t % tm == 0 and b.shape[1] == p * kp

    def body(a_ref, b_ref, h_ref, g_ref, dres_ref, dh_ref, dg_ref, acc_ref):
        i = pl.program_id(0)
        part = pl.program_id(1)
        prod = _dot_nt(a_ref[0].astype(BF16), b_ref[...])

        @pl.when(part == 0)
        def _():
            acc_ref[...] = prod

        @pl.when(part > 0)
        def _():
            acc_ref[...] += prod

        @pl.when(part == p - 1)
        def _():
            x = h_ref[...]
            dy = acc_ref[...]
            r = lax.rsqrt(jnp.mean(x * x, axis=-1, keepdims=True) + EPS)
            gy = dy * g_ref[...]
            dot = jnp.mean(gy * x, axis=-1, keepdims=True)
            dh_ref[...] = dres_ref[...] + r * gy - x * (r * r * r * dot)
            _acc_rows(dg_ref, jnp.sum(dy * x * r, axis=0, keepdims=True), i == 0)

    row = pl.BlockSpec((tm, d), lambda i, q: (i, 0))
    vec = pl.BlockSpec((1, d), lambda i, q: (0, 0))
    return pl.pallas_call(
        body, grid=(t // tm, p),
        in_specs=[pl.BlockSpec((1, tm, kp), lambda i, q: (q, i, 0)), pl.BlockSpec((d, kp), lambda i, q: (0, q)),
                  row, vec, row],
        out_specs=[row, vec],
        out_shape=[jax.ShapeDtypeStruct((t, d), F32), jax.ShapeDtypeStruct((1, d), F32)],
        scratch_shapes=[pltpu.VMEM((tm, d), F32)], compiler_params=_params(), name=name)(a3, b, h, g, dres)


def _causal3(x, w):
    return w[0:1] * pltpu.roll(x, 2, 0) + w[1:2] * pltpu.roll(x, 1, 0) + w[2:3] * x


def _causal3_taps(x_ext, w, tm):
    x2 = pltpu.roll(x_ext, 2, 0)
    x1 = pltpu.roll(x_ext, 1, 0)
    y = w[0:1] * x2 + w[1:2] * x1 + w[2:3] * x_ext
    return y, (x2[HALO:HALO + tm], x1[HALO:HALO + tm], x_ext[HALO:HALO + tm])


def _anticausal3(z, w):
    n = z.shape[0]
    return w[2:3] * z + w[1:2] * pltpu.roll(z, n - 1, 0) + w[0:1] * pltpu.roll(z, n - 2, 0)


def _prev_spec(part, tm, tc, nrow8):
    del nrow8
    return pl.BlockSpec((1, HALO, tc), lambda j, i: (part, jnp.maximum(i * (tm // HALO) - 1, 0), j))


def _next_spec(part, tm, tc, nrow8):
    return pl.BlockSpec((1, HALO, tc), lambda j, i: (part, jnp.minimum((i + 1) * (tm // HALO), nrow8 - 1), j))


def _tile_spec(part, tm, tc):
    return pl.BlockSpec((1, tm, tc), lambda j, i: (part, i, j))


def _acc_rows(ref, val, first):
    @pl.when(first)
    def _():
        ref[...] = val

    @pl.when(jnp.logical_not(first))
    def _():
        ref[...] += val


def ffn_act_down_fwd(up2, cw2, w_down, h, *, name, tm=256, tc=1408):
    _, t, f = up2.shape
    d = h.shape[1]
    tm = min(tm, t)

    def body(g_ref, v_ref, gp_ref, vp_ref, w_ref, wd_ref, h_ref, o_ref, act_ref):
        keep = jnp.where(pl.program_id(0) == 0, 0.0, 1.0)
        for cc in range(f // tc):
            cols = slice(cc * tc, (cc + 1) * tc)
            g_ext = jnp.concatenate([gp_ref[0, :, cols] * keep, g_ref[0, :, cols]], axis=0)
            v_ext = jnp.concatenate([vp_ref[0, :, cols] * keep, v_ref[0, :, cols]], axis=0)
            ug = _causal3(g_ext, w_ref[0, :, cols])[HALO:]
            uv = _causal3(v_ext, w_ref[1, :, cols])[HALO:]
            act_ref[:, cols] = (ug * jax.nn.sigmoid(ug) * uv).astype(BF16)
        o_ref[...] = _dot(act_ref[...], wd_ref[...]) + h_ref[...]

    tile = lambda part: pl.BlockSpec((1, tm, f), lambda i: (part, i, 0))
    prev = lambda part: pl.BlockSpec((1, HALO, f), lambda i: (part, jnp.maximum(i * (tm // HALO) - 1, 0), 0))
    row = pl.BlockSpec((tm, d), lambda i: (i, 0))
    return pl.pallas_call(
        body, grid=(t // tm,),
        in_specs=[tile(0), tile(1), prev(0), prev(1), pl.BlockSpec((2, 3, f), lambda i: (0, 0, 0)),
                  pl.BlockSpec((f, d), lambda i: (0, 0)), row],
        out_specs=[row, pl.BlockSpec((tm, f), lambda i: (i, 0))],
        out_shape=[jax.ShapeDtypeStruct((t, d), F32), jax.ShapeDtypeStruct((t, f), BF16)],
        compiler_params=_params(), name=name)(up2, up2, up2, up2, cw2, w_down, h)


def ffn_act_bwd(dh, w_down, up2, cw2, *, name, tm=256, tc=1408):
    _, t, f = up2.shape
    d = dh.shape[1]
    n8 = t // HALO

    def body(d_ref, dn_ref, wd_ref, g_ref, v_ref, gp_ref, vp_ref, gn_ref, vn_ref, wg_ref, wv_ref, dup_ref, dw_ref):
        i = pl.program_id(1)
        first = i == 0
        keep_p = jnp.where(first, 0.0, 1.0)
        keep_n = jnp.where(i == pl.num_programs(1) - 1, 0.0, 1.0)
        wg = wg_ref[0]
        wv = wv_ref[0]
        g_ext = jnp.concatenate([gp_ref[0] * keep_p, g_ref[0], gn_ref[0]], axis=0)
        v_ext = jnp.concatenate([vp_ref[0] * keep_p, v_ref[0], vn_ref[0]], axis=0)
        dh_ext = jnp.concatenate([d_ref[...], dn_ref[...] * keep_n], axis=0)
        d_ext = _dot_nt(dh_ext.astype(BF16), wd_ref[...])
        ug, (g2, g1, g0) = _causal3_taps(g_ext, wg, tm)
        uv, (v2, v1, v0) = _causal3_taps(v_ext, wv, tm)
        ug = ug[HALO:]
        uv = uv[HALO:]
        s = jax.nn.sigmoid(ug)
        dg = d_ext * uv * (s * (1.0 + ug * (1.0 - s)))
        dv = d_ext * (ug * s)
        dup_ref[0] = _anticausal3(dg, wg)[:tm].astype(BF16)
        dup_ref[1] = _anticausal3(dv, wv)[:tm].astype(BF16)
        dgt = dg[:tm]
        dvt = dv[:tm]
        zero = jnp.zeros((HALO - 3, tc), F32)
        rows_g = [jnp.sum(dgt * x, axis=0, keepdims=True) for x in (g2, g1, g0)] + [zero]
        rows_v = [jnp.sum(dvt * x, axis=0, keepdims=True) for x in (v2, v1, v0)] + [zero]
        _acc_rows(dw_ref, jnp.stack([jnp.concatenate(rows_g, axis=0), jnp.concatenate(rows_v, axis=0)]), first)

    wspec = lambda part: pl.BlockSpec((1, 3, tc), lambda j, i: (part, 0, j))
    return pl.pallas_call(
        body, grid=(f // tc, t // tm),
        in_specs=[pl.BlockSpec((tm, d), lambda j, i: (i, 0)),
                  pl.BlockSpec((HALO, d), lambda j, i: (jnp.minimum((i + 1) * (tm // HALO), n8 - 1), 0)),
                  pl.BlockSpec((tc, d), lambda j, i: (j, 0)),
                  _tile_spec(0, tm, tc), _tile_spec(1, tm, tc), _prev_spec(0, tm, tc, n8), _prev_spec(1, tm, tc, n8),
                  _next_spec(0, tm, tc, n8), _next_spec(1, tm, tc, n8), wspec(0), wspec(1)],
        out_specs=[pl.BlockSpec((2, tm, tc), lambda j, i: (0, i, j)), pl.BlockSpec((2, HALO, tc), lambda j, i: (0, 0, j))],
        out_shape=[jax.ShapeDtypeStruct((2, t, f), BF16), jax.ShapeDtypeStruct((2, HALO, f), F32)],
        compiler_params=_params(), name=name)(dh, dh, w_down, up2, up2, up2, up2, up2, up2, cw2, cw2)


def conv_mix_fwd(proj3, ck, *, name, tm=512, tc=512):
    _, t, c = proj3.shape
    n8 = t // HALO

    def body(b_ref, c_ref, u_ref, cp_ref, up_ref, w_ref, o_ref):
        keep = jnp.where(pl.program_id(1) == 0, 0.0, 1.0)
        cu_ext = jnp.concatenate([cp_ref[0] * up_ref[0] * keep, c_ref[0] * u_ref[0]], axis=0)
        o_ref[...] = (b_ref[0] * _causal3(cu_ext, w_ref[0])[HALO:]).astype(BF16)

    return pl.pallas_call(
        body, grid=(c // tc, t // tm),
        in_specs=[_tile_spec(0, tm, tc), _tile_spec(1, tm, tc), _tile_spec(2, tm, tc), _prev_spec(1, tm, tc, n8),
                  _prev_spec(2, tm, tc, n8), pl.BlockSpec((1, 3, tc), lambda j, i: (0, 0, j))],
        out_specs=pl.BlockSpec((tm, tc), lambda j, i: (i, j)),
        out_shape=jax.ShapeDtypeStruct((t, c), BF16), compiler_params=_params(), name=name)(proj3, proj3, proj3, proj3, proj3, ck)


def conv_mix_bwd(dh, w_out, proj3, ck, *, name, tm=512, tc=512):
    _, t, c = proj3.shape
    d = dh.shape[1]
    n8 = t // HALO

    def body(d_ref, dn_ref, wo_ref, b_ref, c_ref, u_ref, cp_ref, up_ref, bn_ref, w_ref, dp_ref, dw_ref):
        i = pl.program_id(1)
        first = i == 0
        keep_p = jnp.where(first, 0.0, 1.0)
        keep_n = jnp.where(i == pl.num_programs(1) - 1, 0.0, 1.0)
        w = w_ref[0]
        cu_ext = jnp.concatenate([cp_ref[0] * up_ref[0] * keep_p, c_ref[0] * u_ref[0]], axis=0)
        cv, (x2, x1, x0) = _causal3_taps(cu_ext, w, tm)
        cv = cv[HALO:]
        dh_ext = jnp.concatenate([d_ref[...], dn_ref[...] * keep_n], axis=0)
        d_ext = _dot_nt(dh_ext.astype(BF16), wo_ref[...])
        dyt = d_ext[:tm]
        b_ext = jnp.concatenate([b_ref[0], bn_ref[0]], axis=0)
        dcv = d_ext * b_ext
        dcu = _anticausal3(dcv, w)[:tm]
        dp_ref[0] = (dyt * cv).astype(BF16)
        dp_ref[1] = (dcu * u_ref[0]).astype(BF16)
        dp_ref[2] = (dcu * c_ref[0]).astype(BF16)
        dcvt = dcv[:tm]
        rows = [jnp.sum(dcvt * x, axis=0, keepdims=True) for x in (x2, x1, x0)] + [jnp.zeros((HALO - 3, tc), F32)]
        _acc_rows(dw_ref, jnp.concatenate(rows, axis=0)[None], first)

    return pl.pallas_call(
        body, grid=(c // tc, t // tm),
        in_specs=[pl.BlockSpec((tm, d), lambda j, i: (i, 0)),
                  pl.BlockSpec((HALO, d), lambda j, i: (jnp.minimum((i + 1) * (tm // HALO), n8 - 1), 0)),
                  pl.BlockSpec((tc, d), lambda j, i: (j, 0)),
                  _tile_spec(0, tm, tc), _tile_spec(1, tm, tc), _tile_spec(2, tm, tc),
                  _prev_spec(1, tm, tc, n8), _prev_spec(2, tm, tc, n8),
                  _next_spec(0, tm, tc, n8), pl.BlockSpec((1, 3, tc), lambda j, i: (0, 0, j))],
        out_specs=[pl.BlockSpec((3, tm, tc), lambda j, i: (0, i, j)), pl.BlockSpec((1, HALO, tc), lambda j, i: (0, 0, j))],
        out_shape=[jax.ShapeDtypeStruct((3, t, c), BF16), jax.ShapeDtypeStruct((1, HALO, c), F32)],
        compiler_params=_params(), name=name)(dh, dh, w_out, proj3, proj3, proj3, proj3, proj3, proj3, ck)


def _head_sums(x, bd):
    hi, lo = _split2(x)
    return _dot(hi, bd) + _dot(lo, bd)


def attn_prep_fwd(proj, gq, gk, fbias, bd, *, name, tm=256):
    t = proj.shape[0]

    def body(q_ref, k_ref, v_ref, f_ref, gq_ref, gk_ref, fb_ref, bd_ref, qs_ref, kn_ref, vb_ref, lf_ref):
        bd = bd_ref[...]

        def headnorm(x_ref, g_ref, o_ref, scale):
            for c in range(MIX // 128):
                sl = slice(128 * c, 128 * (c + 1))
                x = x_ref[:, sl]
                r = lax.rsqrt(_head_sums(x * x, bd) * (1.0 / HEAD_DIM) + EPS)
                o_ref[:, sl] = (x * r * (g_ref[:, sl] * scale)).astype(BF16)

        headnorm(q_ref, gq_ref, qs_ref, SCALE)
        headnorm(k_ref, gk_ref, kn_ref, 1.0)
        vb_ref[...] = v_ref[...].astype(BF16)
        fl = f_ref[...] + fb_ref[...]
        logf = jnp.minimum(fl, 0.0) - jnp.log(1.0 + jnp.exp(-jnp.abs(fl)))
        lf_ref[...] = logf.T[0:H_FOX, :]

    col = lambda c: pl.BlockSpec((tm, MIX), lambda i: (i, c))
    vec = pl.BlockSpec((1, MIX), lambda i: (0, 0))
    out = pl.BlockSpec((tm, MIX), lambda i: (i, 0))
    return pl.pallas_call(
        body, grid=(t // tm,),
        in_specs=[col(0), col(1), col(2), pl.BlockSpec((tm, 128), lambda i: (i, 3 * MIX // 128)), vec, vec,
                  pl.BlockSpec((1, 128), lambda i: (0, 0)), pl.BlockSpec((128, 128), lambda i: (0, 0))],
        out_specs=[out, out, out, pl.BlockSpec((H_FOX, tm), lambda i: (0, i))],
        out_shape=[jax.ShapeDtypeStruct((t, MIX), BF16)] * 3 + [jax.ShapeDtypeStruct((H_FOX, t), F32)],
        compiler_params=_params(), name=name)(proj, proj, proj, proj, gq, gk, fbias, bd)


def attn_prep_bwd(proj, dq_f, dq_s, dkv_f, dkv_s, dfl, gq, gk, bd, *, name, tm=256):
    t = proj.shape[0]
    per_group = H_FOX // 2

    def body(q_ref, k_ref, dqf_ref, dqs_ref, dkvf_ref, dkvs_ref, dfl_ref, gq_ref, gk_ref, bd_ref, dp_ref, dgq_ref,
             dgk_ref):
        bd = bd_ref[...]
        first = pl.program_id(0) == 0
        low = lax.broadcasted_iota(jnp.int32, (tm, 128), 1) < HEAD_DIM

        def two_heads(fox_ref, sb_ref, c):
            ref = fox_ref if c < per_group else sb_ref
            return ref[2 * (c % per_group)], ref[2 * (c % per_group) + 1]

        def low_halves(ab):
            return jnp.where(low, ab[0], pltpu.roll(ab[1], HEAD_DIM, 1))

        def high_halves(ab):
            return jnp.where(low, pltpu.roll(ab[0], HEAD_DIM, 1), ab[1])

        def back(x_ref, grad, g_ref, col0, scale, dg_ref):
            parts = []
            for c in range(MIX // 128):
                sl = slice(128 * c, 128 * (c + 1))
                x = x_ref[:, sl]
                r = lax.rsqrt(_head_sums(x * x, bd) * (1.0 / HEAD_DIM) + EPS)
                dn = grad(c) * scale
                gy = dn * g_ref[:, sl]
                hs = _head_sums(gy * x, bd) * (1.0 / HEAD_DIM)
                dp_ref[:, col0 + 128 * c:col0 + 128 * (c + 1)] = (r * gy - x * (r * r * r * hs)).astype(BF16)
                parts.append(jnp.sum(dn * x * r, axis=0, keepdims=True))
            _acc_rows(dg_ref, jnp.concatenate(parts, axis=1), first)

        back(q_ref, lambda c: low_halves(two_heads(dqf_ref, dqs_ref, c)), gq_ref, 0, SCALE, dgq_ref)
        back(k_ref, lambda c: low_halves(two_heads(dkvf_ref, dkvs_ref, c)), gk_ref, MIX, 1.0, dgk_ref)
        for c in range(MIX // 128):
            dp_ref[:, 2 * MIX + 128 * c:2 * MIX + 128 * (c + 1)] = high_halves(two_heads(dkvf_ref, dkvs_ref, c)).astype(BF16)
        dp_ref[:, 3 * MIX:] = dfl_ref[...]

    col = lambda c: pl.BlockSpec((tm, MIX), lambda i: (i, c))
    heads = pl.BlockSpec((H_FOX, tm, 128), lambda i: (0, i, 0))
    vec = pl.BlockSpec((1, MIX), lambda i: (0, 0))
    return pl.pallas_call(
        body, grid=(t // tm,),
        in_specs=[col(0), col(1), heads, heads, heads, heads, pl.BlockSpec((tm, 128), lambda i: (i, 0)), vec, vec,
                  pl.BlockSpec((128, 128), lambda i: (0, 0))],
        out_specs=[pl.BlockSpec((tm, ATTN_IN_PAD), lambda i: (i, 0)), vec, vec],
        out_shape=[jax.ShapeDtypeStruct((t, ATTN_IN_PAD), BF16), jax.ShapeDtypeStruct((1, MIX), F32),
                   jax.ShapeDtypeStruct((1, MIX), F32)],
        compiler_params=_params(), name=name)(proj, proj, dq_f, dq_s, dkv_f, dkv_s, dfl, gq, gk, bd)


def gate_cumsum(logf3, tri, *, name):
    nc, r, _ = logf3.shape

    def body(x_ref, tri_ref, o_ref):
        tri_m = tri_ref[...]

        def step(c, carry):
            hi, mid, lo = _split3(x_ref[c])
            cs = _dot(hi, tri_m) + _dot(mid, tri_m) + _dot(lo, tri_m) + carry
            o_ref[c] = cs
            return cs[:, 127:128]

        lax.fori_loop(0, nc, step, jnp.zeros((r, 1), F32))

    return pl.pallas_call(body, out_shape=jax.ShapeDtypeStruct(logf3.shape, F32), compiler_params=_params(),
                          name=name)(logf3, tri)


def gate_cumsum_bwd(dcum3, logf3, tri, *, name):
    nc, r, _ = dcum3.shape

    def body(x_ref, lf_ref, tri_ref, o_ref, s_ref):
        tri_m = tri_ref[...]

        def step(n, carry):
            car, tot = carry
            c = nc - 1 - n
            hi, mid, lo = _split3(x_ref[c])
            cs = _dot(hi, tri_m) + _dot(mid, tri_m) + _dot(lo, tri_m) + car
            dl = cs * (1.0 - jnp.exp(lf_ref[c]))
            o_ref[c] = dl
            return cs[:, 0:1], tot + dl

        _, tot = lax.fori_loop(0, nc, step, (jnp.zeros((r, 1), F32), jnp.zeros((r, 128), F32)))
        s_ref[...] = jnp.broadcast_to(jnp.sum(tot, axis=1, keepdims=True), tot.shape)

    return pl.pallas_call(body, out_shape=[jax.ShapeDtypeStruct(dcum3.shape, F32), jax.ShapeDtypeStruct((r, 128), F32)],
                          compiler_params=_params(), name=name)(dcum3, logf3, tri)


def _causal_iota():
    row = lax.broadcasted_iota(jnp.int32, (BQ, BQ), 0)
    col = lax.broadcasted_iota(jnp.int32, (BQ, BQ), 1)
    return row, col


def _head_specs(nj, head0):
    qin = pl.BlockSpec((1, BQ, HEAD_DIM), lambda h, i: (h + head0, i, 0))
    kin = pl.BlockSpec((1, nj, BQ, HEAD_DIM), lambda h, i: (h + head0, 0, 0, 0))
    qin2 = pl.BlockSpec((1, BQ, 2 * HEAD_DIM), lambda h, i: (h + head0, i, 0))
    kin2 = pl.BlockSpec((1, nj, BQ, 2 * HEAD_DIM), lambda h, i: (h + head0, 0, 0, 0))
    qspec = pl.BlockSpec((1, BQ, HEAD_DIM), lambda h, i: (h, i, 0))
    kspec2 = pl.BlockSpec((1, nj, BQ, 2 * HEAD_DIM), lambda h, i: (h, 0, 0, 0))
    return qin, kin, qin2, kin2, qspec, kspec2


STOP = -105.0
STOP_WIDE = -115.0
FIXED_REF_MAX = 40.0


def _store_kmax(k_ref, kmax_ref, nj):
    def step(j, mx):
        kf = k_ref[0, j].astype(F32)
        return jnp.maximum(mx, jnp.max(jnp.sum(kf * kf, axis=1, keepdims=True), axis=0, keepdims=True))

    mx = lax.fori_loop(0, nj, step, jnp.zeros((1, 1), F32))
    kmax_ref[...] = jnp.broadcast_to(jnp.sqrt(mx), kmax_ref.shape)


def _qk_bound(q, kmax_ref):
    qf = q.astype(F32)
    return jnp.sqrt(jnp.sum(qf * qf, axis=1, keepdims=True)) * kmax_ref[0:1, 0:1] * 1.001


def _first_and_last_step():
    h, i = pl.program_id(0), pl.program_id(1)
    first = jnp.logical_and(h == 0, i == 0)
    last = jnp.logical_and(h == pl.num_programs(0) - 1, i == pl.num_programs(1) - 1)
    return first, last


def fox_fwd(qs, kn4, va4, fcol, frow4, *, name, gather=None):
    _, t, dh = qs.shape
    nh = H_FOX
    nj = t // BQ

    def body(*refs):
        if gather is None:
            q_ref, k_ref, v_ref, fc_ref, fr_ref, o_ref, lse_ref, kmax_ref = refs
        else:
            q_ref, k_ref, v_ref, fc_ref, fr_ref, src_ref, o_ref, lse_ref, dst_ref, kmax_ref = refs[:10]
            first_step, last_step = _first_and_last_step()

            @pl.when(first_step)
            def _():
                _chip_gather(src_ref, dst_ref, *refs[10:])[0]()

        i = pl.program_id(1)

        @pl.when(i == 0)
        def _():
            _store_kmax(k_ref, kmax_ref, nj)

        q = q_ref[0]
        fq = fc_ref[0]
        bound = _qk_bound(q, kmax_ref)
        row, col = _causal_iota()

        def gate_at_block_end(j):
            return fr_ref[0, j][:, BQ - 1:BQ]

        def pv(p, j):
            p_hi, p_lo = _split2(p)
            return _dot(p_hi, v_ref[0, j]) + _dot(p_lo, v_ref[0, j])

        def walk(block, live, init):
            carry = block(i, init, True)

            def cond(c):
                n, carry = c
                return jnp.logical_and(n < i, live(jnp.maximum(i - 1 - n, 0), carry))

            _, carry = lax.while_loop(cond, lambda c: (c[0] + 1, block(i - 1 - c[0], c[1], False)), (0, carry))
            return carry

        def fixed_reference(_):
            shift = fq - bound

            def probs(j, offset):
                return jnp.exp(_dot_nt(q, k_ref[0, j]) + (shift + offset) - fr_ref[0, j])

            def live(c):
                n, acc = c
                gate = gate_at_block_end(jnp.maximum(i - 1 - n, 0))
                return jnp.logical_and(n < i, jnp.max(fq - gate - jnp.log(acc[:, dh:dh + 1])) >= STOP_WIDE)

            def two_blocks(c):
                n, acc = c
                ja = i - 1 - n
                jb = i - 2 - n
                absent = jnp.where(jb >= 0, 0.0, NEG)
                jb = jnp.maximum(jb, 0)
                return n + 2, acc + (pv(probs(ja, 0.0), ja) + pv(probs(jb, absent), jb))

            acc = pv(jnp.where(col <= row, probs(i, 0.0), 0.0), i)
            _, acc = lax.while_loop(live, two_blocks, (0, acc))
            l = acc[:, dh:dh + 1]
            return acc[:, :dh] / l, bound + jnp.log(l)

        def running_maximum(_):
            def block(j, carry, diag):
                m, acc = carry
                s = _dot_nt(q, k_ref[0, j]) + fq - fr_ref[0, j]
                if diag:
                    s = jnp.where(col <= row, s, NEG)
                m_new = jnp.maximum(m, jnp.max(s, axis=1, keepdims=True))
                return m_new, jnp.exp(m - m_new) * acc + pv(jnp.exp(s - m_new), j)

            def live(j, carry):
                return jnp.max(bound + fq - gate_at_block_end(j) - carry[0]) >= STOP

            m, acc = walk(block, live, (jnp.full((BQ, 1), NEG, F32), jnp.zeros((BQ, 2 * dh), F32)))
            l = acc[:, dh:dh + 1]
            return acc[:, :dh] / l, m + jnp.log(l)

        o, lse = lax.cond(jnp.max(bound) < FIXED_REF_MAX, fixed_reference, running_maximum, 0)
        o_ref[0] = o
        lse_ref[0] = lse

        if gather is not None:
            @pl.when(last_step)
            def _():
                _chip_gather(src_ref, dst_ref, *refs[10:])[1]()

    qin, kin, _, kin2, qspec, _ = _head_specs(nj, 0)
    cspec = pl.BlockSpec((1, BQ, 1), lambda h, i: (h, i, 0))
    in_specs = [qin, kin, kin2, cspec, pl.BlockSpec((1, nj, 1, BQ), lambda h, i: (h, 0, 0, 0))]
    out_specs = [qspec, cspec]
    out_shape = [jax.ShapeDtypeStruct((nh, t, dh), F32), jax.ShapeDtypeStruct((nh, t, 1), F32)]
    scratch = [pltpu.VMEM((8, 128), F32)]
    args = [qs, kn4, va4, fcol, frow4]
    if gather is not None:
        in_specs.append(_ANY)
        out_specs.append(_ANY)
        out_shape.append(jax.ShapeDtypeStruct((N_CHIPS,) + gather.shape, gather.dtype))
        scratch += _chip_sems()
        args.append(gather)
    return pl.pallas_call(body, grid=(nh, nj), in_specs=in_specs, out_specs=out_specs, out_shape=out_shape,
                          scratch_shapes=scratch, compiler_params=_params(), name=name)(*args)


def fox_bwd(qs, kn4, v4, qa, doa, fcol, frow4, o, do, lse, *, name, scatter=None):
    _, t, dh = qs.shape
    nh = H_FOX
    nj = t // BQ

    def body(*refs):
        q_ref, k_ref, v_ref, qa_ref, doa_ref, fc_ref, fr_ref, o_ref, do_ref, lse_ref = refs[:10]
        if scatter is None:
            dq_ref, dkv_ref, dfk_ref, kmax_ref = refs[10:]
        else:
            g_ref, dq_ref, dkv_ref, dfk_ref, land_ref, kmax_ref = refs[10:16]
            first_step, last_step = _first_and_last_step()

            @pl.when(first_step)
            def _():
                _chip_scatter(g_ref, land_ref, *refs[16:])[0]()

        i = pl.program_id(1)

        @pl.when(i == 0)
        def _():
            dkv_ref[...] = jnp.zeros_like(dkv_ref)
            dfk_ref[...] = jnp.zeros_like(dfk_ref)
            _store_kmax(k_ref, kmax_ref, nj)

        q = q_ref[0]
        do_b = do_ref[0]
        fq = fc_ref[0]
        lse_q = lse_ref[0]
        dd = jnp.sum(do_b.astype(F32) * o_ref[0], axis=1, keepdims=True)
        rhs = jnp.concatenate([qa_ref[0], doa_ref[0]], axis=0)
        edge = _qk_bound(q, kmax_ref) + fq - lse_q

        def negligible(j):
            return jnp.logical_and(j < i, jnp.max(edge - fr_ref[0, j][:, BQ - 1:BQ]) < STOP_WIDE)

        first = lax.while_loop(negligible, lambda j: j + 1, 0)

        shift = fq - lse_q

        def block(j, offset, diag):
            k = k_ref[0, j]
            p = jnp.exp(_dot_nt(q, k) + (shift + offset) - fr_ref[0, j])
            if diag:
                row, col = _causal_iota()
                p = jnp.where(col <= row, p, 0.0)
            ds = p * (_dot_nt(do_b, v_ref[0, j]) - dd)
            ds_b = ds.astype(BF16)
            dkv_ref[0, j] += _dot_tn(jnp.concatenate([ds_b, p.astype(BF16)], axis=0), rhs)
            dfk_ref[0, j] -= jnp.sum(ds, axis=0, keepdims=True)
            return _dot(ds_b, k)

        def two_blocks(n, dq):
            ja = first + 2 * n
            jb = ja + 1
            absent = jnp.where(jb < i, 0.0, NEG)
            jb = jnp.minimum(jb, i - 1)
            return dq + (block(ja, 0.0, False) + block(jb, absent, False))

        dq = lax.fori_loop(0, (i - first + 1) // 2, two_blocks, jnp.zeros((BQ, dh), F32))
        dq_ref[0] = jnp.zeros((BQ, 2 * dh), F32)
        dq_ref[0, :, :dh] = dq + block(i, 0.0, True)

        if scatter is not None:
            @pl.when(last_step)
            def _():
                _chip_scatter(g_ref, land_ref, *refs[16:])[1]()

    qin, kin, qin2, _, qspec, kspec2 = _head_specs(nj, 0)
    cspec = pl.BlockSpec((1, BQ, 1), lambda h, i: (h, i, 0))
    rspec = pl.BlockSpec((1, nj, 1, BQ), lambda h, i: (h, 0, 0, 0))
    in_specs = [qin, kin, kin, qin2, qin2, cspec, rspec, qspec, qin, cspec]
    out_specs = [pl.BlockSpec((1, BQ, 2 * dh), lambda h, i: (h, i, 0)), kspec2, rspec]
    out_shape = [jax.ShapeDtypeStruct((nh, t, 2 * dh), F32), jax.ShapeDtypeStruct((nh, nj, BQ, 2 * dh), F32),
                 jax.ShapeDtypeStruct((nh, nj, 1, BQ), F32)]
    scratch = [pltpu.VMEM((8, 128), F32)]
    args = [qs, kn4, v4, qa, doa, fcol, frow4, o, do, lse]
    if scatter is not None:
        in_specs.append(_ANY)
        out_specs.append(_ANY)
        out_shape.append(jax.ShapeDtypeStruct(scatter.shape, scatter.dtype))
        scratch += _chip_sems()
        args.append(scatter)
    return pl.pallas_call(body, grid=(nh, nj), in_specs=in_specs, out_specs=out_specs, out_shape=out_shape,
                          scratch_shapes=scratch, compiler_params=_params(), name=name)(*args)


def _sb_logs(z, diag):
    e = jnp.exp(-jnp.abs(z))
    sp = jnp.log(1.0 + e)
    logb = jnp.minimum(z, 0.0) - sp
    lom = -jnp.maximum(z, 0.0) - sp
    strict = None
    if diag:
        row, col = _causal_iota()
        strict = col < row
        lom = jnp.where(strict, lom, 0.0)
    return logb, lom, e, strict


SB_GROUP = BQ // 2


def _sums_over_later_keys(lom, tri_m):
    halves = [lom[:, :SB_GROUP], lom[:, SB_GROUP:]]
    totals = [jnp.sum(x, axis=1, keepdims=True) for x in halves]
    within = []
    for x in halves:
        hi, lo = _split2(x)
        within.append(_dot(hi, tri_m) + _dot(lo, tri_m))
    return jnp.concatenate([within[0] + totals[1], within[1]], axis=1), totals[0] + totals[1]


def _sums_over_earlier_keys(da, tri_m):
    halves = [da[:, :SB_GROUP], da[:, SB_GROUP:]]
    totals = [jnp.sum(x, axis=1, keepdims=True) for x in halves]
    within = [_dot_nt(x.astype(BF16), tri_m) for x in halves]
    return jnp.concatenate([within[0], within[1] + totals[0]], axis=1), totals[0] + totals[1]


def sb_fwd(qs, kn4, va4, tri, *, name, gather=None):
    _, t, dh = qs.shape
    nh = H_SB
    nj = t // BQ
    assert nj <= 128

    def body(*refs):
        if gather is None:
            q_ref, k_ref, v_ref, tri_ref, o_ref, rs_ref = refs
        else:
            q_ref, k_ref, v_ref, tri_ref, src_ref, o_ref, rs_ref, dst_ref = refs[:8]
            first_step, last_step = _first_and_last_step()

            @pl.when(first_step)
            def _():
                _chip_gather(src_ref, dst_ref, *refs[8:])[0]()

        i = pl.program_id(1)
        q = q_ref[0]
        tri_m = tri_ref[...]
        lane = lax.broadcasted_iota(jnp.int32, (BQ, 128), 1)

        def block(j, carry, diag):
            run, acc, rall = carry
            logb, lom, _, strict = _sb_logs(_dot_nt(q, k_ref[0, j]), diag)
            later, total = _sums_over_later_keys(lom, tri_m)
            w = jnp.exp(logb + later + run)
            if diag:
                w = jnp.where(strict, w, 0.0)
            acc = acc + _dot(w.astype(BF16), v_ref[0, j])
            rall = jnp.where(lane == j, run, rall)
            return run + total, acc, rall

        init = (jnp.zeros((BQ, 1), F32), jnp.zeros((BQ, 2 * dh), F32), jnp.full((BQ, 128), NEG, F32))
        carry = block(i, init, True)

        def cond(c):
            n, carry = c
            return jnp.logical_and(n < i, jnp.max(carry[0]) >= STOP)

        _, (_, acc, rall) = lax.while_loop(cond, lambda c: (c[0] + 1, block(i - 1 - c[0], c[1], False)), (0, carry))
        o_ref[0] = acc[:, :dh].astype(BF16)
        rs_ref[0] = rall

        if gather is not None:
            @pl.when(last_step)
            def _():
                _chip_gather(src_ref, dst_ref, *refs[8:])[1]()

    qin, kin, _, kin2, qspec, _ = _head_specs(nj, H_FOX)
    rspec = pl.BlockSpec((1, BQ, 128), lambda h, i: (h, i, 0))
    in_specs = [qin, kin, kin2, pl.BlockSpec((SB_GROUP, SB_GROUP), lambda h, i: (0, 0))]
    out_specs = [qspec, rspec]
    out_shape = [jax.ShapeDtypeStruct((nh, t, dh), BF16), jax.ShapeDtypeStruct((nh, t, 128), F32)]
    scratch = []
    args = [qs, kn4, va4, tri]
    if gather is not None:
        in_specs.append(_ANY)
        out_specs.append(_ANY)
        out_shape.append(jax.ShapeDtypeStruct((N_CHIPS,) + gather.shape, gather.dtype))
        scratch += _chip_sems()
        args.append(gather)
    return pl.pallas_call(body, grid=(nh, nj), in_specs=in_specs, out_specs=out_specs, out_shape=out_shape,
                          scratch_shapes=scratch, compiler_params=_params(), name=name)(*args)


def sb_bwd(qs, kn4, v4, qa, doa, tri, do, rsave, *, name, scatter=None):
    _, t, dh = qs.shape
    nh = H_SB
    nj = t // BQ

    def body(*refs):
        q_ref, k_ref, v_ref, qa_ref, doa_ref, tri_ref, do_ref, rs_ref = refs[:8]
        if scatter is None:
            dq_ref, dkv_ref = refs[8:]
        else:
            g_ref, dq_ref, dkv_ref, land_ref = refs[8:12]
            first_step, last_step = _first_and_last_step()

            @pl.when(first_step)
            def _():
                _chip_scatter(g_ref, land_ref, *refs[12:])[0]()

        i = pl.program_id(1)

        @pl.when(i == 0)
        def _():
            dkv_ref[...] = jnp.zeros_like(dkv_ref)

        q = q_ref[0]
        do_b = do_ref[0]
        tri_m = tri_ref[...]
        rall = rs_ref[0]
        lane = lax.broadcasted_iota(jnp.int32, (BQ, 128), 1)
        rhs = jnp.concatenate([qa_ref[0], doa_ref[0]], axis=0)
        lane1 = lax.broadcasted_iota(jnp.int32, (1, 128), 1)
        unvisited = jnp.logical_and(lane1 < i, jnp.max(rall, axis=0, keepdims=True) < STOP)
        first = jnp.sum(unvisited.astype(jnp.int32))

        def block(j, carry, diag):
            dq, ecar = carry
            k = k_ref[0, j]
            z = _dot_nt(q, k)
            logb, lom, e, strict = _sb_logs(z, diag)
            run = jnp.sum(jnp.where(lane == j, rall, 0.0), axis=1, keepdims=True)
            w = jnp.exp(logb + _sums_over_later_keys(lom, tri_m)[0] + run)
            if diag:
                w = jnp.where(strict, w, 0.0)
            da = w * _dot_nt(do_b, v_ref[0, j])
            earlier, da_total = _sums_over_earlier_keys(da, tri_m)
            before = earlier + ecar
            inv = 1.0 / (1.0 + e)
            beta = jnp.where(z >= 0.0, 1.0, e) * inv
            one_minus = jnp.where(z >= 0.0, e, 1.0) * inv
            dz = da * one_minus - before * beta
            if diag:
                dz = jnp.where(strict, dz, 0.0)
            dz_b = dz.astype(BF16)
            dkv_ref[0, j] += _dot_tn(jnp.concatenate([dz_b, w.astype(BF16)], axis=0), rhs)
            return dq + _dot(dz_b, k), ecar + da_total

        carry = lax.fori_loop(first, i, lambda j, c: block(j, c, False),
                              (jnp.zeros((BQ, dh), F32), jnp.zeros((BQ, 1), F32)))
        dq, _ = block(i, carry, True)
        dq_ref[0] = jnp.zeros((BQ, 2 * dh), F32)
        dq_ref[0, :, :dh] = dq

        if scatter is not None:
            @pl.when(last_step)
            def _():
                _chip_scatter(g_ref, land_ref, *refs[12:])[1]()

    qin, kin, qin2, _, qspec, kspec2 = _head_specs(nj, H_FOX)
    in_specs = [qin, kin, kin, qin2, qin2, pl.BlockSpec((SB_GROUP, SB_GROUP), lambda h, i: (0, 0)), qin,
                pl.BlockSpec((1, BQ, 128), lambda h, i: (h, i, 0))]
    out_specs = [pl.BlockSpec((1, BQ, 2 * dh), lambda h, i: (h, i, 0)), kspec2]
    out_shape = [jax.ShapeDtypeStruct((nh, t, 2 * dh), F32), jax.ShapeDtypeStruct((nh, nj, BQ, 2 * dh), F32)]
    scratch = []
    args = [qs, kn4, v4, qa, doa, tri, do, rsave]
    if scatter is not None:
        in_specs.append(_ANY)
        out_specs.append(_ANY)
        out_shape.append(jax.ShapeDtypeStruct(scatter.shape, scatter.dtype))
        scratch += _chip_sems()
        args.append(scatter)
    return pl.pallas_call(body, grid=(nh, nj), in_specs=in_specs, out_specs=out_specs, out_shape=out_shape,
                          scratch_shapes=scratch, compiler_params=_params(), name=name)(*args)


def loss_head(y, target, *, name, tm=512):
    t, d = y.shape

    def body(y_ref, t_ref, l_ref, dy_ref, acc_ref):
        i = pl.program_id(0)
        diff = y_ref[...] - t_ref[...]
        dy_ref[...] = diff * (1.0 / d)
        part = jnp.sum(diff * diff, axis=0, keepdims=True)

        @pl.when(i == 0)
        def _():
            acc_ref[...] = part

        @pl.when(i > 0)
        def _():
            acc_ref[...] += part

        @pl.when(i == pl.num_programs(0) - 1)
        def _():
            l_ref[...] = jnp.full(l_ref.shape, (0.5 / d) * jnp.sum(acc_ref[...]), F32)

    row = pl.BlockSpec((tm, d), lambda i: (i, 0))
    return pl.pallas_call(
        body, grid=(t // tm,), in_specs=[row, row],
        out_specs=[pl.BlockSpec((8, 128), lambda i: (0, 0)), row],
        out_shape=[jax.ShapeDtypeStruct((8, 128), F32), jax.ShapeDtypeStruct((t, d), F32)],
        scratch_shapes=[pltpu.VMEM((1, d), F32)], compiler_params=_params(), name=name)(y, target)


def _to_heads(a):
    t = a.shape[0]
    return a.reshape(t, N_HEADS, HEAD_DIM).transpose(1, 0, 2)


def _from_heads(a):
    t = a.shape[1]
    return a.transpose(1, 0, 2).reshape(t, MIX)


def _lanes_to_chunks(a):
    r, t = a.shape
    return a.reshape(r, t // 128, 128).transpose(1, 0, 2)


def _chunks_to_lanes(a):
    nc, r, _ = a.shape
    return a.transpose(1, 0, 2).reshape(r, nc * 128)


def _constants():
    idx = jnp.arange(128)
    bd = (idx[:, None] // HEAD_DIM == idx[None, :] // HEAD_DIM).astype(BF16)
    tri_le = (idx[:, None] <= idx[None, :]).astype(BF16)
    tri_ge = (idx[:, None] >= idx[None, :]).astype(BF16)
    jdx = jnp.arange(SB_GROUP)
    tri_gt = (jdx[:, None] > jdx[None, :]).astype(BF16)
    return dict(bd=bd, tri_le=tri_le, tri_ge=tri_ge, tri_gt=tri_gt)


def attn_layer_fwd(h, w, cst, gather=None):
    t = h.shape[0]
    nj = t // BQ
    xn, proj = rms_mm_nn(h, w["norm"], w["w_in"], tn=640, name="attn_in_proj")
    qs, kn, vb, logf = attn_prep_fwd(proj, w["gq"], w["gk"], w["fbias"], cst["bd"], name="attn_prep_fwd")
    logf3 = _lanes_to_chunks(logf)
    cum = _chunks_to_lanes(gate_cumsum(logf3, cst["tri_le"], name="gate_cumsum"))
    fcol = cum.reshape(H_FOX, t, 1)
    frow4 = cum.reshape(H_FOX, nj, 1, BQ)
    qh = _to_heads(qs)
    kh4 = _to_heads(kn).reshape(N_HEADS, nj, BQ, HEAD_DIM)
    vh4 = _to_heads(vb).reshape(N_HEADS, nj, BQ, HEAD_DIM)
    ones = jnp.ones(vh4.shape[:-1] + (1,), BF16)
    va4 = jnp.concatenate([vh4, ones, jnp.zeros(vh4.shape[:-1] + (HEAD_DIM - 1,), BF16)], axis=-1)
    if gather is None:
        (o_f, lse), (o_s, rsave), gathered = (fox_fwd(qh, kh4, va4, fcol, frow4, name="fox_fwd"),
                                              sb_fwd(qh, kh4, va4, cst["tri_gt"], name="sb_fwd"), None)
    else:
        o_f, lse, gathered_a = fox_fwd(qh, kh4, va4, fcol, frow4, name="fox_fwd_gather", gather=gather[0])
        o_s, rsave, gathered_b = sb_fwd(qh, kh4, va4, cst["tri_gt"], name="sb_fwd_gather", gather=gather[1])
        gathered = gather[2]((gathered_a, gathered_b))
        w = gathered[0][0]
    o = _from_heads(jnp.concatenate([o_f.astype(BF16), o_s], axis=0))
    h2 = mm_nn(o, w["w_out"], add=h, name="mix_out_proj")
    saved = dict(h=h, xn=xn, proj=proj, logf3=logf3, fcol=fcol, frow4=frow4, qh=qh, kh4=kh4, vh4=vh4,
                 o_f=o_f, lse=lse, rsave=rsave, o=o)
    return h2, saved, gathered


def attn_layer_bwd(dh, w, s, cst, scatter=None):
    t = dh.shape[0]
    dh3 = dh[None]
    do = mm_nt(dh3, w["w_out"], out_dtype=BF16, name="mix_out_bwd_bf16")
    g_w_out = mm_tn(s["o"], dh3, name="mix_out_wgrad")
    doh = _to_heads(do)
    zeros = jnp.zeros_like(doh)
    qa = jnp.concatenate([s["qh"], zeros], axis=-1)
    doa = jnp.concatenate([zeros, doh], axis=-1)
    fox_args = (s["qh"], s["kh4"], s["vh4"], qa, doa, s["fcol"], s["frow4"], s["o_f"], doh, s["lse"])
    sb_args = (s["qh"], s["kh4"], s["vh4"], qa, doa, cst["tri_gt"], doh, s["rsave"])
    if scatter is None:
        (dq_f, dkv_f, dfk), (dq_s, dkv_s), landed = fox_bwd(*fox_args, name="fox_bwd"), sb_bwd(*sb_args, name="sb_bwd"), None
    else:
        chunks_a, chunks_b = scatter(g_w_out)
        dq_f, dkv_f, dfk, landed_a = fox_bwd(*fox_args, name="fox_bwd_scatter", scatter=chunks_a)
        dq_s, dkv_s, landed_b = sb_bwd(*sb_args, name="sb_bwd_scatter", scatter=chunks_b)
        landed = (landed_a, landed_b)
    dcum3 = _lanes_to_chunks(dfk.reshape(H_FOX, t))
    dfl3, dbias = gate_cumsum_bwd(dcum3, s["logf3"], cst["tri_ge"], name="gate_cumsum_bwd")
    dfl = jnp.pad(_chunks_to_lanes(dfl3).T, ((0, 0), (0, 128 - H_FOX))).astype(BF16)
    wide = (H_FOX, t, 2 * HEAD_DIM)
    dproj, dgq, dgk = attn_prep_bwd(s["proj"], dq_f, dq_s, dkv_f.reshape(wide), dkv_s.reshape(wide), dfl, w["gq"],
                                    w["gk"], cst["bd"], name="attn_prep_bwd")
    g_w_in = mm_tn(s["xn"], dproj[None], tn=640, name="attn_in_wgrad")[:, :ATTN_IN]
    dh2, g_norm = mm_nt_rms_bwd(dproj[None], w["w_in"], s["h"], w["norm"], dh, name="attn_in_bwd")
    dgq = dgq.reshape(N_HEADS, HEAD_DIM)
    dgk = dgk.reshape(N_HEADS, HEAD_DIM)
    grads = dict(norm=g_norm[0], w_in=g_w_in, f_bias=dbias[:, 0], fox_q=dgq[:H_FOX].sum(0), fox_k=dgk[:H_FOX].sum(0),
                 sb_q=dgq[H_FOX:].sum(0), sb_k=dgk[H_FOX:].sum(0), w_out=g_w_out)
    return dh2, grads, landed


def conv_layer_fwd(h, w):
    xn, proj3 = rms_mm_nn(h, w["norm"], w["w_in"], parts=3, name="conv_in_proj")
    y = conv_mix_fwd(proj3, w["ck"], name="conv_mix_fwd")
    h2 = mm_nn(y, w["w_out"], add=h, name="mix_out_proj")
    return h2, dict(h=h, xn=xn, proj3=proj3, y=y)


def conv_layer_bwd(dh, w, s):
    dh3 = dh[None]
    g_w_out = mm_tn(s["y"], dh3, name="mix_out_wgrad")
    dproj3, dck = conv_mix_bwd(dh, w["w_out"], s["proj3"], w["ck"], name="conv_mix_bwd")
    g_w_in = mm_tn(s["xn"], dproj3, name="conv_in_wgrad")
    dh2, g_norm = mm_nt_rms_bwd(dproj3, w["w_in"], s["h"], w["norm"], dh, name="conv_in_bwd")
    return dh2, dict(norm=g_norm[0], w_in=g_w_in, ck=dck[0, :3], w_out=g_w_out)


def ffn_layer_fwd(h, w):
    xn, up2 = rms_mm_nn(h, w["norm"], w["w_up"], parts=2, tn=1408, name="ffn_up_proj")
    h2, act = ffn_act_down_fwd(up2, w["cw2"], w["w_down"], h, name="ffn_act_down_fwd")
    return h2, dict(h=h, xn=xn, up2=up2, act=act)


def ffn_layer_bwd(dh, w, s):
    dh3 = dh[None]
    g_w_down = mm_tn(s["act"], dh3, tk=1408, name="ffn_down_wgrad")
    dup2, dcw = ffn_act_bwd(dh, w["w_down"], s["up2"], w["cw2"], name="ffn_act_bwd")
    g_w_up = mm_tn(s["xn"], dup2, tn=1408, name="ffn_up_wgrad")
    dh2, g_norm = mm_nt_rms_bwd(dup2, w["w_up"], s["h"], w["norm"], dh, name="ffn_up_bwd")
    g_cw = jnp.concatenate([dcw[0, :3], dcw[1, :3]], axis=1)
    return dh2, dict(norm=g_norm[0], w_up=g_w_up, cw=g_cw, w_down=g_w_down)


def forward_backward(x, target, wa, wc, wf, *, late_weights=None, late_chunks=None):
    cst = _constants()
    h = x
    saved = []
    layer = 0
    while layer == 0 or layer < len(wf):
        i = layer // 2
        if layer % 2 == 0:
            h, sm, built = attn_layer_fwd(h, wa[i], cst, gather=late_weights if late_weights and layer == 0 else None)
            if built is not None:
                wa, wc, wf = built
        else:
            h, sm = conv_layer_fwd(h, wc[i])
        h, sf = ffn_layer_fwd(h, wf[layer])
        saved.append((sm, sf))
        layer += 1
    depth = len(wf)
    loss_blk, dh = loss_head(h, target, name="loss_head")
    ga, gc, gf = [None] * len(wa), [None] * len(wc), [None] * depth
    landed = None
    for layer in reversed(range(depth)):
        i = layer // 2
        sm, sf = saved[layer]
        dh, gf[layer] = ffn_layer_bwd(dh, wf[layer], sf)
        if layer % 2 == 0:
            chunks = None
            if late_chunks and layer == 0:
                chunks = lambda g_w_out: late_chunks([dict(w_out=g_w_out)] + ga[1:], gc, gf)
            dh, ga[i], got = attn_layer_bwd(dh, wa[i], sm, cst, scatter=chunks)
            landed = got if got is not None else landed
        else:
            dh, gc[i] = conv_layer_bwd(dh, wc[i], sm)
    return loss_blk, dh, ga, gc, gf, landed


def _part_rows(shape, width, row_mult):
    n = 1
    for s in shape:
        n *= s
    rows = -(-n // width)
    return -(-rows // row_mult) * row_mult


def _pack_rows(arrs, width, row_mult, dtype, total_rows=None, lead=0):
    parts = []
    used = 0
    for a in arrs:
        outer = a.shape[:lead]
        rows = _part_rows(a.shape[lead:], width, row_mult)
        flat = a.astype(dtype).reshape(outer + (-1,))
        flat = jnp.pad(flat, ((0, 0),) * lead + ((0, rows * width - flat.shape[-1]),))
        parts.append(flat.reshape(outer + (rows, width)))
        used += rows
    if total_rows is not None and total_rows > used:
        parts.append(jnp.zeros(parts[0].shape[:lead] + (total_rows - used, width), dtype))
    return jnp.concatenate(parts, axis=lead)


def _unpack_rows(packed, shapes, width, row_mult, lead=0):
    outer = packed.shape[:lead]
    out = []
    off = 0
    for shape in shapes:
        rows = _part_rows(shape, width, row_mult)
        n = 1
        for s in shape:
            n *= s
        flat = lax.slice_in_dim(packed, off, off + rows, axis=lead).reshape(outer + (-1,))
        out.append(lax.slice_in_dim(flat, 0, n, axis=lead).reshape(outer + tuple(shape)))
        off += rows
    return out


BIG_NAMES = ("attn_w_in", "attn_w_out", "conv_w_in", "conv_w_out", "ffn_w_up", "ffn_w_down")
BIG_AXIS = {"attn_w_in": 2, "attn_w_out": 1, "conv_w_in": 2, "conv_w_out": 1, "ffn_w_up": 2, "ffn_w_down": 1}
BIG_WIDTH = 1024
BIG_ROW_MULT = 16
BIG_TILE = 512
SMALL_TILE = 128
SMALL_SHARDED = ("conv_norm", "conv_kernel", "ffn_conv")
SMALL_AXIS = {"conv_norm": 1, "conv_kernel": 2, "ffn_conv": 2}
SMALL_REPLICATED = ("attn_norm", "attn_f_bias", "fox_q_gain", "fox_k_gain", "sb_q_gain", "sb_k_gain", "ffn_norm")
WEIGHT_ORDER = ("attn_norm", "attn_w_in", "attn_f_bias", "fox_q_gain", "fox_k_gain", "sb_q_gain", "sb_k_gain",
                "attn_w_out", "conv_norm", "conv_w_in", "conv_kernel", "conv_w_out", "ffn_norm", "ffn_w_up",
                "ffn_conv", "ffn_w_down")


def _big_total_rows(shapes):
    used = sum(_part_rows(s, BIG_WIDTH, BIG_ROW_MULT) for s in shapes)
    tile = BIG_TILE if used >= 8 * BIG_TILE else SMALL_TILE
    return -(-used // tile) * tile


def _place():
    x, y, c = lax.axis_index("x"), lax.axis_index("y"), lax.axis_index("c")
    other_chips = [(1 - x, y), (x, 1 - y), (1 - x, 1 - y)]
    return x, y, c, other_chips


_ANY = pl.BlockSpec(memory_space=pl.ANY)


def _chip_sems():
    return [pltpu.SemaphoreType.DMA((3,)), pltpu.SemaphoreType.DMA((3,)), pltpu.SemaphoreType.DMA]


def _chip_gather(src_ref, dst_ref, send_sems, recv_sems, local_sem):
    x, y, c, chips = _place()
    k = 2 * x + y

    def copy(j, slot):
        px, py = chips[j]
        return pltpu.make_async_remote_copy(src_ref=src_ref, dst_ref=dst_ref.at[slot], send_sem=send_sems.at[j],
                                            recv_sem=recv_sems.at[j], device_id=(px, py, c), device_id_type=MESH)

    def local():
        return pltpu.make_async_copy(src_ref, dst_ref.at[k], local_sem)

    def start():
        local().start()
        for j in range(3):
            copy(j, k).start()

    def finish():
        for j, (px, py) in enumerate(chips):
            copy(j, 2 * px + py).wait_recv()
        for j in range(3):
            copy(j, k).wait_send()
        local().wait()

    return start, finish


def _chip_scatter(g_ref, o_ref, send_sems, recv_sems, local_sem):
    x, y, c, chips = _place()
    k = 2 * x + y

    def copy(j, src_slot, dst_slot):
        px, py = chips[j]
        return pltpu.make_async_remote_copy(src_ref=g_ref.at[src_slot], dst_ref=o_ref.at[dst_slot],
                                            send_sem=send_sems.at[j], recv_sem=recv_sems.at[j],
                                            device_id=(px, py, c), device_id_type=MESH)

    def local():
        return pltpu.make_async_copy(g_ref.at[k], o_ref.at[k], local_sem)

    def start():
        local().start()
        for j, (px, py) in enumerate(chips):
            copy(j, 2 * px + py, k).start()

    def finish():
        for j, (px, py) in enumerate(chips):
            copy(j, k, 2 * px + py).wait_recv()
        for j, (px, py) in enumerate(chips):
            copy(j, 2 * px + py, k).wait_send()
        local().wait()

    return start, finish


def gather_chips(arrs, *, name):
    n = len(arrs)

    def body(*refs):
        hooks = [_chip_gather(refs[m], refs[n + m], *refs[2 * n + 3 * m:2 * n + 3 * m + 3]) for m in range(n)]
        for start, _ in hooks:
            start()
        for _, finish in hooks:
            finish()

    return pl.pallas_call(
        body, in_specs=[_ANY] * n, out_specs=[_ANY] * n,
        out_shape=[jax.ShapeDtypeStruct((N_CHIPS,) + a.shape, a.dtype) for a in arrs],
        scratch_shapes=_chip_sems() * n, name=name)(*arrs)


def scatter_chips(chunks, *, name):
    def body(g_ref, o_ref, send_sems, recv_sems, local_sem):
        start, finish = _chip_scatter(g_ref, o_ref, send_sems, recv_sems, local_sem)
        start()
        finish()

    return pl.pallas_call(
        body, in_specs=[_ANY], out_specs=_ANY, out_shape=jax.ShapeDtypeStruct(chunks.shape, chunks.dtype),
        scratch_shapes=_chip_sems(), name=name)(chunks)


def swap_cores(arrs, *, name):
    n = len(arrs)

    def body(*refs):
        x, y, c, _ = _place()
        copies = [pltpu.make_async_remote_copy(src_ref=refs[m], dst_ref=refs[n + m], send_sem=refs[2 * n + 2 * m],
                                               recv_sem=refs[2 * n + 2 * m + 1], device_id=(x, y, 1 - c),
                                               device_id_type=MESH) for m in range(n)]
        for cp in copies:
            cp.start()
        for cp in copies:
            cp.wait()

    return pl.pallas_call(
        body, in_specs=[_ANY] * n, out_specs=[_ANY] * n,
        out_shape=[jax.ShapeDtypeStruct(a.shape, a.dtype) for a in arrs],
        scratch_shapes=[pltpu.SemaphoreType.DMA, pltpu.SemaphoreType.DMA] * n, name=name)(*arrs)


def allreduce_small(p, *, name):
    r, w = p.shape

    def body(p_ref, o_ref, buf, send_sems, recv_sems):
        x, y, c, _ = _place()
        me = 4 * x + 2 * y + c
        buf[me] = p_ref[...]

        def peer_of(m):
            return (1 - x if m & 4 else x, 1 - y if m & 2 else y, 1 - c if m & 1 else c)

        def copy(m, slot):
            return pltpu.make_async_remote_copy(src_ref=p_ref, dst_ref=buf.at[slot], send_sem=send_sems.at[m - 1],
                                                recv_sem=recv_sems.at[m - 1], device_id=peer_of(m),
                                                device_id_type=MESH)

        sends = [copy(m, me) for m in range(1, 8)]
        for cp in sends:
            cp.start()
        for m in range(1, 8):
            px, py, pc = peer_of(m)
            copy(m, 4 * px + 2 * py + pc).wait_recv()
        for cp in sends:
            cp.wait_send()
        acc = buf[0]
        for d in range(1, 8):
            acc = acc + buf[d]
        o_ref[...] = acc

    vm = pl.BlockSpec(memory_space=pltpu.VMEM)
    return pl.pallas_call(
        body, in_specs=[vm], out_specs=vm, out_shape=jax.ShapeDtypeStruct((r, w), F32),
        scratch_shapes=[pltpu.VMEM((8, r, w), F32), pltpu.SemaphoreType.DMA((7,)), pltpu.SemaphoreType.DMA((7,))],
        name=name)(p)


def sum_chips(rv, *, name):
    _, r, w = rv.shape
    tile = BIG_TILE if r % BIG_TILE == 0 else SMALL_TILE

    def body(a_ref, b_ref, c_ref, d_ref, o_ref):
        o_ref[...] = ((a_ref[0].astype(F32) + b_ref[0].astype(F32)) + c_ref[0].astype(F32)) + d_ref[0].astype(F32)

    spec = lambda kk: pl.BlockSpec((1, tile, w), lambda i: (kk, i, 0))
    return pl.pallas_call(
        body, grid=(r // tile,), in_specs=[spec(0), spec(1), spec(2), spec(3)],
        out_specs=pl.BlockSpec((tile, w), lambda i: (i, 0)), out_shape=jax.ShapeDtypeStruct((r, w), F32),
        compiler_params=_params(), name=name)(rv, rv, rv, rv)


def add_pair(a, b, *, name):
    r, w = a.shape
    tile = BIG_TILE if r % BIG_TILE == 0 else SMALL_TILE

    def body(a_ref, b_ref, o_ref):
        o_ref[...] = a_ref[...] + b_ref[...]

    spec = pl.BlockSpec((tile, w), lambda i: (i, 0))
    return pl.pallas_call(body, grid=(r // tile,), in_specs=[spec, spec], out_specs=spec,
                          out_shape=jax.ShapeDtypeStruct((r, w), F32), compiler_params=_params(), name=name)(a, b)


def adamw(w, g, m, v, *, tm, name):
    r, c = w.shape
    assert r % tm == 0

    def body(w_ref, g_ref, m_ref, v_ref, d_ref, nm_ref, nv_ref):
        g_ = g_ref[...]
        m_ = ADAM_B1 * m_ref[...] + (1.0 - ADAM_B1) * g_
        v_ = ADAM_B2 * v_ref[...] + (1.0 - ADAM_B2) * (g_ * g_)
        m_hat = m_ / (1.0 - ADAM_B1 ** ADAM_STEP)
        v_hat = v_ / (1.0 - ADAM_B2 ** ADAM_STEP)
        d_ref[...] = -ADAM_LR * (m_hat / (jnp.sqrt(v_hat) + ADAM_EPS) + ADAM_WD * w_ref[...])
        nm_ref[...] = m_
        nv_ref[...] = v_

    spec = pl.BlockSpec((tm, c), lambda i: (i, 0))
    return pl.pallas_call(body, grid=(r // tm,), in_specs=[spec] * 4, out_specs=[spec] * 3,
                          out_shape=[jax.ShapeDtypeStruct((r, c), F32)] * 3, compiler_params=_params(), name=name)(w, g, m, v)


def kernel(x, attn_norm, attn_w_in, attn_f_bias, fox_q_gain, fox_k_gain, sb_q_gain, sb_k_gain, attn_w_out, conv_norm, conv_w_in, conv_kernel, conv_w_out, ffn_norm, ffn_w_up, ffn_conv, ffn_w_down, loss_target, m_attn_norm, m_attn_w_in, m_attn_f_bias, m_fox_q_gain, m_fox_k_gain, m_sb_q_gain, m_sb_k_gain, m_attn_w_out, m_conv_norm, m_conv_w_in, m_conv_kernel, m_conv_w_out, m_ffn_norm, m_ffn_w_up, m_ffn_conv, m_ffn_w_down, v_attn_norm, v_attn_w_in, v_attn_f_bias, v_fox_q_gain, v_fox_k_gain, v_sb_q_gain, v_sb_k_gain, v_attn_w_out, v_conv_norm, v_conv_w_in, v_conv_kernel, v_conv_w_out, v_ffn_norm, v_ffn_w_up, v_ffn_conv, v_ffn_w_down):
    a = dict(locals())
    chip = 2 * lax.axis_index("x") + lax.axis_index("y")
    n_attn, n_conv, depth = attn_norm.shape[0], conv_norm.shape[0], ffn_norm.shape[0]

    units = [(name, l) for name in BIG_NAMES for l in range(a[name].shape[0])]
    early = [("attn_w_in", 0)]
    late = [u for u in units if u not in early]
    late_b = [("attn_w_out", 0), ("conv_w_in", n_conv - 1), ("conv_w_out", n_conv - 1), ("ffn_w_up", depth - 1),
              ("ffn_w_down", depth - 1)]
    late_a = [u for u in late if u not in late_b]
    late_sb = [("ffn_w_up", 0), ("ffn_w_down", 0), ("ffn_w_up", 1), ("ffn_w_down", 1)]
    late_sa = [u for u in late if u not in late_sb]

    def unit_shape(u):
        return a[u[0]].shape[1:]

    def pack_units(us, get, lead=0):
        return _pack_rows([get(u) for u in us], BIG_WIDTH, BIG_ROW_MULT, BF16, _big_total_rows([unit_shape(u) for u in us]),
                          lead=lead)

    def unpack_units(packed, us, lead=0):
        return dict(zip(us, _unpack_rows(packed, [unit_shape(u) for u in us], BIG_WIDTH, BIG_ROW_MULT, lead=lead)))

    def full_units(gathered, us):
        full_size = {}
        for u, g4 in unpack_units(gathered, us, lead=1).items():
            _, rows, cols = g4.shape
            if BIG_AXIS[u[0]] - 1 == 0:
                full_size[u] = g4.reshape(N_CHIPS * rows, cols)
            else:
                full_size[u] = g4.transpose(1, 0, 2).reshape(rows, N_CHIPS * cols)
        return full_size

    def shard(u):
        return a[u[0]][u[1]]

    small_shapes = [a[n].shape for n in SMALL_SHARDED]
    packed_s = _pack_rows([a[n] for n in SMALL_SHARDED], 128, 8, F32)
    gath_e, gath_s = gather_chips([pack_units(early, shard), packed_s], name="gather_weights")
    full_e = full_units(gath_e, early)
    full = {}
    per_chip = [_unpack_rows(gath_s[kk], small_shapes, 128, 8) for kk in range(N_CHIPS)]
    for n, name in enumerate(SMALL_SHARDED):
        full[name] = jnp.concatenate([per_chip[kk][n] for kk in range(N_CHIPS)], axis=SMALL_AXIS[name])

    def attn_weights(i, fu):
        return dict(
            norm=attn_norm[i][None],
            w_in=jnp.pad(fu[("attn_w_in", i)], ((0, 0), (0, ATTN_IN_PAD - ATTN_IN))),
            fbias=jnp.pad(attn_f_bias[i], (0, 128 - H_FOX))[None],
            gq=jnp.concatenate([jnp.tile(fox_q_gain[i], H_FOX), jnp.tile(sb_q_gain[i], H_SB)])[None],
            gk=jnp.concatenate([jnp.tile(fox_k_gain[i], H_FOX), jnp.tile(sb_k_gain[i], H_SB)])[None],
            w_out=fu.get(("attn_w_out", i)))

    def build_weights(gathered):
        fu = {**full_e, **full_units(gathered[0], late_a), **full_units(gathered[1], late_b)}
        wa = [attn_weights(i, fu) for i in range(n_attn)]
        wc = [dict(norm=full["conv_norm"][i][None], w_in=fu[("conv_w_in", i)], ck=full["conv_kernel"][i][None],
                   w_out=fu[("conv_w_out", i)]) for i in range(n_conv)]
        wf = []
        for l in range(depth):
            cw = full["ffn_conv"][l]
            wf.append(dict(norm=ffn_norm[l][None], w_up=fu[("ffn_w_up", l)], cw2=jnp.stack([cw[:, :D_FF], cw[:, D_FF:]]),
                           w_down=fu[("ffn_w_down", l)]))
        return wa, wc, wf

    def chunks_of_unit(u, ga, gc, gf):
        name, l = u
        g = {"attn_w_in": lambda: ga[l]["w_in"], "attn_w_out": lambda: ga[l]["w_out"],
             "conv_w_in": lambda: gc[l]["w_in"], "conv_w_out": lambda: gc[l]["w_out"],
             "ffn_w_up": lambda: gf[l]["w_up"], "ffn_w_down": lambda: gf[l]["w_down"]}[name]()
        rows, cols = unit_shape(u)
        if BIG_AXIS[name] - 1 == 0:
            return g.reshape(N_CHIPS, rows, cols)
        return g.reshape(rows, N_CHIPS, cols).transpose(1, 0, 2)

    def chunks_of(us, ga, gc, gf):
        return pack_units(us, lambda u: chunks_of_unit(u, ga, gc, gf), lead=1)

    loss_blk, grad_x, ga, gc, gf, landed_late = forward_backward(
        x[0], loss_target[0], [attn_weights(0, full_e)], [], [],
        late_weights=(pack_units(late_a, shard), pack_units(late_b, shard), build_weights),
        late_chunks=lambda ga, gc, gf: (chunks_of(late_sa, ga, gc, gf), chunks_of(late_sb, ga, gc, gf)))

    landed = [scatter_chips(chunks_of(early, ga, gc, gf), name="scatter_grads"), landed_late[0], landed_late[1]]
    mine = [sum_chips(buf, name="sum_chips") for buf in landed]
    theirs = swap_cores(mine, name="swap_cores")
    g_units = {}
    for us, m, th in zip((early, late_sa, late_sb), mine, theirs):
        g_units.update(unpack_units(add_pair(m, th, name="add_cores"), us))
    grads = {name: jnp.stack([g_units[(name, l)] for l in range(a[name].shape[0])]) for name in BIG_NAMES}

    small_full = [
        loss_blk,
        jnp.stack([g["norm"] for g in ga]), jnp.stack([g["f_bias"] for g in ga]),
        jnp.stack([g["fox_q"] for g in ga]), jnp.stack([g["fox_k"] for g in ga]),
        jnp.stack([g["sb_q"] for g in ga]), jnp.stack([g["sb_k"] for g in ga]),
        jnp.stack([g["norm"] for g in gf]),
        jnp.stack([g["norm"] for g in gc]), jnp.stack([g["ck"] for g in gc]), jnp.stack([g["cw"] for g in gf]),
    ]
    summed = allreduce_small(_pack_rows(small_full, 128, 8, F32), name="allreduce_small")
    parts = _unpack_rows(summed, [p.shape for p in small_full], 128, 8)
    loss = parts[0][0, 0]
    for name, g in zip(SMALL_REPLICATED, parts[1:8]):
        grads[name] = g
    for name, g in zip(SMALL_SHARDED, parts[8:]):
        width = a[name].shape[SMALL_AXIS[name]]
        grads[name] = lax.dynamic_slice_in_dim(g, chip * width, width, axis=SMALL_AXIS[name])

    delta, new_m, new_v = {}, {}, {}
    for name in BIG_NAMES:
        shape = a[name].shape
        flat = lambda arr: arr.reshape(-1, shape[-1])
        d_, m_, v_ = adamw(flat(a[name]), flat(grads[name]), flat(a["m_" + name]), flat(a["v_" + name]), tm=256,
                           name="adamw")
        delta[name], new_m[name], new_v[name] = d_.reshape(shape), m_.reshape(shape), v_.reshape(shape)
    small_names = SMALL_REPLICATED + SMALL_SHARDED
    small_shapes_local = [a[n].shape for n in small_names]
    pack = lambda prefix, src: _pack_rows([src[prefix + n] for n in small_names], 128, 8, F32)
    packed = adamw(pack("", a), pack("", grads), pack("m_", a), pack("v_", a), tm=8, name="adamw_small")
    for store, buf in zip((delta, new_m, new_v), packed):
        for name, arr in zip(small_names, _unpack_rows(buf, small_shapes_local, 128, 8)):
            store[name] = arr

    return (loss, grad_x[None], *[grads[n] for n in WEIGHT_ORDER], *[delta[n] for n in WEIGHT_ORDER],
            *[new_m[n] for n in WEIGHT_ORDER], *[new_v[n] for n in WEIGHT_ORDER])
```

```python
import functools

import jax
import jax.numpy as jnp
from jax import lax
from jax.experimental import pallas as pl
from jax.experimental.pallas import tpu as pltpu

F32 = jnp.float32
BF16 = jnp.bfloat16

D_MODEL = 1024
HEAD_DIM = 64
H_FOX = 8
H_SB = 8
N_HEADS = H_FOX + H_SB
MIX = N_HEADS * HEAD_DIM
ATTN_IN = 3 * MIX + H_FOX
ATTN_IN_PAD = 3 * MIX + 128
D_FF = 2816
EPS = 1e-6
SCALE = HEAD_DIM ** -0.5
NEG = -1e30

ADAM_LR = 0.001
ADAM_B1 = 0.9
ADAM_B2 = 0.999
ADAM_EPS = 1e-08
ADAM_WD = 0.01
ADAM_STEP = 10

VMEM_LIMIT = 56 * 1024 * 1024
HALO = 8
BQ = 512
N_CHIPS = 4
MESH = pl.DeviceIdType.MESH


def _params(**kw):
    return pltpu.CompilerParams(vmem_limit_bytes=VMEM_LIMIT, **kw)


def _dot(a, b):
    return jnp.dot(a, b, preferred_element_type=F32)


def _dot_nt(a, b):
    return lax.dot_general(a, b, (((1,), (1,)), ((), ())), preferred_element_type=F32)


def _dot_tn(a, b):
    return lax.dot_general(a, b, (((0,), (0,)), ((), ())), preferred_element_type=F32)


def _split2(x):
    hi = x.astype(BF16)
    lo = (x - hi.astype(F32)).astype(BF16)
    return hi, lo


def _split3(x):
    hi = x.astype(BF16)
    r = x - hi.astype(F32)
    mid = r.astype(BF16)
    lo = (r - mid.astype(F32)).astype(BF16)
    return hi, mid, lo


def mm_nn(a, b, *, add=None, out_dtype=F32, parts=1, tm=1024, tn=512, name):
    m, k = a.shape
    n = b.shape[1]
    np_ = n // parts
    nb = np_ // tn
    tm = min(tm, m)
    assert m % tm == 0 and np_ % tn == 0

    def body(*refs):
        if add is None:
            a_ref, b_ref, o_ref = refs
            acc = _dot(a_ref[...].astype(BF16), b_ref[...])
        else:
            a_ref, b_ref, r_ref, o_ref = refs
            acc = _dot(a_ref[...].astype(BF16), b_ref[...]) + r_ref[...]
        o_ref[...] = acc.astype(out_dtype).reshape(o_ref.shape)

    in_specs = [pl.BlockSpec((tm, k), lambda i, j: (i, 0)), pl.BlockSpec((k, tn), lambda i, j: (0, j))]
    args = [a, b]
    if add is not None:
        in_specs.append(pl.BlockSpec((tm, tn), lambda i, j: (i, j)))
        args.append(add)
    if parts == 1:
        out_spec = pl.BlockSpec((tm, tn), lambda i, j: (i, j))
        out_shape = jax.ShapeDtypeStruct((m, n), out_dtype)
    else:
        out_spec = pl.BlockSpec((1, tm, tn), lambda i, j: (j // nb, i, j % nb))
        out_shape = jax.ShapeDtypeStruct((parts, m, np_), out_dtype)
    return pl.pallas_call(body, grid=(m // tm, n // tn), in_specs=in_specs, out_specs=out_spec,
                          out_shape=out_shape, compiler_params=_params(), name=name)(*args)


def mm_nt(a3, b, *, out_dtype=F32, tm=1024, tn=512, name):
    p, m, kp = a3.shape
    n = b.shape[0]
    tm = min(tm, m)
    assert m % tm == 0 and n % tn == 0 and b.shape[1] == p * kp

    def body(a_ref, b_ref, o_ref, acc_ref):
        part = pl.program_id(2)
        prod = _dot_nt(a_ref[0].astype(BF16), b_ref[...])

        @pl.when(part == 0)
        def _():
            acc_ref[...] = prod

        @pl.when(part > 0)
        def _():
            acc_ref[...] += prod

        @pl.when(part == p - 1)
        def _():
            o_ref[...] = acc_ref[...].astype(out_dtype)

    return pl.pallas_call(
        body, grid=(m // tm, n // tn, p),
        in_specs=[pl.BlockSpec((1, tm, kp), lambda i, j, q: (q, i, 0)), pl.BlockSpec((tn, kp), lambda i, j, q: (j, q))],
        out_specs=pl.BlockSpec((tm, tn), lambda i, j, q: (i, j)),
        out_shape=jax.ShapeDtypeStruct((m, n), out_dtype),
        scratch_shapes=[pltpu.VMEM((tm, tn), F32)],
        compiler_params=_params(), name=name)(a3, b)


def mm_tn(a, b3, *, tk=512, tn=512, tt=2048, name):
    t, k = a.shape
    p, _, np_ = b3.shape
    nb = np_ // tn
    tt = min(tt, t)
    assert t % tt == 0 and k % tk == 0 and np_ % tn == 0

    def body(a_ref, b_ref, o_ref):
        prod = _dot_tn(a_ref[...].astype(BF16), b_ref[0].astype(BF16))

        @pl.when(pl.program_id(2) == 0)
        def _():
            o_ref[...] = prod

        @pl.when(pl.program_id(2) > 0)
        def _():
            o_ref[...] += prod

    return pl.pallas_call(
        body, grid=(k // tk, p * nb, t // tt),
        in_specs=[pl.BlockSpec((tt, tk), lambda i, j, s: (s, i)), pl.BlockSpec((1, tt, tn), lambda i, j, s: (j // nb, s, j % nb))],
        out_specs=pl.BlockSpec((tk, tn), lambda i, j, s: (i, j)),
        out_shape=jax.ShapeDtypeStruct((k, p * np_), F32),
        compiler_params=_params(), name=name)(a, b3)


def rms_mm_nn(h, g, b, *, parts=1, tm=1024, tn=512, name):
    t, d = h.shape
    n = b.shape[1]
    np_ = n // parts
    nb = np_ // tn
    tm = min(tm, t)
    assert t % tm == 0 and np_ % tn == 0

    def body(h_ref, g_ref, b_ref, xn_ref, o_ref):
        @pl.when(pl.program_id(1) == 0)
        def _():
            x = h_ref[...]
            r = lax.rsqrt(jnp.mean(x * x, axis=-1, keepdims=True) + EPS)
            xn_ref[...] = (x * r * g_ref[...]).astype(BF16)

        o_ref[...] = _dot(xn_ref[...], b_ref[...]).reshape(o_ref.shape)

    if parts == 1:
        out_spec = pl.BlockSpec((tm, tn), lambda i, j: (i, j))
        out_shape = jax.ShapeDtypeStruct((t, n), F32)
    else:
        out_spec = pl.BlockSpec((1, tm, tn), lambda i, j: (j // nb, i, j % nb))
        out_shape = jax.ShapeDtypeStruct((parts, t, np_), F32)
    row = pl.BlockSpec((tm, d), lambda i, j: (i, 0))
    return pl.pallas_call(
        body, grid=(t // tm, n // tn),
        in_specs=[row, pl.BlockSpec((1, d), lambda i, j: (0, 0)), pl.BlockSpec((d, tn), lambda i, j: (0, j))],
        out_specs=[row, out_spec], out_shape=[jax.ShapeDtypeStruct((t, d), BF16), out_shape],
        compiler_params=_params(), name=name)(h, g, b)


def mm_nt_rms_bwd(a3, b, h, g, dres, *, name, tm=512):
    p, t, kp = a3.shape
    d = b.shape[0]
    tm = min(tm, t)
    assert t % tm == 0 and b.shape[1] == p * kp

    def body(a_ref, b_ref, h_ref, g_ref, dres_ref, dh_ref, dg_ref, acc_ref):
        i = pl.program_id(0)
        part = pl.program_id(1)
        prod = _dot_nt(a_ref[0].astype(BF16), b_ref[...])

        @pl.when(part == 0)
        def _():
            acc_ref[...] = prod

        @pl.when(part > 0)
        def _():
            acc_ref[...] += prod

        @pl.when(part == p - 1)
        def _():
            x = h_ref[...]
            dy = acc_ref[...]
            r = lax.rsqrt(jnp.mean(x * x, axis=-1, keepdims=True) + EPS)
            gy = dy * g_ref[...]
            dot = jnp.mean(gy * x, axis=-1, keepdims=True)
            dh_ref[...] = dres_ref[...] + r * gy - x * (r * r * r * dot)
            _acc_rows(dg_ref, jnp.sum(dy * x * r, axis=0, keepdims=True), i == 0)

    row = pl.BlockSpec((tm, d), lambda i, q: (i, 0))
    vec = pl.BlockSpec((1, d), lambda i, q: (0, 0))
    return pl.pallas_call(
        body, grid=(t // tm, p),
        in_specs=[pl.BlockSpec((1, tm, kp), lambda i, q: (q, i, 0)), pl.BlockSpec((d, kp), lambda i, q: (0, q)),
                  row, vec, row],
        out_specs=[row, vec],
        out_shape=[jax.ShapeDtypeStruct((t, d), F32), jax.ShapeDtypeStruct((1, d), F32)],
        scratch_shapes=[pltpu.VMEM((tm, d), F32)], compiler_params=_params(), name=name)(a3, b, h, g, dres)


def _causal3(x, w):
    return w[0:1] * pltpu.roll(x, 2, 0) + w[1:2] * pltpu.roll(x, 1, 0) + w[2:3] * x


def _causal3_taps(x_ext, w, tm):
    x2 = pltpu.roll(x_ext, 2, 0)
    x1 = pltpu.roll(x_ext, 1, 0)
    y = w[0:1] * x2 + w[1:2] * x1 + w[2:3] * x_ext
    return y, (x2[HALO:HALO + tm], x1[HALO:HALO + tm], x_ext[HALO:HALO + tm])


def _anticausal3(z, w):
    n = z.shape[0]
    return w[2:3] * z + w[1:2] * pltpu.roll(z, n - 1, 0) + w[0:1] * pltpu.roll(z, n - 2, 0)


def _prev_spec(part, tm, tc, nrow8):
    del nrow8
    return pl.BlockSpec((1, HALO, tc), lambda j, i: (part, jnp.maximum(i * (tm // HALO) - 1, 0), j))


def _next_spec(part, tm, tc, nrow8):
    return pl.BlockSpec((1, HALO, tc), lambda j, i: (part, jnp.minimum((i + 1) * (tm // HALO), nrow8 - 1), j))


def _tile_spec(part, tm, tc):
    return pl.BlockSpec((1, tm, tc), lambda j, i: (part, i, j))


def _acc_rows(ref, val, first):
    @pl.when(first)
    def _():
        ref[...] = val

    @pl.when(jnp.logical_not(first))
    def _():
        ref[...] += val


def ffn_act_down_fwd(up2, cw2, w_down, h, *, name, tm=256, tc=1408):
    _, t, f = up2.shape
    d = h.shape[1]
    tm = min(tm, t)

    def body(g_ref, v_ref, gp_ref, vp_ref, w_ref, wd_ref, h_ref, o_ref, act_ref):
        keep = jnp.where(pl.program_id(0) == 0, 0.0, 1.0)
        for cc in range(f // tc):
            cols = slice(cc * tc, (cc + 1) * tc)
            g_ext = jnp.concatenate([gp_ref[0, :, cols] * keep, g_ref[0, :, cols]], axis=0)
            v_ext = jnp.concatenate([vp_ref[0, :, cols] * keep, v_ref[0, :, cols]], axis=0)
            ug = _causal3(g_ext, w_ref[0, :, cols])[HALO:]
            uv = _causal3(v_ext, w_ref[1, :, cols])[HALO:]
            act_ref[:, cols] = (ug * jax.nn.sigmoid(ug) * uv).astype(BF16)
        o_ref[...] = _dot(act_ref[...], wd_ref[...]) + h_ref[...]

    tile = lambda part: pl.BlockSpec((1, tm, f), lambda i: (part, i, 0))
    prev = lambda part: pl.BlockSpec((1, HALO, f), lambda i: (part, jnp.maximum(i * (tm // HALO) - 1, 0), 0))
    row = pl.BlockSpec((tm, d), lambda i: (i, 0))
    return pl.pallas_call(
        body, grid=(t // tm,),
        in_specs=[tile(0), tile(1), prev(0), prev(1), pl.BlockSpec((2, 3, f), lambda i: (0, 0, 0)),
                  pl.BlockSpec((f, d), lambda i: (0, 0)), row],
        out_specs=[row, pl.BlockSpec((tm, f), lambda i: (i, 0))],
        out_shape=[jax.ShapeDtypeStruct((t, d), F32), jax.ShapeDtypeStruct((t, f), BF16)],
        compiler_params=_params(), name=name)(up2, up2, up2, up2, cw2, w_down, h)


def ffn_act_bwd(dh, w_down, up2, cw2, *, name, tm=256, tc=1408):
    _, t, f = up2.shape
    d = dh.shape[1]
    n8 = t // HALO

    def body(d_ref, dn_ref, wd_ref, g_ref, v_ref, gp_ref, vp_ref, gn_ref, vn_ref, wg_ref, wv_ref, dup_ref, dw_ref):
        i = pl.program_id(1)
        first = i == 0
        keep_p = jnp.where(first, 0.0, 1.0)
        keep_n = jnp.where(i == pl.num_programs(1) - 1, 0.0, 1.0)
        wg = wg_ref[0]
        wv = wv_ref[0]
        g_ext = jnp.concatenate([gp_ref[0] * keep_p, g_ref[0], gn_ref[0]], axis=0)
        v_ext = jnp.concatenate([vp_ref[0] * keep_p, v_ref[0], vn_ref[0]], axis=0)
        dh_ext = jnp.concatenate([d_ref[...], dn_ref[...] * keep_n], axis=0)
        d_ext = _dot_nt(dh_ext.astype(BF16), wd_ref[...])
        ug, (g2, g1, g0) = _causal3_taps(g_ext, wg, tm)
        uv, (v2, v1, v0) = _causal3_taps(v_ext, wv, tm)
        ug = ug[HALO:]
        uv = uv[HALO:]
        s = jax.nn.sigmoid(ug)
        dg = d_ext * uv * (s * (1.0 + ug * (1.0 - s)))
        dv = d_ext * (ug * s)
        dup_ref[0] = _anticausal3(dg, wg)[:tm].astype(BF16)
        dup_ref[1] = _anticausal3(dv, wv)[:tm].astype(BF16)
        dgt = dg[:tm]
        dvt = dv[:tm]
        zero = jnp.zeros((HALO - 3, tc), F32)
        rows_g = [jnp.sum(dgt * x, axis=0, keepdims=True) for x in (g2, g1, g0)] + [zero]
        rows_v = [jnp.sum(dvt * x, axis=0, keepdims=True) for x in (v2, v1, v0)] + [zero]
        _acc_rows(dw_ref, jnp.stack([jnp.concatenate(rows_g, axis=0), jnp.concatenate(rows_v, axis=0)]), first)

    wspec = lambda part: pl.BlockSpec((1, 3, tc), lambda j, i: (part, 0, j))
    return pl.pallas_call(
        body, grid=(f // tc, t // tm),
        in_specs=[pl.BlockSpec((tm, d), lambda j, i: (i, 0)),
                  pl.BlockSpec((HALO, d), lambda j, i: (jnp.minimum((i + 1) * (tm // HALO), n8 - 1), 0)),
                  pl.BlockSpec((tc, d), lambda j, i: (j, 0)),
                  _tile_spec(0, tm, tc), _tile_spec(1, tm, tc), _prev_spec(0, tm, tc, n8), _prev_spec(1, tm, tc, n8),
                  _next_spec(0, tm, tc, n8), _next_spec(1, tm, tc, n8), wspec(0), wspec(1)],
        out_specs=[pl.BlockSpec((2, tm, tc), lambda j, i: (0, i, j)), pl.BlockSpec((2, HALO, tc), lambda j, i: (0, 0, j))],
        out_shape=[jax.ShapeDtypeStruct((2, t, f), BF16), jax.ShapeDtypeStruct((2, HALO, f), F32)],
        compiler_params=_params(), name=name)(dh, dh, w_down, up2, up2, up2, up2, up2, up2, cw2, cw2)


def conv_mix_fwd(proj3, ck, *, name, tm=512, tc=512):
    _, t, c = proj3.shape
    n8 = t // HALO

    def body(b_ref, c_ref, u_ref, cp_ref, up_ref, w_ref, o_ref):
        keep = jnp.where(pl.program_id(1) == 0, 0.0, 1.0)
        cu_ext = jnp.concatenate([cp_ref[0] * up_ref[0] * keep, c_ref[0] * u_ref[0]], axis=0)
        o_ref[...] = (b_ref[0] * _causal3(cu_ext, w_ref[0])[HALO:]).astype(BF16)

    return pl.pallas_call(
        body, grid=(c // tc, t // tm),
        in_specs=[_tile_spec(0, tm, tc), _tile_spec(1, tm, tc), _tile_spec(2, tm, tc), _prev_spec(1, tm, tc, n8),
                  _prev_spec(2, tm, tc, n8), pl.BlockSpec((1, 3, tc), lambda j, i: (0, 0, j))],
        out_specs=pl.BlockSpec((tm, tc), lambda j, i: (i, j)),
        out_shape=jax.ShapeDtypeStruct((t, c), BF16), compiler_params=_params(), name=name)(proj3, proj3, proj3, proj3, proj3, ck)


def conv_mix_bwd(dh, w_out, proj3, ck, *, name, tm=512, tc=512):
    _, t, c = proj3.shape
    d = dh.shape[1]
    n8 = t // HALO

    def body(d_ref, dn_ref, wo_ref, b_ref, c_ref, u_ref, cp_ref, up_ref, bn_ref, w_ref, dp_ref, dw_ref):
        i = pl.program_id(1)
        first = i == 0
        keep_p = jnp.where(first, 0.0, 1.0)
        keep_n = jnp.where(i == pl.num_programs(1) - 1, 0.0, 1.0)
        w = w_ref[0]
        cu_ext = jnp.concatenate([cp_ref[0] * up_ref[0] * keep_p, c_ref[0] * u_ref[0]], axis=0)
        cv, (x2, x1, x0) = _causal3_taps(cu_ext, w, tm)
        cv = cv[HALO:]
        dh_ext = jnp.concatenate([d_ref[...], dn_ref[...] * keep_n], axis=0)
        d_ext = _dot_nt(dh_ext.astype(BF16), wo_ref[...])
        dyt = d_ext[:tm]
        b_ext = jnp.concatenate([b_ref[0], bn_ref[0]], axis=0)
        dcv = d_ext * b_ext
        dcu = _anticausal3(dcv, w)[:tm]
        dp_ref[0] = (dyt * cv).astype(BF16)
        dp_ref[1] = (dcu * u_ref[0]).astype(BF16)
        dp_ref[2] = (dcu * c_ref[0]).astype(BF16)
        dcvt = dcv[:tm]
        rows = [jnp.sum(dcvt * x, axis=0, keepdims=True) for x in (x2, x1, x0)] + [jnp.zeros((HALO - 3, tc), F32)]
        _acc_rows(dw_ref, jnp.concatenate(rows, axis=0)[None], first)

    return pl.pallas_call(
        body, grid=(c // tc, t // tm),
        in_specs=[pl.BlockSpec((tm, d), lambda j, i: (i, 0)),
                  pl.BlockSpec((HALO, d), lambda j, i: (jnp.minimum((i + 1) * (tm // HALO), n8 - 1), 0)),
                  pl.BlockSpec((tc, d), lambda j, i: (j, 0)),
                  _tile_spec(0, tm, tc), _tile_spec(1, tm, tc), _tile_spec(2, tm, tc),
                  _prev_spec(1, tm, tc, n8), _prev_spec(2, tm, tc, n8),
                  _next_spec(0, tm, tc, n8), pl.BlockSpec((1, 3, tc), lambda j, i: (0, 0, j))],
        out_specs=[pl.BlockSpec((3, tm, tc), lambda j, i: (0, i, j)), pl.BlockSpec((1, HALO, tc), lambda j, i: (0, 0, j))],
        out_shape=[jax.ShapeDtypeStruct((3, t, c), BF16), jax.ShapeDtypeStruct((1, HALO, c), F32)],
        compiler_params=_params(), name=name)(dh, dh, w_out, proj3, proj3, proj3, proj3, proj3, proj3, ck)


def _head_sums(x, bd):
    hi, lo = _split2(x)
    return _dot(hi, bd) + _dot(lo, bd)


def attn_prep_fwd(proj, gq, gk, fbias, bd, *, name, tm=256):
    t = proj.shape[0]

    def body(q_ref, k_ref, v_ref, f_ref, gq_ref, gk_ref, fb_ref, bd_ref, qa_ref, ka_ref, va_ref, lf_ref):
        bd = bd_ref[...]
        lane = lax.broadcasted_iota(jnp.int32, (tm, 128), 1)
        low = lane < HEAD_DIM

        def two_heads(y, o_ref, c, rest):
            o_ref[2 * c] = jnp.where(low, y, rest).astype(BF16)
            o_ref[2 * c + 1] = jnp.where(low, pltpu.roll(y, HEAD_DIM, 1), rest).astype(BF16)

        def headnorm(x_ref, g_ref, o_ref, scale):
            for c in range(MIX // 128):
                sl = slice(128 * c, 128 * (c + 1))
                x = x_ref[:, sl]
                r = lax.rsqrt(_head_sums(x * x, bd) * (1.0 / HEAD_DIM) + EPS)
                two_heads(x * r * (g_ref[:, sl] * scale), o_ref, c, 0.0)

        headnorm(q_ref, gq_ref, qa_ref, SCALE)
        headnorm(k_ref, gk_ref, ka_ref, 1.0)
        one_at_64 = jnp.where(lane == HEAD_DIM, 1.0, 0.0)
        for c in range(MIX // 128):
            two_heads(v_ref[:, 128 * c:128 * (c + 1)], va_ref, c, one_at_64)
        fl = f_ref[...] + fb_ref[...]
        logf = jnp.minimum(fl, 0.0) - jnp.log(1.0 + jnp.exp(-jnp.abs(fl)))
        lf_ref[...] = logf.T[0:H_FOX, :]

    col = lambda c: pl.BlockSpec((tm, MIX), lambda i: (i, c))
    vec = pl.BlockSpec((1, MIX), lambda i: (0, 0))
    out = pl.BlockSpec((N_HEADS, tm, 2 * HEAD_DIM), lambda i: (0, i, 0))
    return pl.pallas_call(
        body, grid=(t // tm,),
        in_specs=[col(0), col(1), col(2), pl.BlockSpec((tm, 128), lambda i: (i, 3 * MIX // 128)), vec, vec,
                  pl.BlockSpec((1, 128), lambda i: (0, 0)), pl.BlockSpec((128, 128), lambda i: (0, 0))],
        out_specs=[out, out, out, pl.BlockSpec((H_FOX, tm), lambda i: (0, i))],
        out_shape=[jax.ShapeDtypeStruct((N_HEADS, t, 2 * HEAD_DIM), BF16)] * 3 + [jax.ShapeDtypeStruct((H_FOX, t), F32)],
        compiler_params=_params(), name=name)(proj, proj, proj, proj, gq, gk, fbias, bd)


def attn_prep_bwd(proj, dq_f, dq_s, dkv_f, dkv_s, dfl, gq, gk, bd, *, name, tm=256):
    t = proj.shape[0]
    per_group = H_FOX // 2

    def body(q_ref, k_ref, dqf_ref, dqs_ref, dkvf_ref, dkvs_ref, dfl_ref, gq_ref, gk_ref, bd_ref, dp_ref, dgq_ref,
             dgk_ref):
        bd = bd_ref[...]
        first = pl.program_id(0) == 0
        low = lax.broadcasted_iota(jnp.int32, (tm, 128), 1) < HEAD_DIM

        def two_heads(fox_ref, sb_ref, c):
            ref = fox_ref if c < per_group else sb_ref
            return ref[2 * (c % per_group)], ref[2 * (c % per_group) + 1]

        def low_halves(ab):
            return jnp.where(low, ab[0], pltpu.roll(ab[1], HEAD_DIM, 1))

        def high_halves(ab):
            return jnp.where(low, pltpu.roll(ab[0], HEAD_DIM, 1), ab[1])

        def back(x_ref, grad, g_ref, col0, scale, dg_ref):
            parts = []
            for c in range(MIX // 128):
                sl = slice(128 * c, 128 * (c + 1))
                x = x_ref[:, sl]
                r = lax.rsqrt(_head_sums(x * x, bd) * (1.0 / HEAD_DIM) + EPS)
                dn = grad(c) * scale
                gy = dn * g_ref[:, sl]
                hs = _head_sums(gy * x, bd) * (1.0 / HEAD_DIM)
                dp_ref[:, col0 + 128 * c:col0 + 128 * (c + 1)] = (r * gy - x * (r * r * r * hs)).astype(BF16)
                parts.append(jnp.sum(dn * x * r, axis=0, keepdims=True))
            _acc_rows(dg_ref, jnp.concatenate(parts, axis=1), first)

        back(q_ref, lambda c: low_halves(two_heads(dqf_ref, dqs_ref, c)), gq_ref, 0, SCALE, dgq_ref)
        back(k_ref, lambda c: low_halves(two_heads(dkvf_ref, dkvs_ref, c)), gk_ref, MIX, 1.0, dgk_ref)
        for c in range(MIX // 128):
            dp_ref[:, 2 * MIX + 128 * c:2 * MIX + 128 * (c + 1)] = high_halves(two_heads(dkvf_ref, dkvs_ref, c)).astype(BF16)
        dp_ref[:, 3 * MIX:] = dfl_ref[...]

    col = lambda c: pl.BlockSpec((tm, MIX), lambda i: (i, c))
    heads = pl.BlockSpec((H_FOX, tm, 128), lambda i: (0, i, 0))
    vec = pl.BlockSpec((1, MIX), lambda i: (0, 0))
    return pl.pallas_call(
        body, grid=(t // tm,),
        in_specs=[col(0), col(1), heads, heads, heads, heads, pl.BlockSpec((tm, 128), lambda i: (i, 0)), vec, vec,
                  pl.BlockSpec((128, 128), lambda i: (0, 0))],
        out_specs=[pl.BlockSpec((tm, ATTN_IN_PAD), lambda i: (i, 0)), vec, vec],
        out_shape=[jax.ShapeDtypeStruct((t, ATTN_IN_PAD), BF16), jax.ShapeDtypeStruct((1, MIX), F32),
                   jax.ShapeDtypeStruct((1, MIX), F32)],
        compiler_params=_params(), name=name)(proj, proj, dq_f, dq_s, dkv_f, dkv_s, dfl, gq, gk, bd)


def gate_cumsum(logf3, tri, *, name):
    nc, r, _ = logf3.shape

    def body(x_ref, tri_ref, o_ref):
        tri_m = tri_ref[...]

        def step(c, carry):
            hi, mid, lo = _split3(x_ref[c])
            cs = _dot(hi, tri_m) + _dot(mid, tri_m) + _dot(lo, tri_m) + carry
            o_ref[c] = cs
            return cs[:, 127:128]

        lax.fori_loop(0, nc, step, jnp.zeros((r, 1), F32))

    return pl.pallas_call(body, out_shape=jax.ShapeDtypeStruct(logf3.shape, F32), compiler_params=_params(),
                          name=name)(logf3, tri)


def gate_cumsum_bwd(dcum3, logf3, tri, *, name):
    nc, r, _ = dcum3.shape

    def body(x_ref, lf_ref, tri_ref, o_ref, s_ref):
        tri_m = tri_ref[...]

        def step(n, carry):
            car, tot = carry
            c = nc - 1 - n
            hi, mid, lo = _split3(x_ref[c])
            cs = _dot(hi, tri_m) + _dot(mid, tri_m) + _dot(lo, tri_m) + car
            dl = cs * (1.0 - jnp.exp(lf_ref[c]))
            o_ref[c] = dl
            return cs[:, 0:1], tot + dl

        _, tot = lax.fori_loop(0, nc, step, (jnp.zeros((r, 1), F32), jnp.zeros((r, 128), F32)))
        s_ref[...] = jnp.broadcast_to(jnp.sum(tot, axis=1, keepdims=True), tot.shape)

    return pl.pallas_call(body, out_shape=[jax.ShapeDtypeStruct(dcum3.shape, F32), jax.ShapeDtypeStruct((r, 128), F32)],
                          compiler_params=_params(), name=name)(dcum3, logf3, tri)


def _causal_iota():
    row = lax.broadcasted_iota(jnp.int32, (BQ, BQ), 0)
    col = lax.broadcasted_iota(jnp.int32, (BQ, BQ), 1)
    return row, col


def _head_specs(nj, head0):
    qin = pl.BlockSpec((1, BQ, HEAD_DIM), lambda h, i: (h + head0, i, 0))
    kin = pl.BlockSpec((1, nj, BQ, HEAD_DIM), lambda h, i: (h + head0, 0, 0, 0))
    qin2 = pl.BlockSpec((1, BQ, 2 * HEAD_DIM), lambda h, i: (h + head0, i, 0))
    kin2 = pl.BlockSpec((1, nj, BQ, 2 * HEAD_DIM), lambda h, i: (h + head0, 0, 0, 0))
    qspec = pl.BlockSpec((1, BQ, HEAD_DIM), lambda h, i: (h, i, 0))
    kspec2 = pl.BlockSpec((1, nj, BQ, 2 * HEAD_DIM), lambda h, i: (h, 0, 0, 0))
    return qin, kin, qin2, kin2, qspec, kspec2


STOP = -105.0
STOP_WIDE = -115.0
FIXED_REF_MAX = 40.0


def _store_kmax(k_ref, kmax_ref, nj):
    def step(j, mx):
        kf = k_ref[0, j].astype(F32)
        return jnp.maximum(mx, jnp.max(jnp.sum(kf * kf, axis=1, keepdims=True), axis=0, keepdims=True))

    mx = lax.fori_loop(0, nj, step, jnp.zeros((1, 1), F32))
    kmax_ref[...] = jnp.broadcast_to(jnp.sqrt(mx), kmax_ref.shape)


def _qk_bound(q, kmax_ref):
    qf = q.astype(F32)
    return jnp.sqrt(jnp.sum(qf * qf, axis=1, keepdims=True)) * kmax_ref[0:1, 0:1] * 1.001


def _first_and_last_step():
    h, i = pl.program_id(0), pl.program_id(1)
    first = jnp.logical_and(h == 0, i == 0)
    last = jnp.logical_and(h == pl.num_programs(0) - 1, i == pl.num_programs(1) - 1)
    return first, last


def fox_fwd(qa, ka4, va4, fcol, frow4, *, name, gather=None):
    t = qa.shape[1]
    dh = HEAD_DIM
    nh = H_FOX
    nj = t // BQ

    def body(*refs):
        if gather is None:
            q_ref, k_ref, v_ref, fc_ref, fr_ref, o_ref, lse_ref, kmax_ref = refs
        else:
            q_ref, k_ref, v_ref, fc_ref, fr_ref, src_ref, o_ref, lse_ref, dst_ref, kmax_ref = refs[:10]
            first_step, last_step = _first_and_last_step()

            @pl.when(first_step)
            def _():
                _chip_gather(src_ref, dst_ref, *refs[10:])[0]()

        i = pl.program_id(1)

        @pl.when(i == 0)
        def _():
            _store_kmax(k_ref, kmax_ref, nj)

        q = q_ref[0]
        fq = fc_ref[0]
        bound = _qk_bound(q, kmax_ref)
        row, col = _causal_iota()

        def gate_at_block_end(j):
            return fr_ref[0, j][:, BQ - 1:BQ]

        def pv(p, j):
            p_hi, p_lo = _split2(p)
            return _dot(p_hi, v_ref[0, j]) + _dot(p_lo, v_ref[0, j])

        def walk(block, live, init):
            carry = block(i, init, True)

            def cond(c):
                n, carry = c
                return jnp.logical_and(n < i, live(jnp.maximum(i - 1 - n, 0), carry))

            _, carry = lax.while_loop(cond, lambda c: (c[0] + 1, block(i - 1 - c[0], c[1], False)), (0, carry))
            return carry

        def fixed_reference(_):
            shift = fq - bound

            def probs(j, offset):
                return jnp.exp(_dot_nt(q, k_ref[0, j]) + (shift + offset) - fr_ref[0, j])

            def live(c):
                n, acc = c
                gate = gate_at_block_end(jnp.maximum(i - 1 - n, 0))
                return jnp.logical_and(n < i, jnp.max(fq - gate - jnp.log(acc[:, dh:dh + 1])) >= STOP_WIDE)

            def two_blocks(c):
                n, acc = c
                ja = i - 1 - n
                jb = i - 2 - n
                absent = jnp.where(jb >= 0, 0.0, NEG)
                jb = jnp.maximum(jb, 0)
                return n + 2, acc + (pv(probs(ja, 0.0), ja) + pv(probs(jb, absent), jb))

            acc = pv(jnp.where(col <= row, probs(i, 0.0), 0.0), i)
            _, acc = lax.while_loop(live, two_blocks, (0, acc))
            l = acc[:, dh:dh + 1]
            return acc[:, :dh] / l, bound + jnp.log(l)

        def running_maximum(_):
            def block(j, carry, diag):
                m, acc = carry
                s = _dot_nt(q, k_ref[0, j]) + fq - fr_ref[0, j]
                if diag:
                    s = jnp.where(col <= row, s, NEG)
                m_new = jnp.maximum(m, jnp.max(s, axis=1, keepdims=True))
                return m_new, jnp.exp(m - m_new) * acc + pv(jnp.exp(s - m_new), j)

            def live(j, carry):
                return jnp.max(bound + fq - gate_at_block_end(j) - carry[0]) >= STOP

            m, acc = walk(block, live, (jnp.full((BQ, 1), NEG, F32), jnp.zeros((BQ, 2 * dh), F32)))
            l = acc[:, dh:dh + 1]
            return acc[:, :dh] / l, m + jnp.log(l)

        o, lse = lax.cond(jnp.max(bound) < FIXED_REF_MAX, fixed_reference, running_maximum, 0)
        o_ref[0] = o
        lse_ref[0] = lse

        if gather is not None:
            @pl.when(last_step)
            def _():
                _chip_gather(src_ref, dst_ref, *refs[10:])[1]()

    _, _, qin2, kin2, qspec, _ = _head_specs(nj, 0)
    cspec = pl.BlockSpec((1, BQ, 1), lambda h, i: (h, i, 0))
    in_specs = [qin2, kin2, kin2, cspec, pl.BlockSpec((1, nj, 1, BQ), lambda h, i: (h, 0, 0, 0))]
    out_specs = [qspec, cspec]
    out_shape = [jax.ShapeDtypeStruct((nh, t, dh), F32), jax.ShapeDtypeStruct((nh, t, 1), F32)]
    scratch = [pltpu.VMEM((8, 128), F32)]
    args = [qa, ka4, va4, fcol, frow4]
    if gather is not None:
        in_specs.append(_ANY)
        out_specs.append(_ANY)
        out_shape.append(jax.ShapeDtypeStruct((N_CHIPS,) + gather.shape, gather.dtype))
        scratch += _chip_sems()
        args.append(gather)
    return pl.pallas_call(body, grid=(nh, nj), in_specs=in_specs, out_specs=out_specs, out_shape=out_shape,
                          scratch_shapes=scratch, compiler_params=_params(), name=name)(*args)


def fox_bwd(qa, ka4, va4, dox, doa, fcol, frow4, o, do, lse, *, name, scatter=None):
    t = qa.shape[1]
    dh = HEAD_DIM
    nh = H_FOX
    nj = t // BQ

    def body(*refs):
        q_ref, k_ref, v_ref, dox_ref, doa_ref, fc_ref, fr_ref, o_ref, do_ref, lse_ref = refs[:10]
        if scatter is None:
            dq_ref, dkv_ref, dfk_ref, kmax_ref = refs[10:]
        else:
            g_ref, dq_ref, dkv_ref, dfk_ref, land_ref, kmax_ref = refs[10:16]
            first_step, last_step = _first_and_last_step()

            @pl.when(first_step)
            def _():
                _chip_scatter(g_ref, land_ref, *refs[16:])[0]()

        i = pl.program_id(1)

        @pl.when(i == 0)
        def _():
            dkv_ref[...] = jnp.zeros_like(dkv_ref)
            dfk_ref[...] = jnp.zeros_like(dfk_ref)
            _store_kmax(k_ref, kmax_ref, nj)

        q = q_ref[0]
        do_b = do_ref[0]
        fq = fc_ref[0]
        lse_q = lse_ref[0]
        do_x = dox_ref[0]
        dd = jnp.sum(do_b.astype(F32) * o_ref[0], axis=1, keepdims=True)
        rhs = jnp.concatenate([q, doa_ref[0]], axis=0)
        edge = _qk_bound(q, kmax_ref) + fq - lse_q

        def negligible(j):
            return jnp.logical_and(j < i, jnp.max(edge - fr_ref[0, j][:, BQ - 1:BQ]) < STOP_WIDE)

        first = lax.while_loop(negligible, lambda j: j + 1, 0)

        shift = fq - lse_q

        def block(j, offset, diag):
            k = k_ref[0, j]
            p = jnp.exp(_dot_nt(q, k) + (shift + offset) - fr_ref[0, j])
            if diag:
                row, col = _causal_iota()
                p = jnp.where(col <= row, p, 0.0)
            ds = p * (_dot_nt(do_x, v_ref[0, j]) - dd)
            ds_b = ds.astype(BF16)
            dkv_ref[0, j] += _dot_tn(jnp.concatenate([ds_b, p.astype(BF16)], axis=0), rhs)
            dfk_ref[0, j] -= jnp.sum(ds, axis=0, keepdims=True)
            return _dot(ds_b, k)

        def two_blocks(n, dq):
            ja = first + 2 * n
            jb = ja + 1
            absent = jnp.where(jb < i, 0.0, NEG)
            jb = jnp.minimum(jb, i - 1)
            return dq + (block(ja, 0.0, False) + block(jb, absent, False))

        dq = lax.fori_loop(0, (i - first + 1) // 2, two_blocks, jnp.zeros((BQ, 2 * dh), F32))
        dq_ref[0] = dq + block(i, 0.0, True)

        if scatter is not None:
            @pl.when(last_step)
            def _():
                _chip_scatter(g_ref, land_ref, *refs[16:])[1]()

    qin, _, qin2, kin2, qspec, kspec2 = _head_specs(nj, 0)
    cspec = pl.BlockSpec((1, BQ, 1), lambda h, i: (h, i, 0))
    rspec = pl.BlockSpec((1, nj, 1, BQ), lambda h, i: (h, 0, 0, 0))
    in_specs = [qin2, kin2, kin2, qin2, qin2, cspec, rspec, qspec, qin, cspec]
    out_specs = [pl.BlockSpec((1, BQ, 2 * dh), lambda h, i: (h, i, 0)), kspec2, rspec]
    out_shape = [jax.ShapeDtypeStruct((nh, t, 2 * dh), F32), jax.ShapeDtypeStruct((nh, nj, BQ, 2 * dh), F32),
                 jax.ShapeDtypeStruct((nh, nj, 1, BQ), F32)]
    scratch = [pltpu.VMEM((8, 128), F32)]
    args = [qa, ka4, va4, dox, doa, fcol, frow4, o, do, lse]
    if scatter is not None:
        in_specs.append(_ANY)
        out_specs.append(_ANY)
        out_shape.append(jax.ShapeDtypeStruct(scatter.shape, scatter.dtype))
        scratch += _chip_sems()
        args.append(scatter)
    return pl.pallas_call(body, grid=(nh, nj), in_specs=in_specs, out_specs=out_specs, out_shape=out_shape,
                          scratch_shapes=scratch, compiler_params=_params(), name=name)(*args)


def _sb_logs(z, diag):
    e = jnp.exp(-jnp.abs(z))
    sp = jnp.log(1.0 + e)
    logb = jnp.minimum(z, 0.0) - sp
    lom = -jnp.maximum(z, 0.0) - sp
    strict = None
    if diag:
        row, col = _causal_iota()
        strict = col < row
        lom = jnp.where(strict, lom, 0.0)
    return logb, lom, e, strict


SB_GROUP = BQ // 2


def _sums_over_later_keys(lom, tri_m):
    halves = [lom[:, :SB_GROUP], lom[:, SB_GROUP:]]
    totals = [jnp.sum(x, axis=1, keepdims=True) for x in halves]
    within = []
    for x in halves:
        hi, lo = _split2(x)
        within.append(_dot(hi, tri_m) + _dot(lo, tri_m))
    return jnp.concatenate([within[0] + totals[1], within[1]], axis=1), totals[0] + totals[1]


def _sums_over_earlier_keys(da, tri_m):
    halves = [da[:, :SB_GROUP], da[:, SB_GROUP:]]
    totals = [jnp.sum(x, axis=1, keepdims=True) for x in halves]
    within = [_dot_nt(x.astype(BF16), tri_m) for x in halves]
    return jnp.concatenate([within[0], within[1] + totals[0]], axis=1), totals[0] + totals[1]


def sb_fwd(qa, ka4, va4, tri, *, name, gather=None):
    t = qa.shape[1]
    dh = HEAD_DIM
    nh = H_SB
    nj = t // BQ
    assert nj <= 128

    def body(*refs):
        if gather is None:
            q_ref, k_ref, v_ref, tri_ref, o_ref, rs_ref = refs
        else:
            q_ref, k_ref, v_ref, tri_ref, src_ref, o_ref, rs_ref, dst_ref = refs[:8]
            first_step, last_step = _first_and_last_step()

            @pl.when(first_step)
            def _():
                _chip_gather(src_ref, dst_ref, *refs[8:])[0]()

        i = pl.program_id(1)
        q = q_ref[0]
        tri_m = tri_ref[...]
        lane = lax.broadcasted_iota(jnp.int32, (BQ, 128), 1)

        def block(j, carry, diag):
            run, acc, rall = carry
            logb, lom, _, strict = _sb_logs(_dot_nt(q, k_ref[0, j]), diag)
            later, total = _sums_over_later_keys(lom, tri_m)
            w = jnp.exp(logb + later + run)
            if diag:
                w = jnp.where(strict, w, 0.0)
            acc = acc + _dot(w.astype(BF16), v_ref[0, j])
            rall = jnp.where(lane == j, run, rall)
            return run + total, acc, rall

        init = (jnp.zeros((BQ, 1), F32), jnp.zeros((BQ, 2 * dh), F32), jnp.full((BQ, 128), NEG, F32))
        carry = block(i, init, True)

        def cond(c):
            n, carry = c
            return jnp.logical_and(n < i, jnp.max(carry[0]) >= STOP)

        _, (_, acc, rall) = lax.while_loop(cond, lambda c: (c[0] + 1, block(i - 1 - c[0], c[1], False)), (0, carry))
        o_ref[0] = acc[:, :dh].astype(BF16)
        rs_ref[0] = rall

        if gather is not None:
            @pl.when(last_step)
            def _():
                _chip_gather(src_ref, dst_ref, *refs[8:])[1]()

    _, _, qin2, kin2, qspec, _ = _head_specs(nj, H_FOX)
    rspec = pl.BlockSpec((1, BQ, 128), lambda h, i: (h, i, 0))
    in_specs = [qin2, kin2, kin2, pl.BlockSpec((SB_GROUP, SB_GROUP), lambda h, i: (0, 0))]
    out_specs = [qspec, rspec]
    out_shape = [jax.ShapeDtypeStruct((nh, t, dh), BF16), jax.ShapeDtypeStruct((nh, t, 128), F32)]
    scratch = []
    args = [qa, ka4, va4, tri]
    if gather is not None:
        in_specs.append(_ANY)
        out_specs.append(_ANY)
        out_shape.append(jax.ShapeDtypeStruct((N_CHIPS,) + gather.shape, gather.dtype))
        scratch += _chip_sems()
        args.append(gather)
    return pl.pallas_call(body, grid=(nh, nj), in_specs=in_specs, out_specs=out_specs, out_shape=out_shape,
                          scratch_shapes=scratch, compiler_params=_params(), name=name)(*args)


def sb_bwd(qa, ka4, va4, dox, doa, tri, rsave, *, name, scatter=None):
    t = qa.shape[1]
    dh = HEAD_DIM
    nh = H_SB
    nj = t // BQ
    n_in = 7

    def body(*refs):
        q_ref, k_ref, v_ref, dox_ref, doa_ref, tri_ref, rs_ref = refs[:n_in]
        if scatter is None:
            dq_ref, dkv_ref = refs[n_in:]
        else:
            g_ref, dq_ref, dkv_ref, land_ref = refs[n_in:n_in + 4]
            first_step, last_step = _first_and_last_step()

            @pl.when(first_step)
            def _():
                _chip_scatter(g_ref, land_ref, *refs[n_in + 4:])[0]()

        i = pl.program_id(1)

        @pl.when(i == 0)
        def _():
            dkv_ref[...] = jnp.zeros_like(dkv_ref)

        q = q_ref[0]
        do_x = dox_ref[0]
        tri_m = tri_ref[...]
        rall = rs_ref[0]
        lane = lax.broadcasted_iota(jnp.int32, (BQ, 128), 1)
        rhs = jnp.concatenate([q, doa_ref[0]], axis=0)
        lane1 = lax.broadcasted_iota(jnp.int32, (1, 128), 1)
        unvisited = jnp.logical_and(lane1 < i, jnp.max(rall, axis=0, keepdims=True) < STOP)
        first = jnp.sum(unvisited.astype(jnp.int32))

        def block(j, carry, diag):
            dq, ecar = carry
            k = k_ref[0, j]
            z = _dot_nt(q, k)
            logb, lom, e, strict = _sb_logs(z, diag)
            run = jnp.sum(jnp.where(lane == j, rall, 0.0), axis=1, keepdims=True)
            w = jnp.exp(logb + _sums_over_later_keys(lom, tri_m)[0] + run)
            if diag:
                w = jnp.where(strict, w, 0.0)
            da = w * _dot_nt(do_x, v_ref[0, j])
            earlier, da_total = _sums_over_earlier_keys(da, tri_m)
            before = earlier + ecar
            inv = 1.0 / (1.0 + e)
            beta = jnp.where(z >= 0.0, 1.0, e) * inv
            one_minus = jnp.where(z >= 0.0, e, 1.0) * inv
            dz = da * one_minus - before * beta
            if diag:
                dz = jnp.where(strict, dz, 0.0)
            dz_b = dz.astype(BF16)
            dkv_ref[0, j] += _dot_tn(jnp.concatenate([dz_b, w.astype(BF16)], axis=0), rhs)
            return dq + _dot(dz_b, k), ecar + da_total

        carry = lax.fori_loop(first, i, lambda j, c: block(j, c, False),
                              (jnp.zeros((BQ, 2 * dh), F32), jnp.zeros((BQ, 1), F32)))
        dq, _ = block(i, carry, True)
        dq_ref[0] = dq

        if scatter is not None:
            @pl.when(last_step)
            def _():
                _chip_scatter(g_ref, land_ref, *refs[n_in + 4:])[1]()

    _, _, qin2, kin2, _, kspec2 = _head_specs(nj, H_FOX)
    in_specs = [qin2, kin2, kin2, qin2, qin2, pl.BlockSpec((SB_GROUP, SB_GROUP), lambda h, i: (0, 0)),
                pl.BlockSpec((1, BQ, 128), lambda h, i: (h, i, 0))]
    out_specs = [pl.BlockSpec((1, BQ, 2 * dh), lambda h, i: (h, i, 0)), kspec2]
    out_shape = [jax.ShapeDtypeStruct((nh, t, 2 * dh), F32), jax.ShapeDtypeStruct((nh, nj, BQ, 2 * dh), F32)]
    scratch = []
    args = [qa, ka4, va4, dox, doa, tri, rsave]
    if scatter is not None:
        in_specs.append(_ANY)
        out_specs.append(_ANY)
        out_shape.append(jax.ShapeDtypeStruct(scatter.shape, scatter.dtype))
        scratch += _chip_sems()
        args.append(scatter)
    return pl.pallas_call(body, grid=(nh, nj), in_specs=in_specs, out_specs=out_specs, out_shape=out_shape,
                          scratch_shapes=scratch, compiler_params=_params(), name=name)(*args)


def loss_head(y, target, *, name, tm=512):
    t, d = y.shape

    def body(y_ref, t_ref, l_ref, dy_ref, acc_ref):
        i = pl.program_id(0)
        diff = y_ref[...] - t_ref[...]
        dy_ref[...] = diff * (1.0 / d)
        part = jnp.sum(diff * diff, axis=0, keepdims=True)

        @pl.when(i == 0)
        def _():
            acc_ref[...] = part

        @pl.when(i > 0)
        def _():
            acc_ref[...] += part

        @pl.when(i == pl.num_programs(0) - 1)
        def _():
            l_ref[...] = jnp.full(l_ref.shape, (0.5 / d) * jnp.sum(acc_ref[...]), F32)

    row = pl.BlockSpec((tm, d), lambda i: (i, 0))
    return pl.pallas_call(
        body, grid=(t // tm,), in_specs=[row, row],
        out_specs=[pl.BlockSpec((8, 128), lambda i: (0, 0)), row],
        out_shape=[jax.ShapeDtypeStruct((8, 128), F32), jax.ShapeDtypeStruct((t, d), F32)],
        scratch_shapes=[pltpu.VMEM((1, d), F32)], compiler_params=_params(), name=name)(y, target)


def _to_heads(a):
    t = a.shape[0]
    return a.reshape(t, N_HEADS, HEAD_DIM).transpose(1, 0, 2)


def _from_heads(a):
    t = a.shape[1]
    return a.transpose(1, 0, 2).reshape(t, MIX)


def _lanes_to_chunks(a):
    r, t = a.shape
    return a.reshape(r, t // 128, 128).transpose(1, 0, 2)


def _chunks_to_lanes(a):
    nc, r, _ = a.shape
    return a.transpose(1, 0, 2).reshape(r, nc * 128)


def _constants():
    idx = jnp.arange(128)
    bd = (idx[:, None] // HEAD_DIM == idx[None, :] // HEAD_DIM).astype(BF16)
    tri_le = (idx[:, None] <= idx[None, :]).astype(BF16)
    tri_ge = (idx[:, None] >= idx[None, :]).astype(BF16)
    jdx = jnp.arange(SB_GROUP)
    tri_gt = (jdx[:, None] > jdx[None, :]).astype(BF16)
    return dict(bd=bd, tri_le=tri_le, tri_ge=tri_ge, tri_gt=tri_gt)


def attn_layer_fwd(h, w, cst, gather=None):
    t = h.shape[0]
    nj = t // BQ
    xn, proj = rms_mm_nn(h, w["norm"], w["w_in"], tn=640, name="attn_in_proj")
    qa, ka, va, logf = attn_prep_fwd(proj, w["gq"], w["gk"], w["fbias"], cst["bd"], name="attn_prep_fwd")
    logf3 = _lanes_to_chunks(logf)
    cum = _chunks_to_lanes(gate_cumsum(logf3, cst["tri_le"], name="gate_cumsum"))
    fcol = cum.reshape(H_FOX, t, 1)
    frow4 = cum.reshape(H_FOX, nj, 1, BQ)
    ka4 = ka.reshape(N_HEADS, nj, BQ, 2 * HEAD_DIM)
    va4 = va.reshape(N_HEADS, nj, BQ, 2 * HEAD_DIM)
    if gather is None:
        (o_f, lse), (o_s, rsave), gathered = (fox_fwd(qa, ka4, va4, fcol, frow4, name="fox_fwd"),
                                              sb_fwd(qa, ka4, va4, cst["tri_gt"], name="sb_fwd"), None)
    else:
        o_f, lse, gathered_a = fox_fwd(qa, ka4, va4, fcol, frow4, name="fox_fwd_gather", gather=gather[0])
        o_s, rsave, gathered_b = sb_fwd(qa, ka4, va4, cst["tri_gt"], name="sb_fwd_gather", gather=gather[1])
        gathered = gather[2]((gathered_a, gathered_b))
        w = gathered[0][0]
    o = _from_heads(jnp.concatenate([o_f.astype(BF16), o_s], axis=0))
    h2 = mm_nn(o, w["w_out"], add=h, name="mix_out_proj")
    saved = dict(h=h, xn=xn, proj=proj, logf3=logf3, fcol=fcol, frow4=frow4, qa=qa, ka4=ka4, va4=va4,
                 o_f=o_f, lse=lse, rsave=rsave, o=o)
    return h2, saved, gathered


def attn_layer_bwd(dh, w, s, cst, scatter=None):
    t = dh.shape[0]
    dh3 = dh[None]
    do = mm_nt(dh3, w["w_out"], out_dtype=BF16, name="mix_out_bwd_bf16")
    g_w_out = mm_tn(s["o"], dh3, name="mix_out_wgrad")
    doh = _to_heads(do)
    zeros = jnp.zeros_like(doh)
    dox = jnp.concatenate([doh, zeros], axis=-1)
    doa = jnp.concatenate([zeros, doh], axis=-1)
    fox_args = (s["qa"], s["ka4"], s["va4"], dox, doa, s["fcol"], s["frow4"], s["o_f"], doh, s["lse"])
    sb_args = (s["qa"], s["ka4"], s["va4"], dox, doa, cst["tri_gt"], s["rsave"])
    if scatter is None:
        (dq_f, dkv_f, dfk), (dq_s, dkv_s), landed = fox_bwd(*fox_args, name="fox_bwd"), sb_bwd(*sb_args, name="sb_bwd"), None
    else:
        chunks_a, chunks_b = scatter(g_w_out)
        dq_f, dkv_f, dfk, landed_a = fox_bwd(*fox_args, name="fox_bwd_scatter", scatter=chunks_a)
        dq_s, dkv_s, landed_b = sb_bwd(*sb_args, name="sb_bwd_scatter", scatter=chunks_b)
        landed = (landed_a, landed_b)
    dcum3 = _lanes_to_chunks(dfk.reshape(H_FOX, t))
    dfl3, dbias = gate_cumsum_bwd(dcum3, s["logf3"], cst["tri_ge"], name="gate_cumsum_bwd")
    dfl = jnp.pad(_chunks_to_lanes(dfl3).T, ((0, 0), (0, 128 - H_FOX))).astype(BF16)
    wide = (H_FOX, t, 2 * HEAD_DIM)
    dproj, dgq, dgk = attn_prep_bwd(s["proj"], dq_f, dq_s, dkv_f.reshape(wide), dkv_s.reshape(wide), dfl, w["gq"],
                                    w["gk"], cst["bd"], name="attn_prep_bwd")
    g_w_in = mm_tn(s["xn"], dproj[None], tn=640, name="attn_in_wgrad")[:, :ATTN_IN]
    dh2, g_norm = mm_nt_rms_bwd(dproj[None], w["w_in"], s["h"], w["norm"], dh, name="attn_in_bwd")
    dgq = dgq.reshape(N_HEADS, HEAD_DIM)
    dgk = dgk.reshape(N_HEADS, HEAD_DIM)
    grads = dict(norm=g_norm[0], w_in=g_w_in, f_bias=dbias[:, 0], fox_q=dgq[:H_FOX].sum(0), fox_k=dgk[:H_FOX].sum(0),
                 sb_q=dgq[H_FOX:].sum(0), sb_k=dgk[H_FOX:].sum(0), w_out=g_w_out)
    return dh2, grads, landed


def conv_layer_fwd(h, w):
    xn, proj3 = rms_mm_nn(h, w["norm"], w["w_in"], parts=3, name="conv_in_proj")
    y = conv_mix_fwd(proj3, w["ck"], name="conv_mix_fwd")
    h2 = mm_nn(y, w["w_out"], add=h, name="mix_out_proj")
    return h2, dict(h=h, xn=xn, proj3=proj3, y=y)


def conv_layer_bwd(dh, w, s):
    dh3 = dh[None]
    g_w_out = mm_tn(s["y"], dh3, name="mix_out_wgrad")
    dproj3, dck = conv_mix_bwd(dh, w["w_out"], s["proj3"], w["ck"], name="conv_mix_bwd")
    g_w_in = mm_tn(s["xn"], dproj3, name="conv_in_wgrad")
    dh2, g_norm = mm_nt_rms_bwd(dproj3, w["w_in"], s["h"], w["norm"], dh, name="conv_in_bwd")
    return dh2, dict(norm=g_norm[0], w_in=g_w_in, ck=dck[0, :3], w_out=g_w_out)


def ffn_layer_fwd(h, w):
    xn, up2 = rms_mm_nn(h, w["norm"], w["w_up"], parts=2, tn=1408, name="ffn_up_proj")
    h2, act = ffn_act_down_fwd(up2, w["cw2"], w["w_down"], h, name="ffn_act_down_fwd")
    return h2, dict(h=h, xn=xn, up2=up2, act=act)


def ffn_layer_bwd(dh, w, s):
    dh3 = dh[None]
    g_w_down = mm_tn(s["act"], dh3, tk=1408, name="ffn_down_wgrad")
    dup2, dcw = ffn_act_bwd(dh, w["w_down"], s["up2"], w["cw2"], name="ffn_act_bwd")
    g_w_up = mm_tn(s["xn"], dup2, tn=1408, name="ffn_up_wgrad")
    dh2, g_norm = mm_nt_rms_bwd(dup2, w["w_up"], s["h"], w["norm"], dh, name="ffn_up_bwd")
    g_cw = jnp.concatenate([dcw[0, :3], dcw[1, :3]], axis=1)
    return dh2, dict(norm=g_norm[0], w_up=g_w_up, cw=g_cw, w_down=g_w_down)


def forward_backward(x, target, wa, wc, wf, *, late_weights=None, late_chunks=None):
    cst = _constants()
    h = x
    saved = []
    layer = 0
    while layer == 0 or layer < len(wf):
        i = layer // 2
        if layer % 2 == 0:
            h, sm, built = attn_layer_fwd(h, wa[i], cst, gather=late_weights if late_weights and layer == 0 else None)
            if built is not None:
                wa, wc, wf = built
        else:
            h, sm = conv_layer_fwd(h, wc[i])
        h, sf = ffn_layer_fwd(h, wf[layer])
        saved.append((sm, sf))
        layer += 1
    depth = len(wf)
    loss_blk, dh = loss_head(h, target, name="loss_head")
    ga, gc, gf = [None] * len(wa), [None] * len(wc), [None] * depth
    landed = None
    for layer in reversed(range(depth)):
        i = layer // 2
        sm, sf = saved[layer]
        dh, gf[layer] = ffn_layer_bwd(dh, wf[layer], sf)
        if layer % 2 == 0:
            chunks = None
            if late_chunks and layer == 0:
                chunks = lambda g_w_out: late_chunks([dict(w_out=g_w_out)] + ga[1:], gc, gf)
            dh, ga[i], got = attn_layer_bwd(dh, wa[i], sm, cst, scatter=chunks)
            landed = got if got is not None else landed
        else:
            dh, gc[i] = conv_layer_bwd(dh, wc[i], sm)
    return loss_blk, dh, ga, gc, gf, landed


def _part_rows(shape, width, row_mult):
    n = 1
    for s in shape:
        n *= s
    rows = -(-n // width)
    return -(-rows // row_mult) * row_mult


def _pack_rows(arrs, width, row_mult, dtype, total_rows=None, lead=0):
    parts = []
    used = 0
    for a in arrs:
        outer = a.shape[:lead]
        rows = _part_rows(a.shape[lead:], width, row_mult)
        flat = a.astype(dtype).reshape(outer + (-1,))
        flat = jnp.pad(flat, ((0, 0),) * lead + ((0, rows * width - flat.shape[-1]),))
        parts.append(flat.reshape(outer + (rows, width)))
        used += rows
    if total_rows is not None and total_rows > used:
        parts.append(jnp.zeros(parts[0].shape[:lead] + (total_rows - used, width), dtype))
    return jnp.concatenate(parts, axis=lead)


def _unpack_rows(packed, shapes, width, row_mult, lead=0):
    outer = packed.shape[:lead]
    out = []
    off = 0
    for shape in shapes:
        rows = _part_rows(shape, width, row_mult)
        n = 1
        for s in shape:
            n *= s
        flat = lax.slice_in_dim(packed, off, off + rows, axis=lead).reshape(outer + (-1,))
        out.append(lax.slice_in_dim(flat, 0, n, axis=lead).reshape(outer + tuple(shape)))
        off += rows
    return out


BIG_NAMES = ("attn_w_in", "attn_w_out", "conv_w_in", "conv_w_out", "ffn_w_up", "ffn_w_down")
BIG_AXIS = {"attn_w_in": 2, "attn_w_out": 1, "conv_w_in": 2, "conv_w_out": 1, "ffn_w_up": 2, "ffn_w_down": 1}
BIG_WIDTH = 1024
BIG_ROW_MULT = 16
BIG_TILE = 512
SMALL_TILE = 128
SMALL_SHARDED = ("conv_norm", "conv_kernel", "ffn_conv")
SMALL_AXIS = {"conv_norm": 1, "conv_kernel": 2, "ffn_conv": 2}
SMALL_REPLICATED = ("attn_norm", "attn_f_bias", "fox_q_gain", "fox_k_gain", "sb_q_gain", "sb_k_gain", "ffn_norm")
WEIGHT_ORDER = ("attn_norm", "attn_w_in", "attn_f_bias", "fox_q_gain", "fox_k_gain", "sb_q_gain", "sb_k_gain",
                "attn_w_out", "conv_norm", "conv_w_in", "conv_kernel", "conv_w_out", "ffn_norm", "ffn_w_up",
                "ffn_conv", "ffn_w_down")


def _big_total_rows(shapes):
    used = sum(_part_rows(s, BIG_WIDTH, BIG_ROW_MULT) for s in shapes)
    tile = BIG_TILE if used >= 8 * BIG_TILE else SMALL_TILE
    return -(-used // tile) * tile


def _place():
    x, y, c = lax.axis_index("x"), lax.axis_index("y"), lax.axis_index("c")
    other_chips = [(1 - x, y), (x, 1 - y), (1 - x, 1 - y)]
    return x, y, c, other_chips


_ANY = pl.BlockSpec(memory_space=pl.ANY)


def _chip_sems():
    return [pltpu.SemaphoreType.DMA((3,)), pltpu.SemaphoreType.DMA((3,)), pltpu.SemaphoreType.DMA]


def _chip_gather(src_ref, dst_ref, send_sems, recv_sems, local_sem):
    x, y, c, chips = _place()
    k = 2 * x + y

    def copy(j, slot):
        px, py = chips[j]
        return pltpu.make_async_remote_copy(src_ref=src_ref, dst_ref=dst_ref.at[slot], send_sem=send_sems.at[j],
                                            recv_sem=recv_sems.at[j], device_id=(px, py, c), device_id_type=MESH)

    def local():
        return pltpu.make_async_copy(src_ref, dst_ref.at[k], local_sem)

    def start():
        local().start()
        for j in range(3):
            copy(j, k).start()

    def finish():
        for j, (px, py) in enumerate(chips):
            copy(j, 2 * px + py).wait_recv()
        for j in range(3):
            copy(j, k).wait_send()
        local().wait()

    return start, finish


def _chip_scatter(g_ref, o_ref, send_sems, recv_sems, local_sem):
    x, y, c, chips = _place()
    k = 2 * x + y

    def copy(j, src_slot, dst_slot):
        px, py = chips[j]
        return pltpu.make_async_remote_copy(src_ref=g_ref.at[src_slot], dst_ref=o_ref.at[dst_slot],
                                            send_sem=send_sems.at[j], recv_sem=recv_sems.at[j],
                                            device_id=(px, py, c), device_id_type=MESH)

    def local():
        return pltpu.make_async_copy(g_ref.at[k], o_ref.at[k], local_sem)

    def start():
        local().start()
        for j, (px, py) in enumerate(chips):
            copy(j, 2 * px + py, k).start()

    def finish():
        for j, (px, py) in enumerate(chips):
            copy(j, k, 2 * px + py).wait_recv()
        for j, (px, py) in enumerate(chips):
            copy(j, 2 * px + py, k).wait_send()
        local().wait()

    return start, finish


def gather_chips(arrs, *, name):
    n = len(arrs)

    def body(*refs):
        hooks = [_chip_gather(refs[m], refs[n + m], *refs[2 * n + 3 * m:2 * n + 3 * m + 3]) for m in range(n)]
        for start, _ in hooks:
            start()
        for _, finish in hooks:
            finish()

    return pl.pallas_call(
        body, in_specs=[_ANY] * n, out_specs=[_ANY] * n,
        out_shape=[jax.ShapeDtypeStruct((N_CHIPS,) + a.shape, a.dtype) for a in arrs],
        scratch_shapes=_chip_sems() * n, name=name)(*arrs)


def scatter_chips(chunks, *, name):
    def body(g_ref, o_ref, send_sems, recv_sems, local_sem):
        start, finish = _chip_scatter(g_ref, o_ref, send_sems, recv_sems, local_sem)
        start()
        finish()

    return pl.pallas_call(
        body, in_specs=[_ANY], out_specs=_ANY, out_shape=jax.ShapeDtypeStruct(chunks.shape, chunks.dtype),
        scratch_shapes=_chip_sems(), name=name)(chunks)


def swap_cores(arrs, *, name):
    n = len(arrs)

    def body(*refs):
        x, y, c, _ = _place()
        copies = [pltpu.make_async_remote_copy(src_ref=refs[m], dst_ref=refs[n + m], send_sem=refs[2 * n + 2 * m],
                                               recv_sem=refs[2 * n + 2 * m + 1], device_id=(x, y, 1 - c),
                                               device_id_type=MESH) for m in range(n)]
        for cp in copies:
            cp.start()
        for cp in copies:
            cp.wait()

    return pl.pallas_call(
        body, in_specs=[_ANY] * n, out_specs=[_ANY] * n,
        out_shape=[jax.ShapeDtypeStruct(a.shape, a.dtype) for a in arrs],
        scratch_shapes=[pltpu.SemaphoreType.DMA, pltpu.SemaphoreType.DMA] * n, name=name)(*arrs)


def allreduce_small(p, *, name):
    r, w = p.shape

    def body(p_ref, o_ref, buf, send_sems, recv_sems):
        x, y, c, _ = _place()
        me = 4 * x + 2 * y + c
        buf[me] = p_ref[...]

        def peer_of(m):
            return (1 - x if m & 4 else x, 1 - y if m & 2 else y, 1 - c if m & 1 else c)

        def copy(m, slot):
            return pltpu.make_async_remote_copy(src_ref=p_ref, dst_ref=buf.at[slot], send_sem=send_sems.at[m - 1],
                                                recv_sem=recv_sems.at[m - 1], device_id=peer_of(m),
                                                device_id_type=MESH)

        sends = [copy(m, me) for m in range(1, 8)]
        for cp in sends:
            cp.start()
        for m in range(1, 8):
            px, py, pc = peer_of(m)
            copy(m, 4 * px + 2 * py + pc).wait_recv()
        for cp in sends:
            cp.wait_send()
        acc = buf[0]
        for d in range(1, 8):
            acc = acc + buf[d]
        o_ref[...] = acc

    vm = pl.BlockSpec(memory_space=pltpu.VMEM)
    return pl.pallas_call(
        body, in_specs=[vm], out_specs=vm, out_shape=jax.ShapeDtypeStruct((r, w), F32),
        scratch_shapes=[pltpu.VMEM((8, r, w), F32), pltpu.SemaphoreType.DMA((7,)), pltpu.SemaphoreType.DMA((7,))],
        name=name)(p)


def sum_chips(rv, *, name):
    _, r, w = rv.shape
    tile = BIG_TILE if r % BIG_TILE == 0 else SMALL_TILE

    def body(a_ref, b_ref, c_ref, d_ref, o_ref):
        o_ref[...] = ((a_ref[0].astype(F32) + b_ref[0].astype(F32)) + c_ref[0].astype(F32)) + d_ref[0].astype(F32)

    spec = lambda kk: pl.BlockSpec((1, tile, w), lambda i: (kk, i, 0))
    return pl.pallas_call(
        body, grid=(r // tile,), in_specs=[spec(0), spec(1), spec(2), spec(3)],
        out_specs=pl.BlockSpec((tile, w), lambda i: (i, 0)), out_shape=jax.ShapeDtypeStruct((r, w), F32),
        compiler_params=_params(), name=name)(rv, rv, rv, rv)


def add_pair(a, b, *, name):
    r, w = a.shape
    tile = BIG_TILE if r % BIG_TILE == 0 else SMALL_TILE

    def body(a_ref, b_ref, o_ref):
        o_ref[...] = a_ref[...] + b_ref[...]

    spec = pl.BlockSpec((tile, w), lambda i: (i, 0))
    return pl.pallas_call(body, grid=(r // tile,), in_specs=[spec, spec], out_specs=spec,
                          out_shape=jax.ShapeDtypeStruct((r, w), F32), compiler_params=_params(), name=name)(a, b)


def adamw(w, g, m, v, *, tm, name):
    r, c = w.shape
    assert r % tm == 0

    def body(w_ref, g_ref, m_ref, v_ref, d_ref, nm_ref, nv_ref):
        g_ = g_ref[...]
        m_ = ADAM_B1 * m_ref[...] + (1.0 - ADAM_B1) * g_
        v_ = ADAM_B2 * v_ref[...] + (1.0 - ADAM_B2) * (g_ * g_)
        m_hat = m_ / (1.0 - ADAM_B1 ** ADAM_STEP)
        v_hat = v_ / (1.0 - ADAM_B2 ** ADAM_STEP)
        d_ref[...] = -ADAM_LR * (m_hat / (jnp.sqrt(v_hat) + ADAM_EPS) + ADAM_WD * w_ref[...])
        nm_ref[...] = m_
        nv_ref[...] = v_

    spec = pl.BlockSpec((tm, c), lambda i: (i, 0))
    return pl.pallas_call(body, grid=(r // tm,), in_specs=[spec] * 4, out_specs=[spec] * 3,
                          out_shape=[jax.ShapeDtypeStruct((r, c), F32)] * 3, compiler_params=_params(), name=name)(w, g, m, v)


def kernel(x, attn_norm, attn_w_in, attn_f_bias, fox_q_gain, fox_k_gain, sb_q_gain, sb_k_gain, attn_w_out, conv_norm, conv_w_in, conv_kernel, conv_w_out, ffn_norm, ffn_w_up, ffn_conv, ffn_w_down, loss_target, m_attn_norm, m_attn_w_in, m_attn_f_bias, m_fox_q_gain, m_fox_k_gain, m_sb_q_gain, m_sb_k_gain, m_attn_w_out, m_conv_norm, m_conv_w_in, m_conv_kernel, m_conv_w_out, m_ffn_norm, m_ffn_w_up, m_ffn_conv, m_ffn_w_down, v_attn_norm, v_attn_w_in, v_attn_f_bias, v_fox_q_gain, v_fox_k_gain, v_sb_q_gain, v_sb_k_gain, v_attn_w_out, v_conv_norm, v_conv_w_in, v_conv_kernel, v_conv_w_out, v_ffn_norm, v_ffn_w_up, v_ffn_conv, v_ffn_w_down):
    a = dict(locals())
    chip = 2 * lax.axis_index("x") + lax.axis_index("y")
    n_attn, n_conv, depth = attn_norm.shape[0], conv_norm.shape[0], ffn_norm.shape[0]

    units = [(name, l) for name in BIG_NAMES for l in range(a[name].shape[0])]
    early = [("attn_w_in", 0)]
    late = [u for u in units if u not in early]
    late_b = [("attn_w_out", 0), ("conv_w_in", n_conv - 1), ("conv_w_out", n_conv - 1), ("ffn_w_up", depth - 1),
              ("ffn_w_down", depth - 1)]
    late_a = [u for u in late if u not in late_b]
    late_sb = [("ffn_w_up", 0), ("ffn_w_down", 0), ("ffn_w_up", 1), ("ffn_w_down", 1)]
    late_sa = [u for u in late if u not in late_sb]

    def unit_shape(u):
        return a[u[0]].shape[1:]

    def pack_units(us, get, lead=0):
        return _pack_rows([get(u) for u in us], BIG_WIDTH, BIG_ROW_MULT, BF16, _big_total_rows([unit_shape(u) for u in us]),
                          lead=lead)

    def unpack_units(packed, us, lead=0):
        return dict(zip(us, _unpack_rows(packed, [unit_shape(u) for u in us], BIG_WIDTH, BIG_ROW_MULT, lead=lead)))

    def full_units(gathered, us):
        full_size = {}
        for u, g4 in unpack_units(gathered, us, lead=1).items():
            _, rows, cols = g4.shape
            if BIG_AXIS[u[0]] - 1 == 0:
                full_size[u] = g4.reshape(N_CHIPS * rows, cols)
            else:
                full_size[u] = g4.transpose(1, 0, 2).reshape(rows, N_CHIPS * cols)
        return full_size

    def shard(u):
        return a[u[0]][u[1]]

    small_shapes = [a[n].shape for n in SMALL_SHARDED]
    packed_s = _pack_rows([a[n] for n in SMALL_SHARDED], 128, 8, F32)
    gath_e, gath_s = gather_chips([pack_units(early, shard), packed_s], name="gather_weights")
    full_e = full_units(gath_e, early)
    full = {}
    per_chip = [_unpack_rows(gath_s[kk], small_shapes, 128, 8) for kk in range(N_CHIPS)]
    for n, name in enumerate(SMALL_SHARDED):
        full[name] = jnp.concatenate([per_chip[kk][n] for kk in range(N_CHIPS)], axis=SMALL_AXIS[name])

    def attn_weights(i, fu):
        return dict(
            norm=attn_norm[i][None],
            w_in=jnp.pad(fu[("attn_w_in", i)], ((0, 0), (0, ATTN_IN_PAD - ATTN_IN))),
            fbias=jnp.pad(attn_f_bias[i], (0, 128 - H_FOX))[None],
            gq=jnp.concatenate([jnp.tile(fox_q_gain[i], H_FOX), jnp.tile(sb_q_gain[i], H_SB)])[None],
            gk=jnp.concatenate([jnp.tile(fox_k_gain[i], H_FOX), jnp.tile(sb_k_gain[i], H_SB)])[None],
            w_out=fu.get(("attn_w_out", i)))

    def build_weights(gathered):
        fu = {**full_e, **full_units(gathered[0], late_a), **full_units(gathered[1], late_b)}
        wa = [attn_weights(i, fu) for i in range(n_attn)]
        wc = [dict(norm=full["conv_norm"][i][None], w_in=fu[("conv_w_in", i)], ck=full["conv_kernel"][i][None],
                   w_out=fu[("conv_w_out", i)]) for i in range(n_conv)]
        wf = []
        for l in range(depth):
            cw = full["ffn_conv"][l]
            wf.append(dict(norm=ffn_norm[l][None], w_up=fu[("ffn_w_up", l)], cw2=jnp.stack([cw[:, :D_FF], cw[:, D_FF:]]),
                           w_down=fu[("ffn_w_down", l)]))
        return wa, wc, wf

    def chunks_of_unit(u, ga, gc, gf):
        name, l = u
        g = {"attn_w_in": lambda: ga[l]["w_in"], "attn_w_out": lambda: ga[l]["w_out"],
             "conv_w_in": lambda: gc[l]["w_in"], "conv_w_out": lambda: gc[l]["w_out"],
             "ffn_w_up": lambda: gf[l]["w_up"], "ffn_w_down": lambda: gf[l]["w_down"]}[name]()
        rows, cols = unit_shape(u)
        if BIG_AXIS[name] - 1 == 0:
            return g.reshape(N_CHIPS, rows, cols)
        return g.reshape(rows, N_CHIPS, cols).transpose(1, 0, 2)

    def chunks_of(us, ga, gc, gf):
        return pack_units(us, lambda u: chunks_of_unit(u, ga, gc, gf), lead=1)

    loss_blk, grad_x, ga, gc, gf, landed_late = forward_backward(
        x[0], loss_target[0], [attn_weights(0, full_e)], [], [],
        late_weights=(pack_units(late_a, shard), pack_units(late_b, shard), build_weights),
        late_chunks=lambda ga, gc, gf: (chunks_of(late_sa, ga, gc, gf), chunks_of(late_sb, ga, gc, gf)))

    landed = [scatter_chips(chunks_of(early, ga, gc, gf), name="scatter_grads"), landed_late[0], landed_late[1]]
    mine = [sum_chips(buf, name="sum_chips") for buf in landed]
    theirs = swap_cores(mine, name="swap_cores")
    g_units = {}
    for us, m, th in zip((early, late_sa, late_sb), mine, theirs):
        g_units.update(unpack_units(add_pair(m, th, name="add_cores"), us))
    grads = {name: jnp.stack([g_units[(name, l)] for l in range(a[name].shape[0])]) for name in BIG_NAMES}

    small_full = [
        loss_blk,
        jnp.stack([g["norm"] for g in ga]), jnp.stack([g["f_bias"] for g in ga]),
        jnp.stack([g["fox_q"] for g in ga]), jnp.stack([g["fox_k"] for g in ga]),
        jnp.stack([g["sb_q"] for g in ga]), jnp.stack([g["sb_k"] for g in ga]),
        jnp.stack([g["norm"] for g in gf]),
        jnp.stack([g["norm"] for g in gc]), jnp.stack([g["ck"] for g in gc]), jnp.stack([g["cw"] for g in gf]),
    ]
    summed = allreduce_small(_pack_rows(small_full, 128, 8, F32), name="allreduce_small")
    parts = _unpack_rows(summed, [p.shape for p in small_full], 128, 8)
    loss = parts[0][0, 0]
    for name, g in zip(SMALL_REPLICATED, parts[1:8]):
        grads[name] = g
    for name, g in zip(SMALL_SHARDED, parts[8:]):
        width = a[name].shape[SMALL_AXIS[name]]
        grads[name] = lax.dynamic_slice_in_dim(g, chip * width, width, axis=SMALL_AXIS[name])

    delta, new_m, new_v = {}, {}, {}
    for name in BIG_NAMES:
        shape = a[name].shape
        flat = lambda arr: arr.reshape(-1, shape[-1])
        d_, m_, v_ = adamw(flat(a[name]), flat(grads[name]), flat(a["m_" + name]), flat(a["v_" + name]), tm=256,
                           name="adamw")
        delta[name], new_m[name], new_v[name] = d_.reshape(shape), m_.reshape(shape), v_.reshape(shape)
    small_names = SMALL_REPLICATED + SMALL_SHARDED
    small_shapes_local = [a[n].shape for n in small_names]
    pack = lambda prefix, src: _pack_rows([src[prefix + n] for n in small_names], 128, 8, F32)
    packed = adamw(pack("", a), pack("", grads), pack("m_", a), pack("v_", a), tm=8, name="adamw_small")
    for store, buf in zip((delta, new_m, new_v), packed):
        for name, arr in zip(small_names, _unpack_rows(buf, small_shapes_local, 128, 8)):
            store[name] = arr

    return (loss, grad_x[None], *[grads[n] for n in WEIGHT_ORDER], *[delta[n] for n in WEIGHT_ORDER],
            *[new_m[n] for n in WEIGHT_ORDER], *[new_v[n] for n in WEIGHT_ORDER])
```

```python
import functools

import jax
import jax.numpy as jnp
from jax import lax
from jax.experimental import pallas as pl
from jax.experimental.pallas import tpu as pltpu

F32 = jnp.float32
BF16 = jnp.bfloat16

D_MODEL = 1024
HEAD_DIM = 64
H_FOX = 8
H_SB = 8
N_HEADS = H_FOX + H_SB
MIX = N_HEADS * HEAD_DIM
ATTN_IN = 3 * MIX + H_FOX
ATTN_IN_PAD = 3 * MIX + 128
D_FF = 2816
EPS = 1e-6
SCALE = HEAD_DIM ** -0.5
NEG = -1e30

ADAM_LR = 0.001
ADAM_B1 = 0.9
ADAM_B2 = 0.999
ADAM_EPS = 1e-08
ADAM_WD = 0.01
ADAM_STEP = 10

VMEM_LIMIT = 56 * 1024 * 1024
HALO = 8
BQ = 512
N_CHIPS = 4
MESH = pl.DeviceIdType.MESH


def _params(**kw):
    return pltpu.CompilerParams(vmem_limit_bytes=VMEM_LIMIT, **kw)


def _dot(a, b):
    return jnp.dot(a, b, preferred_element_type=F32)


def _dot_nt(a, b):
    return lax.dot_general(a, b, (((1,), (1,)), ((), ())), preferred_element_type=F32)


def _dot_tn(a, b):
    return lax.dot_general(a, b, (((0,), (0,)), ((), ())), preferred_element_type=F32)


def _split2(x):
    hi = x.astype(BF16)
    lo = (x - hi.astype(F32)).astype(BF16)
    return hi, lo


def _split3(x):
    hi = x.astype(BF16)
    r = x - hi.astype(F32)
    mid = r.astype(BF16)
    lo = (r - mid.astype(F32)).astype(BF16)
    return hi, mid, lo


def mm_nn(a, b, *, add=None, out_dtype=F32, parts=1, tm=1024, tn=512, name):
    m, k = a.shape
    n = b.shape[1]
    np_ = n // parts
    nb = np_ // tn
    tm = min(tm, m)
    assert m % tm == 0 and np_ % tn == 0

    def body(*refs):
        if add is None:
            a_ref, b_ref, o_ref = refs
            acc = _dot(a_ref[...].astype(BF16), b_ref[...])
        else:
            a_ref, b_ref, r_ref, o_ref = refs
            acc = _dot(a_ref[...].astype(BF16), b_ref[...]) + r_ref[...]
        o_ref[...] = acc.astype(out_dtype).reshape(o_ref.shape)

    in_specs = [pl.BlockSpec((tm, k), lambda i, j: (i, 0)), pl.BlockSpec((k, tn), lambda i, j: (0, j))]
    args = [a, b]
    if add is not None:
        in_specs.append(pl.BlockSpec((tm, tn), lambda i, j: (i, j)))
        args.append(add)
    if parts == 1:
        out_spec = pl.BlockSpec((tm, tn), lambda i, j: (i, j))
        out_shape = jax.ShapeDtypeStruct((m, n), out_dtype)
    else:
        out_spec = pl.BlockSpec((1, tm, tn), lambda i, j: (j // nb, i, j % nb))
        out_shape = jax.ShapeDtypeStruct((parts, m, np_), out_dtype)
    return pl.pallas_call(body, grid=(m // tm, n // tn), in_specs=in_specs, out_specs=out_spec,
                          out_shape=out_shape, compiler_params=_params(), name=name)(*args)


def mm_nt_heads(a, b3, *, tm=1024, name):
    t, d = a.shape
    nh, n, _ = b3.shape
    tm = min(tm, t)
    assert t % tm == 0

    def body(a_ref, b_ref, o_ref):
        o_ref[0] = _dot_nt(a_ref[...].astype(BF16), b_ref[0]).astype(BF16)

    return pl.pallas_call(
        body, grid=(t // tm, nh),
        in_specs=[pl.BlockSpec((tm, d), lambda i, h: (i, 0)), pl.BlockSpec((1, n, d), lambda i, h: (h, 0, 0))],
        out_specs=pl.BlockSpec((1, tm, n), lambda i, h: (h, i, 0)),
        out_shape=jax.ShapeDtypeStruct((nh, t, n), BF16), compiler_params=_params(), name=name)(a, b3)


def mm_tn(a, b3, *, tk=512, tn=512, tt=2048, name):
    t, k = a.shape
    p, _, np_ = b3.shape
    nb = np_ // tn
    tt = min(tt, t)
    assert t % tt == 0 and k % tk == 0 and np_ % tn == 0

    def body(a_ref, b_ref, o_ref):
        prod = _dot_tn(a_ref[...].astype(BF16), b_ref[0].astype(BF16))

        @pl.when(pl.program_id(2) == 0)
        def _():
            o_ref[...] = prod

        @pl.when(pl.program_id(2) > 0)
        def _():
            o_ref[...] += prod

    return pl.pallas_call(
        body, grid=(k // tk, p * nb, t // tt),
        in_specs=[pl.BlockSpec((tt, tk), lambda i, j, s: (s, i)), pl.BlockSpec((1, tt, tn), lambda i, j, s: (j // nb, s, j % nb))],
        out_specs=pl.BlockSpec((tk, tn), lambda i, j, s: (i, j)),
        out_shape=jax.ShapeDtypeStruct((k, p * np_), F32),
        compiler_params=_params(), name=name)(a, b3)


def rms_mm_nn(h, g, b, *, parts=1, tm=1024, tn=512, name):
    t, d = h.shape
    n = b.shape[1]
    np_ = n // parts
    nb = np_ // tn
    tm = min(tm, t)
    assert t % tm == 0 and np_ % tn == 0

    def body(h_ref, g_ref, b_ref, xn_ref, o_ref):
        @pl.when(pl.program_id(1) == 0)
        def _():
            x = h_ref[...]
            r = lax.rsqrt(jnp.mean(x * x, axis=-1, keepdims=True) + EPS)
            xn_ref[...] = (x * r * g_ref[...]).astype(BF16)

        o_ref[...] = _dot(xn_ref[...], b_ref[...]).reshape(o_ref.shape)

    if parts == 1:
        out_spec = pl.BlockSpec((tm, tn), lambda i, j: (i, j))
        out_shape = jax.ShapeDtypeStruct((t, n), F32)
    else:
        out_spec = pl.BlockSpec((1, tm, tn), lambda i, j: (j // nb, i, j % nb))
        out_shape = jax.ShapeDtypeStruct((parts, t, np_), F32)
    row = pl.BlockSpec((tm, d), lambda i, j: (i, 0))
    return pl.pallas_call(
        body, grid=(t // tm, n // tn),
        in_specs=[row, pl.BlockSpec((1, d), lambda i, j: (0, 0)), pl.BlockSpec((d, tn), lambda i, j: (0, j))],
        out_specs=[row, out_spec], out_shape=[jax.ShapeDtypeStruct((t, d), BF16), out_shape],
        compiler_params=_params(), name=name)(h, g, b)


def mm_nt_rms_bwd(a3, b, h, g, dres, *, name, tm=512):
    p, t, kp = a3.shape
    d = b.shape[0]
    tm = min(tm, t)
    assert t % tm == 0 and b.shape[1] == p * kp

    def body(a_ref, b_ref, h_ref, g_ref, dres_ref, dh_ref, dg_ref, acc_ref):
        i = pl.program_id(0)
        part = pl.program_id(1)
        prod = _dot_nt(a_ref[0].astype(BF16), b_ref[...])

        @pl.when(part == 0)
        def _():
            acc_ref[...] = prod

        @pl.when(part > 0)
        def _():
            acc_ref[...] += prod

        @pl.when(part == p - 1)
        def _():
            x = h_ref[...]
            dy = acc_ref[...]
            r = lax.rsqrt(jnp.mean(x * x, axis=-1, keepdims=True) + EPS)
            gy = dy * g_ref[...]
            dot = jnp.mean(gy * x, axis=-1, keepdims=True)
            dh_ref[...] = dres_ref[...] + r * gy - x * (r * r * r * dot)
            _acc_rows(dg_ref, jnp.sum(dy * x * r, axis=0, keepdims=True), i == 0)

    row = pl.BlockSpec((tm, d), lambda i, q: (i, 0))
    vec = pl.BlockSpec((1, d), lambda i, q: (0, 0))
    return pl.pallas_call(
        body, grid=(t // tm, p),
        in_specs=[pl.BlockSpec((1, tm, kp), lambda i, q: (q, i, 0)), pl.BlockSpec((d, kp), lambda i, q: (0, q)),
                  row, vec, row],
        out_specs=[row, vec],
        out_shape=[jax.ShapeDtypeStruct((t, d), F32), jax.ShapeDtypeStruct((1, d), F32)],
        scratch_shapes=[pltpu.VMEM((tm, d), F32)], compiler_params=_params(), name=name)(a3, b, h, g, dres)


def _causal3(x, w):
    return w[0:1] * pltpu.roll(x, 2, 0) + w[1:2] * pltpu.roll(x, 1, 0) + w[2:3] * x


def _causal3_taps(x_ext, w, tm):
    x2 = pltpu.roll(x_ext, 2, 0)
    x1 = pltpu.roll(x_ext, 1, 0)
    y = w[0:1] * x2 + w[1:2] * x1 + w[2:3] * x_ext
    return y, (x2[HALO:HALO + tm], x1[HALO:HALO + tm], x_ext[HALO:HALO + tm])


def _anticausal3(z, w):
    n = z.shape[0]
    return w[2:3] * z + w[1:2] * pltpu.roll(z, n - 1, 0) + w[0:1] * pltpu.roll(z, n - 2, 0)


def _prev_spec(part, tm, tc, nrow8):
    del nrow8
    return pl.BlockSpec((1, HALO, tc), lambda j, i: (part, jnp.maximum(i * (tm // HALO) - 1, 0), j))


def _next_spec(part, tm, tc, nrow8):
    return pl.BlockSpec((1, HALO, tc), lambda j, i: (part, jnp.minimum((i + 1) * (tm // HALO), nrow8 - 1), j))


def _tile_spec(part, tm, tc):
    return pl.BlockSpec((1, tm, tc), lambda j, i: (part, i, j))


def _acc_rows(ref, val, first):
    @pl.when(first)
    def _():
        ref[...] = val

    @pl.when(jnp.logical_not(first))
    def _():
        ref[...] += val


def ffn_act_down_fwd(up2, cw2, w_down, h, *, name, tm=256, tc=1408):
    _, t, f = up2.shape
    d = h.shape[1]
    tm = min(tm, t)

    def body(g_ref, v_ref, gp_ref, vp_ref, w_ref, wd_ref, h_ref, o_ref, act_ref):
        keep = jnp.where(pl.program_id(0) == 0, 0.0, 1.0)
        for cc in range(f // tc):
            cols = slice(cc * tc, (cc + 1) * tc)
            g_ext = jnp.concatenate([gp_ref[0, :, cols] * keep, g_ref[0, :, cols]], axis=0)
            v_ext = jnp.concatenate([vp_ref[0, :, cols] * keep, v_ref[0, :, cols]], axis=0)
            ug = _causal3(g_ext, w_ref[0, :, cols])[HALO:]
            uv = _causal3(v_ext, w_ref[1, :, cols])[HALO:]
            act_ref[:, cols] = (ug * jax.nn.sigmoid(ug) * uv).astype(BF16)
        o_ref[...] = _dot(act_ref[...], wd_ref[...]) + h_ref[...]

    tile = lambda part: pl.BlockSpec((1, tm, f), lambda i: (part, i, 0))
    prev = lambda part: pl.BlockSpec((1, HALO, f), lambda i: (part, jnp.maximum(i * (tm // HALO) - 1, 0), 0))
    row = pl.BlockSpec((tm, d), lambda i: (i, 0))
    return pl.pallas_call(
        body, grid=(t // tm,),
        in_specs=[tile(0), tile(1), prev(0), prev(1), pl.BlockSpec((2, 3, f), lambda i: (0, 0, 0)),
                  pl.BlockSpec((f, d), lambda i: (0, 0)), row],
        out_specs=[row, pl.BlockSpec((tm, f), lambda i: (i, 0))],
        out_shape=[jax.ShapeDtypeStruct((t, d), F32), jax.ShapeDtypeStruct((t, f), BF16)],
        compiler_params=_params(), name=name)(up2, up2, up2, up2, cw2, w_down, h)


def ffn_act_bwd(dh, w_down, up2, cw2, *, name, tm=256, tc=1408):
    _, t, f = up2.shape
    d = dh.shape[1]
    n8 = t // HALO

    def body(d_ref, dn_ref, wd_ref, g_ref, v_ref, gp_ref, vp_ref, gn_ref, vn_ref, wg_ref, wv_ref, dup_ref, dw_ref):
        i = pl.program_id(1)
        first = i == 0
        keep_p = jnp.where(first, 0.0, 1.0)
        keep_n = jnp.where(i == pl.num_programs(1) - 1, 0.0, 1.0)
        wg = wg_ref[0]
        wv = wv_ref[0]
        g_ext = jnp.concatenate([gp_ref[0] * keep_p, g_ref[0], gn_ref[0]], axis=0)
        v_ext = jnp.concatenate([vp_ref[0] * keep_p, v_ref[0], vn_ref[0]], axis=0)
        dh_ext = jnp.concatenate([d_ref[...], dn_ref[...] * keep_n], axis=0)
        d_ext = _dot_nt(dh_ext.astype(BF16), wd_ref[...])
        ug, (g2, g1, g0) = _causal3_taps(g_ext, wg, tm)
        uv, (v2, v1, v0) = _causal3_taps(v_ext, wv, tm)
        ug = ug[HALO:]
        uv = uv[HALO:]
        s = jax.nn.sigmoid(ug)
        dg = d_ext * uv * (s * (1.0 + ug * (1.0 - s)))
        dv = d_ext * (ug * s)
        dup_ref[0] = _anticausal3(dg, wg)[:tm].astype(BF16)
        dup_ref[1] = _anticausal3(dv, wv)[:tm].astype(BF16)
        dgt = dg[:tm]
        dvt = dv[:tm]
        zero = jnp.zeros((HALO - 3, tc), F32)
        rows_g = [jnp.sum(dgt * x, axis=0, keepdims=True) for x in (g2, g1, g0)] + [zero]
        rows_v = [jnp.sum(dvt * x, axis=0, keepdims=True) for x in (v2, v1, v0)] + [zero]
        _acc_rows(dw_ref, jnp.stack([jnp.concatenate(rows_g, axis=0), jnp.concatenate(rows_v, axis=0)]), first)

    wspec = lambda part: pl.BlockSpec((1, 3, tc), lambda j, i: (part, 0, j))
    return pl.pallas_call(
        body, grid=(f // tc, t // tm),
        in_specs=[pl.BlockSpec((tm, d), lambda j, i: (i, 0)),
                  pl.BlockSpec((HALO, d), lambda j, i: (jnp.minimum((i + 1) * (tm // HALO), n8 - 1), 0)),
                  pl.BlockSpec((tc, d), lambda j, i: (j, 0)),
                  _tile_spec(0, tm, tc), _tile_spec(1, tm, tc), _prev_spec(0, tm, tc, n8), _prev_spec(1, tm, tc, n8),
                  _next_spec(0, tm, tc, n8), _next_spec(1, tm, tc, n8), wspec(0), wspec(1)],
        out_specs=[pl.BlockSpec((2, tm, tc), lambda j, i: (0, i, j)), pl.BlockSpec((2, HALO, tc), lambda j, i: (0, 0, j))],
        out_shape=[jax.ShapeDtypeStruct((2, t, f), BF16), jax.ShapeDtypeStruct((2, HALO, f), F32)],
        compiler_params=_params(), name=name)(dh, dh, w_down, up2, up2, up2, up2, up2, up2, cw2, cw2)


def conv_mix_fwd(proj3, ck, *, name, tm=512, tc=512):
    _, t, c = proj3.shape
    n8 = t // HALO

    def body(b_ref, c_ref, u_ref, cp_ref, up_ref, w_ref, o_ref):
        keep = jnp.where(pl.program_id(1) == 0, 0.0, 1.0)
        cu_ext = jnp.concatenate([cp_ref[0] * up_ref[0] * keep, c_ref[0] * u_ref[0]], axis=0)
        o_ref[...] = (b_ref[0] * _causal3(cu_ext, w_ref[0])[HALO:]).astype(BF16)

    return pl.pallas_call(
        body, grid=(c // tc, t // tm),
        in_specs=[_tile_spec(0, tm, tc), _tile_spec(1, tm, tc), _tile_spec(2, tm, tc), _prev_spec(1, tm, tc, n8),
                  _prev_spec(2, tm, tc, n8), pl.BlockSpec((1, 3, tc), lambda j, i: (0, 0, j))],
        out_specs=pl.BlockSpec((tm, tc), lambda j, i: (i, j)),
        out_shape=jax.ShapeDtypeStruct((t, c), BF16), compiler_params=_params(), name=name)(proj3, proj3, proj3, proj3, proj3, ck)


def conv_mix_bwd(dh, w_out, proj3, ck, *, name, tm=512, tc=512):
    _, t, c = proj3.shape
    d = dh.shape[1]
    n8 = t // HALO

    def body(d_ref, dn_ref, wo_ref, b_ref, c_ref, u_ref, cp_ref, up_ref, bn_ref, w_ref, dp_ref, dw_ref):
        i = pl.program_id(1)
        first = i == 0
        keep_p = jnp.where(first, 0.0, 1.0)
        keep_n = jnp.where(i == pl.num_programs(1) - 1, 0.0, 1.0)
        w = w_ref[0]
        cu_ext = jnp.concatenate([cp_ref[0] * up_ref[0] * keep_p, c_ref[0] * u_ref[0]], axis=0)
        cv, (x2, x1, x0) = _causal3_taps(cu_ext, w, tm)
        cv = cv[HALO:]
        dh_ext = jnp.concatenate([d_ref[...], dn_ref[...] * keep_n], axis=0)
        d_ext = _dot_nt(dh_ext.astype(BF16), wo_ref[...])
        dyt = d_ext[:tm]
        b_ext = jnp.concatenate([b_ref[0], bn_ref[0]], axis=0)
        dcv = d_ext * b_ext
        dcu = _anticausal3(dcv, w)[:tm]
        dp_ref[0] = (dyt * cv).astype(BF16)
        dp_ref[1] = (dcu * u_ref[0]).astype(BF16)
        dp_ref[2] = (dcu * c_ref[0]).astype(BF16)
        dcvt = dcv[:tm]
        rows = [jnp.sum(dcvt * x, axis=0, keepdims=True) for x in (x2, x1, x0)] + [jnp.zeros((HALO - 3, tc), F32)]
        _acc_rows(dw_ref, jnp.concatenate(rows, axis=0)[None], first)

    return pl.pallas_call(
        body, grid=(c // tc, t // tm),
        in_specs=[pl.BlockSpec((tm, d), lambda j, i: (i, 0)),
                  pl.BlockSpec((HALO, d), lambda j, i: (jnp.minimum((i + 1) * (tm // HALO), n8 - 1), 0)),
                  pl.BlockSpec((tc, d), lambda j, i: (j, 0)),
                  _tile_spec(0, tm, tc), _tile_spec(1, tm, tc), _tile_spec(2, tm, tc),
                  _prev_spec(1, tm, tc, n8), _prev_spec(2, tm, tc, n8),
                  _next_spec(0, tm, tc, n8), pl.BlockSpec((1, 3, tc), lambda j, i: (0, 0, j))],
        out_specs=[pl.BlockSpec((3, tm, tc), lambda j, i: (0, i, j)), pl.BlockSpec((1, HALO, tc), lambda j, i: (0, 0, j))],
        out_shape=[jax.ShapeDtypeStruct((3, t, c), BF16), jax.ShapeDtypeStruct((1, HALO, c), F32)],
        compiler_params=_params(), name=name)(dh, dh, w_out, proj3, proj3, proj3, proj3, proj3, proj3, ck)


def _head_sums(x, bd):
    hi, lo = _split2(x)
    return _dot(hi, bd) + _dot(lo, bd)


def attn_prep_fwd(proj, gq, gk, fbias, bd, *, name, tm=256):
    t = proj.shape[0]

    def body(q_ref, k_ref, v_ref, f_ref, gq_ref, gk_ref, fb_ref, bd_ref, qa_ref, ka_ref, va_ref, lf_ref):
        bd = bd_ref[...]
        lane = lax.broadcasted_iota(jnp.int32, (tm, 128), 1)
        low = lane < HEAD_DIM

        def two_heads(y, o_ref, c, rest):
            o_ref[2 * c] = jnp.where(low, y, rest).astype(BF16)
            o_ref[2 * c + 1] = jnp.where(low, pltpu.roll(y, HEAD_DIM, 1), rest).astype(BF16)

        def headnorm(x_ref, g_ref, o_ref, scale):
            for c in range(MIX // 128):
                sl = slice(128 * c, 128 * (c + 1))
                x = x_ref[:, sl]
                r = lax.rsqrt(_head_sums(x * x, bd) * (1.0 / HEAD_DIM) + EPS)
                two_heads(x * r * (g_ref[:, sl] * scale), o_ref, c, 0.0)

        headnorm(q_ref, gq_ref, qa_ref, SCALE)
        headnorm(k_ref, gk_ref, ka_ref, 1.0)
        one_at_64 = jnp.where(lane == HEAD_DIM, 1.0, 0.0)
        for c in range(MIX // 128):
            two_heads(v_ref[:, 128 * c:128 * (c + 1)], va_ref, c, one_at_64)
        fl = f_ref[...] + fb_ref[...]
        logf = jnp.minimum(fl, 0.0) - jnp.log(1.0 + jnp.exp(-jnp.abs(fl)))
        lf_ref[...] = logf.T[0:H_FOX, :]

    col = lambda c: pl.BlockSpec((tm, MIX), lambda i: (i, c))
    vec = pl.BlockSpec((1, MIX), lambda i: (0, 0))
    out = pl.BlockSpec((N_HEADS, tm, 2 * HEAD_DIM), lambda i: (0, i, 0))
    return pl.pallas_call(
        body, grid=(t // tm,),
        in_specs=[col(0), col(1), col(2), pl.BlockSpec((tm, 128), lambda i: (i, 3 * MIX // 128)), vec, vec,
                  pl.BlockSpec((1, 128), lambda i: (0, 0)), pl.BlockSpec((128, 128), lambda i: (0, 0))],
        out_specs=[out, out, out, pl.BlockSpec((H_FOX, tm), lambda i: (0, i))],
        out_shape=[jax.ShapeDtypeStruct((N_HEADS, t, 2 * HEAD_DIM), BF16)] * 3 + [jax.ShapeDtypeStruct((H_FOX, t), F32)],
        compiler_params=_params(), name=name)(proj, proj, proj, proj, gq, gk, fbias, bd)


def attn_prep_bwd(proj, dq_f, dq_s, dkv_f, dkv_s, dfl, gq, gk, bd, *, name, tm=256):
    t = proj.shape[0]
    per_group = H_FOX // 2

    def body(q_ref, k_ref, dqf_ref, dqs_ref, dkvf_ref, dkvs_ref, dfl_ref, gq_ref, gk_ref, bd_ref, dp_ref, dgq_ref,
             dgk_ref):
        bd = bd_ref[...]
        first = pl.program_id(0) == 0
        low = lax.broadcasted_iota(jnp.int32, (tm, 128), 1) < HEAD_DIM

        def two_heads(fox_ref, sb_ref, c):
            ref = fox_ref if c < per_group else sb_ref
            return ref[2 * (c % per_group)], ref[2 * (c % per_group) + 1]

        def low_halves(ab):
            return jnp.where(low, ab[0], pltpu.roll(ab[1], HEAD_DIM, 1))

        def high_halves(ab):
            return jnp.where(low, pltpu.roll(ab[0], HEAD_DIM, 1), ab[1])

        def back(x_ref, grad, g_ref, col0, scale, dg_ref):
            parts = []
            for c in range(MIX // 128):
                sl = slice(128 * c, 128 * (c + 1))
                x = x_ref[:, sl]
                r = lax.rsqrt(_head_sums(x * x, bd) * (1.0 / HEAD_DIM) + EPS)
                dn = grad(c) * scale
                gy = dn * g_ref[:, sl]
                hs = _head_sums(gy * x, bd) * (1.0 / HEAD_DIM)
                dp_ref[:, col0 + 128 * c:col0 + 128 * (c + 1)] = (r * gy - x * (r * r * r * hs)).astype(BF16)
                parts.append(jnp.sum(dn * x * r, axis=0, keepdims=True))
            _acc_rows(dg_ref, jnp.concatenate(parts, axis=1), first)

        back(q_ref, lambda c: low_halves(two_heads(dqf_ref, dqs_ref, c)), gq_ref, 0, SCALE, dgq_ref)
        back(k_ref, lambda c: low_halves(two_heads(dkvf_ref, dkvs_ref, c)), gk_ref, MIX, 1.0, dgk_ref)
        for c in range(MIX // 128):
            dp_ref[:, 2 * MIX + 128 * c:2 * MIX + 128 * (c + 1)] = high_halves(two_heads(dkvf_ref, dkvs_ref, c)).astype(BF16)
        dp_ref[:, 3 * MIX:] = dfl_ref[...]

    col = lambda c: pl.BlockSpec((tm, MIX), lambda i: (i, c))
    heads = pl.BlockSpec((H_FOX, tm, 128), lambda i: (0, i, 0))
    vec = pl.BlockSpec((1, MIX), lambda i: (0, 0))
    return pl.pallas_call(
        body, grid=(t // tm,),
        in_specs=[col(0), col(1), heads, heads, heads, heads, pl.BlockSpec((tm, 128), lambda i: (i, 0)), vec, vec,
                  pl.BlockSpec((128, 128), lambda i: (0, 0))],
        out_specs=[pl.BlockSpec((tm, ATTN_IN_PAD), lambda i: (i, 0)), vec, vec],
        out_shape=[jax.ShapeDtypeStruct((t, ATTN_IN_PAD), BF16), jax.ShapeDtypeStruct((1, MIX), F32),
                   jax.ShapeDtypeStruct((1, MIX), F32)],
        compiler_params=_params(), name=name)(proj, proj, dq_f, dq_s, dkv_f, dkv_s, dfl, gq, gk, bd)


def gate_cumsum(logf3, tri, *, name):
    nc, r, _ = logf3.shape

    def body(x_ref, tri_ref, o_ref):
        tri_m = tri_ref[...]

        def step(c, carry):
            hi, mid, lo = _split3(x_ref[c])
            cs = _dot(hi, tri_m) + _dot(mid, tri_m) + _dot(lo, tri_m) + carry
            o_ref[c] = cs
            return cs[:, 127:128]

        lax.fori_loop(0, nc, step, jnp.zeros((r, 1), F32))

    return pl.pallas_call(body, out_shape=jax.ShapeDtypeStruct(logf3.shape, F32), compiler_params=_params(),
                          name=name)(logf3, tri)


def gate_cumsum_bwd(dcum3, logf3, tri, *, name):
    nc, r, _ = dcum3.shape

    def body(x_ref, lf_ref, tri_ref, o_ref, s_ref):
        tri_m = tri_ref[...]

        def step(n, carry):
            car, tot = carry
            c = nc - 1 - n
            hi, mid, lo = _split3(x_ref[c])
            cs = _dot(hi, tri_m) + _dot(mid, tri_m) + _dot(lo, tri_m) + car
            dl = cs * (1.0 - jnp.exp(lf_ref[c]))
            o_ref[c] = dl
            return cs[:, 0:1], tot + dl

        _, tot = lax.fori_loop(0, nc, step, (jnp.zeros((r, 1), F32), jnp.zeros((r, 128), F32)))
        s_ref[...] = jnp.broadcast_to(jnp.sum(tot, axis=1, keepdims=True), tot.shape)

    return pl.pallas_call(body, out_shape=[jax.ShapeDtypeStruct(dcum3.shape, F32), jax.ShapeDtypeStruct((r, 128), F32)],
                          compiler_params=_params(), name=name)(dcum3, logf3, tri)


def _causal_iota():
    row = lax.broadcasted_iota(jnp.int32, (BQ, BQ), 0)
    col = lax.broadcasted_iota(jnp.int32, (BQ, BQ), 1)
    return row, col


def _head_specs(nj, head0):
    qin = pl.BlockSpec((1, BQ, HEAD_DIM), lambda h, i: (h + head0, i, 0))
    kin = pl.BlockSpec((1, nj, BQ, HEAD_DIM), lambda h, i: (h + head0, 0, 0, 0))
    qin2 = pl.BlockSpec((1, BQ, 2 * HEAD_DIM), lambda h, i: (h + head0, i, 0))
    kin2 = pl.BlockSpec((1, nj, BQ, 2 * HEAD_DIM), lambda h, i: (h + head0, 0, 0, 0))
    qspec = pl.BlockSpec((1, BQ, HEAD_DIM), lambda h, i: (h, i, 0))
    kspec2 = pl.BlockSpec((1, nj, BQ, 2 * HEAD_DIM), lambda h, i: (h, 0, 0, 0))
    return qin, kin, qin2, kin2, qspec, kspec2


STOP = -105.0
STOP_WIDE = -115.0
FIXED_REF_MAX = 40.0


def _store_kmax(k_ref, kmax_ref, nj):
    def step(j, mx):
        kf = k_ref[0, j].astype(F32)
        return jnp.maximum(mx, jnp.max(jnp.sum(kf * kf, axis=1, keepdims=True), axis=0, keepdims=True))

    mx = lax.fori_loop(0, nj, step, jnp.zeros((1, 1), F32))
    kmax_ref[...] = jnp.broadcast_to(jnp.sqrt(mx), kmax_ref.shape)


def _qk_bound(q, kmax_ref):
    qf = q.astype(F32)
    return jnp.sqrt(jnp.sum(qf * qf, axis=1, keepdims=True)) * kmax_ref[0:1, 0:1] * 1.001


def _first_and_last_step():
    h, i = pl.program_id(0), pl.program_id(1)
    first = jnp.logical_and(h == 0, i == 0)
    last = jnp.logical_and(h == pl.num_programs(0) - 1, i == pl.num_programs(1) - 1)
    return first, last


def fox_fwd(qa, ka4, va4, fcol, frow4, *, name, gather=None):
    t = qa.shape[1]
    dh = HEAD_DIM
    nh = H_FOX
    nj = t // BQ

    def body(*refs):
        if gather is None:
            q_ref, k_ref, v_ref, fc_ref, fr_ref, o_ref, lse_ref, kmax_ref = refs
        else:
            q_ref, k_ref, v_ref, fc_ref, fr_ref, src_ref, o_ref, lse_ref, dst_ref, kmax_ref = refs[:10]
            first_step, last_step = _first_and_last_step()

            @pl.when(first_step)
            def _():
                _chip_gather(src_ref, dst_ref, *refs[10:])[0]()

        i = pl.program_id(1)

        @pl.when(i == 0)
        def _():
            _store_kmax(k_ref, kmax_ref, nj)

        q = q_ref[0]
        fq = fc_ref[0]
        bound = _qk_bound(q, kmax_ref)
        row, col = _causal_iota()

        def gate_at_block_end(j):
            return fr_ref[0, j][:, BQ - 1:BQ]

        def pv(p, j):
            p_hi, p_lo = _split2(p)
            return _dot(p_hi, v_ref[0, j]) + _dot(p_lo, v_ref[0, j])

        def walk(block, live, init):
            carry = block(i, init, True)

            def cond(c):
                n, carry = c
                return jnp.logical_and(n < i, live(jnp.maximum(i - 1 - n, 0), carry))

            _, carry = lax.while_loop(cond, lambda c: (c[0] + 1, block(i - 1 - c[0], c[1], False)), (0, carry))
            return carry

        def fixed_reference(_):
            shift = fq - bound

            def probs(j, offset):
                return jnp.exp(_dot_nt(q, k_ref[0, j]) + (shift + offset) - fr_ref[0, j])

            def live(c):
                n, acc = c
                gate = gate_at_block_end(jnp.maximum(i - 1 - n, 0))
                return jnp.logical_and(n < i, jnp.max(fq - gate - jnp.log(acc[:, dh:dh + 1])) >= STOP_WIDE)

            def two_blocks(c):
                n, acc = c
                ja = i - 1 - n
                jb = i - 2 - n
                absent = jnp.where(jb >= 0, 0.0, NEG)
                jb = jnp.maximum(jb, 0)
                return n + 2, acc + (pv(probs(ja, 0.0), ja) + pv(probs(jb, absent), jb))

            acc = pv(jnp.where(col <= row, probs(i, 0.0), 0.0), i)
            _, acc = lax.while_loop(live, two_blocks, (0, acc))
            l = acc[:, dh:dh + 1]
            return acc / l, bound + jnp.log(l)

        def running_maximum(_):
            def block(j, carry, diag):
                m, acc = carry
                s = _dot_nt(q, k_ref[0, j]) + fq - fr_ref[0, j]
                if diag:
                    s = jnp.where(col <= row, s, NEG)
                m_new = jnp.maximum(m, jnp.max(s, axis=1, keepdims=True))
                return m_new, jnp.exp(m - m_new) * acc + pv(jnp.exp(s - m_new), j)

            def live(j, carry):
                return jnp.max(bound + fq - gate_at_block_end(j) - carry[0]) >= STOP

            m, acc = walk(block, live, (jnp.full((BQ, 1), NEG, F32), jnp.zeros((BQ, 2 * dh), F32)))
            l = acc[:, dh:dh + 1]
            return acc / l, m + jnp.log(l)

        o, lse = lax.cond(jnp.max(bound) < FIXED_REF_MAX, fixed_reference, running_maximum, 0)
        o_ref[0] = o
        lse_ref[0] = lse

        if gather is not None:
            @pl.when(last_step)
            def _():
                _chip_gather(src_ref, dst_ref, *refs[10:])[1]()

    _, _, qin2, kin2, _, _ = _head_specs(nj, 0)
    cspec = pl.BlockSpec((1, BQ, 1), lambda h, i: (h, i, 0))
    in_specs = [qin2, kin2, kin2, cspec, pl.BlockSpec((1, nj, 1, BQ), lambda h, i: (h, 0, 0, 0))]
    out_specs = [pl.BlockSpec((1, BQ, 2 * dh), lambda h, i: (h, i, 0)), cspec]
    out_shape = [jax.ShapeDtypeStruct((nh, t, 2 * dh), F32), jax.ShapeDtypeStruct((nh, t, 1), F32)]
    scratch = [pltpu.VMEM((8, 128), F32)]
    args = [qa, ka4, va4, fcol, frow4]
    if gather is not None:
        in_specs.append(_ANY)
        out_specs.append(_ANY)
        out_shape.append(jax.ShapeDtypeStruct((N_CHIPS,) + gather.shape, gather.dtype))
        scratch += _chip_sems()
        args.append(gather)
    return pl.pallas_call(body, grid=(nh, nj), in_specs=in_specs, out_specs=out_specs, out_shape=out_shape,
                          scratch_shapes=scratch, compiler_params=_params(), name=name)(*args)


def _other_half(x):
    return pltpu.roll(x.astype(F32), HEAD_DIM, 1).astype(BF16)


def fox_bwd(qa, ka4, va4, dox, fcol, frow4, o, lse, *, name, scatter=None):
    t = qa.shape[1]
    dh = HEAD_DIM
    nh = H_FOX
    nj = t // BQ
    n_in = 8

    def body(*refs):
        q_ref, k_ref, v_ref, dox_ref, fc_ref, fr_ref, o_ref, lse_ref = refs[:n_in]
        if scatter is None:
            dq_ref, dkv_ref, dfk_ref, kmax_ref = refs[n_in:]
        else:
            g_ref, dq_ref, dkv_ref, dfk_ref, land_ref, kmax_ref = refs[n_in:n_in + 6]
            first_step, last_step = _first_and_last_step()

            @pl.when(first_step)
            def _():
                _chip_scatter(g_ref, land_ref, *refs[n_in + 6:])[0]()

        i = pl.program_id(1)

        @pl.when(i == 0)
        def _():
            dkv_ref[...] = jnp.zeros_like(dkv_ref)
            dfk_ref[...] = jnp.zeros_like(dfk_ref)
            _store_kmax(k_ref, kmax_ref, nj)

        q = q_ref[0]
        fq = fc_ref[0]
        lse_q = lse_ref[0]
        do_x = dox_ref[0]
        dd = jnp.sum(do_x.astype(F32) * o_ref[0], axis=1, keepdims=True)
        rhs = jnp.concatenate([q, _other_half(do_x)], axis=0)
        edge = _qk_bound(q, kmax_ref) + fq - lse_q

        def negligible(j):
            return jnp.logical_and(j < i, jnp.max(edge - fr_ref[0, j][:, BQ - 1:BQ]) < STOP_WIDE)

        first = lax.while_loop(negligible, lambda j: j + 1, 0)

        shift = fq - lse_q

        def block(j, offset, diag):
            k = k_ref[0, j]
            p = jnp.exp(_dot_nt(q, k) + (shift + offset) - fr_ref[0, j])
            if diag:
                row, col = _causal_iota()
                p = jnp.where(col <= row, p, 0.0)
            ds = p * (_dot_nt(do_x, v_ref[0, j]) - dd)
            ds_b = ds.astype(BF16)
            dkv_ref[0, j] += _dot_tn(jnp.concatenate([ds_b, p.astype(BF16)], axis=0), rhs)
            dfk_ref[0, j] -= jnp.sum(ds, axis=0, keepdims=True)
            return _dot(ds_b, k)

        def two_blocks(n, dq):
            ja = first + 2 * n
            jb = ja + 1
            absent = jnp.where(jb < i, 0.0, NEG)
            jb = jnp.minimum(jb, i - 1)
            return dq + (block(ja, 0.0, False) + block(jb, absent, False))

        dq = lax.fori_loop(0, (i - first + 1) // 2, two_blocks, jnp.zeros((BQ, 2 * dh), F32))
        dq_ref[0] = dq + block(i, 0.0, True)

        if scatter is not None:
            @pl.when(last_step)
            def _():
                _chip_scatter(g_ref, land_ref, *refs[n_in + 6:])[1]()

    _, _, qin2, kin2, _, kspec2 = _head_specs(nj, 0)
    cspec = pl.BlockSpec((1, BQ, 1), lambda h, i: (h, i, 0))
    rspec = pl.BlockSpec((1, nj, 1, BQ), lambda h, i: (h, 0, 0, 0))
    wide = pl.BlockSpec((1, BQ, 2 * dh), lambda h, i: (h, i, 0))
    in_specs = [qin2, kin2, kin2, qin2, cspec, rspec, wide, cspec]
    out_specs = [wide, kspec2, rspec]
    out_shape = [jax.ShapeDtypeStruct((nh, t, 2 * dh), F32), jax.ShapeDtypeStruct((nh, nj, BQ, 2 * dh), F32),
                 jax.ShapeDtypeStruct((nh, nj, 1, BQ), F32)]
    scratch = [pltpu.VMEM((8, 128), F32)]
    args = [qa, ka4, va4, dox, fcol, frow4, o, lse]
    if scatter is not None:
        in_specs.append(_ANY)
        out_specs.append(_ANY)
        out_shape.append(jax.ShapeDtypeStruct(scatter.shape, scatter.dtype))
        scratch += _chip_sems()
        args.append(scatter)
    return pl.pallas_call(body, grid=(nh, nj), in_specs=in_specs, out_specs=out_specs, out_shape=out_shape,
                          scratch_shapes=scratch, compiler_params=_params(), name=name)(*args)


def _sb_logs(z, diag):
    e = jnp.exp(-jnp.abs(z))
    sp = jnp.log(1.0 + e)
    logb = jnp.minimum(z, 0.0) - sp
    lom = -jnp.maximum(z, 0.0) - sp
    strict = None
    if diag:
        row, col = _causal_iota()
        strict = col < row
        lom = jnp.where(strict, lom, 0.0)
    return logb, lom, e, strict


SB_GROUP = BQ // 2


def _sums_over_later_keys(lom, tri_m):
    halves = [lom[:, :SB_GROUP], lom[:, SB_GROUP:]]
    totals = [jnp.sum(x, axis=1, keepdims=True) for x in halves]
    within = []
    for x in halves:
        hi, lo = _split2(x)
        within.append(_dot(hi, tri_m) + _dot(lo, tri_m))
    return jnp.concatenate([within[0] + totals[1], within[1]], axis=1), totals[0] + totals[1]


def _sums_over_earlier_keys(da, tri_m):
    halves = [da[:, :SB_GROUP], da[:, SB_GROUP:]]
    totals = [jnp.sum(x, axis=1, keepdims=True) for x in halves]
    within = [_dot_nt(x.astype(BF16), tri_m) for x in halves]
    return jnp.concatenate([within[0], within[1] + totals[0]], axis=1), totals[0] + totals[1]


def sb_fwd(qa, ka4, va4, tri, *, name, gather=None):
    t = qa.shape[1]
    dh = HEAD_DIM
    nh = H_SB
    nj = t // BQ
    assert nj <= 128

    def body(*refs):
        if gather is None:
            q_ref, k_ref, v_ref, tri_ref, o_ref, rs_ref = refs
        else:
            q_ref, k_ref, v_ref, tri_ref, src_ref, o_ref, rs_ref, dst_ref = refs[:8]
            first_step, last_step = _first_and_last_step()

            @pl.when(first_step)
            def _():
                _chip_gather(src_ref, dst_ref, *refs[8:])[0]()

        i = pl.program_id(1)
        q = q_ref[0]
        tri_m = tri_ref[...]
        lane = lax.broadcasted_iota(jnp.int32, (BQ, 128), 1)

        def block(j, carry, diag):
            run, acc, rall = carry
            logb, lom, _, strict = _sb_logs(_dot_nt(q, k_ref[0, j]), diag)
            later, total = _sums_over_later_keys(lom, tri_m)
            w = jnp.exp(logb + later + run)
            if diag:
                w = jnp.where(strict, w, 0.0)
            acc = acc + _dot(w.astype(BF16), v_ref[0, j])
            rall = jnp.where(lane == j, run, rall)
            return run + total, acc, rall

        init = (jnp.zeros((BQ, 1), F32), jnp.zeros((BQ, 2 * dh), F32), jnp.full((BQ, 128), NEG, F32))
        carry = block(i, init, True)

        def cond(c):
            n, carry = c
            return jnp.logical_and(n < i, jnp.max(carry[0]) >= STOP)

        _, (_, acc, rall) = lax.while_loop(cond, lambda c: (c[0] + 1, block(i - 1 - c[0], c[1], False)), (0, carry))
        o_ref[0] = acc[:, :dh].astype(BF16)
        rs_ref[0] = rall

        if gather is not None:
            @pl.when(last_step)
            def _():
                _chip_gather(src_ref, dst_ref, *refs[8:])[1]()

    _, _, qin2, kin2, qspec, _ = _head_specs(nj, H_FOX)
    rspec = pl.BlockSpec((1, BQ, 128), lambda h, i: (h, i, 0))
    in_specs = [qin2, kin2, kin2, pl.BlockSpec((SB_GROUP, SB_GROUP), lambda h, i: (0, 0))]
    out_specs = [qspec, rspec]
    out_shape = [jax.ShapeDtypeStruct((nh, t, dh), BF16), jax.ShapeDtypeStruct((nh, t, 128), F32)]
    scratch = []
    args = [qa, ka4, va4, tri]
    if gather is not None:
        in_specs.append(_ANY)
        out_specs.append(_ANY)
        out_shape.append(jax.ShapeDtypeStruct((N_CHIPS,) + gather.shape, gather.dtype))
        scratch += _chip_sems()
        args.append(gather)
    return pl.pallas_call(body, grid=(nh, nj), in_specs=in_specs, out_specs=out_specs, out_shape=out_shape,
                          scratch_shapes=scratch, compiler_params=_params(), name=name)(*args)


def sb_bwd(qa, ka4, va4, dox, tri, rsave, *, name, scatter=None):
    t = qa.shape[1]
    dh = HEAD_DIM
    nh = H_SB
    nj = t // BQ
    n_in = 6

    def body(*refs):
        q_ref, k_ref, v_ref, dox_ref, tri_ref, rs_ref = refs[:n_in]
        if scatter is None:
            dq_ref, dkv_ref = refs[n_in:]
        else:
            g_ref, dq_ref, dkv_ref, land_ref = refs[n_in:n_in + 4]
            first_step, last_step = _first_and_last_step()

            @pl.when(first_step)
            def _():
                _chip_scatter(g_ref, land_ref, *refs[n_in + 4:])[0]()

        i = pl.program_id(1)

        @pl.when(i == 0)
        def _():
            dkv_ref[...] = jnp.zeros_like(dkv_ref)

        q = q_ref[0]
        do_x = dox_ref[0]
        tri_m = tri_ref[...]
        rall = rs_ref[0]
        lane = lax.broadcasted_iota(jnp.int32, (BQ, 128), 1)
        rhs = jnp.concatenate([q, _other_half(do_x)], axis=0)
        lane1 = lax.broadcasted_iota(jnp.int32, (1, 128), 1)
        unvisited = jnp.logical_and(lane1 < i, jnp.max(rall, axis=0, keepdims=True) < STOP)
        first = jnp.sum(unvisited.astype(jnp.int32))

        def block(j, carry, diag):
            dq, ecar = carry
            k = k_ref[0, j]
            z = _dot_nt(q, k)
            logb, lom, e, strict = _sb_logs(z, diag)
            run = jnp.sum(jnp.where(lane == j, rall, 0.0), axis=1, keepdims=True)
            w = jnp.exp(logb + _sums_over_later_keys(lom, tri_m)[0] + run)
            if diag:
                w = jnp.where(strict, w, 0.0)
            da = w * _dot_nt(do_x, v_ref[0, j])
            earlier, da_total = _sums_over_earlier_keys(da, tri_m)
            before = earlier + ecar
            inv = 1.0 / (1.0 + e)
            beta = jnp.where(z >= 0.0, 1.0, e) * inv
            one_minus = jnp.where(z >= 0.0, e, 1.0) * inv
            dz = da * one_minus - before * beta
            if diag:
                dz = jnp.where(strict, dz, 0.0)
            dz_b = dz.astype(BF16)
            dkv_ref[0, j] += _dot_tn(jnp.concatenate([dz_b, w.astype(BF16)], axis=0), rhs)
            return dq + _dot(dz_b, k), ecar + da_total

        carry = lax.fori_loop(first, i, lambda j, c: block(j, c, False),
                              (jnp.zeros((BQ, 2 * dh), F32), jnp.zeros((BQ, 1), F32)))
        dq, _ = block(i, carry, True)
        dq_ref[0] = dq

        if scatter is not None:
            @pl.when(last_step)
            def _():
                _chip_scatter(g_ref, land_ref, *refs[n_in + 4:])[1]()

    _, _, qin2, kin2, _, kspec2 = _head_specs(nj, H_FOX)
    in_specs = [qin2, kin2, kin2, qin2, pl.BlockSpec((SB_GROUP, SB_GROUP), lambda h, i: (0, 0)),
                pl.BlockSpec((1, BQ, 128), lambda h, i: (h, i, 0))]
    out_specs = [pl.BlockSpec((1, BQ, 2 * dh), lambda h, i: (h, i, 0)), kspec2]
    out_shape = [jax.ShapeDtypeStruct((nh, t, 2 * dh), F32), jax.ShapeDtypeStruct((nh, nj, BQ, 2 * dh), F32)]
    scratch = []
    args = [qa, ka4, va4, dox, tri, rsave]
    if scatter is not None:
        in_specs.append(_ANY)
        out_specs.append(_ANY)
        out_shape.append(jax.ShapeDtypeStruct(scatter.shape, scatter.dtype))
        scratch += _chip_sems()
        args.append(scatter)
    return pl.pallas_call(body, grid=(nh, nj), in_specs=in_specs, out_specs=out_specs, out_shape=out_shape,
                          scratch_shapes=scratch, compiler_params=_params(), name=name)(*args)


def loss_head(y, target, *, name, tm=512):
    t, d = y.shape

    def body(y_ref, t_ref, l_ref, dy_ref, acc_ref):
        i = pl.program_id(0)
        diff = y_ref[...] - t_ref[...]
        dy_ref[...] = diff * (1.0 / d)
        part = jnp.sum(diff * diff, axis=0, keepdims=True)

        @pl.when(i == 0)
        def _():
            acc_ref[...] = part

        @pl.when(i > 0)
        def _():
            acc_ref[...] += part

        @pl.when(i == pl.num_programs(0) - 1)
        def _():
            l_ref[...] = jnp.full(l_ref.shape, (0.5 / d) * jnp.sum(acc_ref[...]), F32)

    row = pl.BlockSpec((tm, d), lambda i: (i, 0))
    return pl.pallas_call(
        body, grid=(t // tm,), in_specs=[row, row],
        out_specs=[pl.BlockSpec((8, 128), lambda i: (0, 0)), row],
        out_shape=[jax.ShapeDtypeStruct((8, 128), F32), jax.ShapeDtypeStruct((t, d), F32)],
        scratch_shapes=[pltpu.VMEM((1, d), F32)], compiler_params=_params(), name=name)(y, target)


def _from_heads(a):
    t = a.shape[1]
    return a.transpose(1, 0, 2).reshape(t, MIX)


def _lanes_to_chunks(a):
    r, t = a.shape
    return a.reshape(r, t // 128, 128).transpose(1, 0, 2)


def _chunks_to_lanes(a):
    nc, r, _ = a.shape
    return a.transpose(1, 0, 2).reshape(r, nc * 128)


def _constants():
    idx = jnp.arange(128)
    bd = (idx[:, None] // HEAD_DIM == idx[None, :] // HEAD_DIM).astype(BF16)
    tri_le = (idx[:, None] <= idx[None, :]).astype(BF16)
    tri_ge = (idx[:, None] >= idx[None, :]).astype(BF16)
    jdx = jnp.arange(SB_GROUP)
    tri_gt = (jdx[:, None] > jdx[None, :]).astype(BF16)
    return dict(bd=bd, tri_le=tri_le, tri_ge=tri_ge, tri_gt=tri_gt)


def attn_layer_fwd(h, w, cst, gather=None):
    t = h.shape[0]
    nj = t // BQ
    xn, proj = rms_mm_nn(h, w["norm"], w["w_in"], tn=640, name="attn_in_proj")
    qa, ka, va, logf = attn_prep_fwd(proj, w["gq"], w["gk"], w["fbias"], cst["bd"], name="attn_prep_fwd")
    logf3 = _lanes_to_chunks(logf)
    cum = _chunks_to_lanes(gate_cumsum(logf3, cst["tri_le"], name="gate_cumsum"))
    fcol = cum.reshape(H_FOX, t, 1)
    frow4 = cum.reshape(H_FOX, nj, 1, BQ)
    ka4 = ka.reshape(N_HEADS, nj, BQ, 2 * HEAD_DIM)
    va4 = va.reshape(N_HEADS, nj, BQ, 2 * HEAD_DIM)
    if gather is None:
        (o_f, lse), (o_s, rsave), gathered = (fox_fwd(qa, ka4, va4, fcol, frow4, name="fox_fwd"),
                                              sb_fwd(qa, ka4, va4, cst["tri_gt"], name="sb_fwd"), None)
    else:
        o_f, lse, gathered_a = fox_fwd(qa, ka4, va4, fcol, frow4, name="fox_fwd_gather", gather=gather[0])
        o_s, rsave, gathered_b = sb_fwd(qa, ka4, va4, cst["tri_gt"], name="sb_fwd_gather", gather=gather[1])
        gathered = gather[2]((gathered_a, gathered_b))
        w = gathered[0][0]
    o = _from_heads(jnp.concatenate([o_f[:, :, :HEAD_DIM].astype(BF16), o_s], axis=0))
    h2 = mm_nn(o, w["w_out"], add=h, name="mix_out_proj")
    saved = dict(h=h, xn=xn, proj=proj, logf3=logf3, fcol=fcol, frow4=frow4, qa=qa, ka4=ka4, va4=va4,
                 o_f=o_f, lse=lse, rsave=rsave, o=o)
    return h2, saved, gathered


def attn_layer_bwd(dh, w, s, cst, scatter=None):
    t = dh.shape[0]
    dh3 = dh[None]
    w_out_heads = jnp.pad(w["w_out"].reshape(N_HEADS, HEAD_DIM, -1), ((0, 0), (0, HEAD_DIM), (0, 0)))
    dox = mm_nt_heads(dh, w_out_heads, name="mix_out_bwd_heads")
    g_w_out = mm_tn(s["o"], dh3, name="mix_out_wgrad")
    fox_args = (s["qa"], s["ka4"], s["va4"], dox, s["fcol"], s["frow4"], s["o_f"], s["lse"])
    sb_args = (s["qa"], s["ka4"], s["va4"], dox, cst["tri_gt"], s["rsave"])
    if scatter is None:
        (dq_f, dkv_f, dfk), (dq_s, dkv_s), landed = fox_bwd(*fox_args, name="fox_bwd"), sb_bwd(*sb_args, name="sb_bwd"), None
    else:
        chunks_a, chunks_b = scatter(g_w_out)
        dq_f, dkv_f, dfk, landed_a = fox_bwd(*fox_args, name="fox_bwd_scatter", scatter=chunks_a)
        dq_s, dkv_s, landed_b = sb_bwd(*sb_args, name="sb_bwd_scatter", scatter=chunks_b)
        landed = (landed_a, landed_b)
    dcum3 = _lanes_to_chunks(dfk.reshape(H_FOX, t))
    dfl3, dbias = gate_cumsum_bwd(dcum3, s["logf3"], cst["tri_ge"], name="gate_cumsum_bwd")
    dfl = jnp.pad(_chunks_to_lanes(dfl3).T, ((0, 0), (0, 128 - H_FOX))).astype(BF16)
    wide = (H_FOX, t, 2 * HEAD_DIM)
    dproj, dgq, dgk = attn_prep_bwd(s["proj"], dq_f, dq_s, dkv_f.reshape(wide), dkv_s.reshape(wide), dfl, w["gq"],
                                    w["gk"], cst["bd"], name="attn_prep_bwd")
    g_w_in = mm_tn(s["xn"], dproj[None], tn=640, name="attn_in_wgrad")[:, :ATTN_IN]
    dh2, g_norm = mm_nt_rms_bwd(dproj[None], w["w_in"], s["h"], w["norm"], dh, name="attn_in_bwd")
    dgq = dgq.reshape(N_HEADS, HEAD_DIM)
    dgk = dgk.reshape(N_HEADS, HEAD_DIM)
    grads = dict(norm=g_norm[0], w_in=g_w_in, f_bias=dbias[:, 0], fox_q=dgq[:H_FOX].sum(0), fox_k=dgk[:H_FOX].sum(0),
                 sb_q=dgq[H_FOX:].sum(0), sb_k=dgk[H_FOX:].sum(0), w_out=g_w_out)
    return dh2, grads, landed


def conv_layer_fwd(h, w):
    xn, proj3 = rms_mm_nn(h, w["norm"], w["w_in"], parts=3, name="conv_in_proj")
    y = conv_mix_fwd(proj3, w["ck"], name="conv_mix_fwd")
    h2 = mm_nn(y, w["w_out"], add=h, name="mix_out_proj")
    return h2, dict(h=h, xn=xn, proj3=proj3, y=y)


def conv_layer_bwd(dh, w, s):
    dh3 = dh[None]
    g_w_out = mm_tn(s["y"], dh3, name="mix_out_wgrad")
    dproj3, dck = conv_mix_bwd(dh, w["w_out"], s["proj3"], w["ck"], name="conv_mix_bwd")
    g_w_in = mm_tn(s["xn"], dproj3, name="conv_in_wgrad")
    dh2, g_norm = mm_nt_rms_bwd(dproj3, w["w_in"], s["h"], w["norm"], dh, name="conv_in_bwd")
    return dh2, dict(norm=g_norm[0], w_in=g_w_in, ck=dck[0, :3], w_out=g_w_out)


def ffn_layer_fwd(h, w):
    xn, up2 = rms_mm_nn(h, w["norm"], w["w_up"], parts=2, tn=1408, name="ffn_up_proj")
    h2, act = ffn_act_down_fwd(up2, w["cw2"], w["w_down"], h, name="ffn_act_down_fwd")
    return h2, dict(h=h, xn=xn, up2=up2, act=act)


def ffn_layer_bwd(dh, w, s):
    dh3 = dh[None]
    g_w_down = mm_tn(s["act"], dh3, tk=1408, name="ffn_down_wgrad")
    dup2, dcw = ffn_act_bwd(dh, w["w_down"], s["up2"], w["cw2"], name="ffn_act_bwd")
    g_w_up = mm_tn(s["xn"], dup2, tn=1408, name="ffn_up_wgrad")
    dh2, g_norm = mm_nt_rms_bwd(dup2, w["w_up"], s["h"], w["norm"], dh, name="ffn_up_bwd")
    g_cw = jnp.concatenate([dcw[0, :3], dcw[1, :3]], axis=1)
    return dh2, dict(norm=g_norm[0], w_up=g_w_up, cw=g_cw, w_down=g_w_down)


def forward_backward(x, target, wa, wc, wf, *, late_weights=None, late_chunks=None):
    cst = _constants()
    h = x
    saved = []
    layer = 0
    while layer == 0 or layer < len(wf):
        i = layer // 2
        if layer % 2 == 0:
            h, sm, built = attn_layer_fwd(h, wa[i], cst, gather=late_weights if late_weights and layer == 0 else None)
            if built is not None:
                wa, wc, wf = built
        else:
            h, sm = conv_layer_fwd(h, wc[i])
        h, sf = ffn_layer_fwd(h, wf[layer])
        saved.append((sm, sf))
        layer += 1
    depth = len(wf)
    loss_blk, dh = loss_head(h, target, name="loss_head")
    ga, gc, gf = [None] * len(wa), [None] * len(wc), [None] * depth
    landed = None
    for layer in reversed(range(depth)):
        i = layer // 2
        sm, sf = saved[layer]
        dh, gf[layer] = ffn_layer_bwd(dh, wf[layer], sf)
        if layer % 2 == 0:
            chunks = None
            if late_chunks and layer == 0:
                chunks = lambda g_w_out: late_chunks([dict(w_out=g_w_out)] + ga[1:], gc, gf)
            dh, ga[i], got = attn_layer_bwd(dh, wa[i], sm, cst, scatter=chunks)
            landed = got if got is not None else landed
        else:
            dh, gc[i] = conv_layer_bwd(dh, wc[i], sm)
    return loss_blk, dh, ga, gc, gf, landed


def _part_rows(shape, width, row_mult):
    n = 1
    for s in shape:
        n *= s
    rows = -(-n // width)
    return -(-rows // row_mult) * row_mult


def _pack_rows(arrs, width, row_mult, dtype, total_rows=None, lead=0):
    parts = []
    used = 0
    for a in arrs:
        outer = a.shape[:lead]
        rows = _part_rows(a.shape[lead:], width, row_mult)
        flat = a.astype(dtype).reshape(outer + (-1,))
        flat = jnp.pad(flat, ((0, 0),) * lead + ((0, rows * width - flat.shape[-1]),))
        parts.append(flat.reshape(outer + (rows, width)))
        used += rows
    if total_rows is not None and total_rows > used:
        parts.append(jnp.zeros(parts[0].shape[:lead] + (total_rows - used, width), dtype))
    return jnp.concatenate(parts, axis=lead)


def _unpack_rows(packed, shapes, width, row_mult, lead=0):
    outer = packed.shape[:lead]
    out = []
    off = 0
    for shape in shapes:
        rows = _part_rows(shape, width, row_mult)
        n = 1
        for s in shape:
            n *= s
        flat = lax.slice_in_dim(packed, off, off + rows, axis=lead).reshape(outer + (-1,))
        out.append(lax.slice_in_dim(flat, 0, n, axis=lead).reshape(outer + tuple(shape)))
        off += rows
    return out


BIG_NAMES = ("attn_w_in", "attn_w_out", "conv_w_in", "conv_w_out", "ffn_w_up", "ffn_w_down")
BIG_AXIS = {"attn_w_in": 2, "attn_w_out": 1, "conv_w_in": 2, "conv_w_out": 1, "ffn_w_up": 2, "ffn_w_down": 1}
BIG_WIDTH = 1024
BIG_ROW_MULT = 16
BIG_TILE = 512
SMALL_TILE = 128
SMALL_SHARDED = ("conv_norm", "conv_kernel", "ffn_conv")
SMALL_AXIS = {"conv_norm": 1, "conv_kernel": 2, "ffn_conv": 2}
SMALL_REPLICATED = ("attn_norm", "attn_f_bias", "fox_q_gain", "fox_k_gain", "sb_q_gain", "sb_k_gain", "ffn_norm")
WEIGHT_ORDER = ("attn_norm", "attn_w_in", "attn_f_bias", "fox_q_gain", "fox_k_gain", "sb_q_gain", "sb_k_gain",
                "attn_w_out", "conv_norm", "conv_w_in", "conv_kernel", "conv_w_out", "ffn_norm", "ffn_w_up",
                "ffn_conv", "ffn_w_down")


def _big_total_rows(shapes):
    used = sum(_part_rows(s, BIG_WIDTH, BIG_ROW_MULT) for s in shapes)
    tile = BIG_TILE if used >= 8 * BIG_TILE else SMALL_TILE
    return -(-used // tile) * tile


def _place():
    x, y, c = lax.axis_index("x"), lax.axis_index("y"), lax.axis_index("c")
    other_chips = [(1 - x, y), (x, 1 - y), (1 - x, 1 - y)]
    return x, y, c, other_chips


_ANY = pl.BlockSpec(memory_space=pl.ANY)


def _chip_sems():
    return [pltpu.SemaphoreType.DMA((3,)), pltpu.SemaphoreType.DMA((3,)), pltpu.SemaphoreType.DMA]


def _chip_gather(src_ref, dst_ref, send_sems, recv_sems, local_sem):
    x, y, c, chips = _place()
    k = 2 * x + y

    def copy(j, slot):
        px, py = chips[j]
        return pltpu.make_async_remote_copy(src_ref=src_ref, dst_ref=dst_ref.at[slot], send_sem=send_sems.at[j],
                                            recv_sem=recv_sems.at[j], device_id=(px, py, c), device_id_type=MESH)

    def local():
        return pltpu.make_async_copy(src_ref, dst_ref.at[k], local_sem)

    def start():
        local().start()
        for j in range(3):
            copy(j, k).start()

    def finish():
        for j, (px, py) in enumerate(chips):
            copy(j, 2 * px + py).wait_recv()
        for j in range(3):
            copy(j, k).wait_send()
        local().wait()

    return start, finish


def _chip_scatter(g_ref, o_ref, send_sems, recv_sems, local_sem):
    x, y, c, chips = _place()
    k = 2 * x + y

    def copy(j, src_slot, dst_slot):
        px, py = chips[j]
        return pltpu.make_async_remote_copy(src_ref=g_ref.at[src_slot], dst_ref=o_ref.at[dst_slot],
                                            send_sem=send_sems.at[j], recv_sem=recv_sems.at[j],
                                            device_id=(px, py, c), device_id_type=MESH)

    def local():
        return pltpu.make_async_copy(g_ref.at[k], o_ref.at[k], local_sem)

    def start():
        local().start()
        for j, (px, py) in enumerate(chips):
            copy(j, 2 * px + py, k).start()

    def finish():
        for j, (px, py) in enumerate(chips):
            copy(j, k, 2 * px + py).wait_recv()
        for j, (px, py) in enumerate(chips):
            copy(j, 2 * px + py, k).wait_send()
        local().wait()

    return start, finish


def gather_chips(arrs, *, name):
    n = len(arrs)

    def body(*refs):
        hooks = [_chip_gather(refs[m], refs[n + m], *refs[2 * n + 3 * m:2 * n + 3 * m + 3]) for m in range(n)]
        for start, _ in hooks:
            start()
        for _, finish in hooks:
            finish()

    return pl.pallas_call(
        body, in_specs=[_ANY] * n, out_specs=[_ANY] * n,
        out_shape=[jax.ShapeDtypeStruct((N_CHIPS,) + a.shape, a.dtype) for a in arrs],
        scratch_shapes=_chip_sems() * n, name=name)(*arrs)


def scatter_chips(chunks, *, name):
    def body(g_ref, o_ref, send_sems, recv_sems, local_sem):
        start, finish = _chip_scatter(g_ref, o_ref, send_sems, recv_sems, local_sem)
        start()
        finish()

    return pl.pallas_call(
        body, in_specs=[_ANY], out_specs=_ANY, out_shape=jax.ShapeDtypeStruct(chunks.shape, chunks.dtype),
        scratch_shapes=_chip_sems(), name=name)(chunks)


def swap_cores(arrs, *, name):
    n = len(arrs)

    def body(*refs):
        x, y, c, _ = _place()
        copies = [pltpu.make_async_remote_copy(src_ref=refs[m], dst_ref=refs[n + m], send_sem=refs[2 * n + 2 * m],
                                               recv_sem=refs[2 * n + 2 * m + 1], device_id=(x, y, 1 - c),
                                               device_id_type=MESH) for m in range(n)]
        for cp in copies:
            cp.start()
        for cp in copies:
            cp.wait()

    return pl.pallas_call(
        body, in_specs=[_ANY] * n, out_specs=[_ANY] * n,
        out_shape=[jax.ShapeDtypeStruct(a.shape, a.dtype) for a in arrs],
        scratch_shapes=[pltpu.SemaphoreType.DMA, pltpu.SemaphoreType.DMA] * n, name=name)(*arrs)


def allreduce_small(p, *, name):
    r, w = p.shape

    def body(p_ref, o_ref, buf, send_sems, recv_sems):
        x, y, c, _ = _place()
        me = 4 * x + 2 * y + c
        buf[me] = p_ref[...]

        def peer_of(m):
            return (1 - x if m & 4 else x, 1 - y if m & 2 else y, 1 - c if m & 1 else c)

        def copy(m, slot):
            return pltpu.make_async_remote_copy(src_ref=p_ref, dst_ref=buf.at[slot], send_sem=send_sems.at[m - 1],
                                                recv_sem=recv_sems.at[m - 1], device_id=peer_of(m),
                                                device_id_type=MESH)

        sends = [copy(m, me) for m in range(1, 8)]
        for cp in sends:
            cp.start()
        for m in range(1, 8):
            px, py, pc = peer_of(m)
            copy(m, 4 * px + 2 * py + pc).wait_recv()
        for cp in sends:
            cp.wait_send()
        acc = buf[0]
        for d in range(1, 8):
            acc = acc + buf[d]
        o_ref[...] = acc

    vm = pl.BlockSpec(memory_space=pltpu.VMEM)
    return pl.pallas_call(
        body, in_specs=[vm], out_specs=vm, out_shape=jax.ShapeDtypeStruct((r, w), F32),
        scratch_shapes=[pltpu.VMEM((8, r, w), F32), pltpu.SemaphoreType.DMA((7,)), pltpu.SemaphoreType.DMA((7,))],
        name=name)(p)


def sum_chips(rv, *, name):
    _, r, w = rv.shape
    tile = BIG_TILE if r % BIG_TILE == 0 else SMALL_TILE

    def body(a_ref, b_ref, c_ref, d_ref, o_ref):
        o_ref[...] = ((a_ref[0].astype(F32) + b_ref[0].astype(F32)) + c_ref[0].astype(F32)) + d_ref[0].astype(F32)

    spec = lambda kk: pl.BlockSpec((1, tile, w), lambda i: (kk, i, 0))
    return pl.pallas_call(
        body, grid=(r // tile,), in_specs=[spec(0), spec(1), spec(2), spec(3)],
        out_specs=pl.BlockSpec((tile, w), lambda i: (i, 0)), out_shape=jax.ShapeDtypeStruct((r, w), F32),
        compiler_params=_params(), name=name)(rv, rv, rv, rv)


def add_pair(a, b, *, name):
    r, w = a.shape
    tile = BIG_TILE if r % BIG_TILE == 0 else SMALL_TILE

    def body(a_ref, b_ref, o_ref):
        o_ref[...] = a_ref[...] + b_ref[...]

    spec = pl.BlockSpec((tile, w), lambda i: (i, 0))
    return pl.pallas_call(body, grid=(r // tile,), in_specs=[spec, spec], out_specs=spec,
                          out_shape=jax.ShapeDtypeStruct((r, w), F32), compiler_params=_params(), name=name)(a, b)


def adamw(w, g, m, v, *, tm, name):
    r, c = w.shape
    assert r % tm == 0

    def body(w_ref, g_ref, m_ref, v_ref, d_ref, nm_ref, nv_ref):
        g_ = g_ref[...]
        m_ = ADAM_B1 * m_ref[...] + (1.0 - ADAM_B1) * g_
        v_ = ADAM_B2 * v_ref[...] + (1.0 - ADAM_B2) * (g_ * g_)
        m_hat = m_ / (1.0 - ADAM_B1 ** ADAM_STEP)
        v_hat = v_ / (1.0 - ADAM_B2 ** ADAM_STEP)
        d_ref[...] = -ADAM_LR * (m_hat / (jnp.sqrt(v_hat) + ADAM_EPS) + ADAM_WD * w_ref[...])
        nm_ref[...] = m_
        nv_ref[...] = v_

    spec = pl.BlockSpec((tm, c), lambda i: (i, 0))
    return pl.pallas_call(body, grid=(r // tm,), in_specs=[spec] * 4, out_specs=[spec] * 3,
                          out_shape=[jax.ShapeDtypeStruct((r, c), F32)] * 3, compiler_params=_params(), name=name)(w, g, m, v)


def kernel(x, attn_norm, attn_w_in, attn_f_bias, fox_q_gain, fox_k_gain, sb_q_gain, sb_k_gain, attn_w_out, conv_norm, conv_w_in, conv_kernel, conv_w_out, ffn_norm, ffn_w_up, ffn_conv, ffn_w_down, loss_target, m_attn_norm, m_attn_w_in, m_attn_f_bias, m_fox_q_gain, m_fox_k_gain, m_sb_q_gain, m_sb_k_gain, m_attn_w_out, m_conv_norm, m_conv_w_in, m_conv_kernel, m_conv_w_out, m_ffn_norm, m_ffn_w_up, m_ffn_conv, m_ffn_w_down, v_attn_norm, v_attn_w_in, v_attn_f_bias, v_fox_q_gain, v_fox_k_gain, v_sb_q_gain, v_sb_k_gain, v_attn_w_out, v_conv_norm, v_conv_w_in, v_conv_kernel, v_conv_w_out, v_ffn_norm, v_ffn_w_up, v_ffn_conv, v_ffn_w_down):
    a = dict(locals())
    chip = 2 * lax.axis_index("x") + lax.axis_index("y")
    n_attn, n_conv, depth = attn_norm.shape[0], conv_norm.shape[0], ffn_norm.shape[0]

    units = [(name, l) for name in BIG_NAMES for l in range(a[name].shape[0])]
    early = [("attn_w_in", 0)]
    late = [u for u in units if u not in early]
    late_b = [("attn_w_out", 0), ("conv_w_in", n_conv - 1), ("conv_w_out", n_conv - 1), ("ffn_w_up", depth - 1),
              ("ffn_w_down", depth - 1)]
    late_a = [u for u in late if u not in late_b]
    late_sb = [("ffn_w_up", 0), ("ffn_w_down", 0), ("ffn_w_up", 1), ("ffn_w_down", 1)]
    late_sa = [u for u in late if u not in late_sb]

    def unit_shape(u):
        return a[u[0]].shape[1:]

    def pack_units(us, get, lead=0):
        return _pack_rows([get(u) for u in us], BIG_WIDTH, BIG_ROW_MULT, BF16, _big_total_rows([unit_shape(u) for u in us]),
                          lead=lead)

    def unpack_units(packed, us, lead=0):
        return dict(zip(us, _unpack_rows(packed, [unit_shape(u) for u in us], BIG_WIDTH, BIG_ROW_MULT, lead=lead)))

    def full_units(gathered, us):
        full_size = {}
        for u, g4 in unpack_units(gathered, us, lead=1).items():
            _, rows, cols = g4.shape
            if BIG_AXIS[u[0]] - 1 == 0:
                full_size[u] = g4.reshape(N_CHIPS * rows, cols)
            else:
                full_size[u] = g4.transpose(1, 0, 2).reshape(rows, N_CHIPS * cols)
        return full_size

    def shard(u):
        return a[u[0]][u[1]]

    small_shapes = [a[n].shape for n in SMALL_SHARDED]
    packed_s = _pack_rows([a[n] for n in SMALL_SHARDED], 128, 8, F32)
    gath_e, gath_s = gather_chips([pack_units(early, shard), packed_s], name="gather_weights")
    full_e = full_units(gath_e, early)
    full = {}
    per_chip = [_unpack_rows(gath_s[kk], small_shapes, 128, 8) for kk in range(N_CHIPS)]
    for n, name in enumerate(SMALL_SHARDED):
        full[name] = jnp.concatenate([per_chip[kk][n] for kk in range(N_CHIPS)], axis=SMALL_AXIS[name])

    def attn_weights(i, fu):
        return dict(
            norm=attn_norm[i][None],
            w_in=jnp.pad(fu[("attn_w_in", i)], ((0, 0), (0, ATTN_IN_PAD - ATTN_IN))),
            fbias=jnp.pad(attn_f_bias[i], (0, 128 - H_FOX))[None],
            gq=jnp.concatenate([jnp.tile(fox_q_gain[i], H_FOX), jnp.tile(sb_q_gain[i], H_SB)])[None],
            gk=jnp.concatenate([jnp.tile(fox_k_gain[i], H_FOX), jnp.tile(sb_k_gain[i], H_SB)])[None],
            w_out=fu.get(("attn_w_out", i)))

    def build_weights(gathered):
        fu = {**full_e, **full_units(gathered[0], late_a), **full_units(gathered[1], late_b)}
        wa = [attn_weights(i, fu) for i in range(n_attn)]
        wc = [dict(norm=full["conv_norm"][i][None], w_in=fu[("conv_w_in", i)], ck=full["conv_kernel"][i][None],
                   w_out=fu[("conv_w_out", i)]) for i in range(n_conv)]
        wf = []
        for l in range(depth):
            cw = full["ffn_conv"][l]
            wf.append(dict(norm=ffn_norm[l][None], w_up=fu[("ffn_w_up", l)], cw2=jnp.stack([cw[:, :D_FF], cw[:, D_FF:]]),
                           w_down=fu[("ffn_w_down", l)]))
        return wa, wc, wf

    def chunks_of_unit(u, ga, gc, gf):
        name, l = u
        g = {"attn_w_in": lambda: ga[l]["w_in"], "attn_w_out": lambda: ga[l]["w_out"],
             "conv_w_in": lambda: gc[l]["w_in"], "conv_w_out": lambda: gc[l]["w_out"],
             "ffn_w_up": lambda: gf[l]["w_up"], "ffn_w_down": lambda: gf[l]["w_down"]}[name]()
        rows, cols = unit_shape(u)
        if BIG_AXIS[name] - 1 == 0:
            return g.reshape(N_CHIPS, rows, cols)
        return g.reshape(rows, N_CHIPS, cols).transpose(1, 0, 2)

    def chunks_of(us, ga, gc, gf):
        return pack_units(us, lambda u: chunks_of_unit(u, ga, gc, gf), lead=1)

    loss_blk, grad_x, ga, gc, gf, landed_late = forward_backward(
        x[0], loss_target[0], [attn_weights(0, full_e)], [], [],
        late_weights=(pack_units(late_a, shard), pack_units(late_b, shard), build_weights),
        late_chunks=lambda ga, gc, gf: (chunks_of(late_sa, ga, gc, gf), chunks_of(late_sb, ga, gc, gf)))

    landed = [scatter_chips(chunks_of(early, ga, gc, gf), name="scatter_grads"), landed_late[0], landed_late[1]]
    mine = [sum_chips(buf, name="sum_chips") for buf in landed]
    theirs = swap_cores(mine, name="swap_cores")
    g_units = {}
    for us, m, th in zip((early, late_sa, late_sb), mine, theirs):
        g_units.update(unpack_units(add_pair(m, th, name="add_cores"), us))
    grads = {name: jnp.stack([g_units[(name, l)] for l in range(a[name].shape[0])]) for name in BIG_NAMES}

    small_full = [
        loss_blk,
        jnp.stack([g["norm"] for g in ga]), jnp.stack([g["f_bias"] for g in ga]),
        jnp.stack([g["fox_q"] for g in ga]), jnp.stack([g["fox_k"] for g in ga]),
        jnp.stack([g["sb_q"] for g in ga]), jnp.stack([g["sb_k"] for g in ga]),
        jnp.stack([g["norm"] for g in gf]),
        jnp.stack([g["norm"] for g in gc]), jnp.stack([g["ck"] for g in gc]), jnp.stack([g["cw"] for g in gf]),
    ]
    summed = allreduce_small(_pack_rows(small_full, 128, 8, F32), name="allreduce_small")
    parts = _unpack_rows(summed, [p.shape for p in small_full], 128, 8)
    loss = parts[0][0, 0]
    for name, g in zip(SMALL_REPLICATED, parts[1:8]):
        grads[name] = g
    for name, g in zip(SMALL_SHARDED, parts[8:]):
        width = a[name].shape[SMALL_AXIS[name]]
        grads[name] = lax.dynamic_slice_in_dim(g, chip * width, width, axis=SMALL_AXIS[name])

    delta, new_m, new_v = {}, {}, {}
    for name in BIG_NAMES:
        shape = a[name].shape
        flat = lambda arr: arr.reshape(-1, shape[-1])
        d_, m_, v_ = adamw(flat(a[name]), flat(grads[name]), flat(a["m_" + name]), flat(a["v_" + name]), tm=256,
                           name="adamw")
        delta[name], new_m[name], new_v[name] = d_.reshape(shape), m_.reshape(shape), v_.reshape(shape)
    small_names = SMALL_REPLICATED + SMALL_SHARDED
    small_shapes_local = [a[n].shape for n in small_names]
    pack = lambda prefix, src: _pack_rows([src[prefix + n] for n in small_names], 128, 8, F32)
    packed = adamw(pack("", a), pack("", grads), pack("m_", a), pack("v_", a), tm=8, name="adamw_small")
    for store, buf in zip((delta, new_m, new_v), packed):
        for name, arr in zip(small_names, _unpack_rows(buf, small_shapes_local, 128, 8)):
            store[name] = arr

    return (loss, grad_x[None], *[grads[n] for n in WEIGHT_ORDER], *[delta[n] for n in WEIGHT_ORDER],
            *[new_m[n] for n in WEIGHT_ORDER], *[new_v[n] for n in WEIGHT_ORDER])
```

```python
import functools

import jax
import jax.numpy as jnp
from jax import lax
from jax.experimental import pallas as pl
from jax.experimental.pallas import tpu as pltpu

F32 = jnp.float32
BF16 = jnp.bfloat16

D_MODEL = 1024
HEAD_DIM = 64
H_FOX = 8
H_SB = 8
N_HEADS = H_FOX + H_SB
MIX = N_HEADS * HEAD_DIM
ATTN_IN = 3 * MIX + H_FOX
ATTN_IN_PAD = 3 * MIX + 128
D_FF = 2816
EPS = 1e-6
SCALE = HEAD_DIM ** -0.5
NEG = -1e30

ADAM_LR = 0.001
ADAM_B1 = 0.9
ADAM_B2 = 0.999
ADAM_EPS = 1e-08
ADAM_WD = 0.01
ADAM_STEP = 10

VMEM_LIMIT = 56 * 1024 * 1024
HALO = 8
BQ = 512
N_CHIPS = 4
MESH = pl.DeviceIdType.MESH


def _params(**kw):
    return pltpu.CompilerParams(vmem_limit_bytes=VMEM_LIMIT, **kw)


def _dot(a, b):
    return jnp.dot(a, b, preferred_element_type=F32)


def _dot_nt(a, b):
    return lax.dot_general(a, b, (((1,), (1,)), ((), ())), preferred_element_type=F32)


def _dot_tn(a, b):
    return lax.dot_general(a, b, (((0,), (0,)), ((), ())), preferred_element_type=F32)


def _split2(x):
    hi = x.astype(BF16)
    lo = (x - hi.astype(F32)).astype(BF16)
    return hi, lo


def _split3(x):
    hi = x.astype(BF16)
    r = x - hi.astype(F32)
    mid = r.astype(BF16)
    lo = (r - mid.astype(F32)).astype(BF16)
    return hi, mid, lo


def mm_nn(a, b, *, add=None, out_dtype=F32, parts=1, tm=1024, tn=512, name):
    m, k = a.shape
    n = b.shape[1]
    np_ = n // parts
    nb = np_ // tn
    tm = min(tm, m)
    assert m % tm == 0 and np_ % tn == 0

    def body(*refs):
        if add is None:
            a_ref, b_ref, o_ref = refs
            acc = _dot(a_ref[...].astype(BF16), b_ref[...])
        else:
            a_ref, b_ref, r_ref, o_ref = refs
            acc = _dot(a_ref[...].astype(BF16), b_ref[...]) + r_ref[...]
        o_ref[...] = acc.astype(out_dtype).reshape(o_ref.shape)

    in_specs = [pl.BlockSpec((tm, k), lambda i, j: (i, 0)), pl.BlockSpec((k, tn), lambda i, j: (0, j))]
    args = [a, b]
    if add is not None:
        in_specs.append(pl.BlockSpec((tm, tn), lambda i, j: (i, j)))
        args.append(add)
    if parts == 1:
        out_spec = pl.BlockSpec((tm, tn), lambda i, j: (i, j))
        out_shape = jax.ShapeDtypeStruct((m, n), out_dtype)
    else:
        out_spec = pl.BlockSpec((1, tm, tn), lambda i, j: (j // nb, i, j % nb))
        out_shape = jax.ShapeDtypeStruct((parts, m, np_), out_dtype)
    return pl.pallas_call(body, grid=(m // tm, n // tn), in_specs=in_specs, out_specs=out_spec,
                          out_shape=out_shape, compiler_params=_params(), name=name)(*args)


def mm_nt_heads(a, b3, *, tm=512, name):
    t, d = a.shape
    nh, n, _ = b3.shape
    tm = min(tm, t)
    assert t % tm == 0

    def body(a_ref, b_ref, o_ref):
        a_b = a_ref[...].astype(BF16)
        for h in range(nh):
            o_ref[h] = _dot_nt(a_b, b_ref[h]).astype(BF16)

    return pl.pallas_call(
        body, grid=(t // tm,),
        in_specs=[pl.BlockSpec((tm, d), lambda i: (i, 0)), pl.BlockSpec((nh, n, d), lambda i: (0, 0, 0))],
        out_specs=pl.BlockSpec((nh, tm, n), lambda i: (0, i, 0)),
        out_shape=jax.ShapeDtypeStruct((nh, t, n), BF16), compiler_params=_params(), name=name)(a, b3)


def mm_tn(a, b3, *, tk=512, tn=512, tt=2048, name):
    t, k = a.shape
    p, _, np_ = b3.shape
    nb = np_ // tn
    tt = min(tt, t)
    assert t % tt == 0 and k % tk == 0 and np_ % tn == 0

    def body(a_ref, b_ref, o_ref):
        prod = _dot_tn(a_ref[...].astype(BF16), b_ref[0].astype(BF16))

        @pl.when(pl.program_id(2) == 0)
        def _():
            o_ref[...] = prod

        @pl.when(pl.program_id(2) > 0)
        def _():
            o_ref[...] += prod

    return pl.pallas_call(
        body, grid=(k // tk, p * nb, t // tt),
        in_specs=[pl.BlockSpec((tt, tk), lambda i, j, s: (s, i)), pl.BlockSpec((1, tt, tn), lambda i, j, s: (j // nb, s, j % nb))],
        out_specs=pl.BlockSpec((tk, tn), lambda i, j, s: (i, j)),
        out_shape=jax.ShapeDtypeStruct((k, p * np_), F32),
        compiler_params=_params(), name=name)(a, b3)


def rms_mm_nn(h, g, b, *, parts=1, tm=1024, tn=512, name):
    t, d = h.shape
    n = b.shape[1]
    np_ = n // parts
    nb = np_ // tn
    tm = min(tm, t)
    assert t % tm == 0 and np_ % tn == 0

    def body(h_ref, g_ref, b_ref, xn_ref, o_ref):
        @pl.when(pl.program_id(1) == 0)
        def _():
            x = h_ref[...]
            r = lax.rsqrt(jnp.mean(x * x, axis=-1, keepdims=True) + EPS)
            xn_ref[...] = (x * r * g_ref[...]).astype(BF16)

        o_ref[...] = _dot(xn_ref[...], b_ref[...]).reshape(o_ref.shape)

    if parts == 1:
        out_spec = pl.BlockSpec((tm, tn), lambda i, j: (i, j))
        out_shape = jax.ShapeDtypeStruct((t, n), F32)
    else:
        out_spec = pl.BlockSpec((1, tm, tn), lambda i, j: (j // nb, i, j % nb))
        out_shape = jax.ShapeDtypeStruct((parts, t, np_), F32)
    row = pl.BlockSpec((tm, d), lambda i, j: (i, 0))
    return pl.pallas_call(
        body, grid=(t // tm, n // tn),
        in_specs=[row, pl.BlockSpec((1, d), lambda i, j: (0, 0)), pl.BlockSpec((d, tn), lambda i, j: (0, j))],
        out_specs=[row, out_spec], out_shape=[jax.ShapeDtypeStruct((t, d), BF16), out_shape],
        compiler_params=_params(), name=name)(h, g, b)


def mm_nt_rms_bwd(a3, b, h, g, dres, *, name, tm=512):
    p, t, kp = a3.shape
    d = b.shape[0]
    tm = min(tm, t)
    assert t % tm == 0 and b.shape[1] == p * kp

    def body(a_ref, b_ref, h_ref, g_ref, dres_ref, dh_ref, dg_ref, acc_ref):
        i = pl.program_id(0)
        part = pl.program_id(1)
        prod = _dot_nt(a_ref[0].astype(BF16), b_ref[...])

        @pl.when(part == 0)
        def _():
            acc_ref[...] = prod

        @pl.when(part > 0)
        def _():
            acc_ref[...] += prod

        @pl.when(part == p - 1)
        def _():
            x = h_ref[...]
            dy = acc_ref[...]
            r = lax.rsqrt(jnp.mean(x * x, axis=-1, keepdims=True) + EPS)
            gy = dy * g_ref[...]
            dot = jnp.mean(gy * x, axis=-1, keepdims=True)
            dh_ref[...] = dres_ref[...] + r * gy - x * (r * r * r * dot)
            _acc_rows(dg_ref, jnp.sum(dy * x * r, axis=0, keepdims=True), i == 0)

    row = pl.BlockSpec((tm, d), lambda i, q: (i, 0))
    vec = pl.BlockSpec((1, d), lambda i, q: (0, 0))
    return pl.pallas_call(
        body, grid=(t // tm, p),
        in_specs=[pl.BlockSpec((1, tm, kp), lambda i, q: (q, i, 0)), pl.BlockSpec((d, kp), lambda i, q: (0, q)),
                  row, vec, row],
        out_specs=[row, vec],
        out_shape=[jax.ShapeDtypeStruct((t, d), F32), jax.ShapeDtypeStruct((1, d), F32)],
        scratch_shapes=[pltpu.VMEM((tm, d), F32)], compiler_params=_params(), name=name)(a3, b, h, g, dres)


def _causal3(x, w):
    return w[0:1] * pltpu.roll(x, 2, 0) + w[1:2] * pltpu.roll(x, 1, 0) + w[2:3] * x


def _causal3_taps(x_ext, w, tm):
    x2 = pltpu.roll(x_ext, 2, 0)
    x1 = pltpu.roll(x_ext, 1, 0)
    y = w[0:1] * x2 + w[1:2] * x1 + w[2:3] * x_ext
    return y, (x2[HALO:HALO + tm], x1[HALO:HALO + tm], x_ext[HALO:HALO + tm])


def _anticausal3(z, w):
    n = z.shape[0]
    return w[2:3] * z + w[1:2] * pltpu.roll(z, n - 1, 0) + w[0:1] * pltpu.roll(z, n - 2, 0)


def _prev_spec(part, tm, tc, nrow8):
    del nrow8
    return pl.BlockSpec((1, HALO, tc), lambda j, i: (part, jnp.maximum(i * (tm // HALO) - 1, 0), j))


def _next_spec(part, tm, tc, nrow8):
    return pl.BlockSpec((1, HALO, tc), lambda j, i: (part, jnp.minimum((i + 1) * (tm // HALO), nrow8 - 1), j))


def _tile_spec(part, tm, tc):
    return pl.BlockSpec((1, tm, tc), lambda j, i: (part, i, j))


def _acc_rows(ref, val, first):
    @pl.when(first)
    def _():
        ref[...] = val

    @pl.when(jnp.logical_not(first))
    def _():
        ref[...] += val


def ffn_act_down_fwd(up2, cw2, w_down, h, *, name, tm=256, tc=1408):
    _, t, f = up2.shape
    d = h.shape[1]
    tm = min(tm, t)

    def body(g_ref, v_ref, gp_ref, vp_ref, w_ref, wd_ref, h_ref, o_ref, act_ref):
        keep = jnp.where(pl.program_id(0) == 0, 0.0, 1.0)
        for cc in range(f // tc):
            cols = slice(cc * tc, (cc + 1) * tc)
            g_ext = jnp.concatenate([gp_ref[0, :, cols] * keep, g_ref[0, :, cols]], axis=0)
            v_ext = jnp.concatenate([vp_ref[0, :, cols] * keep, v_ref[0, :, cols]], axis=0)
            ug = _causal3(g_ext, w_ref[0, :, cols])[HALO:]
            uv = _causal3(v_ext, w_ref[1, :, cols])[HALO:]
            act_ref[:, cols] = (ug * jax.nn.sigmoid(ug) * uv).astype(BF16)
        o_ref[...] = _dot(act_ref[...], wd_ref[...]) + h_ref[...]

    tile = lambda part: pl.BlockSpec((1, tm, f), lambda i: (part, i, 0))
    prev = lambda part: pl.BlockSpec((1, HALO, f), lambda i: (part, jnp.maximum(i * (tm // HALO) - 1, 0), 0))
    row = pl.BlockSpec((tm, d), lambda i: (i, 0))
    return pl.pallas_call(
        body, grid=(t // tm,),
        in_specs=[tile(0), tile(1), prev(0), prev(1), pl.BlockSpec((2, 3, f), lambda i: (0, 0, 0)),
                  pl.BlockSpec((f, d), lambda i: (0, 0)), row],
        out_specs=[row, pl.BlockSpec((tm, f), lambda i: (i, 0))],
        out_shape=[jax.ShapeDtypeStruct((t, d), F32), jax.ShapeDtypeStruct((t, f), BF16)],
        compiler_params=_params(), name=name)(up2, up2, up2, up2, cw2, w_down, h)


def ffn_act_bwd(dh, w_down, up2, cw2, *, name, tm=256, tc=1408):
    _, t, f = up2.shape
    d = dh.shape[1]
    n8 = t // HALO

    def body(d_ref, dn_ref, wd_ref, g_ref, v_ref, gp_ref, vp_ref, gn_ref, vn_ref, wg_ref, wv_ref, dup_ref, dw_ref):
        i = pl.program_id(1)
        first = i == 0
        keep_p = jnp.where(first, 0.0, 1.0)
        keep_n = jnp.where(i == pl.num_programs(1) - 1, 0.0, 1.0)
        wg = wg_ref[0]
        wv = wv_ref[0]
        g_ext = jnp.concatenate([gp_ref[0] * keep_p, g_ref[0], gn_ref[0]], axis=0)
        v_ext = jnp.concatenate([vp_ref[0] * keep_p, v_ref[0], vn_ref[0]], axis=0)
        dh_ext = jnp.concatenate([d_ref[...], dn_ref[...] * keep_n], axis=0)
        d_ext = _dot_nt(dh_ext.astype(BF16), wd_ref[...])
        ug, (g2, g1, g0) = _causal3_taps(g_ext, wg, tm)
        uv, (v2, v1, v0) = _causal3_taps(v_ext, wv, tm)
        ug = ug[HALO:]
        uv = uv[HALO:]
        s = jax.nn.sigmoid(ug)
        dg = d_ext * uv * (s * (1.0 + ug * (1.0 - s)))
        dv = d_ext * (ug * s)
        dup_ref[0] = _anticausal3(dg, wg)[:tm].astype(BF16)
        dup_ref[1] = _anticausal3(dv, wv)[:tm].astype(BF16)
        dgt = dg[:tm]
        dvt = dv[:tm]
        zero = jnp.zeros((HALO - 3, tc), F32)
        rows_g = [jnp.sum(dgt * x, axis=0, keepdims=True) for x in (g2, g1, g0)] + [zero]
        rows_v = [jnp.sum(dvt * x, axis=0, keepdims=True) for x in (v2, v1, v0)] + [zero]
        _acc_rows(dw_ref, jnp.stack([jnp.concatenate(rows_g, axis=0), jnp.concatenate(rows_v, axis=0)]), first)

    wspec = lambda part: pl.BlockSpec((1, 3, tc), lambda j, i: (part, 0, j))
    return pl.pallas_call(
        body, grid=(f // tc, t // tm),
        in_specs=[pl.BlockSpec((tm, d), lambda j, i: (i, 0)),
                  pl.BlockSpec((HALO, d), lambda j, i: (jnp.minimum((i + 1) * (tm // HALO), n8 - 1), 0)),
                  pl.BlockSpec((tc, d), lambda j, i: (j, 0)),
                  _tile_spec(0, tm, tc), _tile_spec(1, tm, tc), _prev_spec(0, tm, tc, n8), _prev_spec(1, tm, tc, n8),
                  _next_spec(0, tm, tc, n8), _next_spec(1, tm, tc, n8), wspec(0), wspec(1)],
        out_specs=[pl.BlockSpec((2, tm, tc), lambda j, i: (0, i, j)), pl.BlockSpec((2, HALO, tc), lambda j, i: (0, 0, j))],
        out_shape=[jax.ShapeDtypeStruct((2, t, f), BF16), jax.ShapeDtypeStruct((2, HALO, f), F32)],
        compiler_params=_params(), name=name)(dh, dh, w_down, up2, up2, up2, up2, up2, up2, cw2, cw2)


def conv_mix_fwd(proj3, ck, *, name, tm=512, tc=512):
    _, t, c = proj3.shape
    n8 = t // HALO

    def body(b_ref, c_ref, u_ref, cp_ref, up_ref, w_ref, o_ref):
        keep = jnp.where(pl.program_id(1) == 0, 0.0, 1.0)
        cu_ext = jnp.concatenate([cp_ref[0] * up_ref[0] * keep, c_ref[0] * u_ref[0]], axis=0)
        o_ref[...] = (b_ref[0] * _causal3(cu_ext, w_ref[0])[HALO:]).astype(BF16)

    return pl.pallas_call(
        body, grid=(c // tc, t // tm),
        in_specs=[_tile_spec(0, tm, tc), _tile_spec(1, tm, tc), _tile_spec(2, tm, tc), _prev_spec(1, tm, tc, n8),
                  _prev_spec(2, tm, tc, n8), pl.BlockSpec((1, 3, tc), lambda j, i: (0, 0, j))],
        out_specs=pl.BlockSpec((tm, tc), lambda j, i: (i, j)),
        out_shape=jax.ShapeDtypeStruct((t, c), BF16), compiler_params=_params(), name=name)(proj3, proj3, proj3, proj3, proj3, ck)


def conv_mix_bwd(dh, w_out, proj3, ck, *, name, tm=512, tc=512):
    _, t, c = proj3.shape
    d = dh.shape[1]
    n8 = t // HALO

    def body(d_ref, dn_ref, wo_ref, b_ref, c_ref, u_ref, cp_ref, up_ref, bn_ref, w_ref, dp_ref, dw_ref):
        i = pl.program_id(1)
        first = i == 0
        keep_p = jnp.where(first, 0.0, 1.0)
        keep_n = jnp.where(i == pl.num_programs(1) - 1, 0.0, 1.0)
        w = w_ref[0]
        cu_ext = jnp.concatenate([cp_ref[0] * up_ref[0] * keep_p, c_ref[0] * u_ref[0]], axis=0)
        cv, (x2, x1, x0) = _causal3_taps(cu_ext, w, tm)
        cv = cv[HALO:]
        dh_ext = jnp.concatenate([d_ref[...], dn_ref[...] * keep_n], axis=0)
        d_ext = _dot_nt(dh_ext.astype(BF16), wo_ref[...])
        dyt = d_ext[:tm]
        b_ext = jnp.concatenate([b_ref[0], bn_ref[0]], axis=0)
        dcv = d_ext * b_ext
        dcu = _anticausal3(dcv, w)[:tm]
        dp_ref[0] = (dyt * cv).astype(BF16)
        dp_ref[1] = (dcu * u_ref[0]).astype(BF16)
        dp_ref[2] = (dcu * c_ref[0]).astype(BF16)
        dcvt = dcv[:tm]
        rows = [jnp.sum(dcvt * x, axis=0, keepdims=True) for x in (x2, x1, x0)] + [jnp.zeros((HALO - 3, tc), F32)]
        _acc_rows(dw_ref, jnp.concatenate(rows, axis=0)[None], first)

    return pl.pallas_call(
        body, grid=(c // tc, t // tm),
        in_specs=[pl.BlockSpec((tm, d), lambda j, i: (i, 0)),
                  pl.BlockSpec((HALO, d), lambda j, i: (jnp.minimum((i + 1) * (tm // HALO), n8 - 1), 0)),
                  pl.BlockSpec((tc, d), lambda j, i: (j, 0)),
                  _tile_spec(0, tm, tc), _tile_spec(1, tm, tc), _tile_spec(2, tm, tc),
                  _prev_spec(1, tm, tc, n8), _prev_spec(2, tm, tc, n8),
                  _next_spec(0, tm, tc, n8), pl.BlockSpec((1, 3, tc), lambda j, i: (0, 0, j))],
        out_specs=[pl.BlockSpec((3, tm, tc), lambda j, i: (0, i, j)), pl.BlockSpec((1, HALO, tc), lambda j, i: (0, 0, j))],
        out_shape=[jax.ShapeDtypeStruct((3, t, c), BF16), jax.ShapeDtypeStruct((1, HALO, c), F32)],
        compiler_params=_params(), name=name)(dh, dh, w_out, proj3, proj3, proj3, proj3, proj3, proj3, ck)


def _head_sums(x, bd):
    hi, lo = _split2(x)
    return _dot(hi, bd) + _dot(lo, bd)


def attn_prep_fwd(proj, gq, gk, fbias, bd, *, name, tm=256):
    t = proj.shape[0]

    def body(q_ref, k_ref, v_ref, f_ref, gq_ref, gk_ref, fb_ref, bd_ref, qa_ref, ka_ref, va_ref, lf_ref):
        bd = bd_ref[...]
        lane = lax.broadcasted_iota(jnp.int32, (tm, 128), 1)
        low = lane < HEAD_DIM

        def two_heads(y, o_ref, c, rest):
            o_ref[2 * c] = jnp.where(low, y, rest).astype(BF16)
            o_ref[2 * c + 1] = jnp.where(low, pltpu.roll(y, HEAD_DIM, 1), rest).astype(BF16)

        def headnorm(x_ref, g_ref, o_ref, scale):
            for c in range(MIX // 128):
                sl = slice(128 * c, 128 * (c + 1))
                x = x_ref[:, sl]
                r = lax.rsqrt(_head_sums(x * x, bd) * (1.0 / HEAD_DIM) + EPS)
                two_heads(x * r * (g_ref[:, sl] * scale), o_ref, c, 0.0)

        headnorm(q_ref, gq_ref, qa_ref, SCALE)
        headnorm(k_ref, gk_ref, ka_ref, 1.0)
        one_at_64 = jnp.where(lane == HEAD_DIM, 1.0, 0.0)
        for c in range(MIX // 128):
            two_heads(v_ref[:, 128 * c:128 * (c + 1)], va_ref, c, one_at_64)
        fl = f_ref[...] + fb_ref[...]
        logf = jnp.minimum(fl, 0.0) - jnp.log(1.0 + jnp.exp(-jnp.abs(fl)))
        lf_ref[...] = logf.T[0:H_FOX, :]

    col = lambda c: pl.BlockSpec((tm, MIX), lambda i: (i, c))
    vec = pl.BlockSpec((1, MIX), lambda i: (0, 0))
    out = pl.BlockSpec((N_HEADS, tm, 2 * HEAD_DIM), lambda i: (0, i, 0))
    return pl.pallas_call(
        body, grid=(t // tm,),
        in_specs=[col(0), col(1), col(2), pl.BlockSpec((tm, 128), lambda i: (i, 3 * MIX // 128)), vec, vec,
                  pl.BlockSpec((1, 128), lambda i: (0, 0)), pl.BlockSpec((128, 128), lambda i: (0, 0))],
        out_specs=[out, out, out, pl.BlockSpec((H_FOX, tm), lambda i: (0, i))],
        out_shape=[jax.ShapeDtypeStruct((N_HEADS, t, 2 * HEAD_DIM), BF16)] * 3 + [jax.ShapeDtypeStruct((H_FOX, t), F32)],
        compiler_params=_params(), name=name)(proj, proj, proj, proj, gq, gk, fbias, bd)


def attn_prep_bwd(proj, dq_f, dq_s, dkv_f, dkv_s, dfl, gq, gk, bd, *, name, tm=256):
    t = proj.shape[0]
    per_group = H_FOX // 2

    def body(q_ref, k_ref, dqf_ref, dqs_ref, dkvf_ref, dkvs_ref, dfl_ref, gq_ref, gk_ref, bd_ref, dp_ref, dgq_ref,
             dgk_ref):
        bd = bd_ref[...]
        first = pl.program_id(0) == 0
        low = lax.broadcasted_iota(jnp.int32, (tm, 128), 1) < HEAD_DIM

        def two_heads(fox_ref, sb_ref, c):
            ref = fox_ref if c < per_group else sb_ref
            return ref[2 * (c % per_group)], ref[2 * (c % per_group) + 1]

        def low_halves(ab):
            return jnp.where(low, ab[0], pltpu.roll(ab[1], HEAD_DIM, 1))

        def high_halves(ab):
            return jnp.where(low, pltpu.roll(ab[0], HEAD_DIM, 1), ab[1])

        def back(x_ref, grad, g_ref, col0, scale, dg_ref):
            parts = []
            for c in range(MIX // 128):
                sl = slice(128 * c, 128 * (c + 1))
                x = x_ref[:, sl]
                r = lax.rsqrt(_head_sums(x * x, bd) * (1.0 / HEAD_DIM) + EPS)
                dn = grad(c) * scale
                gy = dn * g_ref[:, sl]
                hs = _head_sums(gy * x, bd) * (1.0 / HEAD_DIM)
                dp_ref[:, col0 + 128 * c:col0 + 128 * (c + 1)] = (r * gy - x * (r * r * r * hs)).astype(BF16)
                parts.append(jnp.sum(dn * x * r, axis=0, keepdims=True))
            _acc_rows(dg_ref, jnp.concatenate(parts, axis=1), first)

        back(q_ref, lambda c: low_halves(two_heads(dqf_ref, dqs_ref, c)), gq_ref, 0, SCALE, dgq_ref)
        back(k_ref, lambda c: low_halves(two_heads(dkvf_ref, dkvs_ref, c)), gk_ref, MIX, 1.0, dgk_ref)
        for c in range(MIX // 128):
            dp_ref[:, 2 * MIX + 128 * c:2 * MIX + 128 * (c + 1)] = high_halves(two_heads(dkvf_ref, dkvs_ref, c)).astype(BF16)
        dp_ref[:, 3 * MIX:] = dfl_ref[...]

    col = lambda c: pl.BlockSpec((tm, MIX), lambda i: (i, c))
    heads = pl.BlockSpec((H_FOX, tm, 128), lambda i: (0, i, 0))
    vec = pl.BlockSpec((1, MIX), lambda i: (0, 0))
    return pl.pallas_call(
        body, grid=(t // tm,),
        in_specs=[col(0), col(1), heads, heads, heads, heads, pl.BlockSpec((tm, 128), lambda i: (i, 0)), vec, vec,
                  pl.BlockSpec((128, 128), lambda i: (0, 0))],
        out_specs=[pl.BlockSpec((tm, ATTN_IN_PAD), lambda i: (i, 0)), vec, vec],
        out_shape=[jax.ShapeDtypeStruct((t, ATTN_IN_PAD), BF16), jax.ShapeDtypeStruct((1, MIX), F32),
                   jax.ShapeDtypeStruct((1, MIX), F32)],
        compiler_params=_params(), name=name)(proj, proj, dq_f, dq_s, dkv_f, dkv_s, dfl, gq, gk, bd)


def gate_cumsum(logf3, tri, *, name):
    nc, r, _ = logf3.shape

    def body(x_ref, tri_ref, o_ref):
        tri_m = tri_ref[...]

        def step(c, carry):
            hi, mid, lo = _split3(x_ref[c])
            cs = _dot(hi, tri_m) + _dot(mid, tri_m) + _dot(lo, tri_m) + carry
            o_ref[c] = cs
            return cs[:, 127:128]

        lax.fori_loop(0, nc, step, jnp.zeros((r, 1), F32))

    return pl.pallas_call(body, out_shape=jax.ShapeDtypeStruct(logf3.shape, F32), compiler_params=_params(),
                          name=name)(logf3, tri)


def gate_cumsum_bwd(dcum3, logf3, tri, *, name):
    nc, r, _ = dcum3.shape

    def body(x_ref, lf_ref, tri_ref, o_ref, s_ref):
        tri_m = tri_ref[...]

        def step(n, carry):
            car, tot = carry
            c = nc - 1 - n
            hi, mid, lo = _split3(x_ref[c])
            cs = _dot(hi, tri_m) + _dot(mid, tri_m) + _dot(lo, tri_m) + car
            dl = cs * (1.0 - jnp.exp(lf_ref[c]))
            o_ref[c] = dl
            return cs[:, 0:1], tot + dl

        _, tot = lax.fori_loop(0, nc, step, (jnp.zeros((r, 1), F32), jnp.zeros((r, 128), F32)))
        s_ref[...] = jnp.broadcast_to(jnp.sum(tot, axis=1, keepdims=True), tot.shape)

    return pl.pallas_call(body, out_shape=[jax.ShapeDtypeStruct(dcum3.shape, F32), jax.ShapeDtypeStruct((r, 128), F32)],
                          compiler_params=_params(), name=name)(dcum3, logf3, tri)


def _causal_iota():
    row = lax.broadcasted_iota(jnp.int32, (BQ, BQ), 0)
    col = lax.broadcasted_iota(jnp.int32, (BQ, BQ), 1)
    return row, col


def _head_specs(nj, head0):
    qin = pl.BlockSpec((1, BQ, HEAD_DIM), lambda h, i: (h + head0, i, 0))
    kin = pl.BlockSpec((1, nj, BQ, HEAD_DIM), lambda h, i: (h + head0, 0, 0, 0))
    qin2 = pl.BlockSpec((1, BQ, 2 * HEAD_DIM), lambda h, i: (h + head0, i, 0))
    kin2 = pl.BlockSpec((1, nj, BQ, 2 * HEAD_DIM), lambda h, i: (h + head0, 0, 0, 0))
    qspec = pl.BlockSpec((1, BQ, HEAD_DIM), lambda h, i: (h, i, 0))
    kspec2 = pl.BlockSpec((1, nj, BQ, 2 * HEAD_DIM), lambda h, i: (h, 0, 0, 0))
    return qin, kin, qin2, kin2, qspec, kspec2


STOP = -105.0
STOP_WIDE = -115.0
FIXED_REF_MAX = 40.0


def _store_kmax(k_ref, kmax_ref, nj):
    def step(j, mx):
        kf = k_ref[0, j].astype(F32)
        return jnp.maximum(mx, jnp.max(jnp.sum(kf * kf, axis=1, keepdims=True), axis=0, keepdims=True))

    mx = lax.fori_loop(0, nj, step, jnp.zeros((1, 1), F32))
    kmax_ref[...] = jnp.broadcast_to(jnp.sqrt(mx), kmax_ref.shape)


def _qk_bound(q, kmax_ref):
    qf = q.astype(F32)
    return jnp.sqrt(jnp.sum(qf * qf, axis=1, keepdims=True)) * kmax_ref[0:1, 0:1] * 1.001


def _first_and_last_step():
    h, i = pl.program_id(0), pl.program_id(1)
    first = jnp.logical_and(h == 0, i == 0)
    last = jnp.logical_and(h == pl.num_programs(0) - 1, i == pl.num_programs(1) - 1)
    return first, last


def fox_fwd(qa, ka4, va4, fcol, frow4, *, name, gather=None):
    t = qa.shape[1]
    dh = HEAD_DIM
    nh = H_FOX
    nj = t // BQ

    def body(*refs):
        if gather is None:
            q_ref, k_ref, v_ref, fc_ref, fr_ref, o_ref, lse_ref, kmax_ref = refs
        else:
            q_ref, k_ref, v_ref, fc_ref, fr_ref, src_ref, o_ref, lse_ref, dst_ref, kmax_ref = refs[:10]
            first_step, last_step = _first_and_last_step()

            @pl.when(first_step)
            def _():
                _chip_gather(src_ref, dst_ref, *refs[10:])[0]()

        i = pl.program_id(1)

        @pl.when(i == 0)
        def _():
            _store_kmax(k_ref, kmax_ref, nj)

        q = q_ref[0]
        fq = fc_ref[0]
        bound = _qk_bound(q, kmax_ref)
        row, col = _causal_iota()

        def gate_at_block_end(j):
            return fr_ref[0, j][:, BQ - 1:BQ]

        def pv(p, j):
            p_hi, p_lo = _split2(p)
            return _dot(p_hi, v_ref[0, j]) + _dot(p_lo, v_ref[0, j])

        def walk(block, live, init):
            carry = block(i, init, True)

            def cond(c):
                n, carry = c
                return jnp.logical_and(n < i, live(jnp.maximum(i - 1 - n, 0), carry))

            _, carry = lax.while_loop(cond, lambda c: (c[0] + 1, block(i - 1 - c[0], c[1], False)), (0, carry))
            return carry

        def fixed_reference(_):
            shift = fq - bound

            def probs(j, offset):
                return jnp.exp(_dot_nt(q, k_ref[0, j]) + (shift + offset) - fr_ref[0, j])

            def live(c):
                n, acc = c
                gate = gate_at_block_end(jnp.maximum(i - 1 - n, 0))
                return jnp.logical_and(n < i, jnp.max(fq - gate - jnp.log(acc[:, dh:dh + 1])) >= STOP_WIDE)

            def two_blocks(c):
                n, acc = c
                ja = i - 1 - n
                jb = i - 2 - n
                absent = jnp.where(jb >= 0, 0.0, NEG)
                jb = jnp.maximum(jb, 0)
                return n + 2, acc + (pv(probs(ja, 0.0), ja) + pv(probs(jb, absent), jb))

            acc = pv(jnp.where(col <= row, probs(i, 0.0), 0.0), i)
            _, acc = lax.while_loop(live, two_blocks, (0, acc))
            l = acc[:, dh:dh + 1]
            return acc / l, bound + jnp.log(l)

        def running_maximum(_):
            def block(j, carry, diag):
                m, acc = carry
                s = _dot_nt(q, k_ref[0, j]) + fq - fr_ref[0, j]
                if diag:
                    s = jnp.where(col <= row, s, NEG)
                m_new = jnp.maximum(m, jnp.max(s, axis=1, keepdims=True))
                return m_new, jnp.exp(m - m_new) * acc + pv(jnp.exp(s - m_new), j)

            def live(j, carry):
                return jnp.max(bound + fq - gate_at_block_end(j) - carry[0]) >= STOP

            m, acc = walk(block, live, (jnp.full((BQ, 1), NEG, F32), jnp.zeros((BQ, 2 * dh), F32)))
            l = acc[:, dh:dh + 1]
            return acc / l, m + jnp.log(l)

        o, lse = lax.cond(jnp.max(bound) < FIXED_REF_MAX, fixed_reference, running_maximum, 0)
        o_ref[0] = o
        lse_ref[0] = lse

        if gather is not None:
            @pl.when(last_step)
            def _():
                _chip_gather(src_ref, dst_ref, *refs[10:])[1]()

    _, _, qin2, kin2, _, _ = _head_specs(nj, 0)
    cspec = pl.BlockSpec((1, BQ, 1), lambda h, i: (h, i, 0))
    in_specs = [qin2, kin2, kin2, cspec, pl.BlockSpec((1, nj, 1, BQ), lambda h, i: (h, 0, 0, 0))]
    out_specs = [pl.BlockSpec((1, BQ, 2 * dh), lambda h, i: (h, i, 0)), cspec]
    out_shape = [jax.ShapeDtypeStruct((nh, t, 2 * dh), F32), jax.ShapeDtypeStruct((nh, t, 1), F32)]
    scratch = [pltpu.VMEM((8, 128), F32)]
    args = [qa, ka4, va4, fcol, frow4]
    if gather is not None:
        in_specs.append(_ANY)
        out_specs.append(_ANY)
        out_shape.append(jax.ShapeDtypeStruct((N_CHIPS,) + gather.shape, gather.dtype))
        scratch += _chip_sems()
        args.append(gather)
    return pl.pallas_call(body, grid=(nh, nj), in_specs=in_specs, out_specs=out_specs, out_shape=out_shape,
                          scratch_shapes=scratch, compiler_params=_params(), name=name)(*args)


def _other_half(x):
    return pltpu.roll(x.astype(F32), HEAD_DIM, 1).astype(BF16)


def fox_bwd(qa, ka4, va4, dox, fcol, frow4, o, lse, *, name, scatter=None):
    t = qa.shape[1]
    dh = HEAD_DIM
    nh = H_FOX
    nj = t // BQ
    n_in = 8

    def body(*refs):
        q_ref, k_ref, v_ref, dox_ref, fc_ref, fr_ref, o_ref, lse_ref = refs[:n_in]
        if scatter is None:
            dq_ref, dkv_ref, dfk_ref, kmax_ref = refs[n_in:]
        else:
            g_ref, dq_ref, dkv_ref, dfk_ref, land_ref, kmax_ref = refs[n_in:n_in + 6]
            first_step, last_step = _first_and_last_step()

            @pl.when(first_step)
            def _():
                _chip_scatter(g_ref, land_ref, *refs[n_in + 6:])[0]()

        i = pl.program_id(1)

        @pl.when(i == 0)
        def _():
            dkv_ref[...] = jnp.zeros_like(dkv_ref)
            dfk_ref[...] = jnp.zeros_like(dfk_ref)
            _store_kmax(k_ref, kmax_ref, nj)

        q = q_ref[0]
        fq = fc_ref[0]
        lse_q = lse_ref[0]
        do_x = dox_ref[0]
        dd = jnp.sum(do_x.astype(F32) * o_ref[0], axis=1, keepdims=True)
        rhs = jnp.concatenate([q, _other_half(do_x)], axis=0)
        edge = _qk_bound(q, kmax_ref) + fq - lse_q

        def negligible(j):
            return jnp.logical_and(j < i, jnp.max(edge - fr_ref[0, j][:, BQ - 1:BQ]) < STOP_WIDE)

        first = lax.while_loop(negligible, lambda j: j + 1, 0)

        shift = fq - lse_q

        def block(j, offset, diag):
            k = k_ref[0, j]
            p = jnp.exp(_dot_nt(q, k) + (shift + offset) - fr_ref[0, j])
            if diag:
                row, col = _causal_iota()
                p = jnp.where(col <= row, p, 0.0)
            ds = p * (_dot_nt(do_x, v_ref[0, j]) - dd)
            ds_b = ds.astype(BF16)
            dkv_ref[0, j] += _dot_tn(jnp.concatenate([ds_b, p.astype(BF16)], axis=0), rhs)
            dfk_ref[0, j] -= jnp.sum(ds, axis=0, keepdims=True)
            return _dot(ds_b, k)

        def two_blocks(n, dq):
            ja = first + 2 * n
            jb = ja + 1
            absent = jnp.where(jb < i, 0.0, NEG)
            jb = jnp.minimum(jb, i - 1)
            return dq + (block(ja, 0.0, False) + block(jb, absent, False))

        dq = lax.fori_loop(0, (i - first + 1) // 2, two_blocks, jnp.zeros((BQ, 2 * dh), F32))
        dq_ref[0] = dq + block(i, 0.0, True)

        if scatter is not None:
            @pl.when(last_step)
            def _():
                _chip_scatter(g_ref, land_ref, *refs[n_in + 6:])[1]()

    _, _, qin2, kin2, _, kspec2 = _head_specs(nj, 0)
    cspec = pl.BlockSpec((1, BQ, 1), lambda h, i: (h, i, 0))
    rspec = pl.BlockSpec((1, nj, 1, BQ), lambda h, i: (h, 0, 0, 0))
    wide = pl.BlockSpec((1, BQ, 2 * dh), lambda h, i: (h, i, 0))
    in_specs = [qin2, kin2, kin2, qin2, cspec, rspec, wide, cspec]
    out_specs = [wide, kspec2, rspec]
    out_shape = [jax.ShapeDtypeStruct((nh, t, 2 * dh), F32), jax.ShapeDtypeStruct((nh, nj, BQ, 2 * dh), F32),
                 jax.ShapeDtypeStruct((nh, nj, 1, BQ), F32)]
    scratch = [pltpu.VMEM((8, 128), F32)]
    args = [qa, ka4, va4, dox, fcol, frow4, o, lse]
    if scatter is not None:
        in_specs.append(_ANY)
        out_specs.append(_ANY)
        out_shape.append(jax.ShapeDtypeStruct(scatter.shape, scatter.dtype))
        scratch += _chip_sems()
        args.append(scatter)
    return pl.pallas_call(body, grid=(nh, nj), in_specs=in_specs, out_specs=out_specs, out_shape=out_shape,
                          scratch_shapes=scratch, compiler_params=_params(), name=name)(*args)


def _sb_logs(z, diag):
    e = jnp.exp(-jnp.abs(z))
    sp = jnp.log(1.0 + e)
    logb = jnp.minimum(z, 0.0) - sp
    lom = -jnp.maximum(z, 0.0) - sp
    strict = None
    if diag:
        row, col = _causal_iota()
        strict = col < row
        lom = jnp.where(strict, lom, 0.0)
    return logb, lom, e, strict


SB_GROUP = BQ // 2


def _sums_over_later_keys(lom, tri_m):
    halves = [lom[:, :SB_GROUP], lom[:, SB_GROUP:]]
    totals = [jnp.sum(x, axis=1, keepdims=True) for x in halves]
    within = []
    for x in halves:
        hi, lo = _split2(x)
        within.append(_dot(hi, tri_m) + _dot(lo, tri_m))
    return jnp.concatenate([within[0] + totals[1], within[1]], axis=1), totals[0] + totals[1]


def _sums_over_earlier_keys(da, tri_m):
    halves = [da[:, :SB_GROUP], da[:, SB_GROUP:]]
    totals = [jnp.sum(x, axis=1, keepdims=True) for x in halves]
    within = [_dot_nt(x.astype(BF16), tri_m) for x in halves]
    return jnp.concatenate([within[0], within[1] + totals[0]], axis=1), totals[0] + totals[1]


def sb_fwd(qa, ka4, va4, tri, *, name, gather=None):
    t = qa.shape[1]
    dh = HEAD_DIM
    nh = H_SB
    nj = t // BQ
    assert nj <= 128

    def body(*refs):
        if gather is None:
            q_ref, k_ref, v_ref, tri_ref, o_ref, rs_ref = refs
        else:
            q_ref, k_ref, v_ref, tri_ref, src_ref, o_ref, rs_ref, dst_ref = refs[:8]
            first_step, last_step = _first_and_last_step()

            @pl.when(first_step)
            def _():
                _chip_gather(src_ref, dst_ref, *refs[8:])[0]()

        i = pl.program_id(1)
        q = q_ref[0]
        tri_m = tri_ref[...]
        lane = lax.broadcasted_iota(jnp.int32, (BQ, 128), 1)

        def block(j, carry, diag):
            run, acc, rall = carry
            logb, lom, _, strict = _sb_logs(_dot_nt(q, k_ref[0, j]), diag)
            later, total = _sums_over_later_keys(lom, tri_m)
            w = jnp.exp(logb + later + run)
            if diag:
                w = jnp.where(strict, w, 0.0)
            acc = acc + _dot(w.astype(BF16), v_ref[0, j])
            rall = jnp.where(lane == j, run, rall)
            return run + total, acc, rall

        init = (jnp.zeros((BQ, 1), F32), jnp.zeros((BQ, 2 * dh), F32), jnp.full((BQ, 128), NEG, F32))
        carry = block(i, init, True)

        def cond(c):
            n, carry = c
            return jnp.logical_and(n < i, jnp.max(carry[0]) >= STOP)

        _, (_, acc, rall) = lax.while_loop(cond, lambda c: (c[0] + 1, block(i - 1 - c[0], c[1], False)), (0, carry))
        o_ref[0] = acc[:, :dh].astype(BF16)
        rs_ref[0] = rall

        if gather is not None:
            @pl.when(last_step)
            def _():
                _chip_gather(src_ref, dst_ref, *refs[8:])[1]()

    _, _, qin2, kin2, qspec, _ = _head_specs(nj, H_FOX)
    rspec = pl.BlockSpec((1, BQ, 128), lambda h, i: (h, i, 0))
    in_specs = [qin2, kin2, kin2, pl.BlockSpec((SB_GROUP, SB_GROUP), lambda h, i: (0, 0))]
    out_specs = [qspec, rspec]
    out_shape = [jax.ShapeDtypeStruct((nh, t, dh), BF16), jax.ShapeDtypeStruct((nh, t, 128), F32)]
    scratch = []
    args = [qa, ka4, va4, tri]
    if gather is not None:
        in_specs.append(_ANY)
        out_specs.append(_ANY)
        out_shape.append(jax.ShapeDtypeStruct((N_CHIPS,) + gather.shape, gather.dtype))
        scratch += _chip_sems()
        args.append(gather)
    return pl.pallas_call(body, grid=(nh, nj), in_specs=in_specs, out_specs=out_specs, out_shape=out_shape,
                          scratch_shapes=scratch, compiler_params=_params(), name=name)(*args)


def sb_bwd(qa, ka4, va4, dox, tri, rsave, *, name, scatter=None):
    t = qa.shape[1]
    dh = HEAD_DIM
    nh = H_SB
    nj = t // BQ
    n_in = 6

    def body(*refs):
        q_ref, k_ref, v_ref, dox_ref, tri_ref, rs_ref = refs[:n_in]
        if scatter is None:
            dq_ref, dkv_ref = refs[n_in:]
        else:
            g_ref, dq_ref, dkv_ref, land_ref = refs[n_in:n_in + 4]
            first_step, last_step = _first_and_last_step()

            @pl.when(first_step)
            def _():
                _chip_scatter(g_ref, land_ref, *refs[n_in + 4:])[0]()

        i = pl.program_id(1)

        @pl.when(i == 0)
        def _():
            dkv_ref[...] = jnp.zeros_like(dkv_ref)

        q = q_ref[0]
        do_x = dox_ref[0]
        tri_m = tri_ref[...]
        rall = rs_ref[0]
        lane = lax.broadcasted_iota(jnp.int32, (BQ, 128), 1)
        rhs = jnp.concatenate([q, _other_half(do_x)], axis=0)
        lane1 = lax.broadcasted_iota(jnp.int32, (1, 128), 1)
        unvisited = jnp.logical_and(lane1 < i, jnp.max(rall, axis=0, keepdims=True) < STOP)
        first = jnp.sum(unvisited.astype(jnp.int32))

        def block(j, carry, diag):
            dq, ecar = carry
            k = k_ref[0, j]
            z = _dot_nt(q, k)
            logb, lom, e, strict = _sb_logs(z, diag)
            run = jnp.sum(jnp.where(lane == j, rall, 0.0), axis=1, keepdims=True)
            w = jnp.exp(logb + _sums_over_later_keys(lom, tri_m)[0] + run)
            if diag:
                w = jnp.where(strict, w, 0.0)
            da = w * _dot_nt(do_x, v_ref[0, j])
            earlier, da_total = _sums_over_earlier_keys(da, tri_m)
            before = earlier + ecar
            inv = 1.0 / (1.0 + e)
            beta = jnp.where(z >= 0.0, 1.0, e) * inv
            one_minus = jnp.where(z >= 0.0, e, 1.0) * inv
            dz = da * one_minus - before * beta
            if diag:
                dz = jnp.where(strict, dz, 0.0)
            dz_b = dz.astype(BF16)
            dkv_ref[0, j] += _dot_tn(jnp.concatenate([dz_b, w.astype(BF16)], axis=0), rhs)
            return dq + _dot(dz_b, k), ecar + da_total

        carry = lax.fori_loop(first, i, lambda j, c: block(j, c, False),
                              (jnp.zeros((BQ, 2 * dh), F32), jnp.zeros((BQ, 1), F32)))
        dq, _ = block(i, carry, True)
        dq_ref[0] = dq

        if scatter is not None:
            @pl.when(last_step)
            def _():
                _chip_scatter(g_ref, land_ref, *refs[n_in + 4:])[1]()

    _, _, qin2, kin2, _, kspec2 = _head_specs(nj, H_FOX)
    in_specs = [qin2, kin2, kin2, qin2, pl.BlockSpec((SB_GROUP, SB_GROUP), lambda h, i: (0, 0)),
                pl.BlockSpec((1, BQ, 128), lambda h, i: (h, i, 0))]
    out_specs = [pl.BlockSpec((1, BQ, 2 * dh), lambda h, i: (h, i, 0)), kspec2]
    out_shape = [jax.ShapeDtypeStruct((nh, t, 2 * dh), F32), jax.ShapeDtypeStruct((nh, nj, BQ, 2 * dh), F32)]
    scratch = []
    args = [qa, ka4, va4, dox, tri, rsave]
    if scatter is not None:
        in_specs.append(_ANY)
        out_specs.append(_ANY)
        out_shape.append(jax.ShapeDtypeStruct(scatter.shape, scatter.dtype))
        scratch += _chip_sems()
        args.append(scatter)
    return pl.pallas_call(body, grid=(nh, nj), in_specs=in_specs, out_specs=out_specs, out_shape=out_shape,
                          scratch_shapes=scratch, compiler_params=_params(), name=name)(*args)


def loss_head(y, target, *, name, tm=512):
    t, d = y.shape

    def body(y_ref, t_ref, l_ref, dy_ref, acc_ref):
        i = pl.program_id(0)
        diff = y_ref[...] - t_ref[...]
        dy_ref[...] = diff * (1.0 / d)
        part = jnp.sum(diff * diff, axis=0, keepdims=True)

        @pl.when(i == 0)
        def _():
            acc_ref[...] = part

        @pl.when(i > 0)
        def _():
            acc_ref[...] += part

        @pl.when(i == pl.num_programs(0) - 1)
        def _():
            l_ref[...] = jnp.full(l_ref.shape, (0.5 / d) * jnp.sum(acc_ref[...]), F32)

    row = pl.BlockSpec((tm, d), lambda i: (i, 0))
    return pl.pallas_call(
        body, grid=(t // tm,), in_specs=[row, row],
        out_specs=[pl.BlockSpec((8, 128), lambda i: (0, 0)), row],
        out_shape=[jax.ShapeDtypeStruct((8, 128), F32), jax.ShapeDtypeStruct((t, d), F32)],
        scratch_shapes=[pltpu.VMEM((1, d), F32)], compiler_params=_params(), name=name)(y, target)


def _from_heads(a):
    t = a.shape[1]
    return a.transpose(1, 0, 2).reshape(t, MIX)


def _lanes_to_chunks(a):
    r, t = a.shape
    return a.reshape(r, t // 128, 128).transpose(1, 0, 2)


def _chunks_to_lanes(a):
    nc, r, _ = a.shape
    return a.transpose(1, 0, 2).reshape(r, nc * 128)


def _constants():
    idx = jnp.arange(128)
    bd = (idx[:, None] // HEAD_DIM == idx[None, :] // HEAD_DIM).astype(BF16)
    tri_le = (idx[:, None] <= idx[None, :]).astype(BF16)
    tri_ge = (idx[:, None] >= idx[None, :]).astype(BF16)
    jdx = jnp.arange(SB_GROUP)
    tri_gt = (jdx[:, None] > jdx[None, :]).astype(BF16)
    return dict(bd=bd, tri_le=tri_le, tri_ge=tri_ge, tri_gt=tri_gt)


def attn_layer_fwd(h, w, cst, gather=None):
    t = h.shape[0]
    nj = t // BQ
    xn, proj = rms_mm_nn(h, w["norm"], w["w_in"], tn=640, name="attn_in_proj")
    qa, ka, va, logf = attn_prep_fwd(proj, w["gq"], w["gk"], w["fbias"], cst["bd"], name="attn_prep_fwd")
    logf3 = _lanes_to_chunks(logf)
    cum = _chunks_to_lanes(gate_cumsum(logf3, cst["tri_le"], name="gate_cumsum"))
    fcol = cum.reshape(H_FOX, t, 1)
    frow4 = cum.reshape(H_FOX, nj, 1, BQ)
    ka4 = ka.reshape(N_HEADS, nj, BQ, 2 * HEAD_DIM)
    va4 = va.reshape(N_HEADS, nj, BQ, 2 * HEAD_DIM)
    if gather is None:
        (o_f, lse), (o_s, rsave), gathered = (fox_fwd(qa, ka4, va4, fcol, frow4, name="fox_fwd"),
                                              sb_fwd(qa, ka4, va4, cst["tri_gt"], name="sb_fwd"), None)
    else:
        o_f, lse, gathered_a = fox_fwd(qa, ka4, va4, fcol, frow4, name="fox_fwd_gather", gather=gather[0])
        o_s, rsave, gathered_b = sb_fwd(qa, ka4, va4, cst["tri_gt"], name="sb_fwd_gather", gather=gather[1])
        gathered = gather[2]((gathered_a, gathered_b))
        w = gathered[0][0]
    o = _from_heads(jnp.concatenate([o_f[:, :, :HEAD_DIM].astype(BF16), o_s], axis=0))
    h2 = mm_nn(o, w["w_out"], add=h, name="mix_out_proj")
    saved = dict(h=h, xn=xn, proj=proj, logf3=logf3, fcol=fcol, frow4=frow4, qa=qa, ka4=ka4, va4=va4,
                 o_f=o_f, lse=lse, rsave=rsave, o=o)
    return h2, saved, gathered


def attn_layer_bwd(dh, w, s, cst, scatter=None):
    t = dh.shape[0]
    dh3 = dh[None]
    w_out_heads = jnp.pad(w["w_out"].reshape(N_HEADS, HEAD_DIM, -1), ((0, 0), (0, HEAD_DIM), (0, 0)))
    dox = mm_nt_heads(dh, w_out_heads, name="mix_out_bwd_heads")
    g_w_out = mm_tn(s["o"], dh3, name="mix_out_wgrad")
    fox_args = (s["qa"], s["ka4"], s["va4"], dox, s["fcol"], s["frow4"], s["o_f"], s["lse"])
    sb_args = (s["qa"], s["ka4"], s["va4"], dox, cst["tri_gt"], s["rsave"])
    if scatter is None:
        (dq_f, dkv_f, dfk), (dq_s, dkv_s), landed = fox_bwd(*fox_args, name="fox_bwd"), sb_bwd(*sb_args, name="sb_bwd"), None
    else:
        chunks_a, chunks_b = scatter(g_w_out)
        dq_f, dkv_f, dfk, landed_a = fox_bwd(*fox_args, name="fox_bwd_scatter", scatter=chunks_a)
        dq_s, dkv_s, landed_b = sb_bwd(*sb_args, name="sb_bwd_scatter", scatter=chunks_b)
        landed = (landed_a, landed_b)
    dcum3 = _lanes_to_chunks(dfk.reshape(H_FOX, t))
    dfl3, dbias = gate_cumsum_bwd(dcum3, s["logf3"], cst["tri_ge"], name="gate_cumsum_bwd")
    dfl = jnp.pad(_chunks_to_lanes(dfl3).T, ((0, 0), (0, 128 - H_FOX))).astype(BF16)
    wide = (H_FOX, t, 2 * HEAD_DIM)
    dproj, dgq, dgk = attn_prep_bwd(s["proj"], dq_f, dq_s, dkv_f.reshape(wide), dkv_s.reshape(wide), dfl, w["gq"],
                                    w["gk"], cst["bd"], name="attn_prep_bwd")
    g_w_in = mm_tn(s["xn"], dproj[None], tn=640, name="attn_in_wgrad")[:, :ATTN_IN]
    dh2, g_norm = mm_nt_rms_bwd(dproj[None], w["w_in"], s["h"], w["norm"], dh, name="attn_in_bwd")
    dgq = dgq.reshape(N_HEADS, HEAD_DIM)
    dgk = dgk.reshape(N_HEADS, HEAD_DIM)
    grads = dict(norm=g_norm[0], w_in=g_w_in, f_bias=dbias[:, 0], fox_q=dgq[:H_FOX].sum(0), fox_k=dgk[:H_FOX].sum(0),
                 sb_q=dgq[H_FOX:].sum(0), sb_k=dgk[H_FOX:].sum(0), w_out=g_w_out)
    return dh2, grads, landed


def conv_layer_fwd(h, w):
    xn, proj3 = rms_mm_nn(h, w["norm"], w["w_in"], parts=3, name="conv_in_proj")
    y = conv_mix_fwd(proj3, w["ck"], name="conv_mix_fwd")
    h2 = mm_nn(y, w["w_out"], add=h, name="mix_out_proj")
    return h2, dict(h=h, xn=xn, proj3=proj3, y=y)


def conv_layer_bwd(dh, w, s):
    dh3 = dh[None]
    g_w_out = mm_tn(s["y"], dh3, name="mix_out_wgrad")
    dproj3, dck = conv_mix_bwd(dh, w["w_out"], s["proj3"], w["ck"], name="conv_mix_bwd")
    g_w_in = mm_tn(s["xn"], dproj3, name="conv_in_wgrad")
    dh2, g_norm = mm_nt_rms_bwd(dproj3, w["w_in"], s["h"], w["norm"], dh, name="conv_in_bwd")
    return dh2, dict(norm=g_norm[0], w_in=g_w_in, ck=dck[0, :3], w_out=g_w_out)


def ffn_layer_fwd(h, w):
    xn, up2 = rms_mm_nn(h, w["norm"], w["w_up"], parts=2, tn=1408, name="ffn_up_proj")
    h2, act = ffn_act_down_fwd(up2, w["cw2"], w["w_down"], h, name="ffn_act_down_fwd")
    return h2, dict(h=h, xn=xn, up2=up2, act=act)


def ffn_layer_bwd(dh, w, s):
    dh3 = dh[None]
    g_w_down = mm_tn(s["act"], dh3, tk=1408, name="ffn_down_wgrad")
    dup2, dcw = ffn_act_bwd(dh, w["w_down"], s["up2"], w["cw2"], name="ffn_act_bwd")
    g_w_up = mm_tn(s["xn"], dup2, tn=1408, name="ffn_up_wgrad")
    dh2, g_norm = mm_nt_rms_bwd(dup2, w["w_up"], s["h"], w["norm"], dh, name="ffn_up_bwd")
    g_cw = jnp.concatenate([dcw[0, :3], dcw[1, :3]], axis=1)
    return dh2, dict(norm=g_norm[0], w_up=g_w_up, cw=g_cw, w_down=g_w_down)


def forward_backward(x, target, wa, wc, wf, *, late_weights=None, late_chunks=None):
    cst = _constants()
    h = x
    saved = []
    layer = 0
    while layer == 0 or layer < len(wf):
        i = layer // 2
        if layer % 2 == 0:
            h, sm, built = attn_layer_fwd(h, wa[i], cst, gather=late_weights if late_weights and layer == 0 else None)
            if built is not None:
                wa, wc, wf = built
        else:
            h, sm = conv_layer_fwd(h, wc[i])
        h, sf = ffn_layer_fwd(h, wf[layer])
        saved.append((sm, sf))
        layer += 1
    depth = len(wf)
    loss_blk, dh = loss_head(h, target, name="loss_head")
    ga, gc, gf = [None] * len(wa), [None] * len(wc), [None] * depth
    landed = None
    for layer in reversed(range(depth)):
        i = layer // 2
        sm, sf = saved[layer]
        dh, gf[layer] = ffn_layer_bwd(dh, wf[layer], sf)
        if layer % 2 == 0:
            chunks = None
            if late_chunks and layer == 0:
                chunks = lambda g_w_out: late_chunks([dict(w_out=g_w_out)] + ga[1:], gc, gf)
            dh, ga[i], got = attn_layer_bwd(dh, wa[i], sm, cst, scatter=chunks)
            landed = got if got is not None else landed
        else:
            dh, gc[i] = conv_layer_bwd(dh, wc[i], sm)
    return loss_blk, dh, ga, gc, gf, landed


def _part_rows(shape, width, row_mult):
    n = 1
    for s in shape:
        n *= s
    rows = -(-n // width)
    return -(-rows // row_mult) * row_mult


def _pack_rows(arrs, width, row_mult, dtype, total_rows=None, lead=0):
    parts = []
    used = 0
    for a in arrs:
        outer = a.shape[:lead]
        rows = _part_rows(a.shape[lead:], width, row_mult)
        flat = a.astype(dtype).reshape(outer + (-1,))
        flat = jnp.pad(flat, ((0, 0),) * lead + ((0, rows * width - flat.shape[-1]),))
        parts.append(flat.reshape(outer + (rows, width)))
        used += rows
    if total_rows is not None and total_rows > used:
        parts.append(jnp.zeros(parts[0].shape[:lead] + (total_rows - used, width), dtype))
    return jnp.concatenate(parts, axis=lead)


def _unpack_rows(packed, shapes, width, row_mult, lead=0):
    outer = packed.shape[:lead]
    out = []
    off = 0
    for shape in shapes:
        rows = _part_rows(shape, width, row_mult)
        n = 1
        for s in shape:
            n *= s
        flat = lax.slice_in_dim(packed, off, off + rows, axis=lead).reshape(outer + (-1,))
        out.append(lax.slice_in_dim(flat, 0, n, axis=lead).reshape(outer + tuple(shape)))
        off += rows
    return out


BIG_NAMES = ("attn_w_in", "attn_w_out", "conv_w_in", "conv_w_out", "ffn_w_up", "ffn_w_down")
BIG_AXIS = {"attn_w_in": 2, "attn_w_out": 1, "conv_w_in": 2, "conv_w_out": 1, "ffn_w_up": 2, "ffn_w_down": 1}
BIG_WIDTH = 1024
BIG_ROW_MULT = 16
BIG_TILE = 512
SMALL_TILE = 128
SMALL_SHARDED = ("conv_norm", "conv_kernel", "ffn_conv")
SMALL_AXIS = {"conv_norm": 1, "conv_kernel": 2, "ffn_conv": 2}
SMALL_REPLICATED = ("attn_norm", "attn_f_bias", "fox_q_gain", "fox_k_gain", "sb_q_gain", "sb_k_gain", "ffn_norm")
WEIGHT_ORDER = ("attn_norm", "attn_w_in", "attn_f_bias", "fox_q_gain", "fox_k_gain", "sb_q_gain", "sb_k_gain",
                "attn_w_out", "conv_norm", "conv_w_in", "conv_kernel", "conv_w_out", "ffn_norm", "ffn_w_up",
                "ffn_conv", "ffn_w_down")


def _big_total_rows(shapes):
    used = sum(_part_rows(s, BIG_WIDTH, BIG_ROW_MULT) for s in shapes)
    tile = BIG_TILE if used >= 8 * BIG_TILE else SMALL_TILE
    return -(-used // tile) * tile


def _place():
    x, y, c = lax.axis_index("x"), lax.axis_index("y"), lax.axis_index("c")
    other_chips = [(1 - x, y), (x, 1 - y), (1 - x, 1 - y)]
    return x, y, c, other_chips


_ANY = pl.BlockSpec(memory_space=pl.ANY)


def _chip_sems():
    return [pltpu.SemaphoreType.DMA((3,)), pltpu.SemaphoreType.DMA((3,)), pltpu.SemaphoreType.DMA]


def _chip_gather(src_ref, dst_ref, send_sems, recv_sems, local_sem):
    x, y, c, chips = _place()
    k = 2 * x + y

    def copy(j, slot):
        px, py = chips[j]
        return pltpu.make_async_remote_copy(src_ref=src_ref, dst_ref=dst_ref.at[slot], send_sem=send_sems.at[j],
                                            recv_sem=recv_sems.at[j], device_id=(px, py, c), device_id_type=MESH)

    def local():
        return pltpu.make_async_copy(src_ref, dst_ref.at[k], local_sem)

    def start():
        local().start()
        for j in range(3):
            copy(j, k).start()

    def finish():
        for j, (px, py) in enumerate(chips):
            copy(j, 2 * px + py).wait_recv()
        for j in range(3):
            copy(j, k).wait_send()
        local().wait()

    return start, finish


def _chip_scatter(g_ref, o_ref, send_sems, recv_sems, local_sem):
    x, y, c, chips = _place()
    k = 2 * x + y

    def copy(j, src_slot, dst_slot):
        px, py = chips[j]
        return pltpu.make_async_remote_copy(src_ref=g_ref.at[src_slot], dst_ref=o_ref.at[dst_slot],
                                            send_sem=send_sems.at[j], recv_sem=recv_sems.at[j],
                                            device_id=(px, py, c), device_id_type=MESH)

    def local():
        return pltpu.make_async_copy(g_ref.at[k], o_ref.at[k], local_sem)

    def start():
        local().start()
        for j, (px, py) in enumerate(chips):
            copy(j, 2 * px + py, k).start()

    def finish():
        for j, (px, py) in enumerate(chips):
            copy(j, k, 2 * px + py).wait_recv()
        for j, (px, py) in enumerate(chips):
            copy(j, 2 * px + py, k).wait_send()
        local().wait()

    return start, finish


def gather_chips(arrs, *, name):
    n = len(arrs)

    def body(*refs):
        hooks = [_chip_gather(refs[m], refs[n + m], *refs[2 * n + 3 * m:2 * n + 3 * m + 3]) for m in range(n)]
        for start, _ in hooks:
            start()
        for _, finish in hooks:
            finish()

    return pl.pallas_call(
        body, in_specs=[_ANY] * n, out_specs=[_ANY] * n,
        out_shape=[jax.ShapeDtypeStruct((N_CHIPS,) + a.shape, a.dtype) for a in arrs],
        scratch_shapes=_chip_sems() * n, name=name)(*arrs)


def scatter_chips(chunks, *, name):
    def body(g_ref, o_ref, send_sems, recv_sems, local_sem):
        start, finish = _chip_scatter(g_ref, o_ref, send_sems, recv_sems, local_sem)
        start()
        finish()

    return pl.pallas_call(
        body, in_specs=[_ANY], out_specs=_ANY, out_shape=jax.ShapeDtypeStruct(chunks.shape, chunks.dtype),
        scratch_shapes=_chip_sems(), name=name)(chunks)


def swap_cores(arrs, *, name):
    n = len(arrs)

    def body(*refs):
        x, y, c, _ = _place()
        copies = [pltpu.make_async_remote_copy(src_ref=refs[m], dst_ref=refs[n + m], send_sem=refs[2 * n + 2 * m],
                                               recv_sem=refs[2 * n + 2 * m + 1], device_id=(x, y, 1 - c),
                                               device_id_type=MESH) for m in range(n)]
        for cp in copies:
            cp.start()
        for cp in copies:
            cp.wait()

    return pl.pallas_call(
        body, in_specs=[_ANY] * n, out_specs=[_ANY] * n,
        out_shape=[jax.ShapeDtypeStruct(a.shape, a.dtype) for a in arrs],
        scratch_shapes=[pltpu.SemaphoreType.DMA, pltpu.SemaphoreType.DMA] * n, name=name)(*arrs)


def allreduce_small(p, *, name):
    r, w = p.shape

    def body(p_ref, o_ref, buf, send_sems, recv_sems):
        x, y, c, _ = _place()
        me = 4 * x + 2 * y + c
        buf[me] = p_ref[...]

        def peer_of(m):
            return (1 - x if m & 4 else x, 1 - y if m & 2 else y, 1 - c if m & 1 else c)

        def copy(m, slot):
            return pltpu.make_async_remote_copy(src_ref=p_ref, dst_ref=buf.at[slot], send_sem=send_sems.at[m - 1],
                                                recv_sem=recv_sems.at[m - 1], device_id=peer_of(m),
                                                device_id_type=MESH)

        sends = [copy(m, me) for m in range(1, 8)]
        for cp in sends:
            cp.start()
        for m in range(1, 8):
            px, py, pc = peer_of(m)
            copy(m, 4 * px + 2 * py + pc).wait_recv()
        for cp in sends:
            cp.wait_send()
        acc = buf[0]
        for d in range(1, 8):
            acc = acc + buf[d]
        o_ref[...] = acc

    vm = pl.BlockSpec(memory_space=pltpu.VMEM)
    return pl.pallas_call(
        body, in_specs=[vm], out_specs=vm, out_shape=jax.ShapeDtypeStruct((r, w), F32),
        scratch_shapes=[pltpu.VMEM((8, r, w), F32), pltpu.SemaphoreType.DMA((7,)), pltpu.SemaphoreType.DMA((7,))],
        name=name)(p)


def sum_chips(rv, *, name):
    _, r, w = rv.shape
    tile = BIG_TILE if r % BIG_TILE == 0 else SMALL_TILE

    def body(a_ref, b_ref, c_ref, d_ref, o_ref):
        o_ref[...] = ((a_ref[0].astype(F32) + b_ref[0].astype(F32)) + c_ref[0].astype(F32)) + d_ref[0].astype(F32)

    spec = lambda kk: pl.BlockSpec((1, tile, w), lambda i: (kk, i, 0))
    return pl.pallas_call(
        body, grid=(r // tile,), in_specs=[spec(0), spec(1), spec(2), spec(3)],
        out_specs=pl.BlockSpec((tile, w), lambda i: (i, 0)), out_shape=jax.ShapeDtypeStruct((r, w), F32),
        compiler_params=_params(), name=name)(rv, rv, rv, rv)


def add_pair(a, b, *, name):
    r, w = a.shape
    tile = BIG_TILE if r % BIG_TILE == 0 else SMALL_TILE

    def body(a_ref, b_ref, o_ref):
        o_ref[...] = a_ref[...] + b_ref[...]

    spec = pl.BlockSpec((tile, w), lambda i: (i, 0))
    return pl.pallas_call(body, grid=(r // tile,), in_specs=[spec, spec], out_specs=spec,
                          out_shape=jax.ShapeDtypeStruct((r, w), F32), compiler_params=_params(), name=name)(a, b)


def adamw(w, g, m, v, *, tm, name):
    r, c = w.shape
    assert r % tm == 0

    def body(w_ref, g_ref, m_ref, v_ref, d_ref, nm_ref, nv_ref):
        g_ = g_ref[...]
        m_ = ADAM_B1 * m_ref[...] + (1.0 - ADAM_B1) * g_
        v_ = ADAM_B2 * v_ref[...] + (1.0 - ADAM_B2) * (g_ * g_)
        m_hat = m_ / (1.0 - ADAM_B1 ** ADAM_STEP)
        v_hat = v_ / (1.0 - ADAM_B2 ** ADAM_STEP)
        d_ref[...] = -ADAM_LR * (m_hat / (jnp.sqrt(v_hat) + ADAM_EPS) + ADAM_WD * w_ref[...])
        nm_ref[...] = m_
        nv_ref[...] = v_

    spec = pl.BlockSpec((tm, c), lambda i: (i, 0))
    return pl.pallas_call(body, grid=(r // tm,), in_specs=[spec] * 4, out_specs=[spec] * 3,
                          out_shape=[jax.ShapeDtypeStruct((r, c), F32)] * 3, compiler_params=_params(), name=name)(w, g, m, v)


def kernel(x, attn_norm, attn_w_in, attn_f_bias, fox_q_gain, fox_k_gain, sb_q_gain, sb_k_gain, attn_w_out, conv_norm, conv_w_in, conv_kernel, conv_w_out, ffn_norm, ffn_w_up, ffn_conv, ffn_w_down, loss_target, m_attn_norm, m_attn_w_in, m_attn_f_bias, m_fox_q_gain, m_fox_k_gain, m_sb_q_gain, m_sb_k_gain, m_attn_w_out, m_conv_norm, m_conv_w_in, m_conv_kernel, m_conv_w_out, m_ffn_norm, m_ffn_w_up, m_ffn_conv, m_ffn_w_down, v_attn_norm, v_attn_w_in, v_attn_f_bias, v_fox_q_gain, v_fox_k_gain, v_sb_q_gain, v_sb_k_gain, v_attn_w_out, v_conv_norm, v_conv_w_in, v_conv_kernel, v_conv_w_out, v_ffn_norm, v_ffn_w_up, v_ffn_conv, v_ffn_w_down):
    a = dict(locals())
    chip = 2 * lax.axis_index("x") + lax.axis_index("y")
    n_attn, n_conv, depth = attn_norm.shape[0], conv_norm.shape[0], ffn_norm.shape[0]

    units = [(name, l) for name in BIG_NAMES for l in range(a[name].shape[0])]
    early = [("attn_w_in", 0)]
    late = [u for u in units if u not in early]
    late_b = [("attn_w_out", 0), ("conv_w_in", n_conv - 1), ("conv_w_out", n_conv - 1), ("ffn_w_up", depth - 1),
              ("ffn_w_down", depth - 1)]
    late_a = [u for u in late if u not in late_b]
    late_sb = [("ffn_w_up", 0), ("ffn_w_down", 0), ("ffn_w_up", 1), ("ffn_w_down", 1)]
    late_sa = [u for u in late if u not in late_sb]

    def unit_shape(u):
        return a[u[0]].shape[1:]

    def pack_units(us, get, lead=0):
        return _pack_rows([get(u) for u in us], BIG_WIDTH, BIG_ROW_MULT, BF16, _big_total_rows([unit_shape(u) for u in us]),
                          lead=lead)

    def unpack_units(packed, us, lead=0):
        return dict(zip(us, _unpack_rows(packed, [unit_shape(u) for u in us], BIG_WIDTH, BIG_ROW_MULT, lead=lead)))

    def full_units(gathered, us):
        full_size = {}
        for u, g4 in unpack_units(gathered, us, lead=1).items():
            _, rows, cols = g4.shape
            if BIG_AXIS[u[0]] - 1 == 0:
                full_size[u] = g4.reshape(N_CHIPS * rows, cols)
            else:
                full_size[u] = g4.transpose(1, 0, 2).reshape(rows, N_CHIPS * cols)
        return full_size

    def shard(u):
        return a[u[0]][u[1]]

    small_shapes = [a[n].shape for n in SMALL_SHARDED]
    packed_s = _pack_rows([a[n] for n in SMALL_SHARDED], 128, 8, F32)
    gath_e, gath_s = gather_chips([pack_units(early, shard), packed_s], name="gather_weights")
    full_e = full_units(gath_e, early)
    full = {}
    per_chip = [_unpack_rows(gath_s[kk], small_shapes, 128, 8) for kk in range(N_CHIPS)]
    for n, name in enumerate(SMALL_SHARDED):
        full[name] = jnp.concatenate([per_chip[kk][n] for kk in range(N_CHIPS)], axis=SMALL_AXIS[name])

    def attn_weights(i, fu):
        return dict(
            norm=attn_norm[i][None],
            w_in=jnp.pad(fu[("attn_w_in", i)], ((0, 0), (0, ATTN_IN_PAD - ATTN_IN))),
            fbias=jnp.pad(attn_f_bias[i], (0, 128 - H_FOX))[None],
            gq=jnp.concatenate([jnp.tile(fox_q_gain[i], H_FOX), jnp.tile(sb_q_gain[i], H_SB)])[None],
            gk=jnp.concatenate([jnp.tile(fox_k_gain[i], H_FOX), jnp.tile(sb_k_gain[i], H_SB)])[None],
            w_out=fu.get(("attn_w_out", i)))

    def build_weights(gathered):
        fu = {**full_e, **full_units(gathered[0], late_a), **full_units(gathered[1], late_b)}
        wa = [attn_weights(i, fu) for i in range(n_attn)]
        wc = [dict(norm=full["conv_norm"][i][None], w_in=fu[("conv_w_in", i)], ck=full["conv_kernel"][i][None],
                   w_out=fu[("conv_w_out", i)]) for i in range(n_conv)]
        wf = []
        for l in range(depth):
            cw = full["ffn_conv"][l]
            wf.append(dict(norm=ffn_norm[l][None], w_up=fu[("ffn_w_up", l)], cw2=jnp.stack([cw[:, :D_FF], cw[:, D_FF:]]),
                           w_down=fu[("ffn_w_down", l)]))
        return wa, wc, wf

    def chunks_of_unit(u, ga, gc, gf):
        name, l = u
        g = {"attn_w_in": lambda: ga[l]["w_in"], "attn_w_out": lambda: ga[l]["w_out"],
             "conv_w_in": lambda: gc[l]["w_in"], "conv_w_out": lambda: gc[l]["w_out"],
             "ffn_w_up": lambda: gf[l]["w_up"], "ffn_w_down": lambda: gf[l]["w_down"]}[name]()
        rows, cols = unit_shape(u)
        if BIG_AXIS[name] - 1 == 0:
            return g.reshape(N_CHIPS, rows, cols)
        return g.reshape(rows, N_CHIPS, cols).transpose(1, 0, 2)

    def chunks_of(us, ga, gc, gf):
        return pack_units(us, lambda u: chunks_of_unit(u, ga, gc, gf), lead=1)

    loss_blk, grad_x, ga, gc, gf, landed_late = forward_backward(
        x[0], loss_target[0], [attn_weights(0, full_e)], [], [],
        late_weights=(pack_units(late_a, shard), pack_units(late_b, shard), build_weights),
        late_chunks=lambda ga, gc, gf: (chunks_of(late_sa, ga, gc, gf), chunks_of(late_sb, ga, gc, gf)))

    landed = [scatter_chips(chunks_of(early, ga, gc, gf), name="scatter_grads"), landed_late[0], landed_late[1]]
    mine = [sum_chips(buf, name="sum_chips") for buf in landed]
    theirs = swap_cores(mine, name="swap_cores")
    g_units = {}
    for us, m, th in zip((early, late_sa, late_sb), mine, theirs):
        g_units.update(unpack_units(add_pair(m, th, name="add_cores"), us))
    grads = {name: jnp.stack([g_units[(name, l)] for l in range(a[name].shape[0])]) for name in BIG_NAMES}

    small_full = [
        loss_blk,
        jnp.stack([g["norm"] for g in ga]), jnp.stack([g["f_bias"] for g in ga]),
        jnp.stack([g["fox_q"] for g in ga]), jnp.stack([g["fox_k"] for g in ga]),
        jnp.stack([g["sb_q"] for g in ga]), jnp.stack([g["sb_k"] for g in ga]),
        jnp.stack([g["norm"] for g in gf]),
        jnp.stack([g["norm"] for g in gc]), jnp.stack([g["ck"] for g in gc]), jnp.stack([g["cw"] for g in gf]),
    ]
    summed = allreduce_small(_pack_rows(small_full, 128, 8, F32), name="allreduce_small")
    parts = _unpack_rows(summed, [p.shape for p in small_full], 128, 8)
    loss = parts[0][0, 0]
    for name, g in zip(SMALL_REPLICATED, parts[1:8]):
        grads[name] = g
    for name, g in zip(SMALL_SHARDED, parts[8:]):
        width = a[name].shape[SMALL_AXIS[name]]
        grads[name] = lax.dynamic_slice_in_dim(g, chip * width, width, axis=SMALL_AXIS[name])

    delta, new_m, new_v = {}, {}, {}
    for name in BIG_NAMES:
        shape = a[name].shape
        flat = lambda arr: arr.reshape(-1, shape[-1])
        d_, m_, v_ = adamw(flat(a[name]), flat(grads[name]), flat(a["m_" + name]), flat(a["v_" + name]), tm=256,
                           name="adamw")
        delta[name], new_m[name], new_v[name] = d_.reshape(shape), m_.reshape(shape), v_.reshape(shape)
    small_names = SMALL_REPLICATED + SMALL_SHARDED
    small_shapes_local = [a[n].shape for n in small_names]
    pack = lambda prefix, src: _pack_rows([src[prefix + n] for n in small_names], 128, 8, F32)
    packed = adamw(pack("", a), pack("", grads), pack("m_", a), pack("v_", a), tm=8, name="adamw_small")
    for store, buf in zip((delta, new_m, new_v), packed):
        for name, arr in zip(small_names, _unpack_rows(buf, small_shapes_local, 128, 8)):
            store[name] = arr

    return (loss, grad_x[None], *[grads[n] for n in WEIGHT_ORDER], *[delta[n] for n in WEIGHT_ORDER],
            *[new_m[n] for n in WEIGHT_ORDER], *[new_v[n] for n in WEIGHT_ORDER])
```

```python
import functools

import jax
import jax.numpy as jnp
from jax import lax
from jax.experimental import pallas as pl
from jax.experimental.pallas import tpu as pltpu

F32 = jnp.float32
BF16 = jnp.bfloat16

D_MODEL = 1024
HEAD_DIM = 64
H_FOX = 8
H_SB = 8
N_HEADS = H_FOX + H_SB
MIX = N_HEADS * HEAD_DIM
ATTN_IN = 3 * MIX + H_FOX
ATTN_IN_PAD = 3 * MIX + 128
D_FF = 2816
EPS = 1e-6
SCALE = HEAD_DIM ** -0.5
NEG = -1e30

ADAM_LR = 0.001
ADAM_B1 = 0.9
ADAM_B2 = 0.999
ADAM_EPS = 1e-08
ADAM_WD = 0.01
ADAM_STEP = 10

VMEM_LIMIT = 56 * 1024 * 1024
HALO = 8
BQ = 512
N_CHIPS = 4
MESH = pl.DeviceIdType.MESH


def _params(**kw):
    return pltpu.CompilerParams(vmem_limit_bytes=VMEM_LIMIT, **kw)


def _dot(a, b):
    return jnp.dot(a, b, preferred_element_type=F32)


def _dot_nt(a, b):
    return lax.dot_general(a, b, (((1,), (1,)), ((), ())), preferred_element_type=F32)


def _dot_tn(a, b):
    return lax.dot_general(a, b, (((0,), (0,)), ((), ())), preferred_element_type=F32)


def _split2(x):
    hi = x.astype(BF16)
    lo = (x - hi.astype(F32)).astype(BF16)
    return hi, lo


def _split3(x):
    hi = x.astype(BF16)
    r = x - hi.astype(F32)
    mid = r.astype(BF16)
    lo = (r - mid.astype(F32)).astype(BF16)
    return hi, mid, lo


def mm_nn(a, b, *, add=None, out_dtype=F32, parts=1, tm=1024, tn=512, name):
    m, k = a.shape
    n = b.shape[1]
    np_ = n // parts
    nb = np_ // tn
    tm = min(tm, m)
    assert m % tm == 0 and np_ % tn == 0

    def body(*refs):
        if add is None:
            a_ref, b_ref, o_ref = refs
            acc = _dot(a_ref[...].astype(BF16), b_ref[...])
        else:
            a_ref, b_ref, r_ref, o_ref = refs
            acc = _dot(a_ref[...].astype(BF16), b_ref[...]) + r_ref[...]
        o_ref[...] = acc.astype(out_dtype).reshape(o_ref.shape)

    in_specs = [pl.BlockSpec((tm, k), lambda i, j: (i, 0)), pl.BlockSpec((k, tn), lambda i, j: (0, j))]
    args = [a, b]
    if add is not None:
        in_specs.append(pl.BlockSpec((tm, tn), lambda i, j: (i, j)))
        args.append(add)
    if parts == 1:
        out_spec = pl.BlockSpec((tm, tn), lambda i, j: (i, j))
        out_shape = jax.ShapeDtypeStruct((m, n), out_dtype)
    else:
        out_spec = pl.BlockSpec((1, tm, tn), lambda i, j: (j // nb, i, j % nb))
        out_shape = jax.ShapeDtypeStruct((parts, m, np_), out_dtype)
    return pl.pallas_call(body, grid=(m // tm, n // tn), in_specs=in_specs, out_specs=out_spec,
                          out_shape=out_shape, compiler_params=_params(), name=name)(*args)


def mm_nt_heads(a, b3, *, tm=512, name):
    t, d = a.shape
    nh, n, _ = b3.shape
    tm = min(tm, t)
    assert t % tm == 0

    def body(a_ref, b_ref, o_ref):
        a_b = a_ref[...].astype(BF16)
        for h in range(nh):
            o_ref[h] = _dot_nt(a_b, b_ref[h]).astype(BF16)

    return pl.pallas_call(
        body, grid=(t // tm,),
        in_specs=[pl.BlockSpec((tm, d), lambda i: (i, 0)), pl.BlockSpec((nh, n, d), lambda i: (0, 0, 0))],
        out_specs=pl.BlockSpec((nh, tm, n), lambda i: (0, i, 0)),
        out_shape=jax.ShapeDtypeStruct((nh, t, n), BF16), compiler_params=_params(), name=name)(a, b3)


def mm_tn(a, b3, *, tk=512, tn=512, tt=2048, name):
    t, k = a.shape
    p, _, np_ = b3.shape
    nb = np_ // tn
    tt = min(tt, t)
    assert t % tt == 0 and k % tk == 0 and np_ % tn == 0

    def body(a_ref, b_ref, o_ref):
        prod = _dot_tn(a_ref[...].astype(BF16), b_ref[0].astype(BF16))

        @pl.when(pl.program_id(2) == 0)
        def _():
            o_ref[...] = prod

        @pl.when(pl.program_id(2) > 0)
        def _():
            o_ref[...] += prod

    return pl.pallas_call(
        body, grid=(k // tk, p * nb, t // tt),
        in_specs=[pl.BlockSpec((tt, tk), lambda i, j, s: (s, i)), pl.BlockSpec((1, tt, tn), lambda i, j, s: (j // nb, s, j % nb))],
        out_specs=pl.BlockSpec((tk, tn), lambda i, j, s: (i, j)),
        out_shape=jax.ShapeDtypeStruct((k, p * np_), F32),
        compiler_params=_params(), name=name)(a, b3)


def rms_mm_nn(h, g, b, *, parts=1, tm=1024, tn=512, name):
    t, d = h.shape
    n = b.shape[1]
    np_ = n // parts
    nb = np_ // tn
    tm = min(tm, t)
    assert t % tm == 0 and np_ % tn == 0

    def body(h_ref, g_ref, b_ref, xn_ref, o_ref):
        @pl.when(pl.program_id(1) == 0)
        def _():
            x = h_ref[...]
            r = lax.rsqrt(jnp.mean(x * x, axis=-1, keepdims=True) + EPS)
            xn_ref[...] = (x * r * g_ref[...]).astype(BF16)

        o_ref[...] = _dot(xn_ref[...], b_ref[...]).reshape(o_ref.shape)

    if parts == 1:
        out_spec = pl.BlockSpec((tm, tn), lambda i, j: (i, j))
        out_shape = jax.ShapeDtypeStruct((t, n), F32)
    else:
        out_spec = pl.BlockSpec((1, tm, tn), lambda i, j: (j // nb, i, j % nb))
        out_shape = jax.ShapeDtypeStruct((parts, t, np_), F32)
    row = pl.BlockSpec((tm, d), lambda i, j: (i, 0))
    return pl.pallas_call(
        body, grid=(t // tm, n // tn),
        in_specs=[row, pl.BlockSpec((1, d), lambda i, j: (0, 0)), pl.BlockSpec((d, tn), lambda i, j: (0, j))],
        out_specs=[row, out_spec], out_shape=[jax.ShapeDtypeStruct((t, d), BF16), out_shape],
        compiler_params=_params(), name=name)(h, g, b)


def mm_nt_rms_bwd(a3, b, h, g, dres, *, name, tm=512):
    p, t, kp = a3.shape
    d = b.shape[0]
    tm = min(tm, t)
    assert t % tm == 0 and b.shape[1] == p * kp

    def body(a_ref, b_ref, h_ref, g_ref, dres_ref, dh_ref, dg_ref, acc_ref):
        i = pl.program_id(0)
        part = pl.program_id(1)
        prod = _dot_nt(a_ref[0].astype(BF16), b_ref[...])

        @pl.when(part == 0)
        def _():
            acc_ref[...] = prod

        @pl.when(part > 0)
        def _():
            acc_ref[...] += prod

        @pl.when(part == p - 1)
        def _():
            x = h_ref[...]
            dy = acc_ref[...]
            r = lax.rsqrt(jnp.mean(x * x, axis=-1, keepdims=True) + EPS)
            gy = dy * g_ref[...]
            dot = jnp.mean(gy * x, axis=-1, keepdims=True)
            dh_ref[...] = dres_ref[...] + r * gy - x * (r * r * r * dot)
            _acc_rows(dg_ref, jnp.sum(dy * x * r, axis=0, keepdims=True), i == 0)

    row = pl.BlockSpec((tm, d), lambda i, q: (i, 0))
    vec = pl.BlockSpec((1, d), lambda i, q: (0, 0))
    return pl.pallas_call(
        body, grid=(t // tm, p),
        in_specs=[pl.BlockSpec((1, tm, kp), lambda i, q: (q, i, 0)), pl.BlockSpec((d, kp), lambda i, q: (0, q)),
                  row, vec, row],
        out_specs=[row, vec],
        out_shape=[jax.ShapeDtypeStruct((t, d), F32), jax.ShapeDtypeStruct((1, d), F32)],
        scratch_shapes=[pltpu.VMEM((tm, d), F32)], compiler_params=_params(), name=name)(a3, b, h, g, dres)


def _causal3(x, w):
    return w[0:1] * pltpu.roll(x, 2, 0) + w[1:2] * pltpu.roll(x, 1, 0) + w[2:3] * x


def _causal3_taps(x_ext, w, tm):
    x2 = pltpu.roll(x_ext, 2, 0)
    x1 = pltpu.roll(x_ext, 1, 0)
    y = w[0:1] * x2 + w[1:2] * x1 + w[2:3] * x_ext
    return y, (x2[HALO:HALO + tm], x1[HALO:HALO + tm], x_ext[HALO:HALO + tm])


def _anticausal3(z, w):
    n = z.shape[0]
    return w[2:3] * z + w[1:2] * pltpu.roll(z, n - 1, 0) + w[0:1] * pltpu.roll(z, n - 2, 0)


def _prev_spec(part, tm, tc, nrow8):
    del nrow8
    return pl.BlockSpec((1, HALO, tc), lambda j, i: (part, jnp.maximum(i * (tm // HALO) - 1, 0), j))


def _next_spec(part, tm, tc, nrow8):
    return pl.BlockSpec((1, HALO, tc), lambda j, i: (part, jnp.minimum((i + 1) * (tm // HALO), nrow8 - 1), j))


def _tile_spec(part, tm, tc):
    return pl.BlockSpec((1, tm, tc), lambda j, i: (part, i, j))


def _acc_rows(ref, val, first):
    @pl.when(first)
    def _():
        ref[...] = val

    @pl.when(jnp.logical_not(first))
    def _():
        ref[...] += val


def ffn_act_down_fwd(up2, cw2, w_down, h, *, name, tm=256, tc=1408):
    _, t, f = up2.shape
    d = h.shape[1]
    tm = min(tm, t)

    def body(g_ref, v_ref, gp_ref, vp_ref, w_ref, wd_ref, h_ref, o_ref, act_ref):
        keep = jnp.where(pl.program_id(0) == 0, 0.0, 1.0)
        for cc in range(f // tc):
            cols = slice(cc * tc, (cc + 1) * tc)
            g_ext = jnp.concatenate([gp_ref[0, :, cols] * keep, g_ref[0, :, cols]], axis=0)
            v_ext = jnp.concatenate([vp_ref[0, :, cols] * keep, v_ref[0, :, cols]], axis=0)
            ug = _causal3(g_ext, w_ref[0, :, cols])[HALO:]
            uv = _causal3(v_ext, w_ref[1, :, cols])[HALO:]
            act_ref[:, cols] = (ug * jax.nn.sigmoid(ug) * uv).astype(BF16)
        o_ref[...] = _dot(act_ref[...], wd_ref[...]) + h_ref[...]

    tile = lambda part: pl.BlockSpec((1, tm, f), lambda i: (part, i, 0))
    prev = lambda part: pl.BlockSpec((1, HALO, f), lambda i: (part, jnp.maximum(i * (tm // HALO) - 1, 0), 0))
    row = pl.BlockSpec((tm, d), lambda i: (i, 0))
    return pl.pallas_call(
        body, grid=(t // tm,),
        in_specs=[tile(0), tile(1), prev(0), prev(1), pl.BlockSpec((2, 3, f), lambda i: (0, 0, 0)),
                  pl.BlockSpec((f, d), lambda i: (0, 0)), row],
        out_specs=[row, pl.BlockSpec((tm, f), lambda i: (i, 0))],
        out_shape=[jax.ShapeDtypeStruct((t, d), F32), jax.ShapeDtypeStruct((t, f), BF16)],
        compiler_params=_params(), name=name)(up2, up2, up2, up2, cw2, w_down, h)


def ffn_act_bwd(dh, w_down, up2, cw2, *, name, tm=256, tc=1408):
    _, t, f = up2.shape
    d = dh.shape[1]
    n8 = t // HALO

    def body(d_ref, dn_ref, wd_ref, g_ref, v_ref, gp_ref, vp_ref, gn_ref, vn_ref, wg_ref, wv_ref, dup_ref, dw_ref):
        i = pl.program_id(1)
        first = i == 0
        keep_p = jnp.where(first, 0.0, 1.0)
        keep_n = jnp.where(i == pl.num_programs(1) - 1, 0.0, 1.0)
        wg = wg_ref[0]
        wv = wv_ref[0]
        g_ext = jnp.concatenate([gp_ref[0] * keep_p, g_ref[0], gn_ref[0]], axis=0)
        v_ext = jnp.concatenate([vp_ref[0] * keep_p, v_ref[0], vn_ref[0]], axis=0)
        dh_ext = jnp.concatenate([d_ref[...], dn_ref[...] * keep_n], axis=0)
        d_ext = _dot_nt(dh_ext.astype(BF16), wd_ref[...])
        ug, (g2, g1, g0) = _causal3_taps(g_ext, wg, tm)
        uv, (v2, v1, v0) = _causal3_taps(v_ext, wv, tm)
        ug = ug[HALO:]
        uv = uv[HALO:]
        s = jax.nn.sigmoid(ug)
        dg = d_ext * uv * (s * (1.0 + ug * (1.0 - s)))
        dv = d_ext * (ug * s)
        dup_ref[0] = _anticausal3(dg, wg)[:tm].astype(BF16)
        dup_ref[1] = _anticausal3(dv, wv)[:tm].astype(BF16)
        dgt = dg[:tm]
        dvt = dv[:tm]
        zero = jnp.zeros((HALO - 3, tc), F32)
        rows_g = [jnp.sum(dgt * x, axis=0, keepdims=True) for x in (g2, g1, g0)] + [zero]
        rows_v = [jnp.sum(dvt * x, axis=0, keepdims=True) for x in (v2, v1, v0)] + [zero]
        _acc_rows(dw_ref, jnp.stack([jnp.concatenate(rows_g, axis=0), jnp.concatenate(rows_v, axis=0)]), first)

    wspec = lambda part: pl.BlockSpec((1, 3, tc), lambda j, i: (part, 0, j))
    return pl.pallas_call(
        body, grid=(f // tc, t // tm),
        in_specs=[pl.BlockSpec((tm, d), lambda j, i: (i, 0)),
                  pl.BlockSpec((HALO, d), lambda j, i: (jnp.minimum((i + 1) * (tm // HALO), n8 - 1), 0)),
                  pl.BlockSpec((tc, d), lambda j, i: (j, 0)),
                  _tile_spec(0, tm, tc), _tile_spec(1, tm, tc), _prev_spec(0, tm, tc, n8), _prev_spec(1, tm, tc, n8),
                  _next_spec(0, tm, tc, n8), _next_spec(1, tm, tc, n8), wspec(0), wspec(1)],
        out_specs=[pl.BlockSpec((2, tm, tc), lambda j, i: (0, i, j)), pl.BlockSpec((2, HALO, tc), lambda j, i: (0, 0, j))],
        out_shape=[jax.ShapeDtypeStruct((2, t, f), BF16), jax.ShapeDtypeStruct((2, HALO, f), F32)],
        compiler_params=_params(), name=name)(dh, dh, w_down, up2, up2, up2, up2, up2, up2, cw2, cw2)


def conv_mix_fwd(proj3, ck, *, name, tm=512, tc=512):
    _, t, c = proj3.shape
    n8 = t // HALO

    def body(b_ref, c_ref, u_ref, cp_ref, up_ref, w_ref, o_ref):
        keep = jnp.where(pl.program_id(1) == 0, 0.0, 1.0)
        cu_ext = jnp.concatenate([cp_ref[0] * up_ref[0] * keep, c_ref[0] * u_ref[0]], axis=0)
        o_ref[...] = (b_ref[0] * _causal3(cu_ext, w_ref[0])[HALO:]).astype(BF16)

    return pl.pallas_call(
        body, grid=(c // tc, t // tm),
        in_specs=[_tile_spec(0, tm, tc), _tile_spec(1, tm, tc), _tile_spec(2, tm, tc), _prev_spec(1, tm, tc, n8),
                  _prev_spec(2, tm, tc, n8), pl.BlockSpec((1, 3, tc), lambda j, i: (0, 0, j))],
        out_specs=pl.BlockSpec((tm, tc), lambda j, i: (i, j)),
        out_shape=jax.ShapeDtypeStruct((t, c), BF16), compiler_params=_params(), name=name)(proj3, proj3, proj3, proj3, proj3, ck)


def conv_mix_out_fwd(proj3, ck, w_out, h, *, name, tm=512):
    _, t, c = proj3.shape
    d = h.shape[1]
    tm = min(tm, t)

    def body(b_ref, c_ref, u_ref, cp_ref, up_ref, w_ref, wo_ref, h_ref, o_ref, y_ref):
        keep = jnp.where(pl.program_id(0) == 0, 0.0, 1.0)
        cu_ext = jnp.concatenate([cp_ref[0] * up_ref[0] * keep, c_ref[0] * u_ref[0]], axis=0)
        y_ref[...] = (b_ref[0] * _causal3(cu_ext, w_ref[0])[HALO:]).astype(BF16)
        o_ref[...] = _dot(y_ref[...], wo_ref[...]) + h_ref[...]

    tile = lambda part: pl.BlockSpec((1, tm, c), lambda i: (part, i, 0))
    prev = lambda part: pl.BlockSpec((1, HALO, c), lambda i: (part, jnp.maximum(i * (tm // HALO) - 1, 0), 0))
    row = pl.BlockSpec((tm, d), lambda i: (i, 0))
    return pl.pallas_call(
        body, grid=(t // tm,),
        in_specs=[tile(0), tile(1), tile(2), prev(1), prev(2), pl.BlockSpec((1, 3, c), lambda i: (0, 0, 0)),
                  pl.BlockSpec((c, d), lambda i: (0, 0)), row],
        out_specs=[row, pl.BlockSpec((tm, c), lambda i: (i, 0))],
        out_shape=[jax.ShapeDtypeStruct((t, d), F32), jax.ShapeDtypeStruct((t, c), BF16)],
        compiler_params=_params(), name=name)(proj3, proj3, proj3, proj3, proj3, ck, w_out, h)


def conv_mix_bwd(dh, w_out, proj3, ck, *, name, tm=512, tc=512):
    _, t, c = proj3.shape
    d = dh.shape[1]
    n8 = t // HALO

    def body(d_ref, dn_ref, wo_ref, b_ref, c_ref, u_ref, cp_ref, up_ref, bn_ref, w_ref, dp_ref, dw_ref):
        i = pl.program_id(1)
        first = i == 0
        keep_p = jnp.where(first, 0.0, 1.0)
        keep_n = jnp.where(i == pl.num_programs(1) - 1, 0.0, 1.0)
        w = w_ref[0]
        cu_ext = jnp.concatenate([cp_ref[0] * up_ref[0] * keep_p, c_ref[0] * u_ref[0]], axis=0)
        cv, (x2, x1, x0) = _causal3_taps(cu_ext, w, tm)
        cv = cv[HALO:]
        dh_ext = jnp.concatenate([d_ref[...], dn_ref[...] * keep_n], axis=0)
        d_ext = _dot_nt(dh_ext.astype(BF16), wo_ref[...])
        dyt = d_ext[:tm]
        b_ext = jnp.concatenate([b_ref[0], bn_ref[0]], axis=0)
        dcv = d_ext * b_ext
        dcu = _anticausal3(dcv, w)[:tm]
        dp_ref[0] = (dyt * cv).astype(BF16)
        dp_ref[1] = (dcu * u_ref[0]).astype(BF16)
        dp_ref[2] = (dcu * c_ref[0]).astype(BF16)
        dcvt = dcv[:tm]
        rows = [jnp.sum(dcvt * x, axis=0, keepdims=True) for x in (x2, x1, x0)] + [jnp.zeros((HALO - 3, tc), F32)]
        _acc_rows(dw_ref, jnp.concatenate(rows, axis=0)[None], first)

    return pl.pallas_call(
        body, grid=(c // tc, t // tm),
        in_specs=[pl.BlockSpec((tm, d), lambda j, i: (i, 0)),
                  pl.BlockSpec((HALO, d), lambda j, i: (jnp.minimum((i + 1) * (tm // HALO), n8 - 1), 0)),
                  pl.BlockSpec((tc, d), lambda j, i: (j, 0)),
                  _tile_spec(0, tm, tc), _tile_spec(1, tm, tc), _tile_spec(2, tm, tc),
                  _prev_spec(1, tm, tc, n8), _prev_spec(2, tm, tc, n8),
                  _next_spec(0, tm, tc, n8), pl.BlockSpec((1, 3, tc), lambda j, i: (0, 0, j))],
        out_specs=[pl.BlockSpec((3, tm, tc), lambda j, i: (0, i, j)), pl.BlockSpec((1, HALO, tc), lambda j, i: (0, 0, j))],
        out_shape=[jax.ShapeDtypeStruct((3, t, c), BF16), jax.ShapeDtypeStruct((1, HALO, c), F32)],
        compiler_params=_params(), name=name)(dh, dh, w_out, proj3, proj3, proj3, proj3, proj3, proj3, ck)


def _head_sums(x, bd):
    hi, lo = _split2(x)
    return _dot(hi, bd) + _dot(lo, bd)


def attn_prep_fwd(proj, gq, gk, fbias, bd, *, name, tm=256):
    t = proj.shape[0]

    def body(q_ref, k_ref, v_ref, f_ref, gq_ref, gk_ref, fb_ref, bd_ref, qa_ref, ka_ref, va_ref, lf_ref):
        bd = bd_ref[...]
        lane = lax.broadcasted_iota(jnp.int32, (tm, 128), 1)
        low = lane < HEAD_DIM

        def two_heads(y, o_ref, c, rest):
            o_ref[2 * c] = jnp.where(low, y, rest).astype(BF16)
            o_ref[2 * c + 1] = jnp.where(low, pltpu.roll(y, HEAD_DIM, 1), rest).astype(BF16)

        def headnorm(x_ref, g_ref, o_ref, scale):
            for c in range(MIX // 128):
                sl = slice(128 * c, 128 * (c + 1))
                x = x_ref[:, sl]
                r = lax.rsqrt(_head_sums(x * x, bd) * (1.0 / HEAD_DIM) + EPS)
                two_heads(x * r * (g_ref[:, sl] * scale), o_ref, c, 0.0)

        headnorm(q_ref, gq_ref, qa_ref, SCALE)
        headnorm(k_ref, gk_ref, ka_ref, 1.0)
        one_at_64 = jnp.where(lane == HEAD_DIM, 1.0, 0.0)
        for c in range(MIX // 128):
            two_heads(v_ref[:, 128 * c:128 * (c + 1)], va_ref, c, one_at_64)
        fl = f_ref[...] + fb_ref[...]
        logf = jnp.minimum(fl, 0.0) - jnp.log(1.0 + jnp.exp(-jnp.abs(fl)))
        lf_ref[...] = logf.T[0:H_FOX, :]

    col = lambda c: pl.BlockSpec((tm, MIX), lambda i: (i, c))
    vec = pl.BlockSpec((1, MIX), lambda i: (0, 0))
    out = pl.BlockSpec((N_HEADS, tm, 2 * HEAD_DIM), lambda i: (0, i, 0))
    return pl.pallas_call(
        body, grid=(t // tm,),
        in_specs=[col(0), col(1), col(2), pl.BlockSpec((tm, 128), lambda i: (i, 3 * MIX // 128)), vec, vec,
                  pl.BlockSpec((1, 128), lambda i: (0, 0)), pl.BlockSpec((128, 128), lambda i: (0, 0))],
        out_specs=[out, out, out, pl.BlockSpec((H_FOX, tm), lambda i: (0, i))],
        out_shape=[jax.ShapeDtypeStruct((N_HEADS, t, 2 * HEAD_DIM), BF16)] * 3 + [jax.ShapeDtypeStruct((H_FOX, t), F32)],
        compiler_params=_params(), name=name)(proj, proj, proj, proj, gq, gk, fbias, bd)


def attn_prep_bwd(proj, dq_f, dq_s, dkv_f, dkv_s, dfl, gq, gk, bd, *, name, tm=256):
    t = proj.shape[0]
    per_group = H_FOX // 2

    def body(q_ref, k_ref, dqf_ref, dqs_ref, dkvf_ref, dkvs_ref, dfl_ref, gq_ref, gk_ref, bd_ref, dp_ref, dgq_ref,
             dgk_ref):
        bd = bd_ref[...]
        first = pl.program_id(0) == 0
        low = lax.broadcasted_iota(jnp.int32, (tm, 128), 1) < HEAD_DIM

        def two_heads(fox_ref, sb_ref, c):
            ref = fox_ref if c < per_group else sb_ref
            return ref[2 * (c % per_group)], ref[2 * (c % per_group) + 1]

        def low_halves(ab):
            return jnp.where(low, ab[0], pltpu.roll(ab[1], HEAD_DIM, 1))

        def high_halves(ab):
            return jnp.where(low, pltpu.roll(ab[0], HEAD_DIM, 1), ab[1])

        def back(x_ref, grad, g_ref, col0, scale, dg_ref):
            parts = []
            for c in range(MIX // 128):
                sl = slice(128 * c, 128 * (c + 1))
                x = x_ref[:, sl]
                r = lax.rsqrt(_head_sums(x * x, bd) * (1.0 / HEAD_DIM) + EPS)
                dn = grad(c) * scale
                gy = dn * g_ref[:, sl]
                hs = _head_sums(gy * x, bd) * (1.0 / HEAD_DIM)
                dp_ref[:, col0 + 128 * c:col0 + 128 * (c + 1)] = (r * gy - x * (r * r * r * hs)).astype(BF16)
                parts.append(jnp.sum(dn * x * r, axis=0, keepdims=True))
            _acc_rows(dg_ref, jnp.concatenate(parts, axis=1), first)

        back(q_ref, lambda c: low_halves(two_heads(dqf_ref, dqs_ref, c)), gq_ref, 0, SCALE, dgq_ref)
        back(k_ref, lambda c: low_halves(two_heads(dkvf_ref, dkvs_ref, c)), gk_ref, MIX, 1.0, dgk_ref)
        for c in range(MIX // 128):
            dp_ref[:, 2 * MIX + 128 * c:2 * MIX + 128 * (c + 1)] = high_halves(two_heads(dkvf_ref, dkvs_ref, c)).astype(BF16)
        dp_ref[:, 3 * MIX:] = dfl_ref[...]

    col = lambda c: pl.BlockSpec((tm, MIX), lambda i: (i, c))
    heads = pl.BlockSpec((H_FOX, tm, 128), lambda i: (0, i, 0))
    vec = pl.BlockSpec((1, MIX), lambda i: (0, 0))
    return pl.pallas_call(
        body, grid=(t // tm,),
        in_specs=[col(0), col(1), heads, heads, heads, heads, pl.BlockSpec((tm, 128), lambda i: (i, 0)), vec, vec,
                  pl.BlockSpec((128, 128), lambda i: (0, 0))],
        out_specs=[pl.BlockSpec((tm, ATTN_IN_PAD), lambda i: (i, 0)), vec, vec],
        out_shape=[jax.ShapeDtypeStruct((t, ATTN_IN_PAD), BF16), jax.ShapeDtypeStruct((1, MIX), F32),
                   jax.ShapeDtypeStruct((1, MIX), F32)],
        compiler_params=_params(), name=name)(proj, proj, dq_f, dq_s, dkv_f, dkv_s, dfl, gq, gk, bd)


def gate_cumsum(logf3, tri, *, name):
    nc, r, _ = logf3.shape

    def body(x_ref, tri_ref, o_ref):
        tri_m = tri_ref[...]

        def step(c, carry):
            hi, mid, lo = _split3(x_ref[c])
            cs = _dot(hi, tri_m) + _dot(mid, tri_m) + _dot(lo, tri_m) + carry
            o_ref[c] = cs
            return cs[:, 127:128]

        lax.fori_loop(0, nc, step, jnp.zeros((r, 1), F32))

    return pl.pallas_call(body, out_shape=jax.ShapeDtypeStruct(logf3.shape, F32), compiler_params=_params(),
                          name=name)(logf3, tri)


def gate_cumsum_bwd(dcum3, logf3, tri, *, name):
    nc, r, _ = dcum3.shape

    def body(x_ref, lf_ref, tri_ref, o_ref, s_ref):
        tri_m = tri_ref[...]

        def step(n, carry):
            car, tot = carry
            c = nc - 1 - n
            hi, mid, lo = _split3(x_ref[c])
            cs = _dot(hi, tri_m) + _dot(mid, tri_m) + _dot(lo, tri_m) + car
            dl = cs * (1.0 - jnp.exp(lf_ref[c]))
            o_ref[c] = dl
            return cs[:, 0:1], tot + dl

        _, tot = lax.fori_loop(0, nc, step, (jnp.zeros((r, 1), F32), jnp.zeros((r, 128), F32)))
        s_ref[...] = jnp.broadcast_to(jnp.sum(tot, axis=1, keepdims=True), tot.shape)

    return pl.pallas_call(body, out_shape=[jax.ShapeDtypeStruct(dcum3.shape, F32), jax.ShapeDtypeStruct((r, 128), F32)],
                          compiler_params=_params(), name=name)(dcum3, logf3, tri)


def _causal_iota():
    row = lax.broadcasted_iota(jnp.int32, (BQ, BQ), 0)
    col = lax.broadcasted_iota(jnp.int32, (BQ, BQ), 1)
    return row, col


def _head_specs(nj, head0):
    qin = pl.BlockSpec((1, BQ, HEAD_DIM), lambda h, i: (h + head0, i, 0))
    kin = pl.BlockSpec((1, nj, BQ, HEAD_DIM), lambda h, i: (h + head0, 0, 0, 0))
    qin2 = pl.BlockSpec((1, BQ, 2 * HEAD_DIM), lambda h, i: (h + head0, i, 0))
    kin2 = pl.BlockSpec((1, nj, BQ, 2 * HEAD_DIM), lambda h, i: (h + head0, 0, 0, 0))
    qspec = pl.BlockSpec((1, BQ, HEAD_DIM), lambda h, i: (h, i, 0))
    kspec2 = pl.BlockSpec((1, nj, BQ, 2 * HEAD_DIM), lambda h, i: (h, 0, 0, 0))
    return qin, kin, qin2, kin2, qspec, kspec2


STOP = -105.0
STOP_WIDE = -115.0
FIXED_REF_MAX = 40.0


def _store_kmax(k_ref, kmax_ref, nj):
    def step(j, mx):
        kf = k_ref[0, j].astype(F32)
        return jnp.maximum(mx, jnp.max(jnp.sum(kf * kf, axis=1, keepdims=True), axis=0, keepdims=True))

    mx = lax.fori_loop(0, nj, step, jnp.zeros((1, 1), F32))
    kmax_ref[...] = jnp.broadcast_to(jnp.sqrt(mx), kmax_ref.shape)


def _qk_bound(q, kmax_ref):
    qf = q.astype(F32)
    return jnp.sqrt(jnp.sum(qf * qf, axis=1, keepdims=True)) * kmax_ref[0:1, 0:1] * 1.001


def _first_and_last_step():
    h, i = pl.program_id(0), pl.program_id(1)
    first = jnp.logical_and(h == 0, i == 0)
    last = jnp.logical_and(h == pl.num_programs(0) - 1, i == pl.num_programs(1) - 1)
    return first, last


def fox_fwd(qa, ka4, va4, fcol, frow4, *, name, gather=None):
    t = qa.shape[1]
    dh = HEAD_DIM
    nh = H_FOX
    nj = t // BQ

    def body(*refs):
        if gather is None:
            q_ref, k_ref, v_ref, fc_ref, fr_ref, o_ref, lse_ref, kmax_ref = refs
        else:
            q_ref, k_ref, v_ref, fc_ref, fr_ref, src_ref, o_ref, lse_ref, dst_ref, kmax_ref = refs[:10]
            first_step, last_step = _first_and_last_step()

            @pl.when(first_step)
            def _():
                _chip_gather(src_ref, dst_ref, *refs[10:])[0]()

        i = pl.program_id(1)

        @pl.when(i == 0)
        def _():
            _store_kmax(k_ref, kmax_ref, nj)

        q = q_ref[0]
        fq = fc_ref[0]
        bound = _qk_bound(q, kmax_ref)
        row, col = _causal_iota()

        def gate_at_block_end(j):
            return fr_ref[0, j][:, BQ - 1:BQ]

        def pv(p, j):
            p_hi, p_lo = _split2(p)
            return _dot(p_hi, v_ref[0, j]) + _dot(p_lo, v_ref[0, j])

        def walk(block, live, init):
            carry = block(i, init, True)

            def cond(c):
                n, carry = c
                return jnp.logical_and(n < i, live(jnp.maximum(i - 1 - n, 0), carry))

            _, carry = lax.while_loop(cond, lambda c: (c[0] + 1, block(i - 1 - c[0], c[1], False)), (0, carry))
            return carry

        def fixed_reference(_):
            shift = fq - bound

            def probs(j, offset):
                return jnp.exp(_dot_nt(q, k_ref[0, j]) + (shift + offset) - fr_ref[0, j])

            def live(c):
                n, acc = c
                gate = gate_at_block_end(jnp.maximum(i - 1 - n, 0))
                return jnp.logical_and(n < i, jnp.max(fq - gate - jnp.log(acc[:, dh:dh + 1])) >= STOP_WIDE)

            def two_blocks(c):
                n, acc = c
                ja = i - 1 - n
                jb = i - 2 - n
                absent = jnp.where(jb >= 0, 0.0, NEG)
                jb = jnp.maximum(jb, 0)
                return n + 2, acc + (pv(probs(ja, 0.0), ja) + pv(probs(jb, absent), jb))

            acc = pv(jnp.where(col <= row, probs(i, 0.0), 0.0), i)
            _, acc = lax.while_loop(live, two_blocks, (0, acc))
            l = acc[:, dh:dh + 1]
            return acc / l, bound + jnp.log(l)

        def running_maximum(_):
            def block(j, carry, diag):
                m, acc = carry
                s = _dot_nt(q, k_ref[0, j]) + fq - fr_ref[0, j]
                if diag:
                    s = jnp.where(col <= row, s, NEG)
                m_new = jnp.maximum(m, jnp.max(s, axis=1, keepdims=True))
                return m_new, jnp.exp(m - m_new) * acc + pv(jnp.exp(s - m_new), j)

            def live(j, carry):
                return jnp.max(bound + fq - gate_at_block_end(j) - carry[0]) >= STOP

            m, acc = walk(block, live, (jnp.full((BQ, 1), NEG, F32), jnp.zeros((BQ, 2 * dh), F32)))
            l = acc[:, dh:dh + 1]
            return acc / l, m + jnp.log(l)

        o, lse = lax.cond(jnp.max(bound) < FIXED_REF_MAX, fixed_reference, running_maximum, 0)
        o_ref[0] = o
        lse_ref[0] = lse

        if gather is not None:
            @pl.when(last_step)
            def _():
                _chip_gather(src_ref, dst_ref, *refs[10:])[1]()

    _, _, qin2, kin2, _, _ = _head_specs(nj, 0)
    cspec = pl.BlockSpec((1, BQ, 1), lambda h, i: (h, i, 0))
    in_specs = [qin2, kin2, kin2, cspec, pl.BlockSpec((1, nj, 1, BQ), lambda h, i: (h, 0, 0, 0))]
    out_specs = [pl.BlockSpec((1, BQ, 2 * dh), lambda h, i: (h, i, 0)), cspec]
    out_shape = [jax.ShapeDtypeStruct((nh, t, 2 * dh), F32), jax.ShapeDtypeStruct((nh, t, 1), F32)]
    scratch = [pltpu.VMEM((8, 128), F32)]
    args = [qa, ka4, va4, fcol, frow4]
    if gather is not None:
        in_specs.append(_ANY)
        out_specs.append(_ANY)
        out_shape.append(jax.ShapeDtypeStruct((N_CHIPS,) + gather.shape, gather.dtype))
        scratch += _chip_sems()
        args.append(gather)
    return pl.pallas_call(body, grid=(nh, nj), in_specs=in_specs, out_specs=out_specs, out_shape=out_shape,
                          scratch_shapes=scratch, compiler_params=_params(), name=name)(*args)


def _other_half(x):
    return pltpu.roll(x.astype(F32), HEAD_DIM, 1).astype(BF16)


def fox_bwd(qa, ka4, va4, dox, fcol, frow4, o, lse, *, name, scatter=None):
    t = qa.shape[1]
    dh = HEAD_DIM
    nh = H_FOX
    nj = t // BQ
    n_in = 8

    def body(*refs):
        q_ref, k_ref, v_ref, dox_ref, fc_ref, fr_ref, o_ref, lse_ref = refs[:n_in]
        if scatter is None:
            dq_ref, dkv_ref, dfk_ref, kmax_ref = refs[n_in:]
        else:
            g_ref, dq_ref, dkv_ref, dfk_ref, land_ref, kmax_ref = refs[n_in:n_in + 6]
            first_step, last_step = _first_and_last_step()

            @pl.when(first_step)
            def _():
                _chip_scatter(g_ref, land_ref, *refs[n_in + 6:])[0]()

        i = pl.program_id(1)

        @pl.when(i == 0)
        def _():
            dkv_ref[...] = jnp.zeros_like(dkv_ref)
            dfk_ref[...] = jnp.zeros_like(dfk_ref)
            _store_kmax(k_ref, kmax_ref, nj)

        q = q_ref[0]
        fq = fc_ref[0]
        lse_q = lse_ref[0]
        do_x = dox_ref[0]
        dd = jnp.sum(do_x.astype(F32) * o_ref[0], axis=1, keepdims=True)
        rhs = jnp.concatenate([q, _other_half(do_x)], axis=0)
        edge = _qk_bound(q, kmax_ref) + fq - lse_q

        def negligible(j):
            return jnp.logical_and(j < i, jnp.max(edge - fr_ref[0, j][:, BQ - 1:BQ]) < STOP_WIDE)

        first = lax.while_loop(negligible, lambda j: j + 1, 0)

        shift = fq - lse_q

        def block(j, offset, diag):
            k = k_ref[0, j]
            p = jnp.exp(_dot_nt(q, k) + (shift + offset) - fr_ref[0, j])
            if diag:
                row, col = _causal_iota()
                p = jnp.where(col <= row, p, 0.0)
            ds = p * (_dot_nt(do_x, v_ref[0, j]) - dd)
            ds_b = ds.astype(BF16)
            dkv_ref[0, j] += _dot_tn(jnp.concatenate([ds_b, p.astype(BF16)], axis=0), rhs)
            dfk_ref[0, j] -= jnp.sum(ds, axis=0, keepdims=True)
            return _dot(ds_b, k)

        def two_blocks(n, dq):
            ja = first + 2 * n
            jb = ja + 1
            absent = jnp.where(jb < i, 0.0, NEG)
            jb = jnp.minimum(jb, i - 1)
            return dq + (block(ja, 0.0, False) + block(jb, absent, False))

        dq = lax.fori_loop(0, (i - first + 1) // 2, two_blocks, jnp.zeros((BQ, 2 * dh), F32))
        dq_ref[0] = dq + block(i, 0.0, True)

        if scatter is not None:
            @pl.when(last_step)
            def _():
                _chip_scatter(g_ref, land_ref, *refs[n_in + 6:])[1]()

    _, _, qin2, kin2, _, kspec2 = _head_specs(nj, 0)
    cspec = pl.BlockSpec((1, BQ, 1), lambda h, i: (h, i, 0))
    rspec = pl.BlockSpec((1, nj, 1, BQ), lambda h, i: (h, 0, 0, 0))
    wide = pl.BlockSpec((1, BQ, 2 * dh), lambda h, i: (h, i, 0))
    in_specs = [qin2, kin2, kin2, qin2, cspec, rspec, wide, cspec]
    out_specs = [wide, kspec2, rspec]
    out_shape = [jax.ShapeDtypeStruct((nh, t, 2 * dh), F32), jax.ShapeDtypeStruct((nh, nj, BQ, 2 * dh), F32),
                 jax.ShapeDtypeStruct((nh, nj, 1, BQ), F32)]
    scratch = [pltpu.VMEM((8, 128), F32)]
    args = [qa, ka4, va4, dox, fcol, frow4, o, lse]
    if scatter is not None:
        in_specs.append(_ANY)
        out_specs.append(_ANY)
        out_shape.append(jax.ShapeDtypeStruct(scatter.shape, scatter.dtype))
        scratch += _chip_sems()
        args.append(scatter)
    return pl.pallas_call(body, grid=(nh, nj), in_specs=in_specs, out_specs=out_specs, out_shape=out_shape,
                          scratch_shapes=scratch, compiler_params=_params(), name=name)(*args)


def _sb_logs(z, diag):
    e = jnp.exp(-jnp.abs(z))
    sp = jnp.log(1.0 + e)
    logb = jnp.minimum(z, 0.0) - sp
    lom = -jnp.maximum(z, 0.0) - sp
    strict = None
    if diag:
        row, col = _causal_iota()
        strict = col < row
        lom = jnp.where(strict, lom, 0.0)
    return logb, lom, e, strict


SB_GROUP = BQ // 2


def _sums_over_later_keys(lom, tri_m):
    halves = [lom[:, :SB_GROUP], lom[:, SB_GROUP:]]
    totals = [jnp.sum(x, axis=1, keepdims=True) for x in halves]
    within = []
    for x in halves:
        hi, lo = _split2(x)
        within.append(_dot(hi, tri_m) + _dot(lo, tri_m))
    return jnp.concatenate([within[0] + totals[1], within[1]], axis=1), totals[0] + totals[1]


def _sums_over_earlier_keys(da, tri_m):
    halves = [da[:, :SB_GROUP], da[:, SB_GROUP:]]
    totals = [jnp.sum(x, axis=1, keepdims=True) for x in halves]
    within = [_dot_nt(x.astype(BF16), tri_m) for x in halves]
    return jnp.concatenate([within[0], within[1] + totals[0]], axis=1), totals[0] + totals[1]


def sb_fwd(qa, ka4, va4, tri, *, name, gather=None):
    t = qa.shape[1]
    dh = HEAD_DIM
    nh = H_SB
    nj = t // BQ
    assert nj <= 128

    def body(*refs):
        if gather is None:
            q_ref, k_ref, v_ref, tri_ref, o_ref, rs_ref = refs
        else:
            q_ref, k_ref, v_ref, tri_ref, src_ref, o_ref, rs_ref, dst_ref = refs[:8]
            first_step, last_step = _first_and_last_step()

            @pl.when(first_step)
            def _():
                _chip_gather(src_ref, dst_ref, *refs[8:])[0]()

        i = pl.program_id(1)
        q = q_ref[0]
        tri_m = tri_ref[...]
        lane = lax.broadcasted_iota(jnp.int32, (BQ, 128), 1)

        def block(j, carry, diag):
            run, acc, rall = carry
            logb, lom, _, strict = _sb_logs(_dot_nt(q, k_ref[0, j]), diag)
            later, total = _sums_over_later_keys(lom, tri_m)
            w = jnp.exp(logb + later + run)
            if diag:
                w = jnp.where(strict, w, 0.0)
            acc = acc + _dot(w.astype(BF16), v_ref[0, j])
            rall = jnp.where(lane == j, run, rall)
            return run + total, acc, rall

        init = (jnp.zeros((BQ, 1), F32), jnp.zeros((BQ, 2 * dh), F32), jnp.full((BQ, 128), NEG, F32))
        carry = block(i, init, True)

        def cond(c):
            n, carry = c
            return jnp.logical_and(n < i, jnp.max(carry[0]) >= STOP)

        _, (_, acc, rall) = lax.while_loop(cond, lambda c: (c[0] + 1, block(i - 1 - c[0], c[1], False)), (0, carry))
        o_ref[0] = acc[:, :dh].astype(BF16)
        rs_ref[0] = rall

        if gather is not None:
            @pl.when(last_step)
            def _():
                _chip_gather(src_ref, dst_ref, *refs[8:])[1]()

    _, _, qin2, kin2, qspec, _ = _head_specs(nj, H_FOX)
    rspec = pl.BlockSpec((1, BQ, 128), lambda h, i: (h, i, 0))
    in_specs = [qin2, kin2, kin2, pl.BlockSpec((SB_GROUP, SB_GROUP), lambda h, i: (0, 0))]
    out_specs = [qspec, rspec]
    out_shape = [jax.ShapeDtypeStruct((nh, t, dh), BF16), jax.ShapeDtypeStruct((nh, t, 128), F32)]
    scratch = []
    args = [qa, ka4, va4, tri]
    if gather is not None:
        in_specs.append(_ANY)
        out_specs.append(_ANY)
        out_shape.append(jax.ShapeDtypeStruct((N_CHIPS,) + gather.shape, gather.dtype))
        scratch += _chip_sems()
        args.append(gather)
    return pl.pallas_call(body, grid=(nh, nj), in_specs=in_specs, out_specs=out_specs, out_shape=out_shape,
                          scratch_shapes=scratch, compiler_params=_params(), name=name)(*args)


def sb_bwd(qa, ka4, va4, dox, tri, rsave, *, name, scatter=None):
    t = qa.shape[1]
    dh = HEAD_DIM
    nh = H_SB
    nj = t // BQ
    n_in = 6

    def body(*refs):
        q_ref, k_ref, v_ref, dox_ref, tri_ref, rs_ref = refs[:n_in]
        if scatter is None:
            dq_ref, dkv_ref = refs[n_in:]
        else:
            g_ref, dq_ref, dkv_ref, land_ref = refs[n_in:n_in + 4]
            first_step, last_step = _first_and_last_step()

            @pl.when(first_step)
            def _():
                _chip_scatter(g_ref, land_ref, *refs[n_in + 4:])[0]()

        i = pl.program_id(1)

        @pl.when(i == 0)
        def _():
            dkv_ref[...] = jnp.zeros_like(dkv_ref)

        q = q_ref[0]
        do_x = dox_ref[0]
        tri_m = tri_ref[...]
        rall = rs_ref[0]
        lane = lax.broadcasted_iota(jnp.int32, (BQ, 128), 1)
        rhs = jnp.concatenate([q, _other_half(do_x)], axis=0)
        lane1 = lax.broadcasted_iota(jnp.int32, (1, 128), 1)
        unvisited = jnp.logical_and(lane1 < i, jnp.max(rall, axis=0, keepdims=True) < STOP)
        first = jnp.sum(unvisited.astype(jnp.int32))

        def block(j, carry, diag):
            dq, ecar = carry
            k = k_ref[0, j]
            z = _dot_nt(q, k)
            logb, lom, e, strict = _sb_logs(z, diag)
            run = jnp.sum(jnp.where(lane == j, rall, 0.0), axis=1, keepdims=True)
            w = jnp.exp(logb + _sums_over_later_keys(lom, tri_m)[0] + run)
            if diag:
                w = jnp.where(strict, w, 0.0)
            da = w * _dot_nt(do_x, v_ref[0, j])
            earlier, da_total = _sums_over_earlier_keys(da, tri_m)
            before = earlier + ecar
            inv = 1.0 / (1.0 + e)
            beta = jnp.where(z >= 0.0, 1.0, e) * inv
            one_minus = jnp.where(z >= 0.0, e, 1.0) * inv
            dz = da * one_minus - before * beta
            if diag:
                dz = jnp.where(strict, dz, 0.0)
            dz_b = dz.astype(BF16)
            dkv_ref[0, j] += _dot_tn(jnp.concatenate([dz_b, w.astype(BF16)], axis=0), rhs)
            return dq + _dot(dz_b, k), ecar + da_total

        carry = lax.fori_loop(first, i, lambda j, c: block(j, c, False),
                              (jnp.zeros((BQ, 2 * dh), F32), jnp.zeros((BQ, 1), F32)))
        dq, _ = block(i, carry, True)
        dq_ref[0] = dq

        if scatter is not None:
            @pl.when(last_step)
            def _():
                _chip_scatter(g_ref, land_ref, *refs[n_in + 4:])[1]()

    _, _, qin2, kin2, _, kspec2 = _head_specs(nj, H_FOX)
    in_specs = [qin2, kin2, kin2, qin2, pl.BlockSpec((SB_GROUP, SB_GROUP), lambda h, i: (0, 0)),
                pl.BlockSpec((1, BQ, 128), lambda h, i: (h, i, 0))]
    out_specs = [pl.BlockSpec((1, BQ, 2 * dh), lambda h, i: (h, i, 0)), kspec2]
    out_shape = [jax.ShapeDtypeStruct((nh, t, 2 * dh), F32), jax.ShapeDtypeStruct((nh, nj, BQ, 2 * dh), F32)]
    scratch = []
    args = [qa, ka4, va4, dox, tri, rsave]
    if scatter is not None:
        in_specs.append(_ANY)
        out_specs.append(_ANY)
        out_shape.append(jax.ShapeDtypeStruct(scatter.shape, scatter.dtype))
        scratch += _chip_sems()
        args.append(scatter)
    return pl.pallas_call(body, grid=(nh, nj), in_specs=in_specs, out_specs=out_specs, out_shape=out_shape,
                          scratch_shapes=scratch, compiler_params=_params(), name=name)(*args)


def loss_head(y, target, *, name, tm=512):
    t, d = y.shape

    def body(y_ref, t_ref, l_ref, dy_ref, acc_ref):
        i = pl.program_id(0)
        diff = y_ref[...] - t_ref[...]
        dy_ref[...] = diff * (1.0 / d)
        part = jnp.sum(diff * diff, axis=0, keepdims=True)

        @pl.when(i == 0)
        def _():
            acc_ref[...] = part

        @pl.when(i > 0)
        def _():
            acc_ref[...] += part

        @pl.when(i == pl.num_programs(0) - 1)
        def _():
            l_ref[...] = jnp.full(l_ref.shape, (0.5 / d) * jnp.sum(acc_ref[...]), F32)

    row = pl.BlockSpec((tm, d), lambda i: (i, 0))
    return pl.pallas_call(
        body, grid=(t // tm,), in_specs=[row, row],
        out_specs=[pl.BlockSpec((8, 128), lambda i: (0, 0)), row],
        out_shape=[jax.ShapeDtypeStruct((8, 128), F32), jax.ShapeDtypeStruct((t, d), F32)],
        scratch_shapes=[pltpu.VMEM((1, d), F32)], compiler_params=_params(), name=name)(y, target)


def _from_heads(a):
    t = a.shape[1]
    return a.transpose(1, 0, 2).reshape(t, MIX)


def _lanes_to_chunks(a):
    r, t = a.shape
    return a.reshape(r, t // 128, 128).transpose(1, 0, 2)


def _chunks_to_lanes(a):
    nc, r, _ = a.shape
    return a.transpose(1, 0, 2).reshape(r, nc * 128)


def _constants():
    idx = jnp.arange(128)
    bd = (idx[:, None] // HEAD_DIM == idx[None, :] // HEAD_DIM).astype(BF16)
    tri_le = (idx[:, None] <= idx[None, :]).astype(BF16)
    tri_ge = (idx[:, None] >= idx[None, :]).astype(BF16)
    jdx = jnp.arange(SB_GROUP)
    tri_gt = (jdx[:, None] > jdx[None, :]).astype(BF16)
    return dict(bd=bd, tri_le=tri_le, tri_ge=tri_ge, tri_gt=tri_gt)


def attn_layer_fwd(h, w, cst, gather=None):
    t = h.shape[0]
    nj = t // BQ
    xn, proj = rms_mm_nn(h, w["norm"], w["w_in"], tn=640, name="attn_in_proj")
    qa, ka, va, logf = attn_prep_fwd(proj, w["gq"], w["gk"], w["fbias"], cst["bd"], name="attn_prep_fwd")
    logf3 = _lanes_to_chunks(logf)
    cum = _chunks_to_lanes(gate_cumsum(logf3, cst["tri_le"], name="gate_cumsum"))
    fcol = cum.reshape(H_FOX, t, 1)
    frow4 = cum.reshape(H_FOX, nj, 1, BQ)
    ka4 = ka.reshape(N_HEADS, nj, BQ, 2 * HEAD_DIM)
    va4 = va.reshape(N_HEADS, nj, BQ, 2 * HEAD_DIM)
    if gather is None:
        (o_f, lse), (o_s, rsave), gathered = (fox_fwd(qa, ka4, va4, fcol, frow4, name="fox_fwd"),
                                              sb_fwd(qa, ka4, va4, cst["tri_gt"], name="sb_fwd"), None)
    else:
        o_f, lse, gathered_a = fox_fwd(qa, ka4, va4, fcol, frow4, name="fox_fwd_gather", gather=gather[0])
        o_s, rsave, gathered_b = sb_fwd(qa, ka4, va4, cst["tri_gt"], name="sb_fwd_gather", gather=gather[1])
        gathered = gather[2]((gathered_a, gathered_b))
        w = gathered[0][0]
    o = _from_heads(jnp.concatenate([o_f[:, :, :HEAD_DIM].astype(BF16), o_s], axis=0))
    h2 = mm_nn(o, w["w_out"], add=h, name="mix_out_proj")
    saved = dict(h=h, xn=xn, proj=proj, logf3=logf3, fcol=fcol, frow4=frow4, qa=qa, ka4=ka4, va4=va4,
                 o_f=o_f, lse=lse, rsave=rsave, o=o)
    return h2, saved, gathered


def attn_layer_bwd(dh, w, s, cst, scatter=None):
    t = dh.shape[0]
    dh3 = dh[None]
    w_out_heads = jnp.pad(w["w_out"].reshape(N_HEADS, HEAD_DIM, -1), ((0, 0), (0, HEAD_DIM), (0, 0)))
    dox = mm_nt_heads(dh, w_out_heads, name="mix_out_bwd_heads")
    g_w_out = mm_tn(s["o"], dh3, name="mix_out_wgrad")
    fox_args = (s["qa"], s["ka4"], s["va4"], dox, s["fcol"], s["frow4"], s["o_f"], s["lse"])
    sb_args = (s["qa"], s["ka4"], s["va4"], dox, cst["tri_gt"], s["rsave"])
    if scatter is None:
        (dq_f, dkv_f, dfk), (dq_s, dkv_s), landed = fox_bwd(*fox_args, name="fox_bwd"), sb_bwd(*sb_args, name="sb_bwd"), None
    else:
        chunks_a, chunks_b = scatter(g_w_out)
        dq_f, dkv_f, dfk, landed_a = fox_bwd(*fox_args, name="fox_bwd_scatter", scatter=chunks_a)
        dq_s, dkv_s, landed_b = sb_bwd(*sb_args, name="sb_bwd_scatter", scatter=chunks_b)
        landed = (landed_a, landed_b)
    dcum3 = _lanes_to_chunks(dfk.reshape(H_FOX, t))
    dfl3, dbias = gate_cumsum_bwd(dcum3, s["logf3"], cst["tri_ge"], name="gate_cumsum_bwd")
    dfl = jnp.pad(_chunks_to_lanes(dfl3).T, ((0, 0), (0, 128 - H_FOX))).astype(BF16)
    wide = (H_FOX, t, 2 * HEAD_DIM)
    dproj, dgq, dgk = attn_prep_bwd(s["proj"], dq_f, dq_s, dkv_f.reshape(wide), dkv_s.reshape(wide), dfl, w["gq"],
                                    w["gk"], cst["bd"], name="attn_prep_bwd")
    g_w_in = mm_tn(s["xn"], dproj[None], tn=640, name="attn_in_wgrad")[:, :ATTN_IN]
    dh2, g_norm = mm_nt_rms_bwd(dproj[None], w["w_in"], s["h"], w["norm"], dh, name="attn_in_bwd")
    dgq = dgq.reshape(N_HEADS, HEAD_DIM)
    dgk = dgk.reshape(N_HEADS, HEAD_DIM)
    grads = dict(norm=g_norm[0], w_in=g_w_in, f_bias=dbias[:, 0], fox_q=dgq[:H_FOX].sum(0), fox_k=dgk[:H_FOX].sum(0),
                 sb_q=dgq[H_FOX:].sum(0), sb_k=dgk[H_FOX:].sum(0), w_out=g_w_out)
    return dh2, grads, landed


def conv_layer_fwd(h, w):
    xn, proj3 = rms_mm_nn(h, w["norm"], w["w_in"], parts=3, name="conv_in_proj")
    h2, y = conv_mix_out_fwd(proj3, w["ck"], w["w_out"], h, name="conv_mix_out_fwd")
    return h2, dict(h=h, xn=xn, proj3=proj3, y=y)


def conv_layer_bwd(dh, w, s):
    dh3 = dh[None]
    g_w_out = mm_tn(s["y"], dh3, name="mix_out_wgrad")
    dproj3, dck = conv_mix_bwd(dh, w["w_out"], s["proj3"], w["ck"], name="conv_mix_bwd")
    g_w_in = mm_tn(s["xn"], dproj3, name="conv_in_wgrad")
    dh2, g_norm = mm_nt_rms_bwd(dproj3, w["w_in"], s["h"], w["norm"], dh, name="conv_in_bwd")
    return dh2, dict(norm=g_norm[0], w_in=g_w_in, ck=dck[0, :3], w_out=g_w_out)


def ffn_layer_fwd(h, w):
    xn, up2 = rms_mm_nn(h, w["norm"], w["w_up"], parts=2, tn=1408, name="ffn_up_proj")
    h2, act = ffn_act_down_fwd(up2, w["cw2"], w["w_down"], h, name="ffn_act_down_fwd")
    return h2, dict(h=h, xn=xn, up2=up2, act=act)


def ffn_layer_bwd(dh, w, s):
    dh3 = dh[None]
    g_w_down = mm_tn(s["act"], dh3, tk=1408, name="ffn_down_wgrad")
    dup2, dcw = ffn_act_bwd(dh, w["w_down"], s["up2"], w["cw2"], name="ffn_act_bwd")
    g_w_up = mm_tn(s["xn"], dup2, tn=1408, name="ffn_up_wgrad")
    dh2, g_norm = mm_nt_rms_bwd(dup2, w["w_up"], s["h"], w["norm"], dh, name="ffn_up_bwd")
    g_cw = jnp.concatenate([dcw[0, :3], dcw[1, :3]], axis=1)
    return dh2, dict(norm=g_norm[0], w_up=g_w_up, cw=g_cw, w_down=g_w_down)


def forward_backward(x, target, wa, wc, wf, *, late_weights=None, late_chunks=None):
    cst = _constants()
    h = x
    saved = []
    layer = 0
    while layer == 0 or layer < len(wf):
        i = layer // 2
        if layer % 2 == 0:
            h, sm, built = attn_layer_fwd(h, wa[i], cst, gather=late_weights if late_weights and layer == 0 else None)
            if built is not None:
                wa, wc, wf = built
        else:
            h, sm = conv_layer_fwd(h, wc[i])
        h, sf = ffn_layer_fwd(h, wf[layer])
        saved.append((sm, sf))
        layer += 1
    depth = len(wf)
    loss_blk, dh = loss_head(h, target, name="loss_head")
    ga, gc, gf = [None] * len(wa), [None] * len(wc), [None] * depth
    landed = None
    for layer in reversed(range(depth)):
        i = layer // 2
        sm, sf = saved[layer]
        dh, gf[layer] = ffn_layer_bwd(dh, wf[layer], sf)
        if layer % 2 == 0:
            chunks = None
            if late_chunks and layer == 0:
                chunks = lambda g_w_out: late_chunks([dict(w_out=g_w_out)] + ga[1:], gc, gf)
            dh, ga[i], got = attn_layer_bwd(dh, wa[i], sm, cst, scatter=chunks)
            landed = got if got is not None else landed
        else:
            dh, gc[i] = conv_layer_bwd(dh, wc[i], sm)
    return loss_blk, dh, ga, gc, gf, landed


def _part_rows(shape, width, row_mult):
    n = 1
    for s in shape:
        n *= s
    rows = -(-n // width)
    return -(-rows // row_mult) * row_mult


def _pack_rows(arrs, width, row_mult, dtype, total_rows=None, lead=0):
    parts = []
    used = 0
    for a in arrs:
        outer = a.shape[:lead]
        rows = _part_rows(a.shape[lead:], width, row_mult)
        flat = a.astype(dtype).reshape(outer + (-1,))
        flat = jnp.pad(flat, ((0, 0),) * lead + ((0, rows * width - flat.shape[-1]),))
        parts.append(flat.reshape(outer + (rows, width)))
        used += rows
    if total_rows is not None and total_rows > used:
        parts.append(jnp.zeros(parts[0].shape[:lead] + (total_rows - used, width), dtype))
    return jnp.concatenate(parts, axis=lead)


def _unpack_rows(packed, shapes, width, row_mult, lead=0):
    outer = packed.shape[:lead]
    out = []
    off = 0
    for shape in shapes:
        rows = _part_rows(shape, width, row_mult)
        n = 1
        for s in shape:
            n *= s
        flat = lax.slice_in_dim(packed, off, off + rows, axis=lead).reshape(outer + (-1,))
        out.append(lax.slice_in_dim(flat, 0, n, axis=lead).reshape(outer + tuple(shape)))
        off += rows
    return out


BIG_NAMES = ("attn_w_in", "attn_w_out", "conv_w_in", "conv_w_out", "ffn_w_up", "ffn_w_down")
BIG_AXIS = {"attn_w_in": 2, "attn_w_out": 1, "conv_w_in": 2, "conv_w_out": 1, "ffn_w_up": 2, "ffn_w_down": 1}
BIG_WIDTH = 1024
BIG_ROW_MULT = 16
BIG_TILE = 512
SMALL_TILE = 128
SMALL_SHARDED = ("conv_norm", "conv_kernel", "ffn_conv")
SMALL_AXIS = {"conv_norm": 1, "conv_kernel": 2, "ffn_conv": 2}
SMALL_REPLICATED = ("attn_norm", "attn_f_bias", "fox_q_gain", "fox_k_gain", "sb_q_gain", "sb_k_gain", "ffn_norm")
WEIGHT_ORDER = ("attn_norm", "attn_w_in", "attn_f_bias", "fox_q_gain", "fox_k_gain", "sb_q_gain", "sb_k_gain",
                "attn_w_out", "conv_norm", "conv_w_in", "conv_kernel", "conv_w_out", "ffn_norm", "ffn_w_up",
                "ffn_conv", "ffn_w_down")


def _big_total_rows(shapes):
    used = sum(_part_rows(s, BIG_WIDTH, BIG_ROW_MULT) for s in shapes)
    tile = BIG_TILE if used >= 8 * BIG_TILE else SMALL_TILE
    return -(-used // tile) * tile


def _place():
    x, y, c = lax.axis_index("x"), lax.axis_index("y"), lax.axis_index("c")
    other_chips = [(1 - x, y), (x, 1 - y), (1 - x, 1 - y)]
    return x, y, c, other_chips


_ANY = pl.BlockSpec(memory_space=pl.ANY)


def _chip_sems():
    return [pltpu.SemaphoreType.DMA((3,)), pltpu.SemaphoreType.DMA((3,)), pltpu.SemaphoreType.DMA]


def _chip_gather(src_ref, dst_ref, send_sems, recv_sems, local_sem):
    x, y, c, chips = _place()
    k = 2 * x + y

    def copy(j, slot):
        px, py = chips[j]
        return pltpu.make_async_remote_copy(src_ref=src_ref, dst_ref=dst_ref.at[slot], send_sem=send_sems.at[j],
                                            recv_sem=recv_sems.at[j], device_id=(px, py, c), device_id_type=MESH)

    def local():
        return pltpu.make_async_copy(src_ref, dst_ref.at[k], local_sem)

    def start():
        local().start()
        for j in range(3):
            copy(j, k).start()

    def finish():
        for j, (px, py) in enumerate(chips):
            copy(j, 2 * px + py).wait_recv()
        for j in range(3):
            copy(j, k).wait_send()
        local().wait()

    return start, finish


def _chip_scatter(g_ref, o_ref, send_sems, recv_sems, local_sem):
    x, y, c, chips = _place()
    k = 2 * x + y

    def copy(j, src_slot, dst_slot):
        px, py = chips[j]
        return pltpu.make_async_remote_copy(src_ref=g_ref.at[src_slot], dst_ref=o_ref.at[dst_slot],
                                            send_sem=send_sems.at[j], recv_sem=recv_sems.at[j],
                                            device_id=(px, py, c), device_id_type=MESH)

    def local():
        return pltpu.make_async_copy(g_ref.at[k], o_ref.at[k], local_sem)

    def start():
        local().start()
        for j, (px, py) in enumerate(chips):
            copy(j, 2 * px + py, k).start()

    def finish():
        for j, (px, py) in enumerate(chips):
            copy(j, k, 2 * px + py).wait_recv()
        for j, (px, py) in enumerate(chips):
            copy(j, 2 * px + py, k).wait_send()
        local().wait()

    return start, finish


def gather_chips(arrs, *, name):
    n = len(arrs)

    def body(*refs):
        hooks = [_chip_gather(refs[m], refs[n + m], *refs[2 * n + 3 * m:2 * n + 3 * m + 3]) for m in range(n)]
        for start, _ in hooks:
            start()
        for _, finish in hooks:
            finish()

    return pl.pallas_call(
        body, in_specs=[_ANY] * n, out_specs=[_ANY] * n,
        out_shape=[jax.ShapeDtypeStruct((N_CHIPS,) + a.shape, a.dtype) for a in arrs],
        scratch_shapes=_chip_sems() * n, name=name)(*arrs)


def scatter_chips(chunks, *, name):
    def body(g_ref, o_ref, send_sems, recv_sems, local_sem):
        start, finish = _chip_scatter(g_ref, o_ref, send_sems, recv_sems, local_sem)
        start()
        finish()

    return pl.pallas_call(
        body, in_specs=[_ANY], out_specs=_ANY, out_shape=jax.ShapeDtypeStruct(chunks.shape, chunks.dtype),
        scratch_shapes=_chip_sems(), name=name)(chunks)


def swap_cores(arrs, *, name):
    n = len(arrs)

    def body(*refs):
        x, y, c, _ = _place()
        copies = [pltpu.make_async_remote_copy(src_ref=refs[m], dst_ref=refs[n + m], send_sem=refs[2 * n + 2 * m],
                                               recv_sem=refs[2 * n + 2 * m + 1], device_id=(x, y, 1 - c),
                                               device_id_type=MESH) for m in range(n)]
        for cp in copies:
            cp.start()
        for cp in copies:
            cp.wait()

    return pl.pallas_call(
        body, in_specs=[_ANY] * n, out_specs=[_ANY] * n,
        out_shape=[jax.ShapeDtypeStruct(a.shape, a.dtype) for a in arrs],
        scratch_shapes=[pltpu.SemaphoreType.DMA, pltpu.SemaphoreType.DMA] * n, name=name)(*arrs)


def allreduce_small(p, *, name):
    r, w = p.shape

    def body(p_ref, o_ref, buf, send_sems, recv_sems):
        x, y, c, _ = _place()
        me = 4 * x + 2 * y + c
        buf[me] = p_ref[...]

        def peer_of(m):
            return (1 - x if m & 4 else x, 1 - y if m & 2 else y, 1 - c if m & 1 else c)

        def copy(m, slot):
            return pltpu.make_async_remote_copy(src_ref=p_ref, dst_ref=buf.at[slot], send_sem=send_sems.at[m - 1],
                                                recv_sem=recv_sems.at[m - 1], device_id=peer_of(m),
                                                device_id_type=MESH)

        sends = [copy(m, me) for m in range(1, 8)]
        for cp in sends:
            cp.start()
        for m in range(1, 8):
            px, py, pc = peer_of(m)
            copy(m, 4 * px + 2 * py + pc).wait_recv()
        for cp in sends:
            cp.wait_send()
        acc = buf[0]
        for d in range(1, 8):
            acc = acc + buf[d]
        o_ref[...] = acc

    vm = pl.BlockSpec(memory_space=pltpu.VMEM)
    return pl.pallas_call(
        body, in_specs=[vm], out_specs=vm, out_shape=jax.ShapeDtypeStruct((r, w), F32),
        scratch_shapes=[pltpu.VMEM((8, r, w), F32), pltpu.SemaphoreType.DMA((7,)), pltpu.SemaphoreType.DMA((7,))],
        name=name)(p)


def sum_chips(rv, *, name):
    _, r, w = rv.shape
    tile = BIG_TILE if r % BIG_TILE == 0 else SMALL_TILE

    def body(a_ref, b_ref, c_ref, d_ref, o_ref):
        o_ref[...] = ((a_ref[0].astype(F32) + b_ref[0].astype(F32)) + c_ref[0].astype(F32)) + d_ref[0].astype(F32)

    spec = lambda kk: pl.BlockSpec((1, tile, w), lambda i: (kk, i, 0))
    return pl.pallas_call(
        body, grid=(r // tile,), in_specs=[spec(0), spec(1), spec(2), spec(3)],
        out_specs=pl.BlockSpec((tile, w), lambda i: (i, 0)), out_shape=jax.ShapeDtypeStruct((r, w), F32),
        compiler_params=_params(), name=name)(rv, rv, rv, rv)


def add_pair(a, b, *, name):
    r, w = a.shape
    tile = BIG_TILE if r % BIG_TILE == 0 else SMALL_TILE

    def body(a_ref, b_ref, o_ref):
        o_ref[...] = a_ref[...] + b_ref[...]

    spec = pl.BlockSpec((tile, w), lambda i: (i, 0))
    return pl.pallas_call(body, grid=(r // tile,), in_specs=[spec, spec], out_specs=spec,
                          out_shape=jax.ShapeDtypeStruct((r, w), F32), compiler_params=_params(), name=name)(a, b)


def adamw(w, g, m, v, *, tm, name):
    r, c = w.shape
    assert r % tm == 0

    def body(w_ref, g_ref, m_ref, v_ref, d_ref, nm_ref, nv_ref):
        g_ = g_ref[...]
        m_ = ADAM_B1 * m_ref[...] + (1.0 - ADAM_B1) * g_
        v_ = ADAM_B2 * v_ref[...] + (1.0 - ADAM_B2) * (g_ * g_)
        m_hat = m_ / (1.0 - ADAM_B1 ** ADAM_STEP)
        v_hat = v_ / (1.0 - ADAM_B2 ** ADAM_STEP)
        d_ref[...] = -ADAM_LR * (m_hat / (jnp.sqrt(v_hat) + ADAM_EPS) + ADAM_WD * w_ref[...])
        nm_ref[...] = m_
        nv_ref[...] = v_

    spec = pl.BlockSpec((tm, c), lambda i: (i, 0))
    return pl.pallas_call(body, grid=(r // tm,), in_specs=[spec] * 4, out_specs=[spec] * 3,
                          out_shape=[jax.ShapeDtypeStruct((r, c), F32)] * 3, compiler_params=_params(), name=name)(w, g, m, v)


def kernel(x, attn_norm, attn_w_in, attn_f_bias, fox_q_gain, fox_k_gain, sb_q_gain, sb_k_gain, attn_w_out, conv_norm, conv_w_in, conv_kernel, conv_w_out, ffn_norm, ffn_w_up, ffn_conv, ffn_w_down, loss_target, m_attn_norm, m_attn_w_in, m_attn_f_bias, m_fox_q_gain, m_fox_k_gain, m_sb_q_gain, m_sb_k_gain, m_attn_w_out, m_conv_norm, m_conv_w_in, m_conv_kernel, m_conv_w_out, m_ffn_norm, m_ffn_w_up, m_ffn_conv, m_ffn_w_down, v_attn_norm, v_attn_w_in, v_attn_f_bias, v_fox_q_gain, v_fox_k_gain, v_sb_q_gain, v_sb_k_gain, v_attn_w_out, v_conv_norm, v_conv_w_in, v_conv_kernel, v_conv_w_out, v_ffn_norm, v_ffn_w_up, v_ffn_conv, v_ffn_w_down):
    a = dict(locals())
    chip = 2 * lax.axis_index("x") + lax.axis_index("y")
    n_attn, n_conv, depth = attn_norm.shape[0], conv_norm.shape[0], ffn_norm.shape[0]

    units = [(name, l) for name in BIG_NAMES for l in range(a[name].shape[0])]
    early = [("attn_w_in", 0)]
    late = [u for u in units if u not in early]
    late_b = [("attn_w_out", 0), ("conv_w_in", n_conv - 1), ("conv_w_out", n_conv - 1), ("ffn_w_up", depth - 1),
              ("ffn_w_down", depth - 1)]
    late_a = [u for u in late if u not in late_b]
    late_sb = [("ffn_w_up", 0), ("ffn_w_down", 0), ("ffn_w_up", 1), ("ffn_w_down", 1)]
    late_sa = [u for u in late if u not in late_sb]

    def unit_shape(u):
        return a[u[0]].shape[1:]

    def pack_units(us, get, lead=0):
        return _pack_rows([get(u) for u in us], BIG_WIDTH, BIG_ROW_MULT, BF16, _big_total_rows([unit_shape(u) for u in us]),
                          lead=lead)

    def unpack_units(packed, us, lead=0):
        return dict(zip(us, _unpack_rows(packed, [unit_shape(u) for u in us], BIG_WIDTH, BIG_ROW_MULT, lead=lead)))

    def full_units(gathered, us):
        full_size = {}
        for u, g4 in unpack_units(gathered, us, lead=1).items():
            _, rows, cols = g4.shape
            if BIG_AXIS[u[0]] - 1 == 0:
                full_size[u] = g4.reshape(N_CHIPS * rows, cols)
            else:
                full_size[u] = g4.transpose(1, 0, 2).reshape(rows, N_CHIPS * cols)
        return full_size

    def shard(u):
        return a[u[0]][u[1]]

    small_shapes = [a[n].shape for n in SMALL_SHARDED]
    packed_s = _pack_rows([a[n] for n in SMALL_SHARDED], 128, 8, F32)
    gath_e, gath_s = gather_chips([pack_units(early, shard), packed_s], name="gather_weights")
    full_e = full_units(gath_e, early)
    full = {}
    per_chip = [_unpack_rows(gath_s[kk], small_shapes, 128, 8) for kk in range(N_CHIPS)]
    for n, name in enumerate(SMALL_SHARDED):
        full[name] = jnp.concatenate([per_chip[kk][n] for kk in range(N_CHIPS)], axis=SMALL_AXIS[name])

    def attn_weights(i, fu):
        return dict(
            norm=attn_norm[i][None],
            w_in=jnp.pad(fu[("attn_w_in", i)], ((0, 0), (0, ATTN_IN_PAD - ATTN_IN))),
            fbias=jnp.pad(attn_f_bias[i], (0, 128 - H_FOX))[None],
            gq=jnp.concatenate([jnp.tile(fox_q_gain[i], H_FOX), jnp.tile(sb_q_gain[i], H_SB)])[None],
            gk=jnp.concatenate([jnp.tile(fox_k_gain[i], H_FOX), jnp.tile(sb_k_gain[i], H_SB)])[None],
            w_out=fu.get(("attn_w_out", i)))

    def build_weights(gathered):
        fu = {**full_e, **full_units(gathered[0], late_a), **full_units(gathered[1], late_b)}
        wa = [attn_weights(i, fu) for i in range(n_attn)]
        wc = [dict(norm=full["conv_norm"][i][None], w_in=fu[("conv_w_in", i)], ck=full["conv_kernel"][i][None],
                   w_out=fu[("conv_w_out", i)]) for i in range(n_conv)]
        wf = []
        for l in range(depth):
            cw = full["ffn_conv"][l]
            wf.append(dict(norm=ffn_norm[l][None], w_up=fu[("ffn_w_up", l)], cw2=jnp.stack([cw[:, :D_FF], cw[:, D_FF:]]),
                           w_down=fu[("ffn_w_down", l)]))
        return wa, wc, wf

    def chunks_of_unit(u, ga, gc, gf):
        name, l = u
        g = {"attn_w_in": lambda: ga[l]["w_in"], "attn_w_out": lambda: ga[l]["w_out"],
             "conv_w_in": lambda: gc[l]["w_in"], "conv_w_out": lambda: gc[l]["w_out"],
             "ffn_w_up": lambda: gf[l]["w_up"], "ffn_w_down": lambda: gf[l]["w_down"]}[name]()
        rows, cols = unit_shape(u)
        if BIG_AXIS[name] - 1 == 0:
            return g.reshape(N_CHIPS, rows, cols)
        return g.reshape(rows, N_CHIPS, cols).transpose(1, 0, 2)

    def chunks_of(us, ga, gc, gf):
        return pack_units(us, lambda u: chunks_of_unit(u, ga, gc, gf), lead=1)

    loss_blk, grad_x, ga, gc, gf, landed_late = forward_backward(
        x[0], loss_target[0], [attn_weights(0, full_e)], [], [],
        late_weights=(pack_units(late_a, shard), pack_units(late_b, shard), build_weights),
        late_chunks=lambda ga, gc, gf: (chunks_of(late_sa, ga, gc, gf), chunks_of(late_sb, ga, gc, gf)))

    landed = [scatter_chips(chunks_of(early, ga, gc, gf), name="scatter_grads"), landed_late[0], landed_late[1]]
    mine = [sum_chips(buf, name="sum_chips") for buf in landed]
    theirs = swap_cores(mine, name="swap_cores")
    g_units = {}
    for us, m, th in zip((early, late_sa, late_sb), mine, theirs):
        g_units.update(unpack_units(add_pair(m, th, name="add_cores"), us))
    grads = {name: jnp.stack([g_units[(name, l)] for l in range(a[name].shape[0])]) for name in BIG_NAMES}

    small_full = [
        loss_blk,
        jnp.stack([g["norm"] for g in ga]), jnp.stack([g["f_bias"] for g in ga]),
        jnp.stack([g["fox_q"] for g in ga]), jnp.stack([g["fox_k"] for g in ga]),
        jnp.stack([g["sb_q"] for g in ga]), jnp.stack([g["sb_k"] for g in ga]),
        jnp.stack([g["norm"] for g in gf]),
        jnp.stack([g["norm"] for g in gc]), jnp.stack([g["ck"] for g in gc]), jnp.stack([g["cw"] for g in gf]),
    ]
    summed = allreduce_small(_pack_rows(small_full, 128, 8, F32), name="allreduce_small")
    parts = _unpack_rows(summed, [p.shape for p in small_full], 128, 8)
    loss = parts[0][0, 0]
    for name, g in zip(SMALL_REPLICATED, parts[1:8]):
        grads[name] = g
    for name, g in zip(SMALL_SHARDED, parts[8:]):
        width = a[name].shape[SMALL_AXIS[name]]
        grads[name] = lax.dynamic_slice_in_dim(g, chip * width, width, axis=SMALL_AXIS[name])

    delta, new_m, new_v = {}, {}, {}
    for name in BIG_NAMES:
        shape = a[name].shape
        flat = lambda arr: arr.reshape(-1, shape[-1])
        d_, m_, v_ = adamw(flat(a[name]), flat(grads[name]), flat(a["m_" + name]), flat(a["v_" + name]), tm=256,
                           name="adamw")
        delta[name], new_m[name], new_v[name] = d_.reshape(shape), m_.reshape(shape), v_.reshape(shape)
    small_names = SMALL_REPLICATED + SMALL_SHARDED
    small_shapes_local = [a[n].shape for n in small_names]
    pack = lambda prefix, src: _pack_rows([src[prefix + n] for n in small_names], 128, 8, F32)
    packed = adamw(pack("", a), pack("", grads), pack("m_", a), pack("v_", a), tm=8, name="adamw_small")
    for store, buf in zip((delta, new_m, new_v), packed):
        for name, arr in zip(small_names, _unpack_rows(buf, small_shapes_local, 128, 8)):
            store[name] = arr

    return (loss, grad_x[None], *[grads[n] for n in WEIGHT_ORDER], *[delta[n] for n in WEIGHT_ORDER],
            *[new_m[n] for n in WEIGHT_ORDER], *[new_v[n] for n in WEIGHT_ORDER])
```

```python
import functools

import jax
import jax.numpy as jnp
from jax import lax
from jax.experimental import pallas as pl
from jax.experimental.pallas import tpu as pltpu

F32 = jnp.float32
BF16 = jnp.bfloat16

D_MODEL = 1024
HEAD_DIM = 64
H_FOX = 8
H_SB = 8
N_HEADS = H_FOX + H_SB
MIX = N_HEADS * HEAD_DIM
ATTN_IN = 3 * MIX + H_FOX
ATTN_IN_PAD = 3 * MIX + 128
D_FF = 2816
EPS = 1e-6
SCALE = HEAD_DIM ** -0.5
NEG = -1e30

ADAM_LR = 0.001
ADAM_B1 = 0.9
ADAM_B2 = 0.999
ADAM_EPS = 1e-08
ADAM_WD = 0.01
ADAM_STEP = 10

VMEM_LIMIT = 56 * 1024 * 1024
HALO = 8
BQ = 512
N_CHIPS = 4
MESH = pl.DeviceIdType.MESH


def _params(**kw):
    return pltpu.CompilerParams(vmem_limit_bytes=VMEM_LIMIT, **kw)


def _dot(a, b):
    return jnp.dot(a, b, preferred_element_type=F32)


def _dot_nt(a, b):
    return lax.dot_general(a, b, (((1,), (1,)), ((), ())), preferred_element_type=F32)


def _dot_tn(a, b):
    return lax.dot_general(a, b, (((0,), (0,)), ((), ())), preferred_element_type=F32)


def _split2(x):
    hi = x.astype(BF16)
    lo = (x - hi.astype(F32)).astype(BF16)
    return hi, lo


def _split3(x):
    hi = x.astype(BF16)
    r = x - hi.astype(F32)
    mid = r.astype(BF16)
    lo = (r - mid.astype(F32)).astype(BF16)
    return hi, mid, lo


def mm_nn(a, b, *, add=None, out_dtype=F32, parts=1, tm=1024, tn=512, name):
    m, k = a.shape
    n = b.shape[1]
    np_ = n // parts
    nb = np_ // tn
    tm = min(tm, m)
    assert m % tm == 0 and np_ % tn == 0

    def body(*refs):
        if add is None:
            a_ref, b_ref, o_ref = refs
            acc = _dot(a_ref[...].astype(BF16), b_ref[...])
        else:
            a_ref, b_ref, r_ref, o_ref = refs
            acc = _dot(a_ref[...].astype(BF16), b_ref[...]) + r_ref[...]
        o_ref[...] = acc.astype(out_dtype).reshape(o_ref.shape)

    in_specs = [pl.BlockSpec((tm, k), lambda i, j: (i, 0)), pl.BlockSpec((k, tn), lambda i, j: (0, j))]
    args = [a, b]
    if add is not None:
        in_specs.append(pl.BlockSpec((tm, tn), lambda i, j: (i, j)))
        args.append(add)
    if parts == 1:
        out_spec = pl.BlockSpec((tm, tn), lambda i, j: (i, j))
        out_shape = jax.ShapeDtypeStruct((m, n), out_dtype)
    else:
        out_spec = pl.BlockSpec((1, tm, tn), lambda i, j: (j // nb, i, j % nb))
        out_shape = jax.ShapeDtypeStruct((parts, m, np_), out_dtype)
    return pl.pallas_call(body, grid=(m // tm, n // tn), in_specs=in_specs, out_specs=out_spec,
                          out_shape=out_shape, compiler_params=_params(), name=name)(*args)


def mm_nt_heads(a, b3, *, tm=512, name):
    t, d = a.shape
    nh, n, _ = b3.shape
    tm = min(tm, t)
    assert t % tm == 0

    def body(a_ref, b_ref, o_ref):
        a_b = a_ref[...].astype(BF16)
        for h in range(nh):
            o_ref[h] = _dot_nt(a_b, b_ref[h]).astype(BF16)

    return pl.pallas_call(
        body, grid=(t // tm,),
        in_specs=[pl.BlockSpec((tm, d), lambda i: (i, 0)), pl.BlockSpec((nh, n, d), lambda i: (0, 0, 0))],
        out_specs=pl.BlockSpec((nh, tm, n), lambda i: (0, i, 0)),
        out_shape=jax.ShapeDtypeStruct((nh, t, n), BF16), compiler_params=_params(), name=name)(a, b3)


def mm_tn(a, b3, *, tk=512, tn=512, tt=2048, name):
    t, k = a.shape
    p, _, np_ = b3.shape
    nb = np_ // tn
    tt = min(tt, t)
    assert t % tt == 0 and k % tk == 0 and np_ % tn == 0

    def body(a_ref, b_ref, o_ref):
        prod = _dot_tn(a_ref[...].astype(BF16), b_ref[0].astype(BF16))

        @pl.when(pl.program_id(2) == 0)
        def _():
            o_ref[...] = prod

        @pl.when(pl.program_id(2) > 0)
        def _():
            o_ref[...] += prod

    return pl.pallas_call(
        body, grid=(k // tk, p * nb, t // tt),
        in_specs=[pl.BlockSpec((tt, tk), lambda i, j, s: (s, i)), pl.BlockSpec((1, tt, tn), lambda i, j, s: (j // nb, s, j % nb))],
        out_specs=pl.BlockSpec((tk, tn), lambda i, j, s: (i, j)),
        out_shape=jax.ShapeDtypeStruct((k, p * np_), F32),
        compiler_params=_params(), name=name)(a, b3)


def rms_mm_nn(h, g, b, *, parts=1, tm=1024, tn=512, name):
    t, d = h.shape
    n = b.shape[1]
    np_ = n // parts
    nb = np_ // tn
    tm = min(tm, t)
    assert t % tm == 0 and np_ % tn == 0

    def body(h_ref, g_ref, b_ref, xn_ref, o_ref):
        @pl.when(pl.program_id(1) == 0)
        def _():
            x = h_ref[...]
            r = lax.rsqrt(jnp.mean(x * x, axis=-1, keepdims=True) + EPS)
            xn_ref[...] = (x * r * g_ref[...]).astype(BF16)

        o_ref[...] = _dot(xn_ref[...], b_ref[...]).reshape(o_ref.shape)

    if parts == 1:
        out_spec = pl.BlockSpec((tm, tn), lambda i, j: (i, j))
        out_shape = jax.ShapeDtypeStruct((t, n), F32)
    else:
        out_spec = pl.BlockSpec((1, tm, tn), lambda i, j: (j // nb, i, j % nb))
        out_shape = jax.ShapeDtypeStruct((parts, t, np_), F32)
    row = pl.BlockSpec((tm, d), lambda i, j: (i, 0))
    return pl.pallas_call(
        body, grid=(t // tm, n // tn),
        in_specs=[row, pl.BlockSpec((1, d), lambda i, j: (0, 0)), pl.BlockSpec((d, tn), lambda i, j: (0, j))],
        out_specs=[row, out_spec], out_shape=[jax.ShapeDtypeStruct((t, d), BF16), out_shape],
        compiler_params=_params(), name=name)(h, g, b)


def mm_nt_rms_bwd(a3, b, h, g, dres, *, name, tm=512):
    p, t, kp = a3.shape
    d = b.shape[0]
    tm = min(tm, t)
    assert t % tm == 0 and b.shape[1] == p * kp

    def body(a_ref, b_ref, h_ref, g_ref, dres_ref, dh_ref, dg_ref, acc_ref):
        i = pl.program_id(0)
        part = pl.program_id(1)
        prod = _dot_nt(a_ref[0].astype(BF16), b_ref[...])

        @pl.when(part == 0)
        def _():
            acc_ref[...] = prod

        @pl.when(part > 0)
        def _():
            acc_ref[...] += prod

        @pl.when(part == p - 1)
        def _():
            x = h_ref[...]
            dy = acc_ref[...]
            r = lax.rsqrt(jnp.mean(x * x, axis=-1, keepdims=True) + EPS)
            gy = dy * g_ref[...]
            dot = jnp.mean(gy * x, axis=-1, keepdims=True)
            dh_ref[...] = dres_ref[...] + r * gy - x * (r * r * r * dot)
            _acc_rows(dg_ref, jnp.sum(dy * x * r, axis=0, keepdims=True), i == 0)

    row = pl.BlockSpec((tm, d), lambda i, q: (i, 0))
    vec = pl.BlockSpec((1, d), lambda i, q: (0, 0))
    return pl.pallas_call(
        body, grid=(t // tm, p),
        in_specs=[pl.BlockSpec((1, tm, kp), lambda i, q: (q, i, 0)), pl.BlockSpec((d, kp), lambda i, q: (0, q)),
                  row, vec, row],
        out_specs=[row, vec],
        out_shape=[jax.ShapeDtypeStruct((t, d), F32), jax.ShapeDtypeStruct((1, d), F32)],
        scratch_shapes=[pltpu.VMEM((tm, d), F32)], compiler_params=_params(), name=name)(a3, b, h, g, dres)


def _causal3(x, w):
    return w[0:1] * pltpu.roll(x, 2, 0) + w[1:2] * pltpu.roll(x, 1, 0) + w[2:3] * x


def _causal3_taps(x_ext, w, tm):
    x2 = pltpu.roll(x_ext, 2, 0)
    x1 = pltpu.roll(x_ext, 1, 0)
    y = w[0:1] * x2 + w[1:2] * x1 + w[2:3] * x_ext
    return y, (x2[HALO:HALO + tm], x1[HALO:HALO + tm], x_ext[HALO:HALO + tm])


def _anticausal3(z, w):
    n = z.shape[0]
    return w[2:3] * z + w[1:2] * pltpu.roll(z, n - 1, 0) + w[0:1] * pltpu.roll(z, n - 2, 0)


def _prev_spec(part, tm, tc, nrow8):
    del nrow8
    return pl.BlockSpec((1, HALO, tc), lambda j, i: (part, jnp.maximum(i * (tm // HALO) - 1, 0), j))


def _next_spec(part, tm, tc, nrow8):
    return pl.BlockSpec((1, HALO, tc), lambda j, i: (part, jnp.minimum((i + 1) * (tm // HALO), nrow8 - 1), j))


def _tile_spec(part, tm, tc):
    return pl.BlockSpec((1, tm, tc), lambda j, i: (part, i, j))


def _acc_rows(ref, val, first):
    @pl.when(first)
    def _():
        ref[...] = val

    @pl.when(jnp.logical_not(first))
    def _():
        ref[...] += val


def ffn_act_down_fwd(up2, cw2, w_down, h, *, name, tm=256, tc=1408):
    _, t, f = up2.shape
    d = h.shape[1]
    tm = min(tm, t)

    def body(g_ref, v_ref, gp_ref, vp_ref, w_ref, wd_ref, h_ref, o_ref, act_ref):
        keep = jnp.where(pl.program_id(0) == 0, 0.0, 1.0)
        for cc in range(f // tc):
            cols = slice(cc * tc, (cc + 1) * tc)
            g_ext = jnp.concatenate([gp_ref[0, :, cols] * keep, g_ref[0, :, cols]], axis=0)
            v_ext = jnp.concatenate([vp_ref[0, :, cols] * keep, v_ref[0, :, cols]], axis=0)
            ug = _causal3(g_ext, w_ref[0, :, cols])[HALO:]
            uv = _causal3(v_ext, w_ref[1, :, cols])[HALO:]
            act_ref[:, cols] = (ug * jax.nn.sigmoid(ug) * uv).astype(BF16)
        o_ref[...] = _dot(act_ref[...], wd_ref[...]) + h_ref[...]

    tile = lambda part: pl.BlockSpec((1, tm, f), lambda i: (part, i, 0))
    prev = lambda part: pl.BlockSpec((1, HALO, f), lambda i: (part, jnp.maximum(i * (tm // HALO) - 1, 0), 0))
    row = pl.BlockSpec((tm, d), lambda i: (i, 0))
    return pl.pallas_call(
        body, grid=(t // tm,),
        in_specs=[tile(0), tile(1), prev(0), prev(1), pl.BlockSpec((2, 3, f), lambda i: (0, 0, 0)),
                  pl.BlockSpec((f, d), lambda i: (0, 0)), row],
        out_specs=[row, pl.BlockSpec((tm, f), lambda i: (i, 0))],
        out_shape=[jax.ShapeDtypeStruct((t, d), F32), jax.ShapeDtypeStruct((t, f), BF16)],
        compiler_params=_params(), name=name)(up2, up2, up2, up2, cw2, w_down, h)


def ffn_act_bwd(dh, w_down, up2, cw2, *, name, tm=256, tc=1408):
    _, t, f = up2.shape
    d = dh.shape[1]
    n8 = t // HALO

    def body(d_ref, dn_ref, wd_ref, g_ref, v_ref, gp_ref, vp_ref, gn_ref, vn_ref, wg_ref, wv_ref, dup_ref, dw_ref):
        i = pl.program_id(1)
        first = i == 0
        keep_p = jnp.where(first, 0.0, 1.0)
        keep_n = jnp.where(i == pl.num_programs(1) - 1, 0.0, 1.0)
        dh_ext = jnp.concatenate([d_ref[...], dn_ref[...] * keep_n], axis=0).astype(BF16)
        edges = [0, (tc // 256) * 128, tc] if tc > 256 else [0, tc]
        for lo_, hi_ in zip(edges[:-1], edges[1:]):
            cols = slice(lo_, hi_)
            wg = wg_ref[0, :, cols]
            wv = wv_ref[0, :, cols]
            g_ext = jnp.concatenate([gp_ref[0, :, cols] * keep_p, g_ref[0, :, cols], gn_ref[0, :, cols]], axis=0)
            v_ext = jnp.concatenate([vp_ref[0, :, cols] * keep_p, v_ref[0, :, cols], vn_ref[0, :, cols]], axis=0)
            d_ext = _dot_nt(dh_ext, wd_ref[cols, :])
            ug, (g2, g1, g0) = _causal3_taps(g_ext, wg, tm)
            uv, (v2, v1, v0) = _causal3_taps(v_ext, wv, tm)
            ug = ug[HALO:]
            uv = uv[HALO:]
            s = jax.nn.sigmoid(ug)
            dg = d_ext * uv * (s * (1.0 + ug * (1.0 - s)))
            dv = d_ext * (ug * s)
            dup_ref[0, :, cols] = _anticausal3(dg, wg)[:tm].astype(BF16)
            dup_ref[1, :, cols] = _anticausal3(dv, wv)[:tm].astype(BF16)
            dgt = dg[:tm]
            dvt = dv[:tm]
            zero = jnp.zeros((HALO - 3, hi_ - lo_), F32)
            rows_g = [jnp.sum(dgt * x, axis=0, keepdims=True) for x in (g2, g1, g0)] + [zero]
            rows_v = [jnp.sum(dvt * x, axis=0, keepdims=True) for x in (v2, v1, v0)] + [zero]
            taps = jnp.stack([jnp.concatenate(rows_g, axis=0), jnp.concatenate(rows_v, axis=0)])

            @pl.when(first)
            def _():
                dw_ref[:, :, cols] = taps

            @pl.when(jnp.logical_not(first))
            def _():
                dw_ref[:, :, cols] += taps

    wspec = lambda part: pl.BlockSpec((1, 3, tc), lambda j, i: (part, 0, j))
    return pl.pallas_call(
        body, grid=(f // tc, t // tm),
        in_specs=[pl.BlockSpec((tm, d), lambda j, i: (i, 0)),
                  pl.BlockSpec((HALO, d), lambda j, i: (jnp.minimum((i + 1) * (tm // HALO), n8 - 1), 0)),
                  pl.BlockSpec((tc, d), lambda j, i: (j, 0)),
                  _tile_spec(0, tm, tc), _tile_spec(1, tm, tc), _prev_spec(0, tm, tc, n8), _prev_spec(1, tm, tc, n8),
                  _next_spec(0, tm, tc, n8), _next_spec(1, tm, tc, n8), wspec(0), wspec(1)],
        out_specs=[pl.BlockSpec((2, tm, tc), lambda j, i: (0, i, j)), pl.BlockSpec((2, HALO, tc), lambda j, i: (0, 0, j))],
        out_shape=[jax.ShapeDtypeStruct((2, t, f), BF16), jax.ShapeDtypeStruct((2, HALO, f), F32)],
        compiler_params=_params(), name=name)(dh, dh, w_down, up2, up2, up2, up2, up2, up2, cw2, cw2)


def conv_mix_fwd(proj3, ck, *, name, tm=512, tc=512):
    _, t, c = proj3.shape
    n8 = t // HALO

    def body(b_ref, c_ref, u_ref, cp_ref, up_ref, w_ref, o_ref):
        keep = jnp.where(pl.program_id(1) == 0, 0.0, 1.0)
        cu_ext = jnp.concatenate([cp_ref[0] * up_ref[0] * keep, c_ref[0] * u_ref[0]], axis=0)
        o_ref[...] = (b_ref[0] * _causal3(cu_ext, w_ref[0])[HALO:]).astype(BF16)

    return pl.pallas_call(
        body, grid=(c // tc, t // tm),
        in_specs=[_tile_spec(0, tm, tc), _tile_spec(1, tm, tc), _tile_spec(2, tm, tc), _prev_spec(1, tm, tc, n8),
                  _prev_spec(2, tm, tc, n8), pl.BlockSpec((1, 3, tc), lambda j, i: (0, 0, j))],
        out_specs=pl.BlockSpec((tm, tc), lambda j, i: (i, j)),
        out_shape=jax.ShapeDtypeStruct((t, c), BF16), compiler_params=_params(), name=name)(proj3, proj3, proj3, proj3, proj3, ck)


def conv_mix_out_fwd(proj3, ck, w_out, h, *, name, tm=512):
    _, t, c = proj3.shape
    d = h.shape[1]
    tm = min(tm, t)

    def body(b_ref, c_ref, u_ref, cp_ref, up_ref, w_ref, wo_ref, h_ref, o_ref, y_ref):
        keep = jnp.where(pl.program_id(0) == 0, 0.0, 1.0)
        cu_ext = jnp.concatenate([cp_ref[0] * up_ref[0] * keep, c_ref[0] * u_ref[0]], axis=0)
        y_ref[...] = (b_ref[0] * _causal3(cu_ext, w_ref[0])[HALO:]).astype(BF16)
        o_ref[...] = _dot(y_ref[...], wo_ref[...]) + h_ref[...]

    tile = lambda part: pl.BlockSpec((1, tm, c), lambda i: (part, i, 0))
    prev = lambda part: pl.BlockSpec((1, HALO, c), lambda i: (part, jnp.maximum(i * (tm // HALO) - 1, 0), 0))
    row = pl.BlockSpec((tm, d), lambda i: (i, 0))
    return pl.pallas_call(
        body, grid=(t // tm,),
        in_specs=[tile(0), tile(1), tile(2), prev(1), prev(2), pl.BlockSpec((1, 3, c), lambda i: (0, 0, 0)),
                  pl.BlockSpec((c, d), lambda i: (0, 0)), row],
        out_specs=[row, pl.BlockSpec((tm, c), lambda i: (i, 0))],
        out_shape=[jax.ShapeDtypeStruct((t, d), F32), jax.ShapeDtypeStruct((t, c), BF16)],
        compiler_params=_params(), name=name)(proj3, proj3, proj3, proj3, proj3, ck, w_out, h)


def conv_mix_bwd(dh, w_out, proj3, ck, *, name, tm=512, tc=512):
    _, t, c = proj3.shape
    d = dh.shape[1]
    n8 = t // HALO

    def body(d_ref, dn_ref, wo_ref, b_ref, c_ref, u_ref, cp_ref, up_ref, bn_ref, w_ref, dp_ref, dw_ref):
        i = pl.program_id(1)
        first = i == 0
        keep_p = jnp.where(first, 0.0, 1.0)
        keep_n = jnp.where(i == pl.num_programs(1) - 1, 0.0, 1.0)
        w = w_ref[0]
        cu_ext = jnp.concatenate([cp_ref[0] * up_ref[0] * keep_p, c_ref[0] * u_ref[0]], axis=0)
        cv, (x2, x1, x0) = _causal3_taps(cu_ext, w, tm)
        cv = cv[HALO:]
        dh_ext = jnp.concatenate([d_ref[...], dn_ref[...] * keep_n], axis=0)
        d_ext = _dot_nt(dh_ext.astype(BF16), wo_ref[...])
        dyt = d_ext[:tm]
        b_ext = jnp.concatenate([b_ref[0], bn_ref[0]], axis=0)
        dcv = d_ext * b_ext
        dcu = _anticausal3(dcv, w)[:tm]
        dp_ref[0] = (dyt * cv).astype(BF16)
        dp_ref[1] = (dcu * u_ref[0]).astype(BF16)
        dp_ref[2] = (dcu * c_ref[0]).astype(BF16)
        dcvt = dcv[:tm]
        rows = [jnp.sum(dcvt * x, axis=0, keepdims=True) for x in (x2, x1, x0)] + [jnp.zeros((HALO - 3, tc), F32)]
        _acc_rows(dw_ref, jnp.concatenate(rows, axis=0)[None], first)

    return pl.pallas_call(
        body, grid=(c // tc, t // tm),
        in_specs=[pl.BlockSpec((tm, d), lambda j, i: (i, 0)),
                  pl.BlockSpec((HALO, d), lambda j, i: (jnp.minimum((i + 1) * (tm // HALO), n8 - 1), 0)),
                  pl.BlockSpec((tc, d), lambda j, i: (j, 0)),
                  _tile_spec(0, tm, tc), _tile_spec(1, tm, tc), _tile_spec(2, tm, tc),
                  _prev_spec(1, tm, tc, n8), _prev_spec(2, tm, tc, n8),
                  _next_spec(0, tm, tc, n8), pl.BlockSpec((1, 3, tc), lambda j, i: (0, 0, j))],
        out_specs=[pl.BlockSpec((3, tm, tc), lambda j, i: (0, i, j)), pl.BlockSpec((1, HALO, tc), lambda j, i: (0, 0, j))],
        out_shape=[jax.ShapeDtypeStruct((3, t, c), BF16), jax.ShapeDtypeStruct((1, HALO, c), F32)],
        compiler_params=_params(), name=name)(dh, dh, w_out, proj3, proj3, proj3, proj3, proj3, proj3, ck)


def _head_sums(x, bd):
    hi, lo = _split2(x)
    return _dot(hi, bd) + _dot(lo, bd)


def attn_prep_fwd(proj, gq, gk, fbias, bd, *, name, tm=256):
    t = proj.shape[0]

    def body(q_ref, k_ref, v_ref, f_ref, gq_ref, gk_ref, fb_ref, bd_ref, qa_ref, ka_ref, va_ref, lf_ref):
        bd = bd_ref[...]
        lane = lax.broadcasted_iota(jnp.int32, (tm, 128), 1)
        low = lane < HEAD_DIM

        def two_heads(y, o_ref, c, rest):
            o_ref[2 * c] = jnp.where(low, y, rest).astype(BF16)
            o_ref[2 * c + 1] = jnp.where(low, pltpu.roll(y, HEAD_DIM, 1), rest).astype(BF16)

        def headnorm(x_ref, g_ref, o_ref, scale):
            for c in range(MIX // 128):
                sl = slice(128 * c, 128 * (c + 1))
                x = x_ref[:, sl]
                r = lax.rsqrt(_head_sums(x * x, bd) * (1.0 / HEAD_DIM) + EPS)
                two_heads(x * r * (g_ref[:, sl] * scale), o_ref, c, 0.0)

        headnorm(q_ref, gq_ref, qa_ref, SCALE)
        headnorm(k_ref, gk_ref, ka_ref, 1.0)
        one_at_64 = jnp.where(lane == HEAD_DIM, 1.0, 0.0)
        for c in range(MIX // 128):
            two_heads(v_ref[:, 128 * c:128 * (c + 1)], va_ref, c, one_at_64)
        fl = f_ref[...] + fb_ref[...]
        logf = jnp.minimum(fl, 0.0) - jnp.log(1.0 + jnp.exp(-jnp.abs(fl)))
        lf_ref[...] = logf.T[0:H_FOX, :]

    col = lambda c: pl.BlockSpec((tm, MIX), lambda i: (i, c))
    vec = pl.BlockSpec((1, MIX), lambda i: (0, 0))
    out = pl.BlockSpec((N_HEADS, tm, 2 * HEAD_DIM), lambda i: (0, i, 0))
    return pl.pallas_call(
        body, grid=(t // tm,),
        in_specs=[col(0), col(1), col(2), pl.BlockSpec((tm, 128), lambda i: (i, 3 * MIX // 128)), vec, vec,
                  pl.BlockSpec((1, 128), lambda i: (0, 0)), pl.BlockSpec((128, 128), lambda i: (0, 0))],
        out_specs=[out, out, out, pl.BlockSpec((H_FOX, tm), lambda i: (0, i))],
        out_shape=[jax.ShapeDtypeStruct((N_HEADS, t, 2 * HEAD_DIM), BF16)] * 3 + [jax.ShapeDtypeStruct((H_FOX, t), F32)],
        compiler_params=_params(), name=name)(proj, proj, proj, proj, gq, gk, fbias, bd)


def attn_prep_bwd(proj, dq_f, dq_s, dkv_f, dkv_s, dfl, gq, gk, bd, *, name, tm=256):
    t = proj.shape[0]
    per_group = H_FOX // 2

    def body(q_ref, k_ref, dqf_ref, dqs_ref, dkvf_ref, dkvs_ref, dfl_ref, gq_ref, gk_ref, bd_ref, dp_ref, dgq_ref,
             dgk_ref):
        bd = bd_ref[...]
        first = pl.program_id(0) == 0
        low = lax.broadcasted_iota(jnp.int32, (tm, 128), 1) < HEAD_DIM

        def two_heads(fox_ref, sb_ref, c):
            ref = fox_ref if c < per_group else sb_ref
            return ref[2 * (c % per_group)], ref[2 * (c % per_group) + 1]

        def low_halves(ab):
            return jnp.where(low, ab[0], pltpu.roll(ab[1], HEAD_DIM, 1))

        def high_halves(ab):
            return jnp.where(low, pltpu.roll(ab[0], HEAD_DIM, 1), ab[1])

        def back(x_ref, grad, g_ref, col0, scale, dg_ref):
            parts = []
            for c in range(MIX // 128):
                sl = slice(128 * c, 128 * (c + 1))
                x = x_ref[:, sl]
                r = lax.rsqrt(_head_sums(x * x, bd) * (1.0 / HEAD_DIM) + EPS)
                dn = grad(c) * scale
                gy = dn * g_ref[:, sl]
                hs = _head_sums(gy * x, bd) * (1.0 / HEAD_DIM)
                dp_ref[:, col0 + 128 * c:col0 + 128 * (c + 1)] = (r * gy - x * (r * r * r * hs)).astype(BF16)
                parts.append(jnp.sum(dn * x * r, axis=0, keepdims=True))
            _acc_rows(dg_ref, jnp.concatenate(parts, axis=1), first)

        back(q_ref, lambda c: low_halves(two_heads(dqf_ref, dqs_ref, c)), gq_ref, 0, SCALE, dgq_ref)
        back(k_ref, lambda c: low_halves(two_heads(dkvf_ref, dkvs_ref, c)), gk_ref, MIX, 1.0, dgk_ref)
        for c in range(MIX // 128):
            dp_ref[:, 2 * MIX + 128 * c:2 * MIX + 128 * (c + 1)] = high_halves(two_heads(dkvf_ref, dkvs_ref, c)).astype(BF16)
        dp_ref[:, 3 * MIX:] = dfl_ref[...]

    col = lambda c: pl.BlockSpec((tm, MIX), lambda i: (i, c))
    heads = pl.BlockSpec((H_FOX, tm, 128), lambda i: (0, i, 0))
    vec = pl.BlockSpec((1, MIX), lambda i: (0, 0))
    return pl.pallas_call(
        body, grid=(t // tm,),
        in_specs=[col(0), col(1), heads, heads, heads, heads, pl.BlockSpec((tm, 128), lambda i: (i, 0)), vec, vec,
                  pl.BlockSpec((128, 128), lambda i: (0, 0))],
        out_specs=[pl.BlockSpec((tm, ATTN_IN_PAD), lambda i: (i, 0)), vec, vec],
        out_shape=[jax.ShapeDtypeStruct((t, ATTN_IN_PAD), BF16), jax.ShapeDtypeStruct((1, MIX), F32),
                   jax.ShapeDtypeStruct((1, MIX), F32)],
        compiler_params=_params(), name=name)(proj, proj, dq_f, dq_s, dkv_f, dkv_s, dfl, gq, gk, bd)


def gate_cumsum(logf3, tri, *, name):
    nc, r, _ = logf3.shape

    def body(x_ref, tri_ref, o_ref):
        tri_m = tri_ref[...]

        def step(c, carry):
            hi, mid, lo = _split3(x_ref[c])
            cs = _dot(hi, tri_m) + _dot(mid, tri_m) + _dot(lo, tri_m) + carry
            o_ref[c] = cs
            return cs[:, 127:128]

        lax.fori_loop(0, nc, step, jnp.zeros((r, 1), F32))

    return pl.pallas_call(body, out_shape=jax.ShapeDtypeStruct(logf3.shape, F32), compiler_params=_params(),
                          name=name)(logf3, tri)


def gate_cumsum_bwd(dcum3, logf3, tri, *, name):
    nc, r, _ = dcum3.shape

    def body(x_ref, lf_ref, tri_ref, o_ref, s_ref):
        tri_m = tri_ref[...]

        def step(n, carry):
            car, tot = carry
            c = nc - 1 - n
            hi, mid, lo = _split3(x_ref[c])
            cs = _dot(hi, tri_m) + _dot(mid, tri_m) + _dot(lo, tri_m) + car
            dl = cs * (1.0 - jnp.exp(lf_ref[c]))
            o_ref[c] = dl
            return cs[:, 0:1], tot + dl

        _, tot = lax.fori_loop(0, nc, step, (jnp.zeros((r, 1), F32), jnp.zeros((r, 128), F32)))
        s_ref[...] = jnp.broadcast_to(jnp.sum(tot, axis=1, keepdims=True), tot.shape)

    return pl.pallas_call(body, out_shape=[jax.ShapeDtypeStruct(dcum3.shape, F32), jax.ShapeDtypeStruct((r, 128), F32)],
                          compiler_params=_params(), name=name)(dcum3, logf3, tri)


def _causal_iota():
    row = lax.broadcasted_iota(jnp.int32, (BQ, BQ), 0)
    col = lax.broadcasted_iota(jnp.int32, (BQ, BQ), 1)
    return row, col


def _head_specs(nj, head0):
    qin = pl.BlockSpec((1, BQ, HEAD_DIM), lambda h, i: (h + head0, i, 0))
    kin = pl.BlockSpec((1, nj, BQ, HEAD_DIM), lambda h, i: (h + head0, 0, 0, 0))
    qin2 = pl.BlockSpec((1, BQ, 2 * HEAD_DIM), lambda h, i: (h + head0, i, 0))
    kin2 = pl.BlockSpec((1, nj, BQ, 2 * HEAD_DIM), lambda h, i: (h + head0, 0, 0, 0))
    qspec = pl.BlockSpec((1, BQ, HEAD_DIM), lambda h, i: (h, i, 0))
    kspec2 = pl.BlockSpec((1, nj, BQ, 2 * HEAD_DIM), lambda h, i: (h, 0, 0, 0))
    return qin, kin, qin2, kin2, qspec, kspec2


STOP = -105.0
STOP_WIDE = -115.0
FIXED_REF_MAX = 40.0


def _store_kmax(k_ref, kmax_ref, nj):
    def step(j, mx):
        kf = k_ref[0, j].astype(F32)
        return jnp.maximum(mx, jnp.max(jnp.sum(kf * kf, axis=1, keepdims=True), axis=0, keepdims=True))

    mx = lax.fori_loop(0, nj, step, jnp.zeros((1, 1), F32))
    kmax_ref[...] = jnp.broadcast_to(jnp.sqrt(mx), kmax_ref.shape)


def _qk_bound(q, kmax_ref):
    qf = q.astype(F32)
    return jnp.sqrt(jnp.sum(qf * qf, axis=1, keepdims=True)) * kmax_ref[0:1, 0:1] * 1.001


def _first_and_last_step():
    h, i = pl.program_id(0), pl.program_id(1)
    first = jnp.logical_and(h == 0, i == 0)
    last = jnp.logical_and(h == pl.num_programs(0) - 1, i == pl.num_programs(1) - 1)
    return first, last


def fox_fwd(qa, ka4, va4, fcol, frow4, *, name, gather=None):
    t = qa.shape[1]
    dh = HEAD_DIM
    nh = H_FOX
    nj = t // BQ

    def body(*refs):
        if gather is None:
            q_ref, k_ref, v_ref, fc_ref, fr_ref, o_ref, lse_ref, kmax_ref = refs
        else:
            q_ref, k_ref, v_ref, fc_ref, fr_ref, src_ref, o_ref, lse_ref, dst_ref, kmax_ref = refs[:10]
            first_step, last_step = _first_and_last_step()

            @pl.when(first_step)
            def _():
                _chip_gather(src_ref, dst_ref, *refs[10:])[0]()

        i = pl.program_id(1)

        @pl.when(i == 0)
        def _():
            _store_kmax(k_ref, kmax_ref, nj)

        q = q_ref[0]
        fq = fc_ref[0]
        bound = _qk_bound(q, kmax_ref)
        row, col = _causal_iota()

        def gate_at_block_end(j):
            return fr_ref[0, j][:, BQ - 1:BQ]

        def pv(p, j):
            p_hi, p_lo = _split2(p)
            return _dot(p_hi, v_ref[0, j]) + _dot(p_lo, v_ref[0, j])

        def walk(block, live, init):
            carry = block(i, init, True)

            def cond(c):
                n, carry = c
                return jnp.logical_and(n < i, live(jnp.maximum(i - 1 - n, 0), carry))

            _, carry = lax.while_loop(cond, lambda c: (c[0] + 1, block(i - 1 - c[0], c[1], False)), (0, carry))
            return carry

        def fixed_reference(_):
            shift = fq - bound

            def probs(j, offset):
                return jnp.exp(_dot_nt(q, k_ref[0, j]) + (shift + offset) - fr_ref[0, j])

            def live(c):
                n, acc = c
                gate = gate_at_block_end(jnp.maximum(i - 1 - n, 0))
                return jnp.logical_and(n < i, jnp.max(fq - gate - jnp.log(acc[:, dh:dh + 1])) >= STOP_WIDE)

            def two_blocks(c):
                n, acc = c
                ja = i - 1 - n
                jb = i - 2 - n
                absent = jnp.where(jb >= 0, 0.0, NEG)
                jb = jnp.maximum(jb, 0)
                return n + 2, acc + (pv(probs(ja, 0.0), ja) + pv(probs(jb, absent), jb))

            acc = pv(jnp.where(col <= row, probs(i, 0.0), 0.0), i)
            _, acc = lax.while_loop(live, two_blocks, (0, acc))
            l = acc[:, dh:dh + 1]
            return acc / l, bound + jnp.log(l)

        def running_maximum(_):
            def block(j, carry, diag):
                m, acc = carry
                s = _dot_nt(q, k_ref[0, j]) + fq - fr_ref[0, j]
                if diag:
                    s = jnp.where(col <= row, s, NEG)
                m_new = jnp.maximum(m, jnp.max(s, axis=1, keepdims=True))
                return m_new, jnp.exp(m - m_new) * acc + pv(jnp.exp(s - m_new), j)

            def live(j, carry):
                return jnp.max(bound + fq - gate_at_block_end(j) - carry[0]) >= STOP

            m, acc = walk(block, live, (jnp.full((BQ, 1), NEG, F32), jnp.zeros((BQ, 2 * dh), F32)))
            l = acc[:, dh:dh + 1]
            return acc / l, m + jnp.log(l)

        o, lse = lax.cond(jnp.max(bound) < FIXED_REF_MAX, fixed_reference, running_maximum, 0)
        o_ref[0] = o
        lse_ref[0] = lse

        if gather is not None:
            @pl.when(last_step)
            def _():
                _chip_gather(src_ref, dst_ref, *refs[10:])[1]()

    _, _, qin2, kin2, _, _ = _head_specs(nj, 0)
    cspec = pl.BlockSpec((1, BQ, 1), lambda h, i: (h, i, 0))
    in_specs = [qin2, kin2, kin2, cspec, pl.BlockSpec((1, nj, 1, BQ), lambda h, i: (h, 0, 0, 0))]
    out_specs = [pl.BlockSpec((1, BQ, 2 * dh), lambda h, i: (h, i, 0)), cspec]
    out_shape = [jax.ShapeDtypeStruct((nh, t, 2 * dh), F32), jax.ShapeDtypeStruct((nh, t, 1), F32)]
    scratch = [pltpu.VMEM((8, 128), F32)]
    args = [qa, ka4, va4, fcol, frow4]
    if gather is not None:
        in_specs.append(_ANY)
        out_specs.append(_ANY)
        out_shape.append(jax.ShapeDtypeStruct((N_CHIPS,) + gather.shape, gather.dtype))
        scratch += _chip_sems()
        args.append(gather)
    return pl.pallas_call(body, grid=(nh, nj), in_specs=in_specs, out_specs=out_specs, out_shape=out_shape,
                          scratch_shapes=scratch, compiler_params=_params(), name=name)(*args)


def _other_half(x):
    return pltpu.roll(x.astype(F32), HEAD_DIM, 1).astype(BF16)


def fox_bwd(qa, ka4, va4, dox, fcol, frow4, o, lse, *, name, scatter=None):
    t = qa.shape[1]
    dh = HEAD_DIM
    nh = H_FOX
    nj = t // BQ
    n_in = 8

    def body(*refs):
        q_ref, k_ref, v_ref, dox_ref, fc_ref, fr_ref, o_ref, lse_ref = refs[:n_in]
        if scatter is None:
            dq_ref, dkv_ref, dfk_ref, kmax_ref = refs[n_in:]
        else:
            g_ref, dq_ref, dkv_ref, dfk_ref, land_ref, kmax_ref = refs[n_in:n_in + 6]
            first_step, last_step = _first_and_last_step()

            @pl.when(first_step)
            def _():
                _chip_scatter(g_ref, land_ref, *refs[n_in + 6:])[0]()

        i = pl.program_id(1)

        @pl.when(i == 0)
        def _():
            dkv_ref[...] = jnp.zeros_like(dkv_ref)
            dfk_ref[...] = jnp.zeros_like(dfk_ref)
            _store_kmax(k_ref, kmax_ref, nj)

        q = q_ref[0]
        fq = fc_ref[0]
        lse_q = lse_ref[0]
        do_x = dox_ref[0]
        dd = jnp.sum(do_x.astype(F32) * o_ref[0], axis=1, keepdims=True)
        rhs = jnp.concatenate([q, _other_half(do_x)], axis=0)
        edge = _qk_bound(q, kmax_ref) + fq - lse_q

        def negligible(j):
            return jnp.logical_and(j < i, jnp.max(edge - fr_ref[0, j][:, BQ - 1:BQ]) < STOP_WIDE)

        first = lax.while_loop(negligible, lambda j: j + 1, 0)

        shift = fq - lse_q

        def block(j, offset, diag):
            k = k_ref[0, j]
            p = jnp.exp(_dot_nt(q, k) + (shift + offset) - fr_ref[0, j])
            if diag:
                row, col = _causal_iota()
                p = jnp.where(col <= row, p, 0.0)
            ds = p * (_dot_nt(do_x, v_ref[0, j]) - dd)
            ds_b = ds.astype(BF16)
            dkv_ref[0, j] += _dot_tn(jnp.concatenate([ds_b, p.astype(BF16)], axis=0), rhs)
            dfk_ref[0, j] -= jnp.sum(ds, axis=0, keepdims=True)
            return _dot(ds_b, k)

        def two_blocks(n, dq):
            ja = first + 2 * n
            jb = ja + 1
            absent = jnp.where(jb < i, 0.0, NEG)
            jb = jnp.minimum(jb, i - 1)
            return dq + (block(ja, 0.0, False) + block(jb, absent, False))

        dq = lax.fori_loop(0, (i - first + 1) // 2, two_blocks, jnp.zeros((BQ, 2 * dh), F32))
        dq_ref[0] = dq + block(i, 0.0, True)

        if scatter is not None:
            @pl.when(last_step)
            def _():
                _chip_scatter(g_ref, land_ref, *refs[n_in + 6:])[1]()

    _, _, qin2, kin2, _, kspec2 = _head_specs(nj, 0)
    cspec = pl.BlockSpec((1, BQ, 1), lambda h, i: (h, i, 0))
    rspec = pl.BlockSpec((1, nj, 1, BQ), lambda h, i: (h, 0, 0, 0))
    wide = pl.BlockSpec((1, BQ, 2 * dh), lambda h, i: (h, i, 0))
    in_specs = [qin2, kin2, kin2, qin2, cspec, rspec, wide, cspec]
    out_specs = [wide, kspec2, rspec]
    out_shape = [jax.ShapeDtypeStruct((nh, t, 2 * dh), F32), jax.ShapeDtypeStruct((nh, nj, BQ, 2 * dh), F32),
                 jax.ShapeDtypeStruct((nh, nj, 1, BQ), F32)]
    scratch = [pltpu.VMEM((8, 128), F32)]
    args = [qa, ka4, va4, dox, fcol, frow4, o, lse]
    if scatter is not None:
        in_specs.append(_ANY)
        out_specs.append(_ANY)
        out_shape.append(jax.ShapeDtypeStruct(scatter.shape, scatter.dtype))
        scratch += _chip_sems()
        args.append(scatter)
    return pl.pallas_call(body, grid=(nh, nj), in_specs=in_specs, out_specs=out_specs, out_shape=out_shape,
                          scratch_shapes=scratch, compiler_params=_params(), name=name)(*args)


def _sb_logs(z, diag):
    e = jnp.exp(-jnp.abs(z))
    sp = jnp.log(1.0 + e)
    logb = jnp.minimum(z, 0.0) - sp
    lom = -jnp.maximum(z, 0.0) - sp
    strict = None
    if diag:
        row, col = _causal_iota()
        strict = col < row
        lom = jnp.where(strict, lom, 0.0)
    return logb, lom, e, strict


SB_GROUP = BQ // 2


def _sums_over_later_keys(lom, tri_m):
    halves = [lom[:, :SB_GROUP], lom[:, SB_GROUP:]]
    totals = [jnp.sum(x, axis=1, keepdims=True) for x in halves]
    within = []
    for x in halves:
        hi, lo = _split2(x)
        within.append(_dot(hi, tri_m) + _dot(lo, tri_m))
    return jnp.concatenate([within[0] + totals[1], within[1]], axis=1), totals[0] + totals[1]


def _sums_over_earlier_keys(da, tri_m):
    halves = [da[:, :SB_GROUP], da[:, SB_GROUP:]]
    totals = [jnp.sum(x, axis=1, keepdims=True) for x in halves]
    within = [_dot_nt(x.astype(BF16), tri_m) for x in halves]
    return jnp.concatenate([within[0], within[1] + totals[0]], axis=1), totals[0] + totals[1]


def sb_fwd(qa, ka4, va4, tri, *, name, gather=None):
    t = qa.shape[1]
    dh = HEAD_DIM
    nh = H_SB
    nj = t // BQ
    assert nj <= 128

    def body(*refs):
        if gather is None:
            q_ref, k_ref, v_ref, tri_ref, o_ref, rs_ref = refs
        else:
            q_ref, k_ref, v_ref, tri_ref, src_ref, o_ref, rs_ref, dst_ref = refs[:8]
            first_step, last_step = _first_and_last_step()

            @pl.when(first_step)
            def _():
                _chip_gather(src_ref, dst_ref, *refs[8:])[0]()

        i = pl.program_id(1)
        q = q_ref[0]
        tri_m = tri_ref[...]
        lane = lax.broadcasted_iota(jnp.int32, (BQ, 128), 1)

        def block(j, carry, diag):
            run, acc, rall = carry
            logb, lom, _, strict = _sb_logs(_dot_nt(q, k_ref[0, j]), diag)
            later, total = _sums_over_later_keys(lom, tri_m)
            w = jnp.exp(logb + later + run)
            if diag:
                w = jnp.where(strict, w, 0.0)
            acc = acc + _dot(w.astype(BF16), v_ref[0, j])
            rall = jnp.where(lane == j, run, rall)
            return run + total, acc, rall

        init = (jnp.zeros((BQ, 1), F32), jnp.zeros((BQ, 2 * dh), F32), jnp.full((BQ, 128), NEG, F32))
        carry = block(i, init, True)

        def cond(c):
            n, carry = c
            return jnp.logical_and(n < i, jnp.max(carry[0]) >= STOP)

        _, (_, acc, rall) = lax.while_loop(cond, lambda c: (c[0] + 1, block(i - 1 - c[0], c[1], False)), (0, carry))
        o_ref[0] = acc[:, :dh].astype(BF16)
        rs_ref[0] = rall

        if gather is not None:
            @pl.when(last_step)
            def _():
                _chip_gather(src_ref, dst_ref, *refs[8:])[1]()

    _, _, qin2, kin2, qspec, _ = _head_specs(nj, H_FOX)
    rspec = pl.BlockSpec((1, BQ, 128), lambda h, i: (h, i, 0))
    in_specs = [qin2, kin2, kin2, pl.BlockSpec((SB_GROUP, SB_GROUP), lambda h, i: (0, 0))]
    out_specs = [qspec, rspec]
    out_shape = [jax.ShapeDtypeStruct((nh, t, dh), BF16), jax.ShapeDtypeStruct((nh, t, 128), F32)]
    scratch = []
    args = [qa, ka4, va4, tri]
    if gather is not None:
        in_specs.append(_ANY)
        out_specs.append(_ANY)
        out_shape.append(jax.ShapeDtypeStruct((N_CHIPS,) + gather.shape, gather.dtype))
        scratch += _chip_sems()
        args.append(gather)
    return pl.pallas_call(body, grid=(nh, nj), in_specs=in_specs, out_specs=out_specs, out_shape=out_shape,
                          scratch_shapes=scratch, compiler_params=_params(), name=name)(*args)


def sb_bwd(qa, ka4, va4, dox, tri, rsave, *, name, scatter=None):
    t = qa.shape[1]
    dh = HEAD_DIM
    nh = H_SB
    nj = t // BQ
    n_in = 6

    def body(*refs):
        q_ref, k_ref, v_ref, dox_ref, tri_ref, rs_ref = refs[:n_in]
        if scatter is None:
            dq_ref, dkv_ref = refs[n_in:]
        else:
            g_ref, dq_ref, dkv_ref, land_ref = refs[n_in:n_in + 4]
            first_step, last_step = _first_and_last_step()

            @pl.when(first_step)
            def _():
                _chip_scatter(g_ref, land_ref, *refs[n_in + 4:])[0]()

        i = pl.program_id(1)

        @pl.when(i == 0)
        def _():
            dkv_ref[...] = jnp.zeros_like(dkv_ref)

        q = q_ref[0]
        do_x = dox_ref[0]
        tri_m = tri_ref[...]
        rall = rs_ref[0]
        lane = lax.broadcasted_iota(jnp.int32, (BQ, 128), 1)
        rhs = jnp.concatenate([q, _other_half(do_x)], axis=0)
        lane1 = lax.broadcasted_iota(jnp.int32, (1, 128), 1)
        unvisited = jnp.logical_and(lane1 < i, jnp.max(rall, axis=0, keepdims=True) < STOP)
        first = jnp.sum(unvisited.astype(jnp.int32))

        def block(j, carry, diag):
            dq, ecar = carry
            k = k_ref[0, j]
            z = _dot_nt(q, k)
            logb, lom, e, strict = _sb_logs(z, diag)
            run = jnp.sum(jnp.where(lane == j, rall, 0.0), axis=1, keepdims=True)
            w = jnp.exp(logb + _sums_over_later_keys(lom, tri_m)[0] + run)
            if diag:
                w = jnp.where(strict, w, 0.0)
            da = w * _dot_nt(do_x, v_ref[0, j])
            earlier, da_total = _sums_over_earlier_keys(da, tri_m)
            before = earlier + ecar
            inv = 1.0 / (1.0 + e)
            beta = jnp.where(z >= 0.0, 1.0, e) * inv
            one_minus = jnp.where(z >= 0.0, e, 1.0) * inv
            dz = da * one_minus - before * beta
            if diag:
                dz = jnp.where(strict, dz, 0.0)
            dz_b = dz.astype(BF16)
            dkv_ref[0, j] += _dot_tn(jnp.concatenate([dz_b, w.astype(BF16)], axis=0), rhs)
            return dq + _dot(dz_b, k), ecar + da_total

        carry = lax.fori_loop(first, i, lambda j, c: block(j, c, False),
                              (jnp.zeros((BQ, 2 * dh), F32), jnp.zeros((BQ, 1), F32)))
        dq, _ = block(i, carry, True)
        dq_ref[0] = dq

        if scatter is not None:
            @pl.when(last_step)
            def _():
                _chip_scatter(g_ref, land_ref, *refs[n_in + 4:])[1]()

    _, _, qin2, kin2, _, kspec2 = _head_specs(nj, H_FOX)
    in_specs = [qin2, kin2, kin2, qin2, pl.BlockSpec((SB_GROUP, SB_GROUP), lambda h, i: (0, 0)),
                pl.BlockSpec((1, BQ, 128), lambda h, i: (h, i, 0))]
    out_specs = [pl.BlockSpec((1, BQ, 2 * dh), lambda h, i: (h, i, 0)), kspec2]
    out_shape = [jax.ShapeDtypeStruct((nh, t, 2 * dh), F32), jax.ShapeDtypeStruct((nh, nj, BQ, 2 * dh), F32)]
    scratch = []
    args = [qa, ka4, va4, dox, tri, rsave]
    if scatter is not None:
        in_specs.append(_ANY)
        out_specs.append(_ANY)
        out_shape.append(jax.ShapeDtypeStruct(scatter.shape, scatter.dtype))
        scratch += _chip_sems()
        args.append(scatter)
    return pl.pallas_call(body, grid=(nh, nj), in_specs=in_specs, out_specs=out_specs, out_shape=out_shape,
                          scratch_shapes=scratch, compiler_params=_params(), name=name)(*args)


def loss_head(y, target, *, name, tm=512):
    t, d = y.shape

    def body(y_ref, t_ref, l_ref, dy_ref, acc_ref):
        i = pl.program_id(0)
        diff = y_ref[...] - t_ref[...]
        dy_ref[...] = diff * (1.0 / d)
        part = jnp.sum(diff * diff, axis=0, keepdims=True)

        @pl.when(i == 0)
        def _():
            acc_ref[...] = part

        @pl.when(i > 0)
        def _():
            acc_ref[...] += part

        @pl.when(i == pl.num_programs(0) - 1)
        def _():
            l_ref[...] = jnp.full(l_ref.shape, (0.5 / d) * jnp.sum(acc_ref[...]), F32)

    row = pl.BlockSpec((tm, d), lambda i: (i, 0))
    return pl.pallas_call(
        body, grid=(t // tm,), in_specs=[row, row],
        out_specs=[pl.BlockSpec((8, 128), lambda i: (0, 0)), row],
        out_shape=[jax.ShapeDtypeStruct((8, 128), F32), jax.ShapeDtypeStruct((t, d), F32)],
        scratch_shapes=[pltpu.VMEM((1, d), F32)], compiler_params=_params(), name=name)(y, target)


def _from_heads(a):
    t = a.shape[1]
    return a.transpose(1, 0, 2).reshape(t, MIX)


def _lanes_to_chunks(a):
    r, t = a.shape
    return a.reshape(r, t // 128, 128).transpose(1, 0, 2)


def _chunks_to_lanes(a):
    nc, r, _ = a.shape
    return a.transpose(1, 0, 2).reshape(r, nc * 128)


def _constants():
    idx = jnp.arange(128)
    bd = (idx[:, None] // HEAD_DIM == idx[None, :] // HEAD_DIM).astype(BF16)
    tri_le = (idx[:, None] <= idx[None, :]).astype(BF16)
    tri_ge = (idx[:, None] >= idx[None, :]).astype(BF16)
    jdx = jnp.arange(SB_GROUP)
    tri_gt = (jdx[:, None] > jdx[None, :]).astype(BF16)
    return dict(bd=bd, tri_le=tri_le, tri_ge=tri_ge, tri_gt=tri_gt)


def attn_layer_fwd(h, w, cst, gather=None):
    t = h.shape[0]
    nj = t // BQ
    xn, proj = rms_mm_nn(h, w["norm"], w["w_in"], tn=640, name="attn_in_proj")
    qa, ka, va, logf = attn_prep_fwd(proj, w["gq"], w["gk"], w["fbias"], cst["bd"], name="attn_prep_fwd")
    logf3 = _lanes_to_chunks(logf)
    cum = _chunks_to_lanes(gate_cumsum(logf3, cst["tri_le"], name="gate_cumsum"))
    fcol = cum.reshape(H_FOX, t, 1)
    frow4 = cum.reshape(H_FOX, nj, 1, BQ)
    ka4 = ka.reshape(N_HEADS, nj, BQ, 2 * HEAD_DIM)
    va4 = va.reshape(N_HEADS, nj, BQ, 2 * HEAD_DIM)
    if gather is None:
        (o_f, lse), (o_s, rsave), gathered = (fox_fwd(qa, ka4, va4, fcol, frow4, name="fox_fwd"),
                                              sb_fwd(qa, ka4, va4, cst["tri_gt"], name="sb_fwd"), None)
    else:
        o_f, lse, gathered_a = fox_fwd(qa, ka4, va4, fcol, frow4, name="fox_fwd_gather", gather=gather[0])
        o_s, rsave, gathered_b = sb_fwd(qa, ka4, va4, cst["tri_gt"], name="sb_fwd_gather", gather=gather[1])
        gathered = gather[2]((gathered_a, gathered_b))
        w = gathered[0][0]
    o = _from_heads(jnp.concatenate([o_f[:, :, :HEAD_DIM].astype(BF16), o_s], axis=0))
    h2 = mm_nn(o, w["w_out"], add=h, name="mix_out_proj")
    saved = dict(h=h, xn=xn, proj=proj, logf3=logf3, fcol=fcol, frow4=frow4, qa=qa, ka4=ka4, va4=va4,
                 o_f=o_f, lse=lse, rsave=rsave, o=o)
    return h2, saved, gathered


def attn_layer_bwd(dh, w, s, cst, scatter=None):
    t = dh.shape[0]
    dh3 = dh[None]
    w_out_heads = jnp.pad(w["w_out"].reshape(N_HEADS, HEAD_DIM, -1), ((0, 0), (0, HEAD_DIM), (0, 0)))
    dox = mm_nt_heads(dh, w_out_heads, name="mix_out_bwd_heads")
    g_w_out = mm_tn(s["o"], dh3, name="mix_out_wgrad")
    fox_args = (s["qa"], s["ka4"], s["va4"], dox, s["fcol"], s["frow4"], s["o_f"], s["lse"])
    sb_args = (s["qa"], s["ka4"], s["va4"], dox, cst["tri_gt"], s["rsave"])
    if scatter is None:
        (dq_f, dkv_f, dfk), (dq_s, dkv_s), landed = fox_bwd(*fox_args, name="fox_bwd"), sb_bwd(*sb_args, name="sb_bwd"), None
    else:
        chunks_a, chunks_b = scatter(g_w_out)
        dq_f, dkv_f, dfk, landed_a = fox_bwd(*fox_args, name="fox_bwd_scatter", scatter=chunks_a)
        dq_s, dkv_s, landed_b = sb_bwd(*sb_args, name="sb_bwd_scatter", scatter=chunks_b)
        landed = (landed_a, landed_b)
    dcum3 = _lanes_to_chunks(dfk.reshape(H_FOX, t))
    dfl3, dbias = gate_cumsum_bwd(dcum3, s["logf3"], cst["tri_ge"], name="gate_cumsum_bwd")
    dfl = jnp.pad(_chunks_to_lanes(dfl3).T, ((0, 0), (0, 128 - H_FOX))).astype(BF16)
    wide = (H_FOX, t, 2 * HEAD_DIM)
    dproj, dgq, dgk = attn_prep_bwd(s["proj"], dq_f, dq_s, dkv_f.reshape(wide), dkv_s.reshape(wide), dfl, w["gq"],
                                    w["gk"], cst["bd"], name="attn_prep_bwd")
    g_w_in = mm_tn(s["xn"], dproj[None], tn=640, name="attn_in_wgrad")[:, :ATTN_IN]
    dh2, g_norm = mm_nt_rms_bwd(dproj[None], w["w_in"], s["h"], w["norm"], dh, name="attn_in_bwd")
    dgq = dgq.reshape(N_HEADS, HEAD_DIM)
    dgk = dgk.reshape(N_HEADS, HEAD_DIM)
    grads = dict(norm=g_norm[0], w_in=g_w_in, f_bias=dbias[:, 0], fox_q=dgq[:H_FOX].sum(0), fox_k=dgk[:H_FOX].sum(0),
                 sb_q=dgq[H_FOX:].sum(0), sb_k=dgk[H_FOX:].sum(0), w_out=g_w_out)
    return dh2, grads, landed


def conv_layer_fwd(h, w):
    xn, proj3 = rms_mm_nn(h, w["norm"], w["w_in"], parts=3, name="conv_in_proj")
    h2, y = conv_mix_out_fwd(proj3, w["ck"], w["w_out"], h, name="conv_mix_out_fwd")
    return h2, dict(h=h, xn=xn, proj3=proj3, y=y)


def conv_layer_bwd(dh, w, s):
    dh3 = dh[None]
    g_w_out = mm_tn(s["y"], dh3, name="mix_out_wgrad")
    dproj3, dck = conv_mix_bwd(dh, w["w_out"], s["proj3"], w["ck"], name="conv_mix_bwd")
    g_w_in = mm_tn(s["xn"], dproj3, name="conv_in_wgrad")
    dh2, g_norm = mm_nt_rms_bwd(dproj3, w["w_in"], s["h"], w["norm"], dh, name="conv_in_bwd")
    return dh2, dict(norm=g_norm[0], w_in=g_w_in, ck=dck[0, :3], w_out=g_w_out)


def ffn_layer_fwd(h, w):
    xn, up2 = rms_mm_nn(h, w["norm"], w["w_up"], parts=2, tn=1408, name="ffn_up_proj")
    h2, act = ffn_act_down_fwd(up2, w["cw2"], w["w_down"], h, name="ffn_act_down_fwd")
    return h2, dict(h=h, xn=xn, up2=up2, act=act)


def ffn_layer_bwd(dh, w, s):
    dh3 = dh[None]
    g_w_down = mm_tn(s["act"], dh3, tk=1408, name="ffn_down_wgrad")
    dup2, dcw = ffn_act_bwd(dh, w["w_down"], s["up2"], w["cw2"], name="ffn_act_bwd")
    g_w_up = mm_tn(s["xn"], dup2, tn=1408, name="ffn_up_wgrad")
    dh2, g_norm = mm_nt_rms_bwd(dup2, w["w_up"], s["h"], w["norm"], dh, name="ffn_up_bwd")
    g_cw = jnp.concatenate([dcw[0, :3], dcw[1, :3]], axis=1)
    return dh2, dict(norm=g_norm[0], w_up=g_w_up, cw=g_cw, w_down=g_w_down)


def forward_backward(x, target, wa, wc, wf, *, late_weights=None, late_chunks=None):
    cst = _constants()
    h = x
    saved = []
    layer = 0
    while layer == 0 or layer < len(wf):
        i = layer // 2
        if layer % 2 == 0:
            h, sm, built = attn_layer_fwd(h, wa[i], cst, gather=late_weights if late_weights and layer == 0 else None)
            if built is not None:
                wa, wc, wf = built
        else:
            h, sm = conv_layer_fwd(h, wc[i])
        h, sf = ffn_layer_fwd(h, wf[layer])
        saved.append((sm, sf))
        layer += 1
    depth = len(wf)
    loss_blk, dh = loss_head(h, target, name="loss_head")
    ga, gc, gf = [None] * len(wa), [None] * len(wc), [None] * depth
    landed = None
    for layer in reversed(range(depth)):
        i = layer // 2
        sm, sf = saved[layer]
        dh, gf[layer] = ffn_layer_bwd(dh, wf[layer], sf)
        if layer % 2 == 0:
            chunks = None
            if late_chunks and layer == 0:
                chunks = lambda g_w_out: late_chunks([dict(w_out=g_w_out)] + ga[1:], gc, gf)
            dh, ga[i], got = attn_layer_bwd(dh, wa[i], sm, cst, scatter=chunks)
            landed = got if got is not None else landed
        else:
            dh, gc[i] = conv_layer_bwd(dh, wc[i], sm)
    return loss_blk, dh, ga, gc, gf, landed


def _part_rows(shape, width, row_mult):
    n = 1
    for s in shape:
        n *= s
    rows = -(-n // width)
    return -(-rows // row_mult) * row_mult


def _pack_rows(arrs, width, row_mult, dtype, total_rows=None, lead=0):
    parts = []
    used = 0
    for a in arrs:
        outer = a.shape[:lead]
        rows = _part_rows(a.shape[lead:], width, row_mult)
        flat = a.astype(dtype).reshape(outer + (-1,))
        flat = jnp.pad(flat, ((0, 0),) * lead + ((0, rows * width - flat.shape[-1]),))
        parts.append(flat.reshape(outer + (rows, width)))
        used += rows
    if total_rows is not None and total_rows > used:
        parts.append(jnp.zeros(parts[0].shape[:lead] + (total_rows - used, width), dtype))
    return jnp.concatenate(parts, axis=lead)


def _unpack_rows(packed, shapes, width, row_mult, lead=0):
    outer = packed.shape[:lead]
    out = []
    off = 0
    for shape in shapes:
        rows = _part_rows(shape, width, row_mult)
        n = 1
        for s in shape:
            n *= s
        flat = lax.slice_in_dim(packed, off, off + rows, axis=lead).reshape(outer + (-1,))
        out.append(lax.slice_in_dim(flat, 0, n, axis=lead).reshape(outer + tuple(shape)))
        off += rows
    return out


BIG_NAMES = ("attn_w_in", "attn_w_out", "conv_w_in", "conv_w_out", "ffn_w_up", "ffn_w_down")
BIG_AXIS = {"attn_w_in": 2, "attn_w_out": 1, "conv_w_in": 2, "conv_w_out": 1, "ffn_w_up": 2, "ffn_w_down": 1}
BIG_WIDTH = 1024
BIG_ROW_MULT = 16
BIG_TILE = 512
SMALL_TILE = 128
SMALL_SHARDED = ("conv_norm", "conv_kernel", "ffn_conv")
SMALL_AXIS = {"conv_norm": 1, "conv_kernel": 2, "ffn_conv": 2}
SMALL_REPLICATED = ("attn_norm", "attn_f_bias", "fox_q_gain", "fox_k_gain", "sb_q_gain", "sb_k_gain", "ffn_norm")
WEIGHT_ORDER = ("attn_norm", "attn_w_in", "attn_f_bias", "fox_q_gain", "fox_k_gain", "sb_q_gain", "sb_k_gain",
                "attn_w_out", "conv_norm", "conv_w_in", "conv_kernel", "conv_w_out", "ffn_norm", "ffn_w_up",
                "ffn_conv", "ffn_w_down")


def _big_total_rows(shapes):
    used = sum(_part_rows(s, BIG_WIDTH, BIG_ROW_MULT) for s in shapes)
    tile = BIG_TILE if used >= 8 * BIG_TILE else SMALL_TILE
    return -(-used // tile) * tile


def _place():
    x, y, c = lax.axis_index("x"), lax.axis_index("y"), lax.axis_index("c")
    other_chips = [(1 - x, y), (x, 1 - y), (1 - x, 1 - y)]
    return x, y, c, other_chips


_ANY = pl.BlockSpec(memory_space=pl.ANY)


def _chip_sems():
    return [pltpu.SemaphoreType.DMA((3,)), pltpu.SemaphoreType.DMA((3,)), pltpu.SemaphoreType.DMA]


def _chip_gather(src_ref, dst_ref, send_sems, recv_sems, local_sem):
    x, y, c, chips = _place()
    k = 2 * x + y

    def copy(j, slot):
        px, py = chips[j]
        return pltpu.make_async_remote_copy(src_ref=src_ref, dst_ref=dst_ref.at[slot], send_sem=send_sems.at[j],
                                            recv_sem=recv_sems.at[j], device_id=(px, py, c), device_id_type=MESH)

    def local():
        return pltpu.make_async_copy(src_ref, dst_ref.at[k], local_sem)

    def start():
        local().start()
        for j in range(3):
            copy(j, k).start()

    def finish():
        for j, (px, py) in enumerate(chips):
            copy(j, 2 * px + py).wait_recv()
        for j in range(3):
            copy(j, k).wait_send()
        local().wait()

    return start, finish


def _chip_scatter(g_ref, o_ref, send_sems, recv_sems, local_sem):
    x, y, c, chips = _place()
    k = 2 * x + y

    def copy(j, src_slot, dst_slot):
        px, py = chips[j]
        return pltpu.make_async_remote_copy(src_ref=g_ref.at[src_slot], dst_ref=o_ref.at[dst_slot],
                                            send_sem=send_sems.at[j], recv_sem=recv_sems.at[j],
                                            device_id=(px, py, c), device_id_type=MESH)

    def local():
        return pltpu.make_async_copy(g_ref.at[k], o_ref.at[k], local_sem)

    def start():
        local().start()
        for j, (px, py) in enumerate(chips):
            copy(j, 2 * px + py, k).start()

    def finish():
        for j, (px, py) in enumerate(chips):
            copy(j, k, 2 * px + py).wait_recv()
        for j, (px, py) in enumerate(chips):
            copy(j, 2 * px + py, k).wait_send()
        local().wait()

    return start, finish


def gather_chips(arrs, *, name):
    n = len(arrs)

    def body(*refs):
        hooks = [_chip_gather(refs[m], refs[n + m], *refs[2 * n + 3 * m:2 * n + 3 * m + 3]) for m in range(n)]
        for start, _ in hooks:
            start()
        for _, finish in hooks:
            finish()

    return pl.pallas_call(
        body, in_specs=[_ANY] * n, out_specs=[_ANY] * n,
        out_shape=[jax.ShapeDtypeStruct((N_CHIPS,) + a.shape, a.dtype) for a in arrs],
        scratch_shapes=_chip_sems() * n, name=name)(*arrs)


def scatter_chips(chunks, *, name):
    def body(g_ref, o_ref, send_sems, recv_sems, local_sem):
        start, finish = _chip_scatter(g_ref, o_ref, send_sems, recv_sems, local_sem)
        start()
        finish()

    return pl.pallas_call(
        body, in_specs=[_ANY], out_specs=_ANY, out_shape=jax.ShapeDtypeStruct(chunks.shape, chunks.dtype),
        scratch_shapes=_chip_sems(), name=name)(chunks)


def swap_cores(arrs, *, name):
    n = len(arrs)

    def body(*refs):
        x, y, c, _ = _place()
        copies = [pltpu.make_async_remote_copy(src_ref=refs[m], dst_ref=refs[n + m], send_sem=refs[2 * n + 2 * m],
                                               recv_sem=refs[2 * n + 2 * m + 1], device_id=(x, y, 1 - c),
                                               device_id_type=MESH) for m in range(n)]
        for cp in copies:
            cp.start()
        for cp in copies:
            cp.wait()

    return pl.pallas_call(
        body, in_specs=[_ANY] * n, out_specs=[_ANY] * n,
        out_shape=[jax.ShapeDtypeStruct(a.shape, a.dtype) for a in arrs],
        scratch_shapes=[pltpu.SemaphoreType.DMA, pltpu.SemaphoreType.DMA] * n, name=name)(*arrs)


def allreduce_small(p, *, name):
    r, w = p.shape

    def body(p_ref, o_ref, buf, send_sems, recv_sems):
        x, y, c, _ = _place()
        me = 4 * x + 2 * y + c
        buf[me] = p_ref[...]

        def peer_of(m):
            return (1 - x if m & 4 else x, 1 - y if m & 2 else y, 1 - c if m & 1 else c)

        def copy(m, slot):
            return pltpu.make_async_remote_copy(src_ref=p_ref, dst_ref=buf.at[slot], send_sem=send_sems.at[m - 1],
                                                recv_sem=recv_sems.at[m - 1], device_id=peer_of(m),
                                                device_id_type=MESH)

        sends = [copy(m, me) for m in range(1, 8)]
        for cp in sends:
            cp.start()
        for m in range(1, 8):
            px, py, pc = peer_of(m)
            copy(m, 4 * px + 2 * py + pc).wait_recv()
        for cp in sends:
            cp.wait_send()
        acc = buf[0]
        for d in range(1, 8):
            acc = acc + buf[d]
        o_ref[...] = acc

    vm = pl.BlockSpec(memory_space=pltpu.VMEM)
    return pl.pallas_call(
        body, in_specs=[vm], out_specs=vm, out_shape=jax.ShapeDtypeStruct((r, w), F32),
        scratch_shapes=[pltpu.VMEM((8, r, w), F32), pltpu.SemaphoreType.DMA((7,)), pltpu.SemaphoreType.DMA((7,))],
        name=name)(p)


def sum_chips(rv, *, name):
    _, r, w = rv.shape
    tile = BIG_TILE if r % BIG_TILE == 0 else SMALL_TILE

    def body(a_ref, b_ref, c_ref, d_ref, o_ref):
        o_ref[...] = ((a_ref[0].astype(F32) + b_ref[0].astype(F32)) + c_ref[0].astype(F32)) + d_ref[0].astype(F32)

    spec = lambda kk: pl.BlockSpec((1, tile, w), lambda i: (kk, i, 0))
    return pl.pallas_call(
        body, grid=(r // tile,), in_specs=[spec(0), spec(1), spec(2), spec(3)],
        out_specs=pl.BlockSpec((tile, w), lambda i: (i, 0)), out_shape=jax.ShapeDtypeStruct((r, w), F32),
        compiler_params=_params(), name=name)(rv, rv, rv, rv)


def add_pair(a, b, *, name):
    r, w = a.shape
    tile = BIG_TILE if r % BIG_TILE == 0 else SMALL_TILE

    def body(a_ref, b_ref, o_ref):
        o_ref[...] = a_ref[...] + b_ref[...]

    spec = pl.BlockSpec((tile, w), lambda i: (i, 0))
    return pl.pallas_call(body, grid=(r // tile,), in_specs=[spec, spec], out_specs=spec,
                          out_shape=jax.ShapeDtypeStruct((r, w), F32), compiler_params=_params(), name=name)(a, b)


def adamw(w, g, m, v, *, tm, name):
    r, c = w.shape
    assert r % tm == 0

    def body(w_ref, g_ref, m_ref, v_ref, d_ref, nm_ref, nv_ref):
        g_ = g_ref[...]
        m_ = ADAM_B1 * m_ref[...] + (1.0 - ADAM_B1) * g_
        v_ = ADAM_B2 * v_ref[...] + (1.0 - ADAM_B2) * (g_ * g_)
        m_hat = m_ / (1.0 - ADAM_B1 ** ADAM_STEP)
        v_hat = v_ / (1.0 - ADAM_B2 ** ADAM_STEP)
        d_ref[...] = -ADAM_LR * (m_hat / (jnp.sqrt(v_hat) + ADAM_EPS) + ADAM_WD * w_ref[...])
        nm_ref[...] = m_
        nv_ref[...] = v_

    spec = pl.BlockSpec((tm, c), lambda i: (i, 0))
    return pl.pallas_call(body, grid=(r // tm,), in_specs=[spec] * 4, out_specs=[spec] * 3,
                          out_shape=[jax.ShapeDtypeStruct((r, c), F32)] * 3, compiler_params=_params(), name=name)(w, g, m, v)


def kernel(x, attn_norm, attn_w_in, attn_f_bias, fox_q_gain, fox_k_gain, sb_q_gain, sb_k_gain, attn_w_out, conv_norm, conv_w_in, conv_kernel, conv_w_out, ffn_norm, ffn_w_up, ffn_conv, ffn_w_down, loss_target, m_attn_norm, m_attn_w_in, m_attn_f_bias, m_fox_q_gain, m_fox_k_gain, m_sb_q_gain, m_sb_k_gain, m_attn_w_out, m_conv_norm, m_conv_w_in, m_conv_kernel, m_conv_w_out, m_ffn_norm, m_ffn_w_up, m_ffn_conv, m_ffn_w_down, v_attn_norm, v_attn_w_in, v_attn_f_bias, v_fox_q_gain, v_fox_k_gain, v_sb_q_gain, v_sb_k_gain, v_attn_w_out, v_conv_norm, v_conv_w_in, v_conv_kernel, v_conv_w_out, v_ffn_norm, v_ffn_w_up, v_ffn_conv, v_ffn_w_down):
    a = dict(locals())
    chip = 2 * lax.axis_index("x") + lax.axis_index("y")
    n_attn, n_conv, depth = attn_norm.shape[0], conv_norm.shape[0], ffn_norm.shape[0]

    units = [(name, l) for name in BIG_NAMES for l in range(a[name].shape[0])]
    early = [("attn_w_in", 0)]
    late = [u for u in units if u not in early]
    late_b = [("attn_w_out", 0), ("conv_w_in", n_conv - 1), ("conv_w_out", n_conv - 1), ("ffn_w_up", depth - 1),
              ("ffn_w_down", depth - 1)]
    late_a = [u for u in late if u not in late_b]
    late_sb = [("ffn_w_up", 0), ("ffn_w_down", 0), ("ffn_w_up", 1), ("ffn_w_down", 1)]
    late_sa = [u for u in late if u not in late_sb]

    def unit_shape(u):
        return a[u[0]].shape[1:]

    def pack_units(us, get, lead=0):
        return _pack_rows([get(u) for u in us], BIG_WIDTH, BIG_ROW_MULT, BF16, _big_total_rows([unit_shape(u) for u in us]),
                          lead=lead)

    def unpack_units(packed, us, lead=0):
        return dict(zip(us, _unpack_rows(packed, [unit_shape(u) for u in us], BIG_WIDTH, BIG_ROW_MULT, lead=lead)))

    def full_units(gathered, us):
        full_size = {}
        for u, g4 in unpack_units(gathered, us, lead=1).items():
            _, rows, cols = g4.shape
            if BIG_AXIS[u[0]] - 1 == 0:
                full_size[u] = g4.reshape(N_CHIPS * rows, cols)
            else:
                full_size[u] = g4.transpose(1, 0, 2).reshape(rows, N_CHIPS * cols)
        return full_size

    def shard(u):
        return a[u[0]][u[1]]

    small_shapes = [a[n].shape for n in SMALL_SHARDED]
    packed_s = _pack_rows([a[n] for n in SMALL_SHARDED], 128, 8, F32)
    gath_e, gath_s = gather_chips([pack_units(early, shard), packed_s], name="gather_weights")
    full_e = full_units(gath_e, early)
    full = {}
    per_chip = [_unpack_rows(gath_s[kk], small_shapes, 128, 8) for kk in range(N_CHIPS)]
    for n, name in enumerate(SMALL_SHARDED):
        full[name] = jnp.concatenate([per_chip[kk][n] for kk in range(N_CHIPS)], axis=SMALL_AXIS[name])

    def attn_weights(i, fu):
        return dict(
            norm=attn_norm[i][None],
            w_in=jnp.pad(fu[("attn_w_in", i)], ((0, 0), (0, ATTN_IN_PAD - ATTN_IN))),
            fbias=jnp.pad(attn_f_bias[i], (0, 128 - H_FOX))[None],
            gq=jnp.concatenate([jnp.tile(fox_q_gain[i], H_FOX), jnp.tile(sb_q_gain[i], H_SB)])[None],
            gk=jnp.concatenate([jnp.tile(fox_k_gain[i], H_FOX), jnp.tile(sb_k_gain[i], H_SB)])[None],
            w_out=fu.get(("attn_w_out", i)))

    def build_weights(gathered):
        fu = {**full_e, **full_units(gathered[0], late_a), **full_units(gathered[1], late_b)}
        wa = [attn_weights(i, fu) for i in range(n_attn)]
        wc = [dict(norm=full["conv_norm"][i][None], w_in=fu[("conv_w_in", i)], ck=full["conv_kernel"][i][None],
                   w_out=fu[("conv_w_out", i)]) for i in range(n_conv)]
        wf = []
        for l in range(depth):
            cw = full["ffn_conv"][l]
            wf.append(dict(norm=ffn_norm[l][None], w_up=fu[("ffn_w_up", l)], cw2=jnp.stack([cw[:, :D_FF], cw[:, D_FF:]]),
                           w_down=fu[("ffn_w_down", l)]))
        return wa, wc, wf

    def chunks_of_unit(u, ga, gc, gf):
        name, l = u
        g = {"attn_w_in": lambda: ga[l]["w_in"], "attn_w_out": lambda: ga[l]["w_out"],
             "conv_w_in": lambda: gc[l]["w_in"], "conv_w_out": lambda: gc[l]["w_out"],
             "ffn_w_up": lambda: gf[l]["w_up"], "ffn_w_down": lambda: gf[l]["w_down"]}[name]()
        rows, cols = unit_shape(u)
        if BIG_AXIS[name] - 1 == 0:
            return g.reshape(N_CHIPS, rows, cols)
        return g.reshape(rows, N_CHIPS, cols).transpose(1, 0, 2)

    def chunks_of(us, ga, gc, gf):
        return pack_units(us, lambda u: chunks_of_unit(u, ga, gc, gf), lead=1)

    loss_blk, grad_x, ga, gc, gf, landed_late = forward_backward(
        x[0], loss_target[0], [attn_weights(0, full_e)], [], [],
        late_weights=(pack_units(late_a, shard), pack_units(late_b, shard), build_weights),
        late_chunks=lambda ga, gc, gf: (chunks_of(late_sa, ga, gc, gf), chunks_of(late_sb, ga, gc, gf)))

    landed = [scatter_chips(chunks_of(early, ga, gc, gf), name="scatter_grads"), landed_late[0], landed_late[1]]
    mine = [sum_chips(buf, name="sum_chips") for buf in landed]
    theirs = swap_cores(mine, name="swap_cores")
    g_units = {}
    for us, m, th in zip((early, late_sa, late_sb), mine, theirs):
        g_units.update(unpack_units(add_pair(m, th, name="add_cores"), us))
    grads = {name: jnp.stack([g_units[(name, l)] for l in range(a[name].shape[0])]) for name in BIG_NAMES}

    small_full = [
        loss_blk,
        jnp.stack([g["norm"] for g in ga]), jnp.stack([g["f_bias"] for g in ga]),
        jnp.stack([g["fox_q"] for g in ga]), jnp.stack([g["fox_k"] for g in ga]),
        jnp.stack([g["sb_q"] for g in ga]), jnp.stack([g["sb_k"] for g in ga]),
        jnp.stack([g["norm"] for g in gf]),
        jnp.stack([g["norm"] for g in gc]), jnp.stack([g["ck"] for g in gc]), jnp.stack([g["cw"] for g in gf]),
    ]
    summed = allreduce_small(_pack_rows(small_full, 128, 8, F32), name="allreduce_small")
    parts = _unpack_rows(summed, [p.shape for p in small_full], 128, 8)
    loss = parts[0][0, 0]
    for name, g in zip(SMALL_REPLICATED, parts[1:8]):
        grads[name] = g
    for name, g in zip(SMALL_SHARDED, parts[8:]):
        width = a[name].shape[SMALL_AXIS[name]]
        grads[name] = lax.dynamic_slice_in_dim(g, chip * width, width, axis=SMALL_AXIS[name])

    delta, new_m, new_v = {}, {}, {}
    for name in BIG_NAMES:
        shape = a[name].shape
        flat = lambda arr: arr.reshape(-1, shape[-1])
        d_, m_, v_ = adamw(flat(a[name]), flat(grads[name]), flat(a["m_" + name]), flat(a["v_" + name]), tm=256,
                           name="adamw")
        delta[name], new_m[name], new_v[name] = d_.reshape(shape), m_.reshape(shape), v_.reshape(shape)
    small_names = SMALL_REPLICATED + SMALL_SHARDED
    small_shapes_local = [a[n].shape for n in small_names]
    pack = lambda prefix, src: _pack_rows([src[prefix + n] for n in small_names], 128, 8, F32)
    packed = adamw(pack("", a), pack("", grads), pack("m_", a), pack("v_", a), tm=8, name="adamw_small")
    for store, buf in zip((delta, new_m, new_v), packed):
        for name, arr in zip(small_names, _unpack_rows(buf, small_shapes_local, 128, 8)):
            store[name] = arr

    return (loss, grad_x[None], *[grads[n] for n in WEIGHT_ORDER], *[delta[n] for n in WEIGHT_ORDER],
            *[new_m[n] for n in WEIGHT_ORDER], *[new_v[n] for n in WEIGHT_ORDER])
```
